```python
import jax, jax.numpy as jnp
from jax import lax
import numpy as np

D_MODEL = 1024
BATCH = 8
SEQ = 8192
DEPTH = 1

CHUNK = 64
Q_BLOCK = 128
D_FF = 2816
HG_HEADS = 8
HG_HEAD_K = 128
HG_HEAD_V = 128
HG_WIDTH = HG_HEADS * HG_HEAD_K
HG_VWIDTH = HG_HEADS * HG_HEAD_V
MLA_HEADS = 8
MLA_NOPE = 128
MLA_ROPE = 64
MLA_V = 128
MLA_QK = MLA_NOPE + MLA_ROPE
Q_LORA = 384
KV_LORA = 256
ROPE_THETA = 10000.0
EPS = 1e-6
IN_SPLITS = (HG_WIDTH, HG_WIDTH, HG_VWIDTH, HG_VWIDTH, Q_LORA, KV_LORA, MLA_ROPE)
IN_COLS = 2 * HG_WIDTH + 2 * HG_VWIDTH + Q_LORA + KV_LORA + MLA_ROPE

kernel_name = 'hybrid_hgrn2_mla_macaron'


def _rms_norm(x, gain):
    xf = x.astype(jnp.float32)
    y = xf * lax.rsqrt(jnp.mean(xf * xf, axis=-1, keepdims=True) + EPS)
    return (y * gain.astype(jnp.float32)).astype(x.dtype)


def _swiglu(x, w_in, w_out):
    gate, up = jnp.split(x @ w_in, 2, axis=-1)
    return (jax.nn.silu(gate) * up) @ w_out


def _rotate(x, cos, sin):
    half = x.shape[-1] // 2
    x1, x2 = x[..., :half], x[..., half:]
    return jnp.concatenate([x1 * cos - x2 * sin, x2 * cos + x1 * sin], axis=-1)


def _to_chunks(t):
    b, s, h, d = t.shape
    return t.reshape(b, s // CHUNK, CHUNK, h, d).transpose(1, 0, 3, 2, 4)


def _from_chunks(t):
    n, b, h, c, d = t.shape
    return t.transpose(1, 0, 3, 2, 4).reshape(b, n * c, h, d)


def _hgrn2_chunk_step(state, inputs):
    q, k, v, log_f = inputs
    cum = jnp.cumsum(log_f, axis=2)
    o_inter = jnp.einsum('bhtk,bhkv->bhtv', q * jnp.exp(cum), state)
    causal = jnp.tril(jnp.ones((CHUNK, CHUNK), dtype=bool))[:, :, None]
    rel = cum[:, :, :, None, :] - cum[:, :, None, :, :]
    decay = jnp.exp(jnp.where(causal, rel, -jnp.inf))
    scores = jnp.einsum('bhtk,bhtsk,bhsk->bhts', q, decay, k)
    o_intra = jnp.einsum('bhts,bhsv->bhtv', scores, v)
    last = cum[:, :, -1, :]
    new_state = jnp.exp(last)[..., None] * state + jnp.einsum(
        'bhsk,bhsv->bhkv', k * jnp.exp(last[:, :, None, :] - cum), v)
    return new_state, o_intra + o_inter


def _hgrn2(q_raw, f_raw, i_raw, g_raw, lower_bound, out_gain):
    b, s, _ = q_raw.shape
    f32 = jnp.float32
    q = jax.nn.silu(q_raw.astype(f32)).reshape(b, s, HG_HEADS, HG_HEAD_K)
    z = f_raw.astype(f32).reshape(b, s, HG_HEADS, HG_HEAD_K)
    lb = lower_bound.astype(f32).reshape(HG_HEADS, HG_HEAD_K)
    log_f = jnp.logaddexp(jnp.log(lb), jnp.log1p(-lb) + jax.nn.log_sigmoid(z))
    k = -jnp.expm1(log_f)
    v = i_raw.astype(f32).reshape(b, s, HG_HEADS, HG_HEAD_V)
    state0 = jnp.zeros((b, HG_HEADS, HG_HEAD_K, HG_HEAD_V), f32)
    _, o = lax.scan(_hgrn2_chunk_step, state0,
                    (_to_chunks(q), _to_chunks(k), _to_chunks(v), _to_chunks(log_f)))
    o = _rms_norm(_from_chunks(o), out_gain)
    o = o * jax.nn.silu(g_raw.astype(f32)).reshape(b, s, HG_HEADS, HG_HEAD_V)
    return o.reshape(b, s, HG_VWIDTH).astype(q_raw.dtype)


def _mla(c_q, c_kv, k_pe, positions, q_lora_gain, w_q_up, kv_lora_gain, w_kv_up,
         q_head_gain, k_head_gain):
    b, s, _ = c_q.shape
    q = (_rms_norm(c_q, q_lora_gain) @ w_q_up).reshape(b, s, MLA_HEADS, MLA_QK)
    kv = (_rms_norm(c_kv, kv_lora_gain) @ w_kv_up).reshape(b, s, MLA_HEADS, MLA_NOPE + MLA_V)
    k_nope, v = kv[..., :MLA_NOPE], kv[..., MLA_NOPE:]
    k = jnp.concatenate(
        [k_nope, jnp.broadcast_to(k_pe[:, :, None, :], (b, s, MLA_HEADS, MLA_ROPE))], axis=-1)
    q = _rms_norm(q, q_head_gain)
    k = _rms_norm(k, k_head_gain)
    inv_freq = ROPE_THETA ** (-jnp.arange(0, MLA_ROPE, 2, dtype=jnp.float32) / MLA_ROPE)
    ang = positions.astype(jnp.float32)[:, :, None, None] * inv_freq
    cos = jnp.cos(ang).astype(q.dtype)
    sin = jnp.sin(ang).astype(q.dtype)
    q = jnp.concatenate([q[..., :MLA_NOPE], _rotate(q[..., MLA_NOPE:], cos, sin)], axis=-1)
    k = jnp.concatenate([k[..., :MLA_NOPE], _rotate(k[..., MLA_NOPE:], cos, sin)], axis=-1)
    n_blocks = s // Q_BLOCK
    q_blocks = q.reshape(b, n_blocks, Q_BLOCK, MLA_HEADS, MLA_QK).transpose(1, 0, 2, 3, 4)
    key_chunk = jnp.arange(s) // CHUNK
    scale = MLA_QK ** -0.5

    def attend(args):
        q_blk, blk = args
        q_chunk = (blk * Q_BLOCK + jnp.arange(Q_BLOCK)) // CHUNK
        allowed = key_chunk[None, :] <= q_chunk[:, None]
        logits = jnp.einsum('bqhd,bkhd->bhqk', q_blk, k).astype(jnp.float32) * scale
        logits = jnp.where(allowed, logits, -jnp.inf)
        probs = jax.nn.softmax(logits, axis=-1).astype(v.dtype)
        return jnp.einsum('bhqk,bkhd->bqhd', probs, v)

    o = lax.map(attend, (q_blocks, jnp.arange(n_blocks)))
    return o.transpose(1, 0, 2, 3, 4).reshape(b, s, MLA_HEADS * MLA_V)


def _fwd_setup_inputs(seed: int = 0) -> dict:
    key = jax.random.key(seed)
    ks = jax.random.split(key, 24)
    f32 = jnp.float32
    L = DEPTH

    def w(k, shape, fan_in):
        return jax.random.normal(k, shape, f32) * (fan_in ** -0.5)

    def gain(k, shape):
        return 1.0 + 0.05 * jax.random.normal(k, shape, f32)

    x = jax.random.normal(ks[0], (BATCH, SEQ, D_MODEL), f32)
    offsets = jax.random.randint(ks[1], (BATCH, 1), 0, 64, dtype=jnp.int32) * CHUNK
    positions = (offsets + jnp.arange(SEQ, dtype=jnp.int32)[None, :]).astype(jnp.int32)
    return {
        'x': x,
        'positions': positions,
        'ffn1_norm': gain(ks[2], (L, D_MODEL)),
        'ffn1_w_in': w(ks[3], (L, D_MODEL, 2 * D_FF), D_MODEL),
        'ffn1_w_out': w(ks[4], (L, D_FF, D_MODEL), D_FF),
        'mix_norm': gain(ks[5], (L, D_MODEL)),
        'w_in': w(ks[6], (L, D_MODEL, IN_COLS), D_MODEL),
        'hg_lb_table': 0.5 * jax.random.normal(ks[7], (L + 1, HG_WIDTH), f32),
        'hg_out_norm': gain(ks[8], (L, HG_HEAD_V)),
        'w_hg_branch': w(ks[9], (L, HG_VWIDTH, D_MODEL), HG_VWIDTH),
        'mla_q_lora_norm': gain(ks[10], (L, Q_LORA)),
        'w_q_up': w(ks[11], (L, Q_LORA, MLA_HEADS * MLA_QK), Q_LORA),
        'mla_kv_lora_norm': gain(ks[12], (L, KV_LORA)),
        'w_kv_up': w(ks[13], (L, KV_LORA, MLA_HEADS * (MLA_NOPE + MLA_V)), KV_LORA),
        'q_head_norm': gain(ks[14], (L, MLA_QK)),
        'k_head_norm': gain(ks[15], (L, MLA_QK)),
        'w_mla_branch': w(ks[16], (L, MLA_HEADS * MLA_V, D_MODEL), MLA_HEADS * MLA_V),
        'w_merge': w(ks[17], (L, D_MODEL, 2 * D_MODEL), D_MODEL),
        'b_merge': 0.02 * jax.random.normal(ks[18], (L, 2 * D_MODEL), f32),
        'w_out': w(ks[19], (L, D_MODEL, D_MODEL), D_MODEL),
        'ffn2_norm': gain(ks[20], (L, D_MODEL)),
        'ffn2_w_in': w(ks[21], (L, D_MODEL, 2 * D_FF), D_MODEL),
        'ffn2_w_out': w(ks[22], (L, D_FF, D_MODEL), D_FF),
        'final_norm': gain(ks[23], (L, D_MODEL)),
    }


def _fwd_reference(x, positions, ffn1_norm, ffn1_w_in, ffn1_w_out, mix_norm, w_in, hg_lb_table,
              hg_out_norm, w_hg_branch, mla_q_lora_norm, w_q_up, mla_kv_lora_norm, w_kv_up,
              q_head_norm, k_head_norm, w_mla_branch, w_merge, b_merge, w_out,
              ffn2_norm, ffn2_w_in, ffn2_w_out, final_norm):
    lower_bounds = jnp.cumsum(jax.nn.softmax(hg_lb_table.astype(jnp.float32), axis=0), axis=0)
    split_at = np.cumsum(IN_SPLITS)[:-1].tolist()
    h = x
    for l in range(DEPTH):
        h = h + 0.5 * _swiglu(_rms_norm(h, ffn1_norm[l]), ffn1_w_in[l], ffn1_w_out[l])
        u = _rms_norm(h, mix_norm[l])
        hg_q, hg_f, hg_i, hg_g, c_q, c_kv, k_pe = jnp.split(u @ w_in[l], split_at, axis=-1)
        y_hg = _hgrn2(hg_q, hg_f, hg_i, hg_g, lower_bounds[l], hg_out_norm[l]) @ w_hg_branch[l]
        y_mla = _mla(c_q, c_kv, k_pe, positions, mla_q_lora_norm[l], w_q_up[l],
                     mla_kv_lora_norm[l], w_kv_up[l], q_head_norm[l], k_head_norm[l]) @ w_mla_branch[l]
        g_hg, g_mla = jnp.split(jax.nn.sigmoid(u @ w_merge[l] + b_merge[l]), 2, axis=-1)
        h = h + (g_hg * y_hg + g_mla * y_mla) @ w_out[l]
        h = h + 0.5 * _swiglu(_rms_norm(h, ffn2_norm[l]), ffn2_w_in[l], ffn2_w_out[l])
        h = _rms_norm(h, final_norm[l])
    return h


import jax as _jax
import jax.numpy as _jnp

TWIN_FORMAT = 'train_step'
FWD_PARAMS = ['x', 'positions', 'ffn1_norm', 'ffn1_w_in', 'ffn1_w_out', 'mix_norm', 'w_in', 'hg_lb_table', 'hg_out_norm', 'w_hg_branch', 'mla_q_lora_norm', 'w_q_up', 'mla_kv_lora_norm', 'w_kv_up', 'q_head_norm', 'k_head_norm', 'w_mla_branch', 'w_merge', 'b_merge', 'w_out', 'ffn2_norm', 'ffn2_w_in', 'ffn2_w_out', 'final_norm']
TWIN_WEIGHTS = ['ffn1_norm', 'ffn1_w_in', 'ffn1_w_out', 'mix_norm', 'w_in', 'hg_lb_table', 'hg_out_norm', 'w_hg_branch', 'mla_q_lora_norm', 'w_q_up', 'mla_kv_lora_norm', 'w_kv_up', 'q_head_norm', 'k_head_norm', 'w_mla_branch', 'w_merge', 'b_merge', 'w_out', 'ffn2_norm', 'ffn2_w_in', 'ffn2_w_out', 'final_norm']
TWIN_DIFF_INPUT = 'x'
TWIN_INPUTS = ['x', 'positions', 'ffn1_norm', 'ffn1_w_in', 'ffn1_w_out', 'mix_norm', 'w_in', 'hg_lb_table', 'hg_out_norm', 'w_hg_branch', 'mla_q_lora_norm', 'w_q_up', 'mla_kv_lora_norm', 'w_kv_up', 'q_head_norm', 'k_head_norm', 'w_mla_branch', 'w_merge', 'b_merge', 'w_out', 'ffn2_norm', 'ffn2_w_in', 'ffn2_w_out', 'final_norm', 'loss_target', 'm_ffn1_norm', 'm_ffn1_w_in', 'm_ffn1_w_out', 'm_mix_norm', 'm_w_in', 'm_hg_lb_table', 'm_hg_out_norm', 'm_w_hg_branch', 'm_mla_q_lora_norm', 'm_w_q_up', 'm_mla_kv_lora_norm', 'm_w_kv_up', 'm_q_head_norm', 'm_k_head_norm', 'm_w_mla_branch', 'm_w_merge', 'm_b_merge', 'm_w_out', 'm_ffn2_norm', 'm_ffn2_w_in', 'm_ffn2_w_out', 'm_final_norm', 'v_ffn1_norm', 'v_ffn1_w_in', 'v_ffn1_w_out', 'v_mix_norm', 'v_w_in', 'v_hg_lb_table', 'v_hg_out_norm', 'v_w_hg_branch', 'v_mla_q_lora_norm', 'v_w_q_up', 'v_mla_kv_lora_norm', 'v_w_kv_up', 'v_q_head_norm', 'v_k_head_norm', 'v_w_mla_branch', 'v_w_merge', 'v_b_merge', 'v_w_out', 'v_ffn2_norm', 'v_ffn2_w_in', 'v_ffn2_w_out', 'v_final_norm']
TWIN_OUTPUTS = ['loss', 'grad_x', 'grad_ffn1_norm', 'grad_ffn1_w_in', 'grad_ffn1_w_out', 'grad_mix_norm', 'grad_w_in', 'grad_hg_lb_table', 'grad_hg_out_norm', 'grad_w_hg_branch', 'grad_mla_q_lora_norm', 'grad_w_q_up', 'grad_mla_kv_lora_norm', 'grad_w_kv_up', 'grad_q_head_norm', 'grad_k_head_norm', 'grad_w_mla_branch', 'grad_w_merge', 'grad_b_merge', 'grad_w_out', 'grad_ffn2_norm', 'grad_ffn2_w_in', 'grad_ffn2_w_out', 'grad_final_norm', 'delta_ffn1_norm', 'delta_ffn1_w_in', 'delta_ffn1_w_out', 'delta_mix_norm', 'delta_w_in', 'delta_hg_lb_table', 'delta_hg_out_norm', 'delta_w_hg_branch', 'delta_mla_q_lora_norm', 'delta_w_q_up', 'delta_mla_kv_lora_norm', 'delta_w_kv_up', 'delta_q_head_norm', 'delta_k_head_norm', 'delta_w_mla_branch', 'delta_w_merge', 'delta_b_merge', 'delta_w_out', 'delta_ffn2_norm', 'delta_ffn2_w_in', 'delta_ffn2_w_out', 'delta_final_norm', 'new_m_ffn1_norm', 'new_m_ffn1_w_in', 'new_m_ffn1_w_out', 'new_m_mix_norm', 'new_m_w_in', 'new_m_hg_lb_table', 'new_m_hg_out_norm', 'new_m_w_hg_branch', 'new_m_mla_q_lora_norm', 'new_m_w_q_up', 'new_m_mla_kv_lora_norm', 'new_m_w_kv_up', 'new_m_q_head_norm', 'new_m_k_head_norm', 'new_m_w_mla_branch', 'new_m_w_merge', 'new_m_b_merge', 'new_m_w_out', 'new_m_ffn2_norm', 'new_m_ffn2_w_in', 'new_m_ffn2_w_out', 'new_m_final_norm', 'new_v_ffn1_norm', 'new_v_ffn1_w_in', 'new_v_ffn1_w_out', 'new_v_mix_norm', 'new_v_w_in', 'new_v_hg_lb_table', 'new_v_hg_out_norm', 'new_v_w_hg_branch', 'new_v_mla_q_lora_norm', 'new_v_w_q_up', 'new_v_mla_kv_lora_norm', 'new_v_w_kv_up', 'new_v_q_head_norm', 'new_v_k_head_norm', 'new_v_w_mla_branch', 'new_v_w_merge', 'new_v_b_merge', 'new_v_w_out', 'new_v_ffn2_norm', 'new_v_ffn2_w_in', 'new_v_ffn2_w_out', 'new_v_final_norm']
TWIN_LEAF_KINDS = {'loss': 'loss', 'grad_x': 'grad_x', 'grad_ffn1_norm': 'grad_w', 'grad_ffn1_w_in': 'grad_w', 'grad_ffn1_w_out': 'grad_w', 'grad_mix_norm': 'grad_w', 'grad_w_in': 'grad_w', 'grad_hg_lb_table': 'grad_w', 'grad_hg_out_norm': 'grad_w', 'grad_w_hg_branch': 'grad_w', 'grad_mla_q_lora_norm': 'grad_w', 'grad_w_q_up': 'grad_w', 'grad_mla_kv_lora_norm': 'grad_w', 'grad_w_kv_up': 'grad_w', 'grad_q_head_norm': 'grad_w', 'grad_k_head_norm': 'grad_w', 'grad_w_mla_branch': 'grad_w', 'grad_w_merge': 'grad_w', 'grad_b_merge': 'grad_w', 'grad_w_out': 'grad_w', 'grad_ffn2_norm': 'grad_w', 'grad_ffn2_w_in': 'grad_w', 'grad_ffn2_w_out': 'grad_w', 'grad_final_norm': 'grad_w', 'delta_ffn1_norm': 'delta_w', 'delta_ffn1_w_in': 'delta_w', 'delta_ffn1_w_out': 'delta_w', 'delta_mix_norm': 'delta_w', 'delta_w_in': 'delta_w', 'delta_hg_lb_table': 'delta_w', 'delta_hg_out_norm': 'delta_w', 'delta_w_hg_branch': 'delta_w', 'delta_mla_q_lora_norm': 'delta_w', 'delta_w_q_up': 'delta_w', 'delta_mla_kv_lora_norm': 'delta_w', 'delta_w_kv_up': 'delta_w', 'delta_q_head_norm': 'delta_w', 'delta_k_head_norm': 'delta_w', 'delta_w_mla_branch': 'delta_w', 'delta_w_merge': 'delta_w', 'delta_b_merge': 'delta_w', 'delta_w_out': 'delta_w', 'delta_ffn2_norm': 'delta_w', 'delta_ffn2_w_in': 'delta_w', 'delta_ffn2_w_out': 'delta_w', 'delta_final_norm': 'delta_w', 'new_m_ffn1_norm': 'new_m', 'new_m_ffn1_w_in': 'new_m', 'new_m_ffn1_w_out': 'new_m', 'new_m_mix_norm': 'new_m', 'new_m_w_in': 'new_m', 'new_m_hg_lb_table': 'new_m', 'new_m_hg_out_norm': 'new_m', 'new_m_w_hg_branch': 'new_m', 'new_m_mla_q_lora_norm': 'new_m', 'new_m_w_q_up': 'new_m', 'new_m_mla_kv_lora_norm': 'new_m', 'new_m_w_kv_up': 'new_m', 'new_m_q_head_norm': 'new_m', 'new_m_k_head_norm': 'new_m', 'new_m_w_mla_branch': 'new_m', 'new_m_w_merge': 'new_m', 'new_m_b_merge': 'new_m', 'new_m_w_out': 'new_m', 'new_m_ffn2_norm': 'new_m', 'new_m_ffn2_w_in': 'new_m', 'new_m_ffn2_w_out': 'new_m', 'new_m_final_norm': 'new_m', 'new_v_ffn1_norm': 'new_v', 'new_v_ffn1_w_in': 'new_v', 'new_v_ffn1_w_out': 'new_v', 'new_v_mix_norm': 'new_v', 'new_v_w_in': 'new_v', 'new_v_hg_lb_table': 'new_v', 'new_v_hg_out_norm': 'new_v', 'new_v_w_hg_branch': 'new_v', 'new_v_mla_q_lora_norm': 'new_v', 'new_v_w_q_up': 'new_v', 'new_v_mla_kv_lora_norm': 'new_v', 'new_v_w_kv_up': 'new_v', 'new_v_q_head_norm': 'new_v', 'new_v_k_head_norm': 'new_v', 'new_v_w_mla_branch': 'new_v', 'new_v_w_merge': 'new_v', 'new_v_b_merge': 'new_v', 'new_v_w_out': 'new_v', 'new_v_ffn2_norm': 'new_v', 'new_v_ffn2_w_in': 'new_v', 'new_v_ffn2_w_out': 'new_v', 'new_v_final_norm': 'new_v'}


def _forward(args):
    return _fwd_reference(*[args[k] for k in FWD_PARAMS])


def _output_shape():
    out = _jax.eval_shape(lambda: _forward(_fwd_setup_inputs(0)))
    return out.shape, out.dtype

N_MICROBATCH = 1
ADAM_LR = 0.001
ADAM_B1 = 0.9
ADAM_B2 = 0.999
ADAM_EPS = 1e-08
ADAM_WD = 0.01
ADAM_STEP = 10
PER_EXAMPLE_BATCH_AXIS = {'x': 0, 'positions': 0, 'loss_target': 0}
SHARED_INPUTS = []
_WEIGHT_DTYPES = {'ffn1_norm': _jnp.float32, 'ffn1_w_in': _jnp.float32, 'ffn1_w_out': _jnp.float32, 'mix_norm': _jnp.float32, 'w_in': _jnp.float32, 'hg_lb_table': _jnp.float32, 'hg_out_norm': _jnp.float32, 'w_hg_branch': _jnp.float32, 'mla_q_lora_norm': _jnp.float32, 'w_q_up': _jnp.float32, 'mla_kv_lora_norm': _jnp.float32, 'w_kv_up': _jnp.float32, 'q_head_norm': _jnp.float32, 'k_head_norm': _jnp.float32, 'w_mla_branch': _jnp.float32, 'w_merge': _jnp.float32, 'b_merge': _jnp.float32, 'w_out': _jnp.float32, 'ffn2_norm': _jnp.float32, 'ffn2_w_in': _jnp.float32, 'ffn2_w_out': _jnp.float32, 'final_norm': _jnp.float32}
MOMENT_SCALE = {'ffn1_norm': 1.170331e-01, 'ffn1_w_in': 4.852386e-02, 'ffn1_w_out': 7.924507e-02, 'mix_norm': 1.235454e-01, 'w_in': 5.384164e-02, 'hg_lb_table': 6.785651e-03, 'hg_out_norm': 2.486564e-01, 'w_hg_branch': 7.848575e-02, 'mla_q_lora_norm': 3.159214e-02, 'w_q_up': 1.601653e-02, 'mla_kv_lora_norm': 5.714225e-02, 'w_kv_up': 1.865645e-02, 'q_head_norm': 4.653411e-02, 'k_head_norm': 4.695426e-02, 'w_mla_branch': 2.052524e-02, 'w_merge': 2.206453e-02, 'b_merge': 2.153616e-02, 'w_out': 7.969619e-02, 'ffn2_norm': 9.881469e-02, 'ffn2_w_in': 4.120699e-02, 'ffn2_w_out': 6.743422e-02, 'final_norm': 6.392565e+01}


def _to_microbatches(a, axis):
    t = _jnp.moveaxis(a, axis, 0)
    t = t.reshape((N_MICROBATCH, t.shape[0] // N_MICROBATCH) + t.shape[1:])
    return _jnp.moveaxis(t, 1, axis + 1)


def setup_inputs(seed: int = 0) -> dict:
    inp = _fwd_setup_inputs(seed)
    key = _jax.random.fold_in(_jax.random.key(seed), 7919)
    shape, _ = _output_shape()
    out = dict(inp)
    out["loss_target"] = _jax.random.normal(_jax.random.fold_in(key, 0), shape, _jnp.float32)
    for i, name in enumerate(TWIN_WEIGHTS):
        w = inp[name].astype(_jnp.float32)
        if MOMENT_SCALE is None:
            s = _jnp.sqrt(_jnp.mean(_jnp.square(w)) + 1e-30)
        else:
            s = MOMENT_SCALE[name]
        km, kv = _jax.random.split(_jax.random.fold_in(key, i + 1))
        out[name] = w
        out["m_" + name] = s * _jax.random.normal(km, w.shape, _jnp.float32)
        out["v_" + name] = (s * s) * _jax.random.uniform(kv, w.shape, _jnp.float32, 0.5, 1.5)
    if N_MICROBATCH > 1:
        for name, axis in PER_EXAMPLE_BATCH_AXIS.items():
            out[name] = _to_microbatches(out[name], axis)
    return {'x': out['x'], 'positions': out['positions'], 'ffn1_norm': out['ffn1_norm'], 'ffn1_w_in': out['ffn1_w_in'], 'ffn1_w_out': out['ffn1_w_out'], 'mix_norm': out['mix_norm'], 'w_in': out['w_in'], 'hg_lb_table': out['hg_lb_table'], 'hg_out_norm': out['hg_out_norm'], 'w_hg_branch': out['w_hg_branch'], 'mla_q_lora_norm': out['mla_q_lora_norm'], 'w_q_up': out['w_q_up'], 'mla_kv_lora_norm': out['mla_kv_lora_norm'], 'w_kv_up': out['w_kv_up'], 'q_head_norm': out['q_head_norm'], 'k_head_norm': out['k_head_norm'], 'w_mla_branch': out['w_mla_branch'], 'w_merge': out['w_merge'], 'b_merge': out['b_merge'], 'w_out': out['w_out'], 'ffn2_norm': out['ffn2_norm'], 'ffn2_w_in': out['ffn2_w_in'], 'ffn2_w_out': out['ffn2_w_out'], 'final_norm': out['final_norm'], 'loss_target': out['loss_target'], 'm_ffn1_norm': out['m_ffn1_norm'], 'm_ffn1_w_in': out['m_ffn1_w_in'], 'm_ffn1_w_out': out['m_ffn1_w_out'], 'm_mix_norm': out['m_mix_norm'], 'm_w_in': out['m_w_in'], 'm_hg_lb_table': out['m_hg_lb_table'], 'm_hg_out_norm': out['m_hg_out_norm'], 'm_w_hg_branch': out['m_w_hg_branch'], 'm_mla_q_lora_norm': out['m_mla_q_lora_norm'], 'm_w_q_up': out['m_w_q_up'], 'm_mla_kv_lora_norm': out['m_mla_kv_lora_norm'], 'm_w_kv_up': out['m_w_kv_up'], 'm_q_head_norm': out['m_q_head_norm'], 'm_k_head_norm': out['m_k_head_norm'], 'm_w_mla_branch': out['m_w_mla_branch'], 'm_w_merge': out['m_w_merge'], 'm_b_merge': out['m_b_merge'], 'm_w_out': out['m_w_out'], 'm_ffn2_norm': out['m_ffn2_norm'], 'm_ffn2_w_in': out['m_ffn2_w_in'], 'm_ffn2_w_out': out['m_ffn2_w_out'], 'm_final_norm': out['m_final_norm'], 'v_ffn1_norm': out['v_ffn1_norm'], 'v_ffn1_w_in': out['v_ffn1_w_in'], 'v_ffn1_w_out': out['v_ffn1_w_out'], 'v_mix_norm': out['v_mix_norm'], 'v_w_in': out['v_w_in'], 'v_hg_lb_table': out['v_hg_lb_table'], 'v_hg_out_norm': out['v_hg_out_norm'], 'v_w_hg_branch': out['v_w_hg_branch'], 'v_mla_q_lora_norm': out['v_mla_q_lora_norm'], 'v_w_q_up': out['v_w_q_up'], 'v_mla_kv_lora_norm': out['v_mla_kv_lora_norm'], 'v_w_kv_up': out['v_w_kv_up'], 'v_q_head_norm': out['v_q_head_norm'], 'v_k_head_norm': out['v_k_head_norm'], 'v_w_mla_branch': out['v_w_mla_branch'], 'v_w_merge': out['v_w_merge'], 'v_b_merge': out['v_b_merge'], 'v_w_out': out['v_w_out'], 'v_ffn2_norm': out['v_ffn2_norm'], 'v_ffn2_w_in': out['v_ffn2_w_in'], 'v_ffn2_w_out': out['v_ffn2_w_out'], 'v_final_norm': out['v_final_norm']}


def _loss(weights, diff, rest, loss_target):
    with _jax.named_scope("forward"):
        args = {**rest, TWIN_DIFF_INPUT: diff, **{k: w.astype(_WEIGHT_DTYPES[k]) for k, w in weights.items()}}
        y = _forward(args)
    with _jax.named_scope("loss_head"):
        err = _jnp.square(y.astype(_jnp.float32) - loss_target)
        return 0.5 * _jnp.sum(_jnp.mean(err, axis=-1)) if err.ndim else 0.5 * err


def _adamw(w, g, m, v):
    m = ADAM_B1 * m + (1.0 - ADAM_B1) * g
    v = ADAM_B2 * v + (1.0 - ADAM_B2) * _jnp.square(g)
    m_hat = m / (1.0 - ADAM_B1 ** ADAM_STEP)
    v_hat = v / (1.0 - ADAM_B2 ** ADAM_STEP)
    delta = -ADAM_LR * (m_hat / (_jnp.sqrt(v_hat) + ADAM_EPS) + ADAM_WD * w)
    return delta, m, v


def reference(x, positions, ffn1_norm, ffn1_w_in, ffn1_w_out, mix_norm, w_in, hg_lb_table, hg_out_norm, w_hg_branch, mla_q_lora_norm, w_q_up, mla_kv_lora_norm, w_kv_up, q_head_norm, k_head_norm, w_mla_branch, w_merge, b_merge, w_out, ffn2_norm, ffn2_w_in, ffn2_w_out, final_norm, loss_target, m_ffn1_norm, m_ffn1_w_in, m_ffn1_w_out, m_mix_norm, m_w_in, m_hg_lb_table, m_hg_out_norm, m_w_hg_branch, m_mla_q_lora_norm, m_w_q_up, m_mla_kv_lora_norm, m_w_kv_up, m_q_head_norm, m_k_head_norm, m_w_mla_branch, m_w_merge, m_b_merge, m_w_out, m_ffn2_norm, m_ffn2_w_in, m_ffn2_w_out, m_final_norm, v_ffn1_norm, v_ffn1_w_in, v_ffn1_w_out, v_mix_norm, v_w_in, v_hg_lb_table, v_hg_out_norm, v_w_hg_branch, v_mla_q_lora_norm, v_w_q_up, v_mla_kv_lora_norm, v_w_kv_up, v_q_head_norm, v_k_head_norm, v_w_mla_branch, v_w_merge, v_b_merge, v_w_out, v_ffn2_norm, v_ffn2_w_in, v_ffn2_w_out, v_final_norm):
    given = dict(x=x, positions=positions, ffn1_norm=ffn1_norm, ffn1_w_in=ffn1_w_in, ffn1_w_out=ffn1_w_out, mix_norm=mix_norm, w_in=w_in, hg_lb_table=hg_lb_table, hg_out_norm=hg_out_norm, w_hg_branch=w_hg_branch, mla_q_lora_norm=mla_q_lora_norm, w_q_up=w_q_up, mla_kv_lora_norm=mla_kv_lora_norm, w_kv_up=w_kv_up, q_head_norm=q_head_norm, k_head_norm=k_head_norm, w_mla_branch=w_mla_branch, w_merge=w_merge, b_merge=b_merge, w_out=w_out, ffn2_norm=ffn2_norm, ffn2_w_in=ffn2_w_in, ffn2_w_out=ffn2_w_out, final_norm=final_norm, loss_target=loss_target, m_ffn1_norm=m_ffn1_norm, m_ffn1_w_in=m_ffn1_w_in, m_ffn1_w_out=m_ffn1_w_out, m_mix_norm=m_mix_norm, m_w_in=m_w_in, m_hg_lb_table=m_hg_lb_table, m_hg_out_norm=m_hg_out_norm, m_w_hg_branch=m_w_hg_branch, m_mla_q_lora_norm=m_mla_q_lora_norm, m_w_q_up=m_w_q_up, m_mla_kv_lora_norm=m_mla_kv_lora_norm, m_w_kv_up=m_w_kv_up, m_q_head_norm=m_q_head_norm, m_k_head_norm=m_k_head_norm, m_w_mla_branch=m_w_mla_branch, m_w_merge=m_w_merge, m_b_merge=m_b_merge, m_w_out=m_w_out, m_ffn2_norm=m_ffn2_norm, m_ffn2_w_in=m_ffn2_w_in, m_ffn2_w_out=m_ffn2_w_out, m_final_norm=m_final_norm, v_ffn1_norm=v_ffn1_norm, v_ffn1_w_in=v_ffn1_w_in, v_ffn1_w_out=v_ffn1_w_out, v_mix_norm=v_mix_norm, v_w_in=v_w_in, v_hg_lb_table=v_hg_lb_table, v_hg_out_norm=v_hg_out_norm, v_w_hg_branch=v_w_hg_branch, v_mla_q_lora_norm=v_mla_q_lora_norm, v_w_q_up=v_w_q_up, v_mla_kv_lora_norm=v_mla_kv_lora_norm, v_w_kv_up=v_w_kv_up, v_q_head_norm=v_q_head_norm, v_k_head_norm=v_k_head_norm, v_w_mla_branch=v_w_mla_branch, v_w_merge=v_w_merge, v_b_merge=v_b_merge, v_w_out=v_w_out, v_ffn2_norm=v_ffn2_norm, v_ffn2_w_in=v_ffn2_w_in, v_ffn2_w_out=v_ffn2_w_out, v_final_norm=v_final_norm)
    weights = {n: given[n] for n in TWIN_WEIGHTS}
    shared = {n: given[n] for n in SHARED_INPUTS}
    per_example = {n: given[n] for n in ['x', 'positions']}
    grad_fn = _jax.value_and_grad(_loss, argnums=(0, 1))

    def one_microbatch(ex, loss_target):
        ex = dict(ex)
        diff = ex.pop(TWIN_DIFF_INPUT)
        return grad_fn(weights, diff, {**shared, **ex}, loss_target)

    if N_MICROBATCH == 1:
        loss, (grad_w, grad_x) = one_microbatch(per_example, given["loss_target"])
    else:
        def body(carry, xs):
            loss_sum, grad_sum = carry
            l_k, (gw_k, gx_k) = one_microbatch(xs[0], xs[1])
            with _jax.named_scope("update"):
                return (loss_sum + l_k, _jax.tree.map(_jnp.add, grad_sum, gw_k)), gx_k

        init = (_jnp.zeros((), _jnp.float32), _jax.tree.map(_jnp.zeros_like, weights))
        (loss, grad_w), grad_x = _jax.lax.scan(body, init, (per_example, given["loss_target"]))
    with _jax.named_scope("update"):
        delta_w, new_m, new_v = {}, {}, {}
        for n in TWIN_WEIGHTS:
            delta_w[n], new_m[n], new_v[n] = _adamw(weights[n], grad_w[n], given["m_" + n], given["v_" + n])
    return (loss, grad_x, *[grad_w[n] for n in TWIN_WEIGHTS], *[delta_w[n] for n in TWIN_WEIGHTS],
            *[new_m[n] for n in TWIN_WEIGHTS], *[new_v[n] for n in TWIN_WEIGHTS])
```

```python
import functools

import numpy as np
import jax
import jax.numpy as jnp
from jax import lax
from jax.experimental import pallas as pl
from jax.experimental.pallas import tpu as pltpu

F32 = jnp.float32
BF16 = jnp.bfloat16
MESH = pl.DeviceIdType.MESH

D = 1024
DFF = 2816
HEADS = 8
HK = 128
CHUNK = 64
ROPE = 64
QK = 192
QKP = 256
Q_LORA = 384
KV_LORA = 256
MLA_COLS = 768
EPS = 1e-6
ROPE_THETA = 10000.0
SCALE = QK ** -0.5
NEG = -1e30
EXP_CLAMP = 80.0

ADAM_LR = 0.001
ADAM_B1 = 0.9
ADAM_B2 = 0.999
ADAM_EPS = 1e-08
ADAM_WD = 0.01
ADAM_STEP = 10

PACK_W = 1024
PACK_ALIGN = 1024

TM = 512
TQ = 512
HG_BT = 512
TT = 512
ROW_TM = 256

VMEM_MB = 48

BIG = (
    ("ffn1_w_in", D, 2 * DFF, 1),
    ("ffn1_w_out", DFF, D, 0),
    ("w_in", D, 4800, 1),
    ("w_hg_branch", D, D, 0),
    ("w_q_up", Q_LORA, HEADS * QK, 1),
    ("w_kv_up", KV_LORA, HEADS * 2 * HK, 1),
    ("w_mla_branch", D, D, 0),
    ("w_merge", D, 2 * D, 1),
    ("w_out", D, D, 0),
    ("ffn2_w_in", D, 2 * DFF, 1),
    ("ffn2_w_out", DFF, D, 0),
)
SMALL = (
    ("ffn1_norm", (1, D)),
    ("mix_norm", (1, D)),
    ("hg_lb_table", (2, D)),
    ("hg_out_norm", (1, HK)),
    ("mla_q_lora_norm", (1, Q_LORA)),
    ("mla_kv_lora_norm", (1, KV_LORA)),
    ("q_head_norm", (1, QK)),
    ("k_head_norm", (1, QK)),
    ("b_merge", (1, 2 * D)),
    ("ffn2_norm", (1, D)),
    ("final_norm", (1, D)),
)
WEIGHT_ORDER = ("ffn1_norm", "ffn1_w_in", "ffn1_w_out", "mix_norm", "w_in", "hg_lb_table", "hg_out_norm",
                "w_hg_branch", "mla_q_lora_norm", "w_q_up", "mla_kv_lora_norm", "w_kv_up", "q_head_norm",
                "k_head_norm", "w_mla_branch", "w_merge", "b_merge", "w_out", "ffn2_norm", "ffn2_w_in",
                "ffn2_w_out", "final_norm")


def _call(body, **kw):
    return pl.pallas_call(body, **kw)


def _cp(vmem_mb=VMEM_MB):
    return pltpu.CompilerParams(vmem_limit_bytes=vmem_mb << 20)


def _dot(a, b):
    return lax.dot_general(a, b, (((1,), (0,)), ((), ())), preferred_element_type=F32)


def _dot_nt(a, b):
    return lax.dot_general(a, b, (((1,), (1,)), ((), ())), preferred_element_type=F32)


def _dot_tn(a, b):
    return lax.dot_general(a, b, (((0,), (0,)), ((), ())), preferred_element_type=F32)


def _sig(x):
    return jax.nn.sigmoid(x)


def _silu(x):
    return x * _sig(x)


def _dsilu(x):
    s = _sig(x)
    return s * (1.0 + x * (1.0 - s))


def _a_spec(arr, tm, kblk=None, kidx=0):
    kb = arr.shape[1] if kblk is None else kblk
    return arr, pl.BlockSpec((tm, kb), lambda i, j, kidx=kidx: (i, kidx))


def _b_nn(arr, tn, off=0):
    return arr, pl.BlockSpec((arr.shape[0], tn), lambda i, j, off=off: (0, j + off))


def _b_nt(arr, tn, kblk=None, kidx=0):
    kb = arr.shape[1] if kblk is None else kblk
    return arr, pl.BlockSpec((tn, kb), lambda i, j, kidx=kidx: (j, kidx))


def _e_tile(arr, tm, tn, off=0):
    return arr, pl.BlockSpec((tm, tn), lambda i, j, off=off: (i, j + off))


def _e_row(arr, tn, off=0):
    return arr, pl.BlockSpec((1, tn), lambda i, j, off=off: (0, j + off))


def _mm(name, As, Bs, dots, epi, extras, out_dtypes, m, n, tm, tn, trans_b=False):
    na, nb, ne = len(As), len(Bs), len(extras)

    def body(*refs):
        a_refs = refs[:na]
        b_refs = refs[na:na + nb]
        e_refs = refs[na + nb:na + nb + ne]
        o_refs = refs[na + nb + ne:]
        a_vals = [r[...].astype(BF16) for r in a_refs]
        accs = []
        for ai, bi in dots:
            b = b_refs[bi][...]
            accs.append(_dot_nt(a_vals[ai], b) if trans_b else _dot(a_vals[ai], b))
        outs = epi(accs, [r[...] for r in e_refs])
        for o_ref, o in zip(o_refs, outs):
            o_ref[...] = o.astype(o_ref.dtype)

    ops = list(As) + list(Bs) + list(extras)
    res = _call(
        body, name=name,
        grid=(m // tm, n // tn),
        in_specs=[s for _, s in ops],
        out_specs=[pl.BlockSpec((tm, tn), lambda i, j: (i, j)) for _ in out_dtypes],
        out_shape=[jax.ShapeDtypeStruct((m, n), dt) for dt in out_dtypes],
        compiler_params=_cp(),
    )(*[a for a, _ in ops])
    return res


def _mm_tn(name, a, b, scale=1.0, tm=1024, tn=1024):
    t, m = a.shape
    n = b.shape[1]
    tm, tn, tt = min(tm, m), min(tn, n), min(TT, t)
    nk = t // tt

    def body(a_ref, b_ref, o_ref):
        k = pl.program_id(2)

        @pl.when(k == 0)
        def _():
            o_ref[...] = jnp.zeros_like(o_ref)

        o_ref[...] += _dot_tn(a_ref[...].astype(BF16), b_ref[...].astype(BF16))
        if scale != 1.0:
            @pl.when(k == nk - 1)
            def _():
                o_ref[...] = o_ref[...] * scale

    return _call(
        body, name=name,
        grid=(m // tm, n // tn, nk),
        in_specs=[pl.BlockSpec((tt, tm), lambda i, j, k: (k, i)), pl.BlockSpec((tt, tn), lambda i, j, k: (k, j))],
        out_specs=pl.BlockSpec((tm, tn), lambda i, j, k: (i, j)),
        out_shape=jax.ShapeDtypeStruct((m, n), F32),
        compiler_params=_cp(),
    )(a, b)


def _rms_fwd(name, x, gain):
    t, d = x.shape
    tm = min(ROW_TM, t)

    def body(x_ref, g_ref, o_ref):
        xv = x_ref[...]
        r = lax.rsqrt(jnp.mean(xv * xv, axis=-1, keepdims=True) + EPS)
        o_ref[...] = (xv * r * g_ref[...]).astype(o_ref.dtype)

    return _call(
        body, name=name, grid=(t // tm,),
        in_specs=[pl.BlockSpec((tm, d), lambda i: (i, 0)), pl.BlockSpec((1, d), lambda i: (0, 0))],
        out_specs=pl.BlockSpec((tm, d), lambda i: (i, 0)),
        out_shape=jax.ShapeDtypeStruct((t, d), BF16),
        compiler_params=_cp(),
    )(x, gain)


def _rms_bwd_vals(xv, g, dn):
    r = lax.rsqrt(jnp.mean(xv * xv, axis=-1, keepdims=True) + EPS)
    xh = xv * r
    dxh = dn * g
    c = jnp.mean(dxh * xh, axis=-1, keepdims=True)
    return r * (dxh - xh * c), dn * xh


def _rms_bwd(name, x, gain, dn, dres):
    t, d = x.shape
    tm = min(ROW_TM, t)

    def body(x_ref, g_ref, dn_ref, dr_ref, dx_ref, dg_ref):
        @pl.when(pl.program_id(0) == 0)
        def _():
            dg_ref[...] = jnp.zeros_like(dg_ref)

        dx, dg = _rms_bwd_vals(x_ref[...], g_ref[...], dn_ref[...].astype(F32))
        dx_ref[...] = dr_ref[...] + dx
        dg_ref[...] += jnp.sum(dg, axis=0, keepdims=True)

    row = pl.BlockSpec((tm, d), lambda i: (i, 0))
    one = pl.BlockSpec((1, d), lambda i: (0, 0))
    return _call(
        body, name=name, grid=(t // tm,),
        in_specs=[row, one, row, row],
        out_specs=[row, one],
        out_shape=[jax.ShapeDtypeStruct((t, d), F32), jax.ShapeDtypeStruct((1, d), F32)],
        compiler_params=_cp(),
    )(x, gain, dn, dres)


def _final_loss(h, target, gain):
    t, d = h.shape
    tm = min(ROW_TM, t)

    def body(h_ref, t_ref, g_ref, dh_ref, dg_ref, l_ref):
        @pl.when(pl.program_id(0) == 0)
        def _():
            dg_ref[...] = jnp.zeros_like(dg_ref)
            l_ref[...] = jnp.zeros_like(l_ref)

        hv = h_ref[...]
        g = g_ref[...]
        r = lax.rsqrt(jnp.mean(hv * hv, axis=-1, keepdims=True) + EPS)
        xh = hv * r
        err = xh * g - t_ref[...]
        l_ref[...] += 0.5 * jnp.sum(jnp.mean(err * err, axis=-1, keepdims=True), axis=0, keepdims=True)
        dy = err * (1.0 / d)
        dxh = dy * g
        c = jnp.mean(dxh * xh, axis=-1, keepdims=True)
        dh_ref[...] = r * (dxh - xh * c)
        dg_ref[...] += jnp.sum(dy * xh, axis=0, keepdims=True)

    row = pl.BlockSpec((tm, d), lambda i: (i, 0))
    one = pl.BlockSpec((1, d), lambda i: (0, 0))
    return _call(
        body, name="final_loss", grid=(t // tm,),
        in_specs=[row, row, one],
        out_specs=[row, one, pl.BlockSpec((1, 128), lambda i: (0, 0))],
        out_shape=[jax.ShapeDtypeStruct((t, d), F32), jax.ShapeDtypeStruct((1, d), F32),
                   jax.ShapeDtypeStruct((1, 128), F32)],
        compiler_params=_cp(),
    )(h, target, gain)


def _colsum(name, x):
    t, n = x.shape
    tm = min(TM, t)

    def body(x_ref, o_ref):
        @pl.when(pl.program_id(0) == 0)
        def _():
            o_ref[...] = jnp.zeros_like(o_ref)

        o_ref[...] += jnp.sum(x_ref[...].astype(F32), axis=0, keepdims=True)

    return _call(
        body, name=name, grid=(t // tm,),
        in_specs=[pl.BlockSpec((tm, n), lambda i: (i, 0))],
        out_specs=pl.BlockSpec((1, n), lambda i: (0, 0)),
        out_shape=jax.ShapeDtypeStruct((1, n), F32),
        compiler_params=_cp(),
    )(x)


def _lora_norm_fwd(p_mla, gq, gkv):
    t = p_mla.shape[0]
    tm = min(ROW_TM, t)

    def body(p_ref, gq_ref, gkv_ref, q_ref, kv_ref):
        cq = p_ref[:, 0:Q_LORA]
        ckv = p_ref[:, Q_LORA:Q_LORA + KV_LORA]
        rq = lax.rsqrt(jnp.mean(cq * cq, axis=-1, keepdims=True) + EPS)
        rkv = lax.rsqrt(jnp.mean(ckv * ckv, axis=-1, keepdims=True) + EPS)
        q_ref[...] = (cq * rq * gq_ref[...]).astype(BF16)
        kv_ref[...] = (ckv * rkv * gkv_ref[...]).astype(BF16)

    return _call(
        body, name="lora_norm_fwd", grid=(t // tm,),
        in_specs=[pl.BlockSpec((tm, MLA_COLS), lambda i: (i, 0)),
                  pl.BlockSpec((1, Q_LORA), lambda i: (0, 0)), pl.BlockSpec((1, KV_LORA), lambda i: (0, 0))],
        out_specs=[pl.BlockSpec((tm, Q_LORA), lambda i: (i, 0)), pl.BlockSpec((tm, KV_LORA), lambda i: (i, 0))],
        out_shape=[jax.ShapeDtypeStruct((t, Q_LORA), BF16), jax.ShapeDtypeStruct((t, KV_LORA), BF16)],
        compiler_params=_cp(),
    )(p_mla, gq, gkv)


def _lora_norm_bwd(p_mla, gq, gkv, dcqn, dckvn, dkpe):
    t = p_mla.shape[0]
    tm = min(ROW_TM, t)

    def body(p_ref, gq_ref, gkv_ref, dq_ref, dkv_ref, dkpe_ref, dp_ref, dgq_ref, dgkv_ref):
        @pl.when(pl.program_id(0) == 0)
        def _():
            dgq_ref[...] = jnp.zeros_like(dgq_ref)
            dgkv_ref[...] = jnp.zeros_like(dgkv_ref)

        dcq, dgq = _rms_bwd_vals(p_ref[:, 0:Q_LORA], gq_ref[...], dq_ref[...])
        dckv, dgkv = _rms_bwd_vals(p_ref[:, Q_LORA:Q_LORA + KV_LORA], gkv_ref[...], dkv_ref[...])
        dp_ref[:, 0:Q_LORA] = dcq.astype(BF16)
        dp_ref[:, Q_LORA:Q_LORA + KV_LORA] = dckv.astype(BF16)
        dp_ref[:, Q_LORA + KV_LORA:MLA_COLS] = dkpe_ref[...].astype(BF16)
        dgq_ref[...] += jnp.sum(dgq, axis=0, keepdims=True)
        dgkv_ref[...] += jnp.sum(dgkv, axis=0, keepdims=True)

    return _call(
        body, name="lora_norm_bwd", grid=(t // tm,),
        in_specs=[pl.BlockSpec((tm, MLA_COLS), lambda i: (i, 0)),
                  pl.BlockSpec((1, Q_LORA), lambda i: (0, 0)), pl.BlockSpec((1, KV_LORA), lambda i: (0, 0)),
                  pl.BlockSpec((tm, Q_LORA), lambda i: (i, 0)), pl.BlockSpec((tm, KV_LORA), lambda i: (i, 0)),
                  pl.BlockSpec((tm, HK), lambda i: (i, 0))],
        out_specs=[pl.BlockSpec((tm, MLA_COLS), lambda i: (i, 0)),
                   pl.BlockSpec((1, Q_LORA), lambda i: (0, 0)), pl.BlockSpec((1, KV_LORA), lambda i: (0, 0))],
        out_shape=[jax.ShapeDtypeStruct((t, MLA_COLS), BF16), jax.ShapeDtypeStruct((1, Q_LORA), F32),
                   jax.ShapeDtypeStruct((1, KV_LORA), F32)],
        compiler_params=_cp(),
    )(p_mla, gq, gkv, dcqn, dckvn, dkpe)


def _cumsum_rows(x, row):
    for s in (1, 2, 4, 8, 16, 32):
        x = x + jnp.where(row >= s, pltpu.roll(x, s, 0), 0.0)
    return x


def _rcumsum_rows(x, row):
    for s in (1, 2, 4, 8, 16, 32):
        x = x + jnp.where(row < CHUNK - s, pltpu.roll(x, CHUNK - s, 0), 0.0)
    return x


def _hg_gates(qr, z, lb, row):
    q = _silu(qr)
    sg = _sig(z)
    f = lb + (1.0 - lb) * sg
    lf = jnp.log(f)
    k = (1.0 - lb) * (1.0 - sg)
    cum = _cumsum_rows(lf, row)
    mid = jnp.sum(jnp.where(row < CHUNK // 2, lf, 0.0), axis=0, keepdims=True)
    last = jnp.sum(lf, axis=0, keepdims=True)
    e_q = jnp.exp(jnp.minimum(cum - mid, EXP_CLAMP))
    e_k = jnp.exp(jnp.minimum(mid - cum, EXP_CLAMP))
    e_a = jnp.exp(cum)
    e_l = jnp.exp(last - cum)
    return q, sg, f, k, last, e_q, e_k, e_a, e_l


def _hgrn_fwd(p_hg, tab, gain):
    t = p_hg.shape[0]
    bt = min(HG_BT, t)
    nb, nc = t // bt, bt // CHUNK

    def body(q_ref, f_ref, i_ref, g_ref, tab_ref, gain_ref, o_ref, ho_ref, st_ref, state):
        @pl.when(pl.program_id(1) == 0)
        def _():
            state[...] = jnp.zeros_like(state)

        lb = _sig(tab_ref[0:1, :] - tab_ref[1:2, :])
        row = lax.broadcasted_iota(jnp.int32, (CHUNK, HK), 0)
        tril = lax.broadcasted_iota(jnp.int32, (CHUNK, CHUNK), 0) >= lax.broadcasted_iota(jnp.int32, (CHUNK, CHUNK), 1)
        gain_v = gain_ref[...]

        def chunk(c, carry):
            sl = pl.ds(pl.multiple_of(c * CHUNK, CHUNK), CHUNK)
            v = i_ref[sl, :].astype(BF16)
            q, _, _, k, last, e_q, e_k, e_a, e_l = _hg_gates(q_ref[sl, :], f_ref[sl, :], lb, row)
            st = state[...]
            st_ref[c] = st
            p = jnp.where(tril, _dot_nt((q * e_q).astype(BF16), (k * e_k).astype(BF16)), 0.0)
            o = _dot(p.astype(BF16), v) + _dot_nt((q * e_a).astype(BF16), st.astype(BF16))
            state[...] = jnp.exp(last) * st + _dot_tn(v, (k * e_l).astype(BF16))
            o_ref[sl, :] = o
            r = lax.rsqrt(jnp.mean(o * o, axis=-1, keepdims=True) + EPS)
            ho_ref[sl, :] = (o * r * gain_v * _silu(g_ref[sl, :])).astype(BF16)
            return carry

        lax.fori_loop(0, nc, chunk, 0)

    def col(k):
        return pl.BlockSpec((bt, HK), lambda h, j, k=k: (j, k * HEADS + h))

    return _call(
        body, name="hgrn_fwd", grid=(HEADS, nb),
        in_specs=[col(0), col(1), col(2), col(3),
                  pl.BlockSpec((2, HK), lambda h, j: (0, h)), pl.BlockSpec((1, HK), lambda h, j: (0, 0))],
        out_specs=[pl.BlockSpec((bt, HK), lambda h, j: (j, h)), pl.BlockSpec((bt, HK), lambda h, j: (j, h)),
                   pl.BlockSpec((None, nc, HK, HK), lambda h, j: (h, j, 0, 0))],
        out_shape=[jax.ShapeDtypeStruct((t, D), F32), jax.ShapeDtypeStruct((t, D), BF16),
                   jax.ShapeDtypeStruct((HEADS, t // CHUNK, HK, HK), F32)],
        scratch_shapes=[pltpu.VMEM((HK, HK), F32)],
        compiler_params=_cp(),
    )(p_hg, p_hg, p_hg, p_hg, tab, gain)


def _hgrn_bwd(p_hg, tab, gain, o_raw, states, dho):
    t = p_hg.shape[0]
    bt = min(HG_BT, t)
    nb, nc = t // bt, bt // CHUNK

    def body(q_ref, f_ref, i_ref, g_ref, tab_ref, gain_ref, o_ref, st_ref, dho_ref,
             dq_ref, df_ref, di_ref, dg_ref, dtab_ref, dgain_ref, dstate, dlb):
        h, j = pl.program_id(0), pl.program_id(1)

        @pl.when(jnp.logical_and(h == 0, j == 0))
        def _():
            dgain_ref[...] = jnp.zeros_like(dgain_ref)

        @pl.when(j == 0)
        def _():
            dstate[...] = jnp.zeros_like(dstate)
            dlb[...] = jnp.zeros_like(dlb)

        lb = _sig(tab_ref[0:1, :] - tab_ref[1:2, :])
        row = lax.broadcasted_iota(jnp.int32, (CHUNK, HK), 0)
        tril = lax.broadcasted_iota(jnp.int32, (CHUNK, CHUNK), 0) >= lax.broadcasted_iota(jnp.int32, (CHUNK, CHUNK), 1)
        gain_v = gain_ref[...]

        def chunk(cc, carry):
            c = nc - 1 - cc
            sl = pl.ds(pl.multiple_of(c * CHUNK, CHUNK), CHUNK)
            qr = q_ref[sl, :]
            vf = i_ref[sl, :]
            v = vf.astype(BF16)
            gr = g_ref[sl, :]
            q, sg, f, k, last, e_q, e_k, e_a, e_l = _hg_gates(qr, f_ref[sl, :], lb, row)
            o = o_ref[sl, :]
            r = lax.rsqrt(jnp.mean(o * o, axis=-1, keepdims=True) + EPS)
            oh = o * r
            dh = dho_ref[sl, :].astype(F32)
            dnorm = dh * _silu(gr)
            dg_ref[sl, :] = (dh * oh * gain_v * _dsilu(gr)).astype(BF16)
            dgain_ref[...] += jnp.sum(dnorm * oh, axis=0, keepdims=True)
            dxh = dnorm * gain_v
            do = (r * (dxh - oh * jnp.mean(dxh * oh, axis=-1, keepdims=True))).astype(BF16)
            st0 = st_ref[c]
            st0_b = st0.astype(BF16)
            ds1 = dstate[...]
            ds1_b = ds1.astype(BF16)
            qt = (q * e_q).astype(BF16)
            kt = (k * e_k).astype(BF16)
            qd = (q * e_a).astype(BF16)
            kd = (k * e_l).astype(BF16)
            p = jnp.where(tril, _dot_nt(qt, kt), 0.0).astype(BF16)
            dp = jnp.where(tril, _dot_nt(do, v), 0.0).astype(BF16)
            dv = _dot_tn(p, do) + _dot_nt(kd, ds1_b)
            dqt = _dot(dp, kt)
            dkt = _dot_tn(dp, qt)
            dq_inter = _dot(do, st0_b) * e_a
            dk_inter = _dot(v, ds1_b) * e_l
            dq = dqt * e_q + dq_inter
            dk = dkt * e_k + dk_inter
            e_last = jnp.exp(last)
            dstate[...] = _dot_tn(do, qd) + e_last * ds1
            dlast = jnp.sum(k * dk_inter, axis=0, keepdims=True) + e_last * jnp.sum(ds1 * st0, axis=0, keepdims=True)
            da = (qt.astype(F32) * dqt - kt.astype(F32) * dkt + q * dq_inter - k * dk_inter
                  + jnp.where(row == CHUNK - 1, dlast, 0.0))
            dlf = _rcumsum_rows(da, row)
            dfv = dlf / f - dk
            df_ref[sl, :] = (dfv * (1.0 - lb) * sg * (1.0 - sg)).astype(BF16)
            dlb[...] += jnp.sum(dfv * (1.0 - sg), axis=0, keepdims=True)
            dq_ref[sl, :] = (dq * _dsilu(qr)).astype(BF16)
            di_ref[sl, :] = dv.astype(BF16)
            return carry

        lax.fori_loop(0, nc, chunk, 0)

        @pl.when(j == nb - 1)
        def _():
            d0 = dlb[...] * lb * (1.0 - lb)
            dtab_ref[0:1, :] = d0
            dtab_ref[1:2, :] = -d0

    def col(k):
        return pl.BlockSpec((bt, HK), lambda h, j, k=k: (nb - 1 - j, k * HEADS + h))

    tok = pl.BlockSpec((bt, HK), lambda h, j: (nb - 1 - j, h))
    return _call(
        body, name="hgrn_bwd", grid=(HEADS, nb),
        in_specs=[col(0), col(1), col(2), col(3),
                  pl.BlockSpec((2, HK), lambda h, j: (0, h)), pl.BlockSpec((1, HK), lambda h, j: (0, 0)),
                  tok, pl.BlockSpec((None, nc, HK, HK), lambda h, j: (h, nb - 1 - j, 0, 0)), tok],
        out_specs=[tok, tok, tok, tok,
                   pl.BlockSpec((2, HK), lambda h, j: (0, h)), pl.BlockSpec((1, HK), lambda h, j: (0, 0))],
        out_shape=[jax.ShapeDtypeStruct((t, D), BF16)] * 4
        + [jax.ShapeDtypeStruct((2, D), F32), jax.ShapeDtypeStruct((1, HK), F32)],
        scratch_shapes=[pltpu.VMEM((HK, HK), F32), pltpu.VMEM((1, HK), F32)],
        compiler_params=_cp(),
    )(p_hg, p_hg, p_hg, p_hg, tab, gain, o_raw, states, dho)


def _rope_tables(pos):
    t = pos.shape[0]
    tm = min(ROW_TM, t)
    inv = np.zeros((1, HK), np.float32)
    freq = (ROPE_THETA ** (-np.arange(0, ROPE, 2, dtype=np.float32) / ROPE)).astype(np.float32)
    inv[0, 0:ROPE // 2] = freq
    inv[0, ROPE // 2:ROPE] = freq
    sign = np.zeros((1, HK), np.float32)
    sign[0, 0:ROPE // 2] = -1.0
    sign[0, ROPE // 2:ROPE] = 1.0

    def body(pos_ref, inv_ref, sign_ref, cos_ref, sin_ref):
        ang = pos_ref[...].astype(F32) * inv_ref[...]
        cos_ref[...] = jnp.cos(ang)
        sin_ref[...] = jnp.sin(ang) * sign_ref[...]

    one = pl.BlockSpec((1, HK), lambda i: (0, 0))
    row = pl.BlockSpec((tm, HK), lambda i: (i, 0))
    return _call(
        body, name="rope_tables", grid=(t // tm,),
        in_specs=[pl.BlockSpec((tm, 1), lambda i: (i, 0)), one, one],
        out_specs=[row, row],
        out_shape=[jax.ShapeDtypeStruct((t, HK), F32)] * 2,
        compiler_params=_cp(),
    )(pos, jnp.asarray(inv), jnp.asarray(sign))


def _rope(x, cos, sin_signed):
    lane = lax.broadcasted_iota(jnp.int32, x.shape, 1)
    other = jnp.where(lane < ROPE // 2, pltpu.roll(x, HK - ROPE // 2, 1), pltpu.roll(x, ROPE // 2, 1))
    return x * cos + other * sin_signed


def _head_norm(xn, xr, g):
    ss = jnp.sum(xn * xn, axis=-1, keepdims=True) + jnp.sum(xr * xr, axis=-1, keepdims=True)
    r = lax.rsqrt(ss * (1.0 / QK) + EPS)
    return xn * r, xr * r


def _head_norm_bwd(xn, xr, g_n, g_r, dn, dr):
    hn, hr = _head_norm(xn, xr, None)
    ss = jnp.sum(xn * xn, axis=-1, keepdims=True) + jnp.sum(xr * xr, axis=-1, keepdims=True)
    r = lax.rsqrt(ss * (1.0 / QK) + EPS)
    dxn, dxr = dn * g_n, dr * g_r
    c = (jnp.sum(dxn * hn, axis=-1, keepdims=True) + jnp.sum(dxr * hr, axis=-1, keepdims=True)) * (1.0 / QK)
    return r * (dxn - hn * c), r * (dxr - hr * c), dn * hn, dr * hr


def _mla_prep_fwd(qf, kv, p_mla, cos, sin, gq, gk):
    t = qf.shape[0]
    tm = min(ROW_TM, t)

    def body(qf_ref, kv_ref, kpe_ref, cos_ref, sin_ref, gq_ref, gk_ref, q_ref, k_ref, v_ref):
        cos_v, sin_v = cos_ref[...], sin_ref[...]
        qn, qr = _head_norm(qf_ref[:, 0:HK], qf_ref[:, HK:QKP], None)
        q_ref[:, 0:HK] = (qn * gq_ref[:, 0:HK] * SCALE).astype(BF16)
        q_ref[:, HK:QKP] = (_rope(qr * gq_ref[:, HK:QKP], cos_v, sin_v) * SCALE).astype(BF16)
        kn, kr = _head_norm(kv_ref[:, 0:HK], kpe_ref[...], None)
        k_ref[:, 0:HK] = (kn * gk_ref[:, 0:HK]).astype(BF16)
        k_ref[:, HK:QKP] = _rope(kr * gk_ref[:, HK:QKP], cos_v, sin_v).astype(BF16)
        v_ref[...] = kv_ref[:, HK:QKP].astype(BF16)

    head = pl.BlockSpec((tm, QKP), lambda i, h: (i, h))
    tok = pl.BlockSpec((tm, HK), lambda i, h: (i, 0))
    gain = pl.BlockSpec((1, QKP), lambda i, h: (0, 0))
    return _call(
        body, name="mla_prep_fwd", grid=(t // tm, HEADS),
        in_specs=[head, head, pl.BlockSpec((tm, HK), lambda i, h: (i, MLA_COLS // HK - 1)), tok, tok, gain, gain],
        out_specs=[pl.BlockSpec((None, tm, QKP), lambda i, h: (h, i, 0)),
                   pl.BlockSpec((None, tm, QKP), lambda i, h: (h, i, 0)),
                   pl.BlockSpec((None, tm, HK), lambda i, h: (h, i, 0))],
        out_shape=[jax.ShapeDtypeStruct((HEADS, t, QKP), BF16), jax.ShapeDtypeStruct((HEADS, t, QKP), BF16),
                   jax.ShapeDtypeStruct((HEADS, t, HK), BF16)],
        compiler_params=_cp(),
    )(qf, kv, p_mla, cos, sin, gq, gk)


def _mla_prep_bwd(qf, kv, p_mla, cos, sin, gq, gk, dq, dk, dv):
    t = qf.shape[0]
    tm = min(ROW_TM, t)

    def body(qf_ref, kv_ref, kpe_ref, cos_ref, sin_ref, gq_ref, gk_ref, dq_ref, dk_ref, dv_ref,
             dqf_ref, dkv_ref, dkpe_ref, dgq_ref, dgk_ref):
        i, h = pl.program_id(0), pl.program_id(1)

        @pl.when(jnp.logical_and(i == 0, h == 0))
        def _():
            dgq_ref[...] = jnp.zeros_like(dgq_ref)
            dgk_ref[...] = jnp.zeros_like(dgk_ref)

        @pl.when(h == 0)
        def _():
            dkpe_ref[...] = jnp.zeros_like(dkpe_ref)

        cos_v, sin_v = cos_ref[...], -sin_ref[...]
        dqn = dq_ref[:, 0:HK].astype(F32) * SCALE
        dqr = _rope(dq_ref[:, HK:QKP].astype(F32), cos_v, sin_v) * SCALE
        a, b, ga, gb = _head_norm_bwd(qf_ref[:, 0:HK], qf_ref[:, HK:QKP], gq_ref[:, 0:HK], gq_ref[:, HK:QKP], dqn, dqr)
        dqf_ref[:, 0:HK] = a.astype(BF16)
        dqf_ref[:, HK:QKP] = b.astype(BF16)
        dgq_ref[:, 0:HK] += jnp.sum(ga, axis=0, keepdims=True)
        dgq_ref[:, HK:QKP] += jnp.sum(gb, axis=0, keepdims=True)
        dkn = dk_ref[:, 0:HK].astype(F32)
        dkr = _rope(dk_ref[:, HK:QKP].astype(F32), cos_v, sin_v)
        a, b, ga, gb = _head_norm_bwd(kv_ref[:, 0:HK], kpe_ref[...], gk_ref[:, 0:HK], gk_ref[:, HK:QKP], dkn, dkr)
        dkv_ref[:, 0:HK] = a.astype(BF16)
        dkv_ref[:, HK:QKP] = dv_ref[...].astype(BF16)
        dkpe_ref[...] += b
        dgk_ref[:, 0:HK] += jnp.sum(ga, axis=0, keepdims=True)
        dgk_ref[:, HK:QKP] += jnp.sum(gb, axis=0, keepdims=True)

    head = pl.BlockSpec((tm, QKP), lambda i, h: (i, h))
    tok = pl.BlockSpec((tm, HK), lambda i, h: (i, 0))
    gain = pl.BlockSpec((1, QKP), lambda i, h: (0, 0))
    hq = pl.BlockSpec((None, tm, QKP), lambda i, h: (h, i, 0))
    return _call(
        body, name="mla_prep_bwd", grid=(t // tm, HEADS),
        in_specs=[head, head, pl.BlockSpec((tm, HK), lambda i, h: (i, MLA_COLS // HK - 1)), tok, tok, gain, gain,
                  hq, hq, pl.BlockSpec((None, tm, HK), lambda i, h: (h, i, 0))],
        out_specs=[head, head, tok, gain, gain],
        out_shape=[jax.ShapeDtypeStruct((t, HEADS * QKP), BF16), jax.ShapeDtypeStruct((t, HEADS * QKP), BF16),
                   jax.ShapeDtypeStruct((t, HK), F32), jax.ShapeDtypeStruct((1, QKP), F32),
                   jax.ShapeDtypeStruct((1, QKP), F32)],
        compiler_params=_cp(),
    )(qf, kv, p_mla, cos, sin, gq, gk, dq, dk, dv)


def _chunk_mask(tq):
    r = lax.broadcasted_iota(jnp.int32, (tq, tq), 0)
    c = lax.broadcasted_iota(jnp.int32, (tq, tq), 1)
    return jnp.right_shift(r, 6) >= jnp.right_shift(c, 6)


def _flash_fwd(q, k, v):
    t = q.shape[1]
    tq = min(TQ, t)
    nq = t // tq
    pairs = [(i, j) for i in range(nq) for j in range(i + 1)]
    qi = jnp.asarray([p[0] for p in pairs], jnp.int32)
    kj = jnp.asarray([p[1] for p in pairs], jnp.int32)

    def body(qi_ref, kj_ref, q_ref, k_ref, v_ref, o_ref, lse_ref, m_s, l_s, acc_s):
        n = pl.program_id(1)
        i, j = qi_ref[n], kj_ref[n]

        @pl.when(j == 0)
        def _():
            m_s[...] = jnp.full_like(m_s, NEG)
            l_s[...] = jnp.zeros_like(l_s)
            acc_s[...] = jnp.zeros_like(acc_s)

        s = _dot_nt(q_ref[...], k_ref[...])
        s = jnp.where(jnp.logical_or(j < i, _chunk_mask(tq)), s, NEG)
        m_old = m_s[...]
        m_new = jnp.maximum(m_old, jnp.max(s, axis=-1, keepdims=True))
        alpha = jnp.exp(m_old - m_new)
        p = jnp.exp(s - m_new)
        l_s[...] = alpha * l_s[...] + jnp.sum(p, axis=-1, keepdims=True)
        acc_s[...] = alpha * acc_s[...] + _dot(p.astype(BF16), v_ref[...])
        m_s[...] = m_new

        @pl.when(j == i)
        def _():
            l = l_s[...]
            o_ref[...] = (acc_s[...] / l).astype(BF16)
            lse_ref[...] = jnp.broadcast_to(m_s[...] + jnp.log(l), (tq, HK))

    grid_spec = pltpu.PrefetchScalarGridSpec(
        num_scalar_prefetch=2, grid=(HEADS, len(pairs)),
        in_specs=[pl.BlockSpec((None, tq, QKP), lambda h, n, qi, kj: (h, qi[n], 0)),
                  pl.BlockSpec((None, tq, QKP), lambda h, n, qi, kj: (h, kj[n], 0)),
                  pl.BlockSpec((None, tq, HK), lambda h, n, qi, kj: (h, kj[n], 0))],
        out_specs=[pl.BlockSpec((tq, HK), lambda h, n, qi, kj: (qi[n], h)),
                   pl.BlockSpec((None, tq, HK), lambda h, n, qi, kj: (h, qi[n], 0))],
        scratch_shapes=[pltpu.VMEM((tq, 1), F32), pltpu.VMEM((tq, 1), F32), pltpu.VMEM((tq, HK), F32)],
    )
    return _call(
        body, name="flash_fwd", grid_spec=grid_spec,
        out_shape=[jax.ShapeDtypeStruct((t, D), BF16), jax.ShapeDtypeStruct((HEADS, t, HK), F32)],
        compiler_params=_cp(),
    )(qi, kj, q, k, v)


def _flash_bwd(q, k, v, o, lse, do):
    t = q.shape[1]
    tq = min(TQ, t)
    nq = t // tq
    pairs = [(i, j) for j in range(nq) for i in range(j, nq)]
    qi = jnp.asarray([p[0] for p in pairs], jnp.int32)
    kj = jnp.asarray([p[1] for p in pairs], jnp.int32)
    npairs = len(pairs)

    def body(qi_ref, kj_ref, q_ref, k_ref, v_ref, o_ref, lse_ref, do_ref, dq_ref, dk_ref, dv_ref):
        n = pl.program_id(1)
        i, j = qi_ref[n], kj_ref[n]

        @pl.when(n == 0)
        def _():
            dq_ref[...] = jnp.zeros_like(dq_ref)

        @pl.when(i == j)
        def _():
            dk_ref[...] = jnp.zeros_like(dk_ref)
            dv_ref[...] = jnp.zeros_like(dv_ref)

        qv, kv_, dov = q_ref[...], k_ref[...], do_ref[...]
        s = _dot_nt(qv, kv_)
        keep = jnp.logical_or(j < i, _chunk_mask(tq))
        p = jnp.where(keep, jnp.exp(s - lse_ref[:, 0:1]), 0.0)
        delta = jnp.sum(dov.astype(F32) * o_ref[...].astype(F32), axis=-1, keepdims=True)
        dv_ref[...] += _dot_tn(p.astype(BF16), dov)
        dp = _dot_nt(dov, v_ref[...])
        ds = (p * (dp - delta)).astype(BF16)
        dk_ref[...] += _dot_tn(ds, qv)
        rows = pl.ds(pl.multiple_of(i * tq, tq), tq)
        dq_ref[rows, :] += _dot(ds, kv_)

    grid_spec = pltpu.PrefetchScalarGridSpec(
        num_scalar_prefetch=2, grid=(HEADS, npairs),
        in_specs=[pl.BlockSpec((None, tq, QKP), lambda h, n, qi, kj: (h, qi[n], 0)),
                  pl.BlockSpec((None, tq, QKP), lambda h, n, qi, kj: (h, kj[n], 0)),
                  pl.BlockSpec((None, tq, HK), lambda h, n, qi, kj: (h, kj[n], 0)),
                  pl.BlockSpec((tq, HK), lambda h, n, qi, kj: (qi[n], h)),
                  pl.BlockSpec((None, tq, HK), lambda h, n, qi, kj: (h, qi[n], 0)),
                  pl.BlockSpec((tq, HK), lambda h, n, qi, kj: (qi[n], h))],
        out_specs=[pl.BlockSpec((None, t, QKP), lambda h, n, qi, kj: (h, 0, 0)),
                   pl.BlockSpec((None, tq, QKP), lambda h, n, qi, kj: (h, kj[n], 0)),
                   pl.BlockSpec((None, tq, HK), lambda h, n, qi, kj: (h, kj[n], 0))],
    )
    return _call(
        body, name="flash_bwd", grid_spec=grid_spec,
        out_shape=[jax.ShapeDtypeStruct((HEADS, t, QKP), F32), jax.ShapeDtypeStruct((HEADS, t, QKP), F32),
                   jax.ShapeDtypeStruct((HEADS, t, HK), F32)],
        compiler_params=_cp(56),
    )(qi, kj, q, k, v, o, lse, do)


def _adamw(name, w, g, m, v):
    r, c = w.shape
    tr = r if r <= 256 else next(k for k in (256, 352, 384) if r % k == 0)

    def body(w_ref, g_ref, m_ref, v_ref, d_ref, nm_ref, nv_ref):
        gv = g_ref[...]
        nm = ADAM_B1 * m_ref[...] + (1.0 - ADAM_B1) * gv
        nv = ADAM_B2 * v_ref[...] + (1.0 - ADAM_B2) * (gv * gv)
        m_hat = nm / (1.0 - ADAM_B1 ** ADAM_STEP)
        v_hat = nv / (1.0 - ADAM_B2 ** ADAM_STEP)
        d_ref[...] = -ADAM_LR * (m_hat / (jnp.sqrt(v_hat) + ADAM_EPS) + ADAM_WD * w_ref[...])
        nm_ref[...] = nm
        nv_ref[...] = nv

    blk = pl.BlockSpec((tr, c), lambda i: (i, 0))
    return _call(
        body, name=name, grid=(r // tr,),
        in_specs=[blk] * 4, out_specs=[blk] * 3,
        out_shape=[jax.ShapeDtypeStruct((r, c), F32)] * 3,
        compiler_params=_cp(),
    )(w, g, m, v)


def _place():
    return lax.axis_index("x"), lax.axis_index("y"), lax.axis_index("c")


def _other_chips(x, y):
    return [(1 - x, y), (x, 1 - y), (1 - x, 1 - y)]


def _gather_weights(shard):
    r = shard.shape[0]
    half = r // 2

    def body(s_ref, g_ref, send_sems, recv_sems, local_sem):
        x, y, c = _place()
        sibling = (x, y, 1 - c)
        chips = _other_chips(x, y)

        def rows(px, py, pc):
            return g_ref.at[2 * px + py, pl.ds(pc * half, half), :]

        def copy(k, block, to, src=None):
            return pltpu.make_async_remote_copy(
                src_ref=rows(*block) if src is None else src, dst_ref=rows(*block),
                send_sem=send_sems.at[k], recv_sem=recv_sems.at[k], device_id=to, device_id_type=MESH)

        mine = pltpu.make_async_copy(s_ref, g_ref.at[2 * x + y], local_sem)
        mine.start()
        first = [copy(j, (x, y, c), (*chip, c), src=s_ref.at[pl.ds(c * half, half), :]) for j, chip in enumerate(chips)]
        for cp in first:
            cp.start()
        passed = [copy(3 + j, (*chip, c), sibling) for j, chip in enumerate(chips)]
        for j, chip in enumerate(chips):
            copy(j, (*chip, c), (x, y, c)).wait_recv()
            passed[j].start()
        for j, chip in enumerate(chips):
            copy(3 + j, (*chip, 1 - c), (x, y, c)).wait_recv()
        for cp in first + passed:
            cp.wait_send()
        mine.wait()

    return _call(
        body, name="gather_weights",
        in_specs=[pl.BlockSpec(memory_space=pl.ANY)],
        out_specs=pl.BlockSpec(memory_space=pl.ANY),
        out_shape=jax.ShapeDtypeStruct((4, r, PACK_W), shard.dtype),
        scratch_shapes=[pltpu.SemaphoreType.DMA((6,)), pltpu.SemaphoreType.DMA((6,)), pltpu.SemaphoreType.DMA],
    )(shard)


def _swap_halves(gp):
    r = gp.shape[1]
    half = r // 2

    def body(g_ref, o_ref, send_sem, recv_sem):
        x, y, c = _place()
        cp = pltpu.make_async_remote_copy(
            src_ref=g_ref.at[:, pl.ds((1 - c) * half, half), :], dst_ref=o_ref,
            send_sem=send_sem, recv_sem=recv_sem, device_id=(x, y, 1 - c), device_id_type=MESH)
        cp.start()
        cp.wait()

    return _call(
        body, name="grad_swap_halves",
        in_specs=[pl.BlockSpec(memory_space=pl.ANY)],
        out_specs=pl.BlockSpec(memory_space=pl.ANY),
        out_shape=jax.ShapeDtypeStruct((4, half, PACK_W), gp.dtype),
        scratch_shapes=[pltpu.SemaphoreType.DMA, pltpu.SemaphoreType.DMA],
    )(gp)


def _chip_sum(gp, got, c_arr):
    half = got.shape[1]
    tr = 512
    nb = half // tr

    def body(c_ref, a_ref, b_ref, o_ref):
        o_ref[...] = a_ref[...] + b_ref[...]

    grid_spec = pltpu.PrefetchScalarGridSpec(
        num_scalar_prefetch=1, grid=(4, nb),
        in_specs=[pl.BlockSpec((None, tr, PACK_W), lambda s, i, c: (s, c[0] * nb + i, 0)),
                  pl.BlockSpec((None, tr, PACK_W), lambda s, i, c: (s, i, 0))],
        out_specs=pl.BlockSpec((None, tr, PACK_W), lambda s, i, c: (s, i, 0)),
    )
    return _call(
        body, name="grad_chip_sum", grid_spec=grid_spec,
        out_shape=jax.ShapeDtypeStruct(got.shape, F32),
        compiler_params=_cp(),
    )(c_arr, gp, got)


def _scatter_chip_sums(cs):
    h = cs.shape[1]

    def body(s_ref, o_ref, send_sems, recv_sems):
        x, y, c = _place()
        cps = []
        for j, (px, py) in enumerate(_other_chips(x, y)):
            cps.append(pltpu.make_async_remote_copy(
                src_ref=s_ref.at[2 * px + py], dst_ref=o_ref.at[j],
                send_sem=send_sems.at[j], recv_sem=recv_sems.at[j], device_id=(px, py, c), device_id_type=MESH))
        for cp in cps:
            cp.start()
        for cp in cps:
            cp.wait()

    return _call(
        body, name="grad_scatter",
        in_specs=[pl.BlockSpec(memory_space=pl.ANY)],
        out_specs=pl.BlockSpec(memory_space=pl.ANY),
        out_shape=jax.ShapeDtypeStruct((3, h, PACK_W), cs.dtype),
        scratch_shapes=[pltpu.SemaphoreType.DMA((3,)), pltpu.SemaphoreType.DMA((3,))],
    )(cs)


def _shard_sum(cs, got, k_arr):
    h = cs.shape[1]
    tr = 512

    def body(k_ref, a_ref, b_ref, o_ref):
        o_ref[...] = ((a_ref[...] + b_ref[0]) + b_ref[1]) + b_ref[2]

    grid_spec = pltpu.PrefetchScalarGridSpec(
        num_scalar_prefetch=1, grid=(h // tr,),
        in_specs=[pl.BlockSpec((None, tr, PACK_W), lambda i, k: (k[0], i, 0)),
                  pl.BlockSpec((3, tr, PACK_W), lambda i, k: (0, i, 0))],
        out_specs=pl.BlockSpec((tr, PACK_W), lambda i, k: (i, 0)),
    )
    return _call(
        body, name="grad_shard_sum", grid_spec=grid_spec,
        out_shape=jax.ShapeDtypeStruct((h, PACK_W), F32),
        compiler_params=_cp(),
    )(k_arr, cs, got)


def _join_halves(mine):
    h = mine.shape[0]

    def body(m_ref, o_ref, send_sem, recv_sem, local_sem):
        x, y, c = _place()
        own = pltpu.make_async_copy(m_ref, o_ref.at[pl.ds(c * h, h), :], local_sem)
        own.start()
        cp = pltpu.make_async_remote_copy(
            src_ref=m_ref, dst_ref=o_ref.at[pl.ds(c * h, h), :],
            send_sem=send_sem, recv_sem=recv_sem, device_id=(x, y, 1 - c), device_id_type=MESH)
        cp.start()
        cp.wait_send()
        pltpu.make_async_remote_copy(
            src_ref=m_ref, dst_ref=o_ref.at[pl.ds((1 - c) * h, h), :],
            send_sem=send_sem, recv_sem=recv_sem, device_id=(x, y, 1 - c), device_id_type=MESH).wait_recv()
        own.wait()

    return _call(
        body, name="grad_join_halves",
        in_specs=[pl.BlockSpec(memory_space=pl.ANY)],
        out_specs=pl.BlockSpec(memory_space=pl.ANY),
        out_shape=jax.ShapeDtypeStruct((2 * h, PACK_W), mine.dtype),
        scratch_shapes=[pltpu.SemaphoreType.DMA, pltpu.SemaphoreType.DMA, pltpu.SemaphoreType.DMA],
    )(mine)


def _all_reduce_small(v):
    r = v.shape[0]

    def body(v_ref, o_ref, buf, send_sems, recv_sems):
        x, y, c = _place()
        me = 4 * x + 2 * y + c
        buf[me] = v_ref[...]
        cps = []
        for k in range(1, 8):
            peer = (x ^ (k >> 2), y ^ ((k >> 1) & 1), c ^ (k & 1))
            cps.append(pltpu.make_async_remote_copy(
                src_ref=v_ref, dst_ref=buf.at[me],
                send_sem=send_sems.at[k - 1], recv_sem=recv_sems.at[k - 1], device_id=peer, device_id_type=MESH))
        for cp in cps:
            cp.start()
        for k in range(1, 8):
            pltpu.make_async_remote_copy(
                src_ref=v_ref, dst_ref=buf.at[me ^ k],
                send_sem=send_sems.at[k - 1], recv_sem=recv_sems.at[k - 1],
                device_id=(x, y, c), device_id_type=MESH).wait_recv()
        for cp in cps:
            cp.wait_send()
        acc = buf[0]
        for k in range(1, 8):
            acc = acc + buf[k]
        o_ref[...] = acc

    return _call(
        body, name="all_reduce_small",
        in_specs=[pl.BlockSpec(memory_space=pltpu.VMEM)],
        out_specs=pl.BlockSpec(memory_space=pltpu.VMEM),
        out_shape=jax.ShapeDtypeStruct((r, 128), F32),
        scratch_shapes=[pltpu.VMEM((8, r, 128), F32), pltpu.SemaphoreType.DMA((7,)), pltpu.SemaphoreType.DMA((7,))],
    )(v)


def _pack(shards, dtype):
    parts = [s.astype(dtype).reshape(-1, PACK_W) for s in shards]
    rows = sum(p.shape[0] for p in parts)
    pad = -rows % PACK_ALIGN
    if pad:
        parts.append(jnp.zeros((pad, PACK_W), dtype))
    return jnp.concatenate(parts, axis=0)


def _unpack_full(g):
    out, at = {}, 0
    for name, rows, cols, axis in BIG:
        n = rows * cols // 4 // PACK_W
        blk = g[:, at:at + n, :]
        at += n
        if axis == 1:
            out[name] = blk.reshape(4, rows, cols // 4).transpose(1, 0, 2).reshape(rows, cols)
        else:
            out[name] = blk.reshape(rows, cols)
    return out


def _pack_grads(grads):
    parts = []
    for name, rows, cols, axis in BIG:
        g = grads[name]
        if axis == 1:
            g = g.reshape(rows, 4, cols // 4).transpose(1, 0, 2)
        parts.append(g.reshape(4, -1, PACK_W))
    rows_total = sum(p.shape[1] for p in parts)
    pad = -rows_total % PACK_ALIGN
    if pad:
        parts.append(jnp.zeros((4, pad, PACK_W), F32))
    return jnp.concatenate(parts, axis=1)


def _unpack_shard(s):
    out, at = {}, 0
    for name, rows, cols, axis in BIG:
        n = rows * cols // 4 // PACK_W
        shape = (rows, cols // 4) if axis == 1 else (rows // 4, cols)
        out[name] = s[at:at + n, :].reshape(shape)
        at += n
    return out


def _pack_small(parts):
    flat = jnp.concatenate([p.reshape(-1) for p in parts])
    pad = -flat.shape[0] % 1024
    return jnp.concatenate([flat, jnp.zeros((pad,), F32)]).reshape(-1, 128)


def _ffn_fwd(tag, h, gain, w_in, w_out):
    t = h.shape[0]
    tm = min(TM, t)
    n = _rms_fwd(tag + "_norm", h, gain)
    tn = 256

    def epi(accs, _):
        gate, up = accs
        return gate, up, _silu(gate) * up

    gate, up, act = _mm(tag + "_in", [_a_spec(n, tm)], [_b_nn(w_in, tn), _b_nn(w_in, tn, DFF // tn)], [(0, 0), (0, 1)],
                        epi, [], [BF16, BF16, BF16], t, DFF, tm, tn)
    (out,) = _mm(tag + "_out", [_a_spec(act, tm)], [_b_nn(w_out, 512)], [(0, 0)],
                 lambda accs, ex: (ex[0] + 0.5 * accs[0],), [_e_tile(h, tm, 512)], [F32], t, D, tm, 512)
    return out, (n, gate, up, act)


def _ffn_bwd(tag, h, gain, w_in, w_out, saved, dout):
    t = h.shape[0]
    tm = min(TM, t)
    n, gate, up, act = saved
    tn = 256

    def epi(accs, ex):
        da = 0.5 * accs[0]
        g, u = ex[0].astype(F32), ex[1].astype(F32)
        return da * u * _dsilu(g), da * _silu(g)

    dgate, dup = _mm(tag + "_dact", [_a_spec(dout, tm)], [_b_nt(w_out, tn)], [(0, 0)], epi,
                     [_e_tile(gate, tm, tn), _e_tile(up, tm, tn)], [BF16, BF16], t, DFF, tm, tn, trans_b=True)
    (dn,) = _mm(tag + "_dn", [_a_spec(dgate, tm), _a_spec(dup, tm)],
                [_b_nt(w_in, 512, DFF, 0), _b_nt(w_in, 512, DFF, 1)], [(0, 0), (1, 1)],
                lambda accs, ex: (accs[0] + accs[1],), [], [F32], t, D, tm, 512, trans_b=True)
    dh, dgain = _rms_bwd(tag + "_dnorm", h, gain, dn, dout)
    dw_out = _mm_tn(tag + "_dw_out", act, dout, scale=0.5, tm=DFF // 2, tn=D)
    dw_g = _mm_tn(tag + "_dw_gate", n, dgate, tm=D, tn=DFF // 2)
    dw_u = _mm_tn(tag + "_dw_up", n, dup, tm=D, tn=DFF // 2)
    return dh, dgain, jnp.concatenate([dw_g, dw_u], axis=1), dw_out


def kernel(x, positions, ffn1_norm, ffn1_w_in, ffn1_w_out, mix_norm, w_in, hg_lb_table, hg_out_norm, w_hg_branch, mla_q_lora_norm, w_q_up, mla_kv_lora_norm, w_kv_up, q_head_norm, k_head_norm, w_mla_branch, w_merge, b_merge, w_out, ffn2_norm, ffn2_w_in, ffn2_w_out, final_norm, loss_target, m_ffn1_norm, m_ffn1_w_in, m_ffn1_w_out, m_mix_norm, m_w_in, m_hg_lb_table, m_hg_out_norm, m_w_hg_branch, m_mla_q_lora_norm, m_w_q_up, m_mla_kv_lora_norm, m_w_kv_up, m_q_head_norm, m_k_head_norm, m_w_mla_branch, m_w_merge, m_b_merge, m_w_out, m_ffn2_norm, m_ffn2_w_in, m_ffn2_w_out, m_final_norm, v_ffn1_norm, v_ffn1_w_in, v_ffn1_w_out, v_mix_norm, v_w_in, v_hg_lb_table, v_hg_out_norm, v_w_hg_branch, v_mla_q_lora_norm, v_w_q_up, v_mla_kv_lora_norm, v_w_kv_up, v_q_head_norm, v_k_head_norm, v_w_mla_branch, v_w_merge, v_b_merge, v_w_out, v_ffn2_norm, v_ffn2_w_in, v_ffn2_w_out, v_final_norm):
    a = dict(locals())
    w = {n: a[n] for n in WEIGHT_ORDER}
    mom = {n: a["m_" + n] for n in WEIGHT_ORDER}
    var = {n: a["v_" + n] for n in WEIGHT_ORDER}
    t = x.shape[1]
    tm = min(TM, t)
    xt = x.reshape(t, D)
    target = loss_target.reshape(t, D)
    pos = positions.reshape(t, 1)
    x_i, y_i, c_i = _place()
    c_arr = c_i.astype(jnp.int32).reshape(1)
    k_arr = (2 * x_i + y_i).astype(jnp.int32).reshape(1)

    shard = _pack([w[n][0] for n, _, _, _ in BIG], BF16)
    full = _unpack_full(_gather_weights(shard))
    w_in_full = full["w_in"]
    w_in_hg = w_in_full[:, :4 * D]
    w_in_mla = jnp.pad(w_in_full[:, 4 * D:], ((0, 0), (0, MLA_COLS - (4800 - 4 * D))))
    w_q_pad = jnp.pad(full["w_q_up"].reshape(Q_LORA, HEADS, QK), ((0, 0), (0, 0), (0, QKP - QK))).reshape(Q_LORA, HEADS * QKP)
    w_kv = full["w_kv_up"]
    gq = jnp.pad(w["q_head_norm"], ((0, 0), (0, QKP - QK)))
    gk = jnp.pad(w["k_head_norm"], ((0, 0), (0, QKP - QK)))

    h1, ffn1_saved = _ffn_fwd("ffn1", xt, w["ffn1_norm"], full["ffn1_w_in"], full["ffn1_w_out"])
    u = _rms_fwd("mix_norm", h1, w["mix_norm"])
    ident = lambda accs, ex: (accs[0],)
    (p_hg,) = _mm("in_hg", [_a_spec(u, tm)], [_b_nn(w_in_hg, 512)], [(0, 0)], ident, [], [F32], t, 4 * D, tm, 512)
    (p_mla,) = _mm("in_mla", [_a_spec(u, tm)], [_b_nn(w_in_mla, MLA_COLS)], [(0, 0)], ident, [], [F32], t, MLA_COLS, tm, MLA_COLS)
    o_raw, hg_o, states = _hgrn_fwd(p_hg, w["hg_lb_table"], w["hg_out_norm"])
    (y_hg,) = _mm("hg_branch", [_a_spec(hg_o, tm)], [_b_nn(full["w_hg_branch"], 512)], [(0, 0)], ident, [], [BF16], t, D, tm, 512)
    cqn, ckvn = _lora_norm_fwd(p_mla, w["mla_q_lora_norm"], w["mla_kv_lora_norm"])
    (qf,) = _mm("q_up", [_a_spec(cqn, tm)], [_b_nn(w_q_pad, 512)], [(0, 0)], ident, [], [F32], t, HEADS * QKP, tm, 512)
    (kvf,) = _mm("kv_up", [_a_spec(ckvn, tm)], [_b_nn(w_kv, 512)], [(0, 0)], ident, [], [F32], t, HEADS * QKP, tm, 512)
    cos, sin = _rope_tables(pos)
    qh, kh, vh = _mla_prep_fwd(qf, kvf, p_mla, cos, sin, gq, gk)
    o_mla, lse = _flash_fwd(qh, kh, vh)
    (y_mla,) = _mm("mla_branch", [_a_spec(o_mla, tm)], [_b_nn(full["w_mla_branch"], 512)], [(0, 0)], ident, [], [BF16], t, D, tm, 512)

    def merge_epi(accs, ex):
        g_hg = _sig(accs[0] + ex[2])
        g_mla = _sig(accs[1] + ex[3])
        return g_hg * ex[0].astype(F32) + g_mla * ex[1].astype(F32), g_hg, g_mla

    w_merge_f = full["w_merge"]
    mix, g_hg, g_mla = _mm(
        "merge", [_a_spec(u, tm)], [_b_nn(w_merge_f, 512), _b_nn(w_merge_f, 512, D // 512)], [(0, 0), (0, 1)], merge_epi,
        [_e_tile(y_hg, tm, 512), _e_tile(y_mla, tm, 512), _e_row(w["b_merge"], 512), _e_row(w["b_merge"], 512, D // 512)],
        [BF16, BF16, BF16], t, D, tm, 512)
    (h2,) = _mm("out_proj", [_a_spec(mix, tm)], [_b_nn(full["w_out"], 512)], [(0, 0)],
                lambda accs, ex: (ex[0] + accs[0],), [_e_tile(h1, tm, 512)], [F32], t, D, tm, 512)
    h3, ffn2_saved = _ffn_fwd("ffn2", h2, w["ffn2_norm"], full["ffn2_w_in"], full["ffn2_w_out"])
    dh3, d_final_norm, loss_part = _final_loss(h3, target, w["final_norm"])

    grads, small = {}, {}
    small["final_norm"] = d_final_norm
    dh2, small["ffn2_norm"], grads["ffn2_w_in"], grads["ffn2_w_out"] = _ffn_bwd(
        "ffn2", h2, w["ffn2_norm"], full["ffn2_w_in"], full["ffn2_w_out"], ffn2_saved, dh3)

    def dmix_epi(accs, ex):
        dm = accs[0]
        ghg, gml, yhg, yml = [e.astype(F32) for e in ex]
        return dm * ghg, dm * gml, dm * yhg * ghg * (1.0 - ghg), dm * yml * gml * (1.0 - gml)

    dy_hg, dy_mla, dpre_hg, dpre_mla = _mm(
        "d_mix", [_a_spec(dh2, tm)], [_b_nt(full["w_out"], 512)], [(0, 0)], dmix_epi,
        [_e_tile(g_hg, tm, 512), _e_tile(g_mla, tm, 512), _e_tile(y_hg, tm, 512), _e_tile(y_mla, tm, 512)],
        [BF16, BF16, BF16, BF16], t, D, tm, 512, trans_b=True)
    grads["w_out"] = _mm_tn("dw_out", mix, dh2)
    small["b_merge"] = jnp.concatenate([_colsum("db_hg", dpre_hg), _colsum("db_mla", dpre_mla)], axis=1)
    grads["w_merge"] = jnp.concatenate([_mm_tn("dw_merge_hg", u, dpre_hg), _mm_tn("dw_merge_mla", u, dpre_mla)], axis=1)
    grads["w_hg_branch"] = _mm_tn("dw_hg_branch", hg_o, dy_hg)
    grads["w_mla_branch"] = _mm_tn("dw_mla_branch", o_mla, dy_mla)
    (dho,) = _mm("d_hg_o", [_a_spec(dy_hg, tm)], [_b_nt(full["w_hg_branch"], 512)], [(0, 0)], ident, [], [BF16], t, D, tm, 512, trans_b=True)
    (do_mla,) = _mm("d_o_mla", [_a_spec(dy_mla, tm)], [_b_nt(full["w_mla_branch"], 512)], [(0, 0)], ident, [], [BF16], t, D, tm, 512, trans_b=True)

    dq_raw, df_raw, di_raw, dg_raw, small["hg_lb_table"], small["hg_out_norm"] = _hgrn_bwd(
        p_hg, w["hg_lb_table"], w["hg_out_norm"], o_raw, states, dho)
    dp_hg = [dq_raw, df_raw, di_raw, dg_raw]

    dqh, dkh, dvh = _flash_bwd(qh, kh, vh, o_mla, lse, do_mla)
    dqf, dkvf, dkpe, dgq, dgk = _mla_prep_bwd(qf, kvf, p_mla, cos, sin, gq, gk, dqh, dkh, dvh)
    small["q_head_norm"] = dgq[:, :QK]
    small["k_head_norm"] = dgk[:, :QK]
    dwq_pad = _mm_tn("dw_q_up", cqn, dqf, tm=Q_LORA, tn=1024)
    grads["w_q_up"] = dwq_pad.reshape(Q_LORA, HEADS, QKP)[:, :, :QK].reshape(Q_LORA, HEADS * QK)
    grads["w_kv_up"] = _mm_tn("dw_kv_up", ckvn, dkvf, tm=KV_LORA, tn=1024)
    (dcqn,) = _mm("d_cq", [_a_spec(dqf, tm)], [_b_nt(w_q_pad, Q_LORA)], [(0, 0)], ident, [], [F32], t, Q_LORA, tm, Q_LORA, trans_b=True)
    (dckvn,) = _mm("d_ckv", [_a_spec(dkvf, tm)], [_b_nt(w_kv, KV_LORA)], [(0, 0)], ident, [], [F32], t, KV_LORA, tm, KV_LORA, trans_b=True)
    dp_mla, small["mla_q_lora_norm"], small["mla_kv_lora_norm"] = _lora_norm_bwd(
        p_mla, w["mla_q_lora_norm"], w["mla_kv_lora_norm"], dcqn, dckvn, dkpe)

    dw_in_hg = [_mm_tn("dw_in_hg%d" % k, u, dp_hg[k]) for k in range(4)]
    dw_in_mla = _mm_tn("dw_in_mla", u, dp_mla, tn=MLA_COLS)
    grads["w_in"] = jnp.concatenate(dw_in_hg + [dw_in_mla[:, :4800 - 4 * D]], axis=1)
    (du,) = _mm(
        "d_u",
        [_a_spec(dpre_hg, tm), _a_spec(dpre_mla, tm)] + [_a_spec(d, tm) for d in dp_hg] + [_a_spec(dp_mla, tm)],
        [_b_nt(w_merge_f, 512, D, 0), _b_nt(w_merge_f, 512, D, 1)]
        + [_b_nt(w_in_hg, 512, D, k) for k in range(4)] + [_b_nt(w_in_mla, 512)],
        [(k, k) for k in range(7)],
        lambda accs, ex: (functools.reduce(lambda p, q: p + q, accs),), [], [F32], t, D, tm, 512, trans_b=True)
    dh1, small["mix_norm"] = _rms_bwd("mix_dnorm", h1, w["mix_norm"], du, dh2)
    dx, small["ffn1_norm"], grads["ffn1_w_in"], grads["ffn1_w_out"] = _ffn_bwd(
        "ffn1", xt, w["ffn1_norm"], full["ffn1_w_in"], full["ffn1_w_out"], ffn1_saved, dh1)

    gp = _pack_grads(grads)
    chip_sums = _chip_sum(gp, _swap_halves(gp), c_arr)
    my_half = _shard_sum(chip_sums, _scatter_chip_sums(chip_sums), k_arr)
    g_shard = _unpack_shard(_join_halves(my_half))
    small_sum = _all_reduce_small(_pack_small([small[n] for n, _ in SMALL] + [loss_part])).reshape(-1)
    g_small, at = {}, 0
    for n, shape in SMALL:
        size = shape[0] * shape[1]
        g_small[n] = small_sum[at:at + size].reshape(shape)
        at += size
    loss = small_sum[at]

    g_out, d_out, m_out, v_out = {}, {}, {}, {}
    for n in WEIGHT_ORDER:
        shape = w[n].shape
        g = g_shard[n] if n in g_shard else g_small[n]
        two = g.shape
        d_, m_, v_ = _adamw("adamw_" + n, w[n].reshape(two), g, mom[n].reshape(two), var[n].reshape(two))
        g_out[n], d_out[n], m_out[n], v_out[n] = g.reshape(shape), d_.reshape(shape), m_.reshape(shape), v_.reshape(shape)

    return (loss, dx.reshape(x.shape), *[g_out[n] for n in WEIGHT_ORDER], *[d_out[n] for n in WEIGHT_ORDER],
            *[m_out[n] for n in WEIGHT_ORDER], *[v_out[n] for n in WEIGHT_ORDER])
```

```python
import functools

import numpy as np
import jax
import jax.numpy as jnp
from jax import lax
from jax.experimental import pallas as pl
from jax.experimental.pallas import tpu as pltpu

F32 = jnp.float32
BF16 = jnp.bfloat16
MESH = pl.DeviceIdType.MESH

D = 1024
DFF = 2816
HEADS = 8
HK = 128
CHUNK = 64
ROPE = 64
QK = 192
QKP = 256
Q_LORA = 384
KV_LORA = 256
MLA_COLS = 768
EPS = 1e-6
ROPE_THETA = 10000.0
SCALE = QK ** -0.5
LOG2E = 1.4426950408889634
LN2 = 0.6931471805599453
NEG = -1e30
EXP_CLAMP = 80.0

ADAM_LR = 0.001
ADAM_B1 = 0.9
ADAM_B2 = 0.999
ADAM_EPS = 1e-08
ADAM_WD = 0.01
ADAM_STEP = 10

PACK_W = 1024
PACK_ALIGN = 1024

TM = 512
TQ = 512
SUBQ = 512
HG_BT = 512
HG_HPB = 4
TT = 512
ROW_TM = 256

VMEM_MB = 48

BIG = (
    ("ffn1_w_in", D, 2 * DFF, 1),
    ("ffn1_w_out", DFF, D, 0),
    ("w_in", D, 4800, 1),
    ("w_hg_branch", D, D, 0),
    ("w_q_up", Q_LORA, HEADS * QK, 1),
    ("w_kv_up", KV_LORA, HEADS * 2 * HK, 1),
    ("w_mla_branch", D, D, 0),
    ("w_merge", D, 2 * D, 1),
    ("w_out", D, D, 0),
    ("ffn2_w_in", D, 2 * DFF, 1),
    ("ffn2_w_out", DFF, D, 0),
)
SMALL = (
    ("ffn1_norm", (1, D)),
    ("mix_norm", (1, D)),
    ("hg_lb_table", (2, D)),
    ("hg_out_norm", (1, HK)),
    ("mla_q_lora_norm", (1, Q_LORA)),
    ("mla_kv_lora_norm", (1, KV_LORA)),
    ("q_head_norm", (1, QK)),
    ("k_head_norm", (1, QK)),
    ("b_merge", (1, 2 * D)),
    ("ffn2_norm", (1, D)),
    ("final_norm", (1, D)),
)
WEIGHT_ORDER = ("ffn1_norm", "ffn1_w_in", "ffn1_w_out", "mix_norm", "w_in", "hg_lb_table", "hg_out_norm",
                "w_hg_branch", "mla_q_lora_norm", "w_q_up", "mla_kv_lora_norm", "w_kv_up", "q_head_norm",
                "k_head_norm", "w_mla_branch", "w_merge", "b_merge", "w_out", "ffn2_norm", "ffn2_w_in",
                "ffn2_w_out", "final_norm")


def _call(body, **kw):
    return pl.pallas_call(body, **kw)


def _cp(vmem_mb=VMEM_MB):
    return pltpu.CompilerParams(vmem_limit_bytes=vmem_mb << 20)


def _dot(a, b):
    return lax.dot_general(a, b, (((1,), (0,)), ((), ())), preferred_element_type=F32)


def _dot_nt(a, b):
    return lax.dot_general(a, b, (((1,), (1,)), ((), ())), preferred_element_type=F32)


def _dot_tn(a, b):
    return lax.dot_general(a, b, (((0,), (0,)), ((), ())), preferred_element_type=F32)


def _sig(x):
    return jax.nn.sigmoid(x)


def _silu(x):
    return x * _sig(x)


def _dsilu(x):
    s = _sig(x)
    return s * (1.0 + x * (1.0 - s))


def _a_spec(arr, tm, kblk=None, kidx=0):
    kb = arr.shape[1] if kblk is None else kblk
    return arr, pl.BlockSpec((tm, kb), lambda i, j, kidx=kidx: (i, kidx))


def _b_nn(arr, tn, off=0):
    return arr, pl.BlockSpec((arr.shape[0], tn), lambda i, j, off=off: (0, j + off))


def _b_nt(arr, tn, kblk=None, kidx=0):
    kb = arr.shape[1] if kblk is None else kblk
    return arr, pl.BlockSpec((tn, kb), lambda i, j, kidx=kidx: (j, kidx))


def _e_tile(arr, tm, tn, off=0):
    return arr, pl.BlockSpec((tm, tn), lambda i, j, off=off: (i, j + off))


def _e_row(arr, tn, off=0):
    return arr, pl.BlockSpec((1, tn), lambda i, j, off=off: (0, j + off))


def _mm(name, As, Bs, dots, epi, extras, out_dtypes, m, n, tm, tn, trans_b=False):
    na, nb, ne = len(As), len(Bs), len(extras)

    def body(*refs):
        a_refs = refs[:na]
        b_refs = refs[na:na + nb]
        e_refs = refs[na + nb:na + nb + ne]
        o_refs = refs[na + nb + ne:]
        a_vals = [r[...].astype(BF16) for r in a_refs]
        accs = []
        for ai, bi in dots:
            b = b_refs[bi][...]
            accs.append(_dot_nt(a_vals[ai], b) if trans_b else _dot(a_vals[ai], b))
        outs = epi(accs, [r[...] for r in e_refs])
        for o_ref, o in zip(o_refs, outs):
            o_ref[...] = o.astype(o_ref.dtype)

    ops = list(As) + list(Bs) + list(extras)
    res = _call(
        body, name=name,
        grid=(m // tm, n // tn),
        in_specs=[s for _, s in ops],
        out_specs=[pl.BlockSpec((tm, tn), lambda i, j: (i, j)) for _ in out_dtypes],
        out_shape=[jax.ShapeDtypeStruct((m, n), dt) for dt in out_dtypes],
        compiler_params=_cp(),
    )(*[a for a, _ in ops])
    return res


def _mm_tn(name, a, b, scale=1.0, tm=1024, tn=1024):
    t, m = a.shape
    n = b.shape[1]
    tm, tn, tt = min(tm, m), min(tn, n), min(TT, t)
    nk = t // tt

    def body(a_ref, b_ref, o_ref):
        k = pl.program_id(2)

        @pl.when(k == 0)
        def _():
            o_ref[...] = jnp.zeros_like(o_ref)

        o_ref[...] += _dot_tn(a_ref[...].astype(BF16), b_ref[...].astype(BF16))
        if scale != 1.0:
            @pl.when(k == nk - 1)
            def _():
                o_ref[...] = o_ref[...] * scale

    return _call(
        body, name=name,
        grid=(m // tm, n // tn, nk),
        in_specs=[pl.BlockSpec((tt, tm), lambda i, j, k: (k, i)), pl.BlockSpec((tt, tn), lambda i, j, k: (k, j))],
        out_specs=pl.BlockSpec((tm, tn), lambda i, j, k: (i, j)),
        out_shape=jax.ShapeDtypeStruct((m, n), F32),
        compiler_params=_cp(),
    )(a, b)


def _rms_fwd(name, x, gain):
    t, d = x.shape
    tm = min(ROW_TM, t)

    def body(x_ref, g_ref, o_ref):
        xv = x_ref[...]
        r = lax.rsqrt(jnp.mean(xv * xv, axis=-1, keepdims=True) + EPS)
        o_ref[...] = (xv * r * g_ref[...]).astype(o_ref.dtype)

    return _call(
        body, name=name, grid=(t // tm,),
        in_specs=[pl.BlockSpec((tm, d), lambda i: (i, 0)), pl.BlockSpec((1, d), lambda i: (0, 0))],
        out_specs=pl.BlockSpec((tm, d), lambda i: (i, 0)),
        out_shape=jax.ShapeDtypeStruct((t, d), BF16),
        compiler_params=_cp(),
    )(x, gain)


def _rms_bwd_vals(xv, g, dn):
    r = lax.rsqrt(jnp.mean(xv * xv, axis=-1, keepdims=True) + EPS)
    xh = xv * r
    dxh = dn * g
    c = jnp.mean(dxh * xh, axis=-1, keepdims=True)
    return r * (dxh - xh * c), dn * xh


def _rms_bwd(name, x, gain, dn, dres):
    t, d = x.shape
    tm = min(ROW_TM, t)

    def body(x_ref, g_ref, dn_ref, dr_ref, dx_ref, dg_ref):
        @pl.when(pl.program_id(0) == 0)
        def _():
            dg_ref[...] = jnp.zeros_like(dg_ref)

        dx, dg = _rms_bwd_vals(x_ref[...], g_ref[...], dn_ref[...].astype(F32))
        dx_ref[...] = dr_ref[...] + dx
        dg_ref[...] += jnp.sum(dg, axis=0, keepdims=True)

    row = pl.BlockSpec((tm, d), lambda i: (i, 0))
    one = pl.BlockSpec((1, d), lambda i: (0, 0))
    return _call(
        body, name=name, grid=(t // tm,),
        in_specs=[row, one, row, row],
        out_specs=[row, one],
        out_shape=[jax.ShapeDtypeStruct((t, d), F32), jax.ShapeDtypeStruct((1, d), F32)],
        compiler_params=_cp(),
    )(x, gain, dn, dres)


def _final_loss(h, target, gain):
    t, d = h.shape
    tm = min(ROW_TM, t)

    def body(h_ref, t_ref, g_ref, dh_ref, dg_ref, l_ref):
        @pl.when(pl.program_id(0) == 0)
        def _():
            dg_ref[...] = jnp.zeros_like(dg_ref)
            l_ref[...] = jnp.zeros_like(l_ref)

        hv = h_ref[...]
        g = g_ref[...]
        r = lax.rsqrt(jnp.mean(hv * hv, axis=-1, keepdims=True) + EPS)
        xh = hv * r
        err = xh * g - t_ref[...]
        l_ref[...] += 0.5 * jnp.sum(jnp.mean(err * err, axis=-1, keepdims=True), axis=0, keepdims=True)
        dy = err * (1.0 / d)
        dxh = dy * g
        c = jnp.mean(dxh * xh, axis=-1, keepdims=True)
        dh_ref[...] = r * (dxh - xh * c)
        dg_ref[...] += jnp.sum(dy * xh, axis=0, keepdims=True)

    row = pl.BlockSpec((tm, d), lambda i: (i, 0))
    one = pl.BlockSpec((1, d), lambda i: (0, 0))
    return _call(
        body, name="final_loss", grid=(t // tm,),
        in_specs=[row, row, one],
        out_specs=[row, one, pl.BlockSpec((1, 128), lambda i: (0, 0))],
        out_shape=[jax.ShapeDtypeStruct((t, d), F32), jax.ShapeDtypeStruct((1, d), F32),
                   jax.ShapeDtypeStruct((1, 128), F32)],
        compiler_params=_cp(),
    )(h, target, gain)


def _colsum(name, x):
    t, n = x.shape
    tm = min(TM, t)

    def body(x_ref, o_ref):
        @pl.when(pl.program_id(0) == 0)
        def _():
            o_ref[...] = jnp.zeros_like(o_ref)

        o_ref[...] += jnp.sum(x_ref[...].astype(F32), axis=0, keepdims=True)

    return _call(
        body, name=name, grid=(t // tm,),
        in_specs=[pl.BlockSpec((tm, n), lambda i: (i, 0))],
        out_specs=pl.BlockSpec((1, n), lambda i: (0, 0)),
        out_shape=jax.ShapeDtypeStruct((1, n), F32),
        compiler_params=_cp(),
    )(x)


def _lora_norm_fwd(p_mla, gq, gkv):
    t = p_mla.shape[0]
    tm = min(ROW_TM, t)

    def body(p_ref, gq_ref, gkv_ref, q_ref, kv_ref):
        cq = p_ref[:, 0:Q_LORA]
        ckv = p_ref[:, Q_LORA:Q_LORA + KV_LORA]
        rq = lax.rsqrt(jnp.mean(cq * cq, axis=-1, keepdims=True) + EPS)
        rkv = lax.rsqrt(jnp.mean(ckv * ckv, axis=-1, keepdims=True) + EPS)
        q_ref[...] = (cq * rq * gq_ref[...]).astype(BF16)
        kv_ref[...] = (ckv * rkv * gkv_ref[...]).astype(BF16)

    return _call(
        body, name="lora_norm_fwd", grid=(t // tm,),
        in_specs=[pl.BlockSpec((tm, MLA_COLS), lambda i: (i, 0)),
                  pl.BlockSpec((1, Q_LORA), lambda i: (0, 0)), pl.BlockSpec((1, KV_LORA), lambda i: (0, 0))],
        out_specs=[pl.BlockSpec((tm, Q_LORA), lambda i: (i, 0)), pl.BlockSpec((tm, KV_LORA), lambda i: (i, 0))],
        out_shape=[jax.ShapeDtypeStruct((t, Q_LORA), BF16), jax.ShapeDtypeStruct((t, KV_LORA), BF16)],
        compiler_params=_cp(),
    )(p_mla, gq, gkv)


def _lora_norm_bwd(p_mla, gq, gkv, dcqn, dckvn, dkpe):
    t = p_mla.shape[0]
    tm = min(ROW_TM, t)

    def body(p_ref, gq_ref, gkv_ref, dq_ref, dkv_ref, dkpe_ref, dp_ref, dgq_ref, dgkv_ref):
        @pl.when(pl.program_id(0) == 0)
        def _():
            dgq_ref[...] = jnp.zeros_like(dgq_ref)
            dgkv_ref[...] = jnp.zeros_like(dgkv_ref)

        dcq, dgq = _rms_bwd_vals(p_ref[:, 0:Q_LORA], gq_ref[...], dq_ref[...])
        dckv, dgkv = _rms_bwd_vals(p_ref[:, Q_LORA:Q_LORA + KV_LORA], gkv_ref[...], dkv_ref[...])
        dp_ref[:, 0:Q_LORA] = dcq.astype(BF16)
        dp_ref[:, Q_LORA:Q_LORA + KV_LORA] = dckv.astype(BF16)
        dp_ref[:, Q_LORA + KV_LORA:MLA_COLS] = dkpe_ref[...].astype(BF16)
        dgq_ref[...] += jnp.sum(dgq, axis=0, keepdims=True)
        dgkv_ref[...] += jnp.sum(dgkv, axis=0, keepdims=True)

    return _call(
        body, name="lora_norm_bwd", grid=(t // tm,),
        in_specs=[pl.BlockSpec((tm, MLA_COLS), lambda i: (i, 0)),
                  pl.BlockSpec((1, Q_LORA), lambda i: (0, 0)), pl.BlockSpec((1, KV_LORA), lambda i: (0, 0)),
                  pl.BlockSpec((tm, Q_LORA), lambda i: (i, 0)), pl.BlockSpec((tm, KV_LORA), lambda i: (i, 0)),
                  pl.BlockSpec((tm, HK), lambda i: (i, 0))],
        out_specs=[pl.BlockSpec((tm, MLA_COLS), lambda i: (i, 0)),
                   pl.BlockSpec((1, Q_LORA), lambda i: (0, 0)), pl.BlockSpec((1, KV_LORA), lambda i: (0, 0))],
        out_shape=[jax.ShapeDtypeStruct((t, MLA_COLS), BF16), jax.ShapeDtypeStruct((1, Q_LORA), F32),
                   jax.ShapeDtypeStruct((1, KV_LORA), F32)],
        compiler_params=_cp(),
    )(p_mla, gq, gkv, dcqn, dckvn, dkpe)


def _cumsum_rows(x, row):
    for s in (1, 2, 4, 8, 16, 32):
        x = x + jnp.where(row >= s, pltpu.roll(x, s, 0), 0.0)
    return x


def _rcumsum_rows(x, row):
    for s in (1, 2, 4, 8, 16, 32):
        x = x + jnp.where(row < CHUNK - s, pltpu.roll(x, CHUNK - s, 0), 0.0)
    return x


def _hg_gates(qr, z, lb, row):
    q = _silu(qr)
    sg = _sig(z)
    f = lb + (1.0 - lb) * sg
    lf = jnp.log(f)
    k = (1.0 - lb) * (1.0 - sg)
    cum = _cumsum_rows(lf, row)
    mid = jnp.sum(jnp.where(row < CHUNK // 2, lf, 0.0), axis=0, keepdims=True)
    last = jnp.sum(lf, axis=0, keepdims=True)
    e_q = jnp.exp(jnp.minimum(cum - mid, EXP_CLAMP))
    e_k = jnp.exp(jnp.minimum(mid - cum, EXP_CLAMP))
    e_a = jnp.exp(cum)
    e_l = jnp.exp(last - cum)
    return q, sg, f, k, last, e_q, e_k, e_a, e_l


def _hgrn_fwd(p_hg, tab, gain):
    t = p_hg.shape[0]
    bt = min(HG_BT, t)
    nb, nc = t // bt, bt // CHUNK

    hpb = HG_HPB
    wide = hpb * HK

    def body(q_ref, f_ref, i_ref, g_ref, tab_ref, gain_ref, o_ref, ho_ref, st_ref, state):
        @pl.when(pl.program_id(1) == 0)
        def _():
            state[...] = jnp.zeros_like(state)

        row = lax.broadcasted_iota(jnp.int32, (CHUNK, HK), 0)
        tril = lax.broadcasted_iota(jnp.int32, (CHUNK, CHUNK), 0) >= lax.broadcasted_iota(jnp.int32, (CHUNK, CHUNK), 1)
        gain_v = gain_ref[...]

        def chunk(c, carry):
            sl = pl.ds(pl.multiple_of(c * CHUNK, CHUNK), CHUNK)
            for hh in range(hpb):
                ln = slice(hh * HK, (hh + 1) * HK)
                lb = _sig(tab_ref[0:1, ln] - tab_ref[1:2, ln])
                v = i_ref[sl, ln].astype(BF16)
                q, _, _, k, last, e_q, e_k, e_a, e_l = _hg_gates(q_ref[sl, ln], f_ref[sl, ln], lb, row)
                st = state[hh]
                st_ref[hh, c] = st
                p = jnp.where(tril, _dot_nt((q * e_q).astype(BF16), (k * e_k).astype(BF16)), 0.0)
                o = _dot(p.astype(BF16), v) + _dot_nt((q * e_a).astype(BF16), st.astype(BF16))
                state[hh] = jnp.exp(last) * st + _dot_tn(v, (k * e_l).astype(BF16))
                o_ref[sl, ln] = o
                r = lax.rsqrt(jnp.mean(o * o, axis=-1, keepdims=True) + EPS)
                ho_ref[sl, ln] = (o * r * gain_v * _silu(g_ref[sl, ln])).astype(BF16)
            return carry

        lax.fori_loop(0, nc, chunk, 0)

    def col(k):
        return pl.BlockSpec((bt, wide), lambda h, j, k=k: (j, k * (HEADS // hpb) + h))

    return _call(
        body, name="hgrn_fwd", grid=(HEADS // hpb, nb),
        in_specs=[col(0), col(1), col(2), col(3),
                  pl.BlockSpec((2, wide), lambda h, j: (0, h)), pl.BlockSpec((1, HK), lambda h, j: (0, 0))],
        out_specs=[pl.BlockSpec((bt, wide), lambda h, j: (j, h)), pl.BlockSpec((bt, wide), lambda h, j: (j, h)),
                   pl.BlockSpec((hpb, nc, HK, HK), lambda h, j: (h, j, 0, 0))],
        out_shape=[jax.ShapeDtypeStruct((t, D), F32), jax.ShapeDtypeStruct((t, D), BF16),
                   jax.ShapeDtypeStruct((HEADS, t // CHUNK, HK, HK), F32)],
        scratch_shapes=[pltpu.VMEM((hpb, HK, HK), F32)],
        compiler_params=_cp(),
    )(p_hg, p_hg, p_hg, p_hg, tab, gain)


def _hgrn_bwd(p_hg, tab, gain, o_raw, states, dho):
    t = p_hg.shape[0]
    bt = min(HG_BT, t)
    nb, nc = t // bt, bt // CHUNK
    hpb = HG_HPB
    wide = hpb * HK

    def body(q_ref, f_ref, i_ref, g_ref, tab_ref, gain_ref, o_ref, st_ref, dho_ref,
             dq_ref, df_ref, di_ref, dg_ref, dtab_ref, dgain_ref, dstate, dlb):
        h, j = pl.program_id(0), pl.program_id(1)

        @pl.when(jnp.logical_and(h == 0, j == 0))
        def _():
            dgain_ref[...] = jnp.zeros_like(dgain_ref)

        @pl.when(j == 0)
        def _():
            dstate[...] = jnp.zeros_like(dstate)
            dlb[...] = jnp.zeros_like(dlb)

        row = lax.broadcasted_iota(jnp.int32, (CHUNK, HK), 0)
        tril = lax.broadcasted_iota(jnp.int32, (CHUNK, CHUNK), 0) >= lax.broadcasted_iota(jnp.int32, (CHUNK, CHUNK), 1)
        gain_v = gain_ref[...]

        def chunk(cc, carry):
            c = nc - 1 - cc
            sl = pl.ds(pl.multiple_of(c * CHUNK, CHUNK), CHUNK)
            dgain = jnp.zeros((1, HK), F32)
            for hh in range(hpb):
                ln = slice(hh * HK, (hh + 1) * HK)
                lb = _sig(tab_ref[0:1, ln] - tab_ref[1:2, ln])
                qr = q_ref[sl, ln]
                v = i_ref[sl, ln].astype(BF16)
                gr = g_ref[sl, ln]
                q, sg, f, k, last, e_q, e_k, e_a, e_l = _hg_gates(qr, f_ref[sl, ln], lb, row)
                o = o_ref[sl, ln]
                r = lax.rsqrt(jnp.mean(o * o, axis=-1, keepdims=True) + EPS)
                oh = o * r
                dh = dho_ref[sl, ln].astype(F32)
                dnorm = dh * _silu(gr)
                dg_ref[sl, ln] = (dh * oh * gain_v * _dsilu(gr)).astype(BF16)
                dgain = dgain + jnp.sum(dnorm * oh, axis=0, keepdims=True)
                dxh = dnorm * gain_v
                do = (r * (dxh - oh * jnp.mean(dxh * oh, axis=-1, keepdims=True))).astype(BF16)
                st0 = st_ref[hh, c]
                st0_b = st0.astype(BF16)
                ds1 = dstate[hh]
                ds1_b = ds1.astype(BF16)
                qt = (q * e_q).astype(BF16)
                kt = (k * e_k).astype(BF16)
                qd = (q * e_a).astype(BF16)
                kd = (k * e_l).astype(BF16)
                p = jnp.where(tril, _dot_nt(qt, kt), 0.0).astype(BF16)
                dp = jnp.where(tril, _dot_nt(do, v), 0.0).astype(BF16)
                dv = _dot_tn(p, do) + _dot_nt(kd, ds1_b)
                dqt = _dot(dp, kt)
                dkt = _dot_tn(dp, qt)
                dq_inter = _dot(do, st0_b) * e_a
                dk_inter = _dot(v, ds1_b) * e_l
                dq = dqt * e_q + dq_inter
                dk = dkt * e_k + dk_inter
                e_last = jnp.exp(last)
                dstate[hh] = _dot_tn(do, qd) + e_last * ds1
                dlast = (jnp.sum(k * dk_inter, axis=0, keepdims=True)
                         + e_last * jnp.sum(ds1 * st0, axis=0, keepdims=True))
                da = (qt.astype(F32) * dqt - kt.astype(F32) * dkt + q * dq_inter - k * dk_inter
                      + jnp.where(row == CHUNK - 1, dlast, 0.0))
                dlf = _rcumsum_rows(da, row)
                dfv = dlf / f - dk
                df_ref[sl, ln] = (dfv * (1.0 - lb) * sg * (1.0 - sg)).astype(BF16)
                dlb[:, ln] += jnp.sum(dfv * (1.0 - sg), axis=0, keepdims=True)
                dq_ref[sl, ln] = (dq * _dsilu(qr)).astype(BF16)
                di_ref[sl, ln] = dv.astype(BF16)
            dgain_ref[...] += dgain
            return carry

        lax.fori_loop(0, nc, chunk, 0)

        @pl.when(j == nb - 1)
        def _():
            lb = _sig(tab_ref[0:1, :] - tab_ref[1:2, :])
            d0 = dlb[...] * lb * (1.0 - lb)
            dtab_ref[0:1, :] = d0
            dtab_ref[1:2, :] = -d0

    def col(k):
        return pl.BlockSpec((bt, wide), lambda h, j, k=k: (nb - 1 - j, k * (HEADS // hpb) + h))

    tok = pl.BlockSpec((bt, wide), lambda h, j: (nb - 1 - j, h))
    return _call(
        body, name="hgrn_bwd", grid=(HEADS // hpb, nb),
        in_specs=[col(0), col(1), col(2), col(3),
                  pl.BlockSpec((2, wide), lambda h, j: (0, h)), pl.BlockSpec((1, HK), lambda h, j: (0, 0)),
                  tok, pl.BlockSpec((hpb, nc, HK, HK), lambda h, j: (h, nb - 1 - j, 0, 0)), tok],
        out_specs=[tok, tok, tok, tok,
                   pl.BlockSpec((2, wide), lambda h, j: (0, h)), pl.BlockSpec((1, HK), lambda h, j: (0, 0))],
        out_shape=[jax.ShapeDtypeStruct((t, D), BF16)] * 4
        + [jax.ShapeDtypeStruct((2, D), F32), jax.ShapeDtypeStruct((1, HK), F32)],
        scratch_shapes=[pltpu.VMEM((hpb, HK, HK), F32), pltpu.VMEM((1, wide), F32)],
        compiler_params=_cp(),
    )(p_hg, p_hg, p_hg, p_hg, tab, gain, o_raw, states, dho)


def _rope_tables(pos):
    t = pos.shape[0]
    tm = min(ROW_TM, t)
    inv = np.zeros((1, HK), np.float32)
    freq = (ROPE_THETA ** (-np.arange(0, ROPE, 2, dtype=np.float32) / ROPE)).astype(np.float32)
    inv[0, 0:ROPE // 2] = freq
    inv[0, ROPE // 2:ROPE] = freq
    sign = np.zeros((1, HK), np.float32)
    sign[0, 0:ROPE // 2] = -1.0
    sign[0, ROPE // 2:ROPE] = 1.0

    def body(pos_ref, inv_ref, sign_ref, cos_ref, sin_ref):
        ang = pos_ref[...].astype(F32) * inv_ref[...]
        cos_ref[...] = jnp.cos(ang)
        sin_ref[...] = jnp.sin(ang) * sign_ref[...]

    one = pl.BlockSpec((1, HK), lambda i: (0, 0))
    row = pl.BlockSpec((tm, HK), lambda i: (i, 0))
    return _call(
        body, name="rope_tables", grid=(t // tm,),
        in_specs=[pl.BlockSpec((tm, 1), lambda i: (i, 0)), one, one],
        out_specs=[row, row],
        out_shape=[jax.ShapeDtypeStruct((t, HK), F32)] * 2,
        compiler_params=_cp(),
    )(pos, jnp.asarray(inv), jnp.asarray(sign))


def _rope(x, cos, sin_signed):
    lane = lax.broadcasted_iota(jnp.int32, x.shape, 1)
    other = jnp.where(lane < ROPE // 2, pltpu.roll(x, HK - ROPE // 2, 1), pltpu.roll(x, ROPE // 2, 1))
    return x * cos + other * sin_signed


def _head_norm(xn, xr, g):
    ss = jnp.sum(xn * xn, axis=-1, keepdims=True) + jnp.sum(xr * xr, axis=-1, keepdims=True)
    r = lax.rsqrt(ss * (1.0 / QK) + EPS)
    return xn * r, xr * r


def _head_norm_bwd(xn, xr, g_n, g_r, dn, dr):
    hn, hr = _head_norm(xn, xr, None)
    ss = jnp.sum(xn * xn, axis=-1, keepdims=True) + jnp.sum(xr * xr, axis=-1, keepdims=True)
    r = lax.rsqrt(ss * (1.0 / QK) + EPS)
    dxn, dxr = dn * g_n, dr * g_r
    c = (jnp.sum(dxn * hn, axis=-1, keepdims=True) + jnp.sum(dxr * hr, axis=-1, keepdims=True)) * (1.0 / QK)
    return r * (dxn - hn * c), r * (dxr - hr * c), dn * hn, dr * hr


def _mla_prep_fwd(qf, kv, p_mla, cos, sin, gq, gk):
    t = qf.shape[0]
    tm = min(ROW_TM, t)

    def body(qf_ref, kv_ref, kpe_ref, cos_ref, sin_ref, gq_ref, gk_ref, q_ref, k_ref, v_ref):
        cos_v, sin_v = cos_ref[...], sin_ref[...]
        qn, qr = _head_norm(qf_ref[:, 0:HK], qf_ref[:, HK:QKP], None)
        q_ref[:, 0:HK] = (qn * gq_ref[:, 0:HK] * (SCALE * LOG2E)).astype(BF16)
        q_ref[:, HK:QKP] = (_rope(qr * gq_ref[:, HK:QKP], cos_v, sin_v) * (SCALE * LOG2E)).astype(BF16)
        kn, kr = _head_norm(kv_ref[:, 0:HK], kpe_ref[...], None)
        k_ref[:, 0:HK] = (kn * gk_ref[:, 0:HK]).astype(BF16)
        k_ref[:, HK:QKP] = _rope(kr * gk_ref[:, HK:QKP], cos_v, sin_v).astype(BF16)
        v_ref[...] = kv_ref[:, HK:QKP].astype(BF16)

    head = pl.BlockSpec((tm, QKP), lambda i, h: (i, h))
    tok = pl.BlockSpec((tm, HK), lambda i, h: (i, 0))
    gain = pl.BlockSpec((1, QKP), lambda i, h: (0, 0))
    return _call(
        body, name="mla_prep_fwd", grid=(t // tm, HEADS),
        in_specs=[head, head, pl.BlockSpec((tm, HK), lambda i, h: (i, MLA_COLS // HK - 1)), tok, tok, gain, gain],
        out_specs=[pl.BlockSpec((None, tm, QKP), lambda i, h: (h, i, 0)),
                   pl.BlockSpec((None, tm, QKP), lambda i, h: (h, i, 0)),
                   pl.BlockSpec((None, tm, HK), lambda i, h: (h, i, 0))],
        out_shape=[jax.ShapeDtypeStruct((HEADS, t, QKP), BF16), jax.ShapeDtypeStruct((HEADS, t, QKP), BF16),
                   jax.ShapeDtypeStruct((HEADS, t, HK), BF16)],
        compiler_params=_cp(),
    )(qf, kv, p_mla, cos, sin, gq, gk)


def _mla_prep_bwd(qf, kv, p_mla, cos, sin, gq, gk, dq, dk, dv):
    t = qf.shape[0]
    tm = min(ROW_TM, t)

    def body(qf_ref, kv_ref, kpe_ref, cos_ref, sin_ref, gq_ref, gk_ref, dq_ref, dk_ref, dv_ref,
             dqf_ref, dkv_ref, dkpe_ref, dgq_ref, dgk_ref):
        i, h = pl.program_id(0), pl.program_id(1)

        @pl.when(jnp.logical_and(i == 0, h == 0))
        def _():
            dgq_ref[...] = jnp.zeros_like(dgq_ref)
            dgk_ref[...] = jnp.zeros_like(dgk_ref)

        @pl.when(h == 0)
        def _():
            dkpe_ref[...] = jnp.zeros_like(dkpe_ref)

        cos_v, sin_v = cos_ref[...], -sin_ref[...]
        dqn = dq_ref[:, 0:HK].astype(F32) * SCALE
        dqr = _rope(dq_ref[:, HK:QKP].astype(F32), cos_v, sin_v) * SCALE
        a, b, ga, gb = _head_norm_bwd(qf_ref[:, 0:HK], qf_ref[:, HK:QKP], gq_ref[:, 0:HK], gq_ref[:, HK:QKP], dqn, dqr)
        dqf_ref[:, 0:HK] = a.astype(BF16)
        dqf_ref[:, HK:QKP] = b.astype(BF16)
        dgq_ref[:, 0:HK] += jnp.sum(ga, axis=0, keepdims=True)
        dgq_ref[:, HK:QKP] += jnp.sum(gb, axis=0, keepdims=True)
        dkn = dk_ref[:, 0:HK].astype(F32) * LN2
        dkr = _rope(dk_ref[:, HK:QKP].astype(F32), cos_v, sin_v) * LN2
        a, b, ga, gb = _head_norm_bwd(kv_ref[:, 0:HK], kpe_ref[...], gk_ref[:, 0:HK], gk_ref[:, HK:QKP], dkn, dkr)
        dkv_ref[:, 0:HK] = a.astype(BF16)
        dkv_ref[:, HK:QKP] = dv_ref[...].astype(BF16)
        dkpe_ref[...] += b
        dgk_ref[:, 0:HK] += jnp.sum(ga, axis=0, keepdims=True)
        dgk_ref[:, HK:QKP] += jnp.sum(gb, axis=0, keepdims=True)

    head = pl.BlockSpec((tm, QKP), lambda i, h: (i, h))
    tok = pl.BlockSpec((tm, HK), lambda i, h: (i, 0))
    gain = pl.BlockSpec((1, QKP), lambda i, h: (0, 0))
    hq = pl.BlockSpec((None, tm, QKP), lambda i, h: (h, i, 0))
    return _call(
        body, name="mla_prep_bwd", grid=(t // tm, HEADS),
        in_specs=[head, head, pl.BlockSpec((tm, HK), lambda i, h: (i, MLA_COLS // HK - 1)), tok, tok, gain, gain,
                  hq, hq, pl.BlockSpec((None, tm, HK), lambda i, h: (h, i, 0))],
        out_specs=[head, head, tok, gain, gain],
        out_shape=[jax.ShapeDtypeStruct((t, HEADS * QKP), BF16), jax.ShapeDtypeStruct((t, HEADS * QKP), BF16),
                   jax.ShapeDtypeStruct((t, HK), F32), jax.ShapeDtypeStruct((1, QKP), F32),
                   jax.ShapeDtypeStruct((1, QKP), F32)],
        compiler_params=_cp(),
    )(qf, kv, p_mla, cos, sin, gq, gk, dq, dk, dv)


def _chunk_mask(row0, rows, cols):
    r = lax.broadcasted_iota(jnp.int32, (rows, cols), 0) + row0
    c = lax.broadcasted_iota(jnp.int32, (rows, cols), 1)
    return jnp.right_shift(r, 6) >= jnp.right_shift(c, 6)


def _flash_fwd(q, k, v):
    t = q.shape[1]
    tq = min(TQ, t)
    nq = t // tq
    sub = min(SUBQ, tq)
    pairs = [(i, j) for i in range(nq) for j in range(i + 1)]
    qi = jnp.asarray([p[0] for p in pairs], jnp.int32)
    kj = jnp.asarray([p[1] for p in pairs], jnp.int32)

    def body(qi_ref, kj_ref, q_ref, k_ref, v_ref, o_ref, lse_ref, m_s, l_s, acc_s):
        n = pl.program_id(1)
        i, j = qi_ref[n], kj_ref[n]

        @pl.when(j == 0)
        def _():
            m_s[...] = jnp.full_like(m_s, NEG)
            l_s[...] = jnp.zeros_like(l_s)
            acc_s[...] = jnp.zeros_like(acc_s)

        def step(diag):
            for r in range(tq // sub):
                rows = slice(r * sub, (r + 1) * sub)
                cols = (r + 1) * sub if diag else tq
                s = _dot_nt(q_ref[rows, :], k_ref[0:cols, :])
                if diag:
                    s = jnp.where(_chunk_mask(r * sub, sub, cols), s, NEG)
                m_old = m_s[rows, :]
                m_new = jnp.maximum(m_old, jnp.max(s, axis=-1, keepdims=True))
                alpha = jnp.exp2(m_old - m_new)
                p = jnp.exp2(s - jnp.tile(m_new, (1, cols // HK)))
                l_s[rows, :] = alpha * l_s[rows, :] + jnp.sum(p, axis=-1, keepdims=True)
                acc_s[rows, :] = alpha * acc_s[rows, :] + _dot(p.astype(BF16), v_ref[0:cols, :])
                m_s[rows, :] = m_new

        @pl.when(j < i)
        def _():
            step(False)

        @pl.when(j == i)
        def _():
            step(True)
            l = l_s[...]
            o_ref[...] = (acc_s[...] / l).astype(BF16)
            lse_ref[...] = m_s[...] + jnp.log(l) * LOG2E

    grid_spec = pltpu.PrefetchScalarGridSpec(
        num_scalar_prefetch=2, grid=(HEADS, len(pairs)),
        in_specs=[pl.BlockSpec((None, tq, QKP), lambda h, n, qi, kj: (h, qi[n], 0)),
                  pl.BlockSpec((None, tq, QKP), lambda h, n, qi, kj: (h, kj[n], 0)),
                  pl.BlockSpec((None, tq, HK), lambda h, n, qi, kj: (h, kj[n], 0))],
        out_specs=[pl.BlockSpec((tq, HK), lambda h, n, qi, kj: (qi[n], h)),
                   pl.BlockSpec((None, tq, HK), lambda h, n, qi, kj: (h, qi[n], 0))],
        scratch_shapes=[pltpu.VMEM((tq, HK), F32), pltpu.VMEM((tq, HK), F32), pltpu.VMEM((tq, HK), F32)],
    )
    return _call(
        body, name="flash_fwd", grid_spec=grid_spec,
        out_shape=[jax.ShapeDtypeStruct((t, D), BF16), jax.ShapeDtypeStruct((HEADS, t, HK), F32)],
        compiler_params=_cp(),
    )(qi, kj, q, k, v)


def _attn_delta(do, o):
    t = do.shape[0]
    tm = min(ROW_TM, t)

    def body(do_ref, o_ref, d_ref):
        d = jnp.sum(do_ref[...].astype(F32) * o_ref[...].astype(F32), axis=-1, keepdims=True)
        d_ref[...] = jnp.broadcast_to(d, (tm, HK))

    blk = pl.BlockSpec((tm, HK), lambda i, h: (i, h))
    return _call(
        body, name="attn_delta", grid=(t // tm, HEADS),
        in_specs=[blk, blk],
        out_specs=pl.BlockSpec((None, tm, HK), lambda i, h: (h, i, 0)),
        out_shape=jax.ShapeDtypeStruct((HEADS, t, HK), F32),
        compiler_params=_cp(),
    )(do, o)


def _flash_bwd(q, k, v, lse, delta, do):
    t = q.shape[1]
    tq = min(TQ, t)
    nq = t // tq
    sub = min(SUBQ, tq)
    pairs = [(i, j) for j in range(nq) for i in range(j, nq)]
    qi = jnp.asarray([p[0] for p in pairs], jnp.int32)
    kj = jnp.asarray([p[1] for p in pairs], jnp.int32)
    npairs = len(pairs)

    def body(qi_ref, kj_ref, q_ref, k_ref, v_ref, lse_ref, dl_ref, do_ref, dq_ref, dk_ref, dv_ref):
        n = pl.program_id(1)
        i, j = qi_ref[n], kj_ref[n]

        @pl.when(n == 0)
        def _():
            dq_ref[...] = jnp.zeros_like(dq_ref)

        @pl.when(i == j)
        def _():
            dk_ref[...] = jnp.zeros_like(dk_ref)
            dv_ref[...] = jnp.zeros_like(dv_ref)

        def step(diag):
            for r in range(tq // sub):
                rows = slice(r * sub, (r + 1) * sub)
                cols = (r + 1) * sub if diag else tq
                qv, dov, kv_ = q_ref[rows, :], do_ref[rows, :], k_ref[0:cols, :]
                p = jnp.exp2(_dot_nt(qv, kv_) - jnp.tile(lse_ref[rows, :], (1, cols // HK)))
                if diag:
                    p = jnp.where(_chunk_mask(r * sub, sub, cols), p, 0.0)
                dp = _dot_nt(dov, v_ref[0:cols, :])
                ds = (p * (dp - jnp.tile(dl_ref[rows, :], (1, cols // HK)))).astype(BF16)
                dv_ref[0:cols, :] += _dot_tn(p.astype(BF16), dov)
                dk_ref[0:cols, :] += _dot_tn(ds, qv)
                dq_rows = pl.ds(pl.multiple_of(i * tq + r * sub, sub), sub)
                dq_ref[dq_rows, :] += _dot(ds, kv_)

        @pl.when(j < i)
        def _():
            step(False)

        @pl.when(j == i)
        def _():
            step(True)

    grid_spec = pltpu.PrefetchScalarGridSpec(
        num_scalar_prefetch=2, grid=(HEADS, npairs),
        in_specs=[pl.BlockSpec((None, tq, QKP), lambda h, n, qi, kj: (h, qi[n], 0)),
                  pl.BlockSpec((None, tq, QKP), lambda h, n, qi, kj: (h, kj[n], 0)),
                  pl.BlockSpec((None, tq, HK), lambda h, n, qi, kj: (h, kj[n], 0)),
                  pl.BlockSpec((None, tq, HK), lambda h, n, qi, kj: (h, qi[n], 0)),
                  pl.BlockSpec((None, tq, HK), lambda h, n, qi, kj: (h, qi[n], 0)),
                  pl.BlockSpec((tq, HK), lambda h, n, qi, kj: (qi[n], h))],
        out_specs=[pl.BlockSpec((None, t, QKP), lambda h, n, qi, kj: (h, 0, 0)),
                   pl.BlockSpec((None, tq, QKP), lambda h, n, qi, kj: (h, kj[n], 0)),
                   pl.BlockSpec((None, tq, HK), lambda h, n, qi, kj: (h, kj[n], 0))],
    )
    return _call(
        body, name="flash_bwd", grid_spec=grid_spec,
        out_shape=[jax.ShapeDtypeStruct((HEADS, t, QKP), F32), jax.ShapeDtypeStruct((HEADS, t, QKP), F32),
                   jax.ShapeDtypeStruct((HEADS, t, HK), F32)],
        compiler_params=_cp(56),
    )(qi, kj, q, k, v, lse, delta, do)


def _adamw(name, w, g, m, v):
    r, c = w.shape
    tr = r if r <= 256 else next(k for k in (256, 352, 384) if r % k == 0)

    def body(w_ref, g_ref, m_ref, v_ref, d_ref, nm_ref, nv_ref):
        gv = g_ref[...]
        nm = ADAM_B1 * m_ref[...] + (1.0 - ADAM_B1) * gv
        nv = ADAM_B2 * v_ref[...] + (1.0 - ADAM_B2) * (gv * gv)
        m_hat = nm / (1.0 - ADAM_B1 ** ADAM_STEP)
        v_hat = nv / (1.0 - ADAM_B2 ** ADAM_STEP)
        d_ref[...] = -ADAM_LR * (m_hat / (jnp.sqrt(v_hat) + ADAM_EPS) + ADAM_WD * w_ref[...])
        nm_ref[...] = nm
        nv_ref[...] = nv

    blk = pl.BlockSpec((tr, c), lambda i: (i, 0))
    return _call(
        body, name=name, grid=(r // tr,),
        in_specs=[blk] * 4, out_specs=[blk] * 3,
        out_shape=[jax.ShapeDtypeStruct((r, c), F32)] * 3,
        compiler_params=_cp(),
    )(w, g, m, v)


def _place():
    return lax.axis_index("x"), lax.axis_index("y"), lax.axis_index("c")


def _other_chips(x, y):
    return [(1 - x, y), (x, 1 - y), (1 - x, 1 - y)]


def _gather_weights(shard):
    r = shard.shape[0]
    half = r // 2

    def body(s_ref, g_ref, send_sems, recv_sems, local_sem):
        x, y, c = _place()
        sibling = (x, y, 1 - c)
        chips = _other_chips(x, y)

        def rows(px, py, pc):
            return g_ref.at[2 * px + py, pl.ds(pc * half, half), :]

        def copy(k, block, to, src=None):
            return pltpu.make_async_remote_copy(
                src_ref=rows(*block) if src is None else src, dst_ref=rows(*block),
                send_sem=send_sems.at[k], recv_sem=recv_sems.at[k], device_id=to, device_id_type=MESH)

        mine = pltpu.make_async_copy(s_ref, g_ref.at[2 * x + y], local_sem)
        mine.start()
        first = [copy(j, (x, y, c), (*chip, c), src=s_ref.at[pl.ds(c * half, half), :]) for j, chip in enumerate(chips)]
        for cp in first:
            cp.start()
        passed = [copy(3 + j, (*chip, c), sibling) for j, chip in enumerate(chips)]
        for j, chip in enumerate(chips):
            copy(j, (*chip, c), (x, y, c)).wait_recv()
            passed[j].start()
        for j, chip in enumerate(chips):
            copy(3 + j, (*chip, 1 - c), (x, y, c)).wait_recv()
        for cp in first + passed:
            cp.wait_send()
        mine.wait()

    return _call(
        body, name="gather_weights",
        in_specs=[pl.BlockSpec(memory_space=pl.ANY)],
        out_specs=pl.BlockSpec(memory_space=pl.ANY),
        out_shape=jax.ShapeDtypeStruct((4, r, PACK_W), shard.dtype),
        scratch_shapes=[pltpu.SemaphoreType.DMA((6,)), pltpu.SemaphoreType.DMA((6,)), pltpu.SemaphoreType.DMA],
    )(shard)


def _swap_halves(gp):
    r = gp.shape[1]
    half = r // 2

    def body(g_ref, o_ref, send_sem, recv_sem):
        x, y, c = _place()
        cp = pltpu.make_async_remote_copy(
            src_ref=g_ref.at[:, pl.ds((1 - c) * half, half), :], dst_ref=o_ref,
            send_sem=send_sem, recv_sem=recv_sem, device_id=(x, y, 1 - c), device_id_type=MESH)
        cp.start()
        cp.wait()

    return _call(
        body, name="grad_swap_halves",
        in_specs=[pl.BlockSpec(memory_space=pl.ANY)],
        out_specs=pl.BlockSpec(memory_space=pl.ANY),
        out_shape=jax.ShapeDtypeStruct((4, half, PACK_W), gp.dtype),
        scratch_shapes=[pltpu.SemaphoreType.DMA, pltpu.SemaphoreType.DMA],
    )(gp)


def _chip_sum(gp, got, c_arr):
    half = got.shape[1]
    tr = 512
    nb = half // tr

    def body(c_ref, a_ref, b_ref, o_ref):
        o_ref[...] = a_ref[...] + b_ref[...]

    grid_spec = pltpu.PrefetchScalarGridSpec(
        num_scalar_prefetch=1, grid=(4, nb),
        in_specs=[pl.BlockSpec((None, tr, PACK_W), lambda s, i, c: (s, c[0] * nb + i, 0)),
                  pl.BlockSpec((None, tr, PACK_W), lambda s, i, c: (s, i, 0))],
        out_specs=pl.BlockSpec((None, tr, PACK_W), lambda s, i, c: (s, i, 0)),
    )
    return _call(
        body, name="grad_chip_sum", grid_spec=grid_spec,
        out_shape=jax.ShapeDtypeStruct(got.shape, F32),
        compiler_params=_cp(),
    )(c_arr, gp, got)


def _scatter_chip_sums(cs):
    h = cs.shape[1]

    def body(s_ref, o_ref, send_sems, recv_sems):
        x, y, c = _place()
        cps = []
        for j, (px, py) in enumerate(_other_chips(x, y)):
            cps.append(pltpu.make_async_remote_copy(
                src_ref=s_ref.at[2 * px + py], dst_ref=o_ref.at[j],
                send_sem=send_sems.at[j], recv_sem=recv_sems.at[j], device_id=(px, py, c), device_id_type=MESH))
        for cp in cps:
            cp.start()
        for cp in cps:
            cp.wait()

    return _call(
        body, name="grad_scatter",
        in_specs=[pl.BlockSpec(memory_space=pl.ANY)],
        out_specs=pl.BlockSpec(memory_space=pl.ANY),
        out_shape=jax.ShapeDtypeStruct((3, h, PACK_W), cs.dtype),
        scratch_shapes=[pltpu.SemaphoreType.DMA((3,)), pltpu.SemaphoreType.DMA((3,))],
    )(cs)


def _shard_sum(cs, got, k_arr):
    h = cs.shape[1]
    tr = 512

    def body(k_ref, a_ref, b_ref, o_ref):
        o_ref[...] = ((a_ref[...] + b_ref[0]) + b_ref[1]) + b_ref[2]

    grid_spec = pltpu.PrefetchScalarGridSpec(
        num_scalar_prefetch=1, grid=(h // tr,),
        in_specs=[pl.BlockSpec((None, tr, PACK_W), lambda i, k: (k[0], i, 0)),
                  pl.BlockSpec((3, tr, PACK_W), lambda i, k: (0, i, 0))],
        out_specs=pl.BlockSpec((tr, PACK_W), lambda i, k: (i, 0)),
    )
    return _call(
        body, name="grad_shard_sum", grid_spec=grid_spec,
        out_shape=jax.ShapeDtypeStruct((h, PACK_W), F32),
        compiler_params=_cp(),
    )(k_arr, cs, got)


def _join_halves(mine):
    h = mine.shape[0]

    def body(m_ref, o_ref, send_sem, recv_sem, local_sem):
        x, y, c = _place()
        own = pltpu.make_async_copy(m_ref, o_ref.at[pl.ds(c * h, h), :], local_sem)
        own.start()
        cp = pltpu.make_async_remote_copy(
            src_ref=m_ref, dst_ref=o_ref.at[pl.ds(c * h, h), :],
            send_sem=send_sem, recv_sem=recv_sem, device_id=(x, y, 1 - c), device_id_type=MESH)
        cp.start()
        cp.wait_send()
        pltpu.make_async_remote_copy(
            src_ref=m_ref, dst_ref=o_ref.at[pl.ds((1 - c) * h, h), :],
            send_sem=send_sem, recv_sem=recv_sem, device_id=(x, y, 1 - c), device_id_type=MESH).wait_recv()
        own.wait()

    return _call(
        body, name="grad_join_halves",
        in_specs=[pl.BlockSpec(memory_space=pl.ANY)],
        out_specs=pl.BlockSpec(memory_space=pl.ANY),
        out_shape=jax.ShapeDtypeStruct((2 * h, PACK_W), mine.dtype),
        scratch_shapes=[pltpu.SemaphoreType.DMA, pltpu.SemaphoreType.DMA, pltpu.SemaphoreType.DMA],
    )(mine)


def _all_reduce_small(v):
    r = v.shape[0]

    def body(v_ref, o_ref, buf, send_sems, recv_sems):
        x, y, c = _place()
        me = 4 * x + 2 * y + c
        buf[me] = v_ref[...]
        cps = []
        for k in range(1, 8):
            peer = (x ^ (k >> 2), y ^ ((k >> 1) & 1), c ^ (k & 1))
            cps.append(pltpu.make_async_remote_copy(
                src_ref=v_ref, dst_ref=buf.at[me],
                send_sem=send_sems.at[k - 1], recv_sem=recv_sems.at[k - 1], device_id=peer, device_id_type=MESH))
        for cp in cps:
            cp.start()
        for k in range(1, 8):
            pltpu.make_async_remote_copy(
                src_ref=v_ref, dst_ref=buf.at[me ^ k],
                send_sem=send_sems.at[k - 1], recv_sem=recv_sems.at[k - 1],
                device_id=(x, y, c), device_id_type=MESH).wait_recv()
        for cp in cps:
            cp.wait_send()
        acc = buf[0]
        for k in range(1, 8):
            acc = acc + buf[k]
        o_ref[...] = acc

    return _call(
        body, name="all_reduce_small",
        in_specs=[pl.BlockSpec(memory_space=pltpu.VMEM)],
        out_specs=pl.BlockSpec(memory_space=pltpu.VMEM),
        out_shape=jax.ShapeDtypeStruct((r, 128), F32),
        scratch_shapes=[pltpu.VMEM((8, r, 128), F32), pltpu.SemaphoreType.DMA((7,)), pltpu.SemaphoreType.DMA((7,))],
    )(v)


def _pack(shards, dtype):
    parts = [s.astype(dtype).reshape(-1, PACK_W) for s in shards]
    rows = sum(p.shape[0] for p in parts)
    pad = -rows % PACK_ALIGN
    if pad:
        parts.append(jnp.zeros((pad, PACK_W), dtype))
    return jnp.concatenate(parts, axis=0)


def _unpack_full(g):
    out, at = {}, 0
    for name, rows, cols, axis in BIG:
        n = rows * cols // 4 // PACK_W
        blk = g[:, at:at + n, :]
        at += n
        if axis == 1:
            out[name] = blk.reshape(4, rows, cols // 4).transpose(1, 0, 2).reshape(rows, cols)
        else:
            out[name] = blk.reshape(rows, cols)
    return out


def _pack_grads(grads):
    parts = []
    for name, rows, cols, axis in BIG:
        g = grads[name]
        if axis == 1:
            g = g.reshape(rows, 4, cols // 4).transpose(1, 0, 2)
        parts.append(g.reshape(4, -1, PACK_W))
    rows_total = sum(p.shape[1] for p in parts)
    pad = -rows_total % PACK_ALIGN
    if pad:
        parts.append(jnp.zeros((4, pad, PACK_W), F32))
    return jnp.concatenate(parts, axis=1)


def _unpack_shard(s):
    out, at = {}, 0
    for name, rows, cols, axis in BIG:
        n = rows * cols // 4 // PACK_W
        shape = (rows, cols // 4) if axis == 1 else (rows // 4, cols)
        out[name] = s[at:at + n, :].reshape(shape)
        at += n
    return out


def _pack_small(parts):
    flat = jnp.concatenate([p.reshape(-1) for p in parts])
    pad = -flat.shape[0] % 1024
    return jnp.concatenate([flat, jnp.zeros((pad,), F32)]).reshape(-1, 128)


def _ffn_fwd(tag, h, gain, w_in, w_out):
    t = h.shape[0]
    tm = min(TM, t)
    n = _rms_fwd(tag + "_norm", h, gain)
    tn = 256

    def epi(accs, _):
        gate, up = accs
        return gate, up, _silu(gate) * up

    gate, up, act = _mm(tag + "_in", [_a_spec(n, tm)], [_b_nn(w_in, tn), _b_nn(w_in, tn, DFF // tn)], [(0, 0), (0, 1)],
                        epi, [], [BF16, BF16, BF16], t, DFF, tm, tn)
    (out,) = _mm(tag + "_out", [_a_spec(act, tm)], [_b_nn(w_out, 512)], [(0, 0)],
                 lambda accs, ex: (ex[0] + 0.5 * accs[0],), [_e_tile(h, tm, 512)], [F32], t, D, tm, 512)
    return out, (n, gate, up, act)


def _ffn_bwd(tag, h, gain, w_in, w_out, saved, dout):
    t = h.shape[0]
    tm = min(TM, t)
    n, gate, up, act = saved
    tn = 256

    def epi(accs, ex):
        da = 0.5 * accs[0]
        g, u = ex[0].astype(F32), ex[1].astype(F32)
        return da * u * _dsilu(g), da * _silu(g)

    dgate, dup = _mm(tag + "_dact", [_a_spec(dout, tm)], [_b_nt(w_out, tn)], [(0, 0)], epi,
                     [_e_tile(gate, tm, tn), _e_tile(up, tm, tn)], [BF16, BF16], t, DFF, tm, tn, trans_b=True)
    (dn,) = _mm(tag + "_dn", [_a_spec(dgate, tm), _a_spec(dup, tm)],
                [_b_nt(w_in, 512, DFF, 0), _b_nt(w_in, 512, DFF, 1)], [(0, 0), (1, 1)],
                lambda accs, ex: (accs[0] + accs[1],), [], [F32], t, D, tm, 512, trans_b=True)
    dh, dgain = _rms_bwd(tag + "_dnorm", h, gain, dn, dout)
    dw_out = _mm_tn(tag + "_dw_out", act, dout, scale=0.5, tm=DFF // 2, tn=D)
    dw_g = _mm_tn(tag + "_dw_gate", n, dgate, tm=D, tn=DFF // 2)
    dw_u = _mm_tn(tag + "_dw_up", n, dup, tm=D, tn=DFF // 2)
    return dh, dgain, jnp.concatenate([dw_g, dw_u], axis=1), dw_out


def kernel(x, positions, ffn1_norm, ffn1_w_in, ffn1_w_out, mix_norm, w_in, hg_lb_table, hg_out_norm, w_hg_branch, mla_q_lora_norm, w_q_up, mla_kv_lora_norm, w_kv_up, q_head_norm, k_head_norm, w_mla_branch, w_merge, b_merge, w_out, ffn2_norm, ffn2_w_in, ffn2_w_out, final_norm, loss_target, m_ffn1_norm, m_ffn1_w_in, m_ffn1_w_out, m_mix_norm, m_w_in, m_hg_lb_table, m_hg_out_norm, m_w_hg_branch, m_mla_q_lora_norm, m_w_q_up, m_mla_kv_lora_norm, m_w_kv_up, m_q_head_norm, m_k_head_norm, m_w_mla_branch, m_w_merge, m_b_merge, m_w_out, m_ffn2_norm, m_ffn2_w_in, m_ffn2_w_out, m_final_norm, v_ffn1_norm, v_ffn1_w_in, v_ffn1_w_out, v_mix_norm, v_w_in, v_hg_lb_table, v_hg_out_norm, v_w_hg_branch, v_mla_q_lora_norm, v_w_q_up, v_mla_kv_lora_norm, v_w_kv_up, v_q_head_norm, v_k_head_norm, v_w_mla_branch, v_w_merge, v_b_merge, v_w_out, v_ffn2_norm, v_ffn2_w_in, v_ffn2_w_out, v_final_norm):
    a = dict(locals())
    w = {n: a[n] for n in WEIGHT_ORDER}
    mom = {n: a["m_" + n] for n in WEIGHT_ORDER}
    var = {n: a["v_" + n] for n in WEIGHT_ORDER}
    t = x.shape[1]
    tm = min(TM, t)
    xt = x.reshape(t, D)
    target = loss_target.reshape(t, D)
    pos = positions.reshape(t, 1)
    x_i, y_i, c_i = _place()
    c_arr = c_i.astype(jnp.int32).reshape(1)
    k_arr = (2 * x_i + y_i).astype(jnp.int32).reshape(1)

    shard = _pack([w[n][0] for n, _, _, _ in BIG], BF16)
    full = _unpack_full(_gather_weights(shard))
    w_in_full = full["w_in"]
    w_in_hg = w_in_full[:, :4 * D]
    w_in_mla = jnp.pad(w_in_full[:, 4 * D:], ((0, 0), (0, MLA_COLS - (4800 - 4 * D))))
    w_q_pad = jnp.pad(full["w_q_up"].reshape(Q_LORA, HEADS, QK), ((0, 0), (0, 0), (0, QKP - QK))).reshape(Q_LORA, HEADS * QKP)
    w_kv = full["w_kv_up"]
    gq = jnp.pad(w["q_head_norm"], ((0, 0), (0, QKP - QK)))
    gk = jnp.pad(w["k_head_norm"], ((0, 0), (0, QKP - QK)))

    h1, ffn1_saved = _ffn_fwd("ffn1", xt, w["ffn1_norm"], full["ffn1_w_in"], full["ffn1_w_out"])
    u = _rms_fwd("mix_norm", h1, w["mix_norm"])
    ident = lambda accs, ex: (accs[0],)
    (p_hg,) = _mm("in_hg", [_a_spec(u, tm)], [_b_nn(w_in_hg, 512)], [(0, 0)], ident, [], [F32], t, 4 * D, tm, 512)
    (p_mla,) = _mm("in_mla", [_a_spec(u, tm)], [_b_nn(w_in_mla, MLA_COLS)], [(0, 0)], ident, [], [F32], t, MLA_COLS, tm, MLA_COLS)
    o_raw, hg_o, states = _hgrn_fwd(p_hg, w["hg_lb_table"], w["hg_out_norm"])
    (y_hg,) = _mm("hg_branch", [_a_spec(hg_o, tm)], [_b_nn(full["w_hg_branch"], 512)], [(0, 0)], ident, [], [BF16], t, D, tm, 512)
    cqn, ckvn = _lora_norm_fwd(p_mla, w["mla_q_lora_norm"], w["mla_kv_lora_norm"])
    (qf,) = _mm("q_up", [_a_spec(cqn, tm)], [_b_nn(w_q_pad, 512)], [(0, 0)], ident, [], [F32], t, HEADS * QKP, tm, 512)
    (kvf,) = _mm("kv_up", [_a_spec(ckvn, tm)], [_b_nn(w_kv, 512)], [(0, 0)], ident, [], [F32], t, HEADS * QKP, tm, 512)
    cos, sin = _rope_tables(pos)
    qh, kh, vh = _mla_prep_fwd(qf, kvf, p_mla, cos, sin, gq, gk)
    o_mla, lse = _flash_fwd(qh, kh, vh)
    (y_mla,) = _mm("mla_branch", [_a_spec(o_mla, tm)], [_b_nn(full["w_mla_branch"], 512)], [(0, 0)], ident, [], [BF16], t, D, tm, 512)

    def merge_epi(accs, ex):
        g_hg = _sig(accs[0] + ex[2])
        g_mla = _sig(accs[1] + ex[3])
        return g_hg * ex[0].astype(F32) + g_mla * ex[1].astype(F32), g_hg, g_mla

    w_merge_f = full["w_merge"]
    mix, g_hg, g_mla = _mm(
        "merge", [_a_spec(u, tm)], [_b_nn(w_merge_f, 512), _b_nn(w_merge_f, 512, D // 512)], [(0, 0), (0, 1)], merge_epi,
        [_e_tile(y_hg, tm, 512), _e_tile(y_mla, tm, 512), _e_row(w["b_merge"], 512), _e_row(w["b_merge"], 512, D // 512)],
        [BF16, BF16, BF16], t, D, tm, 512)
    (h2,) = _mm("out_proj", [_a_spec(mix, tm)], [_b_nn(full["w_out"], 512)], [(0, 0)],
                lambda accs, ex: (ex[0] + accs[0],), [_e_tile(h1, tm, 512)], [F32], t, D, tm, 512)
    h3, ffn2_saved = _ffn_fwd("ffn2", h2, w["ffn2_norm"], full["ffn2_w_in"], full["ffn2_w_out"])
    dh3, d_final_norm, loss_part = _final_loss(h3, target, w["final_norm"])

    grads, small = {}, {}
    small["final_norm"] = d_final_norm
    dh2, small["ffn2_norm"], grads["ffn2_w_in"], grads["ffn2_w_out"] = _ffn_bwd(
        "ffn2", h2, w["ffn2_norm"], full["ffn2_w_in"], full["ffn2_w_out"], ffn2_saved, dh3)

    def dmix_epi(accs, ex):
        dm = accs[0]
        ghg, gml, yhg, yml = [e.astype(F32) for e in ex]
        return dm * ghg, dm * gml, dm * yhg * ghg * (1.0 - ghg), dm * yml * gml * (1.0 - gml)

    dy_hg, dy_mla, dpre_hg, dpre_mla = _mm(
        "d_mix", [_a_spec(dh2, tm)], [_b_nt(full["w_out"], 512)], [(0, 0)], dmix_epi,
        [_e_tile(g_hg, tm, 512), _e_tile(g_mla, tm, 512), _e_tile(y_hg, tm, 512), _e_tile(y_mla, tm, 512)],
        [BF16, BF16, BF16, BF16], t, D, tm, 512, trans_b=True)
    grads["w_out"] = _mm_tn("dw_out", mix, dh2)
    small["b_merge"] = jnp.concatenate([_colsum("db_hg", dpre_hg), _colsum("db_mla", dpre_mla)], axis=1)
    grads["w_merge"] = jnp.concatenate([_mm_tn("dw_merge_hg", u, dpre_hg), _mm_tn("dw_merge_mla", u, dpre_mla)], axis=1)
    grads["w_hg_branch"] = _mm_tn("dw_hg_branch", hg_o, dy_hg)
    grads["w_mla_branch"] = _mm_tn("dw_mla_branch", o_mla, dy_mla)
    (dho,) = _mm("d_hg_o", [_a_spec(dy_hg, tm)], [_b_nt(full["w_hg_branch"], 512)], [(0, 0)], ident, [], [BF16], t, D, tm, 512, trans_b=True)
    (do_mla,) = _mm("d_o_mla", [_a_spec(dy_mla, tm)], [_b_nt(full["w_mla_branch"], 512)], [(0, 0)], ident, [], [BF16], t, D, tm, 512, trans_b=True)

    dq_raw, df_raw, di_raw, dg_raw, small["hg_lb_table"], small["hg_out_norm"] = _hgrn_bwd(
        p_hg, w["hg_lb_table"], w["hg_out_norm"], o_raw, states, dho)
    dp_hg = [dq_raw, df_raw, di_raw, dg_raw]

    dqh, dkh, dvh = _flash_bwd(qh, kh, vh, lse, _attn_delta(do_mla, o_mla), do_mla)
    dqf, dkvf, dkpe, dgq, dgk = _mla_prep_bwd(qf, kvf, p_mla, cos, sin, gq, gk, dqh, dkh, dvh)
    small["q_head_norm"] = dgq[:, :QK]
    small["k_head_norm"] = dgk[:, :QK]
    dwq_pad = _mm_tn("dw_q_up", cqn, dqf, tm=Q_LORA, tn=1024)
    grads["w_q_up"] = dwq_pad.reshape(Q_LORA, HEADS, QKP)[:, :, :QK].reshape(Q_LORA, HEADS * QK)
    grads["w_kv_up"] = _mm_tn("dw_kv_up", ckvn, dkvf, tm=KV_LORA, tn=1024)
    (dcqn,) = _mm("d_cq", [_a_spec(dqf, tm)], [_b_nt(w_q_pad, Q_LORA)], [(0, 0)], ident, [], [F32], t, Q_LORA, tm, Q_LORA, trans_b=True)
    (dckvn,) = _mm("d_ckv", [_a_spec(dkvf, tm)], [_b_nt(w_kv, KV_LORA)], [(0, 0)], ident, [], [F32], t, KV_LORA, tm, KV_LORA, trans_b=True)
    dp_mla, small["mla_q_lora_norm"], small["mla_kv_lora_norm"] = _lora_norm_bwd(
        p_mla, w["mla_q_lora_norm"], w["mla_kv_lora_norm"], dcqn, dckvn, dkpe)

    dw_in_hg = [_mm_tn("dw_in_hg%d" % k, u, dp_hg[k]) for k in range(4)]
    dw_in_mla = _mm_tn("dw_in_mla", u, dp_mla, tn=MLA_COLS)
    grads["w_in"] = jnp.concatenate(dw_in_hg + [dw_in_mla[:, :4800 - 4 * D]], axis=1)
    (du,) = _mm(
        "d_u",
        [_a_spec(dpre_hg, tm), _a_spec(dpre_mla, tm)] + [_a_spec(d, tm) for d in dp_hg] + [_a_spec(dp_mla, tm)],
        [_b_nt(w_merge_f, 512, D, 0), _b_nt(w_merge_f, 512, D, 1)]
        + [_b_nt(w_in_hg, 512, D, k) for k in range(4)] + [_b_nt(w_in_mla, 512)],
        [(k, k) for k in range(7)],
        lambda accs, ex: (functools.reduce(lambda p, q: p + q, accs),), [], [F32], t, D, tm, 512, trans_b=True)
    dh1, small["mix_norm"] = _rms_bwd("mix_dnorm", h1, w["mix_norm"], du, dh2)
    dx, small["ffn1_norm"], grads["ffn1_w_in"], grads["ffn1_w_out"] = _ffn_bwd(
        "ffn1", xt, w["ffn1_norm"], full["ffn1_w_in"], full["ffn1_w_out"], ffn1_saved, dh1)

    gp = _pack_grads(grads)
    chip_sums = _chip_sum(gp, _swap_halves(gp), c_arr)
    my_half = _shard_sum(chip_sums, _scatter_chip_sums(chip_sums), k_arr)
    g_shard = _unpack_shard(_join_halves(my_half))
    small_sum = _all_reduce_small(_pack_small([small[n] for n, _ in SMALL] + [loss_part])).reshape(-1)
    g_small, at = {}, 0
    for n, shape in SMALL:
        size = shape[0] * shape[1]
        g_small[n] = small_sum[at:at + size].reshape(shape)
        at += size
    loss = small_sum[at]

    g_out, d_out, m_out, v_out = {}, {}, {}, {}
    for n in WEIGHT_ORDER:
        shape = w[n].shape
        g = g_shard[n] if n in g_shard else g_small[n]
        two = g.shape
        d_, m_, v_ = _adamw("adamw_" + n, w[n].reshape(two), g, mom[n].reshape(two), var[n].reshape(two))
        g_out[n], d_out[n], m_out[n], v_out[n] = g.reshape(shape), d_.reshape(shape), m_.reshape(shape), v_.reshape(shape)

    return (loss, dx.reshape(x.shape), *[g_out[n] for n in WEIGHT_ORDER], *[d_out[n] for n in WEIGHT_ORDER],
            *[m_out[n] for n in WEIGHT_ORDER], *[v_out[n] for n in WEIGHT_ORDER])
```

```python
import functools

import numpy as np
import jax
import jax.numpy as jnp
from jax import lax
from jax.experimental import pallas as pl
from jax.experimental.pallas import tpu as pltpu

F32 = jnp.float32
BF16 = jnp.bfloat16
MESH = pl.DeviceIdType.MESH

D = 1024
DFF = 2816
HEADS = 8
HK = 128
CHUNK = 64
ROPE = 64
QK = 192
QKP = 256
Q_LORA = 384
KV_LORA = 256
MLA_COLS = 768
EPS = 1e-6
ROPE_THETA = 10000.0
SCALE = QK ** -0.5
LOG2E = 1.4426950408889634
LN2 = 0.6931471805599453
NEG = -1e30
EXP_CLAMP = 80.0

ADAM_LR = 0.001
ADAM_B1 = 0.9
ADAM_B2 = 0.999
ADAM_EPS = 1e-08
ADAM_WD = 0.01
ADAM_STEP = 10

PACK_W = 1024
PACK_ALIGN = 1024

TM = 512
TQ = 1024
SUBQ = 512
HG_BT = 512
HG_HPB = 4
TT = 512
ROW_TM = 256

VMEM_MB = 48

BIG = (
    ("ffn1_w_in", D, 2 * DFF, 1),
    ("ffn1_w_out", DFF, D, 0),
    ("w_in", D, 4800, 1),
    ("w_hg_branch", D, D, 0),
    ("w_q_up", Q_LORA, HEADS * QK, 1),
    ("w_kv_up", KV_LORA, HEADS * 2 * HK, 1),
    ("w_mla_branch", D, D, 0),
    ("w_merge", D, 2 * D, 1),
    ("w_out", D, D, 0),
    ("ffn2_w_in", D, 2 * DFF, 1),
    ("ffn2_w_out", DFF, D, 0),
)
SMALL = (
    ("ffn1_norm", (1, D)),
    ("mix_norm", (1, D)),
    ("hg_lb_table", (2, D)),
    ("hg_out_norm", (1, HK)),
    ("mla_q_lora_norm", (1, Q_LORA)),
    ("mla_kv_lora_norm", (1, KV_LORA)),
    ("q_head_norm", (1, QK)),
    ("k_head_norm", (1, QK)),
    ("b_merge", (1, 2 * D)),
    ("ffn2_norm", (1, D)),
    ("final_norm", (1, D)),
)
WEIGHT_ORDER = ("ffn1_norm", "ffn1_w_in", "ffn1_w_out", "mix_norm", "w_in", "hg_lb_table", "hg_out_norm",
                "w_hg_branch", "mla_q_lora_norm", "w_q_up", "mla_kv_lora_norm", "w_kv_up", "q_head_norm",
                "k_head_norm", "w_mla_branch", "w_merge", "b_merge", "w_out", "ffn2_norm", "ffn2_w_in",
                "ffn2_w_out", "final_norm")


def _call(body, **kw):
    return pl.pallas_call(body, **kw)


def _cp(vmem_mb=VMEM_MB):
    return pltpu.CompilerParams(vmem_limit_bytes=vmem_mb << 20)


def _dot(a, b):
    return lax.dot_general(a, b, (((1,), (0,)), ((), ())), preferred_element_type=F32)


def _dot_nt(a, b):
    return lax.dot_general(a, b, (((1,), (1,)), ((), ())), preferred_element_type=F32)


def _dot_tn(a, b):
    return lax.dot_general(a, b, (((0,), (0,)), ((), ())), preferred_element_type=F32)


def _sig(x):
    return jax.nn.sigmoid(x)


def _silu(x):
    return x * _sig(x)


def _dsilu(x):
    s = _sig(x)
    return s * (1.0 + x * (1.0 - s))


def _a_spec(arr, tm, kblk=None, kidx=0):
    kb = arr.shape[1] if kblk is None else kblk
    return arr, pl.BlockSpec((tm, kb), lambda i, j, kidx=kidx: (i, kidx))


def _b_nn(arr, tn, off=0):
    return arr, pl.BlockSpec((arr.shape[0], tn), lambda i, j, off=off: (0, j + off))


def _b_nt(arr, tn, kblk=None, kidx=0):
    kb = arr.shape[1] if kblk is None else kblk
    return arr, pl.BlockSpec((tn, kb), lambda i, j, kidx=kidx: (j, kidx))


def _e_tile(arr, tm, tn, off=0):
    return arr, pl.BlockSpec((tm, tn), lambda i, j, off=off: (i, j + off))


def _e_row(arr, tn, off=0):
    return arr, pl.BlockSpec((1, tn), lambda i, j, off=off: (0, j + off))


def _mm(name, As, Bs, dots, epi, extras, out_dtypes, m, n, tm, tn, trans_b=False):
    na, nb, ne = len(As), len(Bs), len(extras)

    def body(*refs):
        a_refs = refs[:na]
        b_refs = refs[na:na + nb]
        e_refs = refs[na + nb:na + nb + ne]
        o_refs = refs[na + nb + ne:]
        a_vals = [r[...].astype(BF16) for r in a_refs]
        accs = []
        for ai, bi in dots:
            b = b_refs[bi][...]
            accs.append(_dot_nt(a_vals[ai], b) if trans_b else _dot(a_vals[ai], b))
        outs = epi(accs, [r[...] for r in e_refs])
        for o_ref, o in zip(o_refs, outs):
            o_ref[...] = o.astype(o_ref.dtype)

    ops = list(As) + list(Bs) + list(extras)
    res = _call(
        body, name=name,
        grid=(m // tm, n // tn),
        in_specs=[s for _, s in ops],
        out_specs=[pl.BlockSpec((tm, tn), lambda i, j: (i, j)) for _ in out_dtypes],
        out_shape=[jax.ShapeDtypeStruct((m, n), dt) for dt in out_dtypes],
        compiler_params=_cp(),
    )(*[a for a, _ in ops])
    return res


def _mm_tn(name, a, b, scale=1.0, tm=1024, tn=1024):
    t, m = a.shape
    n = b.shape[1]
    tm, tn, tt = min(tm, m), min(tn, n), min(TT, t)
    nk = t // tt

    def body(a_ref, b_ref, o_ref):
        k = pl.program_id(2)

        @pl.when(k == 0)
        def _():
            o_ref[...] = jnp.zeros_like(o_ref)

        o_ref[...] += _dot_tn(a_ref[...].astype(BF16), b_ref[...].astype(BF16))
        if scale != 1.0:
            @pl.when(k == nk - 1)
            def _():
                o_ref[...] = o_ref[...] * scale

    return _call(
        body, name=name,
        grid=(m // tm, n // tn, nk),
        in_specs=[pl.BlockSpec((tt, tm), lambda i, j, k: (k, i)), pl.BlockSpec((tt, tn), lambda i, j, k: (k, j))],
        out_specs=pl.BlockSpec((tm, tn), lambda i, j, k: (i, j)),
        out_shape=jax.ShapeDtypeStruct((m, n), F32),
        compiler_params=_cp(),
    )(a, b)


def _rms_fwd(name, x, gain):
    t, d = x.shape
    tm = min(ROW_TM, t)

    def body(x_ref, g_ref, o_ref):
        xv = x_ref[...]
        r = lax.rsqrt(jnp.mean(xv * xv, axis=-1, keepdims=True) + EPS)
        o_ref[...] = (xv * r * g_ref[...]).astype(o_ref.dtype)

    return _call(
        body, name=name, grid=(t // tm,),
        in_specs=[pl.BlockSpec((tm, d), lambda i: (i, 0)), pl.BlockSpec((1, d), lambda i: (0, 0))],
        out_specs=pl.BlockSpec((tm, d), lambda i: (i, 0)),
        out_shape=jax.ShapeDtypeStruct((t, d), BF16),
        compiler_params=_cp(),
    )(x, gain)


def _rms_bwd_vals(xv, g, dn):
    r = lax.rsqrt(jnp.mean(xv * xv, axis=-1, keepdims=True) + EPS)
    xh = xv * r
    dxh = dn * g
    c = jnp.mean(dxh * xh, axis=-1, keepdims=True)
    return r * (dxh - xh * c), dn * xh


def _rms_bwd(name, x, gain, dn, dres):
    t, d = x.shape
    tm = min(ROW_TM, t)

    def body(x_ref, g_ref, dn_ref, dr_ref, dx_ref, dg_ref):
        @pl.when(pl.program_id(0) == 0)
        def _():
            dg_ref[...] = jnp.zeros_like(dg_ref)

        dx, dg = _rms_bwd_vals(x_ref[...], g_ref[...], dn_ref[...].astype(F32))
        dx_ref[...] = dr_ref[...] + dx
        dg_ref[...] += jnp.sum(dg, axis=0, keepdims=True)

    row = pl.BlockSpec((tm, d), lambda i: (i, 0))
    one = pl.BlockSpec((1, d), lambda i: (0, 0))
    return _call(
        body, name=name, grid=(t // tm,),
        in_specs=[row, one, row, row],
        out_specs=[row, one],
        out_shape=[jax.ShapeDtypeStruct((t, d), F32), jax.ShapeDtypeStruct((1, d), F32)],
        compiler_params=_cp(),
    )(x, gain, dn, dres)


def _final_loss(h, target, gain):
    t, d = h.shape
    tm = min(ROW_TM, t)

    def body(h_ref, t_ref, g_ref, dh_ref, dg_ref, l_ref):
        @pl.when(pl.program_id(0) == 0)
        def _():
            dg_ref[...] = jnp.zeros_like(dg_ref)
            l_ref[...] = jnp.zeros_like(l_ref)

        hv = h_ref[...]
        g = g_ref[...]
        r = lax.rsqrt(jnp.mean(hv * hv, axis=-1, keepdims=True) + EPS)
        xh = hv * r
        err = xh * g - t_ref[...]
        l_ref[...] += 0.5 * jnp.sum(jnp.mean(err * err, axis=-1, keepdims=True), axis=0, keepdims=True)
        dy = err * (1.0 / d)
        dxh = dy * g
        c = jnp.mean(dxh * xh, axis=-1, keepdims=True)
        dh_ref[...] = r * (dxh - xh * c)
        dg_ref[...] += jnp.sum(dy * xh, axis=0, keepdims=True)

    row = pl.BlockSpec((tm, d), lambda i: (i, 0))
    one = pl.BlockSpec((1, d), lambda i: (0, 0))
    return _call(
        body, name="final_loss", grid=(t // tm,),
        in_specs=[row, row, one],
        out_specs=[row, one, pl.BlockSpec((1, 128), lambda i: (0, 0))],
        out_shape=[jax.ShapeDtypeStruct((t, d), F32), jax.ShapeDtypeStruct((1, d), F32),
                   jax.ShapeDtypeStruct((1, 128), F32)],
        compiler_params=_cp(),
    )(h, target, gain)


def _colsum(name, x):
    t, n = x.shape
    tm = min(TM, t)

    def body(x_ref, o_ref):
        @pl.when(pl.program_id(0) == 0)
        def _():
            o_ref[...] = jnp.zeros_like(o_ref)

        o_ref[...] += jnp.sum(x_ref[...].astype(F32), axis=0, keepdims=True)

    return _call(
        body, name=name, grid=(t // tm,),
        in_specs=[pl.BlockSpec((tm, n), lambda i: (i, 0))],
        out_specs=pl.BlockSpec((1, n), lambda i: (0, 0)),
        out_shape=jax.ShapeDtypeStruct((1, n), F32),
        compiler_params=_cp(),
    )(x)


def _lora_norm_fwd(p_mla, gq, gkv):
    t = p_mla.shape[0]
    tm = min(ROW_TM, t)

    def body(p_ref, gq_ref, gkv_ref, q_ref, kv_ref):
        cq = p_ref[:, 0:Q_LORA]
        ckv = p_ref[:, Q_LORA:Q_LORA + KV_LORA]
        rq = lax.rsqrt(jnp.mean(cq * cq, axis=-1, keepdims=True) + EPS)
        rkv = lax.rsqrt(jnp.mean(ckv * ckv, axis=-1, keepdims=True) + EPS)
        q_ref[...] = (cq * rq * gq_ref[...]).astype(BF16)
        kv_ref[...] = (ckv * rkv * gkv_ref[...]).astype(BF16)

    return _call(
        body, name="lora_norm_fwd", grid=(t // tm,),
        in_specs=[pl.BlockSpec((tm, MLA_COLS), lambda i: (i, 0)),
                  pl.BlockSpec((1, Q_LORA), lambda i: (0, 0)), pl.BlockSpec((1, KV_LORA), lambda i: (0, 0))],
        out_specs=[pl.BlockSpec((tm, Q_LORA), lambda i: (i, 0)), pl.BlockSpec((tm, KV_LORA), lambda i: (i, 0))],
        out_shape=[jax.ShapeDtypeStruct((t, Q_LORA), BF16), jax.ShapeDtypeStruct((t, KV_LORA), BF16)],
        compiler_params=_cp(),
    )(p_mla, gq, gkv)


def _lora_norm_bwd(p_mla, gq, gkv, dcqn, dckvn, dkpe):
    t = p_mla.shape[0]
    tm = min(ROW_TM, t)

    def body(p_ref, gq_ref, gkv_ref, dq_ref, dkv_ref, dkpe_ref, dp_ref, dgq_ref, dgkv_ref):
        @pl.when(pl.program_id(0) == 0)
        def _():
            dgq_ref[...] = jnp.zeros_like(dgq_ref)
            dgkv_ref[...] = jnp.zeros_like(dgkv_ref)

        dcq, dgq = _rms_bwd_vals(p_ref[:, 0:Q_LORA], gq_ref[...], dq_ref[...])
        dckv, dgkv = _rms_bwd_vals(p_ref[:, Q_LORA:Q_LORA + KV_LORA], gkv_ref[...], dkv_ref[...])
        dp_ref[:, 0:Q_LORA] = dcq.astype(BF16)
        dp_ref[:, Q_LORA:Q_LORA + KV_LORA] = dckv.astype(BF16)
        dp_ref[:, Q_LORA + KV_LORA:MLA_COLS] = dkpe_ref[...].astype(BF16)
        dgq_ref[...] += jnp.sum(dgq, axis=0, keepdims=True)
        dgkv_ref[...] += jnp.sum(dgkv, axis=0, keepdims=True)

    return _call(
        body, name="lora_norm_bwd", grid=(t // tm,),
        in_specs=[pl.BlockSpec((tm, MLA_COLS), lambda i: (i, 0)),
                  pl.BlockSpec((1, Q_LORA), lambda i: (0, 0)), pl.BlockSpec((1, KV_LORA), lambda i: (0, 0)),
                  pl.BlockSpec((tm, Q_LORA), lambda i: (i, 0)), pl.BlockSpec((tm, KV_LORA), lambda i: (i, 0)),
                  pl.BlockSpec((tm, HK), lambda i: (i, 0))],
        out_specs=[pl.BlockSpec((tm, MLA_COLS), lambda i: (i, 0)),
                   pl.BlockSpec((1, Q_LORA), lambda i: (0, 0)), pl.BlockSpec((1, KV_LORA), lambda i: (0, 0))],
        out_shape=[jax.ShapeDtypeStruct((t, MLA_COLS), BF16), jax.ShapeDtypeStruct((1, Q_LORA), F32),
                   jax.ShapeDtypeStruct((1, KV_LORA), F32)],
        compiler_params=_cp(),
    )(p_mla, gq, gkv, dcqn, dckvn, dkpe)


def _cumsum_rows(x, row):
    for s in (1, 2, 4, 8, 16, 32):
        x = x + jnp.where(row >= s, pltpu.roll(x, s, 0), 0.0)
    return x


def _rcumsum_rows(x, row):
    for s in (1, 2, 4, 8, 16, 32):
        x = x + jnp.where(row < CHUNK - s, pltpu.roll(x, CHUNK - s, 0), 0.0)
    return x


def _hg_gates(qr, z, lb, row):
    q = _silu(qr)
    sg = _sig(z)
    f = lb + (1.0 - lb) * sg
    lf = jnp.log(f)
    k = (1.0 - lb) * (1.0 - sg)
    cum = _cumsum_rows(lf, row)
    mid = jnp.sum(jnp.where(row < CHUNK // 2, lf, 0.0), axis=0, keepdims=True)
    last = jnp.sum(lf, axis=0, keepdims=True)
    e_q = jnp.exp(jnp.minimum(cum - mid, EXP_CLAMP))
    e_k = jnp.exp(jnp.minimum(mid - cum, EXP_CLAMP))
    e_a = jnp.exp(cum)
    e_l = jnp.exp(last - cum)
    return q, sg, f, k, last, e_q, e_k, e_a, e_l


def _hgrn_fwd(p_hg, tab, gain):
    t = p_hg.shape[0]
    bt = min(HG_BT, t)
    nb, nc = t // bt, bt // CHUNK

    hpb = HG_HPB
    wide = hpb * HK

    def body(q_ref, f_ref, i_ref, g_ref, tab_ref, gain_ref, o_ref, ho_ref, st_ref, state):
        @pl.when(pl.program_id(1) == 0)
        def _():
            state[...] = jnp.zeros_like(state)

        row = lax.broadcasted_iota(jnp.int32, (CHUNK, HK), 0)
        tril = lax.broadcasted_iota(jnp.int32, (CHUNK, CHUNK), 0) >= lax.broadcasted_iota(jnp.int32, (CHUNK, CHUNK), 1)
        gain_v = gain_ref[...]

        def chunk(c, carry):
            sl = pl.ds(pl.multiple_of(c * CHUNK, CHUNK), CHUNK)
            for hh in range(hpb):
                ln = slice(hh * HK, (hh + 1) * HK)
                lb = _sig(tab_ref[0:1, ln] - tab_ref[1:2, ln])
                v = i_ref[sl, ln].astype(BF16)
                q, _, _, k, last, e_q, e_k, e_a, e_l = _hg_gates(q_ref[sl, ln], f_ref[sl, ln], lb, row)
                st = state[hh]
                st_ref[hh, c] = st
                p = jnp.where(tril, _dot_nt((q * e_q).astype(BF16), (k * e_k).astype(BF16)), 0.0)
                o = _dot(p.astype(BF16), v) + _dot_nt((q * e_a).astype(BF16), st.astype(BF16))
                state[hh] = jnp.exp(last) * st + _dot_tn(v, (k * e_l).astype(BF16))
                o_ref[sl, ln] = o
                r = lax.rsqrt(jnp.mean(o * o, axis=-1, keepdims=True) + EPS)
                ho_ref[sl, ln] = (o * r * gain_v * _silu(g_ref[sl, ln])).astype(BF16)
            return carry

        lax.fori_loop(0, nc, chunk, 0)

    def col(k):
        return pl.BlockSpec((bt, wide), lambda h, j, k=k: (j, k * (HEADS // hpb) + h))

    return _call(
        body, name="hgrn_fwd", grid=(HEADS // hpb, nb),
        in_specs=[col(0), col(1), col(2), col(3),
                  pl.BlockSpec((2, wide), lambda h, j: (0, h)), pl.BlockSpec((1, HK), lambda h, j: (0, 0))],
        out_specs=[pl.BlockSpec((bt, wide), lambda h, j: (j, h)), pl.BlockSpec((bt, wide), lambda h, j: (j, h)),
                   pl.BlockSpec((hpb, nc, HK, HK), lambda h, j: (h, j, 0, 0))],
        out_shape=[jax.ShapeDtypeStruct((t, D), F32), jax.ShapeDtypeStruct((t, D), BF16),
                   jax.ShapeDtypeStruct((HEADS, t // CHUNK, HK, HK), F32)],
        scratch_shapes=[pltpu.VMEM((hpb, HK, HK), F32)],
        compiler_params=_cp(),
    )(p_hg, p_hg, p_hg, p_hg, tab, gain)


def _hgrn_bwd(p_hg, tab, gain, o_raw, states, dho):
    t = p_hg.shape[0]
    bt = min(HG_BT, t)
    nb, nc = t // bt, bt // CHUNK
    hpb = HG_HPB
    wide = hpb * HK

    def body(q_ref, f_ref, i_ref, g_ref, tab_ref, gain_ref, o_ref, st_ref, dho_ref,
             dq_ref, df_ref, di_ref, dg_ref, dtab_ref, dgain_ref, dstate, dlb):
        h, j = pl.program_id(0), pl.program_id(1)

        @pl.when(jnp.logical_and(h == 0, j == 0))
        def _():
            dgain_ref[...] = jnp.zeros_like(dgain_ref)

        @pl.when(j == 0)
        def _():
            dstate[...] = jnp.zeros_like(dstate)
            dlb[...] = jnp.zeros_like(dlb)

        row = lax.broadcasted_iota(jnp.int32, (CHUNK, HK), 0)
        tril = lax.broadcasted_iota(jnp.int32, (CHUNK, CHUNK), 0) >= lax.broadcasted_iota(jnp.int32, (CHUNK, CHUNK), 1)
        gain_v = gain_ref[...]

        def chunk(cc, carry):
            c = nc - 1 - cc
            sl = pl.ds(pl.multiple_of(c * CHUNK, CHUNK), CHUNK)
            dgain = jnp.zeros((1, HK), F32)
            for hh in range(hpb):
                ln = slice(hh * HK, (hh + 1) * HK)
                lb = _sig(tab_ref[0:1, ln] - tab_ref[1:2, ln])
                qr = q_ref[sl, ln]
                v = i_ref[sl, ln].astype(BF16)
                gr = g_ref[sl, ln]
                q, sg, f, k, last, e_q, e_k, e_a, e_l = _hg_gates(qr, f_ref[sl, ln], lb, row)
                o = o_ref[sl, ln]
                r = lax.rsqrt(jnp.mean(o * o, axis=-1, keepdims=True) + EPS)
                oh = o * r
                dh = dho_ref[sl, ln].astype(F32)
                dnorm = dh * _silu(gr)
                dg_ref[sl, ln] = (dh * oh * gain_v * _dsilu(gr)).astype(BF16)
                dgain = dgain + jnp.sum(dnorm * oh, axis=0, keepdims=True)
                dxh = dnorm * gain_v
                do = (r * (dxh - oh * jnp.mean(dxh * oh, axis=-1, keepdims=True))).astype(BF16)
                st0 = st_ref[hh, c]
                st0_b = st0.astype(BF16)
                ds1 = dstate[hh]
                ds1_b = ds1.astype(BF16)
                qt = (q * e_q).astype(BF16)
                kt = (k * e_k).astype(BF16)
                qd = (q * e_a).astype(BF16)
                kd = (k * e_l).astype(BF16)
                p = jnp.where(tril, _dot_nt(qt, kt), 0.0).astype(BF16)
                dp = jnp.where(tril, _dot_nt(do, v), 0.0).astype(BF16)
                dv = _dot_tn(p, do) + _dot_nt(kd, ds1_b)
                dqt = _dot(dp, kt)
                dkt = _dot_tn(dp, qt)
                dq_inter = _dot(do, st0_b) * e_a
                dk_inter = _dot(v, ds1_b) * e_l
                dq = dqt * e_q + dq_inter
                dk = dkt * e_k + dk_inter
                e_last = jnp.exp(last)
                dstate[hh] = _dot_tn(do, qd) + e_last * ds1
                dlast = (jnp.sum(k * dk_inter, axis=0, keepdims=True)
                         + e_last * jnp.sum(ds1 * st0, axis=0, keepdims=True))
                da = (qt.astype(F32) * dqt - kt.astype(F32) * dkt + q * dq_inter - k * dk_inter
                      + jnp.where(row == CHUNK - 1, dlast, 0.0))
                dlf = _rcumsum_rows(da, row)
                dfv = dlf / f - dk
                df_ref[sl, ln] = (dfv * (1.0 - lb) * sg * (1.0 - sg)).astype(BF16)
                dlb[:, ln] += jnp.sum(dfv * (1.0 - sg), axis=0, keepdims=True)
                dq_ref[sl, ln] = (dq * _dsilu(qr)).astype(BF16)
                di_ref[sl, ln] = dv.astype(BF16)
            dgain_ref[...] += dgain
            return carry

        lax.fori_loop(0, nc, chunk, 0)

        @pl.when(j == nb - 1)
        def _():
            lb = _sig(tab_ref[0:1, :] - tab_ref[1:2, :])
            d0 = dlb[...] * lb * (1.0 - lb)
            dtab_ref[0:1, :] = d0
            dtab_ref[1:2, :] = -d0

    def col(k):
        return pl.BlockSpec((bt, wide), lambda h, j, k=k: (nb - 1 - j, k * (HEADS // hpb) + h))

    tok = pl.BlockSpec((bt, wide), lambda h, j: (nb - 1 - j, h))
    return _call(
        body, name="hgrn_bwd", grid=(HEADS // hpb, nb),
        in_specs=[col(0), col(1), col(2), col(3),
                  pl.BlockSpec((2, wide), lambda h, j: (0, h)), pl.BlockSpec((1, HK), lambda h, j: (0, 0)),
                  tok, pl.BlockSpec((hpb, nc, HK, HK), lambda h, j: (h, nb - 1 - j, 0, 0)), tok],
        out_specs=[tok, tok, tok, tok,
                   pl.BlockSpec((2, wide), lambda h, j: (0, h)), pl.BlockSpec((1, HK), lambda h, j: (0, 0))],
        out_shape=[jax.ShapeDtypeStruct((t, D), BF16)] * 4
        + [jax.ShapeDtypeStruct((2, D), F32), jax.ShapeDtypeStruct((1, HK), F32)],
        scratch_shapes=[pltpu.VMEM((hpb, HK, HK), F32), pltpu.VMEM((1, wide), F32)],
        compiler_params=_cp(),
    )(p_hg, p_hg, p_hg, p_hg, tab, gain, o_raw, states, dho)


def _rope_tables(pos):
    t = pos.shape[0]
    tm = min(ROW_TM, t)
    inv = np.zeros((1, HK), np.float32)
    freq = (ROPE_THETA ** (-np.arange(0, ROPE, 2, dtype=np.float32) / ROPE)).astype(np.float32)
    inv[0, 0:ROPE // 2] = freq
    inv[0, ROPE // 2:ROPE] = freq
    sign = np.zeros((1, HK), np.float32)
    sign[0, 0:ROPE // 2] = -1.0
    sign[0, ROPE // 2:ROPE] = 1.0

    def body(pos_ref, inv_ref, sign_ref, cos_ref, sin_ref):
        ang = pos_ref[...].astype(F32) * inv_ref[...]
        cos_ref[...] = jnp.cos(ang)
        sin_ref[...] = jnp.sin(ang) * sign_ref[...]

    one = pl.BlockSpec((1, HK), lambda i: (0, 0))
    row = pl.BlockSpec((tm, HK), lambda i: (i, 0))
    return _call(
        body, name="rope_tables", grid=(t // tm,),
        in_specs=[pl.BlockSpec((tm, 1), lambda i: (i, 0)), one, one],
        out_specs=[row, row],
        out_shape=[jax.ShapeDtypeStruct((t, HK), F32)] * 2,
        compiler_params=_cp(),
    )(pos, jnp.asarray(inv), jnp.asarray(sign))


def _rope(x, cos, sin_signed):
    r = lax.broadcasted_iota(jnp.int32, (HK, HK), 0)
    c = lax.broadcasted_iota(jnp.int32, (HK, HK), 1)
    half = ROPE // 2
    swap = jnp.logical_or(jnp.logical_and(c < half, r == c + half),
                          jnp.logical_and(jnp.logical_and(c >= half, c < ROPE), r == c - half))
    return x * cos + _dot_split(x, swap.astype(BF16)) * sin_signed


def _dot_split(x, m):
    hi = x.astype(BF16)
    lo = (x - hi.astype(F32)).astype(BF16)
    return _dot(hi, m) + _dot(lo, m)


def _lane_sum(x):
    return _dot_split(x, jnp.ones((HK, HK), BF16))


def _head_norm(xn, xr):
    r = lax.rsqrt(_lane_sum(xn * xn + xr * xr) * (1.0 / QK) + EPS)
    return xn * r, xr * r, r


def _head_norm_bwd(xn, xr, g_n, g_r, dn, dr):
    hn, hr, r = _head_norm(xn, xr)
    dxn, dxr = dn * g_n, dr * g_r
    c = _lane_sum(dxn * hn + dxr * hr) * (1.0 / QK)
    return r * (dxn - hn * c), r * (dxr - hr * c), dn * hn, dr * hr


def _mla_prep_fwd(qf, kv, p_mla, cos, sin, gq, gk):
    t = qf.shape[0]
    tm = min(ROW_TM, t)

    def body(qf_ref, kv_ref, kpe_ref, cos_ref, sin_ref, gq_ref, gk_ref, q_ref, k_ref, v_ref):
        cos_v, sin_v = cos_ref[...], sin_ref[...]
        kpe = kpe_ref[...]
        for h in range(HEADS):
            lo, mid, hi = h * QKP, h * QKP + HK, (h + 1) * QKP
            qn, qr, _ = _head_norm(qf_ref[:, lo:mid], qf_ref[:, mid:hi])
            q_ref[h, :, 0:HK] = (qn * gq_ref[:, 0:HK] * (SCALE * LOG2E)).astype(BF16)
            q_ref[h, :, HK:QKP] = (_rope(qr * gq_ref[:, HK:QKP], cos_v, sin_v) * (SCALE * LOG2E)).astype(BF16)
            kn, kr, _ = _head_norm(kv_ref[:, lo:mid], kpe)
            k_ref[h, :, 0:HK] = (kn * gk_ref[:, 0:HK]).astype(BF16)
            k_ref[h, :, HK:QKP] = _rope(kr * gk_ref[:, HK:QKP], cos_v, sin_v).astype(BF16)
            v_ref[h] = kv_ref[:, mid:hi].astype(BF16)

    head = pl.BlockSpec((tm, HEADS * QKP), lambda i: (i, 0))
    tok = pl.BlockSpec((tm, HK), lambda i: (i, 0))
    gain = pl.BlockSpec((1, QKP), lambda i: (0, 0))
    return _call(
        body, name="mla_prep_fwd", grid=(t // tm,),
        in_specs=[head, head, pl.BlockSpec((tm, HK), lambda i: (i, MLA_COLS // HK - 1)), tok, tok, gain, gain],
        out_specs=[pl.BlockSpec((HEADS, tm, QKP), lambda i: (0, i, 0)),
                   pl.BlockSpec((HEADS, tm, QKP), lambda i: (0, i, 0)),
                   pl.BlockSpec((HEADS, tm, HK), lambda i: (0, i, 0))],
        out_shape=[jax.ShapeDtypeStruct((HEADS, t, QKP), BF16), jax.ShapeDtypeStruct((HEADS, t, QKP), BF16),
                   jax.ShapeDtypeStruct((HEADS, t, HK), BF16)],
        compiler_params=_cp(),
    )(qf, kv, p_mla, cos, sin, gq, gk)


def _mla_prep_bwd(qf, kv, p_mla, cos, sin, gq, gk, dq, dk, dv):
    t = qf.shape[0]
    tm = min(ROW_TM, t)

    def body(qf_ref, kv_ref, kpe_ref, cos_ref, sin_ref, gq_ref, gk_ref, dq_ref, dk_ref, dv_ref,
             dqf_ref, dkv_ref, dkpe_ref, dgq_ref, dgk_ref):
        @pl.when(pl.program_id(0) == 0)
        def _():
            dgq_ref[...] = jnp.zeros_like(dgq_ref)
            dgk_ref[...] = jnp.zeros_like(dgk_ref)

        cos_v, sin_v = cos_ref[...], -sin_ref[...]
        kpe = kpe_ref[...]
        gqn, gqr, gkn, gkr = gq_ref[:, 0:HK], gq_ref[:, HK:QKP], gk_ref[:, 0:HK], gk_ref[:, HK:QKP]
        dkpe = jnp.zeros((tm, HK), F32)
        dgq_n, dgq_r, dgk_n, dgk_r = [jnp.zeros((1, HK), F32) for _ in range(4)]
        for h in range(HEADS):
            lo, mid, hi = h * QKP, h * QKP + HK, (h + 1) * QKP
            dqn = dq_ref[h, :, 0:HK].astype(F32) * SCALE
            dqr = _rope(dq_ref[h, :, HK:QKP].astype(F32), cos_v, sin_v) * SCALE
            a, b, ga, gb = _head_norm_bwd(qf_ref[:, lo:mid], qf_ref[:, mid:hi], gqn, gqr, dqn, dqr)
            dqf_ref[:, lo:mid] = a.astype(BF16)
            dqf_ref[:, mid:hi] = b.astype(BF16)
            dgq_n = dgq_n + jnp.sum(ga, axis=0, keepdims=True)
            dgq_r = dgq_r + jnp.sum(gb, axis=0, keepdims=True)
            dkn = dk_ref[h, :, 0:HK].astype(F32) * LN2
            dkr = _rope(dk_ref[h, :, HK:QKP].astype(F32), cos_v, sin_v) * LN2
            a, b, ga, gb = _head_norm_bwd(kv_ref[:, lo:mid], kpe, gkn, gkr, dkn, dkr)
            dkv_ref[:, lo:mid] = a.astype(BF16)
            dkv_ref[:, mid:hi] = dv_ref[h].astype(BF16)
            dkpe = dkpe + b
            dgk_n = dgk_n + jnp.sum(ga, axis=0, keepdims=True)
            dgk_r = dgk_r + jnp.sum(gb, axis=0, keepdims=True)
        dkpe_ref[...] = dkpe
        dgq_ref[:, 0:HK] += dgq_n
        dgq_ref[:, HK:QKP] += dgq_r
        dgk_ref[:, 0:HK] += dgk_n
        dgk_ref[:, HK:QKP] += dgk_r

    head = pl.BlockSpec((tm, HEADS * QKP), lambda i: (i, 0))
    tok = pl.BlockSpec((tm, HK), lambda i: (i, 0))
    gain = pl.BlockSpec((1, QKP), lambda i: (0, 0))
    hq = pl.BlockSpec((HEADS, tm, QKP), lambda i: (0, i, 0))
    return _call(
        body, name="mla_prep_bwd", grid=(t // tm,),
        in_specs=[head, head, pl.BlockSpec((tm, HK), lambda i: (i, MLA_COLS // HK - 1)), tok, tok, gain, gain,
                  hq, hq, pl.BlockSpec((HEADS, tm, HK), lambda i: (0, i, 0))],
        out_specs=[head, head, tok, gain, gain],
        out_shape=[jax.ShapeDtypeStruct((t, HEADS * QKP), BF16), jax.ShapeDtypeStruct((t, HEADS * QKP), BF16),
                   jax.ShapeDtypeStruct((t, HK), F32), jax.ShapeDtypeStruct((1, QKP), F32),
                   jax.ShapeDtypeStruct((1, QKP), F32)],
        compiler_params=_cp(),
    )(qf, kv, p_mla, cos, sin, gq, gk, dq, dk, dv)


def _chunk_mask(row0, rows, cols):
    r = lax.broadcasted_iota(jnp.int32, (rows, cols), 0) + row0
    c = lax.broadcasted_iota(jnp.int32, (rows, cols), 1)
    return jnp.right_shift(r, 6) >= jnp.right_shift(c, 6)


def _flash_fwd(q, k, v):
    t = q.shape[1]
    tq = min(TQ, t)
    nq = t // tq
    sub = min(SUBQ, tq)
    pairs = [(i, j) for i in range(nq) for j in range(i + 1)]
    qi = jnp.asarray([p[0] for p in pairs], jnp.int32)
    kj = jnp.asarray([p[1] for p in pairs], jnp.int32)

    def body(qi_ref, kj_ref, q_ref, k_ref, v_ref, o_ref, lse_ref, m_s, l_s, acc_s):
        n = pl.program_id(1)
        i, j = qi_ref[n], kj_ref[n]

        @pl.when(j == 0)
        def _():
            m_s[...] = jnp.full_like(m_s, NEG)
            l_s[...] = jnp.zeros_like(l_s)
            acc_s[...] = jnp.zeros_like(acc_s)

        def step(diag):
            for r in range(tq // sub):
                rows = slice(r * sub, (r + 1) * sub)
                cols = (r + 1) * sub if diag else tq
                s = _dot_nt(q_ref[rows, :], k_ref[0:cols, :])
                if diag:
                    s = jnp.where(_chunk_mask(r * sub, sub, cols), s, NEG)
                m_old = m_s[rows, :]
                m_new = jnp.maximum(m_old, jnp.max(s, axis=-1, keepdims=True))
                alpha = jnp.exp2(m_old - m_new)
                p = jnp.exp2(s - jnp.tile(m_new, (1, cols // HK)))
                l_s[rows, :] = alpha * l_s[rows, :] + jnp.sum(p, axis=-1, keepdims=True)
                acc_s[rows, :] = alpha * acc_s[rows, :] + _dot(p.astype(BF16), v_ref[0:cols, :])
                m_s[rows, :] = m_new

        @pl.when(j < i)
        def _():
            step(False)

        @pl.when(j == i)
        def _():
            step(True)
            l = l_s[...]
            o_ref[...] = (acc_s[...] / l).astype(BF16)
            lse_ref[...] = m_s[...] + jnp.log(l) * LOG2E

    grid_spec = pltpu.PrefetchScalarGridSpec(
        num_scalar_prefetch=2, grid=(HEADS, len(pairs)),
        in_specs=[pl.BlockSpec((None, tq, QKP), lambda h, n, qi, kj: (h, qi[n], 0)),
                  pl.BlockSpec((None, tq, QKP), lambda h, n, qi, kj: (h, kj[n], 0)),
                  pl.BlockSpec((None, tq, HK), lambda h, n, qi, kj: (h, kj[n], 0))],
        out_specs=[pl.BlockSpec((tq, HK), lambda h, n, qi, kj: (qi[n], h)),
                   pl.BlockSpec((None, tq, HK), lambda h, n, qi, kj: (h, qi[n], 0))],
        scratch_shapes=[pltpu.VMEM((tq, HK), F32), pltpu.VMEM((tq, HK), F32), pltpu.VMEM((tq, HK), F32)],
    )
    return _call(
        body, name="flash_fwd", grid_spec=grid_spec,
        out_shape=[jax.ShapeDtypeStruct((t, D), BF16), jax.ShapeDtypeStruct((HEADS, t, HK), F32)],
        compiler_params=_cp(),
    )(qi, kj, q, k, v)


def _attn_delta(do, o):
    t = do.shape[0]
    tm = min(TM, t)

    def body(do_ref, o_ref, d_ref):
        for h in range(HEADS):
            ln = slice(h * HK, (h + 1) * HK)
            d = jnp.sum(do_ref[:, ln].astype(F32) * o_ref[:, ln].astype(F32), axis=-1, keepdims=True)
            d_ref[h] = jnp.broadcast_to(d, (tm, HK))

    blk = pl.BlockSpec((tm, D), lambda i: (i, 0))
    return _call(
        body, name="attn_delta", grid=(t // tm,),
        in_specs=[blk, blk],
        out_specs=pl.BlockSpec((HEADS, tm, HK), lambda i: (0, i, 0)),
        out_shape=jax.ShapeDtypeStruct((HEADS, t, HK), F32),
        compiler_params=_cp(),
    )(do, o)


def _flash_bwd(q, k, v, lse, delta, do):
    t = q.shape[1]
    tq = min(TQ, t)
    nq = t // tq
    sub = min(SUBQ, tq)
    pairs = [(i, j) for j in range(nq) for i in range(j, nq)]
    qi = jnp.asarray([p[0] for p in pairs], jnp.int32)
    kj = jnp.asarray([p[1] for p in pairs], jnp.int32)
    npairs = len(pairs)

    def body(qi_ref, kj_ref, q_ref, k_ref, v_ref, lse_ref, dl_ref, do_ref, dq_ref, dk_ref, dv_ref):
        n = pl.program_id(1)
        i, j = qi_ref[n], kj_ref[n]

        @pl.when(n == 0)
        def _():
            dq_ref[...] = jnp.zeros_like(dq_ref)

        @pl.when(i == j)
        def _():
            dk_ref[...] = jnp.zeros_like(dk_ref)
            dv_ref[...] = jnp.zeros_like(dv_ref)

        def step(diag):
            for r in range(tq // sub):
                rows = slice(r * sub, (r + 1) * sub)
                cols = (r + 1) * sub if diag else tq
                qv, dov, kv_ = q_ref[rows, :], do_ref[rows, :], k_ref[0:cols, :]
                p = jnp.exp2(_dot_nt(qv, kv_) - jnp.tile(lse_ref[rows, :], (1, cols // HK)))
                if diag:
                    p = jnp.where(_chunk_mask(r * sub, sub, cols), p, 0.0)
                dp = _dot_nt(dov, v_ref[0:cols, :])
                ds = (p * (dp - jnp.tile(dl_ref[rows, :], (1, cols // HK)))).astype(BF16)
                dv_ref[0:cols, :] += _dot_tn(p.astype(BF16), dov)
                dk_ref[0:cols, :] += _dot_tn(ds, qv)
                dq_rows = pl.ds(pl.multiple_of(i * tq + r * sub, sub), sub)
                dq_ref[dq_rows, :] += _dot(ds, kv_)

        @pl.when(j < i)
        def _():
            step(False)

        @pl.when(j == i)
        def _():
            step(True)

    grid_spec = pltpu.PrefetchScalarGridSpec(
        num_scalar_prefetch=2, grid=(HEADS, npairs),
        in_specs=[pl.BlockSpec((None, tq, QKP), lambda h, n, qi, kj: (h, qi[n], 0)),
                  pl.BlockSpec((None, tq, QKP), lambda h, n, qi, kj: (h, kj[n], 0)),
                  pl.BlockSpec((None, tq, HK), lambda h, n, qi, kj: (h, kj[n], 0)),
                  pl.BlockSpec((None, tq, HK), lambda h, n, qi, kj: (h, qi[n], 0)),
                  pl.BlockSpec((None, tq, HK), lambda h, n, qi, kj: (h, qi[n], 0)),
                  pl.BlockSpec((tq, HK), lambda h, n, qi, kj: (qi[n], h))],
        out_specs=[pl.BlockSpec((None, t, QKP), lambda h, n, qi, kj: (h, 0, 0)),
                   pl.BlockSpec((None, tq, QKP), lambda h, n, qi, kj: (h, kj[n], 0)),
                   pl.BlockSpec((None, tq, HK), lambda h, n, qi, kj: (h, kj[n], 0))],
    )
    return _call(
        body, name="flash_bwd", grid_spec=grid_spec,
        out_shape=[jax.ShapeDtypeStruct((HEADS, t, QKP), F32), jax.ShapeDtypeStruct((HEADS, t, QKP), F32),
                   jax.ShapeDtypeStruct((HEADS, t, HK), F32)],
        compiler_params=_cp(56),
    )(qi, kj, q, k, v, lse, delta, do)


def _adamw(name, w, g, m, v):
    r, c = w.shape
    tr = r if r <= 256 else next(k for k in (256, 352, 384) if r % k == 0)

    def body(w_ref, g_ref, m_ref, v_ref, d_ref, nm_ref, nv_ref):
        gv = g_ref[...]
        nm = ADAM_B1 * m_ref[...] + (1.0 - ADAM_B1) * gv
        nv = ADAM_B2 * v_ref[...] + (1.0 - ADAM_B2) * (gv * gv)
        m_hat = nm / (1.0 - ADAM_B1 ** ADAM_STEP)
        v_hat = nv / (1.0 - ADAM_B2 ** ADAM_STEP)
        d_ref[...] = -ADAM_LR * (m_hat / (jnp.sqrt(v_hat) + ADAM_EPS) + ADAM_WD * w_ref[...])
        nm_ref[...] = nm
        nv_ref[...] = nv

    blk = pl.BlockSpec((tr, c), lambda i: (i, 0))
    return _call(
        body, name=name, grid=(r // tr,),
        in_specs=[blk] * 4, out_specs=[blk] * 3,
        out_shape=[jax.ShapeDtypeStruct((r, c), F32)] * 3,
        compiler_params=_cp(),
    )(w, g, m, v)


def _place():
    return lax.axis_index("x"), lax.axis_index("y"), lax.axis_index("c")


def _other_chips(x, y):
    return [(1 - x, y), (x, 1 - y), (1 - x, 1 - y)]


def _gather_weights(shard):
    r = shard.shape[0]
    half = r // 2

    def body(s_ref, g_ref, send_sems, recv_sems, local_sem):
        x, y, c = _place()
        sibling = (x, y, 1 - c)
        chips = _other_chips(x, y)

        def rows(px, py, pc):
            return g_ref.at[2 * px + py, pl.ds(pc * half, half), :]

        def copy(k, block, to, src=None):
            return pltpu.make_async_remote_copy(
                src_ref=rows(*block) if src is None else src, dst_ref=rows(*block),
                send_sem=send_sems.at[k], recv_sem=recv_sems.at[k], device_id=to, device_id_type=MESH)

        mine = pltpu.make_async_copy(s_ref, g_ref.at[2 * x + y], local_sem)
        mine.start()
        first = [copy(j, (x, y, c), (*chip, c), src=s_ref.at[pl.ds(c * half, half), :]) for j, chip in enumerate(chips)]
        for cp in first:
            cp.start()
        passed = [copy(3 + j, (*chip, c), sibling) for j, chip in enumerate(chips)]
        for j, chip in enumerate(chips):
            copy(j, (*chip, c), (x, y, c)).wait_recv()
            passed[j].start()
        for j, chip in enumerate(chips):
            copy(3 + j, (*chip, 1 - c), (x, y, c)).wait_recv()
        for cp in first + passed:
            cp.wait_send()
        mine.wait()

    return _call(
        body, name="gather_weights",
        in_specs=[pl.BlockSpec(memory_space=pl.ANY)],
        out_specs=pl.BlockSpec(memory_space=pl.ANY),
        out_shape=jax.ShapeDtypeStruct((4, r, PACK_W), shard.dtype),
        scratch_shapes=[pltpu.SemaphoreType.DMA((6,)), pltpu.SemaphoreType.DMA((6,)), pltpu.SemaphoreType.DMA],
    )(shard)


def _swap_halves(gp):
    r = gp.shape[1]
    half = r // 2

    def body(g_ref, o_ref, send_sem, recv_sem):
        x, y, c = _place()
        cp = pltpu.make_async_remote_copy(
            src_ref=g_ref.at[:, pl.ds((1 - c) * half, half), :], dst_ref=o_ref,
            send_sem=send_sem, recv_sem=recv_sem, device_id=(x, y, 1 - c), device_id_type=MESH)
        cp.start()
        cp.wait()

    return _call(
        body, name="grad_swap_halves",
        in_specs=[pl.BlockSpec(memory_space=pl.ANY)],
        out_specs=pl.BlockSpec(memory_space=pl.ANY),
        out_shape=jax.ShapeDtypeStruct((4, half, PACK_W), gp.dtype),
        scratch_shapes=[pltpu.SemaphoreType.DMA, pltpu.SemaphoreType.DMA],
    )(gp)


def _chip_sum(gp, got, c_arr):
    half = got.shape[1]
    tr = 512
    nb = half // tr

    def body(c_ref, a_ref, b_ref, o_ref):
        o_ref[...] = a_ref[...] + b_ref[...]

    grid_spec = pltpu.PrefetchScalarGridSpec(
        num_scalar_prefetch=1, grid=(4, nb),
        in_specs=[pl.BlockSpec((None, tr, PACK_W), lambda s, i, c: (s, c[0] * nb + i, 0)),
                  pl.BlockSpec((None, tr, PACK_W), lambda s, i, c: (s, i, 0))],
        out_specs=pl.BlockSpec((None, tr, PACK_W), lambda s, i, c: (s, i, 0)),
    )
    return _call(
        body, name="grad_chip_sum", grid_spec=grid_spec,
        out_shape=jax.ShapeDtypeStruct(got.shape, F32),
        compiler_params=_cp(),
    )(c_arr, gp, got)


def _scatter_chip_sums(cs):
    h = cs.shape[1]

    def body(s_ref, o_ref, send_sems, recv_sems):
        x, y, c = _place()
        cps = []
        for j, (px, py) in enumerate(_other_chips(x, y)):
            cps.append(pltpu.make_async_remote_copy(
                src_ref=s_ref.at[2 * px + py], dst_ref=o_ref.at[j],
                send_sem=send_sems.at[j], recv_sem=recv_sems.at[j], device_id=(px, py, c), device_id_type=MESH))
        for cp in cps:
            cp.start()
        for cp in cps:
            cp.wait()

    return _call(
        body, name="grad_scatter",
        in_specs=[pl.BlockSpec(memory_space=pl.ANY)],
        out_specs=pl.BlockSpec(memory_space=pl.ANY),
        out_shape=jax.ShapeDtypeStruct((3, h, PACK_W), cs.dtype),
        scratch_shapes=[pltpu.SemaphoreType.DMA((3,)), pltpu.SemaphoreType.DMA((3,))],
    )(cs)


def _shard_sum(cs, got, k_arr):
    h = cs.shape[1]
    tr = 512

    def body(k_ref, a_ref, b_ref, o_ref):
        o_ref[...] = ((a_ref[...] + b_ref[0]) + b_ref[1]) + b_ref[2]

    grid_spec = pltpu.PrefetchScalarGridSpec(
        num_scalar_prefetch=1, grid=(h // tr,),
        in_specs=[pl.BlockSpec((None, tr, PACK_W), lambda i, k: (k[0], i, 0)),
                  pl.BlockSpec((3, tr, PACK_W), lambda i, k: (0, i, 0))],
        out_specs=pl.BlockSpec((tr, PACK_W), lambda i, k: (i, 0)),
    )
    return _call(
        body, name="grad_shard_sum", grid_spec=grid_spec,
        out_shape=jax.ShapeDtypeStruct((h, PACK_W), F32),
        compiler_params=_cp(),
    )(k_arr, cs, got)


def _join_halves(mine):
    h = mine.shape[0]

    def body(m_ref, o_ref, send_sem, recv_sem, local_sem):
        x, y, c = _place()
        own = pltpu.make_async_copy(m_ref, o_ref.at[pl.ds(c * h, h), :], local_sem)
        own.start()
        cp = pltpu.make_async_remote_copy(
            src_ref=m_ref, dst_ref=o_ref.at[pl.ds(c * h, h), :],
            send_sem=send_sem, recv_sem=recv_sem, device_id=(x, y, 1 - c), device_id_type=MESH)
        cp.start()
        cp.wait_send()
        pltpu.make_async_remote_copy(
            src_ref=m_ref, dst_ref=o_ref.at[pl.ds((1 - c) * h, h), :],
            send_sem=send_sem, recv_sem=recv_sem, device_id=(x, y, 1 - c), device_id_type=MESH).wait_recv()
        own.wait()

    return _call(
        body, name="grad_join_halves",
        in_specs=[pl.BlockSpec(memory_space=pl.ANY)],
        out_specs=pl.BlockSpec(memory_space=pl.ANY),
        out_shape=jax.ShapeDtypeStruct((2 * h, PACK_W), mine.dtype),
        scratch_shapes=[pltpu.SemaphoreType.DMA, pltpu.SemaphoreType.DMA, pltpu.SemaphoreType.DMA],
    )(mine)


def _all_reduce_small(v):
    r = v.shape[0]

    def body(v_ref, o_ref, buf, send_sems, recv_sems):
        x, y, c = _place()
        me = 4 * x + 2 * y + c
        buf[me] = v_ref[...]
        cps = []
        for k in range(1, 8):
            peer = (x ^ (k >> 2), y ^ ((k >> 1) & 1), c ^ (k & 1))
            cps.append(pltpu.make_async_remote_copy(
                src_ref=v_ref, dst_ref=buf.at[me],
                send_sem=send_sems.at[k - 1], recv_sem=recv_sems.at[k - 1], device_id=peer, device_id_type=MESH))
        for cp in cps:
            cp.start()
        for k in range(1, 8):
            pltpu.make_async_remote_copy(
                src_ref=v_ref, dst_ref=buf.at[me ^ k],
                send_sem=send_sems.at[k - 1], recv_sem=recv_sems.at[k - 1],
                device_id=(x, y, c), device_id_type=MESH).wait_recv()
        for cp in cps:
            cp.wait_send()
        acc = buf[0]
        for k in range(1, 8):
            acc = acc + buf[k]
        o_ref[...] = acc

    return _call(
        body, name="all_reduce_small",
        in_specs=[pl.BlockSpec(memory_space=pltpu.VMEM)],
        out_specs=pl.BlockSpec(memory_space=pltpu.VMEM),
        out_shape=jax.ShapeDtypeStruct((r, 128), F32),
        scratch_shapes=[pltpu.VMEM((8, r, 128), F32), pltpu.SemaphoreType.DMA((7,)), pltpu.SemaphoreType.DMA((7,))],
    )(v)


def _pack(shards, dtype):
    parts = [s.astype(dtype).reshape(-1, PACK_W) for s in shards]
    rows = sum(p.shape[0] for p in parts)
    pad = -rows % PACK_ALIGN
    if pad:
        parts.append(jnp.zeros((pad, PACK_W), dtype))
    return jnp.concatenate(parts, axis=0)


def _unpack_full(g):
    out, at = {}, 0
    for name, rows, cols, axis in BIG:
        n = rows * cols // 4 // PACK_W
        blk = g[:, at:at + n, :]
        at += n
        if axis == 1:
            out[name] = blk.reshape(4, rows, cols // 4).transpose(1, 0, 2).reshape(rows, cols)
        else:
            out[name] = blk.reshape(rows, cols)
    return out


def _pack_grads(grads):
    parts = []
    for name, rows, cols, axis in BIG:
        g = grads[name]
        if axis == 1:
            g = g.reshape(rows, 4, cols // 4).transpose(1, 0, 2)
        parts.append(g.reshape(4, -1, PACK_W))
    rows_total = sum(p.shape[1] for p in parts)
    pad = -rows_total % PACK_ALIGN
    if pad:
        parts.append(jnp.zeros((4, pad, PACK_W), F32))
    return jnp.concatenate(parts, axis=1)


def _unpack_shard(s):
    out, at = {}, 0
    for name, rows, cols, axis in BIG:
        n = rows * cols // 4 // PACK_W
        shape = (rows, cols // 4) if axis == 1 else (rows // 4, cols)
        out[name] = s[at:at + n, :].reshape(shape)
        at += n
    return out


def _pack_small(parts):
    flat = jnp.concatenate([p.reshape(-1) for p in parts])
    pad = -flat.shape[0] % 1024
    return jnp.concatenate([flat, jnp.zeros((pad,), F32)]).reshape(-1, 128)


def _ffn_fwd(tag, h, gain, w_in, w_out):
    t = h.shape[0]
    tm = min(TM, t)
    n = _rms_fwd(tag + "_norm", h, gain)
    tn = 256

    def epi(accs, _):
        gate, up = accs
        return gate, up, _silu(gate) * up

    gate, up, act = _mm(tag + "_in", [_a_spec(n, tm)], [_b_nn(w_in, tn), _b_nn(w_in, tn, DFF // tn)], [(0, 0), (0, 1)],
                        epi, [], [BF16, BF16, BF16], t, DFF, tm, tn)
    (out,) = _mm(tag + "_out", [_a_spec(act, tm)], [_b_nn(w_out, 512)], [(0, 0)],
                 lambda accs, ex: (ex[0] + 0.5 * accs[0],), [_e_tile(h, tm, 512)], [F32], t, D, tm, 512)
    return out, (n, gate, up, act)


def _ffn_bwd(tag, h, gain, w_in, w_out, saved, dout):
    t = h.shape[0]
    tm = min(TM, t)
    n, gate, up, act = saved
    tn = 256

    def epi(accs, ex):
        da = 0.5 * accs[0]
        g, u = ex[0].astype(F32), ex[1].astype(F32)
        return da * u * _dsilu(g), da * _silu(g)

    dgate, dup = _mm(tag + "_dact", [_a_spec(dout, tm)], [_b_nt(w_out, tn)], [(0, 0)], epi,
                     [_e_tile(gate, tm, tn), _e_tile(up, tm, tn)], [BF16, BF16], t, DFF, tm, tn, trans_b=True)
    (dn,) = _mm(tag + "_dn", [_a_spec(dgate, tm), _a_spec(dup, tm)],
                [_b_nt(w_in, 512, DFF, 0), _b_nt(w_in, 512, DFF, 1)], [(0, 0), (1, 1)],
                lambda accs, ex: (accs[0] + accs[1],), [], [F32], t, D, tm, 512, trans_b=True)
    dh, dgain = _rms_bwd(tag + "_dnorm", h, gain, dn, dout)
    dw_out = _mm_tn(tag + "_dw_out", act, dout, scale=0.5, tm=DFF // 2, tn=D)
    dw_g = _mm_tn(tag + "_dw_gate", n, dgate, tm=D, tn=DFF // 2)
    dw_u = _mm_tn(tag + "_dw_up", n, dup, tm=D, tn=DFF // 2)
    return dh, dgain, jnp.concatenate([dw_g, dw_u], axis=1), dw_out


def kernel(x, positions, ffn1_norm, ffn1_w_in, ffn1_w_out, mix_norm, w_in, hg_lb_table, hg_out_norm, w_hg_branch, mla_q_lora_norm, w_q_up, mla_kv_lora_norm, w_kv_up, q_head_norm, k_head_norm, w_mla_branch, w_merge, b_merge, w_out, ffn2_norm, ffn2_w_in, ffn2_w_out, final_norm, loss_target, m_ffn1_norm, m_ffn1_w_in, m_ffn1_w_out, m_mix_norm, m_w_in, m_hg_lb_table, m_hg_out_norm, m_w_hg_branch, m_mla_q_lora_norm, m_w_q_up, m_mla_kv_lora_norm, m_w_kv_up, m_q_head_norm, m_k_head_norm, m_w_mla_branch, m_w_merge, m_b_merge, m_w_out, m_ffn2_norm, m_ffn2_w_in, m_ffn2_w_out, m_final_norm, v_ffn1_norm, v_ffn1_w_in, v_ffn1_w_out, v_mix_norm, v_w_in, v_hg_lb_table, v_hg_out_norm, v_w_hg_branch, v_mla_q_lora_norm, v_w_q_up, v_mla_kv_lora_norm, v_w_kv_up, v_q_head_norm, v_k_head_norm, v_w_mla_branch, v_w_merge, v_b_merge, v_w_out, v_ffn2_norm, v_ffn2_w_in, v_ffn2_w_out, v_final_norm):
    a = dict(locals())
    w = {n: a[n] for n in WEIGHT_ORDER}
    mom = {n: a["m_" + n] for n in WEIGHT_ORDER}
    var = {n: a["v_" + n] for n in WEIGHT_ORDER}
    t = x.shape[1]
    tm = min(TM, t)
    xt = x.reshape(t, D)
    target = loss_target.reshape(t, D)
    pos = positions.reshape(t, 1)
    x_i, y_i, c_i = _place()
    c_arr = c_i.astype(jnp.int32).reshape(1)
    k_arr = (2 * x_i + y_i).astype(jnp.int32).reshape(1)

    shard = _pack([w[n][0] for n, _, _, _ in BIG], BF16)
    full = _unpack_full(_gather_weights(shard))
    w_in_full = full["w_in"]
    w_in_hg = w_in_full[:, :4 * D]
    w_in_mla = jnp.pad(w_in_full[:, 4 * D:], ((0, 0), (0, MLA_COLS - (4800 - 4 * D))))
    w_q_pad = jnp.pad(full["w_q_up"].reshape(Q_LORA, HEADS, QK), ((0, 0), (0, 0), (0, QKP - QK))).reshape(Q_LORA, HEADS * QKP)
    w_kv = full["w_kv_up"]
    gq = jnp.pad(w["q_head_norm"], ((0, 0), (0, QKP - QK)))
    gk = jnp.pad(w["k_head_norm"], ((0, 0), (0, QKP - QK)))

    h1, ffn1_saved = _ffn_fwd("ffn1", xt, w["ffn1_norm"], full["ffn1_w_in"], full["ffn1_w_out"])
    u = _rms_fwd("mix_norm", h1, w["mix_norm"])
    ident = lambda accs, ex: (accs[0],)
    (p_hg,) = _mm("in_hg", [_a_spec(u, tm)], [_b_nn(w_in_hg, 512)], [(0, 0)], ident, [], [F32], t, 4 * D, tm, 512)
    (p_mla,) = _mm("in_mla", [_a_spec(u, tm)], [_b_nn(w_in_mla, MLA_COLS)], [(0, 0)], ident, [], [F32], t, MLA_COLS, tm, MLA_COLS)
    o_raw, hg_o, states = _hgrn_fwd(p_hg, w["hg_lb_table"], w["hg_out_norm"])
    (y_hg,) = _mm("hg_branch", [_a_spec(hg_o, tm)], [_b_nn(full["w_hg_branch"], 512)], [(0, 0)], ident, [], [BF16], t, D, tm, 512)
    cqn, ckvn = _lora_norm_fwd(p_mla, w["mla_q_lora_norm"], w["mla_kv_lora_norm"])
    (qf,) = _mm("q_up", [_a_spec(cqn, tm)], [_b_nn(w_q_pad, 512)], [(0, 0)], ident, [], [F32], t, HEADS * QKP, tm, 512)
    (kvf,) = _mm("kv_up", [_a_spec(ckvn, tm)], [_b_nn(w_kv, 512)], [(0, 0)], ident, [], [F32], t, HEADS * QKP, tm, 512)
    cos, sin = _rope_tables(pos)
    qh, kh, vh = _mla_prep_fwd(qf, kvf, p_mla, cos, sin, gq, gk)
    o_mla, lse = _flash_fwd(qh, kh, vh)
    (y_mla,) = _mm("mla_branch", [_a_spec(o_mla, tm)], [_b_nn(full["w_mla_branch"], 512)], [(0, 0)], ident, [], [BF16], t, D, tm, 512)

    def merge_epi(accs, ex):
        g_hg = _sig(accs[0] + ex[2])
        g_mla = _sig(accs[1] + ex[3])
        return g_hg * ex[0].astype(F32) + g_mla * ex[1].astype(F32), g_hg, g_mla

    w_merge_f = full["w_merge"]
    mix, g_hg, g_mla = _mm(
        "merge", [_a_spec(u, tm)], [_b_nn(w_merge_f, 512), _b_nn(w_merge_f, 512, D // 512)], [(0, 0), (0, 1)], merge_epi,
        [_e_tile(y_hg, tm, 512), _e_tile(y_mla, tm, 512), _e_row(w["b_merge"], 512), _e_row(w["b_merge"], 512, D // 512)],
        [BF16, BF16, BF16], t, D, tm, 512)
    (h2,) = _mm("out_proj", [_a_spec(mix, tm)], [_b_nn(full["w_out"], 512)], [(0, 0)],
                lambda accs, ex: (ex[0] + accs[0],), [_e_tile(h1, tm, 512)], [F32], t, D, tm, 512)
    h3, ffn2_saved = _ffn_fwd("ffn2", h2, w["ffn2_norm"], full["ffn2_w_in"], full["ffn2_w_out"])
    dh3, d_final_norm, loss_part = _final_loss(h3, target, w["final_norm"])

    grads, small = {}, {}
    small["final_norm"] = d_final_norm
    dh2, small["ffn2_norm"], grads["ffn2_w_in"], grads["ffn2_w_out"] = _ffn_bwd(
        "ffn2", h2, w["ffn2_norm"], full["ffn2_w_in"], full["ffn2_w_out"], ffn2_saved, dh3)

    def dmix_epi(accs, ex):
        dm = accs[0]
        ghg, gml, yhg, yml = [e.astype(F32) for e in ex]
        return dm * ghg, dm * gml, dm * yhg * ghg * (1.0 - ghg), dm * yml * gml * (1.0 - gml)

    dy_hg, dy_mla, dpre_hg, dpre_mla = _mm(
        "d_mix", [_a_spec(dh2, tm)], [_b_nt(full["w_out"], 512)], [(0, 0)], dmix_epi,
        [_e_tile(g_hg, tm, 512), _e_tile(g_mla, tm, 512), _e_tile(y_hg, tm, 512), _e_tile(y_mla, tm, 512)],
        [BF16, BF16, BF16, BF16], t, D, tm, 512, trans_b=True)
    grads["w_out"] = _mm_tn("dw_out", mix, dh2)
    small["b_merge"] = jnp.concatenate([_colsum("db_hg", dpre_hg), _colsum("db_mla", dpre_mla)], axis=1)
    grads["w_merge"] = jnp.concatenate([_mm_tn("dw_merge_hg", u, dpre_hg), _mm_tn("dw_merge_mla", u, dpre_mla)], axis=1)
    grads["w_hg_branch"] = _mm_tn("dw_hg_branch", hg_o, dy_hg)
    grads["w_mla_branch"] = _mm_tn("dw_mla_branch", o_mla, dy_mla)
    (dho,) = _mm("d_hg_o", [_a_spec(dy_hg, tm)], [_b_nt(full["w_hg_branch"], 512)], [(0, 0)], ident, [], [BF16], t, D, tm, 512, trans_b=True)
    (do_mla,) = _mm("d_o_mla", [_a_spec(dy_mla, tm)], [_b_nt(full["w_mla_branch"], 512)], [(0, 0)], ident, [], [BF16], t, D, tm, 512, trans_b=True)

    dq_raw, df_raw, di_raw, dg_raw, small["hg_lb_table"], small["hg_out_norm"] = _hgrn_bwd(
        p_hg, w["hg_lb_table"], w["hg_out_norm"], o_raw, states, dho)
    dp_hg = [dq_raw, df_raw, di_raw, dg_raw]

    dqh, dkh, dvh = _flash_bwd(qh, kh, vh, lse, _attn_delta(do_mla, o_mla), do_mla)
    dqf, dkvf, dkpe, dgq, dgk = _mla_prep_bwd(qf, kvf, p_mla, cos, sin, gq, gk, dqh, dkh, dvh)
    small["q_head_norm"] = dgq[:, :QK]
    small["k_head_norm"] = dgk[:, :QK]
    dwq_pad = _mm_tn("dw_q_up", cqn, dqf, tm=Q_LORA, tn=1024)
    grads["w_q_up"] = dwq_pad.reshape(Q_LORA, HEADS, QKP)[:, :, :QK].reshape(Q_LORA, HEADS * QK)
    grads["w_kv_up"] = _mm_tn("dw_kv_up", ckvn, dkvf, tm=KV_LORA, tn=1024)
    (dcqn,) = _mm("d_cq", [_a_spec(dqf, tm)], [_b_nt(w_q_pad, Q_LORA)], [(0, 0)], ident, [], [F32], t, Q_LORA, tm, Q_LORA, trans_b=True)
    (dckvn,) = _mm("d_ckv", [_a_spec(dkvf, tm)], [_b_nt(w_kv, KV_LORA)], [(0, 0)], ident, [], [F32], t, KV_LORA, tm, KV_LORA, trans_b=True)
    dp_mla, small["mla_q_lora_norm"], small["mla_kv_lora_norm"] = _lora_norm_bwd(
        p_mla, w["mla_q_lora_norm"], w["mla_kv_lora_norm"], dcqn, dckvn, dkpe)

    dw_in_hg = [_mm_tn("dw_in_hg%d" % k, u, dp_hg[k]) for k in range(4)]
    dw_in_mla = _mm_tn("dw_in_mla", u, dp_mla, tn=MLA_COLS)
    grads["w_in"] = jnp.concatenate(dw_in_hg + [dw_in_mla[:, :4800 - 4 * D]], axis=1)
    (du,) = _mm(
        "d_u",
        [_a_spec(dpre_hg, tm), _a_spec(dpre_mla, tm)] + [_a_spec(d, tm) for d in dp_hg] + [_a_spec(dp_mla, tm)],
        [_b_nt(w_merge_f, 512, D, 0), _b_nt(w_merge_f, 512, D, 1)]
        + [_b_nt(w_in_hg, 512, D, k) for k in range(4)] + [_b_nt(w_in_mla, 512)],
        [(k, k) for k in range(7)],
        lambda accs, ex: (functools.reduce(lambda p, q: p + q, accs),), [], [F32], t, D, tm, 512, trans_b=True)
    dh1, small["mix_norm"] = _rms_bwd("mix_dnorm", h1, w["mix_norm"], du, dh2)
    dx, small["ffn1_norm"], grads["ffn1_w_in"], grads["ffn1_w_out"] = _ffn_bwd(
        "ffn1", xt, w["ffn1_norm"], full["ffn1_w_in"], full["ffn1_w_out"], ffn1_saved, dh1)

    gp = _pack_grads(grads)
    chip_sums = _chip_sum(gp, _swap_halves(gp), c_arr)
    my_half = _shard_sum(chip_sums, _scatter_chip_sums(chip_sums), k_arr)
    g_shard = _unpack_shard(_join_halves(my_half))
    small_sum = _all_reduce_small(_pack_small([small[n] for n, _ in SMALL] + [loss_part])).reshape(-1)
    g_small, at = {}, 0
    for n, shape in SMALL:
        size = shape[0] * shape[1]
        g_small[n] = small_sum[at:at + size].reshape(shape)
        at += size
    loss = small_sum[at]

    g_out, d_out, m_out, v_out = {}, {}, {}, {}
    for n in WEIGHT_ORDER:
        shape = w[n].shape
        g = g_shard[n] if n in g_shard else g_small[n]
        two = g.shape
        d_, m_, v_ = _adamw("adamw_" + n, w[n].reshape(two), g, mom[n].reshape(two), var[n].reshape(two))
        g_out[n], d_out[n], m_out[n], v_out[n] = g.reshape(shape), d_.reshape(shape), m_.reshape(shape), v_.reshape(shape)

    return (loss, dx.reshape(x.shape), *[g_out[n] for n in WEIGHT_ORDER], *[d_out[n] for n in WEIGHT_ORDER],
            *[m_out[n] for n in WEIGHT_ORDER], *[v_out[n] for n in WEIGHT_ORDER])
```

```python
import functools

import numpy as np
import jax
import jax.numpy as jnp
from jax import lax
from jax.experimental import pallas as pl
from jax.experimental.pallas import tpu as pltpu

F32 = jnp.float32
BF16 = jnp.bfloat16
MESH = pl.DeviceIdType.MESH

D = 1024
DFF = 2816
HEADS = 8
HK = 128
CHUNK = 64
ROPE = 64
QK = 192
QKP = 256
Q_LORA = 384
KV_LORA = 256
MLA_COLS = 768
EPS = 1e-6
ROPE_THETA = 10000.0
SCALE = QK ** -0.5
LOG2E = 1.4426950408889634
LN2 = 0.6931471805599453
NEG = -1e30
EXP_CLAMP = 80.0

ADAM_LR = 0.001
ADAM_B1 = 0.9
ADAM_B2 = 0.999
ADAM_EPS = 1e-08
ADAM_WD = 0.01
ADAM_STEP = 10

PACK_W = 1024
ADD_ROWS = 352
PACK_ALIGN = 2 * ADD_ROWS

TM = 512
TQ = 1024
SUBQ = 512
HG_BT = 512
HG_HPB = 4
TT = 512
ROW_TM = 256

VMEM_MB = 48

BIG = (
    ("ffn1_w_in", D, 2 * DFF, 1),
    ("ffn1_w_out", DFF, D, 0),
    ("w_in", D, 4800, 1),
    ("w_hg_branch", D, D, 0),
    ("w_q_up", Q_LORA, HEADS * QK, 1),
    ("w_kv_up", KV_LORA, HEADS * 2 * HK, 1),
    ("w_mla_branch", D, D, 0),
    ("w_merge", D, 2 * D, 1),
    ("w_out", D, D, 0),
    ("ffn2_w_in", D, 2 * DFF, 1),
    ("ffn2_w_out", DFF, D, 0),
)
SMALL = (
    ("ffn1_norm", (1, D)),
    ("mix_norm", (1, D)),
    ("hg_lb_table", (2, D)),
    ("hg_out_norm", (1, HK)),
    ("mla_q_lora_norm", (1, Q_LORA)),
    ("mla_kv_lora_norm", (1, KV_LORA)),
    ("q_head_norm", (1, QK)),
    ("k_head_norm", (1, QK)),
    ("b_merge", (1, 2 * D)),
    ("ffn2_norm", (1, D)),
    ("final_norm", (1, D)),
)
WEIGHT_ORDER = ("ffn1_norm", "ffn1_w_in", "ffn1_w_out", "mix_norm", "w_in", "hg_lb_table", "hg_out_norm",
                "w_hg_branch", "mla_q_lora_norm", "w_q_up", "mla_kv_lora_norm", "w_kv_up", "q_head_norm",
                "k_head_norm", "w_mla_branch", "w_merge", "b_merge", "w_out", "ffn2_norm", "ffn2_w_in",
                "ffn2_w_out", "final_norm")


def _call(body, **kw):
    return pl.pallas_call(body, **kw)


def _cp(vmem_mb=VMEM_MB):
    return pltpu.CompilerParams(vmem_limit_bytes=vmem_mb << 20)


def _dot(a, b):
    return lax.dot_general(a, b, (((1,), (0,)), ((), ())), preferred_element_type=F32)


def _dot_nt(a, b):
    return lax.dot_general(a, b, (((1,), (1,)), ((), ())), preferred_element_type=F32)


def _dot_tn(a, b):
    return lax.dot_general(a, b, (((0,), (0,)), ((), ())), preferred_element_type=F32)


def _sig(x):
    return jax.nn.sigmoid(x)


def _silu(x):
    return x * _sig(x)


def _dsilu(x):
    s = _sig(x)
    return s * (1.0 + x * (1.0 - s))


def _a_spec(arr, tm, kblk=None, kidx=0):
    kb = arr.shape[1] if kblk is None else kblk
    return arr, pl.BlockSpec((tm, kb), lambda i, j, kidx=kidx: (i, kidx))


def _b_nn(arr, tn, off=0):
    return arr, pl.BlockSpec((arr.shape[0], tn), lambda i, j, off=off: (0, j + off))


def _b_nt(arr, tn, kblk=None, kidx=0):
    kb = arr.shape[1] if kblk is None else kblk
    return arr, pl.BlockSpec((tn, kb), lambda i, j, kidx=kidx: (j, kidx))


def _e_tile(arr, tm, tn, off=0):
    return arr, pl.BlockSpec((tm, tn), lambda i, j, off=off: (i, j + off))


def _e_row(arr, tn, off=0):
    return arr, pl.BlockSpec((1, tn), lambda i, j, off=off: (0, j + off))


def _mm(name, As, Bs, dots, epi, extras, out_dtypes, m, n, tm, tn, trans_b=False, side=None):
    na, nb, ne, no = len(As), len(Bs), len(extras), len(out_dtypes)
    ni, nj = m // tm, n // tn
    s_in = len(side.inputs) if side else 0
    s_out = len(side.out_shapes) if side else 0

    def body(*refs):
        a_refs = refs[:na]
        b_refs = refs[na:na + nb]
        e_refs = refs[na + nb:na + nb + ne]
        at = na + nb + ne
        side_refs = refs[at:at + s_in]
        o_refs = refs[at + s_in:at + s_in + no]
        side_refs = list(side_refs) + list(refs[at + s_in + no:])
        if side:
            i, j = pl.program_id(0), pl.program_id(1)

            @pl.when(jnp.logical_and(i == 0, j == 0))
            def _():
                side.start(*side_refs)

        a_vals = [r[...].astype(BF16) for r in a_refs]
        accs = []
        for ai, bi in dots:
            b = b_refs[bi][...]
            accs.append(_dot_nt(a_vals[ai], b) if trans_b else _dot(a_vals[ai], b))
        outs = epi(accs, [r[...] for r in e_refs])
        for o_ref, o in zip(o_refs, outs):
            o_ref[...] = o.astype(o_ref.dtype)
        if side:
            @pl.when(jnp.logical_and(i == ni - 1, j == nj - 1))
            def _():
                side.finish(*side_refs)

    ops = list(As) + list(Bs) + list(extras)
    anywhere = pl.BlockSpec(memory_space=pl.ANY)
    res = _call(
        body, name=name,
        grid=(ni, nj),
        in_specs=[s for _, s in ops] + [anywhere] * s_in,
        out_specs=[pl.BlockSpec((tm, tn), lambda i, j: (i, j)) for _ in out_dtypes] + [anywhere] * s_out,
        out_shape=[jax.ShapeDtypeStruct((m, n), dt) for dt in out_dtypes] + (list(side.out_shapes) if side else []),
        scratch_shapes=list(side.scratch) if side else [],
        compiler_params=_cp(),
    )(*[a for a, _ in ops], *(side.inputs if side else []))
    return res


def _mm_tn(name, a, b, scale=1.0, tm=1024, tn=1024):
    t, m = a.shape
    n = b.shape[1]
    tm, tn, tt = min(tm, m), min(tn, n), min(TT, t)
    nk = t // tt

    def body(a_ref, b_ref, o_ref):
        k = pl.program_id(2)

        @pl.when(k == 0)
        def _():
            o_ref[...] = jnp.zeros_like(o_ref)

        o_ref[...] += _dot_tn(a_ref[...].astype(BF16), b_ref[...].astype(BF16))
        if scale != 1.0:
            @pl.when(k == nk - 1)
            def _():
                o_ref[...] = o_ref[...] * scale

    return _call(
        body, name=name,
        grid=(m // tm, n // tn, nk),
        in_specs=[pl.BlockSpec((tt, tm), lambda i, j, k: (k, i)), pl.BlockSpec((tt, tn), lambda i, j, k: (k, j))],
        out_specs=pl.BlockSpec((tm, tn), lambda i, j, k: (i, j)),
        out_shape=jax.ShapeDtypeStruct((m, n), F32),
        compiler_params=_cp(),
    )(a, b)


def _rms_fwd(name, x, gain):
    t, d = x.shape
    tm = min(ROW_TM, t)

    def body(x_ref, g_ref, o_ref):
        xv = x_ref[...]
        r = lax.rsqrt(jnp.mean(xv * xv, axis=-1, keepdims=True) + EPS)
        o_ref[...] = (xv * r * g_ref[...]).astype(o_ref.dtype)

    return _call(
        body, name=name, grid=(t // tm,),
        in_specs=[pl.BlockSpec((tm, d), lambda i: (i, 0)), pl.BlockSpec((1, d), lambda i: (0, 0))],
        out_specs=pl.BlockSpec((tm, d), lambda i: (i, 0)),
        out_shape=jax.ShapeDtypeStruct((t, d), BF16),
        compiler_params=_cp(),
    )(x, gain)


def _rms_bwd_vals(xv, g, dn):
    r = lax.rsqrt(jnp.mean(xv * xv, axis=-1, keepdims=True) + EPS)
    xh = xv * r
    dxh = dn * g
    c = jnp.mean(dxh * xh, axis=-1, keepdims=True)
    return r * (dxh - xh * c), dn * xh


def _rms_bwd(name, x, gain, dn, dres):
    t, d = x.shape
    tm = min(ROW_TM, t)

    def body(x_ref, g_ref, dn_ref, dr_ref, dx_ref, dg_ref):
        @pl.when(pl.program_id(0) == 0)
        def _():
            dg_ref[...] = jnp.zeros_like(dg_ref)

        dx, dg = _rms_bwd_vals(x_ref[...], g_ref[...], dn_ref[...].astype(F32))
        dx_ref[...] = dr_ref[...] + dx
        dg_ref[...] += jnp.sum(dg, axis=0, keepdims=True)

    row = pl.BlockSpec((tm, d), lambda i: (i, 0))
    one = pl.BlockSpec((1, d), lambda i: (0, 0))
    return _call(
        body, name=name, grid=(t // tm,),
        in_specs=[row, one, row, row],
        out_specs=[row, one],
        out_shape=[jax.ShapeDtypeStruct((t, d), F32), jax.ShapeDtypeStruct((1, d), F32)],
        compiler_params=_cp(),
    )(x, gain, dn, dres)


def _final_loss(h, target, gain):
    t, d = h.shape
    tm = min(ROW_TM, t)

    def body(h_ref, t_ref, g_ref, dh_ref, dg_ref, l_ref):
        @pl.when(pl.program_id(0) == 0)
        def _():
            dg_ref[...] = jnp.zeros_like(dg_ref)
            l_ref[...] = jnp.zeros_like(l_ref)

        hv = h_ref[...]
        g = g_ref[...]
        r = lax.rsqrt(jnp.mean(hv * hv, axis=-1, keepdims=True) + EPS)
        xh = hv * r
        err = xh * g - t_ref[...]
        l_ref[...] += 0.5 * jnp.sum(jnp.mean(err * err, axis=-1, keepdims=True), axis=0, keepdims=True)
        dy = err * (1.0 / d)
        dxh = dy * g
        c = jnp.mean(dxh * xh, axis=-1, keepdims=True)
        dh_ref[...] = r * (dxh - xh * c)
        dg_ref[...] += jnp.sum(dy * xh, axis=0, keepdims=True)

    row = pl.BlockSpec((tm, d), lambda i: (i, 0))
    one = pl.BlockSpec((1, d), lambda i: (0, 0))
    return _call(
        body, name="final_loss", grid=(t // tm,),
        in_specs=[row, row, one],
        out_specs=[row, one, pl.BlockSpec((1, 128), lambda i: (0, 0))],
        out_shape=[jax.ShapeDtypeStruct((t, d), F32), jax.ShapeDtypeStruct((1, d), F32),
                   jax.ShapeDtypeStruct((1, 128), F32)],
        compiler_params=_cp(),
    )(h, target, gain)


def _colsum(name, x):
    t, n = x.shape
    tm = min(TM, t)

    def body(x_ref, o_ref):
        @pl.when(pl.program_id(0) == 0)
        def _():
            o_ref[...] = jnp.zeros_like(o_ref)

        o_ref[...] += jnp.sum(x_ref[...].astype(F32), axis=0, keepdims=True)

    return _call(
        body, name=name, grid=(t // tm,),
        in_specs=[pl.BlockSpec((tm, n), lambda i: (i, 0))],
        out_specs=pl.BlockSpec((1, n), lambda i: (0, 0)),
        out_shape=jax.ShapeDtypeStruct((1, n), F32),
        compiler_params=_cp(),
    )(x)


def _lora_norm_fwd(p_mla, gq, gkv):
    t = p_mla.shape[0]
    tm = min(ROW_TM, t)

    def body(p_ref, gq_ref, gkv_ref, q_ref, kv_ref):
        cq = p_ref[:, 0:Q_LORA]
        ckv = p_ref[:, Q_LORA:Q_LORA + KV_LORA]
        rq = lax.rsqrt(jnp.mean(cq * cq, axis=-1, keepdims=True) + EPS)
        rkv = lax.rsqrt(jnp.mean(ckv * ckv, axis=-1, keepdims=True) + EPS)
        q_ref[...] = (cq * rq * gq_ref[...]).astype(BF16)
        kv_ref[...] = (ckv * rkv * gkv_ref[...]).astype(BF16)

    return _call(
        body, name="lora_norm_fwd", grid=(t // tm,),
        in_specs=[pl.BlockSpec((tm, MLA_COLS), lambda i: (i, 0)),
                  pl.BlockSpec((1, Q_LORA), lambda i: (0, 0)), pl.BlockSpec((1, KV_LORA), lambda i: (0, 0))],
        out_specs=[pl.BlockSpec((tm, Q_LORA), lambda i: (i, 0)), pl.BlockSpec((tm, KV_LORA), lambda i: (i, 0))],
        out_shape=[jax.ShapeDtypeStruct((t, Q_LORA), BF16), jax.ShapeDtypeStruct((t, KV_LORA), BF16)],
        compiler_params=_cp(),
    )(p_mla, gq, gkv)


def _lora_norm_bwd(p_mla, gq, gkv, dcqn, dckvn, dkpe):
    t = p_mla.shape[0]
    tm = min(ROW_TM, t)

    def body(p_ref, gq_ref, gkv_ref, dq_ref, dkv_ref, dkpe_ref, dp_ref, dgq_ref, dgkv_ref):
        @pl.when(pl.program_id(0) == 0)
        def _():
            dgq_ref[...] = jnp.zeros_like(dgq_ref)
            dgkv_ref[...] = jnp.zeros_like(dgkv_ref)

        dcq, dgq = _rms_bwd_vals(p_ref[:, 0:Q_LORA], gq_ref[...], dq_ref[...])
        dckv, dgkv = _rms_bwd_vals(p_ref[:, Q_LORA:Q_LORA + KV_LORA], gkv_ref[...], dkv_ref[...])
        dp_ref[:, 0:Q_LORA] = dcq.astype(BF16)
        dp_ref[:, Q_LORA:Q_LORA + KV_LORA] = dckv.astype(BF16)
        dp_ref[:, Q_LORA + KV_LORA:MLA_COLS] = dkpe_ref[...].astype(BF16)
        dgq_ref[...] += jnp.sum(dgq, axis=0, keepdims=True)
        dgkv_ref[...] += jnp.sum(dgkv, axis=0, keepdims=True)

    return _call(
        body, name="lora_norm_bwd", grid=(t // tm,),
        in_specs=[pl.BlockSpec((tm, MLA_COLS), lambda i: (i, 0)),
                  pl.BlockSpec((1, Q_LORA), lambda i: (0, 0)), pl.BlockSpec((1, KV_LORA), lambda i: (0, 0)),
                  pl.BlockSpec((tm, Q_LORA), lambda i: (i, 0)), pl.BlockSpec((tm, KV_LORA), lambda i: (i, 0)),
                  pl.BlockSpec((tm, HK), lambda i: (i, 0))],
        out_specs=[pl.BlockSpec((tm, MLA_COLS), lambda i: (i, 0)),
                   pl.BlockSpec((1, Q_LORA), lambda i: (0, 0)), pl.BlockSpec((1, KV_LORA), lambda i: (0, 0))],
        out_shape=[jax.ShapeDtypeStruct((t, MLA_COLS), BF16), jax.ShapeDtypeStruct((1, Q_LORA), F32),
                   jax.ShapeDtypeStruct((1, KV_LORA), F32)],
        compiler_params=_cp(),
    )(p_mla, gq, gkv, dcqn, dckvn, dkpe)


def _cumsum_rows(x, row):
    for s in (1, 2, 4, 8, 16, 32):
        x = x + jnp.where(row >= s, pltpu.roll(x, s, 0), 0.0)
    return x


def _rcumsum_rows(x, row):
    for s in (1, 2, 4, 8, 16, 32):
        x = x + jnp.where(row < CHUNK - s, pltpu.roll(x, CHUNK - s, 0), 0.0)
    return x


def _hg_gates(qr, z, lb, row):
    q = _silu(qr)
    sg = _sig(z)
    f = lb + (1.0 - lb) * sg
    lf = jnp.log(f)
    k = (1.0 - lb) * (1.0 - sg)
    cum = _cumsum_rows(lf, row)
    mid = jnp.sum(jnp.where(row < CHUNK // 2, lf, 0.0), axis=0, keepdims=True)
    last = jnp.sum(lf, axis=0, keepdims=True)
    e_q = jnp.exp(jnp.minimum(cum - mid, EXP_CLAMP))
    e_k = jnp.exp(jnp.minimum(mid - cum, EXP_CLAMP))
    e_a = jnp.exp(cum)
    e_l = jnp.exp(last - cum)
    return q, sg, f, k, last, e_q, e_k, e_a, e_l


def _hgrn_fwd(p_hg, tab, gain):
    t = p_hg.shape[0]
    bt = min(HG_BT, t)
    nb, nc = t // bt, bt // CHUNK

    hpb = HG_HPB
    wide = hpb * HK

    def body(q_ref, f_ref, i_ref, g_ref, tab_ref, gain_ref, o_ref, ho_ref, st_ref, state):
        @pl.when(pl.program_id(1) == 0)
        def _():
            state[...] = jnp.zeros_like(state)

        row = lax.broadcasted_iota(jnp.int32, (CHUNK, HK), 0)
        tril = lax.broadcasted_iota(jnp.int32, (CHUNK, CHUNK), 0) >= lax.broadcasted_iota(jnp.int32, (CHUNK, CHUNK), 1)
        gain_v = gain_ref[...]

        def chunk(c, carry):
            sl = pl.ds(pl.multiple_of(c * CHUNK, CHUNK), CHUNK)
            for hh in range(hpb):
                ln = slice(hh * HK, (hh + 1) * HK)
                lb = _sig(tab_ref[0:1, ln] - tab_ref[1:2, ln])
                v = i_ref[sl, ln].astype(BF16)
                q, _, _, k, last, e_q, e_k, e_a, e_l = _hg_gates(q_ref[sl, ln], f_ref[sl, ln], lb, row)
                st = state[hh]
                st_ref[hh, c] = st
                p = jnp.where(tril, _dot_nt((q * e_q).astype(BF16), (k * e_k).astype(BF16)), 0.0)
                o = _dot(p.astype(BF16), v) + _dot_nt((q * e_a).astype(BF16), st.astype(BF16))
                state[hh] = jnp.exp(last) * st + _dot_tn(v, (k * e_l).astype(BF16))
                o_ref[sl, ln] = o
                r = lax.rsqrt(jnp.mean(o * o, axis=-1, keepdims=True) + EPS)
                ho_ref[sl, ln] = (o * r * gain_v * _silu(g_ref[sl, ln])).astype(BF16)
            return carry

        lax.fori_loop(0, nc, chunk, 0)

    def col(k):
        return pl.BlockSpec((bt, wide), lambda h, j, k=k: (j, k * (HEADS // hpb) + h))

    return _call(
        body, name="hgrn_fwd", grid=(HEADS // hpb, nb),
        in_specs=[col(0), col(1), col(2), col(3),
                  pl.BlockSpec((2, wide), lambda h, j: (0, h)), pl.BlockSpec((1, HK), lambda h, j: (0, 0))],
        out_specs=[pl.BlockSpec((bt, wide), lambda h, j: (j, h)), pl.BlockSpec((bt, wide), lambda h, j: (j, h)),
                   pl.BlockSpec((hpb, nc, HK, HK), lambda h, j: (h, j, 0, 0))],
        out_shape=[jax.ShapeDtypeStruct((t, D), F32), jax.ShapeDtypeStruct((t, D), BF16),
                   jax.ShapeDtypeStruct((HEADS, t // CHUNK, HK, HK), F32)],
        scratch_shapes=[pltpu.VMEM((hpb, HK, HK), F32)],
        compiler_params=_cp(),
    )(p_hg, p_hg, p_hg, p_hg, tab, gain)


def _hgrn_bwd(p_hg, tab, gain, o_raw, states, dho):
    t = p_hg.shape[0]
    bt = min(HG_BT, t)
    nb, nc = t // bt, bt // CHUNK
    hpb = HG_HPB
    wide = hpb * HK

    def body(q_ref, f_ref, i_ref, g_ref, tab_ref, gain_ref, o_ref, st_ref, dho_ref,
             dq_ref, df_ref, di_ref, dg_ref, dtab_ref, dgain_ref, dstate, dlb):
        h, j = pl.program_id(0), pl.program_id(1)

        @pl.when(jnp.logical_and(h == 0, j == 0))
        def _():
            dgain_ref[...] = jnp.zeros_like(dgain_ref)

        @pl.when(j == 0)
        def _():
            dstate[...] = jnp.zeros_like(dstate)
            dlb[...] = jnp.zeros_like(dlb)

        row = lax.broadcasted_iota(jnp.int32, (CHUNK, HK), 0)
        tril = lax.broadcasted_iota(jnp.int32, (CHUNK, CHUNK), 0) >= lax.broadcasted_iota(jnp.int32, (CHUNK, CHUNK), 1)
        gain_v = gain_ref[...]

        def chunk(cc, carry):
            c = nc - 1 - cc
            sl = pl.ds(pl.multiple_of(c * CHUNK, CHUNK), CHUNK)
            dgain = jnp.zeros((1, HK), F32)
            for hh in range(hpb):
                ln = slice(hh * HK, (hh + 1) * HK)
                lb = _sig(tab_ref[0:1, ln] - tab_ref[1:2, ln])
                qr = q_ref[sl, ln]
                v = i_ref[sl, ln].astype(BF16)
                gr = g_ref[sl, ln]
                q, sg, f, k, last, e_q, e_k, e_a, e_l = _hg_gates(qr, f_ref[sl, ln], lb, row)
                o = o_ref[sl, ln]
                r = lax.rsqrt(jnp.mean(o * o, axis=-1, keepdims=True) + EPS)
                oh = o * r
                dh = dho_ref[sl, ln].astype(F32)
                dnorm = dh * _silu(gr)
                dg_ref[sl, ln] = (dh * oh * gain_v * _dsilu(gr)).astype(BF16)
                dgain = dgain + jnp.sum(dnorm * oh, axis=0, keepdims=True)
                dxh = dnorm * gain_v
                do = (r * (dxh - oh * jnp.mean(dxh * oh, axis=-1, keepdims=True))).astype(BF16)
                st0 = st_ref[hh, c]
                st0_b = st0.astype(BF16)
                ds1 = dstate[hh]
                ds1_b = ds1.astype(BF16)
                qt = (q * e_q).astype(BF16)
                kt = (k * e_k).astype(BF16)
                qd = (q * e_a).astype(BF16)
                kd = (k * e_l).astype(BF16)
                p = jnp.where(tril, _dot_nt(qt, kt), 0.0).astype(BF16)
                dp = jnp.where(tril, _dot_nt(do, v), 0.0).astype(BF16)
                dv = _dot_tn(p, do) + _dot_nt(kd, ds1_b)
                dqt = _dot(dp, kt)
                dkt = _dot_tn(dp, qt)
                dq_inter = _dot(do, st0_b) * e_a
                dk_inter = _dot(v, ds1_b) * e_l
                dq = dqt * e_q + dq_inter
                dk = dkt * e_k + dk_inter
                e_last = jnp.exp(last)
                dstate[hh] = _dot_tn(do, qd) + e_last * ds1
                dlast = (jnp.sum(k * dk_inter, axis=0, keepdims=True)
                         + e_last * jnp.sum(ds1 * st0, axis=0, keepdims=True))
                da = (qt.astype(F32) * dqt - kt.astype(F32) * dkt + q * dq_inter - k * dk_inter
                      + jnp.where(row == CHUNK - 1, dlast, 0.0))
                dlf = _rcumsum_rows(da, row)
                dfv = dlf / f - dk
                df_ref[sl, ln] = (dfv * (1.0 - lb) * sg * (1.0 - sg)).astype(BF16)
                dlb[:, ln] += jnp.sum(dfv * (1.0 - sg), axis=0, keepdims=True)
                dq_ref[sl, ln] = (dq * _dsilu(qr)).astype(BF16)
                di_ref[sl, ln] = dv.astype(BF16)
            dgain_ref[...] += dgain
            return carry

        lax.fori_loop(0, nc, chunk, 0)

        @pl.when(j == nb - 1)
        def _():
            lb = _sig(tab_ref[0:1, :] - tab_ref[1:2, :])
            d0 = dlb[...] * lb * (1.0 - lb)
            dtab_ref[0:1, :] = d0
            dtab_ref[1:2, :] = -d0

    def col(k):
        return pl.BlockSpec((bt, wide), lambda h, j, k=k: (nb - 1 - j, k * (HEADS // hpb) + h))

    tok = pl.BlockSpec((bt, wide), lambda h, j: (nb - 1 - j, h))
    return _call(
        body, name="hgrn_bwd", grid=(HEADS // hpb, nb),
        in_specs=[col(0), col(1), col(2), col(3),
                  pl.BlockSpec((2, wide), lambda h, j: (0, h)), pl.BlockSpec((1, HK), lambda h, j: (0, 0)),
                  tok, pl.BlockSpec((hpb, nc, HK, HK), lambda h, j: (h, nb - 1 - j, 0, 0)), tok],
        out_specs=[tok, tok, tok, tok,
                   pl.BlockSpec((2, wide), lambda h, j: (0, h)), pl.BlockSpec((1, HK), lambda h, j: (0, 0))],
        out_shape=[jax.ShapeDtypeStruct((t, D), BF16)] * 4
        + [jax.ShapeDtypeStruct((2, D), F32), jax.ShapeDtypeStruct((1, HK), F32)],
        scratch_shapes=[pltpu.VMEM((hpb, HK, HK), F32), pltpu.VMEM((1, wide), F32)],
        compiler_params=_cp(),
    )(p_hg, p_hg, p_hg, p_hg, tab, gain, o_raw, states, dho)


def _rope_tables(pos):
    t = pos.shape[0]
    tm = min(ROW_TM, t)
    inv = np.zeros((1, HK), np.float32)
    freq = (ROPE_THETA ** (-np.arange(0, ROPE, 2, dtype=np.float32) / ROPE)).astype(np.float32)
    inv[0, 0:ROPE // 2] = freq
    inv[0, ROPE // 2:ROPE] = freq
    sign = np.zeros((1, HK), np.float32)
    sign[0, 0:ROPE // 2] = -1.0
    sign[0, ROPE // 2:ROPE] = 1.0

    def body(pos_ref, inv_ref, sign_ref, cos_ref, sin_ref):
        ang = pos_ref[...].astype(F32) * inv_ref[...]
        cos_ref[...] = jnp.cos(ang)
        sin_ref[...] = jnp.sin(ang) * sign_ref[...]

    one = pl.BlockSpec((1, HK), lambda i: (0, 0))
    row = pl.BlockSpec((tm, HK), lambda i: (i, 0))
    return _call(
        body, name="rope_tables", grid=(t // tm,),
        in_specs=[pl.BlockSpec((tm, 1), lambda i: (i, 0)), one, one],
        out_specs=[row, row],
        out_shape=[jax.ShapeDtypeStruct((t, HK), F32)] * 2,
        compiler_params=_cp(),
    )(pos, jnp.asarray(inv), jnp.asarray(sign))


def _rope(x, cos, sin_signed):
    r = lax.broadcasted_iota(jnp.int32, (HK, HK), 0)
    c = lax.broadcasted_iota(jnp.int32, (HK, HK), 1)
    half = ROPE // 2
    swap = jnp.logical_or(jnp.logical_and(c < half, r == c + half),
                          jnp.logical_and(jnp.logical_and(c >= half, c < ROPE), r == c - half))
    return x * cos + _dot_split(x, swap.astype(BF16)) * sin_signed


def _dot_split(x, m):
    hi = x.astype(BF16)
    lo = (x - hi.astype(F32)).astype(BF16)
    return _dot(hi, m) + _dot(lo, m)


def _lane_sum(x):
    return _dot_split(x, jnp.ones((HK, HK), BF16))


def _head_norm(xn, xr):
    r = lax.rsqrt(_lane_sum(xn * xn + xr * xr) * (1.0 / QK) + EPS)
    return xn * r, xr * r, r


def _head_norm_bwd(xn, xr, g_n, g_r, dn, dr):
    hn, hr, r = _head_norm(xn, xr)
    dxn, dxr = dn * g_n, dr * g_r
    c = _lane_sum(dxn * hn + dxr * hr) * (1.0 / QK)
    return r * (dxn - hn * c), r * (dxr - hr * c), dn * hn, dr * hr


def _mla_prep_fwd(qf, kv, p_mla, cos, sin, gq, gk):
    t = qf.shape[0]
    tm = min(ROW_TM, t)

    def body(qf_ref, kv_ref, kpe_ref, cos_ref, sin_ref, gq_ref, gk_ref, q_ref, k_ref, v_ref):
        cos_v, sin_v = cos_ref[...], sin_ref[...]
        kpe = kpe_ref[...]
        for h in range(HEADS):
            lo, mid, hi = h * QKP, h * QKP + HK, (h + 1) * QKP
            qn, qr, _ = _head_norm(qf_ref[:, lo:mid], qf_ref[:, mid:hi])
            q_ref[h, :, 0:HK] = (qn * gq_ref[:, 0:HK] * (SCALE * LOG2E)).astype(BF16)
            q_ref[h, :, HK:QKP] = (_rope(qr * gq_ref[:, HK:QKP], cos_v, sin_v) * (SCALE * LOG2E)).astype(BF16)
            kn, kr, _ = _head_norm(kv_ref[:, lo:mid], kpe)
            k_ref[h, :, 0:HK] = (kn * gk_ref[:, 0:HK]).astype(BF16)
            k_ref[h, :, HK:QKP] = _rope(kr * gk_ref[:, HK:QKP], cos_v, sin_v).astype(BF16)
            v_ref[h] = kv_ref[:, mid:hi].astype(BF16)

    head = pl.BlockSpec((tm, HEADS * QKP), lambda i: (i, 0))
    tok = pl.BlockSpec((tm, HK), lambda i: (i, 0))
    gain = pl.BlockSpec((1, QKP), lambda i: (0, 0))
    return _call(
        body, name="mla_prep_fwd", grid=(t // tm,),
        in_specs=[head, head, pl.BlockSpec((tm, HK), lambda i: (i, MLA_COLS // HK - 1)), tok, tok, gain, gain],
        out_specs=[pl.BlockSpec((HEADS, tm, QKP), lambda i: (0, i, 0)),
                   pl.BlockSpec((HEADS, tm, QKP), lambda i: (0, i, 0)),
                   pl.BlockSpec((HEADS, tm, HK), lambda i: (0, i, 0))],
        out_shape=[jax.ShapeDtypeStruct((HEADS, t, QKP), BF16), jax.ShapeDtypeStruct((HEADS, t, QKP), BF16),
                   jax.ShapeDtypeStruct((HEADS, t, HK), BF16)],
        compiler_params=_cp(),
    )(qf, kv, p_mla, cos, sin, gq, gk)


def _mla_prep_bwd(qf, kv, p_mla, cos, sin, gq, gk, dq, dk, dv):
    t = qf.shape[0]
    tm = min(ROW_TM, t)

    def body(qf_ref, kv_ref, kpe_ref, cos_ref, sin_ref, gq_ref, gk_ref, dq_ref, dk_ref, dv_ref,
             dqf_ref, dkv_ref, dkpe_ref, dgq_ref, dgk_ref):
        @pl.when(pl.program_id(0) == 0)
        def _():
            dgq_ref[...] = jnp.zeros_like(dgq_ref)
            dgk_ref[...] = jnp.zeros_like(dgk_ref)

        cos_v, sin_v = cos_ref[...], -sin_ref[...]
        kpe = kpe_ref[...]
        gqn, gqr, gkn, gkr = gq_ref[:, 0:HK], gq_ref[:, HK:QKP], gk_ref[:, 0:HK], gk_ref[:, HK:QKP]
        dkpe = jnp.zeros((tm, HK), F32)
        dgq_n, dgq_r, dgk_n, dgk_r = [jnp.zeros((1, HK), F32) for _ in range(4)]
        for h in range(HEADS):
            lo, mid, hi = h * QKP, h * QKP + HK, (h + 1) * QKP
            dqn = dq_ref[h, :, 0:HK].astype(F32) * SCALE
            dqr = _rope(dq_ref[h, :, HK:QKP].astype(F32), cos_v, sin_v) * SCALE
            a, b, ga, gb = _head_norm_bwd(qf_ref[:, lo:mid], qf_ref[:, mid:hi], gqn, gqr, dqn, dqr)
            dqf_ref[:, lo:mid] = a.astype(BF16)
            dqf_ref[:, mid:hi] = b.astype(BF16)
            dgq_n = dgq_n + jnp.sum(ga, axis=0, keepdims=True)
            dgq_r = dgq_r + jnp.sum(gb, axis=0, keepdims=True)
            dkn = dk_ref[h, :, 0:HK].astype(F32) * LN2
            dkr = _rope(dk_ref[h, :, HK:QKP].astype(F32), cos_v, sin_v) * LN2
            a, b, ga, gb = _head_norm_bwd(kv_ref[:, lo:mid], kpe, gkn, gkr, dkn, dkr)
            dkv_ref[:, lo:mid] = a.astype(BF16)
            dkv_ref[:, mid:hi] = dv_ref[h].astype(BF16)
            dkpe = dkpe + b
            dgk_n = dgk_n + jnp.sum(ga, axis=0, keepdims=True)
            dgk_r = dgk_r + jnp.sum(gb, axis=0, keepdims=True)
        dkpe_ref[...] = dkpe
        dgq_ref[:, 0:HK] += dgq_n
        dgq_ref[:, HK:QKP] += dgq_r
        dgk_ref[:, 0:HK] += dgk_n
        dgk_ref[:, HK:QKP] += dgk_r

    head = pl.BlockSpec((tm, HEADS * QKP), lambda i: (i, 0))
    tok = pl.BlockSpec((tm, HK), lambda i: (i, 0))
    gain = pl.BlockSpec((1, QKP), lambda i: (0, 0))
    hq = pl.BlockSpec((HEADS, tm, QKP), lambda i: (0, i, 0))
    return _call(
        body, name="mla_prep_bwd", grid=(t // tm,),
        in_specs=[head, head, pl.BlockSpec((tm, HK), lambda i: (i, MLA_COLS // HK - 1)), tok, tok, gain, gain,
                  hq, hq, pl.BlockSpec((HEADS, tm, HK), lambda i: (0, i, 0))],
        out_specs=[head, head, tok, gain, gain],
        out_shape=[jax.ShapeDtypeStruct((t, HEADS * QKP), BF16), jax.ShapeDtypeStruct((t, HEADS * QKP), BF16),
                   jax.ShapeDtypeStruct((t, HK), F32), jax.ShapeDtypeStruct((1, QKP), F32),
                   jax.ShapeDtypeStruct((1, QKP), F32)],
        compiler_params=_cp(),
    )(qf, kv, p_mla, cos, sin, gq, gk, dq, dk, dv)


def _chunk_mask(row0, rows, cols):
    r = lax.broadcasted_iota(jnp.int32, (rows, cols), 0) + row0
    c = lax.broadcasted_iota(jnp.int32, (rows, cols), 1)
    return jnp.right_shift(r, 6) >= jnp.right_shift(c, 6)


def _flash_fwd(q, k, v):
    t = q.shape[1]
    tq = min(TQ, t)
    nq = t // tq
    sub = min(SUBQ, tq)
    pairs = [(i, j) for i in range(nq) for j in range(i + 1)]
    qi = jnp.asarray([p[0] for p in pairs], jnp.int32)
    kj = jnp.asarray([p[1] for p in pairs], jnp.int32)

    def body(qi_ref, kj_ref, q_ref, k_ref, v_ref, o_ref, lse_ref, m_s, l_s, acc_s):
        n = pl.program_id(1)
        i, j = qi_ref[n], kj_ref[n]

        @pl.when(j == 0)
        def _():
            m_s[...] = jnp.full_like(m_s, NEG)
            l_s[...] = jnp.zeros_like(l_s)
            acc_s[...] = jnp.zeros_like(acc_s)

        def step(diag):
            for r in range(tq // sub):
                rows = slice(r * sub, (r + 1) * sub)
                cols = (r + 1) * sub if diag else tq
                s = _dot_nt(q_ref[rows, :], k_ref[0:cols, :])
                if diag:
                    s = jnp.where(_chunk_mask(r * sub, sub, cols), s, NEG)
                m_old = m_s[rows, :]
                m_new = jnp.maximum(m_old, jnp.max(s, axis=-1, keepdims=True))
                alpha = jnp.exp2(m_old - m_new)
                p = jnp.exp2(s - jnp.tile(m_new, (1, cols // HK)))
                l_s[rows, :] = alpha * l_s[rows, :] + jnp.sum(p, axis=-1, keepdims=True)
                acc_s[rows, :] = alpha * acc_s[rows, :] + _dot(p.astype(BF16), v_ref[0:cols, :])
                m_s[rows, :] = m_new

        @pl.when(j < i)
        def _():
            step(False)

        @pl.when(j == i)
        def _():
            step(True)
            l = l_s[...]
            o_ref[...] = (acc_s[...] / l).astype(BF16)
            lse_ref[...] = m_s[...] + jnp.log(l) * LOG2E

    grid_spec = pltpu.PrefetchScalarGridSpec(
        num_scalar_prefetch=2, grid=(HEADS, len(pairs)),
        in_specs=[pl.BlockSpec((None, tq, QKP), lambda h, n, qi, kj: (h, qi[n], 0)),
                  pl.BlockSpec((None, tq, QKP), lambda h, n, qi, kj: (h, kj[n], 0)),
                  pl.BlockSpec((None, tq, HK), lambda h, n, qi, kj: (h, kj[n], 0))],
        out_specs=[pl.BlockSpec((tq, HK), lambda h, n, qi, kj: (qi[n], h)),
                   pl.BlockSpec((None, tq, HK), lambda h, n, qi, kj: (h, qi[n], 0))],
        scratch_shapes=[pltpu.VMEM((tq, HK), F32), pltpu.VMEM((tq, HK), F32), pltpu.VMEM((tq, HK), F32)],
    )
    return _call(
        body, name="flash_fwd", grid_spec=grid_spec,
        out_shape=[jax.ShapeDtypeStruct((t, D), BF16), jax.ShapeDtypeStruct((HEADS, t, HK), F32)],
        compiler_params=_cp(),
    )(qi, kj, q, k, v)


def _attn_delta(do, o):
    t = do.shape[0]
    tm = min(TM, t)

    def body(do_ref, o_ref, d_ref):
        for h in range(HEADS):
            ln = slice(h * HK, (h + 1) * HK)
            d = jnp.sum(do_ref[:, ln].astype(F32) * o_ref[:, ln].astype(F32), axis=-1, keepdims=True)
            d_ref[h] = jnp.broadcast_to(d, (tm, HK))

    blk = pl.BlockSpec((tm, D), lambda i: (i, 0))
    return _call(
        body, name="attn_delta", grid=(t // tm,),
        in_specs=[blk, blk],
        out_specs=pl.BlockSpec((HEADS, tm, HK), lambda i: (0, i, 0)),
        out_shape=jax.ShapeDtypeStruct((HEADS, t, HK), F32),
        compiler_params=_cp(),
    )(do, o)


def _flash_bwd(q, k, v, lse, delta, do):
    t = q.shape[1]
    tq = min(TQ, t)
    nq = t // tq
    sub = min(SUBQ, tq)
    pairs = [(i, j) for j in range(nq) for i in range(j, nq)]
    qi = jnp.asarray([p[0] for p in pairs], jnp.int32)
    kj = jnp.asarray([p[1] for p in pairs], jnp.int32)
    npairs = len(pairs)

    def body(qi_ref, kj_ref, q_ref, k_ref, v_ref, lse_ref, dl_ref, do_ref, dq_ref, dk_ref, dv_ref):
        n = pl.program_id(1)
        i, j = qi_ref[n], kj_ref[n]

        @pl.when(n == 0)
        def _():
            dq_ref[...] = jnp.zeros_like(dq_ref)

        @pl.when(i == j)
        def _():
            dk_ref[...] = jnp.zeros_like(dk_ref)
            dv_ref[...] = jnp.zeros_like(dv_ref)

        def step(diag):
            for r in range(tq // sub):
                rows = slice(r * sub, (r + 1) * sub)
                cols = (r + 1) * sub if diag else tq
                qv, dov, kv_ = q_ref[rows, :], do_ref[rows, :], k_ref[0:cols, :]
                p = jnp.exp2(_dot_nt(qv, kv_) - jnp.tile(lse_ref[rows, :], (1, cols // HK)))
                if diag:
                    p = jnp.where(_chunk_mask(r * sub, sub, cols), p, 0.0)
                dp = _dot_nt(dov, v_ref[0:cols, :])
                ds = (p * (dp - jnp.tile(dl_ref[rows, :], (1, cols // HK)))).astype(BF16)
                dv_ref[0:cols, :] += _dot_tn(p.astype(BF16), dov)
                dk_ref[0:cols, :] += _dot_tn(ds, qv)
                dq_rows = pl.ds(pl.multiple_of(i * tq + r * sub, sub), sub)
                dq_ref[dq_rows, :] += _dot(ds, kv_)

        @pl.when(j < i)
        def _():
            step(False)

        @pl.when(j == i)
        def _():
            step(True)

    grid_spec = pltpu.PrefetchScalarGridSpec(
        num_scalar_prefetch=2, grid=(HEADS, npairs),
        in_specs=[pl.BlockSpec((None, tq, QKP), lambda h, n, qi, kj: (h, qi[n], 0)),
                  pl.BlockSpec((None, tq, QKP), lambda h, n, qi, kj: (h, kj[n], 0)),
                  pl.BlockSpec((None, tq, HK), lambda h, n, qi, kj: (h, kj[n], 0)),
                  pl.BlockSpec((None, tq, HK), lambda h, n, qi, kj: (h, qi[n], 0)),
                  pl.BlockSpec((None, tq, HK), lambda h, n, qi, kj: (h, qi[n], 0)),
                  pl.BlockSpec((tq, HK), lambda h, n, qi, kj: (qi[n], h))],
        out_specs=[pl.BlockSpec((None, t, QKP), lambda h, n, qi, kj: (h, 0, 0)),
                   pl.BlockSpec((None, tq, QKP), lambda h, n, qi, kj: (h, kj[n], 0)),
                   pl.BlockSpec((None, tq, HK), lambda h, n, qi, kj: (h, kj[n], 0))],
    )
    return _call(
        body, name="flash_bwd", grid_spec=grid_spec,
        out_shape=[jax.ShapeDtypeStruct((HEADS, t, QKP), F32), jax.ShapeDtypeStruct((HEADS, t, QKP), F32),
                   jax.ShapeDtypeStruct((HEADS, t, HK), F32)],
        compiler_params=_cp(56),
    )(qi, kj, q, k, v, lse, delta, do)


def _adamw(name, w, g, m, v):
    r, c = w.shape
    tr = r if r <= 256 else next(k for k in (256, 352, 384) if r % k == 0)

    def body(w_ref, g_ref, m_ref, v_ref, d_ref, nm_ref, nv_ref):
        gv = g_ref[...]
        nm = ADAM_B1 * m_ref[...] + (1.0 - ADAM_B1) * gv
        nv = ADAM_B2 * v_ref[...] + (1.0 - ADAM_B2) * (gv * gv)
        m_hat = nm / (1.0 - ADAM_B1 ** ADAM_STEP)
        v_hat = nv / (1.0 - ADAM_B2 ** ADAM_STEP)
        d_ref[...] = -ADAM_LR * (m_hat / (jnp.sqrt(v_hat) + ADAM_EPS) + ADAM_WD * w_ref[...])
        nm_ref[...] = nm
        nv_ref[...] = nv

    blk = pl.BlockSpec((tr, c), lambda i: (i, 0))
    return _call(
        body, name=name, grid=(r // tr,),
        in_specs=[blk] * 4, out_specs=[blk] * 3,
        out_shape=[jax.ShapeDtypeStruct((r, c), F32)] * 3,
        compiler_params=_cp(),
    )(w, g, m, v)


def _place():
    return lax.axis_index("x"), lax.axis_index("y"), lax.axis_index("c")


def _other_chips(x, y):
    return [(1 - x, y), (x, 1 - y), (1 - x, 1 - y)]


class _Exchange:
    inputs = ()
    out_shapes = ()
    scratch = ()

    def start(self, *refs):
        raise NotImplementedError

    def finish(self, *refs):
        raise NotImplementedError

    def alone(self, name):
        def body(*refs):
            self.start(*refs)
            self.finish(*refs)

        anywhere = pl.BlockSpec(memory_space=pl.ANY)
        return _call(
            body, name=name,
            in_specs=[anywhere] * len(self.inputs), out_specs=[anywhere] * len(self.out_shapes),
            out_shape=list(self.out_shapes), scratch_shapes=list(self.scratch),
        )(*self.inputs)


class _GatherWeights(_Exchange):
    def __init__(self, shard):
        self.r = shard.shape[0]
        self.inputs = (shard,)
        self.out_shapes = (jax.ShapeDtypeStruct((4, self.r, PACK_W), shard.dtype),)
        self.scratch = (pltpu.SemaphoreType.DMA((6,)), pltpu.SemaphoreType.DMA((6,)), pltpu.SemaphoreType.DMA)

    def _copies(self, s_ref, g_ref, send_sems, recv_sems, local_sem):
        half = self.r // 2
        x, y, c = _place()
        chips = _other_chips(x, y)

        def rows(px, py, pc):
            return g_ref.at[2 * px + py, pl.ds(pc * half, half), :]

        def copy(k, block, to, src=None):
            return pltpu.make_async_remote_copy(
                src_ref=rows(*block) if src is None else src, dst_ref=rows(*block),
                send_sem=send_sems.at[k], recv_sem=recv_sems.at[k], device_id=to, device_id_type=MESH)

        mine = pltpu.make_async_copy(s_ref, g_ref.at[2 * x + y], local_sem)
        first = [copy(j, (x, y, c), (*chip, c), src=s_ref.at[pl.ds(c * half, half), :]) for j, chip in enumerate(chips)]
        passed = [copy(3 + j, (*chip, c), (x, y, 1 - c)) for j, chip in enumerate(chips)]
        landed = [copy(j, (*chip, c), (x, y, c)) for j, chip in enumerate(chips)]
        landed += [copy(3 + j, (*chip, 1 - c), (x, y, c)) for j, chip in enumerate(chips)]
        return mine, first, passed, landed

    def start(self, *refs):
        mine, first, _, _ = self._copies(*refs)
        mine.start()
        for cp in first:
            cp.start()

    def finish(self, *refs):
        mine, first, passed, landed = self._copies(*refs)
        for j in range(3):
            landed[j].wait_recv()
            passed[j].start()
        for j in range(3):
            landed[3 + j].wait_recv()
        for cp in first + passed:
            cp.wait_send()
        mine.wait()


def _swap_halves(name, gp):
    r = gp.shape[1]
    half = r // 2

    def body(g_ref, o_ref, send_sem, recv_sem):
        x, y, c = _place()
        cp = pltpu.make_async_remote_copy(
            src_ref=g_ref.at[:, pl.ds((1 - c) * half, half), :], dst_ref=o_ref,
            send_sem=send_sem, recv_sem=recv_sem, device_id=(x, y, 1 - c), device_id_type=MESH)
        cp.start()
        cp.wait()

    return _call(
        body, name=name,
        in_specs=[pl.BlockSpec(memory_space=pl.ANY)],
        out_specs=pl.BlockSpec(memory_space=pl.ANY),
        out_shape=jax.ShapeDtypeStruct((4, half, PACK_W), gp.dtype),
        scratch_shapes=[pltpu.SemaphoreType.DMA, pltpu.SemaphoreType.DMA],
    )(gp)


def _chip_sum(name, gp, got, c_arr):
    half = got.shape[1]
    tr = ADD_ROWS
    nb = half // tr

    def body(c_ref, a_ref, b_ref, o_ref, ob_ref):
        s = a_ref[...] + b_ref[...]
        o_ref[...] = s
        ob_ref[...] = s.astype(BF16)

    grid_spec = pltpu.PrefetchScalarGridSpec(
        num_scalar_prefetch=1, grid=(4, nb),
        in_specs=[pl.BlockSpec((None, tr, PACK_W), lambda s, i, c: (s, c[0] * nb + i, 0)),
                  pl.BlockSpec((None, tr, PACK_W), lambda s, i, c: (s, i, 0))],
        out_specs=[pl.BlockSpec((None, tr, PACK_W), lambda s, i, c: (s, i, 0)),
                   pl.BlockSpec((None, tr, PACK_W), lambda s, i, c: (s, i, 0))],
    )
    return _call(
        body, name=name, grid_spec=grid_spec,
        out_shape=[jax.ShapeDtypeStruct(got.shape, F32), jax.ShapeDtypeStruct(got.shape, BF16)],
        compiler_params=_cp(),
    )(c_arr, gp, got)


class _ScatterChipSums(_Exchange):
    def __init__(self, cs):
        self.inputs = (cs,)
        self.out_shapes = (jax.ShapeDtypeStruct((3,) + cs.shape[1:], cs.dtype),)
        self.scratch = (pltpu.SemaphoreType.DMA((3,)), pltpu.SemaphoreType.DMA((3,)))

    def _copies(self, s_ref, o_ref, send_sems, recv_sems):
        x, y, c = _place()
        return [pltpu.make_async_remote_copy(
            src_ref=s_ref.at[2 * px + py], dst_ref=o_ref.at[j],
            send_sem=send_sems.at[j], recv_sem=recv_sems.at[j], device_id=(px, py, c), device_id_type=MESH)
            for j, (px, py) in enumerate(_other_chips(x, y))]

    def start(self, *refs):
        for cp in self._copies(*refs):
            cp.start()

    def finish(self, *refs):
        for cp in self._copies(*refs):
            cp.wait()


def _shard_sum(name, cs, got, k_arr):
    h = cs.shape[1]
    tr = ADD_ROWS

    def body(k_ref, a_ref, b_ref, o_ref):
        o_ref[...] = ((a_ref[...] + b_ref[0].astype(F32)) + b_ref[1].astype(F32)) + b_ref[2].astype(F32)

    grid_spec = pltpu.PrefetchScalarGridSpec(
        num_scalar_prefetch=1, grid=(h // tr,),
        in_specs=[pl.BlockSpec((None, tr, PACK_W), lambda i, k: (k[0], i, 0)),
                  pl.BlockSpec((3, tr, PACK_W), lambda i, k: (0, i, 0))],
        out_specs=pl.BlockSpec((tr, PACK_W), lambda i, k: (i, 0)),
    )
    return _call(
        body, name=name, grid_spec=grid_spec,
        out_shape=jax.ShapeDtypeStruct((h, PACK_W), F32),
        compiler_params=_cp(),
    )(k_arr, cs, got)


def _join_halves(name, mine):
    h = mine.shape[0]

    def body(m_ref, o_ref, send_sem, recv_sem, local_sem):
        x, y, c = _place()
        own = pltpu.make_async_copy(m_ref, o_ref.at[pl.ds(c * h, h), :], local_sem)
        own.start()
        cp = pltpu.make_async_remote_copy(
            src_ref=m_ref, dst_ref=o_ref.at[pl.ds(c * h, h), :],
            send_sem=send_sem, recv_sem=recv_sem, device_id=(x, y, 1 - c), device_id_type=MESH)
        cp.start()
        cp.wait_send()
        pltpu.make_async_remote_copy(
            src_ref=m_ref, dst_ref=o_ref.at[pl.ds((1 - c) * h, h), :],
            send_sem=send_sem, recv_sem=recv_sem, device_id=(x, y, 1 - c), device_id_type=MESH).wait_recv()
        own.wait()

    return _call(
        body, name=name,
        in_specs=[pl.BlockSpec(memory_space=pl.ANY)],
        out_specs=pl.BlockSpec(memory_space=pl.ANY),
        out_shape=jax.ShapeDtypeStruct((2 * h, PACK_W), mine.dtype),
        scratch_shapes=[pltpu.SemaphoreType.DMA, pltpu.SemaphoreType.DMA, pltpu.SemaphoreType.DMA],
    )(mine)


def _all_reduce_small(v):
    r = v.shape[0]

    def body(v_ref, o_ref, buf, send_sems, recv_sems):
        x, y, c = _place()
        me = 4 * x + 2 * y + c
        buf[me] = v_ref[...]
        cps = []
        for k in range(1, 8):
            peer = (x ^ (k >> 2), y ^ ((k >> 1) & 1), c ^ (k & 1))
            cps.append(pltpu.make_async_remote_copy(
                src_ref=v_ref, dst_ref=buf.at[me],
                send_sem=send_sems.at[k - 1], recv_sem=recv_sems.at[k - 1], device_id=peer, device_id_type=MESH))
        for cp in cps:
            cp.start()
        for k in range(1, 8):
            pltpu.make_async_remote_copy(
                src_ref=v_ref, dst_ref=buf.at[me ^ k],
                send_sem=send_sems.at[k - 1], recv_sem=recv_sems.at[k - 1],
                device_id=(x, y, c), device_id_type=MESH).wait_recv()
        for cp in cps:
            cp.wait_send()
        acc = buf[0]
        for k in range(1, 8):
            acc = acc + buf[k]
        o_ref[...] = acc

    return _call(
        body, name="all_reduce_small",
        in_specs=[pl.BlockSpec(memory_space=pltpu.VMEM)],
        out_specs=pl.BlockSpec(memory_space=pltpu.VMEM),
        out_shape=jax.ShapeDtypeStruct((r, 128), F32),
        scratch_shapes=[pltpu.VMEM((8, r, 128), F32), pltpu.SemaphoreType.DMA((7,)), pltpu.SemaphoreType.DMA((7,))],
    )(v)


def _group(names):
    return tuple(e for e in BIG if e[0] in names)


def _pack(shards, dtype):
    return jnp.concatenate([s.astype(dtype).reshape(-1, PACK_W) for s in shards], axis=0)


def _unpack_full(g, group):
    out, at = {}, 0
    for name, rows, cols, axis in group:
        n = rows * cols // 4 // PACK_W
        blk = g[:, at:at + n, :]
        at += n
        if axis == 1:
            out[name] = blk.reshape(4, rows, cols // 4).transpose(1, 0, 2).reshape(rows, cols)
        else:
            out[name] = blk.reshape(rows, cols)
    return out


def _pack_grads(grads, group):
    parts = []
    for name, rows, cols, axis in group:
        g = grads[name]
        if axis == 1:
            g = g.reshape(rows, 4, cols // 4).transpose(1, 0, 2)
        parts.append(g.reshape(4, -1, PACK_W))
    rows_total = sum(p.shape[1] for p in parts)
    pad = -rows_total % PACK_ALIGN
    if pad:
        parts.append(jnp.zeros((4, pad, PACK_W), F32))
    return jnp.concatenate(parts, axis=1)


def _unpack_shard(s, group):
    out, at = {}, 0
    for name, rows, cols, axis in group:
        n = rows * cols // 4 // PACK_W
        shape = (rows, cols // 4) if axis == 1 else (rows // 4, cols)
        out[name] = s[at:at + n, :].reshape(shape)
        at += n
    return out


def _pack_small(parts):
    flat = jnp.concatenate([p.reshape(-1) for p in parts])
    pad = -flat.shape[0] % 1024
    return jnp.concatenate([flat, jnp.zeros((pad,), F32)]).reshape(-1, 128)


def _ffn_fwd(tag, h, gain, w_in, w_out, side=None):
    t = h.shape[0]
    tm = min(TM, t)
    n = _rms_fwd(tag + "_norm", h, gain)
    tn = 256

    def epi(accs, _):
        gate, up = accs
        return gate, up, _silu(gate) * up

    gate, up, act, *side_out = _mm(tag + "_in", [_a_spec(n, tm)], [_b_nn(w_in, tn), _b_nn(w_in, tn, DFF // tn)],
                                   [(0, 0), (0, 1)], epi, [], [BF16, BF16, BF16], t, DFF, tm, tn, side=side)
    (out,) = _mm(tag + "_out", [_a_spec(act, tm)], [_b_nn(w_out, 512)], [(0, 0)],
                 lambda accs, ex: (ex[0] + 0.5 * accs[0],), [_e_tile(h, tm, 512)], [F32], t, D, tm, 512)
    return out, (n, gate, up, act), side_out


class _Reduction:
    def __init__(self, tag, group, c_arr, k_arr):
        self.tag, self.group, self.c_arr, self.k_arr = tag, group, c_arr, k_arr

    def begin(self, grads):
        gp = _pack_grads(grads, self.group)
        self.sums, sums_bf16 = _chip_sum("grad_chip_sum_" + self.tag, gp, _swap_halves("grad_swap_" + self.tag, gp), self.c_arr)
        return _ScatterChipSums(sums_bf16)

    def end(self, got):
        mine = _shard_sum("grad_shard_sum_" + self.tag, self.sums, got, self.k_arr)
        return _unpack_shard(_join_halves("grad_join_" + self.tag, mine), self.group)


def _ffn_bwd(tag, h, gain, w_in, w_out, saved, dout, side, reduction):
    t = h.shape[0]
    tm = min(TM, t)
    n, gate, up, act = saved
    tn = 256

    def epi(accs, ex):
        da = 0.5 * accs[0]
        g, u = ex[0].astype(F32), ex[1].astype(F32)
        return da * u * _dsilu(g), da * _silu(g)

    dgate, dup, *side_out = _mm(tag + "_dact", [_a_spec(dout, tm)], [_b_nt(w_out, tn)], [(0, 0)], epi,
                                [_e_tile(gate, tm, tn), _e_tile(up, tm, tn)], [BF16, BF16], t, DFF, tm, tn,
                                trans_b=True, side=side)
    dw_out = _mm_tn(tag + "_dw_out", act, dout, scale=0.5, tm=DFF // 2, tn=D)
    dw_g = _mm_tn(tag + "_dw_gate", n, dgate, tm=D, tn=DFF // 2)
    dw_u = _mm_tn(tag + "_dw_up", n, dup, tm=D, tn=DFF // 2)
    sending = reduction.begin({tag + "_w_in": jnp.concatenate([dw_g, dw_u], axis=1), tag + "_w_out": dw_out})
    dn, got = _mm(tag + "_dn", [_a_spec(dgate, tm), _a_spec(dup, tm)],
                  [_b_nt(w_in, 512, DFF, 0), _b_nt(w_in, 512, DFF, 1)], [(0, 0), (1, 1)],
                  lambda accs, ex: (accs[0] + accs[1],), [], [F32], t, D, tm, 512, trans_b=True, side=sending)
    dh, dgain = _rms_bwd(tag + "_dnorm", h, gain, dn, dout)
    return dh, dgain, side_out, got


def kernel(x, positions, ffn1_norm, ffn1_w_in, ffn1_w_out, mix_norm, w_in, hg_lb_table, hg_out_norm, w_hg_branch, mla_q_lora_norm, w_q_up, mla_kv_lora_norm, w_kv_up, q_head_norm, k_head_norm, w_mla_branch, w_merge, b_merge, w_out, ffn2_norm, ffn2_w_in, ffn2_w_out, final_norm, loss_target, m_ffn1_norm, m_ffn1_w_in, m_ffn1_w_out, m_mix_norm, m_w_in, m_hg_lb_table, m_hg_out_norm, m_w_hg_branch, m_mla_q_lora_norm, m_w_q_up, m_mla_kv_lora_norm, m_w_kv_up, m_q_head_norm, m_k_head_norm, m_w_mla_branch, m_w_merge, m_b_merge, m_w_out, m_ffn2_norm, m_ffn2_w_in, m_ffn2_w_out, m_final_norm, v_ffn1_norm, v_ffn1_w_in, v_ffn1_w_out, v_mix_norm, v_w_in, v_hg_lb_table, v_hg_out_norm, v_w_hg_branch, v_mla_q_lora_norm, v_w_q_up, v_mla_kv_lora_norm, v_w_kv_up, v_q_head_norm, v_k_head_norm, v_w_mla_branch, v_w_merge, v_b_merge, v_w_out, v_ffn2_norm, v_ffn2_w_in, v_ffn2_w_out, v_final_norm):
    a = dict(locals())
    w = {n: a[n] for n in WEIGHT_ORDER}
    mom = {n: a["m_" + n] for n in WEIGHT_ORDER}
    var = {n: a["v_" + n] for n in WEIGHT_ORDER}
    t = x.shape[1]
    tm = min(TM, t)
    xt = x.reshape(t, D)
    target = loss_target.reshape(t, D)
    pos = positions.reshape(t, 1)
    x_i, y_i, c_i = _place()
    c_arr = c_i.astype(jnp.int32).reshape(1)
    k_arr = (2 * x_i + y_i).astype(jnp.int32).reshape(1)

    group_first = _group(("ffn1_w_in", "ffn1_w_out"))
    group_mid = _group(("w_in", "w_hg_branch", "w_q_up", "w_kv_up", "w_mla_branch", "w_merge", "w_out"))
    group_last = _group(("ffn2_w_in", "ffn2_w_out"))
    (gathered,) = _GatherWeights(_pack([w[e[0]][0] for e in group_first], BF16)).alone("gather_first")
    full = _unpack_full(gathered, group_first)
    h1, ffn1_saved, (gathered,) = _ffn_fwd(
        "ffn1", xt, w["ffn1_norm"], full["ffn1_w_in"], full["ffn1_w_out"],
        side=_GatherWeights(_pack([w[e[0]][0] for e in group_mid + group_last], BF16)))
    full.update(_unpack_full(gathered, group_mid + group_last))
    w_in_full = full["w_in"]
    w_in_hg = w_in_full[:, :4 * D]
    w_in_mla = jnp.pad(w_in_full[:, 4 * D:], ((0, 0), (0, MLA_COLS - (4800 - 4 * D))))
    w_q_pad = jnp.pad(full["w_q_up"].reshape(Q_LORA, HEADS, QK), ((0, 0), (0, 0), (0, QKP - QK))).reshape(Q_LORA, HEADS * QKP)
    w_kv = full["w_kv_up"]
    gq = jnp.pad(w["q_head_norm"], ((0, 0), (0, QKP - QK)))
    gk = jnp.pad(w["k_head_norm"], ((0, 0), (0, QKP - QK)))

    u = _rms_fwd("mix_norm", h1, w["mix_norm"])
    ident = lambda accs, ex: (accs[0],)
    (p_hg,) = _mm("in_hg", [_a_spec(u, tm)], [_b_nn(w_in_hg, 512)], [(0, 0)], ident, [], [F32], t, 4 * D, tm, 512)
    (p_mla,) = _mm("in_mla", [_a_spec(u, tm)], [_b_nn(w_in_mla, MLA_COLS)], [(0, 0)], ident, [], [F32], t, MLA_COLS, tm, MLA_COLS)
    o_raw, hg_o, states = _hgrn_fwd(p_hg, w["hg_lb_table"], w["hg_out_norm"])
    (y_hg,) = _mm("hg_branch", [_a_spec(hg_o, tm)], [_b_nn(full["w_hg_branch"], 512)], [(0, 0)], ident, [], [BF16], t, D, tm, 512)
    cqn, ckvn = _lora_norm_fwd(p_mla, w["mla_q_lora_norm"], w["mla_kv_lora_norm"])
    (qf,) = _mm("q_up", [_a_spec(cqn, tm)], [_b_nn(w_q_pad, 512)], [(0, 0)], ident, [], [F32], t, HEADS * QKP, tm, 512)
    (kvf,) = _mm("kv_up", [_a_spec(ckvn, tm)], [_b_nn(w_kv, 512)], [(0, 0)], ident, [], [F32], t, HEADS * QKP, tm, 512)
    cos, sin = _rope_tables(pos)
    qh, kh, vh = _mla_prep_fwd(qf, kvf, p_mla, cos, sin, gq, gk)
    o_mla, lse = _flash_fwd(qh, kh, vh)
    (y_mla,) = _mm("mla_branch", [_a_spec(o_mla, tm)], [_b_nn(full["w_mla_branch"], 512)], [(0, 0)], ident, [], [BF16], t, D, tm, 512)

    def merge_epi(accs, ex):
        g_hg = _sig(accs[0] + ex[2])
        g_mla = _sig(accs[1] + ex[3])
        return g_hg * ex[0].astype(F32) + g_mla * ex[1].astype(F32), g_hg, g_mla

    w_merge_f = full["w_merge"]
    mix, g_hg, g_mla = _mm(
        "merge", [_a_spec(u, tm)], [_b_nn(w_merge_f, 512), _b_nn(w_merge_f, 512, D // 512)], [(0, 0), (0, 1)], merge_epi,
        [_e_tile(y_hg, tm, 512), _e_tile(y_mla, tm, 512), _e_row(w["b_merge"], 512), _e_row(w["b_merge"], 512, D // 512)],
        [BF16, BF16, BF16], t, D, tm, 512)
    (h2,) = _mm("out_proj", [_a_spec(mix, tm)], [_b_nn(full["w_out"], 512)], [(0, 0)],
                lambda accs, ex: (ex[0] + accs[0],), [_e_tile(h1, tm, 512)], [F32], t, D, tm, 512)
    h3, ffn2_saved, _ = _ffn_fwd("ffn2", h2, w["ffn2_norm"], full["ffn2_w_in"], full["ffn2_w_out"])
    dh3, d_final_norm, loss_part = _final_loss(h3, target, w["final_norm"])

    grads, small = {}, {}
    small["final_norm"] = d_final_norm
    reduce_last = _Reduction("last", group_last, c_arr, k_arr)
    reduce_mid = _Reduction("mid", group_mid, c_arr, k_arr)
    reduce_first = _Reduction("first", group_first, c_arr, k_arr)
    dh2, small["ffn2_norm"], _, got_last = _ffn_bwd(
        "ffn2", h2, w["ffn2_norm"], full["ffn2_w_in"], full["ffn2_w_out"], ffn2_saved, dh3, None, reduce_last)

    def dmix_epi(accs, ex):
        dm = accs[0]
        ghg, gml, yhg, yml = [e.astype(F32) for e in ex]
        return dm * ghg, dm * gml, dm * yhg * ghg * (1.0 - ghg), dm * yml * gml * (1.0 - gml)

    dy_hg, dy_mla, dpre_hg, dpre_mla = _mm(
        "d_mix", [_a_spec(dh2, tm)], [_b_nt(full["w_out"], 512)], [(0, 0)], dmix_epi,
        [_e_tile(g_hg, tm, 512), _e_tile(g_mla, tm, 512), _e_tile(y_hg, tm, 512), _e_tile(y_mla, tm, 512)],
        [BF16, BF16, BF16, BF16], t, D, tm, 512, trans_b=True)
    grads["w_out"] = _mm_tn("dw_out", mix, dh2)
    small["b_merge"] = jnp.concatenate([_colsum("db_hg", dpre_hg), _colsum("db_mla", dpre_mla)], axis=1)
    grads["w_merge"] = jnp.concatenate([_mm_tn("dw_merge_hg", u, dpre_hg), _mm_tn("dw_merge_mla", u, dpre_mla)], axis=1)
    grads["w_hg_branch"] = _mm_tn("dw_hg_branch", hg_o, dy_hg)
    grads["w_mla_branch"] = _mm_tn("dw_mla_branch", o_mla, dy_mla)
    (dho,) = _mm("d_hg_o", [_a_spec(dy_hg, tm)], [_b_nt(full["w_hg_branch"], 512)], [(0, 0)], ident, [], [BF16], t, D, tm, 512, trans_b=True)
    (do_mla,) = _mm("d_o_mla", [_a_spec(dy_mla, tm)], [_b_nt(full["w_mla_branch"], 512)], [(0, 0)], ident, [], [BF16], t, D, tm, 512, trans_b=True)

    dq_raw, df_raw, di_raw, dg_raw, small["hg_lb_table"], small["hg_out_norm"] = _hgrn_bwd(
        p_hg, w["hg_lb_table"], w["hg_out_norm"], o_raw, states, dho)
    dp_hg = [dq_raw, df_raw, di_raw, dg_raw]

    dqh, dkh, dvh = _flash_bwd(qh, kh, vh, lse, _attn_delta(do_mla, o_mla), do_mla)
    dqf, dkvf, dkpe, dgq, dgk = _mla_prep_bwd(qf, kvf, p_mla, cos, sin, gq, gk, dqh, dkh, dvh)
    small["q_head_norm"] = dgq[:, :QK]
    small["k_head_norm"] = dgk[:, :QK]
    dwq_pad = _mm_tn("dw_q_up", cqn, dqf, tm=Q_LORA, tn=1024)
    grads["w_q_up"] = dwq_pad.reshape(Q_LORA, HEADS, QKP)[:, :, :QK].reshape(Q_LORA, HEADS * QK)
    grads["w_kv_up"] = _mm_tn("dw_kv_up", ckvn, dkvf, tm=KV_LORA, tn=1024)
    (dcqn,) = _mm("d_cq", [_a_spec(dqf, tm)], [_b_nt(w_q_pad, Q_LORA)], [(0, 0)], ident, [], [F32], t, Q_LORA, tm, Q_LORA, trans_b=True)
    (dckvn,) = _mm("d_ckv", [_a_spec(dkvf, tm)], [_b_nt(w_kv, KV_LORA)], [(0, 0)], ident, [], [F32], t, KV_LORA, tm, KV_LORA, trans_b=True)
    dp_mla, small["mla_q_lora_norm"], small["mla_kv_lora_norm"] = _lora_norm_bwd(
        p_mla, w["mla_q_lora_norm"], w["mla_kv_lora_norm"], dcqn, dckvn, dkpe)

    dw_in_hg = [_mm_tn("dw_in_hg%d" % k, u, dp_hg[k]) for k in range(4)]
    dw_in_mla = _mm_tn("dw_in_mla", u, dp_mla, tn=MLA_COLS)
    grads["w_in"] = jnp.concatenate(dw_in_hg + [dw_in_mla[:, :4800 - 4 * D]], axis=1)
    (du,) = _mm(
        "d_u",
        [_a_spec(dpre_hg, tm), _a_spec(dpre_mla, tm)] + [_a_spec(d, tm) for d in dp_hg] + [_a_spec(dp_mla, tm)],
        [_b_nt(w_merge_f, 512, D, 0), _b_nt(w_merge_f, 512, D, 1)]
        + [_b_nt(w_in_hg, 512, D, k) for k in range(4)] + [_b_nt(w_in_mla, 512)],
        [(k, k) for k in range(7)],
        lambda accs, ex: (functools.reduce(lambda p, q: p + q, accs),), [], [F32], t, D, tm, 512, trans_b=True)
    dh1, small["mix_norm"] = _rms_bwd("mix_dnorm", h1, w["mix_norm"], du, dh2)
    dx, small["ffn1_norm"], (got_mid,), got_first = _ffn_bwd(
        "ffn1", xt, w["ffn1_norm"], full["ffn1_w_in"], full["ffn1_w_out"], ffn1_saved, dh1,
        reduce_mid.begin(grads), reduce_first)

    g_shard = {**reduce_last.end(got_last), **reduce_mid.end(got_mid), **reduce_first.end(got_first)}
    small_sum = _all_reduce_small(_pack_small([small[n] for n, _ in SMALL] + [loss_part])).reshape(-1)
    g_small, at = {}, 0
    for n, shape in SMALL:
        size = shape[0] * shape[1]
        g_small[n] = small_sum[at:at + size].reshape(shape)
        at += size
    loss = small_sum[at]

    g_out, d_out, m_out, v_out = {}, {}, {}, {}
    for n in WEIGHT_ORDER:
        shape = w[n].shape
        g = g_shard[n] if n in g_shard else g_small[n]
        two = g.shape
        d_, m_, v_ = _adamw("adamw_" + n, w[n].reshape(two), g, mom[n].reshape(two), var[n].reshape(two))
        g_out[n], d_out[n], m_out[n], v_out[n] = g.reshape(shape), d_.reshape(shape), m_.reshape(shape), v_.reshape(shape)

    return (loss, dx.reshape(x.shape), *[g_out[n] for n in WEIGHT_ORDER], *[d_out[n] for n in WEIGHT_ORDER],
            *[m_out[n] for n in WEIGHT_ORDER], *[v_out[n] for n in WEIGHT_ORDER])
```

```python
import functools

import numpy as np
import jax
import jax.numpy as jnp
from jax import lax
from jax.experimental import pallas as pl
from jax.experimental.pallas import tpu as pltpu

F32 = jnp.float32
BF16 = jnp.bfloat16
MESH = pl.DeviceIdType.MESH

D = 1024
DFF = 2816
HEADS = 8
HK = 128
CHUNK = 64
ROPE = 64
QK = 192
QKP = 256
Q_LORA = 384
KV_LORA = 256
MLA_COLS = 768
EPS = 1e-6
ROPE_THETA = 10000.0
SCALE = QK ** -0.5
LOG2E = 1.4426950408889634
LN2 = 0.6931471805599453
NEG = -1e30
EXP_CLAMP = 80.0

ADAM_LR = 0.001
ADAM_B1 = 0.9
ADAM_B2 = 0.999
ADAM_EPS = 1e-08
ADAM_WD = 0.01
ADAM_STEP = 10

PACK_W = 1024
ADD_ROWS = 352
PACK_ALIGN = 2 * ADD_ROWS

TM = 1024
TQ = 1024
SUBQ = 512
HG_BT = 512
HG_HPB = 4
TT = 512
ROW_TM = 256

VMEM_MB = 48

BIG = (
    ("ffn1_w_in", D, 2 * DFF, 1),
    ("ffn1_w_out", DFF, D, 0),
    ("w_in", D, 4800, 1),
    ("w_hg_branch", D, D, 0),
    ("w_q_up", Q_LORA, HEADS * QK, 1),
    ("w_kv_up", KV_LORA, HEADS * 2 * HK, 1),
    ("w_mla_branch", D, D, 0),
    ("w_merge", D, 2 * D, 1),
    ("w_out", D, D, 0),
    ("ffn2_w_in", D, 2 * DFF, 1),
    ("ffn2_w_out", DFF, D, 0),
)
SMALL = (
    ("ffn1_norm", (1, D)),
    ("mix_norm", (1, D)),
    ("hg_lb_table", (2, D)),
    ("hg_out_norm", (1, HK)),
    ("mla_q_lora_norm", (1, Q_LORA)),
    ("mla_kv_lora_norm", (1, KV_LORA)),
    ("q_head_norm", (1, QK)),
    ("k_head_norm", (1, QK)),
    ("b_merge", (1, 2 * D)),
    ("ffn2_norm", (1, D)),
    ("final_norm", (1, D)),
)
WEIGHT_ORDER = ("ffn1_norm", "ffn1_w_in", "ffn1_w_out", "mix_norm", "w_in", "hg_lb_table", "hg_out_norm",
                "w_hg_branch", "mla_q_lora_norm", "w_q_up", "mla_kv_lora_norm", "w_kv_up", "q_head_norm",
                "k_head_norm", "w_mla_branch", "w_merge", "b_merge", "w_out", "ffn2_norm", "ffn2_w_in",
                "ffn2_w_out", "final_norm")


def _call(body, **kw):
    return pl.pallas_call(body, **kw)


def _cp(vmem_mb=VMEM_MB):
    return pltpu.CompilerParams(vmem_limit_bytes=vmem_mb << 20)


def _dot(a, b):
    return lax.dot_general(a, b, (((1,), (0,)), ((), ())), preferred_element_type=F32)


def _dot_nt(a, b):
    return lax.dot_general(a, b, (((1,), (1,)), ((), ())), preferred_element_type=F32)


def _dot_tn(a, b):
    return lax.dot_general(a, b, (((0,), (0,)), ((), ())), preferred_element_type=F32)


def _sig(x):
    return jax.nn.sigmoid(x)


def _silu(x):
    return x * _sig(x)


def _dsilu(x):
    s = _sig(x)
    return s * (1.0 + x * (1.0 - s))


def _a_spec(arr, tm, kblk=None, kidx=0):
    kb = arr.shape[1] if kblk is None else kblk
    return arr, pl.BlockSpec((tm, kb), lambda i, j, kidx=kidx: (i, kidx))


def _b_nn(arr, tn, off=0):
    return arr, pl.BlockSpec((arr.shape[0], tn), lambda i, j, off=off: (0, j + off))


def _b_nt(arr, tn, kblk=None, kidx=0):
    kb = arr.shape[1] if kblk is None else kblk
    return arr, pl.BlockSpec((tn, kb), lambda i, j, kidx=kidx: (j, kidx))


def _e_tile(arr, tm, tn, off=0):
    return arr, pl.BlockSpec((tm, tn), lambda i, j, off=off: (i, j + off))


def _e_row(arr, tn, off=0):
    return arr, pl.BlockSpec((1, tn), lambda i, j, off=off: (0, j + off))


def _mm(name, As, Bs, dots, epi, extras, out_dtypes, m, n, tm, tn, trans_b=False, side=None):
    na, nb, ne, no = len(As), len(Bs), len(extras), len(out_dtypes)
    ni, nj = m // tm, n // tn
    s_in = len(side.inputs) if side else 0
    s_out = len(side.out_shapes) if side else 0

    def body(*refs):
        a_refs = refs[:na]
        b_refs = refs[na:na + nb]
        e_refs = refs[na + nb:na + nb + ne]
        at = na + nb + ne
        side_refs = refs[at:at + s_in]
        o_refs = refs[at + s_in:at + s_in + no]
        side_refs = list(side_refs) + list(refs[at + s_in + no:])
        if side:
            i, j = pl.program_id(0), pl.program_id(1)

            @pl.when(jnp.logical_and(i == 0, j == 0))
            def _():
                side.start(*side_refs)

        a_vals = [r[...].astype(BF16) for r in a_refs]
        accs = []
        for ai, bi in dots:
            b = b_refs[bi][...]
            accs.append(_dot_nt(a_vals[ai], b) if trans_b else _dot(a_vals[ai], b))
        outs = epi(accs, [r[...] for r in e_refs])
        for o_ref, o in zip(o_refs, outs):
            o_ref[...] = o.astype(o_ref.dtype)
        if side:
            @pl.when(jnp.logical_and(i == ni - 1, j == nj - 1))
            def _():
                side.finish(*side_refs)

    ops = list(As) + list(Bs) + list(extras)
    anywhere = pl.BlockSpec(memory_space=pl.ANY)
    res = _call(
        body, name=name,
        grid=(ni, nj),
        in_specs=[s for _, s in ops] + [anywhere] * s_in,
        out_specs=[pl.BlockSpec((tm, tn), lambda i, j: (i, j)) for _ in out_dtypes] + [anywhere] * s_out,
        out_shape=[jax.ShapeDtypeStruct((m, n), dt) for dt in out_dtypes] + (list(side.out_shapes) if side else []),
        scratch_shapes=list(side.scratch) if side else [],
        compiler_params=_cp(),
    )(*[a for a, _ in ops], *(side.inputs if side else []))
    return res


def _mm_tn(name, a, b, scale=1.0, tm=1024, tn=1024):
    t, m = a.shape
    n = b.shape[1]
    tm, tn, tt = min(tm, m), min(tn, n), min(TT, t)
    nk = t // tt

    def body(a_ref, b_ref, o_ref):
        k = pl.program_id(2)

        @pl.when(k == 0)
        def _():
            o_ref[...] = jnp.zeros_like(o_ref)

        o_ref[...] += _dot_tn(a_ref[...].astype(BF16), b_ref[...].astype(BF16))
        if scale != 1.0:
            @pl.when(k == nk - 1)
            def _():
                o_ref[...] = o_ref[...] * scale

    return _call(
        body, name=name,
        grid=(m // tm, n // tn, nk),
        in_specs=[pl.BlockSpec((tt, tm), lambda i, j, k: (k, i)), pl.BlockSpec((tt, tn), lambda i, j, k: (k, j))],
        out_specs=pl.BlockSpec((tm, tn), lambda i, j, k: (i, j)),
        out_shape=jax.ShapeDtypeStruct((m, n), F32),
        compiler_params=_cp(),
    )(a, b)


def _rms_fwd(name, x, gain):
    t, d = x.shape
    tm = min(ROW_TM, t)

    def body(x_ref, g_ref, o_ref):
        xv = x_ref[...]
        r = lax.rsqrt(jnp.mean(xv * xv, axis=-1, keepdims=True) + EPS)
        o_ref[...] = (xv * r * g_ref[...]).astype(o_ref.dtype)

    return _call(
        body, name=name, grid=(t // tm,),
        in_specs=[pl.BlockSpec((tm, d), lambda i: (i, 0)), pl.BlockSpec((1, d), lambda i: (0, 0))],
        out_specs=pl.BlockSpec((tm, d), lambda i: (i, 0)),
        out_shape=jax.ShapeDtypeStruct((t, d), BF16),
        compiler_params=_cp(),
    )(x, gain)


def _rms_bwd_vals(xv, g, dn):
    r = lax.rsqrt(jnp.mean(xv * xv, axis=-1, keepdims=True) + EPS)
    xh = xv * r
    dxh = dn * g
    c = jnp.mean(dxh * xh, axis=-1, keepdims=True)
    return r * (dxh - xh * c), dn * xh


def _rms_bwd(name, x, gain, dn, dres):
    t, d = x.shape
    tm = min(ROW_TM, t)

    def body(x_ref, g_ref, dn_ref, dr_ref, dx_ref, dg_ref):
        @pl.when(pl.program_id(0) == 0)
        def _():
            dg_ref[...] = jnp.zeros_like(dg_ref)

        dx, dg = _rms_bwd_vals(x_ref[...], g_ref[...], dn_ref[...].astype(F32))
        dx_ref[...] = dr_ref[...] + dx
        dg_ref[...] += jnp.sum(dg, axis=0, keepdims=True)

    row = pl.BlockSpec((tm, d), lambda i: (i, 0))
    one = pl.BlockSpec((1, d), lambda i: (0, 0))
    return _call(
        body, name=name, grid=(t // tm,),
        in_specs=[row, one, row, row],
        out_specs=[row, one],
        out_shape=[jax.ShapeDtypeStruct((t, d), F32), jax.ShapeDtypeStruct((1, d), F32)],
        compiler_params=_cp(),
    )(x, gain, dn, dres)


def _final_loss(h, target, gain):
    t, d = h.shape
    tm = min(ROW_TM, t)

    def body(h_ref, t_ref, g_ref, dh_ref, dg_ref, l_ref):
        @pl.when(pl.program_id(0) == 0)
        def _():
            dg_ref[...] = jnp.zeros_like(dg_ref)
            l_ref[...] = jnp.zeros_like(l_ref)

        hv = h_ref[...]
        g = g_ref[...]
        r = lax.rsqrt(jnp.mean(hv * hv, axis=-1, keepdims=True) + EPS)
        xh = hv * r
        err = xh * g - t_ref[...]
        l_ref[...] += 0.5 * jnp.sum(jnp.mean(err * err, axis=-1, keepdims=True), axis=0, keepdims=True)
        dy = err * (1.0 / d)
        dxh = dy * g
        c = jnp.mean(dxh * xh, axis=-1, keepdims=True)
        dh_ref[...] = r * (dxh - xh * c)
        dg_ref[...] += jnp.sum(dy * xh, axis=0, keepdims=True)

    row = pl.BlockSpec((tm, d), lambda i: (i, 0))
    one = pl.BlockSpec((1, d), lambda i: (0, 0))
    return _call(
        body, name="final_loss", grid=(t // tm,),
        in_specs=[row, row, one],
        out_specs=[row, one, pl.BlockSpec((1, 128), lambda i: (0, 0))],
        out_shape=[jax.ShapeDtypeStruct((t, d), F32), jax.ShapeDtypeStruct((1, d), F32),
                   jax.ShapeDtypeStruct((1, 128), F32)],
        compiler_params=_cp(),
    )(h, target, gain)


def _colsum(name, x):
    t, n = x.shape
    tm = min(TM, t)

    def body(x_ref, o_ref):
        @pl.when(pl.program_id(0) == 0)
        def _():
            o_ref[...] = jnp.zeros_like(o_ref)

        o_ref[...] += jnp.sum(x_ref[...].astype(F32), axis=0, keepdims=True)

    return _call(
        body, name=name, grid=(t // tm,),
        in_specs=[pl.BlockSpec((tm, n), lambda i: (i, 0))],
        out_specs=pl.BlockSpec((1, n), lambda i: (0, 0)),
        out_shape=jax.ShapeDtypeStruct((1, n), F32),
        compiler_params=_cp(),
    )(x)


def _lora_norm_fwd(p_mla, gq, gkv):
    t = p_mla.shape[0]
    tm = min(ROW_TM, t)

    def body(p_ref, gq_ref, gkv_ref, q_ref, kv_ref):
        cq = p_ref[:, 0:Q_LORA]
        ckv = p_ref[:, Q_LORA:Q_LORA + KV_LORA]
        rq = lax.rsqrt(jnp.mean(cq * cq, axis=-1, keepdims=True) + EPS)
        rkv = lax.rsqrt(jnp.mean(ckv * ckv, axis=-1, keepdims=True) + EPS)
        q_ref[...] = (cq * rq * gq_ref[...]).astype(BF16)
        kv_ref[...] = (ckv * rkv * gkv_ref[...]).astype(BF16)

    return _call(
        body, name="lora_norm_fwd", grid=(t // tm,),
        in_specs=[pl.BlockSpec((tm, MLA_COLS), lambda i: (i, 0)),
                  pl.BlockSpec((1, Q_LORA), lambda i: (0, 0)), pl.BlockSpec((1, KV_LORA), lambda i: (0, 0))],
        out_specs=[pl.BlockSpec((tm, Q_LORA), lambda i: (i, 0)), pl.BlockSpec((tm, KV_LORA), lambda i: (i, 0))],
        out_shape=[jax.ShapeDtypeStruct((t, Q_LORA), BF16), jax.ShapeDtypeStruct((t, KV_LORA), BF16)],
        compiler_params=_cp(),
    )(p_mla, gq, gkv)


def _lora_norm_bwd(p_mla, gq, gkv, dcqn, dckvn, dkpe):
    t = p_mla.shape[0]
    tm = min(ROW_TM, t)

    def body(p_ref, gq_ref, gkv_ref, dq_ref, dkv_ref, dkpe_ref, dp_ref, dgq_ref, dgkv_ref):
        @pl.when(pl.program_id(0) == 0)
        def _():
            dgq_ref[...] = jnp.zeros_like(dgq_ref)
            dgkv_ref[...] = jnp.zeros_like(dgkv_ref)

        dcq, dgq = _rms_bwd_vals(p_ref[:, 0:Q_LORA], gq_ref[...], dq_ref[...])
        dckv, dgkv = _rms_bwd_vals(p_ref[:, Q_LORA:Q_LORA + KV_LORA], gkv_ref[...], dkv_ref[...])
        dp_ref[:, 0:Q_LORA] = dcq.astype(BF16)
        dp_ref[:, Q_LORA:Q_LORA + KV_LORA] = dckv.astype(BF16)
        dp_ref[:, Q_LORA + KV_LORA:MLA_COLS] = dkpe_ref[...].astype(BF16)
        dgq_ref[...] += jnp.sum(dgq, axis=0, keepdims=True)
        dgkv_ref[...] += jnp.sum(dgkv, axis=0, keepdims=True)

    return _call(
        body, name="lora_norm_bwd", grid=(t // tm,),
        in_specs=[pl.BlockSpec((tm, MLA_COLS), lambda i: (i, 0)),
                  pl.BlockSpec((1, Q_LORA), lambda i: (0, 0)), pl.BlockSpec((1, KV_LORA), lambda i: (0, 0)),
                  pl.BlockSpec((tm, Q_LORA), lambda i: (i, 0)), pl.BlockSpec((tm, KV_LORA), lambda i: (i, 0)),
                  pl.BlockSpec((tm, HK), lambda i: (i, 0))],
        out_specs=[pl.BlockSpec((tm, MLA_COLS), lambda i: (i, 0)),
                   pl.BlockSpec((1, Q_LORA), lambda i: (0, 0)), pl.BlockSpec((1, KV_LORA), lambda i: (0, 0))],
        out_shape=[jax.ShapeDtypeStruct((t, MLA_COLS), BF16), jax.ShapeDtypeStruct((1, Q_LORA), F32),
                   jax.ShapeDtypeStruct((1, KV_LORA), F32)],
        compiler_params=_cp(),
    )(p_mla, gq, gkv, dcqn, dckvn, dkpe)


def _cumsum_rows(x, row):
    for s in (1, 2, 4, 8, 16, 32):
        x = x + jnp.where(row >= s, pltpu.roll(x, s, 0), 0.0)
    return x


def _rcumsum_rows(x, row):
    for s in (1, 2, 4, 8, 16, 32):
        x = x + jnp.where(row < CHUNK - s, pltpu.roll(x, CHUNK - s, 0), 0.0)
    return x


def _hg_gates(qr, z, lb, row):
    q = _silu(qr)
    sg = _sig(z)
    f = lb + (1.0 - lb) * sg
    lf = jnp.log(f)
    k = (1.0 - lb) * (1.0 - sg)
    cum = _cumsum_rows(lf, row)
    mid = jnp.sum(jnp.where(row < CHUNK // 2, lf, 0.0), axis=0, keepdims=True)
    last = jnp.sum(lf, axis=0, keepdims=True)
    e_q = jnp.exp(jnp.minimum(cum - mid, EXP_CLAMP))
    e_k = jnp.exp(jnp.minimum(mid - cum, EXP_CLAMP))
    e_a = jnp.exp(cum)
    e_l = jnp.exp(last - cum)
    return q, sg, f, k, last, e_q, e_k, e_a, e_l


def _hgrn_fwd(p_hg, tab, gain):
    t = p_hg.shape[0]
    bt = min(HG_BT, t)
    nb, nc = t // bt, bt // CHUNK

    hpb = HG_HPB
    wide = hpb * HK

    def body(q_ref, f_ref, i_ref, g_ref, tab_ref, gain_ref, o_ref, ho_ref, st_ref, state):
        @pl.when(pl.program_id(1) == 0)
        def _():
            state[...] = jnp.zeros_like(state)

        row = lax.broadcasted_iota(jnp.int32, (CHUNK, HK), 0)
        tril = lax.broadcasted_iota(jnp.int32, (CHUNK, CHUNK), 0) >= lax.broadcasted_iota(jnp.int32, (CHUNK, CHUNK), 1)
        gain_v = gain_ref[...]

        def chunk(c, carry):
            sl = pl.ds(pl.multiple_of(c * CHUNK, CHUNK), CHUNK)
            for hh in range(hpb):
                ln = slice(hh * HK, (hh + 1) * HK)
                lb = _sig(tab_ref[0:1, ln] - tab_ref[1:2, ln])
                v = i_ref[sl, ln].astype(BF16)
                q, _, _, k, last, e_q, e_k, e_a, e_l = _hg_gates(q_ref[sl, ln], f_ref[sl, ln], lb, row)
                st = state[hh]
                st_ref[hh, c] = st
                p = jnp.where(tril, _dot_nt((q * e_q).astype(BF16), (k * e_k).astype(BF16)), 0.0)
                o = _dot(p.astype(BF16), v) + _dot_nt((q * e_a).astype(BF16), st.astype(BF16))
                state[hh] = jnp.exp(last) * st + _dot_tn(v, (k * e_l).astype(BF16))
                o_ref[sl, ln] = o
                r = lax.rsqrt(jnp.mean(o * o, axis=-1, keepdims=True) + EPS)
                ho_ref[sl, ln] = (o * r * gain_v * _silu(g_ref[sl, ln])).astype(BF16)
            return carry

        lax.fori_loop(0, nc, chunk, 0)

    def col(k):
        return pl.BlockSpec((bt, wide), lambda h, j, k=k: (j, k * (HEADS // hpb) + h))

    return _call(
        body, name="hgrn_fwd", grid=(HEADS // hpb, nb),
        in_specs=[col(0), col(1), col(2), col(3),
                  pl.BlockSpec((2, wide), lambda h, j: (0, h)), pl.BlockSpec((1, HK), lambda h, j: (0, 0))],
        out_specs=[pl.BlockSpec((bt, wide), lambda h, j: (j, h)), pl.BlockSpec((bt, wide), lambda h, j: (j, h)),
                   pl.BlockSpec((hpb, nc, HK, HK), lambda h, j: (h, j, 0, 0))],
        out_shape=[jax.ShapeDtypeStruct((t, D), F32), jax.ShapeDtypeStruct((t, D), BF16),
                   jax.ShapeDtypeStruct((HEADS, t // CHUNK, HK, HK), F32)],
        scratch_shapes=[pltpu.VMEM((hpb, HK, HK), F32)],
        compiler_params=_cp(),
    )(p_hg, p_hg, p_hg, p_hg, tab, gain)


def _hgrn_bwd(p_hg, tab, gain, o_raw, states, dho):
    t = p_hg.shape[0]
    bt = min(HG_BT, t)
    nb, nc = t // bt, bt // CHUNK
    hpb = HG_HPB
    wide = hpb * HK

    def body(q_ref, f_ref, i_ref, g_ref, tab_ref, gain_ref, o_ref, st_ref, dho_ref,
             dq_ref, df_ref, di_ref, dg_ref, dtab_ref, dgain_ref, dstate, dlb):
        h, j = pl.program_id(0), pl.program_id(1)

        @pl.when(jnp.logical_and(h == 0, j == 0))
        def _():
            dgain_ref[...] = jnp.zeros_like(dgain_ref)

        @pl.when(j == 0)
        def _():
            dstate[...] = jnp.zeros_like(dstate)
            dlb[...] = jnp.zeros_like(dlb)

        row = lax.broadcasted_iota(jnp.int32, (CHUNK, HK), 0)
        tril = lax.broadcasted_iota(jnp.int32, (CHUNK, CHUNK), 0) >= lax.broadcasted_iota(jnp.int32, (CHUNK, CHUNK), 1)
        gain_v = gain_ref[...]

        def chunk(cc, carry):
            c = nc - 1 - cc
            sl = pl.ds(pl.multiple_of(c * CHUNK, CHUNK), CHUNK)
            dgain = jnp.zeros((1, HK), F32)
            for hh in range(hpb):
                ln = slice(hh * HK, (hh + 1) * HK)
                lb = _sig(tab_ref[0:1, ln] - tab_ref[1:2, ln])
                qr = q_ref[sl, ln]
                v = i_ref[sl, ln].astype(BF16)
                gr = g_ref[sl, ln]
                q, sg, f, k, last, e_q, e_k, e_a, e_l = _hg_gates(qr, f_ref[sl, ln], lb, row)
                o = o_ref[sl, ln]
                r = lax.rsqrt(jnp.mean(o * o, axis=-1, keepdims=True) + EPS)
                oh = o * r
                dh = dho_ref[sl, ln].astype(F32)
                dnorm = dh * _silu(gr)
                dg_ref[sl, ln] = (dh * oh * gain_v * _dsilu(gr)).astype(BF16)
                dgain = dgain + jnp.sum(dnorm * oh, axis=0, keepdims=True)
                dxh = dnorm * gain_v
                do = (r * (dxh - oh * jnp.mean(dxh * oh, axis=-1, keepdims=True))).astype(BF16)
                st0 = st_ref[hh, c]
                st0_b = st0.astype(BF16)
                ds1 = dstate[hh]
                ds1_b = ds1.astype(BF16)
                qt = (q * e_q).astype(BF16)
                kt = (k * e_k).astype(BF16)
                qd = (q * e_a).astype(BF16)
                kd = (k * e_l).astype(BF16)
                p = jnp.where(tril, _dot_nt(qt, kt), 0.0).astype(BF16)
                dp = jnp.where(tril, _dot_nt(do, v), 0.0).astype(BF16)
                dv = _dot_tn(p, do) + _dot_nt(kd, ds1_b)
                dqt = _dot(dp, kt)
                dkt = _dot_tn(dp, qt)
                dq_inter = _dot(do, st0_b) * e_a
                dk_inter = _dot(v, ds1_b) * e_l
                dq = dqt * e_q + dq_inter
                dk = dkt * e_k + dk_inter
                e_last = jnp.exp(last)
                dstate[hh] = _dot_tn(do, qd) + e_last * ds1
                dlast = (jnp.sum(k * dk_inter, axis=0, keepdims=True)
                         + e_last * jnp.sum(ds1 * st0, axis=0, keepdims=True))
                da = (qt.astype(F32) * dqt - kt.astype(F32) * dkt + q * dq_inter - k * dk_inter
                      + jnp.where(row == CHUNK - 1, dlast, 0.0))
                dlf = _rcumsum_rows(da, row)
                dfv = dlf / f - dk
                df_ref[sl, ln] = (dfv * (1.0 - lb) * sg * (1.0 - sg)).astype(BF16)
                dlb[:, ln] += jnp.sum(dfv * (1.0 - sg), axis=0, keepdims=True)
                dq_ref[sl, ln] = (dq * _dsilu(qr)).astype(BF16)
                di_ref[sl, ln] = dv.astype(BF16)
            dgain_ref[...] += dgain
            return carry

        lax.fori_loop(0, nc, chunk, 0)

        @pl.when(j == nb - 1)
        def _():
            lb = _sig(tab_ref[0:1, :] - tab_ref[1:2, :])
            d0 = dlb[...] * lb * (1.0 - lb)
            dtab_ref[0:1, :] = d0
            dtab_ref[1:2, :] = -d0

    def col(k):
        return pl.BlockSpec((bt, wide), lambda h, j, k=k: (nb - 1 - j, k * (HEADS // hpb) + h))

    tok = pl.BlockSpec((bt, wide), lambda h, j: (nb - 1 - j, h))
    return _call(
        body, name="hgrn_bwd", grid=(HEADS // hpb, nb),
        in_specs=[col(0), col(1), col(2), col(3),
                  pl.BlockSpec((2, wide), lambda h, j: (0, h)), pl.BlockSpec((1, HK), lambda h, j: (0, 0)),
                  tok, pl.BlockSpec((hpb, nc, HK, HK), lambda h, j: (h, nb - 1 - j, 0, 0)), tok],
        out_specs=[tok, tok, tok, tok,
                   pl.BlockSpec((2, wide), lambda h, j: (0, h)), pl.BlockSpec((1, HK), lambda h, j: (0, 0))],
        out_shape=[jax.ShapeDtypeStruct((t, D), BF16)] * 4
        + [jax.ShapeDtypeStruct((2, D), F32), jax.ShapeDtypeStruct((1, HK), F32)],
        scratch_shapes=[pltpu.VMEM((hpb, HK, HK), F32), pltpu.VMEM((1, wide), F32)],
        compiler_params=_cp(),
    )(p_hg, p_hg, p_hg, p_hg, tab, gain, o_raw, states, dho)


def _rope_tables(pos):
    t = pos.shape[0]
    tm = min(ROW_TM, t)
    inv = np.zeros((1, HK), np.float32)
    freq = (ROPE_THETA ** (-np.arange(0, ROPE, 2, dtype=np.float32) / ROPE)).astype(np.float32)
    inv[0, 0:ROPE // 2] = freq
    inv[0, ROPE // 2:ROPE] = freq
    sign = np.zeros((1, HK), np.float32)
    sign[0, 0:ROPE // 2] = -1.0
    sign[0, ROPE // 2:ROPE] = 1.0

    def body(pos_ref, inv_ref, sign_ref, cos_ref, sin_ref):
        ang = pos_ref[...].astype(F32) * inv_ref[...]
        cos_ref[...] = jnp.cos(ang)
        sin_ref[...] = jnp.sin(ang) * sign_ref[...]

    one = pl.BlockSpec((1, HK), lambda i: (0, 0))
    row = pl.BlockSpec((tm, HK), lambda i: (i, 0))
    return _call(
        body, name="rope_tables", grid=(t // tm,),
        in_specs=[pl.BlockSpec((tm, 1), lambda i: (i, 0)), one, one],
        out_specs=[row, row],
        out_shape=[jax.ShapeDtypeStruct((t, HK), F32)] * 2,
        compiler_params=_cp(),
    )(pos, jnp.asarray(inv), jnp.asarray(sign))


def _rope(x, cos, sin_signed):
    r = lax.broadcasted_iota(jnp.int32, (HK, HK), 0)
    c = lax.broadcasted_iota(jnp.int32, (HK, HK), 1)
    half = ROPE // 2
    swap = jnp.logical_or(jnp.logical_and(c < half, r == c + half),
                          jnp.logical_and(jnp.logical_and(c >= half, c < ROPE), r == c - half))
    return x * cos + _dot_split(x, swap.astype(BF16)) * sin_signed


def _dot_split(x, m):
    hi = x.astype(BF16)
    lo = (x - hi.astype(F32)).astype(BF16)
    return _dot(hi, m) + _dot(lo, m)


def _lane_sum(x):
    return _dot_split(x, jnp.ones((HK, HK), BF16))


def _head_norm(xn, xr):
    r = lax.rsqrt(_lane_sum(xn * xn + xr * xr) * (1.0 / QK) + EPS)
    return xn * r, xr * r, r


def _head_norm_bwd(xn, xr, g_n, g_r, dn, dr):
    hn, hr, r = _head_norm(xn, xr)
    dxn, dxr = dn * g_n, dr * g_r
    c = _lane_sum(dxn * hn + dxr * hr) * (1.0 / QK)
    return r * (dxn - hn * c), r * (dxr - hr * c), dn * hn, dr * hr


def _mla_prep_fwd(qf, kv, p_mla, cos, sin, gq, gk):
    t = qf.shape[0]
    tm = min(ROW_TM, t)

    def body(qf_ref, kv_ref, kpe_ref, cos_ref, sin_ref, gq_ref, gk_ref, q_ref, k_ref, v_ref):
        cos_v, sin_v = cos_ref[...], sin_ref[...]
        kpe = kpe_ref[...]
        for h in range(HEADS):
            lo, mid, hi = h * QKP, h * QKP + HK, (h + 1) * QKP
            qn, qr, _ = _head_norm(qf_ref[:, lo:mid], qf_ref[:, mid:hi])
            q_ref[h, :, 0:HK] = (qn * gq_ref[:, 0:HK] * (SCALE * LOG2E)).astype(BF16)
            q_ref[h, :, HK:QKP] = (_rope(qr * gq_ref[:, HK:QKP], cos_v, sin_v) * (SCALE * LOG2E)).astype(BF16)
            kn, kr, _ = _head_norm(kv_ref[:, lo:mid], kpe)
            k_ref[h, :, 0:HK] = (kn * gk_ref[:, 0:HK]).astype(BF16)
            k_ref[h, :, HK:QKP] = _rope(kr * gk_ref[:, HK:QKP], cos_v, sin_v).astype(BF16)
            v_ref[h] = kv_ref[:, mid:hi].astype(BF16)

    head = pl.BlockSpec((tm, HEADS * QKP), lambda i: (i, 0))
    tok = pl.BlockSpec((tm, HK), lambda i: (i, 0))
    gain = pl.BlockSpec((1, QKP), lambda i: (0, 0))
    return _call(
        body, name="mla_prep_fwd", grid=(t // tm,),
        in_specs=[head, head, pl.BlockSpec((tm, HK), lambda i: (i, MLA_COLS // HK - 1)), tok, tok, gain, gain],
        out_specs=[pl.BlockSpec((HEADS, tm, QKP), lambda i: (0, i, 0)),
                   pl.BlockSpec((HEADS, tm, QKP), lambda i: (0, i, 0)),
                   pl.BlockSpec((HEADS, tm, HK), lambda i: (0, i, 0))],
        out_shape=[jax.ShapeDtypeStruct((HEADS, t, QKP), BF16), jax.ShapeDtypeStruct((HEADS, t, QKP), BF16),
                   jax.ShapeDtypeStruct((HEADS, t, HK), BF16)],
        compiler_params=_cp(),
    )(qf, kv, p_mla, cos, sin, gq, gk)


def _mla_prep_bwd(qf, kv, p_mla, cos, sin, gq, gk, dq, dk, dv):
    t = qf.shape[0]
    tm = min(ROW_TM, t)

    def body(qf_ref, kv_ref, kpe_ref, cos_ref, sin_ref, gq_ref, gk_ref, dq_ref, dk_ref, dv_ref,
             dqf_ref, dkv_ref, dkpe_ref, dgq_ref, dgk_ref):
        @pl.when(pl.program_id(0) == 0)
        def _():
            dgq_ref[...] = jnp.zeros_like(dgq_ref)
            dgk_ref[...] = jnp.zeros_like(dgk_ref)

        cos_v, sin_v = cos_ref[...], -sin_ref[...]
        kpe = kpe_ref[...]
        gqn, gqr, gkn, gkr = gq_ref[:, 0:HK], gq_ref[:, HK:QKP], gk_ref[:, 0:HK], gk_ref[:, HK:QKP]
        dkpe = jnp.zeros((tm, HK), F32)
        dgq_n, dgq_r, dgk_n, dgk_r = [jnp.zeros((1, HK), F32) for _ in range(4)]
        for h in range(HEADS):
            lo, mid, hi = h * QKP, h * QKP + HK, (h + 1) * QKP
            dqn = dq_ref[h, :, 0:HK].astype(F32) * SCALE
            dqr = _rope(dq_ref[h, :, HK:QKP].astype(F32), cos_v, sin_v) * SCALE
            a, b, ga, gb = _head_norm_bwd(qf_ref[:, lo:mid], qf_ref[:, mid:hi], gqn, gqr, dqn, dqr)
            dqf_ref[:, lo:mid] = a.astype(BF16)
            dqf_ref[:, mid:hi] = b.astype(BF16)
            dgq_n = dgq_n + jnp.sum(ga, axis=0, keepdims=True)
            dgq_r = dgq_r + jnp.sum(gb, axis=0, keepdims=True)
            dkn = dk_ref[h, :, 0:HK].astype(F32) * LN2
            dkr = _rope(dk_ref[h, :, HK:QKP].astype(F32), cos_v, sin_v) * LN2
            a, b, ga, gb = _head_norm_bwd(kv_ref[:, lo:mid], kpe, gkn, gkr, dkn, dkr)
            dkv_ref[:, lo:mid] = a.astype(BF16)
            dkv_ref[:, mid:hi] = dv_ref[h].astype(BF16)
            dkpe = dkpe + b
            dgk_n = dgk_n + jnp.sum(ga, axis=0, keepdims=True)
            dgk_r = dgk_r + jnp.sum(gb, axis=0, keepdims=True)
        dkpe_ref[...] = dkpe
        dgq_ref[:, 0:HK] += dgq_n
        dgq_ref[:, HK:QKP] += dgq_r
        dgk_ref[:, 0:HK] += dgk_n
        dgk_ref[:, HK:QKP] += dgk_r

    head = pl.BlockSpec((tm, HEADS * QKP), lambda i: (i, 0))
    tok = pl.BlockSpec((tm, HK), lambda i: (i, 0))
    gain = pl.BlockSpec((1, QKP), lambda i: (0, 0))
    hq = pl.BlockSpec((HEADS, tm, QKP), lambda i: (0, i, 0))
    return _call(
        body, name="mla_prep_bwd", grid=(t // tm,),
        in_specs=[head, head, pl.BlockSpec((tm, HK), lambda i: (i, MLA_COLS // HK - 1)), tok, tok, gain, gain,
                  hq, hq, pl.BlockSpec((HEADS, tm, HK), lambda i: (0, i, 0))],
        out_specs=[head, head, tok, gain, gain],
        out_shape=[jax.ShapeDtypeStruct((t, HEADS * QKP), BF16), jax.ShapeDtypeStruct((t, HEADS * QKP), BF16),
                   jax.ShapeDtypeStruct((t, HK), F32), jax.ShapeDtypeStruct((1, QKP), F32),
                   jax.ShapeDtypeStruct((1, QKP), F32)],
        compiler_params=_cp(),
    )(qf, kv, p_mla, cos, sin, gq, gk, dq, dk, dv)


def _chunk_mask(row0, rows, cols):
    r = lax.broadcasted_iota(jnp.int32, (rows, cols), 0) + row0
    c = lax.broadcasted_iota(jnp.int32, (rows, cols), 1)
    return jnp.right_shift(r, 6) >= jnp.right_shift(c, 6)


def _flash_fwd(q, k, v):
    t = q.shape[1]
    tq = min(TQ, t)
    nq = t // tq
    sub = min(SUBQ, tq)
    pairs = [(i, j) for i in range(nq) for j in range(i + 1)]
    qi = jnp.asarray([p[0] for p in pairs], jnp.int32)
    kj = jnp.asarray([p[1] for p in pairs], jnp.int32)

    def body(qi_ref, kj_ref, q_ref, k_ref, v_ref, o_ref, lse_ref, m_s, l_s, acc_s):
        n = pl.program_id(1)
        i, j = qi_ref[n], kj_ref[n]

        @pl.when(j == 0)
        def _():
            m_s[...] = jnp.full_like(m_s, NEG)
            l_s[...] = jnp.zeros_like(l_s)
            acc_s[...] = jnp.zeros_like(acc_s)

        def step(diag):
            for r in range(tq // sub):
                rows = slice(r * sub, (r + 1) * sub)
                cols = (r + 1) * sub if diag else tq
                s = _dot_nt(q_ref[rows, :], k_ref[0:cols, :])
                if diag:
                    s = jnp.where(_chunk_mask(r * sub, sub, cols), s, NEG)
                m_old = m_s[rows, :]
                m_new = jnp.maximum(m_old, jnp.max(s, axis=-1, keepdims=True))
                alpha = jnp.exp2(m_old - m_new)
                p = jnp.exp2(s - jnp.tile(m_new, (1, cols // HK)))
                l_s[rows, :] = alpha * l_s[rows, :] + jnp.sum(p, axis=-1, keepdims=True)
                acc_s[rows, :] = alpha * acc_s[rows, :] + _dot(p.astype(BF16), v_ref[0:cols, :])
                m_s[rows, :] = m_new

        @pl.when(j < i)
        def _():
            step(False)

        @pl.when(j == i)
        def _():
            step(True)
            l = l_s[...]
            o_ref[...] = (acc_s[...] / l).astype(BF16)
            lse_ref[...] = m_s[...] + jnp.log(l) * LOG2E

    grid_spec = pltpu.PrefetchScalarGridSpec(
        num_scalar_prefetch=2, grid=(HEADS, len(pairs)),
        in_specs=[pl.BlockSpec((None, tq, QKP), lambda h, n, qi, kj: (h, qi[n], 0)),
                  pl.BlockSpec((None, tq, QKP), lambda h, n, qi, kj: (h, kj[n], 0)),
                  pl.BlockSpec((None, tq, HK), lambda h, n, qi, kj: (h, kj[n], 0))],
        out_specs=[pl.BlockSpec((tq, HK), lambda h, n, qi, kj: (qi[n], h)),
                   pl.BlockSpec((None, tq, HK), lambda h, n, qi, kj: (h, qi[n], 0))],
        scratch_shapes=[pltpu.VMEM((tq, HK), F32), pltpu.VMEM((tq, HK), F32), pltpu.VMEM((tq, HK), F32)],
    )
    return _call(
        body, name="flash_fwd", grid_spec=grid_spec,
        out_shape=[jax.ShapeDtypeStruct((t, D), BF16), jax.ShapeDtypeStruct((HEADS, t, HK), F32)],
        compiler_params=_cp(),
    )(qi, kj, q, k, v)


def _attn_delta(do, o):
    t = do.shape[0]
    tm = min(TM, t)

    def body(do_ref, o_ref, d_ref):
        for h in range(HEADS):
            ln = slice(h * HK, (h + 1) * HK)
            d = jnp.sum(do_ref[:, ln].astype(F32) * o_ref[:, ln].astype(F32), axis=-1, keepdims=True)
            d_ref[h] = jnp.broadcast_to(d, (tm, HK))

    blk = pl.BlockSpec((tm, D), lambda i: (i, 0))
    return _call(
        body, name="attn_delta", grid=(t // tm,),
        in_specs=[blk, blk],
        out_specs=pl.BlockSpec((HEADS, tm, HK), lambda i: (0, i, 0)),
        out_shape=jax.ShapeDtypeStruct((HEADS, t, HK), F32),
        compiler_params=_cp(),
    )(do, o)


def _flash_bwd(q, k, v, lse, delta, do):
    t = q.shape[1]
    tq = min(TQ, t)
    nq = t // tq
    sub = min(SUBQ, tq)
    pairs = [(i, j) for j in range(nq) for i in range(j, nq)]
    qi = jnp.asarray([p[0] for p in pairs], jnp.int32)
    kj = jnp.asarray([p[1] for p in pairs], jnp.int32)
    npairs = len(pairs)

    def body(qi_ref, kj_ref, q_ref, k_ref, v_ref, lse_ref, dl_ref, do_ref, dq_ref, dk_ref, dv_ref):
        n = pl.program_id(1)
        i, j = qi_ref[n], kj_ref[n]

        @pl.when(n == 0)
        def _():
            dq_ref[...] = jnp.zeros_like(dq_ref)

        @pl.when(i == j)
        def _():
            dk_ref[...] = jnp.zeros_like(dk_ref)
            dv_ref[...] = jnp.zeros_like(dv_ref)

        def step(diag):
            for r in range(tq // sub):
                rows = slice(r * sub, (r + 1) * sub)
                cols = (r + 1) * sub if diag else tq
                qv, dov, kv_ = q_ref[rows, :], do_ref[rows, :], k_ref[0:cols, :]
                p = jnp.exp2(_dot_nt(qv, kv_) - jnp.tile(lse_ref[rows, :], (1, cols // HK)))
                if diag:
                    p = jnp.where(_chunk_mask(r * sub, sub, cols), p, 0.0)
                dp = _dot_nt(dov, v_ref[0:cols, :])
                ds = (p * (dp - jnp.tile(dl_ref[rows, :], (1, cols // HK)))).astype(BF16)
                dv_ref[0:cols, :] += _dot_tn(p.astype(BF16), dov)
                dk_ref[0:cols, :] += _dot_tn(ds, qv)
                dq_rows = pl.ds(pl.multiple_of(i * tq + r * sub, sub), sub)
                dq_ref[dq_rows, :] += _dot(ds, kv_)

        @pl.when(j < i)
        def _():
            step(False)

        @pl.when(j == i)
        def _():
            step(True)

    grid_spec = pltpu.PrefetchScalarGridSpec(
        num_scalar_prefetch=2, grid=(HEADS, npairs),
        in_specs=[pl.BlockSpec((None, tq, QKP), lambda h, n, qi, kj: (h, qi[n], 0)),
                  pl.BlockSpec((None, tq, QKP), lambda h, n, qi, kj: (h, kj[n], 0)),
                  pl.BlockSpec((None, tq, HK), lambda h, n, qi, kj: (h, kj[n], 0)),
                  pl.BlockSpec((None, tq, HK), lambda h, n, qi, kj: (h, qi[n], 0)),
                  pl.BlockSpec((None, tq, HK), lambda h, n, qi, kj: (h, qi[n], 0)),
                  pl.BlockSpec((tq, HK), lambda h, n, qi, kj: (qi[n], h))],
        out_specs=[pl.BlockSpec((None, t, QKP), lambda h, n, qi, kj: (h, 0, 0)),
                   pl.BlockSpec((None, tq, QKP), lambda h, n, qi, kj: (h, kj[n], 0)),
                   pl.BlockSpec((None, tq, HK), lambda h, n, qi, kj: (h, kj[n], 0))],
    )
    return _call(
        body, name="flash_bwd", grid_spec=grid_spec,
        out_shape=[jax.ShapeDtypeStruct((HEADS, t, QKP), F32), jax.ShapeDtypeStruct((HEADS, t, QKP), F32),
                   jax.ShapeDtypeStruct((HEADS, t, HK), F32)],
        compiler_params=_cp(56),
    )(qi, kj, q, k, v, lse, delta, do)


def _adamw(name, w, g, m, v):
    r, c = w.shape
    tr = r if r <= 256 else next(k for k in (256, 352, 384) if r % k == 0)

    def body(w_ref, g_ref, m_ref, v_ref, d_ref, nm_ref, nv_ref):
        gv = g_ref[...]
        nm = ADAM_B1 * m_ref[...] + (1.0 - ADAM_B1) * gv
        nv = ADAM_B2 * v_ref[...] + (1.0 - ADAM_B2) * (gv * gv)
        m_hat = nm / (1.0 - ADAM_B1 ** ADAM_STEP)
        v_hat = nv / (1.0 - ADAM_B2 ** ADAM_STEP)
        d_ref[...] = -ADAM_LR * (m_hat / (jnp.sqrt(v_hat) + ADAM_EPS) + ADAM_WD * w_ref[...])
        nm_ref[...] = nm
        nv_ref[...] = nv

    blk = pl.BlockSpec((tr, c), lambda i: (i, 0))
    return _call(
        body, name=name, grid=(r // tr,),
        in_specs=[blk] * 4, out_specs=[blk] * 3,
        out_shape=[jax.ShapeDtypeStruct((r, c), F32)] * 3,
        compiler_params=_cp(),
    )(w, g, m, v)


def _place():
    return lax.axis_index("x"), lax.axis_index("y"), lax.axis_index("c")


def _other_chips(x, y):
    return [(1 - x, y), (x, 1 - y), (1 - x, 1 - y)]


class _Exchange:
    inputs = ()
    out_shapes = ()
    scratch = ()

    def start(self, *refs):
        raise NotImplementedError

    def finish(self, *refs):
        raise NotImplementedError

    def alone(self, name):
        def body(*refs):
            self.start(*refs)
            self.finish(*refs)

        anywhere = pl.BlockSpec(memory_space=pl.ANY)
        return _call(
            body, name=name,
            in_specs=[anywhere] * len(self.inputs), out_specs=[anywhere] * len(self.out_shapes),
            out_shape=list(self.out_shapes), scratch_shapes=list(self.scratch),
        )(*self.inputs)


class _GatherWeights(_Exchange):
    def __init__(self, shard):
        self.r = shard.shape[0]
        self.inputs = (shard,)
        self.out_shapes = (jax.ShapeDtypeStruct((4, self.r, PACK_W), shard.dtype),)
        self.scratch = (pltpu.SemaphoreType.DMA((6,)), pltpu.SemaphoreType.DMA((6,)))

    def gathered(self, got, k):
        return lax.dynamic_update_slice(got, self.inputs[0][None], (k, 0, 0))

    def _copies(self, s_ref, g_ref, send_sems, recv_sems):
        half = self.r // 2
        x, y, c = _place()
        chips = _other_chips(x, y)

        def rows(px, py, pc):
            return g_ref.at[2 * px + py, pl.ds(pc * half, half), :]

        def copy(k, block, to, src=None):
            return pltpu.make_async_remote_copy(
                src_ref=rows(*block) if src is None else src, dst_ref=rows(*block),
                send_sem=send_sems.at[k], recv_sem=recv_sems.at[k], device_id=to, device_id_type=MESH)

        first = [copy(j, (x, y, c), (*chip, c), src=s_ref.at[pl.ds(c * half, half), :]) for j, chip in enumerate(chips)]
        passed = [copy(3 + j, (*chip, c), (x, y, 1 - c)) for j, chip in enumerate(chips)]
        landed = [copy(j, (*chip, c), (x, y, c)) for j, chip in enumerate(chips)]
        landed += [copy(3 + j, (*chip, 1 - c), (x, y, c)) for j, chip in enumerate(chips)]
        return first, passed, landed

    def start(self, *refs):
        first, _, _ = self._copies(*refs)
        for cp in first:
            cp.start()

    def finish(self, *refs):
        first, passed, landed = self._copies(*refs)
        for j in range(3):
            landed[j].wait_recv()
            passed[j].start()
        for j in range(3):
            landed[3 + j].wait_recv()
        for cp in first + passed:
            cp.wait_send()


def _swap_halves(name, gp):
    r = gp.shape[1]
    half = r // 2

    def body(g_ref, o_ref, send_sem, recv_sem):
        x, y, c = _place()
        cp = pltpu.make_async_remote_copy(
            src_ref=g_ref.at[:, pl.ds((1 - c) * half, half), :], dst_ref=o_ref,
            send_sem=send_sem, recv_sem=recv_sem, device_id=(x, y, 1 - c), device_id_type=MESH)
        cp.start()
        cp.wait()

    return _call(
        body, name=name,
        in_specs=[pl.BlockSpec(memory_space=pl.ANY)],
        out_specs=pl.BlockSpec(memory_space=pl.ANY),
        out_shape=jax.ShapeDtypeStruct((4, half, PACK_W), gp.dtype),
        scratch_shapes=[pltpu.SemaphoreType.DMA, pltpu.SemaphoreType.DMA],
    )(gp)


def _chip_sum(name, gp, got, c_arr):
    half = got.shape[1]
    tr = ADD_ROWS
    nb = half // tr

    def body(c_ref, a_ref, b_ref, o_ref, ob_ref):
        s = a_ref[...] + b_ref[...]
        o_ref[...] = s
        ob_ref[...] = s.astype(BF16)

    grid_spec = pltpu.PrefetchScalarGridSpec(
        num_scalar_prefetch=1, grid=(4, nb),
        in_specs=[pl.BlockSpec((None, tr, PACK_W), lambda s, i, c: (s, c[0] * nb + i, 0)),
                  pl.BlockSpec((None, tr, PACK_W), lambda s, i, c: (s, i, 0))],
        out_specs=[pl.BlockSpec((None, tr, PACK_W), lambda s, i, c: (s, i, 0)),
                   pl.BlockSpec((None, tr, PACK_W), lambda s, i, c: (s, i, 0))],
    )
    return _call(
        body, name=name, grid_spec=grid_spec,
        out_shape=[jax.ShapeDtypeStruct(got.shape, F32), jax.ShapeDtypeStruct(got.shape, BF16)],
        compiler_params=_cp(),
    )(c_arr, gp, got)


class _ScatterChipSums(_Exchange):
    def __init__(self, cs):
        self.inputs = (cs,)
        self.out_shapes = (jax.ShapeDtypeStruct((3,) + cs.shape[1:], cs.dtype),)
        self.scratch = (pltpu.SemaphoreType.DMA((3,)), pltpu.SemaphoreType.DMA((3,)))

    def _copies(self, s_ref, o_ref, send_sems, recv_sems):
        x, y, c = _place()
        return [pltpu.make_async_remote_copy(
            src_ref=s_ref.at[2 * px + py], dst_ref=o_ref.at[j],
            send_sem=send_sems.at[j], recv_sem=recv_sems.at[j], device_id=(px, py, c), device_id_type=MESH)
            for j, (px, py) in enumerate(_other_chips(x, y))]

    def start(self, *refs):
        for cp in self._copies(*refs):
            cp.start()

    def finish(self, *refs):
        for cp in self._copies(*refs):
            cp.wait()


def _shard_sum(name, cs, got, kc_arr):
    h = cs.shape[1]
    tr = ADD_ROWS
    nb = h // tr

    def body(k_ref, a_ref, b_ref, o_ref):
        o_ref[...] = ((a_ref[...] + b_ref[0].astype(F32)) + b_ref[1].astype(F32)) + b_ref[2].astype(F32)

    grid_spec = pltpu.PrefetchScalarGridSpec(
        num_scalar_prefetch=1, grid=(nb,),
        in_specs=[pl.BlockSpec((None, tr, PACK_W), lambda i, k: (k[0], i, 0)),
                  pl.BlockSpec((3, tr, PACK_W), lambda i, k: (0, i, 0))],
        out_specs=pl.BlockSpec((tr, PACK_W), lambda i, k: (k[1] * nb + i, 0)),
    )
    return _call(
        body, name=name, grid_spec=grid_spec,
        out_shape=jax.ShapeDtypeStruct((2 * h, PACK_W), F32),
        compiler_params=_cp(),
    )(kc_arr, cs, got)


def _join_halves(name, both):
    h = both.shape[0] // 2

    def body(m_ref, o_ref, send_sem, recv_sem):
        x, y, c = _place()
        cp = pltpu.make_async_remote_copy(
            src_ref=m_ref.at[pl.ds(c * h, h), :], dst_ref=o_ref.at[pl.ds(c * h, h), :],
            send_sem=send_sem, recv_sem=recv_sem, device_id=(x, y, 1 - c), device_id_type=MESH)
        cp.start()
        cp.wait_send()
        pltpu.make_async_remote_copy(
            src_ref=m_ref.at[pl.ds(c * h, h), :], dst_ref=o_ref.at[pl.ds((1 - c) * h, h), :],
            send_sem=send_sem, recv_sem=recv_sem, device_id=(x, y, 1 - c), device_id_type=MESH).wait_recv()

    return _call(
        body, name=name,
        in_specs=[pl.BlockSpec(memory_space=pl.ANY)],
        out_specs=pl.BlockSpec(memory_space=pl.ANY),
        out_shape=jax.ShapeDtypeStruct(both.shape, both.dtype),
        input_output_aliases={0: 0},
        scratch_shapes=[pltpu.SemaphoreType.DMA, pltpu.SemaphoreType.DMA],
    )(both)


def _all_reduce_small(v):
    r = v.shape[0]

    def body(v_ref, o_ref, buf, send_sems, recv_sems):
        x, y, c = _place()
        me = 4 * x + 2 * y + c
        buf[me] = v_ref[...]
        cps = []
        for k in range(1, 8):
            peer = (x ^ (k >> 2), y ^ ((k >> 1) & 1), c ^ (k & 1))
            cps.append(pltpu.make_async_remote_copy(
                src_ref=v_ref, dst_ref=buf.at[me],
                send_sem=send_sems.at[k - 1], recv_sem=recv_sems.at[k - 1], device_id=peer, device_id_type=MESH))
        for cp in cps:
            cp.start()
        for k in range(1, 8):
            pltpu.make_async_remote_copy(
                src_ref=v_ref, dst_ref=buf.at[me ^ k],
                send_sem=send_sems.at[k - 1], recv_sem=recv_sems.at[k - 1],
                device_id=(x, y, c), device_id_type=MESH).wait_recv()
        for cp in cps:
            cp.wait_send()
        acc = buf[0]
        for k in range(1, 8):
            acc = acc + buf[k]
        o_ref[...] = acc

    return _call(
        body, name="all_reduce_small",
        in_specs=[pl.BlockSpec(memory_space=pltpu.VMEM)],
        out_specs=pl.BlockSpec(memory_space=pltpu.VMEM),
        out_shape=jax.ShapeDtypeStruct((r, 128), F32),
        scratch_shapes=[pltpu.VMEM((8, r, 128), F32), pltpu.SemaphoreType.DMA((7,)), pltpu.SemaphoreType.DMA((7,))],
    )(v)


def _group(names):
    return tuple(e for e in BIG if e[0] in names)


def _pack(shards, dtype):
    return jnp.concatenate([s.astype(dtype).reshape(-1, PACK_W) for s in shards], axis=0)


def _unpack_full(g, group):
    out, at = {}, 0
    for name, rows, cols, axis in group:
        n = rows * cols // 4 // PACK_W
        blk = g[:, at:at + n, :]
        at += n
        if axis == 1:
            out[name] = blk.reshape(4, rows, cols // 4).transpose(1, 0, 2).reshape(rows, cols)
        else:
            out[name] = blk.reshape(rows, cols)
    return out


def _pack_grads(grads, group):
    parts = []
    for name, rows, cols, axis in group:
        g = grads[name]
        if axis == 1:
            g = g.reshape(rows, 4, cols // 4).transpose(1, 0, 2)
        parts.append(g.reshape(4, -1, PACK_W))
    rows_total = sum(p.shape[1] for p in parts)
    pad = -rows_total % PACK_ALIGN
    if pad:
        parts.append(jnp.zeros((4, pad, PACK_W), F32))
    return jnp.concatenate(parts, axis=1)


def _unpack_shard(s, group):
    out, at = {}, 0
    for name, rows, cols, axis in group:
        n = rows * cols // 4 // PACK_W
        shape = (rows, cols // 4) if axis == 1 else (rows // 4, cols)
        out[name] = s[at:at + n, :].reshape(shape)
        at += n
    return out


def _pack_small(parts):
    flat = jnp.concatenate([p.reshape(-1) for p in parts])
    pad = -flat.shape[0] % 1024
    return jnp.concatenate([flat, jnp.zeros((pad,), F32)]).reshape(-1, 128)


def _ffn_fwd(tag, h, gain, w_in, w_out, side=None):
    t = h.shape[0]
    tm = min(TM, t)
    n = _rms_fwd(tag + "_norm", h, gain)
    tn = 256

    def epi(accs, _):
        gate, up = accs
        return gate, up, _silu(gate) * up

    gate, up, act, *side_out = _mm(tag + "_in", [_a_spec(n, tm)], [_b_nn(w_in, tn), _b_nn(w_in, tn, DFF // tn)],
                                   [(0, 0), (0, 1)], epi, [], [BF16, BF16, BF16], t, DFF, tm, tn, side=side)
    (out,) = _mm(tag + "_out", [_a_spec(act, tm)], [_b_nn(w_out, 512)], [(0, 0)],
                 lambda accs, ex: (ex[0] + 0.5 * accs[0],), [_e_tile(h, tm, 512)], [F32], t, D, tm, 512)
    return out, (n, gate, up, act), side_out


class _Reduction:
    def __init__(self, tag, group, c_arr, k_arr):
        self.tag, self.group, self.c_arr, self.k_arr = tag, group, c_arr, k_arr

    def begin(self, grads):
        gp = _pack_grads(grads, self.group)
        self.sums, sums_bf16 = _chip_sum("grad_chip_sum_" + self.tag, gp, _swap_halves("grad_swap_" + self.tag, gp), self.c_arr)
        return _ScatterChipSums(sums_bf16)

    def end(self, got):
        mine = _shard_sum("grad_shard_sum_" + self.tag, self.sums, got, self.k_arr)
        return _unpack_shard(_join_halves("grad_join_" + self.tag, mine), self.group)


def _ffn_bwd(tag, h, gain, w_in, w_out, saved, dout, side, reduction):
    t = h.shape[0]
    tm = min(TM, t)
    n, gate, up, act = saved
    tn = 256

    def epi(accs, ex):
        da = 0.5 * accs[0]
        g, u = ex[0].astype(F32), ex[1].astype(F32)
        return da * u * _dsilu(g), da * _silu(g)

    dgate, dup, *side_out = _mm(tag + "_dact", [_a_spec(dout, tm)], [_b_nt(w_out, tn)], [(0, 0)], epi,
                                [_e_tile(gate, tm, tn), _e_tile(up, tm, tn)], [BF16, BF16], t, DFF, tm, tn,
                                trans_b=True, side=side)
    dw_out = _mm_tn(tag + "_dw_out", act, dout, scale=0.5, tm=DFF // 2, tn=D)
    dw_g = _mm_tn(tag + "_dw_gate", n, dgate, tm=D, tn=DFF // 2)
    dw_u = _mm_tn(tag + "_dw_up", n, dup, tm=D, tn=DFF // 2)
    sending = reduction.begin({tag + "_w_in": jnp.concatenate([dw_g, dw_u], axis=1), tag + "_w_out": dw_out})
    dn, got = _mm(tag + "_dn", [_a_spec(dgate, tm), _a_spec(dup, tm)],
                  [_b_nt(w_in, 512, DFF, 0), _b_nt(w_in, 512, DFF, 1)], [(0, 0), (1, 1)],
                  lambda accs, ex: (accs[0] + accs[1],), [], [F32], t, D, tm, 512, trans_b=True, side=sending)
    dh, dgain = _rms_bwd(tag + "_dnorm", h, gain, dn, dout)
    return dh, dgain, side_out, got


def kernel(x, positions, ffn1_norm, ffn1_w_in, ffn1_w_out, mix_norm, w_in, hg_lb_table, hg_out_norm, w_hg_branch, mla_q_lora_norm, w_q_up, mla_kv_lora_norm, w_kv_up, q_head_norm, k_head_norm, w_mla_branch, w_merge, b_merge, w_out, ffn2_norm, ffn2_w_in, ffn2_w_out, final_norm, loss_target, m_ffn1_norm, m_ffn1_w_in, m_ffn1_w_out, m_mix_norm, m_w_in, m_hg_lb_table, m_hg_out_norm, m_w_hg_branch, m_mla_q_lora_norm, m_w_q_up, m_mla_kv_lora_norm, m_w_kv_up, m_q_head_norm, m_k_head_norm, m_w_mla_branch, m_w_merge, m_b_merge, m_w_out, m_ffn2_norm, m_ffn2_w_in, m_ffn2_w_out, m_final_norm, v_ffn1_norm, v_ffn1_w_in, v_ffn1_w_out, v_mix_norm, v_w_in, v_hg_lb_table, v_hg_out_norm, v_w_hg_branch, v_mla_q_lora_norm, v_w_q_up, v_mla_kv_lora_norm, v_w_kv_up, v_q_head_norm, v_k_head_norm, v_w_mla_branch, v_w_merge, v_b_merge, v_w_out, v_ffn2_norm, v_ffn2_w_in, v_ffn2_w_out, v_final_norm):
    a = dict(locals())
    w = {n: a[n] for n in WEIGHT_ORDER}
    mom = {n: a["m_" + n] for n in WEIGHT_ORDER}
    var = {n: a["v_" + n] for n in WEIGHT_ORDER}
    t = x.shape[1]
    tm = min(TM, t)
    xt = x.reshape(t, D)
    target = loss_target.reshape(t, D)
    pos = positions.reshape(t, 1)
    x_i, y_i, c_i = _place()
    k_idx = (2 * x_i + y_i).astype(jnp.int32)
    c_arr = c_i.astype(jnp.int32).reshape(1)
    k_arr = jnp.stack([k_idx, c_i.astype(jnp.int32)])

    group_first = _group(("ffn1_w_in", "ffn1_w_out"))
    group_mid = _group(("w_in", "w_hg_branch", "w_q_up", "w_kv_up", "w_mla_branch", "w_merge", "w_out"))
    group_last = _group(("ffn2_w_in", "ffn2_w_out"))
    gather_first = _GatherWeights(_pack([w[e[0]][0] for e in group_first], BF16))
    gather_rest = _GatherWeights(_pack([w[e[0]][0] for e in group_mid + group_last], BF16))
    (got,) = gather_first.alone("gather_first")
    full = _unpack_full(gather_first.gathered(got, k_idx), group_first)
    h1, ffn1_saved, (got,) = _ffn_fwd(
        "ffn1", xt, w["ffn1_norm"], full["ffn1_w_in"], full["ffn1_w_out"], side=gather_rest)
    full.update(_unpack_full(gather_rest.gathered(got, k_idx), group_mid + group_last))
    w_in_full = full["w_in"]
    w_in_hg = w_in_full[:, :4 * D]
    w_in_mla = jnp.pad(w_in_full[:, 4 * D:], ((0, 0), (0, MLA_COLS - (4800 - 4 * D))))
    w_q_pad = jnp.pad(full["w_q_up"].reshape(Q_LORA, HEADS, QK), ((0, 0), (0, 0), (0, QKP - QK))).reshape(Q_LORA, HEADS * QKP)
    w_kv = full["w_kv_up"]
    gq = jnp.pad(w["q_head_norm"], ((0, 0), (0, QKP - QK)))
    gk = jnp.pad(w["k_head_norm"], ((0, 0), (0, QKP - QK)))

    u = _rms_fwd("mix_norm", h1, w["mix_norm"])
    ident = lambda accs, ex: (accs[0],)
    (p_hg,) = _mm("in_hg", [_a_spec(u, tm)], [_b_nn(w_in_hg, 512)], [(0, 0)], ident, [], [F32], t, 4 * D, tm, 512)
    (p_mla,) = _mm("in_mla", [_a_spec(u, tm)], [_b_nn(w_in_mla, MLA_COLS)], [(0, 0)], ident, [], [F32], t, MLA_COLS, tm, MLA_COLS)
    o_raw, hg_o, states = _hgrn_fwd(p_hg, w["hg_lb_table"], w["hg_out_norm"])
    (y_hg,) = _mm("hg_branch", [_a_spec(hg_o, tm)], [_b_nn(full["w_hg_branch"], 512)], [(0, 0)], ident, [], [BF16], t, D, tm, 512)
    cqn, ckvn = _lora_norm_fwd(p_mla, w["mla_q_lora_norm"], w["mla_kv_lora_norm"])
    (qf,) = _mm("q_up", [_a_spec(cqn, tm)], [_b_nn(w_q_pad, 512)], [(0, 0)], ident, [], [F32], t, HEADS * QKP, tm, 512)
    (kvf,) = _mm("kv_up", [_a_spec(ckvn, tm)], [_b_nn(w_kv, 512)], [(0, 0)], ident, [], [F32], t, HEADS * QKP, tm, 512)
    cos, sin = _rope_tables(pos)
    qh, kh, vh = _mla_prep_fwd(qf, kvf, p_mla, cos, sin, gq, gk)
    o_mla, lse = _flash_fwd(qh, kh, vh)
    (y_mla,) = _mm("mla_branch", [_a_spec(o_mla, tm)], [_b_nn(full["w_mla_branch"], 512)], [(0, 0)], ident, [], [BF16], t, D, tm, 512)

    def merge_epi(accs, ex):
        g_hg = _sig(accs[0] + ex[2])
        g_mla = _sig(accs[1] + ex[3])
        return g_hg * ex[0].astype(F32) + g_mla * ex[1].astype(F32), g_hg, g_mla

    w_merge_f = full["w_merge"]
    mix, g_hg, g_mla = _mm(
        "merge", [_a_spec(u, tm)], [_b_nn(w_merge_f, 512), _b_nn(w_merge_f, 512, D // 512)], [(0, 0), (0, 1)], merge_epi,
        [_e_tile(y_hg, tm, 512), _e_tile(y_mla, tm, 512), _e_row(w["b_merge"], 512), _e_row(w["b_merge"], 512, D // 512)],
        [BF16, BF16, BF16], t, D, tm, 512)
    (h2,) = _mm("out_proj", [_a_spec(mix, tm)], [_b_nn(full["w_out"], 512)], [(0, 0)],
                lambda accs, ex: (ex[0] + accs[0],), [_e_tile(h1, tm, 512)], [F32], t, D, tm, 512)
    h3, ffn2_saved, _ = _ffn_fwd("ffn2", h2, w["ffn2_norm"], full["ffn2_w_in"], full["ffn2_w_out"])
    dh3, d_final_norm, loss_part = _final_loss(h3, target, w["final_norm"])

    grads, small = {}, {}
    small["final_norm"] = d_final_norm
    reduce_last = _Reduction("last", group_last, c_arr, k_arr)
    reduce_mid = _Reduction("mid", group_mid, c_arr, k_arr)
    reduce_first = _Reduction("first", group_first, c_arr, k_arr)
    dh2, small["ffn2_norm"], _, got_last = _ffn_bwd(
        "ffn2", h2, w["ffn2_norm"], full["ffn2_w_in"], full["ffn2_w_out"], ffn2_saved, dh3, None, reduce_last)

    def dmix_epi(accs, ex):
        dm = accs[0]
        ghg, gml, yhg, yml = [e.astype(F32) for e in ex]
        return dm * ghg, dm * gml, dm * yhg * ghg * (1.0 - ghg), dm * yml * gml * (1.0 - gml)

    dy_hg, dy_mla, dpre_hg, dpre_mla = _mm(
        "d_mix", [_a_spec(dh2, tm)], [_b_nt(full["w_out"], 512)], [(0, 0)], dmix_epi,
        [_e_tile(g_hg, tm, 512), _e_tile(g_mla, tm, 512), _e_tile(y_hg, tm, 512), _e_tile(y_mla, tm, 512)],
        [BF16, BF16, BF16, BF16], t, D, tm, 512, trans_b=True)
    grads["w_out"] = _mm_tn("dw_out", mix, dh2)
    small["b_merge"] = jnp.concatenate([_colsum("db_hg", dpre_hg), _colsum("db_mla", dpre_mla)], axis=1)
    grads["w_merge"] = jnp.concatenate([_mm_tn("dw_merge_hg", u, dpre_hg), _mm_tn("dw_merge_mla", u, dpre_mla)], axis=1)
    grads["w_hg_branch"] = _mm_tn("dw_hg_branch", hg_o, dy_hg)
    grads["w_mla_branch"] = _mm_tn("dw_mla_branch", o_mla, dy_mla)
    (dho,) = _mm("d_hg_o", [_a_spec(dy_hg, tm)], [_b_nt(full["w_hg_branch"], 512)], [(0, 0)], ident, [], [BF16], t, D, tm, 512, trans_b=True)
    (do_mla,) = _mm("d_o_mla", [_a_spec(dy_mla, tm)], [_b_nt(full["w_mla_branch"], 512)], [(0, 0)], ident, [], [BF16], t, D, tm, 512, trans_b=True)

    dq_raw, df_raw, di_raw, dg_raw, small["hg_lb_table"], small["hg_out_norm"] = _hgrn_bwd(
        p_hg, w["hg_lb_table"], w["hg_out_norm"], o_raw, states, dho)
    dp_hg = [dq_raw, df_raw, di_raw, dg_raw]

    dqh, dkh, dvh = _flash_bwd(qh, kh, vh, lse, _attn_delta(do_mla, o_mla), do_mla)
    dqf, dkvf, dkpe, dgq, dgk = _mla_prep_bwd(qf, kvf, p_mla, cos, sin, gq, gk, dqh, dkh, dvh)
    small["q_head_norm"] = dgq[:, :QK]
    small["k_head_norm"] = dgk[:, :QK]
    dwq_pad = _mm_tn("dw_q_up", cqn, dqf, tm=Q_LORA, tn=1024)
    grads["w_q_up"] = dwq_pad.reshape(Q_LORA, HEADS, QKP)[:, :, :QK].reshape(Q_LORA, HEADS * QK)
    grads["w_kv_up"] = _mm_tn("dw_kv_up", ckvn, dkvf, tm=KV_LORA, tn=1024)
    (dcqn,) = _mm("d_cq", [_a_spec(dqf, tm)], [_b_nt(w_q_pad, Q_LORA)], [(0, 0)], ident, [], [F32], t, Q_LORA, tm, Q_LORA, trans_b=True)
    (dckvn,) = _mm("d_ckv", [_a_spec(dkvf, tm)], [_b_nt(w_kv, KV_LORA)], [(0, 0)], ident, [], [F32], t, KV_LORA, tm, KV_LORA, trans_b=True)
    dp_mla, small["mla_q_lora_norm"], small["mla_kv_lora_norm"] = _lora_norm_bwd(
        p_mla, w["mla_q_lora_norm"], w["mla_kv_lora_norm"], dcqn, dckvn, dkpe)

    dw_in_hg = [_mm_tn("dw_in_hg%d" % k, u, dp_hg[k]) for k in range(4)]
    dw_in_mla = _mm_tn("dw_in_mla", u, dp_mla, tn=MLA_COLS)
    grads["w_in"] = jnp.concatenate(dw_in_hg + [dw_in_mla[:, :4800 - 4 * D]], axis=1)
    tm_du = min(TM // 2, t)
    (du,) = _mm(
        "d_u",
        [_a_spec(dpre_hg, tm_du), _a_spec(dpre_mla, tm_du)] + [_a_spec(d, tm_du) for d in dp_hg] + [_a_spec(dp_mla, tm_du)],
        [_b_nt(w_merge_f, 512, D, 0), _b_nt(w_merge_f, 512, D, 1)]
        + [_b_nt(w_in_hg, 512, D, k) for k in range(4)] + [_b_nt(w_in_mla, 512)],
        [(k, k) for k in range(7)],
        lambda accs, ex: (functools.reduce(lambda p, q: p + q, accs),), [], [F32], t, D, tm_du, 512, trans_b=True)
    dh1, small["mix_norm"] = _rms_bwd("mix_dnorm", h1, w["mix_norm"], du, dh2)
    dx, small["ffn1_norm"], (got_mid,), got_first = _ffn_bwd(
        "ffn1", xt, w["ffn1_norm"], full["ffn1_w_in"], full["ffn1_w_out"], ffn1_saved, dh1,
        reduce_mid.begin(grads), reduce_first)

    g_shard = {**reduce_last.end(got_last), **reduce_mid.end(got_mid), **reduce_first.end(got_first)}
    small_sum = _all_reduce_small(_pack_small([small[n] for n, _ in SMALL] + [loss_part])).reshape(-1)
    g_small, at = {}, 0
    for n, shape in SMALL:
        size = shape[0] * shape[1]
        g_small[n] = small_sum[at:at + size].reshape(shape)
        at += size
    loss = small_sum[at]

    g_out, d_out, m_out, v_out = {}, {}, {}, {}
    for n in WEIGHT_ORDER:
        shape = w[n].shape
        g = g_shard[n] if n in g_shard else g_small[n]
        two = g.shape
        d_, m_, v_ = _adamw("adamw_" + n, w[n].reshape(two), g, mom[n].reshape(two), var[n].reshape(two))
        g_out[n], d_out[n], m_out[n], v_out[n] = g.reshape(shape), d_.reshape(shape), m_.reshape(shape), v_.reshape(shape)

    return (loss, dx.reshape(x.shape), *[g_out[n] for n in WEIGHT_ORDER], *[d_out[n] for n in WEIGHT_ORDER],
            *[m_out[n] for n in WEIGHT_ORDER], *[v_out[n] for n in WEIGHT_ORDER])
```

```python
import functools

import numpy as np
import jax
import jax.numpy as jnp
from jax import lax
from jax.experimental import pallas as pl
from jax.experimental.pallas import tpu as pltpu

F32 = jnp.float32
BF16 = jnp.bfloat16
MESH = pl.DeviceIdType.MESH

D = 1024
DFF = 2816
HEADS = 8
HK = 128
CHUNK = 64
ROPE = 64
QK = 192
QKP = 256
Q_LORA = 384
KV_LORA = 256
MLA_COLS = 768
EPS = 1e-6
ROPE_THETA = 10000.0
SCALE = QK ** -0.5
LOG2E = 1.4426950408889634
LN2 = 0.6931471805599453
NEG = -1e30
EXP_CLAMP = 80.0

ADAM_LR = 0.001
ADAM_B1 = 0.9
ADAM_B2 = 0.999
ADAM_EPS = 1e-08
ADAM_WD = 0.01
ADAM_STEP = 10

PACK_W = 1024
ADD_ROWS = 352
PACK_ALIGN = 2 * ADD_ROWS

TM = 1024
FFN_TM = 512
FFN_CHUNK = 256
TQ = 1024
SUBQ = 512
HG_BT = 512
HG_HPB = 4
TT = 512
ROW_TM = 256

VMEM_MB = 48

BIG = (
    ("ffn1_w_in", D, 2 * DFF, 1),
    ("ffn1_w_out", DFF, D, 0),
    ("w_in", D, 4800, 1),
    ("w_hg_branch", D, D, 0),
    ("w_q_up", Q_LORA, HEADS * QK, 1),
    ("w_kv_up", KV_LORA, HEADS * 2 * HK, 1),
    ("w_mla_branch", D, D, 0),
    ("w_merge", D, 2 * D, 1),
    ("w_out", D, D, 0),
    ("ffn2_w_in", D, 2 * DFF, 1),
    ("ffn2_w_out", DFF, D, 0),
)
SMALL = (
    ("ffn1_norm", (1, D)),
    ("mix_norm", (1, D)),
    ("hg_lb_table", (2, D)),
    ("hg_out_norm", (1, HK)),
    ("mla_q_lora_norm", (1, Q_LORA)),
    ("mla_kv_lora_norm", (1, KV_LORA)),
    ("q_head_norm", (1, QK)),
    ("k_head_norm", (1, QK)),
    ("b_merge", (1, 2 * D)),
    ("ffn2_norm", (1, D)),
    ("final_norm", (1, D)),
)
WEIGHT_ORDER = ("ffn1_norm", "ffn1_w_in", "ffn1_w_out", "mix_norm", "w_in", "hg_lb_table", "hg_out_norm",
                "w_hg_branch", "mla_q_lora_norm", "w_q_up", "mla_kv_lora_norm", "w_kv_up", "q_head_norm",
                "k_head_norm", "w_mla_branch", "w_merge", "b_merge", "w_out", "ffn2_norm", "ffn2_w_in",
                "ffn2_w_out", "final_norm")


def _call(body, **kw):
    return pl.pallas_call(body, **kw)


def _cp(vmem_mb=VMEM_MB):
    return pltpu.CompilerParams(vmem_limit_bytes=vmem_mb << 20)


def _dot(a, b):
    return lax.dot_general(a, b, (((1,), (0,)), ((), ())), preferred_element_type=F32)


def _dot_nt(a, b):
    return lax.dot_general(a, b, (((1,), (1,)), ((), ())), preferred_element_type=F32)


def _dot_tn(a, b):
    return lax.dot_general(a, b, (((0,), (0,)), ((), ())), preferred_element_type=F32)


def _sig(x):
    return jax.nn.sigmoid(x)


def _silu(x):
    return x * _sig(x)


def _dsilu(x):
    s = _sig(x)
    return s * (1.0 + x * (1.0 - s))


def _a_spec(arr, tm, kblk=None, kidx=0):
    kb = arr.shape[1] if kblk is None else kblk
    return arr, pl.BlockSpec((tm, kb), lambda i, j, kidx=kidx: (i, kidx))


def _b_nn(arr, tn, off=0):
    return arr, pl.BlockSpec((arr.shape[0], tn), lambda i, j, off=off: (0, j + off))


def _b_nt(arr, tn, kblk=None, kidx=0):
    kb = arr.shape[1] if kblk is None else kblk
    return arr, pl.BlockSpec((tn, kb), lambda i, j, kidx=kidx: (j, kidx))


def _e_tile(arr, tm, tn, off=0):
    return arr, pl.BlockSpec((tm, tn), lambda i, j, off=off: (i, j + off))


def _e_row(arr, tn, off=0):
    return arr, pl.BlockSpec((1, tn), lambda i, j, off=off: (0, j + off))


def _mm(name, As, Bs, dots, epi, extras, out_dtypes, m, n, tm, tn, trans_b=False, side=None):
    na, nb, ne, no = len(As), len(Bs), len(extras), len(out_dtypes)
    ni, nj = m // tm, n // tn
    s_in = len(side.inputs) if side else 0
    s_out = len(side.out_shapes) if side else 0

    def body(*refs):
        a_refs = refs[:na]
        b_refs = refs[na:na + nb]
        e_refs = refs[na + nb:na + nb + ne]
        at = na + nb + ne
        side_refs = refs[at:at + s_in]
        o_refs = refs[at + s_in:at + s_in + no]
        side_refs = list(side_refs) + list(refs[at + s_in + no:])
        if side:
            i, j = pl.program_id(0), pl.program_id(1)

            @pl.when(jnp.logical_and(i == 0, j == 0))
            def _():
                side.start(*side_refs)

        a_vals = [r[...].astype(BF16) for r in a_refs]
        accs = []
        for ai, bi in dots:
            b = b_refs[bi][...]
            accs.append(_dot_nt(a_vals[ai], b) if trans_b else _dot(a_vals[ai], b))
        outs = epi(accs, [r[...] for r in e_refs])
        for o_ref, o in zip(o_refs, outs):
            o_ref[...] = o.astype(o_ref.dtype)
        if side:
            @pl.when(jnp.logical_and(i == ni - 1, j == nj - 1))
            def _():
                side.finish(*side_refs)

    ops = list(As) + list(Bs) + list(extras)
    anywhere = pl.BlockSpec(memory_space=pl.ANY)
    res = _call(
        body, name=name,
        grid=(ni, nj),
        in_specs=[s for _, s in ops] + [anywhere] * s_in,
        out_specs=[pl.BlockSpec((tm, tn), lambda i, j: (i, j)) for _ in out_dtypes] + [anywhere] * s_out,
        out_shape=[jax.ShapeDtypeStruct((m, n), dt) for dt in out_dtypes] + (list(side.out_shapes) if side else []),
        scratch_shapes=list(side.scratch) if side else [],
        compiler_params=_cp(),
    )(*[a for a, _ in ops], *(side.inputs if side else []))
    return res


def _rows_call(name, rows, weights, outs, compute, tm, side=None):
    t = rows[0].shape[0]
    nr, nw, no = len(rows), len(weights), len(outs)
    ni = t // tm
    s_in = len(side.inputs) if side else 0
    s_out = len(side.out_shapes) if side else 0

    def body(*refs):
        at = nr + nw
        side_refs = list(refs[at:at + s_in]) + list(refs[at + s_in + no:])
        if side:
            @pl.when(pl.program_id(0) == 0)
            def _():
                side.start(*side_refs)

        compute(refs[:nr], refs[nr:at], refs[at + s_in:at + s_in + no])
        if side:
            @pl.when(pl.program_id(0) == ni - 1)
            def _():
                side.finish(*side_refs)

    anywhere = pl.BlockSpec(memory_space=pl.ANY)
    return _call(
        body, name=name, grid=(ni,),
        in_specs=[pl.BlockSpec((tm, r.shape[1]), lambda i: (i, 0)) for r in rows]
        + [pl.BlockSpec(wt.shape, lambda i: (0, 0)) for wt in weights] + [anywhere] * s_in,
        out_specs=[pl.BlockSpec((tm, width), lambda i: (i, 0)) for width, _ in outs] + [anywhere] * s_out,
        out_shape=[jax.ShapeDtypeStruct((t, width), dt) for width, dt in outs] + (list(side.out_shapes) if side else []),
        scratch_shapes=list(side.scratch) if side else [],
        compiler_params=_cp(),
    )(*rows, *weights, *(side.inputs if side else []))


def _mm_tn(name, a, b, scale=1.0, tm=1024, tn=1024):
    t, m = a.shape
    n = b.shape[1]
    tm, tn, tt = min(tm, m), min(tn, n), min(TT, t)
    nk = t // tt

    def body(a_ref, b_ref, o_ref):
        k = pl.program_id(2)

        @pl.when(k == 0)
        def _():
            o_ref[...] = jnp.zeros_like(o_ref)

        o_ref[...] += _dot_tn(a_ref[...].astype(BF16), b_ref[...].astype(BF16))
        if scale != 1.0:
            @pl.when(k == nk - 1)
            def _():
                o_ref[...] = o_ref[...] * scale

    return _call(
        body, name=name,
        grid=(m // tm, n // tn, nk),
        in_specs=[pl.BlockSpec((tt, tm), lambda i, j, k: (k, i)), pl.BlockSpec((tt, tn), lambda i, j, k: (k, j))],
        out_specs=pl.BlockSpec((tm, tn), lambda i, j, k: (i, j)),
        out_shape=jax.ShapeDtypeStruct((m, n), F32),
        compiler_params=_cp(),
    )(a, b)


def _rms_fwd(name, x, gain):
    t, d = x.shape
    tm = min(ROW_TM, t)

    def body(x_ref, g_ref, o_ref):
        xv = x_ref[...]
        r = lax.rsqrt(jnp.mean(xv * xv, axis=-1, keepdims=True) + EPS)
        o_ref[...] = (xv * r * g_ref[...]).astype(o_ref.dtype)

    return _call(
        body, name=name, grid=(t // tm,),
        in_specs=[pl.BlockSpec((tm, d), lambda i: (i, 0)), pl.BlockSpec((1, d), lambda i: (0, 0))],
        out_specs=pl.BlockSpec((tm, d), lambda i: (i, 0)),
        out_shape=jax.ShapeDtypeStruct((t, d), BF16),
        compiler_params=_cp(),
    )(x, gain)


def _rms_bwd_vals(xv, g, dn):
    r = lax.rsqrt(jnp.mean(xv * xv, axis=-1, keepdims=True) + EPS)
    xh = xv * r
    dxh = dn * g
    c = jnp.mean(dxh * xh, axis=-1, keepdims=True)
    return r * (dxh - xh * c), dn * xh


def _rms_bwd(name, x, gain, dn, dres):
    t, d = x.shape
    tm = min(ROW_TM, t)

    def body(x_ref, g_ref, dn_ref, dr_ref, dx_ref, dg_ref):
        @pl.when(pl.program_id(0) == 0)
        def _():
            dg_ref[...] = jnp.zeros_like(dg_ref)

        dx, dg = _rms_bwd_vals(x_ref[...], g_ref[...], dn_ref[...].astype(F32))
        dx_ref[...] = dr_ref[...] + dx
        dg_ref[...] += jnp.sum(dg, axis=0, keepdims=True)

    row = pl.BlockSpec((tm, d), lambda i: (i, 0))
    one = pl.BlockSpec((1, d), lambda i: (0, 0))
    return _call(
        body, name=name, grid=(t // tm,),
        in_specs=[row, one, row, row],
        out_specs=[row, one],
        out_shape=[jax.ShapeDtypeStruct((t, d), F32), jax.ShapeDtypeStruct((1, d), F32)],
        compiler_params=_cp(),
    )(x, gain, dn, dres)


def _final_loss(h, target, gain):
    t, d = h.shape
    tm = min(ROW_TM, t)

    def body(h_ref, t_ref, g_ref, dh_ref, dg_ref, l_ref):
        @pl.when(pl.program_id(0) == 0)
        def _():
            dg_ref[...] = jnp.zeros_like(dg_ref)
            l_ref[...] = jnp.zeros_like(l_ref)

        hv = h_ref[...]
        g = g_ref[...]
        r = lax.rsqrt(jnp.mean(hv * hv, axis=-1, keepdims=True) + EPS)
        xh = hv * r
        err = xh * g - t_ref[...]
        l_ref[...] += 0.5 * jnp.sum(jnp.mean(err * err, axis=-1, keepdims=True), axis=0, keepdims=True)
        dy = err * (1.0 / d)
        dxh = dy * g
        c = jnp.mean(dxh * xh, axis=-1, keepdims=True)
        dh_ref[...] = r * (dxh - xh * c)
        dg_ref[...] += jnp.sum(dy * xh, axis=0, keepdims=True)

    row = pl.BlockSpec((tm, d), lambda i: (i, 0))
    one = pl.BlockSpec((1, d), lambda i: (0, 0))
    return _call(
        body, name="final_loss", grid=(t // tm,),
        in_specs=[row, row, one],
        out_specs=[row, one, pl.BlockSpec((1, 128), lambda i: (0, 0))],
        out_shape=[jax.ShapeDtypeStruct((t, d), F32), jax.ShapeDtypeStruct((1, d), F32),
                   jax.ShapeDtypeStruct((1, 128), F32)],
        compiler_params=_cp(),
    )(h, target, gain)


def _colsum(name, x):
    t, n = x.shape
    tm = min(TM, t)

    def body(x_ref, o_ref):
        @pl.when(pl.program_id(0) == 0)
        def _():
            o_ref[...] = jnp.zeros_like(o_ref)

        o_ref[...] += jnp.sum(x_ref[...].astype(F32), axis=0, keepdims=True)

    return _call(
        body, name=name, grid=(t // tm,),
        in_specs=[pl.BlockSpec((tm, n), lambda i: (i, 0))],
        out_specs=pl.BlockSpec((1, n), lambda i: (0, 0)),
        out_shape=jax.ShapeDtypeStruct((1, n), F32),
        compiler_params=_cp(),
    )(x)


def _lora_norm_fwd(p_mla, gq, gkv):
    t = p_mla.shape[0]
    tm = min(ROW_TM, t)

    def body(p_ref, gq_ref, gkv_ref, q_ref, kv_ref):
        cq = p_ref[:, 0:Q_LORA]
        ckv = p_ref[:, Q_LORA:Q_LORA + KV_LORA]
        rq = lax.rsqrt(jnp.mean(cq * cq, axis=-1, keepdims=True) + EPS)
        rkv = lax.rsqrt(jnp.mean(ckv * ckv, axis=-1, keepdims=True) + EPS)
        q_ref[...] = (cq * rq * gq_ref[...]).astype(BF16)
        kv_ref[...] = (ckv * rkv * gkv_ref[...]).astype(BF16)

    return _call(
        body, name="lora_norm_fwd", grid=(t // tm,),
        in_specs=[pl.BlockSpec((tm, MLA_COLS), lambda i: (i, 0)),
                  pl.BlockSpec((1, Q_LORA), lambda i: (0, 0)), pl.BlockSpec((1, KV_LORA), lambda i: (0, 0))],
        out_specs=[pl.BlockSpec((tm, Q_LORA), lambda i: (i, 0)), pl.BlockSpec((tm, KV_LORA), lambda i: (i, 0))],
        out_shape=[jax.ShapeDtypeStruct((t, Q_LORA), BF16), jax.ShapeDtypeStruct((t, KV_LORA), BF16)],
        compiler_params=_cp(),
    )(p_mla, gq, gkv)


def _lora_norm_bwd(p_mla, gq, gkv, dcqn, dckvn, dkpe):
    t = p_mla.shape[0]
    tm = min(ROW_TM, t)

    def body(p_ref, gq_ref, gkv_ref, dq_ref, dkv_ref, dkpe_ref, dp_ref, dgq_ref, dgkv_ref):
        @pl.when(pl.program_id(0) == 0)
        def _():
            dgq_ref[...] = jnp.zeros_like(dgq_ref)
            dgkv_ref[...] = jnp.zeros_like(dgkv_ref)

        dcq, dgq = _rms_bwd_vals(p_ref[:, 0:Q_LORA], gq_ref[...], dq_ref[...])
        dckv, dgkv = _rms_bwd_vals(p_ref[:, Q_LORA:Q_LORA + KV_LORA], gkv_ref[...], dkv_ref[...])
        dp_ref[:, 0:Q_LORA] = dcq.astype(BF16)
        dp_ref[:, Q_LORA:Q_LORA + KV_LORA] = dckv.astype(BF16)
        dp_ref[:, Q_LORA + KV_LORA:MLA_COLS] = dkpe_ref[...].astype(BF16)
        dgq_ref[...] += jnp.sum(dgq, axis=0, keepdims=True)
        dgkv_ref[...] += jnp.sum(dgkv, axis=0, keepdims=True)

    return _call(
        body, name="lora_norm_bwd", grid=(t // tm,),
        in_specs=[pl.BlockSpec((tm, MLA_COLS), lambda i: (i, 0)),
                  pl.BlockSpec((1, Q_LORA), lambda i: (0, 0)), pl.BlockSpec((1, KV_LORA), lambda i: (0, 0)),
                  pl.BlockSpec((tm, Q_LORA), lambda i: (i, 0)), pl.BlockSpec((tm, KV_LORA), lambda i: (i, 0)),
                  pl.BlockSpec((tm, HK), lambda i: (i, 0))],
        out_specs=[pl.BlockSpec((tm, MLA_COLS), lambda i: (i, 0)),
                   pl.BlockSpec((1, Q_LORA), lambda i: (0, 0)), pl.BlockSpec((1, KV_LORA), lambda i: (0, 0))],
        out_shape=[jax.ShapeDtypeStruct((t, MLA_COLS), BF16), jax.ShapeDtypeStruct((1, Q_LORA), F32),
                   jax.ShapeDtypeStruct((1, KV_LORA), F32)],
        compiler_params=_cp(),
    )(p_mla, gq, gkv, dcqn, dckvn, dkpe)


def _cumsum_rows(x, row):
    for s in (1, 2, 4, 8, 16, 32):
        x = x + jnp.where(row >= s, pltpu.roll(x, s, 0), 0.0)
    return x


def _rcumsum_rows(x, row):
    for s in (1, 2, 4, 8, 16, 32):
        x = x + jnp.where(row < CHUNK - s, pltpu.roll(x, CHUNK - s, 0), 0.0)
    return x


def _hg_gates(qr, z, lb, row):
    q = _silu(qr)
    sg = _sig(z)
    f = lb + (1.0 - lb) * sg
    lf = jnp.log(f)
    k = (1.0 - lb) * (1.0 - sg)
    cum = _cumsum_rows(lf, row)
    mid = jnp.sum(jnp.where(row < CHUNK // 2, lf, 0.0), axis=0, keepdims=True)
    last = jnp.sum(lf, axis=0, keepdims=True)
    e_q = jnp.exp(jnp.minimum(cum - mid, EXP_CLAMP))
    e_k = jnp.exp(jnp.minimum(mid - cum, EXP_CLAMP))
    e_a = jnp.exp(cum)
    e_l = jnp.exp(last - cum)
    return q, sg, f, k, last, e_q, e_k, e_a, e_l


def _hgrn_fwd(p_hg, tab, gain):
    t = p_hg.shape[0]
    bt = min(HG_BT, t)
    nb, nc = t // bt, bt // CHUNK

    hpb = HG_HPB
    wide = hpb * HK

    def body(q_ref, f_ref, i_ref, g_ref, tab_ref, gain_ref, o_ref, ho_ref, st_ref, state):
        @pl.when(pl.program_id(1) == 0)
        def _():
            state[...] = jnp.zeros_like(state)

        row = lax.broadcasted_iota(jnp.int32, (CHUNK, HK), 0)
        tril = lax.broadcasted_iota(jnp.int32, (CHUNK, CHUNK), 0) >= lax.broadcasted_iota(jnp.int32, (CHUNK, CHUNK), 1)
        gain_v = gain_ref[...]

        def chunk(c, carry):
            sl = pl.ds(pl.multiple_of(c * CHUNK, CHUNK), CHUNK)
            for hh in range(hpb):
                ln = slice(hh * HK, (hh + 1) * HK)
                lb = _sig(tab_ref[0:1, ln] - tab_ref[1:2, ln])
                v = i_ref[sl, ln].astype(BF16)
                q, _, _, k, last, e_q, e_k, e_a, e_l = _hg_gates(q_ref[sl, ln], f_ref[sl, ln], lb, row)
                st = state[hh]
                st_ref[hh, c] = st
                p = jnp.where(tril, _dot_nt((q * e_q).astype(BF16), (k * e_k).astype(BF16)), 0.0)
                o = _dot(p.astype(BF16), v) + _dot_nt((q * e_a).astype(BF16), st.astype(BF16))
                state[hh] = jnp.exp(last) * st + _dot_tn(v, (k * e_l).astype(BF16))
                o_ref[sl, ln] = o
                r = lax.rsqrt(jnp.mean(o * o, axis=-1, keepdims=True) + EPS)
                ho_ref[sl, ln] = (o * r * gain_v * _silu(g_ref[sl, ln])).astype(BF16)
            return carry

        lax.fori_loop(0, nc, chunk, 0)

    def col(k):
        return pl.BlockSpec((bt, wide), lambda h, j, k=k: (j, k * (HEADS // hpb) + h))

    return _call(
        body, name="hgrn_fwd", grid=(HEADS // hpb, nb),
        in_specs=[col(0), col(1), col(2), col(3),
                  pl.BlockSpec((2, wide), lambda h, j: (0, h)), pl.BlockSpec((1, HK), lambda h, j: (0, 0))],
        out_specs=[pl.BlockSpec((bt, wide), lambda h, j: (j, h)), pl.BlockSpec((bt, wide), lambda h, j: (j, h)),
                   pl.BlockSpec((hpb, nc, HK, HK), lambda h, j: (h, j, 0, 0))],
        out_shape=[jax.ShapeDtypeStruct((t, D), F32), jax.ShapeDtypeStruct((t, D), BF16),
                   jax.ShapeDtypeStruct((HEADS, t // CHUNK, HK, HK), F32)],
        scratch_shapes=[pltpu.VMEM((hpb, HK, HK), F32)],
        compiler_params=_cp(),
    )(p_hg, p_hg, p_hg, p_hg, tab, gain)


def _hgrn_bwd(p_hg, tab, gain, o_raw, states, dho):
    t = p_hg.shape[0]
    bt = min(HG_BT, t)
    nb, nc = t // bt, bt // CHUNK
    hpb = HG_HPB
    wide = hpb * HK

    def body(q_ref, f_ref, i_ref, g_ref, tab_ref, gain_ref, o_ref, st_ref, dho_ref,
             dq_ref, df_ref, di_ref, dg_ref, dtab_ref, dgain_ref, dstate, dlb):
        h, j = pl.program_id(0), pl.program_id(1)

        @pl.when(jnp.logical_and(h == 0, j == 0))
        def _():
            dgain_ref[...] = jnp.zeros_like(dgain_ref)

        @pl.when(j == 0)
        def _():
            dstate[...] = jnp.zeros_like(dstate)
            dlb[...] = jnp.zeros_like(dlb)

        row = lax.broadcasted_iota(jnp.int32, (CHUNK, HK), 0)
        tril = lax.broadcasted_iota(jnp.int32, (CHUNK, CHUNK), 0) >= lax.broadcasted_iota(jnp.int32, (CHUNK, CHUNK), 1)
        gain_v = gain_ref[...]

        def chunk(cc, carry):
            c = nc - 1 - cc
            sl = pl.ds(pl.multiple_of(c * CHUNK, CHUNK), CHUNK)
            dgain = jnp.zeros((1, HK), F32)
            for hh in range(hpb):
                ln = slice(hh * HK, (hh + 1) * HK)
                lb = _sig(tab_ref[0:1, ln] - tab_ref[1:2, ln])
                qr = q_ref[sl, ln]
                v = i_ref[sl, ln].astype(BF16)
                gr = g_ref[sl, ln]
                q, sg, f, k, last, e_q, e_k, e_a, e_l = _hg_gates(qr, f_ref[sl, ln], lb, row)
                o = o_ref[sl, ln]
                r = lax.rsqrt(jnp.mean(o * o, axis=-1, keepdims=True) + EPS)
                oh = o * r
                dh = dho_ref[sl, ln].astype(F32)
                dnorm = dh * _silu(gr)
                dg_ref[sl, ln] = (dh * oh * gain_v * _dsilu(gr)).astype(BF16)
                dgain = dgain + jnp.sum(dnorm * oh, axis=0, keepdims=True)
                dxh = dnorm * gain_v
                do = (r * (dxh - oh * jnp.mean(dxh * oh, axis=-1, keepdims=True))).astype(BF16)
                st0 = st_ref[hh, c]
                st0_b = st0.astype(BF16)
                ds1 = dstate[hh]
                ds1_b = ds1.astype(BF16)
                qt = (q * e_q).astype(BF16)
                kt = (k * e_k).astype(BF16)
                qd = (q * e_a).astype(BF16)
                kd = (k * e_l).astype(BF16)
                p = jnp.where(tril, _dot_nt(qt, kt), 0.0).astype(BF16)
                dp = jnp.where(tril, _dot_nt(do, v), 0.0).astype(BF16)
                dv = _dot_tn(p, do) + _dot_nt(kd, ds1_b)
                dqt = _dot(dp, kt)
                dkt = _dot_tn(dp, qt)
                dq_inter = _dot(do, st0_b) * e_a
                dk_inter = _dot(v, ds1_b) * e_l
                dq = dqt * e_q + dq_inter
                dk = dkt * e_k + dk_inter
                e_last = jnp.exp(last)
                dstate[hh] = _dot_tn(do, qd) + e_last * ds1
                dlast = (jnp.sum(k * dk_inter, axis=0, keepdims=True)
                         + e_last * jnp.sum(ds1 * st0, axis=0, keepdims=True))
                da = (qt.astype(F32) * dqt - kt.astype(F32) * dkt + q * dq_inter - k * dk_inter
                      + jnp.where(row == CHUNK - 1, dlast, 0.0))
                dlf = _rcumsum_rows(da, row)
                dfv = dlf / f - dk
                df_ref[sl, ln] = (dfv * (1.0 - lb) * sg * (1.0 - sg)).astype(BF16)
                dlb[:, ln] += jnp.sum(dfv * (1.0 - sg), axis=0, keepdims=True)
                dq_ref[sl, ln] = (dq * _dsilu(qr)).astype(BF16)
                di_ref[sl, ln] = dv.astype(BF16)
            dgain_ref[...] += dgain
            return carry

        lax.fori_loop(0, nc, chunk, 0)

        @pl.when(j == nb - 1)
        def _():
            lb = _sig(tab_ref[0:1, :] - tab_ref[1:2, :])
            d0 = dlb[...] * lb * (1.0 - lb)
            dtab_ref[0:1, :] = d0
            dtab_ref[1:2, :] = -d0

    def col(k):
        return pl.BlockSpec((bt, wide), lambda h, j, k=k: (nb - 1 - j, k * (HEADS // hpb) + h))

    tok = pl.BlockSpec((bt, wide), lambda h, j: (nb - 1 - j, h))
    return _call(
        body, name="hgrn_bwd", grid=(HEADS // hpb, nb),
        in_specs=[col(0), col(1), col(2), col(3),
                  pl.BlockSpec((2, wide), lambda h, j: (0, h)), pl.BlockSpec((1, HK), lambda h, j: (0, 0)),
                  tok, pl.BlockSpec((hpb, nc, HK, HK), lambda h, j: (h, nb - 1 - j, 0, 0)), tok],
        out_specs=[tok, tok, tok, tok,
                   pl.BlockSpec((2, wide), lambda h, j: (0, h)), pl.BlockSpec((1, HK), lambda h, j: (0, 0))],
        out_shape=[jax.ShapeDtypeStruct((t, D), BF16)] * 4
        + [jax.ShapeDtypeStruct((2, D), F32), jax.ShapeDtypeStruct((1, HK), F32)],
        scratch_shapes=[pltpu.VMEM((hpb, HK, HK), F32), pltpu.VMEM((1, wide), F32)],
        compiler_params=_cp(),
    )(p_hg, p_hg, p_hg, p_hg, tab, gain, o_raw, states, dho)


def _rope_tables(pos):
    t = pos.shape[0]
    tm = min(ROW_TM, t)
    inv = np.zeros((1, HK), np.float32)
    freq = (ROPE_THETA ** (-np.arange(0, ROPE, 2, dtype=np.float32) / ROPE)).astype(np.float32)
    inv[0, 0:ROPE // 2] = freq
    inv[0, ROPE // 2:ROPE] = freq
    sign = np.zeros((1, HK), np.float32)
    sign[0, 0:ROPE // 2] = -1.0
    sign[0, ROPE // 2:ROPE] = 1.0

    def body(pos_ref, inv_ref, sign_ref, cos_ref, sin_ref):
        ang = pos_ref[...].astype(F32) * inv_ref[...]
        cos_ref[...] = jnp.cos(ang)
        sin_ref[...] = jnp.sin(ang) * sign_ref[...]

    one = pl.BlockSpec((1, HK), lambda i: (0, 0))
    row = pl.BlockSpec((tm, HK), lambda i: (i, 0))
    return _call(
        body, name="rope_tables", grid=(t // tm,),
        in_specs=[pl.BlockSpec((tm, 1), lambda i: (i, 0)), one, one],
        out_specs=[row, row],
        out_shape=[jax.ShapeDtypeStruct((t, HK), F32)] * 2,
        compiler_params=_cp(),
    )(pos, jnp.asarray(inv), jnp.asarray(sign))


def _rope(x, cos, sin_signed):
    r = lax.broadcasted_iota(jnp.int32, (HK, HK), 0)
    c = lax.broadcasted_iota(jnp.int32, (HK, HK), 1)
    half = ROPE // 2
    swap = jnp.logical_or(jnp.logical_and(c < half, r == c + half),
                          jnp.logical_and(jnp.logical_and(c >= half, c < ROPE), r == c - half))
    return x * cos + _dot_split(x, swap.astype(BF16)) * sin_signed


def _dot_split(x, m):
    hi = x.astype(BF16)
    lo = (x - hi.astype(F32)).astype(BF16)
    return _dot(hi, m) + _dot(lo, m)


def _lane_sum(x):
    return _dot_split(x, jnp.ones((HK, HK), BF16))


def _head_norm(xn, xr):
    r = lax.rsqrt(_lane_sum(xn * xn + xr * xr) * (1.0 / QK) + EPS)
    return xn * r, xr * r, r


def _head_norm_bwd(xn, xr, g_n, g_r, dn, dr):
    hn, hr, r = _head_norm(xn, xr)
    dxn, dxr = dn * g_n, dr * g_r
    c = _lane_sum(dxn * hn + dxr * hr) * (1.0 / QK)
    return r * (dxn - hn * c), r * (dxr - hr * c), dn * hn, dr * hr


def _mla_prep_fwd(qf, kv, p_mla, cos, sin, gq, gk):
    t = qf.shape[0]
    tm = min(ROW_TM, t)

    def body(qf_ref, kv_ref, kpe_ref, cos_ref, sin_ref, gq_ref, gk_ref, q_ref, k_ref, v_ref):
        cos_v, sin_v = cos_ref[...], sin_ref[...]
        kpe = kpe_ref[...]
        for h in range(HEADS):
            lo, mid, hi = h * QKP, h * QKP + HK, (h + 1) * QKP
            qn, qr, _ = _head_norm(qf_ref[:, lo:mid], qf_ref[:, mid:hi])
            q_ref[h, :, 0:HK] = (qn * gq_ref[:, 0:HK] * (SCALE * LOG2E)).astype(BF16)
            q_ref[h, :, HK:QKP] = (_rope(qr * gq_ref[:, HK:QKP], cos_v, sin_v) * (SCALE * LOG2E)).astype(BF16)
            kn, kr, _ = _head_norm(kv_ref[:, lo:mid], kpe)
            k_ref[h, :, 0:HK] = (kn * gk_ref[:, 0:HK]).astype(BF16)
            k_ref[h, :, HK:QKP] = _rope(kr * gk_ref[:, HK:QKP], cos_v, sin_v).astype(BF16)
            v_ref[h] = kv_ref[:, mid:hi].astype(BF16)

    head = pl.BlockSpec((tm, HEADS * QKP), lambda i: (i, 0))
    tok = pl.BlockSpec((tm, HK), lambda i: (i, 0))
    gain = pl.BlockSpec((1, QKP), lambda i: (0, 0))
    return _call(
        body, name="mla_prep_fwd", grid=(t // tm,),
        in_specs=[head, head, pl.BlockSpec((tm, HK), lambda i: (i, MLA_COLS // HK - 1)), tok, tok, gain, gain],
        out_specs=[pl.BlockSpec((HEADS, tm, QKP), lambda i: (0, i, 0)),
                   pl.BlockSpec((HEADS, tm, QKP), lambda i: (0, i, 0)),
                   pl.BlockSpec((HEADS, tm, HK), lambda i: (0, i, 0))],
        out_shape=[jax.ShapeDtypeStruct((HEADS, t, QKP), BF16), jax.ShapeDtypeStruct((HEADS, t, QKP), BF16),
                   jax.ShapeDtypeStruct((HEADS, t, HK), BF16)],
        compiler_params=_cp(),
    )(qf, kv, p_mla, cos, sin, gq, gk)


def _mla_prep_bwd(qf, kv, p_mla, cos, sin, gq, gk, dq, dk, dv):
    t = qf.shape[0]
    tm = min(ROW_TM, t)

    def body(qf_ref, kv_ref, kpe_ref, cos_ref, sin_ref, gq_ref, gk_ref, dq_ref, dk_ref, dv_ref,
             dqf_ref, dkv_ref, dkpe_ref, dgq_ref, dgk_ref):
        @pl.when(pl.program_id(0) == 0)
        def _():
            dgq_ref[...] = jnp.zeros_like(dgq_ref)
            dgk_ref[...] = jnp.zeros_like(dgk_ref)

        cos_v, sin_v = cos_ref[...], -sin_ref[...]
        kpe = kpe_ref[...]
        gqn, gqr, gkn, gkr = gq_ref[:, 0:HK], gq_ref[:, HK:QKP], gk_ref[:, 0:HK], gk_ref[:, HK:QKP]
        dkpe = jnp.zeros((tm, HK), F32)
        dgq_n, dgq_r, dgk_n, dgk_r = [jnp.zeros((1, HK), F32) for _ in range(4)]
        for h in range(HEADS):
            lo, mid, hi = h * QKP, h * QKP + HK, (h + 1) * QKP
            dqn = dq_ref[h, :, 0:HK].astype(F32) * SCALE
            dqr = _rope(dq_ref[h, :, HK:QKP].astype(F32), cos_v, sin_v) * SCALE
            a, b, ga, gb = _head_norm_bwd(qf_ref[:, lo:mid], qf_ref[:, mid:hi], gqn, gqr, dqn, dqr)
            dqf_ref[:, lo:mid] = a.astype(BF16)
            dqf_ref[:, mid:hi] = b.astype(BF16)
            dgq_n = dgq_n + jnp.sum(ga, axis=0, keepdims=True)
            dgq_r = dgq_r + jnp.sum(gb, axis=0, keepdims=True)
            dkn = dk_ref[h, :, 0:HK].astype(F32) * LN2
            dkr = _rope(dk_ref[h, :, HK:QKP].astype(F32), cos_v, sin_v) * LN2
            a, b, ga, gb = _head_norm_bwd(kv_ref[:, lo:mid], kpe, gkn, gkr, dkn, dkr)
            dkv_ref[:, lo:mid] = a.astype(BF16)
            dkv_ref[:, mid:hi] = dv_ref[h].astype(BF16)
            dkpe = dkpe + b
            dgk_n = dgk_n + jnp.sum(ga, axis=0, keepdims=True)
            dgk_r = dgk_r + jnp.sum(gb, axis=0, keepdims=True)
        dkpe_ref[...] = dkpe
        dgq_ref[:, 0:HK] += dgq_n
        dgq_ref[:, HK:QKP] += dgq_r
        dgk_ref[:, 0:HK] += dgk_n
        dgk_ref[:, HK:QKP] += dgk_r

    head = pl.BlockSpec((tm, HEADS * QKP), lambda i: (i, 0))
    tok = pl.BlockSpec((tm, HK), lambda i: (i, 0))
    gain = pl.BlockSpec((1, QKP), lambda i: (0, 0))
    hq = pl.BlockSpec((HEADS, tm, QKP), lambda i: (0, i, 0))
    return _call(
        body, name="mla_prep_bwd", grid=(t // tm,),
        in_specs=[head, head, pl.BlockSpec((tm, HK), lambda i: (i, MLA_COLS // HK - 1)), tok, tok, gain, gain,
                  hq, hq, pl.BlockSpec((HEADS, tm, HK), lambda i: (0, i, 0))],
        out_specs=[head, head, tok, gain, gain],
        out_shape=[jax.ShapeDtypeStruct((t, HEADS * QKP), BF16), jax.ShapeDtypeStruct((t, HEADS * QKP), BF16),
                   jax.ShapeDtypeStruct((t, HK), F32), jax.ShapeDtypeStruct((1, QKP), F32),
                   jax.ShapeDtypeStruct((1, QKP), F32)],
        compiler_params=_cp(),
    )(qf, kv, p_mla, cos, sin, gq, gk, dq, dk, dv)


def _chunk_mask(row0, rows, cols):
    r = lax.broadcasted_iota(jnp.int32, (rows, cols), 0) + row0
    c = lax.broadcasted_iota(jnp.int32, (rows, cols), 1)
    return jnp.right_shift(r, 6) >= jnp.right_shift(c, 6)


def _flash_fwd(q, k, v):
    t = q.shape[1]
    tq = min(TQ, t)
    nq = t // tq
    sub = min(SUBQ, tq)
    pairs = [(i, j) for i in range(nq) for j in range(i + 1)]
    qi = jnp.asarray([p[0] for p in pairs], jnp.int32)
    kj = jnp.asarray([p[1] for p in pairs], jnp.int32)

    def body(qi_ref, kj_ref, q_ref, k_ref, v_ref, o_ref, lse_ref, m_s, l_s, acc_s):
        n = pl.program_id(1)
        i, j = qi_ref[n], kj_ref[n]

        @pl.when(j == 0)
        def _():
            m_s[...] = jnp.full_like(m_s, NEG)
            l_s[...] = jnp.zeros_like(l_s)
            acc_s[...] = jnp.zeros_like(acc_s)

        def step(diag):
            for r in range(tq // sub):
                rows = slice(r * sub, (r + 1) * sub)
                cols = (r + 1) * sub if diag else tq
                s = _dot_nt(q_ref[rows, :], k_ref[0:cols, :])
                if diag:
                    s = jnp.where(_chunk_mask(r * sub, sub, cols), s, NEG)
                m_old = m_s[rows, :]
                m_new = jnp.maximum(m_old, jnp.max(s, axis=-1, keepdims=True))
                alpha = jnp.exp2(m_old - m_new)
                p = jnp.exp2(s - jnp.tile(m_new, (1, cols // HK)))
                l_s[rows, :] = alpha * l_s[rows, :] + jnp.sum(p, axis=-1, keepdims=True)
                acc_s[rows, :] = alpha * acc_s[rows, :] + _dot(p.astype(BF16), v_ref[0:cols, :])
                m_s[rows, :] = m_new

        @pl.when(j < i)
        def _():
            step(False)

        @pl.when(j == i)
        def _():
            step(True)
            l = l_s[...]
            o_ref[...] = (acc_s[...] / l).astype(BF16)
            lse_ref[...] = m_s[...] + jnp.log(l) * LOG2E

    grid_spec = pltpu.PrefetchScalarGridSpec(
        num_scalar_prefetch=2, grid=(HEADS, len(pairs)),
        in_specs=[pl.BlockSpec((None, tq, QKP), lambda h, n, qi, kj: (h, qi[n], 0)),
                  pl.BlockSpec((None, tq, QKP), lambda h, n, qi, kj: (h, kj[n], 0)),
                  pl.BlockSpec((None, tq, HK), lambda h, n, qi, kj: (h, kj[n], 0))],
        out_specs=[pl.BlockSpec((tq, HK), lambda h, n, qi, kj: (qi[n], h)),
                   pl.BlockSpec((None, tq, HK), lambda h, n, qi, kj: (h, qi[n], 0))],
        scratch_shapes=[pltpu.VMEM((tq, HK), F32), pltpu.VMEM((tq, HK), F32), pltpu.VMEM((tq, HK), F32)],
    )
    return _call(
        body, name="flash_fwd", grid_spec=grid_spec,
        out_shape=[jax.ShapeDtypeStruct((t, D), BF16), jax.ShapeDtypeStruct((HEADS, t, HK), F32)],
        compiler_params=_cp(),
    )(qi, kj, q, k, v)


def _attn_delta(do, o):
    t = do.shape[0]
    tm = min(TM, t)

    def body(do_ref, o_ref, d_ref):
        for h in range(HEADS):
            ln = slice(h * HK, (h + 1) * HK)
            d = jnp.sum(do_ref[:, ln].astype(F32) * o_ref[:, ln].astype(F32), axis=-1, keepdims=True)
            d_ref[h] = jnp.broadcast_to(d, (tm, HK))

    blk = pl.BlockSpec((tm, D), lambda i: (i, 0))
    return _call(
        body, name="attn_delta", grid=(t // tm,),
        in_specs=[blk, blk],
        out_specs=pl.BlockSpec((HEADS, tm, HK), lambda i: (0, i, 0)),
        out_shape=jax.ShapeDtypeStruct((HEADS, t, HK), F32),
        compiler_params=_cp(),
    )(do, o)


def _flash_bwd(q, k, v, lse, delta, do):
    t = q.shape[1]
    tq = min(TQ, t)
    nq = t // tq
    sub = min(SUBQ, tq)
    pairs = [(i, j) for j in range(nq) for i in range(j, nq)]
    qi = jnp.asarray([p[0] for p in pairs], jnp.int32)
    kj = jnp.asarray([p[1] for p in pairs], jnp.int32)
    npairs = len(pairs)

    def body(qi_ref, kj_ref, q_ref, k_ref, v_ref, lse_ref, dl_ref, do_ref, dq_ref, dk_ref, dv_ref):
        n = pl.program_id(1)
        i, j = qi_ref[n], kj_ref[n]

        @pl.when(n == 0)
        def _():
            dq_ref[...] = jnp.zeros_like(dq_ref)

        @pl.when(i == j)
        def _():
            dk_ref[...] = jnp.zeros_like(dk_ref)
            dv_ref[...] = jnp.zeros_like(dv_ref)

        def step(diag):
            for r in range(tq // sub):
                rows = slice(r * sub, (r + 1) * sub)
                cols = (r + 1) * sub if diag else tq
                qv, dov, kv_ = q_ref[rows, :], do_ref[rows, :], k_ref[0:cols, :]
                p = jnp.exp2(_dot_nt(qv, kv_) - jnp.tile(lse_ref[rows, :], (1, cols // HK)))
                if diag:
                    p = jnp.where(_chunk_mask(r * sub, sub, cols), p, 0.0)
                dp = _dot_nt(dov, v_ref[0:cols, :])
                ds = (p * (dp - jnp.tile(dl_ref[rows, :], (1, cols // HK)))).astype(BF16)
                dv_ref[0:cols, :] += _dot_tn(p.astype(BF16), dov)
                dk_ref[0:cols, :] += _dot_tn(ds, qv)
                dq_rows = pl.ds(pl.multiple_of(i * tq + r * sub, sub), sub)
                dq_ref[dq_rows, :] += _dot(ds, kv_)

        @pl.when(j < i)
        def _():
            step(False)

        @pl.when(j == i)
        def _():
            step(True)

    grid_spec = pltpu.PrefetchScalarGridSpec(
        num_scalar_prefetch=2, grid=(HEADS, npairs),
        in_specs=[pl.BlockSpec((None, tq, QKP), lambda h, n, qi, kj: (h, qi[n], 0)),
                  pl.BlockSpec((None, tq, QKP), lambda h, n, qi, kj: (h, kj[n], 0)),
                  pl.BlockSpec((None, tq, HK), lambda h, n, qi, kj: (h, kj[n], 0)),
                  pl.BlockSpec((None, tq, HK), lambda h, n, qi, kj: (h, qi[n], 0)),
                  pl.BlockSpec((None, tq, HK), lambda h, n, qi, kj: (h, qi[n], 0)),
                  pl.BlockSpec((tq, HK), lambda h, n, qi, kj: (qi[n], h))],
        out_specs=[pl.BlockSpec((None, t, QKP), lambda h, n, qi, kj: (h, 0, 0)),
                   pl.BlockSpec((None, tq, QKP), lambda h, n, qi, kj: (h, kj[n], 0)),
                   pl.BlockSpec((None, tq, HK), lambda h, n, qi, kj: (h, kj[n], 0))],
    )
    return _call(
        body, name="flash_bwd", grid_spec=grid_spec,
        out_shape=[jax.ShapeDtypeStruct((HEADS, t, QKP), F32), jax.ShapeDtypeStruct((HEADS, t, QKP), F32),
                   jax.ShapeDtypeStruct((HEADS, t, HK), F32)],
        compiler_params=_cp(56),
    )(qi, kj, q, k, v, lse, delta, do)


def _adamw(name, w, g, m, v):
    r, c = w.shape
    tr = r if r <= 256 else next(k for k in (256, 352, 384) if r % k == 0)

    def body(w_ref, g_ref, m_ref, v_ref, d_ref, nm_ref, nv_ref):
        gv = g_ref[...]
        nm = ADAM_B1 * m_ref[...] + (1.0 - ADAM_B1) * gv
        nv = ADAM_B2 * v_ref[...] + (1.0 - ADAM_B2) * (gv * gv)
        m_hat = nm / (1.0 - ADAM_B1 ** ADAM_STEP)
        v_hat = nv / (1.0 - ADAM_B2 ** ADAM_STEP)
        d_ref[...] = -ADAM_LR * (m_hat / (jnp.sqrt(v_hat) + ADAM_EPS) + ADAM_WD * w_ref[...])
        nm_ref[...] = nm
        nv_ref[...] = nv

    blk = pl.BlockSpec((tr, c), lambda i: (i, 0))
    return _call(
        body, name=name, grid=(r // tr,),
        in_specs=[blk] * 4, out_specs=[blk] * 3,
        out_shape=[jax.ShapeDtypeStruct((r, c), F32)] * 3,
        compiler_params=_cp(),
    )(w, g, m, v)


def _place():
    return lax.axis_index("x"), lax.axis_index("y"), lax.axis_index("c")


def _other_chips(x, y):
    return [(1 - x, y), (x, 1 - y), (1 - x, 1 - y)]


class _Exchange:
    inputs = ()
    out_shapes = ()
    scratch = ()

    def start(self, *refs):
        raise NotImplementedError

    def finish(self, *refs):
        raise NotImplementedError

    def alone(self, name):
        def body(*refs):
            self.start(*refs)
            self.finish(*refs)

        anywhere = pl.BlockSpec(memory_space=pl.ANY)
        return _call(
            body, name=name,
            in_specs=[anywhere] * len(self.inputs), out_specs=[anywhere] * len(self.out_shapes),
            out_shape=list(self.out_shapes), scratch_shapes=list(self.scratch),
        )(*self.inputs)


class _GatherWeights(_Exchange):
    def __init__(self, shard):
        self.r = shard.shape[0]
        self.inputs = (shard,)
        self.out_shapes = (jax.ShapeDtypeStruct((4, self.r, PACK_W), shard.dtype),)
        self.scratch = (pltpu.SemaphoreType.DMA((6,)), pltpu.SemaphoreType.DMA((6,)))

    def gathered(self, got, k):
        return lax.dynamic_update_slice(got, self.inputs[0][None], (k, 0, 0))

    def _copies(self, s_ref, g_ref, send_sems, recv_sems):
        half = self.r // 2
        x, y, c = _place()
        chips = _other_chips(x, y)

        def rows(px, py, pc):
            return g_ref.at[2 * px + py, pl.ds(pc * half, half), :]

        def copy(k, block, to, src=None):
            return pltpu.make_async_remote_copy(
                src_ref=rows(*block) if src is None else src, dst_ref=rows(*block),
                send_sem=send_sems.at[k], recv_sem=recv_sems.at[k], device_id=to, device_id_type=MESH)

        first = [copy(j, (x, y, c), (*chip, c), src=s_ref.at[pl.ds(c * half, half), :]) for j, chip in enumerate(chips)]
        passed = [copy(3 + j, (*chip, c), (x, y, 1 - c)) for j, chip in enumerate(chips)]
        landed = [copy(j, (*chip, c), (x, y, c)) for j, chip in enumerate(chips)]
        landed += [copy(3 + j, (*chip, 1 - c), (x, y, c)) for j, chip in enumerate(chips)]
        return first, passed, landed

    def start(self, *refs):
        first, _, _ = self._copies(*refs)
        for cp in first:
            cp.start()

    def finish(self, *refs):
        first, passed, landed = self._copies(*refs)
        for j in range(3):
            landed[j].wait_recv()
            passed[j].start()
        for j in range(3):
            landed[3 + j].wait_recv()
        for cp in first + passed:
            cp.wait_send()


def _swap_halves(name, gp):
    r = gp.shape[1]
    half = r // 2

    def body(g_ref, o_ref, send_sem, recv_sem):
        x, y, c = _place()
        cp = pltpu.make_async_remote_copy(
            src_ref=g_ref.at[:, pl.ds((1 - c) * half, half), :], dst_ref=o_ref,
            send_sem=send_sem, recv_sem=recv_sem, device_id=(x, y, 1 - c), device_id_type=MESH)
        cp.start()
        cp.wait()

    return _call(
        body, name=name,
        in_specs=[pl.BlockSpec(memory_space=pl.ANY)],
        out_specs=pl.BlockSpec(memory_space=pl.ANY),
        out_shape=jax.ShapeDtypeStruct((4, half, PACK_W), gp.dtype),
        scratch_shapes=[pltpu.SemaphoreType.DMA, pltpu.SemaphoreType.DMA],
    )(gp)


def _chip_sum(name, gp, got, c_arr):
    half = got.shape[1]
    tr = ADD_ROWS
    nb = half // tr

    def body(c_ref, a_ref, b_ref, o_ref, ob_ref):
        s = a_ref[...] + b_ref[...]
        o_ref[...] = s
        ob_ref[...] = s.astype(BF16)

    grid_spec = pltpu.PrefetchScalarGridSpec(
        num_scalar_prefetch=1, grid=(4, nb),
        in_specs=[pl.BlockSpec((None, tr, PACK_W), lambda s, i, c: (s, c[0] * nb + i, 0)),
                  pl.BlockSpec((None, tr, PACK_W), lambda s, i, c: (s, i, 0))],
        out_specs=[pl.BlockSpec((None, tr, PACK_W), lambda s, i, c: (s, i, 0)),
                   pl.BlockSpec((None, tr, PACK_W), lambda s, i, c: (s, i, 0))],
    )
    return _call(
        body, name=name, grid_spec=grid_spec,
        out_shape=[jax.ShapeDtypeStruct(got.shape, F32), jax.ShapeDtypeStruct(got.shape, BF16)],
        compiler_params=_cp(),
    )(c_arr, gp, got)


class _ScatterChipSums(_Exchange):
    def __init__(self, cs):
        self.inputs = (cs,)
        self.out_shapes = (jax.ShapeDtypeStruct((3,) + cs.shape[1:], cs.dtype),)
        self.scratch = (pltpu.SemaphoreType.DMA((3,)), pltpu.SemaphoreType.DMA((3,)))

    def _copies(self, s_ref, o_ref, send_sems, recv_sems):
        x, y, c = _place()
        return [pltpu.make_async_remote_copy(
            src_ref=s_ref.at[2 * px + py], dst_ref=o_ref.at[j],
            send_sem=send_sems.at[j], recv_sem=recv_sems.at[j], device_id=(px, py, c), device_id_type=MESH)
            for j, (px, py) in enumerate(_other_chips(x, y))]

    def start(self, *refs):
        for cp in self._copies(*refs):
            cp.start()

    def finish(self, *refs):
        for cp in self._copies(*refs):
            cp.wait()


def _shard_sum(name, cs, got, kc_arr):
    h = cs.shape[1]
    tr = ADD_ROWS
    nb = h // tr

    def body(k_ref, a_ref, b_ref, o_ref):
        o_ref[...] = ((a_ref[...] + b_ref[0].astype(F32)) + b_ref[1].astype(F32)) + b_ref[2].astype(F32)

    grid_spec = pltpu.PrefetchScalarGridSpec(
        num_scalar_prefetch=1, grid=(nb,),
        in_specs=[pl.BlockSpec((None, tr, PACK_W), lambda i, k: (k[0], i, 0)),
                  pl.BlockSpec((3, tr, PACK_W), lambda i, k: (0, i, 0))],
        out_specs=pl.BlockSpec((tr, PACK_W), lambda i, k: (k[1] * nb + i, 0)),
    )
    return _call(
        body, name=name, grid_spec=grid_spec,
        out_shape=jax.ShapeDtypeStruct((2 * h, PACK_W), F32),
        compiler_params=_cp(),
    )(kc_arr, cs, got)


def _join_halves(name, both):
    h = both.shape[0] // 2

    def body(m_ref, o_ref, send_sem, recv_sem):
        x, y, c = _place()
        cp = pltpu.make_async_remote_copy(
            src_ref=m_ref.at[pl.ds(c * h, h), :], dst_ref=o_ref.at[pl.ds(c * h, h), :],
            send_sem=send_sem, recv_sem=recv_sem, device_id=(x, y, 1 - c), device_id_type=MESH)
        cp.start()
        cp.wait_send()
        pltpu.make_async_remote_copy(
            src_ref=m_ref.at[pl.ds(c * h, h), :], dst_ref=o_ref.at[pl.ds((1 - c) * h, h), :],
            send_sem=send_sem, recv_sem=recv_sem, device_id=(x, y, 1 - c), device_id_type=MESH).wait_recv()

    return _call(
        body, name=name,
        in_specs=[pl.BlockSpec(memory_space=pl.ANY)],
        out_specs=pl.BlockSpec(memory_space=pl.ANY),
        out_shape=jax.ShapeDtypeStruct(both.shape, both.dtype),
        input_output_aliases={0: 0},
        scratch_shapes=[pltpu.SemaphoreType.DMA, pltpu.SemaphoreType.DMA],
    )(both)


def _all_reduce_small(v):
    r = v.shape[0]

    def body(v_ref, o_ref, buf, send_sems, recv_sems):
        x, y, c = _place()
        me = 4 * x + 2 * y + c
        buf[me] = v_ref[...]
        cps = []
        for k in range(1, 8):
            peer = (x ^ (k >> 2), y ^ ((k >> 1) & 1), c ^ (k & 1))
            cps.append(pltpu.make_async_remote_copy(
                src_ref=v_ref, dst_ref=buf.at[me],
                send_sem=send_sems.at[k - 1], recv_sem=recv_sems.at[k - 1], device_id=peer, device_id_type=MESH))
        for cp in cps:
            cp.start()
        for k in range(1, 8):
            pltpu.make_async_remote_copy(
                src_ref=v_ref, dst_ref=buf.at[me ^ k],
                send_sem=send_sems.at[k - 1], recv_sem=recv_sems.at[k - 1],
                device_id=(x, y, c), device_id_type=MESH).wait_recv()
        for cp in cps:
            cp.wait_send()
        acc = buf[0]
        for k in range(1, 8):
            acc = acc + buf[k]
        o_ref[...] = acc

    return _call(
        body, name="all_reduce_small",
        in_specs=[pl.BlockSpec(memory_space=pltpu.VMEM)],
        out_specs=pl.BlockSpec(memory_space=pltpu.VMEM),
        out_shape=jax.ShapeDtypeStruct((r, 128), F32),
        scratch_shapes=[pltpu.VMEM((8, r, 128), F32), pltpu.SemaphoreType.DMA((7,)), pltpu.SemaphoreType.DMA((7,))],
    )(v)


def _group(names):
    return tuple(e for e in BIG if e[0] in names)


def _pack(shards, dtype):
    return jnp.concatenate([s.astype(dtype).reshape(-1, PACK_W) for s in shards], axis=0)


def _unpack_full(g, group):
    out, at = {}, 0
    for name, rows, cols, axis in group:
        n = rows * cols // 4 // PACK_W
        blk = g[:, at:at + n, :]
        at += n
        if axis == 1:
            out[name] = blk.reshape(4, rows, cols // 4).transpose(1, 0, 2).reshape(rows, cols)
        else:
            out[name] = blk.reshape(rows, cols)
    return out


def _pack_grads(grads, group):
    parts = []
    for name, rows, cols, axis in group:
        g = grads[name]
        if axis == 1:
            g = g.reshape(rows, 4, cols // 4).transpose(1, 0, 2)
        parts.append(g.reshape(4, -1, PACK_W))
    rows_total = sum(p.shape[1] for p in parts)
    pad = -rows_total % PACK_ALIGN
    if pad:
        parts.append(jnp.zeros((4, pad, PACK_W), F32))
    return jnp.concatenate(parts, axis=1)


def _unpack_shard(s, group):
    out, at = {}, 0
    for name, rows, cols, axis in group:
        n = rows * cols // 4 // PACK_W
        shape = (rows, cols // 4) if axis == 1 else (rows // 4, cols)
        out[name] = s[at:at + n, :].reshape(shape)
        at += n
    return out


def _pack_small(parts):
    flat = jnp.concatenate([p.reshape(-1) for p in parts])
    pad = -flat.shape[0] % 1024
    return jnp.concatenate([flat, jnp.zeros((pad,), F32)]).reshape(-1, 128)


def _ffn_fwd(tag, h, gain, w_in, w_out, side=None):
    t = h.shape[0]
    tm = min(TM, t)
    n = _rms_fwd(tag + "_norm", h, gain)

    def compute(rows, weights, outs):
        a, w_ref = rows[0][...], weights[0]
        for j in range(DFF // FFN_CHUNK):
            cols = slice(j * FFN_CHUNK, (j + 1) * FFN_CHUNK)
            gate = _dot(a, w_ref[:, cols])
            up = _dot(a, w_ref[:, DFF + j * FFN_CHUNK:DFF + (j + 1) * FFN_CHUNK])
            outs[0][:, cols] = gate.astype(BF16)
            outs[1][:, cols] = up.astype(BF16)
            outs[2][:, cols] = (_silu(gate) * up).astype(BF16)

    gate, up, act, *side_out = _rows_call(tag + "_in", [n], [w_in], [(DFF, BF16)] * 3, compute, min(FFN_TM, t), side=side)
    (out,) = _mm(tag + "_out", [_a_spec(act, tm)], [_b_nn(w_out, 512)], [(0, 0)],
                 lambda accs, ex: (ex[0] + 0.5 * accs[0],), [_e_tile(h, tm, 512)], [F32], t, D, tm, 512)
    return out, (n, gate, up, act), side_out


class _Reduction:
    def __init__(self, tag, group, c_arr, k_arr):
        self.tag, self.group, self.c_arr, self.k_arr = tag, group, c_arr, k_arr

    def begin(self, grads):
        gp = _pack_grads(grads, self.group)
        self.sums, sums_bf16 = _chip_sum("grad_chip_sum_" + self.tag, gp, _swap_halves("grad_swap_" + self.tag, gp), self.c_arr)
        return _ScatterChipSums(sums_bf16)

    def end(self, got):
        mine = _shard_sum("grad_shard_sum_" + self.tag, self.sums, got, self.k_arr)
        return _unpack_shard(_join_halves("grad_join_" + self.tag, mine), self.group)


def _ffn_bwd(tag, h, gain, w_in, w_out, saved, dout, side, reduction):
    t = h.shape[0]
    tm = min(TM, t)
    n, gate, up, act = saved

    def compute(rows, weights, outs):
        d = rows[0][...].astype(BF16)
        for j in range(DFF // FFN_CHUNK):
            cols = slice(j * FFN_CHUNK, (j + 1) * FFN_CHUNK)
            da = 0.5 * _dot_nt(d, weights[0][cols, :])
            g, u = rows[1][:, cols].astype(F32), rows[2][:, cols].astype(F32)
            s = _sig(g)
            silu = g * s
            outs[0][:, cols] = (da * u * (s + silu * (1.0 - s))).astype(BF16)
            outs[1][:, cols] = (da * silu).astype(BF16)

    dgate, dup, *side_out = _rows_call(tag + "_dact", [dout, gate, up], [w_out], [(DFF, BF16)] * 2, compute,
                                       min(FFN_TM, t), side=side)
    dw_out = _mm_tn(tag + "_dw_out", act, dout, scale=0.5, tm=DFF // 2, tn=D)
    dw_g = _mm_tn(tag + "_dw_gate", n, dgate, tm=D, tn=DFF // 2)
    dw_u = _mm_tn(tag + "_dw_up", n, dup, tm=D, tn=DFF // 2)
    sending = reduction.begin({tag + "_w_in": jnp.concatenate([dw_g, dw_u], axis=1), tag + "_w_out": dw_out})
    dn, got = _mm(tag + "_dn", [_a_spec(dgate, tm), _a_spec(dup, tm)],
                  [_b_nt(w_in, 512, DFF, 0), _b_nt(w_in, 512, DFF, 1)], [(0, 0), (1, 1)],
                  lambda accs, ex: (accs[0] + accs[1],), [], [F32], t, D, tm, 512, trans_b=True, side=sending)
    dh, dgain = _rms_bwd(tag + "_dnorm", h, gain, dn, dout)
    return dh, dgain, side_out, got


def kernel(x, positions, ffn1_norm, ffn1_w_in, ffn1_w_out, mix_norm, w_in, hg_lb_table, hg_out_norm, w_hg_branch, mla_q_lora_norm, w_q_up, mla_kv_lora_norm, w_kv_up, q_head_norm, k_head_norm, w_mla_branch, w_merge, b_merge, w_out, ffn2_norm, ffn2_w_in, ffn2_w_out, final_norm, loss_target, m_ffn1_norm, m_ffn1_w_in, m_ffn1_w_out, m_mix_norm, m_w_in, m_hg_lb_table, m_hg_out_norm, m_w_hg_branch, m_mla_q_lora_norm, m_w_q_up, m_mla_kv_lora_norm, m_w_kv_up, m_q_head_norm, m_k_head_norm, m_w_mla_branch, m_w_merge, m_b_merge, m_w_out, m_ffn2_norm, m_ffn2_w_in, m_ffn2_w_out, m_final_norm, v_ffn1_norm, v_ffn1_w_in, v_ffn1_w_out, v_mix_norm, v_w_in, v_hg_lb_table, v_hg_out_norm, v_w_hg_branch, v_mla_q_lora_norm, v_w_q_up, v_mla_kv_lora_norm, v_w_kv_up, v_q_head_norm, v_k_head_norm, v_w_mla_branch, v_w_merge, v_b_merge, v_w_out, v_ffn2_norm, v_ffn2_w_in, v_ffn2_w_out, v_final_norm):
    a = dict(locals())
    w = {n: a[n] for n in WEIGHT_ORDER}
    mom = {n: a["m_" + n] for n in WEIGHT_ORDER}
    var = {n: a["v_" + n] for n in WEIGHT_ORDER}
    t = x.shape[1]
    tm = min(TM, t)
    xt = x.reshape(t, D)
    target = loss_target.reshape(t, D)
    pos = positions.reshape(t, 1)
    x_i, y_i, c_i = _place()
    k_idx = (2 * x_i + y_i).astype(jnp.int32)
    c_arr = c_i.astype(jnp.int32).reshape(1)
    k_arr = jnp.stack([k_idx, c_i.astype(jnp.int32)])

    group_first = _group(("ffn1_w_in", "ffn1_w_out"))
    group_mid = _group(("w_in", "w_hg_branch", "w_q_up", "w_kv_up", "w_mla_branch", "w_merge", "w_out"))
    group_last = _group(("ffn2_w_in", "ffn2_w_out"))
    gather_first = _GatherWeights(_pack([w[e[0]][0] for e in group_first], BF16))
    gather_mid = _GatherWeights(_pack([w[e[0]][0] for e in group_mid], BF16))
    gather_last = _GatherWeights(_pack([w[e[0]][0] for e in group_last], BF16))
    (got,) = gather_first.alone("gather_first")
    full = _unpack_full(gather_first.gathered(got, k_idx), group_first)
    h1, ffn1_saved, (got,) = _ffn_fwd(
        "ffn1", xt, w["ffn1_norm"], full["ffn1_w_in"], full["ffn1_w_out"], side=gather_mid)
    full.update(_unpack_full(gather_mid.gathered(got, k_idx), group_mid))
    w_in_full = full["w_in"]
    w_in_hg = w_in_full[:, :4 * D]
    w_in_mla = jnp.pad(w_in_full[:, 4 * D:], ((0, 0), (0, MLA_COLS - (4800 - 4 * D))))
    w_q_pad = jnp.pad(full["w_q_up"].reshape(Q_LORA, HEADS, QK), ((0, 0), (0, 0), (0, QKP - QK))).reshape(Q_LORA, HEADS * QKP)
    w_kv = full["w_kv_up"]
    gq = jnp.pad(w["q_head_norm"], ((0, 0), (0, QKP - QK)))
    gk = jnp.pad(w["k_head_norm"], ((0, 0), (0, QKP - QK)))

    u = _rms_fwd("mix_norm", h1, w["mix_norm"])
    ident = lambda accs, ex: (accs[0],)
    p_hg, got = _mm("in_hg", [_a_spec(u, tm)], [_b_nn(w_in_hg, 512)], [(0, 0)], ident, [], [F32], t, 4 * D, tm, 512,
                    side=gather_last)
    full.update(_unpack_full(gather_last.gathered(got, k_idx), group_last))
    (p_mla,) = _mm("in_mla", [_a_spec(u, tm)], [_b_nn(w_in_mla, MLA_COLS)], [(0, 0)], ident, [], [F32], t, MLA_COLS, tm, MLA_COLS)
    o_raw, hg_o, states = _hgrn_fwd(p_hg, w["hg_lb_table"], w["hg_out_norm"])
    (y_hg,) = _mm("hg_branch", [_a_spec(hg_o, tm)], [_b_nn(full["w_hg_branch"], 512)], [(0, 0)], ident, [], [BF16], t, D, tm, 512)
    cqn, ckvn = _lora_norm_fwd(p_mla, w["mla_q_lora_norm"], w["mla_kv_lora_norm"])
    (qf,) = _mm("q_up", [_a_spec(cqn, tm)], [_b_nn(w_q_pad, 512)], [(0, 0)], ident, [], [F32], t, HEADS * QKP, tm, 512)
    (kvf,) = _mm("kv_up", [_a_spec(ckvn, tm)], [_b_nn(w_kv, 512)], [(0, 0)], ident, [], [F32], t, HEADS * QKP, tm, 512)
    cos, sin = _rope_tables(pos)
    qh, kh, vh = _mla_prep_fwd(qf, kvf, p_mla, cos, sin, gq, gk)
    o_mla, lse = _flash_fwd(qh, kh, vh)
    (y_mla,) = _mm("mla_branch", [_a_spec(o_mla, tm)], [_b_nn(full["w_mla_branch"], 512)], [(0, 0)], ident, [], [BF16], t, D, tm, 512)

    def merge_epi(accs, ex):
        g_hg = _sig(accs[0] + ex[2])
        g_mla = _sig(accs[1] + ex[3])
        return g_hg * ex[0].astype(F32) + g_mla * ex[1].astype(F32), g_hg, g_mla

    w_merge_f = full["w_merge"]
    mix, g_hg, g_mla = _mm(
        "merge", [_a_spec(u, tm)], [_b_nn(w_merge_f, 512), _b_nn(w_merge_f, 512, D // 512)], [(0, 0), (0, 1)], merge_epi,
        [_e_tile(y_hg, tm, 512), _e_tile(y_mla, tm, 512), _e_row(w["b_merge"], 512), _e_row(w["b_merge"], 512, D // 512)],
        [BF16, BF16, BF16], t, D, tm, 512)
    (h2,) = _mm("out_proj", [_a_spec(mix, tm)], [_b_nn(full["w_out"], 512)], [(0, 0)],
                lambda accs, ex: (ex[0] + accs[0],), [_e_tile(h1, tm, 512)], [F32], t, D, tm, 512)
    h3, ffn2_saved, _ = _ffn_fwd("ffn2", h2, w["ffn2_norm"], full["ffn2_w_in"], full["ffn2_w_out"])
    dh3, d_final_norm, loss_part = _final_loss(h3, target, w["final_norm"])

    grads, small = {}, {}
    small["final_norm"] = d_final_norm
    reduce_last = _Reduction("last", group_last, c_arr, k_arr)
    reduce_mid = _Reduction("mid", group_mid, c_arr, k_arr)
    reduce_first = _Reduction("first", group_first, c_arr, k_arr)
    dh2, small["ffn2_norm"], _, got_last = _ffn_bwd(
        "ffn2", h2, w["ffn2_norm"], full["ffn2_w_in"], full["ffn2_w_out"], ffn2_saved, dh3, None, reduce_last)

    def dmix_epi(accs, ex):
        dm = accs[0]
        ghg, gml, yhg, yml = [e.astype(F32) for e in ex]
        return dm * ghg, dm * gml, dm * yhg * ghg * (1.0 - ghg), dm * yml * gml * (1.0 - gml)

    dy_hg, dy_mla, dpre_hg, dpre_mla = _mm(
        "d_mix", [_a_spec(dh2, tm)], [_b_nt(full["w_out"], 512)], [(0, 0)], dmix_epi,
        [_e_tile(g_hg, tm, 512), _e_tile(g_mla, tm, 512), _e_tile(y_hg, tm, 512), _e_tile(y_mla, tm, 512)],
        [BF16, BF16, BF16, BF16], t, D, tm, 512, trans_b=True)
    grads["w_out"] = _mm_tn("dw_out", mix, dh2)
    small["b_merge"] = jnp.concatenate([_colsum("db_hg", dpre_hg), _colsum("db_mla", dpre_mla)], axis=1)
    grads["w_merge"] = jnp.concatenate([_mm_tn("dw_merge_hg", u, dpre_hg), _mm_tn("dw_merge_mla", u, dpre_mla)], axis=1)
    grads["w_hg_branch"] = _mm_tn("dw_hg_branch", hg_o, dy_hg)
    grads["w_mla_branch"] = _mm_tn("dw_mla_branch", o_mla, dy_mla)
    (dho,) = _mm("d_hg_o", [_a_spec(dy_hg, tm)], [_b_nt(full["w_hg_branch"], 512)], [(0, 0)], ident, [], [BF16], t, D, tm, 512, trans_b=True)
    (do_mla,) = _mm("d_o_mla", [_a_spec(dy_mla, tm)], [_b_nt(full["w_mla_branch"], 512)], [(0, 0)], ident, [], [BF16], t, D, tm, 512, trans_b=True)

    dq_raw, df_raw, di_raw, dg_raw, small["hg_lb_table"], small["hg_out_norm"] = _hgrn_bwd(
        p_hg, w["hg_lb_table"], w["hg_out_norm"], o_raw, states, dho)
    dp_hg = [dq_raw, df_raw, di_raw, dg_raw]

    dqh, dkh, dvh = _flash_bwd(qh, kh, vh, lse, _attn_delta(do_mla, o_mla), do_mla)
    dqf, dkvf, dkpe, dgq, dgk = _mla_prep_bwd(qf, kvf, p_mla, cos, sin, gq, gk, dqh, dkh, dvh)
    small["q_head_norm"] = dgq[:, :QK]
    small["k_head_norm"] = dgk[:, :QK]
    dwq_pad = _mm_tn("dw_q_up", cqn, dqf, tm=Q_LORA, tn=1024)
    grads["w_q_up"] = dwq_pad.reshape(Q_LORA, HEADS, QKP)[:, :, :QK].reshape(Q_LORA, HEADS * QK)
    grads["w_kv_up"] = _mm_tn("dw_kv_up", ckvn, dkvf, tm=KV_LORA, tn=1024)
    (dcqn,) = _mm("d_cq", [_a_spec(dqf, tm)], [_b_nt(w_q_pad, Q_LORA)], [(0, 0)], ident, [], [F32], t, Q_LORA, tm, Q_LORA, trans_b=True)
    (dckvn,) = _mm("d_ckv", [_a_spec(dkvf, tm)], [_b_nt(w_kv, KV_LORA)], [(0, 0)], ident, [], [F32], t, KV_LORA, tm, KV_LORA, trans_b=True)
    dp_mla, small["mla_q_lora_norm"], small["mla_kv_lora_norm"] = _lora_norm_bwd(
        p_mla, w["mla_q_lora_norm"], w["mla_kv_lora_norm"], dcqn, dckvn, dkpe)

    dw_in_hg = [_mm_tn("dw_in_hg%d" % k, u, dp_hg[k]) for k in range(4)]
    dw_in_mla = _mm_tn("dw_in_mla", u, dp_mla, tn=MLA_COLS)
    grads["w_in"] = jnp.concatenate(dw_in_hg + [dw_in_mla[:, :4800 - 4 * D]], axis=1)
    tm_du = min(TM // 2, t)
    (du,) = _mm(
        "d_u",
        [_a_spec(dpre_hg, tm_du), _a_spec(dpre_mla, tm_du)] + [_a_spec(d, tm_du) for d in dp_hg] + [_a_spec(dp_mla, tm_du)],
        [_b_nt(w_merge_f, 512, D, 0), _b_nt(w_merge_f, 512, D, 1)]
        + [_b_nt(w_in_hg, 512, D, k) for k in range(4)] + [_b_nt(w_in_mla, 512)],
        [(k, k) for k in range(7)],
        lambda accs, ex: (functools.reduce(lambda p, q: p + q, accs),), [], [F32], t, D, tm_du, 512, trans_b=True)
    dh1, small["mix_norm"] = _rms_bwd("mix_dnorm", h1, w["mix_norm"], du, dh2)
    dx, small["ffn1_norm"], (got_mid,), got_first = _ffn_bwd(
        "ffn1", xt, w["ffn1_norm"], full["ffn1_w_in"], full["ffn1_w_out"], ffn1_saved, dh1,
        reduce_mid.begin(grads), reduce_first)

    g_shard = {**reduce_last.end(got_last), **reduce_mid.end(got_mid), **reduce_first.end(got_first)}
    small_sum = _all_reduce_small(_pack_small([small[n] for n, _ in SMALL] + [loss_part])).reshape(-1)
    g_small, at = {}, 0
    for n, shape in SMALL:
        size = shape[0] * shape[1]
        g_small[n] = small_sum[at:at + size].reshape(shape)
        at += size
    loss = small_sum[at]

    g_out, d_out, m_out, v_out = {}, {}, {}, {}
    for n in WEIGHT_ORDER:
        shape = w[n].shape
        g = g_shard[n] if n in g_shard else g_small[n]
        two = g.shape
        d_, m_, v_ = _adamw("adamw_" + n, w[n].reshape(two), g, mom[n].reshape(two), var[n].reshape(two))
        g_out[n], d_out[n], m_out[n], v_out[n] = g.reshape(shape), d_.reshape(shape), m_.reshape(shape), v_.reshape(shape)

    return (loss, dx.reshape(x.shape), *[g_out[n] for n in WEIGHT_ORDER], *[d_out[n] for n in WEIGHT_ORDER],
            *[m_out[n] for n in WEIGHT_ORDER], *[v_out[n] for n in WEIGHT_ORDER])
```

```python
import functools

import numpy as np
import jax
import jax.numpy as jnp
from jax import lax
from jax.experimental import pallas as pl
from jax.experimental.pallas import tpu as pltpu

F32 = jnp.float32
BF16 = jnp.bfloat16
MESH = pl.DeviceIdType.MESH

D = 1024
DFF = 2816
HEADS = 8
HK = 128
CHUNK = 64
ROPE = 64
QK = 192
QKP = 256
Q_LORA = 384
KV_LORA = 256
MLA_COLS = 768
EPS = 1e-6
ROPE_THETA = 10000.0
SCALE = QK ** -0.5
LOG2E = 1.4426950408889634
LN2 = 0.6931471805599453
NEG = -1e30
EXP_CLAMP = 80.0

ADAM_LR = 0.001
ADAM_B1 = 0.9
ADAM_B2 = 0.999
ADAM_EPS = 1e-08
ADAM_WD = 0.01
ADAM_STEP = 10

PACK_W = 1024
ADD_ROWS = 352
PACK_ALIGN = 2 * ADD_ROWS

TM = 1024
FFN_TM = 512
FFN_CHUNK = 256
TQ = 1024
SUBQ = 256
HG_BT = 512
HG_HPB = 4
TT = 512
ROW_TM = 256

VMEM_MB = 48

BIG = (
    ("ffn1_w_in", D, 2 * DFF, 1),
    ("ffn1_w_out", DFF, D, 0),
    ("w_in", D, 4800, 1),
    ("w_hg_branch", D, D, 0),
    ("w_q_up", Q_LORA, HEADS * QK, 1),
    ("w_kv_up", KV_LORA, HEADS * 2 * HK, 1),
    ("w_mla_branch", D, D, 0),
    ("w_merge", D, 2 * D, 1),
    ("w_out", D, D, 0),
    ("ffn2_w_in", D, 2 * DFF, 1),
    ("ffn2_w_out", DFF, D, 0),
)
SMALL = (
    ("ffn1_norm", (1, D)),
    ("mix_norm", (1, D)),
    ("hg_lb_table", (2, D)),
    ("hg_out_norm", (1, HK)),
    ("mla_q_lora_norm", (1, Q_LORA)),
    ("mla_kv_lora_norm", (1, KV_LORA)),
    ("q_head_norm", (1, QK)),
    ("k_head_norm", (1, QK)),
    ("b_merge", (1, 2 * D)),
    ("ffn2_norm", (1, D)),
    ("final_norm", (1, D)),
)
WEIGHT_ORDER = ("ffn1_norm", "ffn1_w_in", "ffn1_w_out", "mix_norm", "w_in", "hg_lb_table", "hg_out_norm",
                "w_hg_branch", "mla_q_lora_norm", "w_q_up", "mla_kv_lora_norm", "w_kv_up", "q_head_norm",
                "k_head_norm", "w_mla_branch", "w_merge", "b_merge", "w_out", "ffn2_norm", "ffn2_w_in",
                "ffn2_w_out", "final_norm")


def _call(body, **kw):
    return pl.pallas_call(body, **kw)


def _cp(vmem_mb=VMEM_MB):
    return pltpu.CompilerParams(vmem_limit_bytes=vmem_mb << 20)


def _dot(a, b):
    return lax.dot_general(a, b, (((1,), (0,)), ((), ())), preferred_element_type=F32)


def _dot_nt(a, b):
    return lax.dot_general(a, b, (((1,), (1,)), ((), ())), preferred_element_type=F32)


def _dot_tn(a, b):
    return lax.dot_general(a, b, (((0,), (0,)), ((), ())), preferred_element_type=F32)


def _sig(x):
    return jax.nn.sigmoid(x)


def _silu(x):
    return x * _sig(x)


def _dsilu(x):
    s = _sig(x)
    return s * (1.0 + x * (1.0 - s))


def _a_spec(arr, tm, kblk=None, kidx=0):
    kb = arr.shape[1] if kblk is None else kblk
    return arr, pl.BlockSpec((tm, kb), lambda i, j, kidx=kidx: (i, kidx))


def _b_nn(arr, tn, off=0):
    return arr, pl.BlockSpec((arr.shape[0], tn), lambda i, j, off=off: (0, j + off))


def _b_nt(arr, tn, kblk=None, kidx=0):
    kb = arr.shape[1] if kblk is None else kblk
    return arr, pl.BlockSpec((tn, kb), lambda i, j, kidx=kidx: (j, kidx))


def _e_tile(arr, tm, tn, off=0):
    return arr, pl.BlockSpec((tm, tn), lambda i, j, off=off: (i, j + off))


def _e_row(arr, tn, off=0):
    return arr, pl.BlockSpec((1, tn), lambda i, j, off=off: (0, j + off))


def _mm(name, As, Bs, dots, epi, extras, out_dtypes, m, n, tm, tn, trans_b=False, side=None):
    na, nb, ne, no = len(As), len(Bs), len(extras), len(out_dtypes)
    ni, nj = m // tm, n // tn
    s_in = len(side.inputs) if side else 0
    s_out = len(side.out_shapes) if side else 0

    def body(*refs):
        a_refs = refs[:na]
        b_refs = refs[na:na + nb]
        e_refs = refs[na + nb:na + nb + ne]
        at = na + nb + ne
        side_refs = refs[at:at + s_in]
        o_refs = refs[at + s_in:at + s_in + no]
        side_refs = list(side_refs) + list(refs[at + s_in + no:])
        if side:
            i, j = pl.program_id(0), pl.program_id(1)

            @pl.when(jnp.logical_and(i == 0, j == 0))
            def _():
                side.start(*side_refs)

        a_vals = [r[...].astype(BF16) for r in a_refs]
        accs = []
        for ai, bi in dots:
            b = b_refs[bi][...]
            accs.append(_dot_nt(a_vals[ai], b) if trans_b else _dot(a_vals[ai], b))
        outs = epi(accs, [r[...] for r in e_refs])
        for o_ref, o in zip(o_refs, outs):
            o_ref[...] = o.astype(o_ref.dtype)
        if side:
            @pl.when(jnp.logical_and(i == ni - 1, j == nj - 1))
            def _():
                side.finish(*side_refs)

    ops = list(As) + list(Bs) + list(extras)
    anywhere = pl.BlockSpec(memory_space=pl.ANY)
    res = _call(
        body, name=name,
        grid=(ni, nj),
        in_specs=[s for _, s in ops] + [anywhere] * s_in,
        out_specs=[pl.BlockSpec((tm, tn), lambda i, j: (i, j)) for _ in out_dtypes] + [anywhere] * s_out,
        out_shape=[jax.ShapeDtypeStruct((m, n), dt) for dt in out_dtypes] + (list(side.out_shapes) if side else []),
        scratch_shapes=list(side.scratch) if side else [],
        compiler_params=_cp(),
    )(*[a for a, _ in ops], *(side.inputs if side else []))
    return res


def _rows_call(name, rows, weights, outs, compute, tm, side=None):
    t = rows[0].shape[0]
    nr, nw, no = len(rows), len(weights), len(outs)
    ni = t // tm
    s_in = len(side.inputs) if side else 0
    s_out = len(side.out_shapes) if side else 0

    def body(*refs):
        at = nr + nw
        side_refs = list(refs[at:at + s_in]) + list(refs[at + s_in + no:])
        if side:
            @pl.when(pl.program_id(0) == 0)
            def _():
                side.start(*side_refs)

        compute(refs[:nr], refs[nr:at], refs[at + s_in:at + s_in + no])
        if side:
            @pl.when(pl.program_id(0) == ni - 1)
            def _():
                side.finish(*side_refs)

    anywhere = pl.BlockSpec(memory_space=pl.ANY)
    return _call(
        body, name=name, grid=(ni,),
        in_specs=[pl.BlockSpec((tm, r.shape[1]), lambda i: (i, 0)) for r in rows]
        + [pl.BlockSpec(wt.shape, lambda i: (0, 0)) for wt in weights] + [anywhere] * s_in,
        out_specs=[pl.BlockSpec((tm, width), lambda i: (i, 0)) for width, _ in outs] + [anywhere] * s_out,
        out_shape=[jax.ShapeDtypeStruct((t, width), dt) for width, dt in outs] + (list(side.out_shapes) if side else []),
        scratch_shapes=list(side.scratch) if side else [],
        compiler_params=_cp(),
    )(*rows, *weights, *(side.inputs if side else []))


def _mm_tn(name, a, b, scale=1.0, tm=1024, tn=1024):
    t, m = a.shape
    n = b.shape[1]
    tm, tn, tt = min(tm, m), min(tn, n), min(TT, t)
    nk = t // tt

    def body(a_ref, b_ref, o_ref):
        k = pl.program_id(2)

        @pl.when(k == 0)
        def _():
            o_ref[...] = jnp.zeros_like(o_ref)

        o_ref[...] += _dot_tn(a_ref[...].astype(BF16), b_ref[...].astype(BF16))
        if scale != 1.0:
            @pl.when(k == nk - 1)
            def _():
                o_ref[...] = o_ref[...] * scale

    return _call(
        body, name=name,
        grid=(m // tm, n // tn, nk),
        in_specs=[pl.BlockSpec((tt, tm), lambda i, j, k: (k, i)), pl.BlockSpec((tt, tn), lambda i, j, k: (k, j))],
        out_specs=pl.BlockSpec((tm, tn), lambda i, j, k: (i, j)),
        out_shape=jax.ShapeDtypeStruct((m, n), F32),
        compiler_params=_cp(),
    )(a, b)


def _rms_fwd(name, x, gain):
    t, d = x.shape
    tm = min(ROW_TM, t)

    def body(x_ref, g_ref, o_ref):
        xv = x_ref[...]
        r = lax.rsqrt(jnp.mean(xv * xv, axis=-1, keepdims=True) + EPS)
        o_ref[...] = (xv * r * g_ref[...]).astype(o_ref.dtype)

    return _call(
        body, name=name, grid=(t // tm,),
        in_specs=[pl.BlockSpec((tm, d), lambda i: (i, 0)), pl.BlockSpec((1, d), lambda i: (0, 0))],
        out_specs=pl.BlockSpec((tm, d), lambda i: (i, 0)),
        out_shape=jax.ShapeDtypeStruct((t, d), BF16),
        compiler_params=_cp(),
    )(x, gain)


def _rms_bwd_vals(xv, g, dn):
    r = lax.rsqrt(jnp.mean(xv * xv, axis=-1, keepdims=True) + EPS)
    xh = xv * r
    dxh = dn * g
    c = jnp.mean(dxh * xh, axis=-1, keepdims=True)
    return r * (dxh - xh * c), dn * xh


def _rms_bwd(name, x, gain, dn, dres):
    t, d = x.shape
    tm = min(ROW_TM, t)

    def body(x_ref, g_ref, dn_ref, dr_ref, dx_ref, dg_ref):
        @pl.when(pl.program_id(0) == 0)
        def _():
            dg_ref[...] = jnp.zeros_like(dg_ref)

        dx, dg = _rms_bwd_vals(x_ref[...], g_ref[...], dn_ref[...].astype(F32))
        dx_ref[...] = dr_ref[...] + dx
        dg_ref[...] += jnp.sum(dg, axis=0, keepdims=True)

    row = pl.BlockSpec((tm, d), lambda i: (i, 0))
    one = pl.BlockSpec((1, d), lambda i: (0, 0))
    return _call(
        body, name=name, grid=(t // tm,),
        in_specs=[row, one, row, row],
        out_specs=[row, one],
        out_shape=[jax.ShapeDtypeStruct((t, d), F32), jax.ShapeDtypeStruct((1, d), F32)],
        compiler_params=_cp(),
    )(x, gain, dn, dres)


def _final_loss(h, target, gain):
    t, d = h.shape
    tm = min(ROW_TM, t)

    def body(h_ref, t_ref, g_ref, dh_ref, dg_ref, l_ref):
        @pl.when(pl.program_id(0) == 0)
        def _():
            dg_ref[...] = jnp.zeros_like(dg_ref)
            l_ref[...] = jnp.zeros_like(l_ref)

        hv = h_ref[...]
        g = g_ref[...]
        r = lax.rsqrt(jnp.mean(hv * hv, axis=-1, keepdims=True) + EPS)
        xh = hv * r
        err = xh * g - t_ref[...]
        l_ref[...] += 0.5 * jnp.sum(jnp.mean(err * err, axis=-1, keepdims=True), axis=0, keepdims=True)
        dy = err * (1.0 / d)
        dxh = dy * g
        c = jnp.mean(dxh * xh, axis=-1, keepdims=True)
        dh_ref[...] = r * (dxh - xh * c)
        dg_ref[...] += jnp.sum(dy * xh, axis=0, keepdims=True)

    row = pl.BlockSpec((tm, d), lambda i: (i, 0))
    one = pl.BlockSpec((1, d), lambda i: (0, 0))
    return _call(
        body, name="final_loss", grid=(t // tm,),
        in_specs=[row, row, one],
        out_specs=[row, one, pl.BlockSpec((1, 128), lambda i: (0, 0))],
        out_shape=[jax.ShapeDtypeStruct((t, d), F32), jax.ShapeDtypeStruct((1, d), F32),
                   jax.ShapeDtypeStruct((1, 128), F32)],
        compiler_params=_cp(),
    )(h, target, gain)


def _colsum(name, x):
    t, n = x.shape
    tm = min(TM, t)

    def body(x_ref, o_ref):
        @pl.when(pl.program_id(0) == 0)
        def _():
            o_ref[...] = jnp.zeros_like(o_ref)

        o_ref[...] += jnp.sum(x_ref[...].astype(F32), axis=0, keepdims=True)

    return _call(
        body, name=name, grid=(t // tm,),
        in_specs=[pl.BlockSpec((tm, n), lambda i: (i, 0))],
        out_specs=pl.BlockSpec((1, n), lambda i: (0, 0)),
        out_shape=jax.ShapeDtypeStruct((1, n), F32),
        compiler_params=_cp(),
    )(x)


def _lora_norm_fwd(p_mla, gq, gkv):
    t = p_mla.shape[0]
    tm = min(ROW_TM, t)

    def body(p_ref, gq_ref, gkv_ref, q_ref, kv_ref):
        cq = p_ref[:, 0:Q_LORA]
        ckv = p_ref[:, Q_LORA:Q_LORA + KV_LORA]
        rq = lax.rsqrt(jnp.mean(cq * cq, axis=-1, keepdims=True) + EPS)
        rkv = lax.rsqrt(jnp.mean(ckv * ckv, axis=-1, keepdims=True) + EPS)
        q_ref[...] = (cq * rq * gq_ref[...]).astype(BF16)
        kv_ref[...] = (ckv * rkv * gkv_ref[...]).astype(BF16)

    return _call(
        body, name="lora_norm_fwd", grid=(t // tm,),
        in_specs=[pl.BlockSpec((tm, MLA_COLS), lambda i: (i, 0)),
                  pl.BlockSpec((1, Q_LORA), lambda i: (0, 0)), pl.BlockSpec((1, KV_LORA), lambda i: (0, 0))],
        out_specs=[pl.BlockSpec((tm, Q_LORA), lambda i: (i, 0)), pl.BlockSpec((tm, KV_LORA), lambda i: (i, 0))],
        out_shape=[jax.ShapeDtypeStruct((t, Q_LORA), BF16), jax.ShapeDtypeStruct((t, KV_LORA), BF16)],
        compiler_params=_cp(),
    )(p_mla, gq, gkv)


def _lora_norm_bwd(p_mla, gq, gkv, dcqn, dckvn, dkpe):
    t = p_mla.shape[0]
    tm = min(ROW_TM, t)

    def body(p_ref, gq_ref, gkv_ref, dq_ref, dkv_ref, dkpe_ref, dp_ref, dgq_ref, dgkv_ref):
        @pl.when(pl.program_id(0) == 0)
        def _():
            dgq_ref[...] = jnp.zeros_like(dgq_ref)
            dgkv_ref[...] = jnp.zeros_like(dgkv_ref)

        dcq, dgq = _rms_bwd_vals(p_ref[:, 0:Q_LORA], gq_ref[...], dq_ref[...])
        dckv, dgkv = _rms_bwd_vals(p_ref[:, Q_LORA:Q_LORA + KV_LORA], gkv_ref[...], dkv_ref[...])
        dp_ref[:, 0:Q_LORA] = dcq.astype(BF16)
        dp_ref[:, Q_LORA:Q_LORA + KV_LORA] = dckv.astype(BF16)
        dp_ref[:, Q_LORA + KV_LORA:MLA_COLS] = dkpe_ref[...].astype(BF16)
        dgq_ref[...] += jnp.sum(dgq, axis=0, keepdims=True)
        dgkv_ref[...] += jnp.sum(dgkv, axis=0, keepdims=True)

    return _call(
        body, name="lora_norm_bwd", grid=(t // tm,),
        in_specs=[pl.BlockSpec((tm, MLA_COLS), lambda i: (i, 0)),
                  pl.BlockSpec((1, Q_LORA), lambda i: (0, 0)), pl.BlockSpec((1, KV_LORA), lambda i: (0, 0)),
                  pl.BlockSpec((tm, Q_LORA), lambda i: (i, 0)), pl.BlockSpec((tm, KV_LORA), lambda i: (i, 0)),
                  pl.BlockSpec((tm, HK), lambda i: (i, 0))],
        out_specs=[pl.BlockSpec((tm, MLA_COLS), lambda i: (i, 0)),
                   pl.BlockSpec((1, Q_LORA), lambda i: (0, 0)), pl.BlockSpec((1, KV_LORA), lambda i: (0, 0))],
        out_shape=[jax.ShapeDtypeStruct((t, MLA_COLS), BF16), jax.ShapeDtypeStruct((1, Q_LORA), F32),
                   jax.ShapeDtypeStruct((1, KV_LORA), F32)],
        compiler_params=_cp(),
    )(p_mla, gq, gkv, dcqn, dckvn, dkpe)


def _cumsum_rows(x, row):
    for s in (1, 2, 4, 8, 16, 32):
        x = x + jnp.where(row >= s, pltpu.roll(x, s, 0), 0.0)
    return x


def _rcumsum_rows(x, row):
    for s in (1, 2, 4, 8, 16, 32):
        x = x + jnp.where(row < CHUNK - s, pltpu.roll(x, CHUNK - s, 0), 0.0)
    return x


def _hg_gates(qr, z, lb, row):
    q = _silu(qr)
    sg = _sig(z)
    f = lb + (1.0 - lb) * sg
    lf = jnp.log(f)
    k = (1.0 - lb) * (1.0 - sg)
    cum = _cumsum_rows(lf, row)
    mid = jnp.sum(jnp.where(row < CHUNK // 2, lf, 0.0), axis=0, keepdims=True)
    last = jnp.sum(lf, axis=0, keepdims=True)
    e_q = jnp.exp(jnp.minimum(cum - mid, EXP_CLAMP))
    e_k = jnp.exp(jnp.minimum(mid - cum, EXP_CLAMP))
    e_a = jnp.exp(cum)
    e_l = jnp.exp(last - cum)
    return q, sg, f, k, last, e_q, e_k, e_a, e_l


def _hgrn_fwd(p_hg, tab, gain):
    t = p_hg.shape[0]
    bt = min(HG_BT, t)
    nb, nc = t // bt, bt // CHUNK

    hpb = HG_HPB
    wide = hpb * HK

    def body(q_ref, f_ref, i_ref, g_ref, tab_ref, gain_ref, o_ref, ho_ref, st_ref, state):
        @pl.when(pl.program_id(1) == 0)
        def _():
            state[...] = jnp.zeros_like(state)

        row = lax.broadcasted_iota(jnp.int32, (CHUNK, HK), 0)
        tril = lax.broadcasted_iota(jnp.int32, (CHUNK, CHUNK), 0) >= lax.broadcasted_iota(jnp.int32, (CHUNK, CHUNK), 1)
        gain_v = gain_ref[...]

        def chunk(c, carry):
            sl = pl.ds(pl.multiple_of(c * CHUNK, CHUNK), CHUNK)
            for hh in range(hpb):
                ln = slice(hh * HK, (hh + 1) * HK)
                lb = _sig(tab_ref[0:1, ln] - tab_ref[1:2, ln])
                v = i_ref[sl, ln].astype(BF16)
                q, _, _, k, last, e_q, e_k, e_a, e_l = _hg_gates(q_ref[sl, ln], f_ref[sl, ln], lb, row)
                st = state[hh]
                st_ref[hh, c] = st
                p = jnp.where(tril, _dot_nt((q * e_q).astype(BF16), (k * e_k).astype(BF16)), 0.0)
                o = _dot(p.astype(BF16), v) + _dot_nt((q * e_a).astype(BF16), st.astype(BF16))
                state[hh] = jnp.exp(last) * st + _dot_tn(v, (k * e_l).astype(BF16))
                o_ref[sl, ln] = o
                r = lax.rsqrt(jnp.mean(o * o, axis=-1, keepdims=True) + EPS)
                ho_ref[sl, ln] = (o * r * gain_v * _silu(g_ref[sl, ln])).astype(BF16)
            return carry

        lax.fori_loop(0, nc, chunk, 0)

    def col(k):
        return pl.BlockSpec((bt, wide), lambda h, j, k=k: (j, k * (HEADS // hpb) + h))

    return _call(
        body, name="hgrn_fwd", grid=(HEADS // hpb, nb),
        in_specs=[col(0), col(1), col(2), col(3),
                  pl.BlockSpec((2, wide), lambda h, j: (0, h)), pl.BlockSpec((1, HK), lambda h, j: (0, 0))],
        out_specs=[pl.BlockSpec((bt, wide), lambda h, j: (j, h)), pl.BlockSpec((bt, wide), lambda h, j: (j, h)),
                   pl.BlockSpec((hpb, nc, HK, HK), lambda h, j: (h, j, 0, 0))],
        out_shape=[jax.ShapeDtypeStruct((t, D), F32), jax.ShapeDtypeStruct((t, D), BF16),
                   jax.ShapeDtypeStruct((HEADS, t // CHUNK, HK, HK), F32)],
        scratch_shapes=[pltpu.VMEM((hpb, HK, HK), F32)],
        compiler_params=_cp(),
    )(p_hg, p_hg, p_hg, p_hg, tab, gain)


def _hgrn_bwd(p_hg, tab, gain, o_raw, states, dho):
    t = p_hg.shape[0]
    bt = min(HG_BT, t)
    nb, nc = t // bt, bt // CHUNK
    hpb = HG_HPB
    wide = hpb * HK

    def body(q_ref, f_ref, i_ref, g_ref, tab_ref, gain_ref, o_ref, st_ref, dho_ref,
             dq_ref, df_ref, di_ref, dg_ref, dtab_ref, dgain_ref, dstate, dlb):
        h, j = pl.program_id(0), pl.program_id(1)

        @pl.when(jnp.logical_and(h == 0, j == 0))
        def _():
            dgain_ref[...] = jnp.zeros_like(dgain_ref)

        @pl.when(j == 0)
        def _():
            dstate[...] = jnp.zeros_like(dstate)
            dlb[...] = jnp.zeros_like(dlb)

        row = lax.broadcasted_iota(jnp.int32, (CHUNK, HK), 0)
        tril = lax.broadcasted_iota(jnp.int32, (CHUNK, CHUNK), 0) >= lax.broadcasted_iota(jnp.int32, (CHUNK, CHUNK), 1)
        gain_v = gain_ref[...]

        def chunk(cc, carry):
            c = nc - 1 - cc
            sl = pl.ds(pl.multiple_of(c * CHUNK, CHUNK), CHUNK)
            dgain = jnp.zeros((1, HK), F32)
            for hh in range(hpb):
                ln = slice(hh * HK, (hh + 1) * HK)
                lb = _sig(tab_ref[0:1, ln] - tab_ref[1:2, ln])
                qr = q_ref[sl, ln]
                v = i_ref[sl, ln].astype(BF16)
                gr = g_ref[sl, ln]
                q, sg, f, k, last, e_q, e_k, e_a, e_l = _hg_gates(qr, f_ref[sl, ln], lb, row)
                o = o_ref[sl, ln]
                r = lax.rsqrt(jnp.mean(o * o, axis=-1, keepdims=True) + EPS)
                oh = o * r
                dh = dho_ref[sl, ln].astype(F32)
                dnorm = dh * _silu(gr)
                dg_ref[sl, ln] = (dh * oh * gain_v * _dsilu(gr)).astype(BF16)
                dgain = dgain + jnp.sum(dnorm * oh, axis=0, keepdims=True)
                dxh = dnorm * gain_v
                do = (r * (dxh - oh * jnp.mean(dxh * oh, axis=-1, keepdims=True))).astype(BF16)
                st0 = st_ref[hh, c]
                st0_b = st0.astype(BF16)
                ds1 = dstate[hh]
                ds1_b = ds1.astype(BF16)
                qt = (q * e_q).astype(BF16)
                kt = (k * e_k).astype(BF16)
                qd = (q * e_a).astype(BF16)
                kd = (k * e_l).astype(BF16)
                p = jnp.where(tril, _dot_nt(qt, kt), 0.0).astype(BF16)
                dp = jnp.where(tril, _dot_nt(do, v), 0.0).astype(BF16)
                dv = _dot_tn(p, do) + _dot_nt(kd, ds1_b)
                dqt = _dot(dp, kt)
                dkt = _dot_tn(dp, qt)
                dq_inter = _dot(do, st0_b) * e_a
                dk_inter = _dot(v, ds1_b) * e_l
                dq = dqt * e_q + dq_inter
                dk = dkt * e_k + dk_inter
                e_last = jnp.exp(last)
                dstate[hh] = _dot_tn(do, qd) + e_last * ds1
                dlast = (jnp.sum(k * dk_inter, axis=0, keepdims=True)
                         + e_last * jnp.sum(ds1 * st0, axis=0, keepdims=True))
                da = (qt.astype(F32) * dqt - kt.astype(F32) * dkt + q * dq_inter - k * dk_inter
                      + jnp.where(row == CHUNK - 1, dlast, 0.0))
                dlf = _rcumsum_rows(da, row)
                dfv = dlf / f - dk
                df_ref[sl, ln] = (dfv * (1.0 - lb) * sg * (1.0 - sg)).astype(BF16)
                dlb[:, ln] += jnp.sum(dfv * (1.0 - sg), axis=0, keepdims=True)
                dq_ref[sl, ln] = (dq * _dsilu(qr)).astype(BF16)
                di_ref[sl, ln] = dv.astype(BF16)
            dgain_ref[...] += dgain
            return carry

        lax.fori_loop(0, nc, chunk, 0)

        @pl.when(j == nb - 1)
        def _():
            lb = _sig(tab_ref[0:1, :] - tab_ref[1:2, :])
            d0 = dlb[...] * lb * (1.0 - lb)
            dtab_ref[0:1, :] = d0
            dtab_ref[1:2, :] = -d0

    def col(k):
        return pl.BlockSpec((bt, wide), lambda h, j, k=k: (nb - 1 - j, k * (HEADS // hpb) + h))

    tok = pl.BlockSpec((bt, wide), lambda h, j: (nb - 1 - j, h))
    return _call(
        body, name="hgrn_bwd", grid=(HEADS // hpb, nb),
        in_specs=[col(0), col(1), col(2), col(3),
                  pl.BlockSpec((2, wide), lambda h, j: (0, h)), pl.BlockSpec((1, HK), lambda h, j: (0, 0)),
                  tok, pl.BlockSpec((hpb, nc, HK, HK), lambda h, j: (h, nb - 1 - j, 0, 0)), tok],
        out_specs=[tok, tok, tok, tok,
                   pl.BlockSpec((2, wide), lambda h, j: (0, h)), pl.BlockSpec((1, HK), lambda h, j: (0, 0))],
        out_shape=[jax.ShapeDtypeStruct((t, D), BF16)] * 4
        + [jax.ShapeDtypeStruct((2, D), F32), jax.ShapeDtypeStruct((1, HK), F32)],
        scratch_shapes=[pltpu.VMEM((hpb, HK, HK), F32), pltpu.VMEM((1, wide), F32)],
        compiler_params=_cp(),
    )(p_hg, p_hg, p_hg, p_hg, tab, gain, o_raw, states, dho)


def _rope_tables(pos):
    t = pos.shape[0]
    tm = min(ROW_TM, t)
    inv = np.zeros((1, HK), np.float32)
    freq = (ROPE_THETA ** (-np.arange(0, ROPE, 2, dtype=np.float32) / ROPE)).astype(np.float32)
    inv[0, 0:ROPE // 2] = freq
    inv[0, ROPE // 2:ROPE] = freq
    sign = np.zeros((1, HK), np.float32)
    sign[0, 0:ROPE // 2] = -1.0
    sign[0, ROPE // 2:ROPE] = 1.0

    def body(pos_ref, inv_ref, sign_ref, cos_ref, sin_ref):
        ang = pos_ref[...].astype(F32) * inv_ref[...]
        cos_ref[...] = jnp.cos(ang)
        sin_ref[...] = jnp.sin(ang) * sign_ref[...]

    one = pl.BlockSpec((1, HK), lambda i: (0, 0))
    row = pl.BlockSpec((tm, HK), lambda i: (i, 0))
    return _call(
        body, name="rope_tables", grid=(t // tm,),
        in_specs=[pl.BlockSpec((tm, 1), lambda i: (i, 0)), one, one],
        out_specs=[row, row],
        out_shape=[jax.ShapeDtypeStruct((t, HK), F32)] * 2,
        compiler_params=_cp(),
    )(pos, jnp.asarray(inv), jnp.asarray(sign))


def _rope(x, cos, sin_signed):
    r = lax.broadcasted_iota(jnp.int32, (HK, HK), 0)
    c = lax.broadcasted_iota(jnp.int32, (HK, HK), 1)
    half = ROPE // 2
    swap = jnp.logical_or(jnp.logical_and(c < half, r == c + half),
                          jnp.logical_and(jnp.logical_and(c >= half, c < ROPE), r == c - half))
    return x * cos + _dot_split(x, swap.astype(BF16)) * sin_signed


def _dot_split(x, m):
    hi = x.astype(BF16)
    lo = (x - hi.astype(F32)).astype(BF16)
    return _dot(hi, m) + _dot(lo, m)


def _lane_sum(x):
    return _dot_split(x, jnp.ones((HK, HK), BF16))


def _head_norm(xn, xr):
    r = lax.rsqrt(_lane_sum(xn * xn + xr * xr) * (1.0 / QK) + EPS)
    return xn * r, xr * r, r


def _head_norm_bwd(xn, xr, g_n, g_r, dn, dr):
    hn, hr, r = _head_norm(xn, xr)
    dxn, dxr = dn * g_n, dr * g_r
    c = _lane_sum(dxn * hn + dxr * hr) * (1.0 / QK)
    return r * (dxn - hn * c), r * (dxr - hr * c), dn * hn, dr * hr


def _mla_prep_fwd(qf, kv, p_mla, cos, sin, gq, gk):
    t = qf.shape[0]
    tm = min(ROW_TM, t)

    def body(qf_ref, kv_ref, kpe_ref, cos_ref, sin_ref, gq_ref, gk_ref, q_ref, k_ref, v_ref):
        cos_v, sin_v = cos_ref[...], sin_ref[...]
        kpe = kpe_ref[...]
        for h in range(HEADS):
            lo, mid, hi = h * QKP, h * QKP + HK, (h + 1) * QKP
            qn, qr, _ = _head_norm(qf_ref[:, lo:mid], qf_ref[:, mid:hi])
            q_ref[h, :, 0:HK] = (qn * gq_ref[:, 0:HK] * (SCALE * LOG2E)).astype(BF16)
            q_ref[h, :, HK:QKP] = (_rope(qr * gq_ref[:, HK:QKP], cos_v, sin_v) * (SCALE * LOG2E)).astype(BF16)
            kn, kr, _ = _head_norm(kv_ref[:, lo:mid], kpe)
            k_ref[h, :, 0:HK] = (kn * gk_ref[:, 0:HK]).astype(BF16)
            k_ref[h, :, HK:QKP] = _rope(kr * gk_ref[:, HK:QKP], cos_v, sin_v).astype(BF16)
            v_ref[h, :, 0:HK] = kv_ref[:, mid:hi].astype(BF16)
            v_ref[h, :, HK:QKP] = jnp.full((tm, HK), -1.0, BF16)

    head = pl.BlockSpec((tm, HEADS * QKP), lambda i: (i, 0))
    tok = pl.BlockSpec((tm, HK), lambda i: (i, 0))
    gain = pl.BlockSpec((1, QKP), lambda i: (0, 0))
    return _call(
        body, name="mla_prep_fwd", grid=(t // tm,),
        in_specs=[head, head, pl.BlockSpec((tm, HK), lambda i: (i, MLA_COLS // HK - 1)), tok, tok, gain, gain],
        out_specs=[pl.BlockSpec((HEADS, tm, QKP), lambda i: (0, i, 0)),
                   pl.BlockSpec((HEADS, tm, QKP), lambda i: (0, i, 0)),
                   pl.BlockSpec((HEADS, tm, QKP), lambda i: (0, i, 0))],
        out_shape=[jax.ShapeDtypeStruct((HEADS, t, QKP), BF16), jax.ShapeDtypeStruct((HEADS, t, QKP), BF16),
                   jax.ShapeDtypeStruct((HEADS, t, QKP), BF16)],
        compiler_params=_cp(),
    )(qf, kv, p_mla, cos, sin, gq, gk)


def _mla_prep_bwd(qf, kv, p_mla, cos, sin, gq, gk, dq, dk, dv):
    t = qf.shape[0]
    tm = min(ROW_TM, t)

    def body(qf_ref, kv_ref, kpe_ref, cos_ref, sin_ref, gq_ref, gk_ref, dq_ref, dk_ref, dv_ref,
             dqf_ref, dkv_ref, dkpe_ref, dgq_ref, dgk_ref):
        @pl.when(pl.program_id(0) == 0)
        def _():
            dgq_ref[...] = jnp.zeros_like(dgq_ref)
            dgk_ref[...] = jnp.zeros_like(dgk_ref)

        cos_v, sin_v = cos_ref[...], -sin_ref[...]
        kpe = kpe_ref[...]
        gqn, gqr, gkn, gkr = gq_ref[:, 0:HK], gq_ref[:, HK:QKP], gk_ref[:, 0:HK], gk_ref[:, HK:QKP]
        dkpe = jnp.zeros((tm, HK), F32)
        dgq_n, dgq_r, dgk_n, dgk_r = [jnp.zeros((1, HK), F32) for _ in range(4)]
        for h in range(HEADS):
            lo, mid, hi = h * QKP, h * QKP + HK, (h + 1) * QKP
            dqn = dq_ref[h, :, 0:HK].astype(F32) * SCALE
            dqr = _rope(dq_ref[h, :, HK:QKP].astype(F32), cos_v, sin_v) * SCALE
            a, b, ga, gb = _head_norm_bwd(qf_ref[:, lo:mid], qf_ref[:, mid:hi], gqn, gqr, dqn, dqr)
            dqf_ref[:, lo:mid] = a.astype(BF16)
            dqf_ref[:, mid:hi] = b.astype(BF16)
            dgq_n = dgq_n + jnp.sum(ga, axis=0, keepdims=True)
            dgq_r = dgq_r + jnp.sum(gb, axis=0, keepdims=True)
            dkn = dk_ref[h, :, 0:HK].astype(F32) * LN2
            dkr = _rope(dk_ref[h, :, HK:QKP].astype(F32), cos_v, sin_v) * LN2
            a, b, ga, gb = _head_norm_bwd(kv_ref[:, lo:mid], kpe, gkn, gkr, dkn, dkr)
            dkv_ref[:, lo:mid] = a.astype(BF16)
            dkv_ref[:, mid:hi] = dv_ref[h].astype(BF16)
            dkpe = dkpe + b
            dgk_n = dgk_n + jnp.sum(ga, axis=0, keepdims=True)
            dgk_r = dgk_r + jnp.sum(gb, axis=0, keepdims=True)
        dkpe_ref[...] = dkpe
        dgq_ref[:, 0:HK] += dgq_n
        dgq_ref[:, HK:QKP] += dgq_r
        dgk_ref[:, 0:HK] += dgk_n
        dgk_ref[:, HK:QKP] += dgk_r

    head = pl.BlockSpec((tm, HEADS * QKP), lambda i: (i, 0))
    tok = pl.BlockSpec((tm, HK), lambda i: (i, 0))
    gain = pl.BlockSpec((1, QKP), lambda i: (0, 0))
    hq = pl.BlockSpec((HEADS, tm, QKP), lambda i: (0, i, 0))
    return _call(
        body, name="mla_prep_bwd", grid=(t // tm,),
        in_specs=[head, head, pl.BlockSpec((tm, HK), lambda i: (i, MLA_COLS // HK - 1)), tok, tok, gain, gain,
                  hq, hq, pl.BlockSpec((HEADS, tm, HK), lambda i: (0, i, 0))],
        out_specs=[head, head, tok, gain, gain],
        out_shape=[jax.ShapeDtypeStruct((t, HEADS * QKP), BF16), jax.ShapeDtypeStruct((t, HEADS * QKP), BF16),
                   jax.ShapeDtypeStruct((t, HK), F32), jax.ShapeDtypeStruct((1, QKP), F32),
                   jax.ShapeDtypeStruct((1, QKP), F32)],
        compiler_params=_cp(),
    )(qf, kv, p_mla, cos, sin, gq, gk, dq, dk, dv)


def _chunk_mask(row0, rows, cols):
    r = lax.broadcasted_iota(jnp.int32, (rows, cols), 0) + row0
    c = lax.broadcasted_iota(jnp.int32, (rows, cols), 1)
    return jnp.right_shift(r, 6) >= jnp.right_shift(c, 6)


def _flash_fwd(q, k, v):
    t = q.shape[1]
    tq = min(TQ, t)
    nq = t // tq
    sub = min(SUBQ, tq)
    pairs = [(i, j) for i in range(nq) for j in range(i + 1)]
    qi = jnp.asarray([p[0] for p in pairs], jnp.int32)
    kj = jnp.asarray([p[1] for p in pairs], jnp.int32)

    def body(qi_ref, kj_ref, q_ref, k_ref, v_ref, o_ref, lse_ref, m_s, acc_s):
        n = pl.program_id(1)
        i, j = qi_ref[n], kj_ref[n]

        @pl.when(j == 0)
        def _():
            m_s[...] = jnp.full_like(m_s, NEG)
            acc_s[...] = jnp.zeros_like(acc_s)

        def step(diag):
            subs = range(tq // sub)
            width = [(r + 1) * sub if diag else tq for r in subs]
            logits = [_dot_nt(q_ref[r * sub:(r + 1) * sub, :], k_ref[0:width[r], :]) for r in subs]
            for r in subs:
                rows = slice(r * sub, (r + 1) * sub)
                cols = width[r]
                s = logits[r]
                if diag:
                    s = jnp.where(_chunk_mask(r * sub, sub, cols), s, NEG)
                m_old = m_s[rows, :]
                m_new = jnp.maximum(m_old, jnp.max(s, axis=-1, keepdims=True))
                alpha = jnp.exp2(m_old - m_new)
                p = jnp.exp2((s - jnp.tile(m_new, (1, cols // HK))).astype(BF16))
                acc_s[rows, :] = jnp.tile(alpha, (1, 2)) * acc_s[rows, :] + _dot(p, v_ref[0:cols, :])
                m_s[rows, :] = m_new

        @pl.when(j < i)
        def _():
            step(False)

        @pl.when(j == i)
        def _():
            step(True)
            l = -acc_s[:, HK:QKP]
            o_ref[...] = (acc_s[:, 0:HK] / l).astype(BF16)
            lse_ref[...] = m_s[...] + jnp.log(l) * LOG2E

    grid_spec = pltpu.PrefetchScalarGridSpec(
        num_scalar_prefetch=2, grid=(HEADS, len(pairs)),
        in_specs=[pl.BlockSpec((None, tq, QKP), lambda h, n, qi, kj: (h, qi[n], 0)),
                  pl.BlockSpec((None, tq, QKP), lambda h, n, qi, kj: (h, kj[n], 0)),
                  pl.BlockSpec((None, tq, QKP), lambda h, n, qi, kj: (h, kj[n], 0))],
        out_specs=[pl.BlockSpec((tq, HK), lambda h, n, qi, kj: (qi[n], h)),
                   pl.BlockSpec((None, tq, HK), lambda h, n, qi, kj: (h, qi[n], 0))],
        scratch_shapes=[pltpu.VMEM((tq, HK), F32), pltpu.VMEM((tq, QKP), F32)],
    )
    return _call(
        body, name="flash_fwd", grid_spec=grid_spec,
        out_shape=[jax.ShapeDtypeStruct((t, D), BF16), jax.ShapeDtypeStruct((HEADS, t, HK), F32)],
        compiler_params=_cp(),
    )(qi, kj, q, k, v)


def _attn_do(do, o):
    t = do.shape[0]
    tm = min(TM, t)

    def body(do_ref, o_ref, d_ref):
        lane = lax.broadcasted_iota(jnp.int32, (tm, HK), 1)
        for h in range(HEADS):
            ln = slice(h * HK, (h + 1) * HK)
            dov = do_ref[:, ln]
            d = jnp.sum(dov.astype(F32) * o_ref[:, ln].astype(F32), axis=-1, keepdims=True)
            hi = d.astype(BF16).astype(F32)
            d_ref[h, :, 0:HK] = dov
            d_ref[h, :, HK:QKP] = jnp.where(lane == 0, hi, jnp.where(lane == 1, d - hi, 0.0)).astype(BF16)

    blk = pl.BlockSpec((tm, D), lambda i: (i, 0))
    return _call(
        body, name="attn_do", grid=(t // tm,),
        in_specs=[blk, blk],
        out_specs=pl.BlockSpec((HEADS, tm, QKP), lambda i: (0, i, 0)),
        out_shape=jax.ShapeDtypeStruct((HEADS, t, QKP), BF16),
        compiler_params=_cp(),
    )(do, o)


def _flash_bwd(q, k, v, lse, do):
    t = q.shape[1]
    tq = min(TQ, t)
    nq = t // tq
    sub = min(SUBQ, tq)
    pairs = [(i, j) for j in range(nq) for i in range(j, nq)]
    qi = jnp.asarray([p[0] for p in pairs], jnp.int32)
    kj = jnp.asarray([p[1] for p in pairs], jnp.int32)
    npairs = len(pairs)

    def body(qi_ref, kj_ref, q_ref, k_ref, v_ref, lse_ref, do_ref, dq_ref, dk_ref, dv_ref):
        n = pl.program_id(1)
        i, j = qi_ref[n], kj_ref[n]

        @pl.when(n == 0)
        def _():
            dq_ref[...] = jnp.zeros_like(dq_ref)

        @pl.when(i == j)
        def _():
            dk_ref[...] = jnp.zeros_like(dk_ref)
            dv_ref[...] = jnp.zeros_like(dv_ref)

        def step(diag):
            for r in range(tq // sub):
                rows = slice(r * sub, (r + 1) * sub)
                cols = (r + 1) * sub if diag else tq
                qv, kv_ = q_ref[rows, :], k_ref[0:cols, :]
                p = jnp.exp2(_dot_nt(qv, kv_) - jnp.tile(lse_ref[rows, :], (1, cols // HK)))
                if diag:
                    p = jnp.where(_chunk_mask(r * sub, sub, cols), p, 0.0)
                dp_less_delta = _dot_nt(do_ref[rows, :], v_ref[0:cols, :])
                ds = (p * dp_less_delta).astype(BF16)
                dv_ref[0:cols, :] += _dot_tn(p.astype(BF16), do_ref[rows, 0:HK])
                dk_ref[0:cols, :] += _dot_tn(ds, qv)
                dq_rows = pl.ds(pl.multiple_of(i * tq + r * sub, sub), sub)
                dq_ref[dq_rows, :] += _dot(ds, kv_)

        @pl.when(j < i)
        def _():
            step(False)

        @pl.when(j == i)
        def _():
            step(True)

    grid_spec = pltpu.PrefetchScalarGridSpec(
        num_scalar_prefetch=2, grid=(HEADS, npairs),
        in_specs=[pl.BlockSpec((None, tq, QKP), lambda h, n, qi, kj: (h, qi[n], 0)),
                  pl.BlockSpec((None, tq, QKP), lambda h, n, qi, kj: (h, kj[n], 0)),
                  pl.BlockSpec((None, tq, QKP), lambda h, n, qi, kj: (h, kj[n], 0)),
                  pl.BlockSpec((None, tq, HK), lambda h, n, qi, kj: (h, qi[n], 0)),
                  pl.BlockSpec((None, tq, QKP), lambda h, n, qi, kj: (h, qi[n], 0))],
        out_specs=[pl.BlockSpec((None, t, QKP), lambda h, n, qi, kj: (h, 0, 0)),
                   pl.BlockSpec((None, tq, QKP), lambda h, n, qi, kj: (h, kj[n], 0)),
                   pl.BlockSpec((None, tq, HK), lambda h, n, qi, kj: (h, kj[n], 0))],
    )
    return _call(
        body, name="flash_bwd", grid_spec=grid_spec,
        out_shape=[jax.ShapeDtypeStruct((HEADS, t, QKP), F32), jax.ShapeDtypeStruct((HEADS, t, QKP), F32),
                   jax.ShapeDtypeStruct((HEADS, t, HK), F32)],
        compiler_params=_cp(56),
    )(qi, kj, q, k, v, lse, do)


def _adamw(name, w, g, m, v):
    r, c = w.shape
    tr = r if r <= 256 else next(k for k in (256, 352, 384) if r % k == 0)

    def body(w_ref, g_ref, m_ref, v_ref, d_ref, nm_ref, nv_ref):
        gv = g_ref[...]
        nm = ADAM_B1 * m_ref[...] + (1.0 - ADAM_B1) * gv
        nv = ADAM_B2 * v_ref[...] + (1.0 - ADAM_B2) * (gv * gv)
        m_hat = nm / (1.0 - ADAM_B1 ** ADAM_STEP)
        v_hat = nv / (1.0 - ADAM_B2 ** ADAM_STEP)
        d_ref[...] = -ADAM_LR * (m_hat / (jnp.sqrt(v_hat) + ADAM_EPS) + ADAM_WD * w_ref[...])
        nm_ref[...] = nm
        nv_ref[...] = nv

    blk = pl.BlockSpec((tr, c), lambda i: (i, 0))
    return _call(
        body, name=name, grid=(r // tr,),
        in_specs=[blk] * 4, out_specs=[blk] * 3,
        out_shape=[jax.ShapeDtypeStruct((r, c), F32)] * 3,
        compiler_params=_cp(),
    )(w, g, m, v)


def _place():
    return lax.axis_index("x"), lax.axis_index("y"), lax.axis_index("c")


def _other_chips(x, y):
    return [(1 - x, y), (x, 1 - y), (1 - x, 1 - y)]


class _Exchange:
    inputs = ()
    out_shapes = ()
    scratch = ()

    def start(self, *refs):
        raise NotImplementedError

    def finish(self, *refs):
        raise NotImplementedError

    def alone(self, name):
        def body(*refs):
            self.start(*refs)
            self.finish(*refs)

        anywhere = pl.BlockSpec(memory_space=pl.ANY)
        return _call(
            body, name=name,
            in_specs=[anywhere] * len(self.inputs), out_specs=[anywhere] * len(self.out_shapes),
            out_shape=list(self.out_shapes), scratch_shapes=list(self.scratch),
        )(*self.inputs)


class _GatherWeights(_Exchange):
    def __init__(self, shard):
        self.r = shard.shape[0]
        self.inputs = (shard,)
        self.out_shapes = (jax.ShapeDtypeStruct((4, self.r, PACK_W), shard.dtype),)
        self.scratch = (pltpu.SemaphoreType.DMA((6,)), pltpu.SemaphoreType.DMA((6,)))

    def gathered(self, got, k):
        return lax.dynamic_update_slice(got, self.inputs[0][None], (k, 0, 0))

    def _copies(self, s_ref, g_ref, send_sems, recv_sems):
        half = self.r // 2
        x, y, c = _place()
        chips = _other_chips(x, y)

        def rows(px, py, pc):
            return g_ref.at[2 * px + py, pl.ds(pc * half, half), :]

        def copy(k, block, to, src=None):
            return pltpu.make_async_remote_copy(
                src_ref=rows(*block) if src is None else src, dst_ref=rows(*block),
                send_sem=send_sems.at[k], recv_sem=recv_sems.at[k], device_id=to, device_id_type=MESH)

        first = [copy(j, (x, y, c), (*chip, c), src=s_ref.at[pl.ds(c * half, half), :]) for j, chip in enumerate(chips)]
        passed = [copy(3 + j, (*chip, c), (x, y, 1 - c)) for j, chip in enumerate(chips)]
        landed = [copy(j, (*chip, c), (x, y, c)) for j, chip in enumerate(chips)]
        landed += [copy(3 + j, (*chip, 1 - c), (x, y, c)) for j, chip in enumerate(chips)]
        return first, passed, landed

    def start(self, *refs):
        first, _, _ = self._copies(*refs)
        for cp in first:
            cp.start()

    def finish(self, *refs):
        first, passed, landed = self._copies(*refs)
        for j in range(3):
            landed[j].wait_recv()
            passed[j].start()
        for j in range(3):
            landed[3 + j].wait_recv()
        for cp in first + passed:
            cp.wait_send()


def _swap_halves(name, gp):
    r = gp.shape[1]
    half = r // 2

    def body(g_ref, o_ref, send_sem, recv_sem):
        x, y, c = _place()
        cp = pltpu.make_async_remote_copy(
            src_ref=g_ref.at[:, pl.ds((1 - c) * half, half), :], dst_ref=o_ref,
            send_sem=send_sem, recv_sem=recv_sem, device_id=(x, y, 1 - c), device_id_type=MESH)
        cp.start()
        cp.wait()

    return _call(
        body, name=name,
        in_specs=[pl.BlockSpec(memory_space=pl.ANY)],
        out_specs=pl.BlockSpec(memory_space=pl.ANY),
        out_shape=jax.ShapeDtypeStruct((4, half, PACK_W), gp.dtype),
        scratch_shapes=[pltpu.SemaphoreType.DMA, pltpu.SemaphoreType.DMA],
    )(gp)


def _chip_sum(name, gp, got, c_arr):
    half = got.shape[1]
    tr = ADD_ROWS
    nb = half // tr

    def body(c_ref, a_ref, b_ref, o_ref, ob_ref):
        s = a_ref[...] + b_ref[...]
        o_ref[...] = s
        ob_ref[...] = s.astype(BF16)

    grid_spec = pltpu.PrefetchScalarGridSpec(
        num_scalar_prefetch=1, grid=(4, nb),
        in_specs=[pl.BlockSpec((None, tr, PACK_W), lambda s, i, c: (s, c[0] * nb + i, 0)),
                  pl.BlockSpec((None, tr, PACK_W), lambda s, i, c: (s, i, 0))],
        out_specs=[pl.BlockSpec((None, tr, PACK_W), lambda s, i, c: (s, i, 0)),
                   pl.BlockSpec((None, tr, PACK_W), lambda s, i, c: (s, i, 0))],
    )
    return _call(
        body, name=name, grid_spec=grid_spec,
        out_shape=[jax.ShapeDtypeStruct(got.shape, F32), jax.ShapeDtypeStruct(got.shape, BF16)],
        compiler_params=_cp(),
    )(c_arr, gp, got)


class _ScatterChipSums(_Exchange):
    def __init__(self, cs):
        self.inputs = (cs,)
        self.out_shapes = (jax.ShapeDtypeStruct((3,) + cs.shape[1:], cs.dtype),)
        self.scratch = (pltpu.SemaphoreType.DMA((3,)), pltpu.SemaphoreType.DMA((3,)))

    def _copies(self, s_ref, o_ref, send_sems, recv_sems):
        x, y, c = _place()
        return [pltpu.make_async_remote_copy(
            src_ref=s_ref.at[2 * px + py], dst_ref=o_ref.at[j],
            send_sem=send_sems.at[j], recv_sem=recv_sems.at[j], device_id=(px, py, c), device_id_type=MESH)
            for j, (px, py) in enumerate(_other_chips(x, y))]

    def start(self, *refs):
        for cp in self._copies(*refs):
            cp.start()

    def finish(self, *refs):
        for cp in self._copies(*refs):
            cp.wait()


def _shard_sum(name, cs, got, kc_arr):
    h = cs.shape[1]
    tr = ADD_ROWS
    nb = h // tr

    def body(k_ref, a_ref, b_ref, o_ref):
        o_ref[...] = ((a_ref[...] + b_ref[0].astype(F32)) + b_ref[1].astype(F32)) + b_ref[2].astype(F32)

    grid_spec = pltpu.PrefetchScalarGridSpec(
        num_scalar_prefetch=1, grid=(nb,),
        in_specs=[pl.BlockSpec((None, tr, PACK_W), lambda i, k: (k[0], i, 0)),
                  pl.BlockSpec((3, tr, PACK_W), lambda i, k: (0, i, 0))],
        out_specs=pl.BlockSpec((tr, PACK_W), lambda i, k: (k[1] * nb + i, 0)),
    )
    return _call(
        body, name=name, grid_spec=grid_spec,
        out_shape=jax.ShapeDtypeStruct((2 * h, PACK_W), F32),
        compiler_params=_cp(),
    )(kc_arr, cs, got)


def _join_halves(name, both):
    h = both.shape[0] // 2

    def body(m_ref, o_ref, send_sem, recv_sem):
        x, y, c = _place()
        cp = pltpu.make_async_remote_copy(
            src_ref=m_ref.at[pl.ds(c * h, h), :], dst_ref=o_ref.at[pl.ds(c * h, h), :],
            send_sem=send_sem, recv_sem=recv_sem, device_id=(x, y, 1 - c), device_id_type=MESH)
        cp.start()
        cp.wait_send()
        pltpu.make_async_remote_copy(
            src_ref=m_ref.at[pl.ds(c * h, h), :], dst_ref=o_ref.at[pl.ds((1 - c) * h, h), :],
            send_sem=send_sem, recv_sem=recv_sem, device_id=(x, y, 1 - c), device_id_type=MESH).wait_recv()

    return _call(
        body, name=name,
        in_specs=[pl.BlockSpec(memory_space=pl.ANY)],
        out_specs=pl.BlockSpec(memory_space=pl.ANY),
        out_shape=jax.ShapeDtypeStruct(both.shape, both.dtype),
        input_output_aliases={0: 0},
        scratch_shapes=[pltpu.SemaphoreType.DMA, pltpu.SemaphoreType.DMA],
    )(both)


def _all_reduce_small(v):
    r = v.shape[0]

    def body(v_ref, o_ref, buf, send_sems, recv_sems):
        x, y, c = _place()
        me = 4 * x + 2 * y + c
        buf[me] = v_ref[...]
        cps = []
        for k in range(1, 8):
            peer = (x ^ (k >> 2), y ^ ((k >> 1) & 1), c ^ (k & 1))
            cps.append(pltpu.make_async_remote_copy(
                src_ref=v_ref, dst_ref=buf.at[me],
                send_sem=send_sems.at[k - 1], recv_sem=recv_sems.at[k - 1], device_id=peer, device_id_type=MESH))
        for cp in cps:
            cp.start()
        for k in range(1, 8):
            pltpu.make_async_remote_copy(
                src_ref=v_ref, dst_ref=buf.at[me ^ k],
                send_sem=send_sems.at[k - 1], recv_sem=recv_sems.at[k - 1],
                device_id=(x, y, c), device_id_type=MESH).wait_recv()
        for cp in cps:
            cp.wait_send()
        acc = buf[0]
        for k in range(1, 8):
            acc = acc + buf[k]
        o_ref[...] = acc

    return _call(
        body, name="all_reduce_small",
        in_specs=[pl.BlockSpec(memory_space=pltpu.VMEM)],
        out_specs=pl.BlockSpec(memory_space=pltpu.VMEM),
        out_shape=jax.ShapeDtypeStruct((r, 128), F32),
        scratch_shapes=[pltpu.VMEM((8, r, 128), F32), pltpu.SemaphoreType.DMA((7,)), pltpu.SemaphoreType.DMA((7,))],
    )(v)


def _group(names):
    return tuple(e for e in BIG if e[0] in names)


def _pack(shards, dtype):
    return jnp.concatenate([s.astype(dtype).reshape(-1, PACK_W) for s in shards], axis=0)


def _unpack_full(g, group):
    out, at = {}, 0
    for name, rows, cols, axis in group:
        n = rows * cols // 4 // PACK_W
        blk = g[:, at:at + n, :]
        at += n
        if axis == 1:
            out[name] = blk.reshape(4, rows, cols // 4).transpose(1, 0, 2).reshape(rows, cols)
        else:
            out[name] = blk.reshape(rows, cols)
    return out


def _pack_grads(grads, group):
    parts = []
    for name, rows, cols, axis in group:
        g = grads[name]
        if axis == 1:
            g = g.reshape(rows, 4, cols // 4).transpose(1, 0, 2)
        parts.append(g.reshape(4, -1, PACK_W))
    rows_total = sum(p.shape[1] for p in parts)
    pad = -rows_total % PACK_ALIGN
    if pad:
        parts.append(jnp.zeros((4, pad, PACK_W), F32))
    return jnp.concatenate(parts, axis=1)


def _unpack_shard(s, group):
    out, at = {}, 0
    for name, rows, cols, axis in group:
        n = rows * cols // 4 // PACK_W
        shape = (rows, cols // 4) if axis == 1 else (rows // 4, cols)
        out[name] = s[at:at + n, :].reshape(shape)
        at += n
    return out


def _pack_small(parts):
    flat = jnp.concatenate([p.reshape(-1) for p in parts])
    pad = -flat.shape[0] % 1024
    return jnp.concatenate([flat, jnp.zeros((pad,), F32)]).reshape(-1, 128)


def _ffn_fwd(tag, h, gain, w_in, w_out, side=None):
    t = h.shape[0]
    tm = min(TM, t)
    n = _rms_fwd(tag + "_norm", h, gain)

    def compute(rows, weights, outs):
        a, w_ref = rows[0][...], weights[0]
        for j in range(DFF // FFN_CHUNK):
            cols = slice(j * FFN_CHUNK, (j + 1) * FFN_CHUNK)
            gate = _dot(a, w_ref[:, cols])
            up = _dot(a, w_ref[:, DFF + j * FFN_CHUNK:DFF + (j + 1) * FFN_CHUNK])
            outs[0][:, cols] = gate.astype(BF16)
            outs[1][:, cols] = up.astype(BF16)
            outs[2][:, cols] = (_silu(gate) * up).astype(BF16)

    gate, up, act, *side_out = _rows_call(tag + "_in", [n], [w_in], [(DFF, BF16)] * 3, compute, min(FFN_TM, t), side=side)
    (out,) = _mm(tag + "_out", [_a_spec(act, tm)], [_b_nn(w_out, 512)], [(0, 0)],
                 lambda accs, ex: (ex[0] + 0.5 * accs[0],), [_e_tile(h, tm, 512)], [F32], t, D, tm, 512)
    return out, (n, gate, up, act), side_out


class _Reduction:
    def __init__(self, tag, group, c_arr, k_arr):
        self.tag, self.group, self.c_arr, self.k_arr = tag, group, c_arr, k_arr

    def begin(self, grads):
        gp = _pack_grads(grads, self.group)
        self.sums, sums_bf16 = _chip_sum("grad_chip_sum_" + self.tag, gp, _swap_halves("grad_swap_" + self.tag, gp), self.c_arr)
        return _ScatterChipSums(sums_bf16)

    def end(self, got):
        mine = _shard_sum("grad_shard_sum_" + self.tag, self.sums, got, self.k_arr)
        return _unpack_shard(_join_halves("grad_join_" + self.tag, mine), self.group)


def _ffn_bwd(tag, h, gain, w_in, w_out, saved, dout, side, reduction):
    t = h.shape[0]
    tm = min(TM, t)
    n, gate, up, act = saved

    def compute(rows, weights, outs):
        d = rows[0][...].astype(BF16)
        for j in range(DFF // FFN_CHUNK):
            cols = slice(j * FFN_CHUNK, (j + 1) * FFN_CHUNK)
            da = 0.5 * _dot_nt(d, weights[0][cols, :])
            g, u = rows[1][:, cols].astype(F32), rows[2][:, cols].astype(F32)
            s = _sig(g)
            silu = g * s
            outs[0][:, cols] = (da * u * (s + silu * (1.0 - s))).astype(BF16)
            outs[1][:, cols] = (da * silu).astype(BF16)

    dgate, dup, *side_out = _rows_call(tag + "_dact", [dout, gate, up], [w_out], [(DFF, BF16)] * 2, compute,
                                       min(FFN_TM, t), side=side)
    dw_out = _mm_tn(tag + "_dw_out", act, dout, scale=0.5, tm=DFF // 2, tn=D)
    dw_g = _mm_tn(tag + "_dw_gate", n, dgate, tm=D, tn=DFF // 2)
    dw_u = _mm_tn(tag + "_dw_up", n, dup, tm=D, tn=DFF // 2)
    sending = reduction.begin({tag + "_w_in": jnp.concatenate([dw_g, dw_u], axis=1), tag + "_w_out": dw_out})
    dn, got = _mm(tag + "_dn", [_a_spec(dgate, tm), _a_spec(dup, tm)],
                  [_b_nt(w_in, 512, DFF, 0), _b_nt(w_in, 512, DFF, 1)], [(0, 0), (1, 1)],
                  lambda accs, ex: (accs[0] + accs[1],), [], [F32], t, D, tm, 512, trans_b=True, side=sending)
    dh, dgain = _rms_bwd(tag + "_dnorm", h, gain, dn, dout)
    return dh, dgain, side_out, got


def kernel(x, positions, ffn1_norm, ffn1_w_in, ffn1_w_out, mix_norm, w_in, hg_lb_table, hg_out_norm, w_hg_branch, mla_q_lora_norm, w_q_up, mla_kv_lora_norm, w_kv_up, q_head_norm, k_head_norm, w_mla_branch, w_merge, b_merge, w_out, ffn2_norm, ffn2_w_in, ffn2_w_out, final_norm, loss_target, m_ffn1_norm, m_ffn1_w_in, m_ffn1_w_out, m_mix_norm, m_w_in, m_hg_lb_table, m_hg_out_norm, m_w_hg_branch, m_mla_q_lora_norm, m_w_q_up, m_mla_kv_lora_norm, m_w_kv_up, m_q_head_norm, m_k_head_norm, m_w_mla_branch, m_w_merge, m_b_merge, m_w_out, m_ffn2_norm, m_ffn2_w_in, m_ffn2_w_out, m_final_norm, v_ffn1_norm, v_ffn1_w_in, v_ffn1_w_out, v_mix_norm, v_w_in, v_hg_lb_table, v_hg_out_norm, v_w_hg_branch, v_mla_q_lora_norm, v_w_q_up, v_mla_kv_lora_norm, v_w_kv_up, v_q_head_norm, v_k_head_norm, v_w_mla_branch, v_w_merge, v_b_merge, v_w_out, v_ffn2_norm, v_ffn2_w_in, v_ffn2_w_out, v_final_norm):
    a = dict(locals())
    w = {n: a[n] for n in WEIGHT_ORDER}
    mom = {n: a["m_" + n] for n in WEIGHT_ORDER}
    var = {n: a["v_" + n] for n in WEIGHT_ORDER}
    t = x.shape[1]
    tm = min(TM, t)
    xt = x.reshape(t, D)
    target = loss_target.reshape(t, D)
    pos = positions.reshape(t, 1)
    x_i, y_i, c_i = _place()
    k_idx = (2 * x_i + y_i).astype(jnp.int32)
    c_arr = c_i.astype(jnp.int32).reshape(1)
    k_arr = jnp.stack([k_idx, c_i.astype(jnp.int32)])

    group_first = _group(("ffn1_w_in", "ffn1_w_out"))
    group_mid = _group(("w_in", "w_hg_branch", "w_q_up", "w_kv_up", "w_mla_branch", "w_merge", "w_out"))
    group_last = _group(("ffn2_w_in", "ffn2_w_out"))
    gather_first = _GatherWeights(_pack([w[e[0]][0] for e in group_first], BF16))
    gather_mid = _GatherWeights(_pack([w[e[0]][0] for e in group_mid], BF16))
    gather_last = _GatherWeights(_pack([w[e[0]][0] for e in group_last], BF16))
    (got,) = gather_first.alone("gather_first")
    full = _unpack_full(gather_first.gathered(got, k_idx), group_first)
    h1, ffn1_saved, (got,) = _ffn_fwd(
        "ffn1", xt, w["ffn1_norm"], full["ffn1_w_in"], full["ffn1_w_out"], side=gather_mid)
    full.update(_unpack_full(gather_mid.gathered(got, k_idx), group_mid))
    w_in_full = full["w_in"]
    w_in_hg = w_in_full[:, :4 * D]
    w_in_mla = jnp.pad(w_in_full[:, 4 * D:], ((0, 0), (0, MLA_COLS - (4800 - 4 * D))))
    w_q_pad = jnp.pad(full["w_q_up"].reshape(Q_LORA, HEADS, QK), ((0, 0), (0, 0), (0, QKP - QK))).reshape(Q_LORA, HEADS * QKP)
    w_kv = full["w_kv_up"]
    gq = jnp.pad(w["q_head_norm"], ((0, 0), (0, QKP - QK)))
    gk = jnp.pad(w["k_head_norm"], ((0, 0), (0, QKP - QK)))

    u = _rms_fwd("mix_norm", h1, w["mix_norm"])
    ident = lambda accs, ex: (accs[0],)
    p_hg, got = _mm("in_hg", [_a_spec(u, tm)], [_b_nn(w_in_hg, 512)], [(0, 0)], ident, [], [F32], t, 4 * D, tm, 512,
                    side=gather_last)
    full.update(_unpack_full(gather_last.gathered(got, k_idx), group_last))
    (p_mla,) = _mm("in_mla", [_a_spec(u, tm)], [_b_nn(w_in_mla, MLA_COLS)], [(0, 0)], ident, [], [F32], t, MLA_COLS, tm, MLA_COLS)
    o_raw, hg_o, states = _hgrn_fwd(p_hg, w["hg_lb_table"], w["hg_out_norm"])
    (y_hg,) = _mm("hg_branch", [_a_spec(hg_o, tm)], [_b_nn(full["w_hg_branch"], 512)], [(0, 0)], ident, [], [BF16], t, D, tm, 512)
    cqn, ckvn = _lora_norm_fwd(p_mla, w["mla_q_lora_norm"], w["mla_kv_lora_norm"])
    (qf,) = _mm("q_up", [_a_spec(cqn, tm)], [_b_nn(w_q_pad, 512)], [(0, 0)], ident, [], [F32], t, HEADS * QKP, tm, 512)
    (kvf,) = _mm("kv_up", [_a_spec(ckvn, tm)], [_b_nn(w_kv, 512)], [(0, 0)], ident, [], [F32], t, HEADS * QKP, tm, 512)
    cos, sin = _rope_tables(pos)
    qh, kh, vh = _mla_prep_fwd(qf, kvf, p_mla, cos, sin, gq, gk)
    o_mla, lse = _flash_fwd(qh, kh, vh)
    (y_mla,) = _mm("mla_branch", [_a_spec(o_mla, tm)], [_b_nn(full["w_mla_branch"], 512)], [(0, 0)], ident, [], [BF16], t, D, tm, 512)

    def merge_epi(accs, ex):
        g_hg = _sig(accs[0] + ex[2])
        g_mla = _sig(accs[1] + ex[3])
        return g_hg * ex[0].astype(F32) + g_mla * ex[1].astype(F32), g_hg, g_mla

    w_merge_f = full["w_merge"]
    mix, g_hg, g_mla = _mm(
        "merge", [_a_spec(u, tm)], [_b_nn(w_merge_f, 512), _b_nn(w_merge_f, 512, D // 512)], [(0, 0), (0, 1)], merge_epi,
        [_e_tile(y_hg, tm, 512), _e_tile(y_mla, tm, 512), _e_row(w["b_merge"], 512), _e_row(w["b_merge"], 512, D // 512)],
        [BF16, BF16, BF16], t, D, tm, 512)
    (h2,) = _mm("out_proj", [_a_spec(mix, tm)], [_b_nn(full["w_out"], 512)], [(0, 0)],
                lambda accs, ex: (ex[0] + accs[0],), [_e_tile(h1, tm, 512)], [F32], t, D, tm, 512)
    h3, ffn2_saved, _ = _ffn_fwd("ffn2", h2, w["ffn2_norm"], full["ffn2_w_in"], full["ffn2_w_out"])
    dh3, d_final_norm, loss_part = _final_loss(h3, target, w["final_norm"])

    grads, small = {}, {}
    small["final_norm"] = d_final_norm
    reduce_last = _Reduction("last", group_last, c_arr, k_arr)
    reduce_mid = _Reduction("mid", group_mid, c_arr, k_arr)
    reduce_first = _Reduction("first", group_first, c_arr, k_arr)
    dh2, small["ffn2_norm"], _, got_last = _ffn_bwd(
        "ffn2", h2, w["ffn2_norm"], full["ffn2_w_in"], full["ffn2_w_out"], ffn2_saved, dh3, None, reduce_last)

    def dmix_epi(accs, ex):
        dm = accs[0]
        ghg, gml, yhg, yml = [e.astype(F32) for e in ex]
        return dm * ghg, dm * gml, dm * yhg * ghg * (1.0 - ghg), dm * yml * gml * (1.0 - gml)

    dy_hg, dy_mla, dpre_hg, dpre_mla = _mm(
        "d_mix", [_a_spec(dh2, tm)], [_b_nt(full["w_out"], 512)], [(0, 0)], dmix_epi,
        [_e_tile(g_hg, tm, 512), _e_tile(g_mla, tm, 512), _e_tile(y_hg, tm, 512), _e_tile(y_mla, tm, 512)],
        [BF16, BF16, BF16, BF16], t, D, tm, 512, trans_b=True)
    grads["w_out"] = _mm_tn("dw_out", mix, dh2)
    small["b_merge"] = jnp.concatenate([_colsum("db_hg", dpre_hg), _colsum("db_mla", dpre_mla)], axis=1)
    grads["w_merge"] = jnp.concatenate([_mm_tn("dw_merge_hg", u, dpre_hg), _mm_tn("dw_merge_mla", u, dpre_mla)], axis=1)
    grads["w_hg_branch"] = _mm_tn("dw_hg_branch", hg_o, dy_hg)
    grads["w_mla_branch"] = _mm_tn("dw_mla_branch", o_mla, dy_mla)
    (dho,) = _mm("d_hg_o", [_a_spec(dy_hg, tm)], [_b_nt(full["w_hg_branch"], 512)], [(0, 0)], ident, [], [BF16], t, D, tm, 512, trans_b=True)
    (do_mla,) = _mm("d_o_mla", [_a_spec(dy_mla, tm)], [_b_nt(full["w_mla_branch"], 512)], [(0, 0)], ident, [], [BF16], t, D, tm, 512, trans_b=True)

    dq_raw, df_raw, di_raw, dg_raw, small["hg_lb_table"], small["hg_out_norm"] = _hgrn_bwd(
        p_hg, w["hg_lb_table"], w["hg_out_norm"], o_raw, states, dho)
    dp_hg = [dq_raw, df_raw, di_raw, dg_raw]

    dqh, dkh, dvh = _flash_bwd(qh, kh, vh, lse, _attn_do(do_mla, o_mla))
    dqf, dkvf, dkpe, dgq, dgk = _mla_prep_bwd(qf, kvf, p_mla, cos, sin, gq, gk, dqh, dkh, dvh)
    small["q_head_norm"] = dgq[:, :QK]
    small["k_head_norm"] = dgk[:, :QK]
    dwq_pad = _mm_tn("dw_q_up", cqn, dqf, tm=Q_LORA, tn=1024)
    grads["w_q_up"] = dwq_pad.reshape(Q_LORA, HEADS, QKP)[:, :, :QK].reshape(Q_LORA, HEADS * QK)
    grads["w_kv_up"] = _mm_tn("dw_kv_up", ckvn, dkvf, tm=KV_LORA, tn=1024)
    (dcqn,) = _mm("d_cq", [_a_spec(dqf, tm)], [_b_nt(w_q_pad, Q_LORA)], [(0, 0)], ident, [], [F32], t, Q_LORA, tm, Q_LORA, trans_b=True)
    (dckvn,) = _mm("d_ckv", [_a_spec(dkvf, tm)], [_b_nt(w_kv, KV_LORA)], [(0, 0)], ident, [], [F32], t, KV_LORA, tm, KV_LORA, trans_b=True)
    dp_mla, small["mla_q_lora_norm"], small["mla_kv_lora_norm"] = _lora_norm_bwd(
        p_mla, w["mla_q_lora_norm"], w["mla_kv_lora_norm"], dcqn, dckvn, dkpe)

    dw_in_hg = [_mm_tn("dw_in_hg%d" % k, u, dp_hg[k]) for k in range(4)]
    dw_in_mla = _mm_tn("dw_in_mla", u, dp_mla, tn=MLA_COLS)
    grads["w_in"] = jnp.concatenate(dw_in_hg + [dw_in_mla[:, :4800 - 4 * D]], axis=1)
    tm_du = min(TM // 2, t)
    du, got_mid = _mm(
        "d_u",
        [_a_spec(dpre_hg, tm_du), _a_spec(dpre_mla, tm_du)] + [_a_spec(d, tm_du) for d in dp_hg] + [_a_spec(dp_mla, tm_du)],
        [_b_nt(w_merge_f, 512, D, 0), _b_nt(w_merge_f, 512, D, 1)]
        + [_b_nt(w_in_hg, 512, D, k) for k in range(4)] + [_b_nt(w_in_mla, 512)],
        [(k, k) for k in range(7)],
        lambda accs, ex: (functools.reduce(lambda p, q: p + q, accs),), [], [F32], t, D, tm_du, 512, trans_b=True,
        side=reduce_mid.begin(grads))
    dh1, small["mix_norm"] = _rms_bwd("mix_dnorm", h1, w["mix_norm"], du, dh2)
    dx, small["ffn1_norm"], _, got_first = _ffn_bwd(
        "ffn1", xt, w["ffn1_norm"], full["ffn1_w_in"], full["ffn1_w_out"], ffn1_saved, dh1, None, reduce_first)

    g_shard = {**reduce_last.end(got_last), **reduce_mid.end(got_mid), **reduce_first.end(got_first)}
    small_sum = _all_reduce_small(_pack_small([small[n] for n, _ in SMALL] + [loss_part])).reshape(-1)
    g_small, at = {}, 0
    for n, shape in SMALL:
        size = shape[0] * shape[1]
        g_small[n] = small_sum[at:at + size].reshape(shape)
        at += size
    loss = small_sum[at]

    g_out, d_out, m_out, v_out = {}, {}, {}, {}
    for n in WEIGHT_ORDER:
        shape = w[n].shape
        g = g_shard[n] if n in g_shard else g_small[n]
        two = g.shape
        d_, m_, v_ = _adamw("adamw_" + n, w[n].reshape(two), g, mom[n].reshape(two), var[n].reshape(two))
        g_out[n], d_out[n], m_out[n], v_out[n] = g.reshape(shape), d_.reshape(shape), m_.reshape(shape), v_.reshape(shape)

    return (loss, dx.reshape(x.shape), *[g_out[n] for n in WEIGHT_ORDER], *[d_out[n] for n in WEIGHT_ORDER],
            *[m_out[n] for n in WEIGHT_ORDER], *[v_out[n] for n in WEIGHT_ORDER])
```

```python
import functools

import numpy as np
import jax
import jax.numpy as jnp
from jax import lax
from jax.experimental import pallas as pl
from jax.experimental.pallas import tpu as pltpu

F32 = jnp.float32
BF16 = jnp.bfloat16
MESH = pl.DeviceIdType.MESH

D = 1024
DFF = 2816
HEADS = 8
HK = 128
CHUNK = 64
ROPE = 64
QK = 192
QKP = 256
Q_LORA = 384
KV_LORA = 256
MLA_COLS = 768
EPS = 1e-6
ROPE_THETA = 10000.0
SCALE = QK ** -0.5
LOG2E = 1.4426950408889634
LN2 = 0.6931471805599453
NEG = -1e30
EXP_CLAMP = 80.0

ADAM_LR = 0.001
ADAM_B1 = 0.9
ADAM_B2 = 0.999
ADAM_EPS = 1e-08
ADAM_WD = 0.01
ADAM_STEP = 10

PACK_W = 1024
ADD_ROWS = 352
PACK_ALIGN = 2 * ADD_ROWS

TM = 1024
FFN_TM = 512
FFN_CHUNK = 256
TQ = 1024
SUBQ = 256
HG_BT = 512
HG_HPB = 4
TT = 1024
ROW_TM = 256

VMEM_MB = 48

BIG = (
    ("ffn1_w_in", D, 2 * DFF, 1),
    ("ffn1_w_out", DFF, D, 0),
    ("w_in", D, 4800, 1),
    ("w_hg_branch", D, D, 0),
    ("w_q_up", Q_LORA, HEADS * QK, 1),
    ("w_kv_up", KV_LORA, HEADS * 2 * HK, 1),
    ("w_mla_branch", D, D, 0),
    ("w_merge", D, 2 * D, 1),
    ("w_out", D, D, 0),
    ("ffn2_w_in", D, 2 * DFF, 1),
    ("ffn2_w_out", DFF, D, 0),
)
SMALL = (
    ("ffn1_norm", (1, D)),
    ("mix_norm", (1, D)),
    ("hg_lb_table", (2, D)),
    ("hg_out_norm", (1, HK)),
    ("mla_q_lora_norm", (1, Q_LORA)),
    ("mla_kv_lora_norm", (1, KV_LORA)),
    ("q_head_norm", (1, QK)),
    ("k_head_norm", (1, QK)),
    ("b_merge", (1, 2 * D)),
    ("ffn2_norm", (1, D)),
    ("final_norm", (1, D)),
)
WEIGHT_ORDER = ("ffn1_norm", "ffn1_w_in", "ffn1_w_out", "mix_norm", "w_in", "hg_lb_table", "hg_out_norm",
                "w_hg_branch", "mla_q_lora_norm", "w_q_up", "mla_kv_lora_norm", "w_kv_up", "q_head_norm",
                "k_head_norm", "w_mla_branch", "w_merge", "b_merge", "w_out", "ffn2_norm", "ffn2_w_in",
                "ffn2_w_out", "final_norm")


def _call(body, **kw):
    return pl.pallas_call(body, **kw)


def _cp(vmem_mb=VMEM_MB):
    return pltpu.CompilerParams(vmem_limit_bytes=vmem_mb << 20)


def _dot(a, b):
    return lax.dot_general(a, b, (((1,), (0,)), ((), ())), preferred_element_type=F32)


def _dot_nt(a, b):
    return lax.dot_general(a, b, (((1,), (1,)), ((), ())), preferred_element_type=F32)


def _dot_tn(a, b):
    return lax.dot_general(a, b, (((0,), (0,)), ((), ())), preferred_element_type=F32)


def _sig(x):
    return jax.nn.sigmoid(x)


def _silu(x):
    return x * _sig(x)


def _dsilu(x):
    s = _sig(x)
    return s * (1.0 + x * (1.0 - s))


def _a_spec(arr, tm, kblk=None, kidx=0):
    kb = arr.shape[1] if kblk is None else kblk
    return arr, pl.BlockSpec((tm, kb), lambda i, j, kidx=kidx: (i, kidx))


def _b_nn(arr, tn, off=0):
    return arr, pl.BlockSpec((arr.shape[0], tn), lambda i, j, off=off: (0, j + off))


def _b_nt(arr, tn, kblk=None, kidx=0):
    kb = arr.shape[1] if kblk is None else kblk
    return arr, pl.BlockSpec((tn, kb), lambda i, j, kidx=kidx: (j, kidx))


def _e_tile(arr, tm, tn, off=0):
    return arr, pl.BlockSpec((tm, tn), lambda i, j, off=off: (i, j + off))


def _e_row(arr, tn, off=0):
    return arr, pl.BlockSpec((1, tn), lambda i, j, off=off: (0, j + off))


def _mm(name, As, Bs, dots, epi, extras, out_dtypes, m, n, tm, tn, trans_b=False, side=None):
    na, nb, ne, no = len(As), len(Bs), len(extras), len(out_dtypes)
    ni, nj = m // tm, n // tn
    s_in = len(side.inputs) if side else 0
    s_out = len(side.out_shapes) if side else 0

    def body(*refs):
        a_refs = refs[:na]
        b_refs = refs[na:na + nb]
        e_refs = refs[na + nb:na + nb + ne]
        at = na + nb + ne
        side_refs = refs[at:at + s_in]
        o_refs = refs[at + s_in:at + s_in + no]
        side_refs = list(side_refs) + list(refs[at + s_in + no:])
        if side:
            i, j = pl.program_id(0), pl.program_id(1)

            @pl.when(jnp.logical_and(i == 0, j == 0))
            def _():
                side.start(*side_refs)

        a_vals = [r[...].astype(BF16) for r in a_refs]
        accs = []
        for ai, bi in dots:
            b = b_refs[bi][...]
            accs.append(_dot_nt(a_vals[ai], b) if trans_b else _dot(a_vals[ai], b))
        outs = epi(accs, [r[...] for r in e_refs])
        for o_ref, o in zip(o_refs, outs):
            o_ref[...] = o.astype(o_ref.dtype)
        if side:
            @pl.when(jnp.logical_and(i == ni - 1, j == nj - 1))
            def _():
                side.finish(*side_refs)

    ops = list(As) + list(Bs) + list(extras)
    anywhere = pl.BlockSpec(memory_space=pl.ANY)
    res = _call(
        body, name=name,
        grid=(ni, nj),
        in_specs=[s for _, s in ops] + [anywhere] * s_in,
        out_specs=[pl.BlockSpec((tm, tn), lambda i, j: (i, j)) for _ in out_dtypes] + [anywhere] * s_out,
        out_shape=[jax.ShapeDtypeStruct((m, n), dt) for dt in out_dtypes] + (list(side.out_shapes) if side else []),
        scratch_shapes=list(side.scratch) if side else [],
        compiler_params=_cp(),
    )(*[a for a, _ in ops], *(side.inputs if side else []))
    return res


def _rows_call(name, rows, weights, outs, compute, tm, side=None, sums=(), vmem_mb=VMEM_MB):
    t = rows[0].shape[0]
    nr, nw, no = len(rows), len(weights), len(outs) + len(sums)
    ni = t // tm
    s_in = len(side.inputs) if side else 0
    s_out = len(side.out_shapes) if side else 0

    def body(*refs):
        at = nr + nw
        side_refs = list(refs[at:at + s_in]) + list(refs[at + s_in + no:])
        if side:
            @pl.when(pl.program_id(0) == 0)
            def _():
                side.start(*side_refs)

        out_refs = refs[at + s_in:at + s_in + no]
        if sums:
            @pl.when(pl.program_id(0) == 0)
            def _():
                for r in out_refs[len(outs):]:
                    r[...] = jnp.zeros_like(r)

        compute(refs[:nr], refs[nr:at], out_refs)
        if side:
            @pl.when(pl.program_id(0) == ni - 1)
            def _():
                side.finish(*side_refs)

    anywhere = pl.BlockSpec(memory_space=pl.ANY)
    return _call(
        body, name=name, grid=(ni,),
        in_specs=[pl.BlockSpec((tm, r.shape[1]), lambda i: (i, 0)) for r in rows]
        + [pl.BlockSpec(wt.shape, lambda i: (0, 0)) for wt in weights] + [anywhere] * s_in,
        out_specs=[pl.BlockSpec((tm, width), lambda i: (i, 0)) for width, _ in outs]
        + [pl.BlockSpec((1, width), lambda i: (0, 0)) for width in sums] + [anywhere] * s_out,
        out_shape=[jax.ShapeDtypeStruct((t, width), dt) for width, dt in outs]
        + [jax.ShapeDtypeStruct((1, width), F32) for width in sums] + (list(side.out_shapes) if side else []),
        scratch_shapes=list(side.scratch) if side else [],
        compiler_params=_cp(vmem_mb),
    )(*rows, *weights, *(side.inputs if side else []))


def _mm_tn(name, a, b, scale=1.0, tm=1024, tn=1024):
    t, m = a.shape
    n = b.shape[1]
    tm, tn, tt = min(tm, m), min(tn, n), min(TT, t)
    nk = t // tt

    def body(a_ref, b_ref, o_ref):
        k = pl.program_id(2)

        @pl.when(k == 0)
        def _():
            o_ref[...] = jnp.zeros_like(o_ref)

        o_ref[...] += _dot_tn(a_ref[...].astype(BF16), b_ref[...].astype(BF16))
        if scale != 1.0:
            @pl.when(k == nk - 1)
            def _():
                o_ref[...] = o_ref[...] * scale

    return _call(
        body, name=name,
        grid=(m // tm, n // tn, nk),
        in_specs=[pl.BlockSpec((tt, tm), lambda i, j, k: (k, i)), pl.BlockSpec((tt, tn), lambda i, j, k: (k, j))],
        out_specs=pl.BlockSpec((tm, tn), lambda i, j, k: (i, j)),
        out_shape=jax.ShapeDtypeStruct((m, n), F32),
        compiler_params=_cp(),
    )(a, b)


def _rms_bwd_vals(xv, g, dn):
    r = lax.rsqrt(jnp.mean(xv * xv, axis=-1, keepdims=True) + EPS)
    xh = xv * r
    dxh = dn * g
    c = jnp.mean(dxh * xh, axis=-1, keepdims=True)
    return r * (dxh - xh * c), dn * xh


def _rms_bwd(name, x, gain, dn, dres):
    t, d = x.shape
    tm = min(ROW_TM, t)

    def body(x_ref, g_ref, dn_ref, dr_ref, dx_ref, dg_ref):
        @pl.when(pl.program_id(0) == 0)
        def _():
            dg_ref[...] = jnp.zeros_like(dg_ref)

        dx, dg = _rms_bwd_vals(x_ref[...], g_ref[...], dn_ref[...].astype(F32))
        dx_ref[...] = dr_ref[...] + dx
        dg_ref[...] += jnp.sum(dg, axis=0, keepdims=True)

    row = pl.BlockSpec((tm, d), lambda i: (i, 0))
    one = pl.BlockSpec((1, d), lambda i: (0, 0))
    return _call(
        body, name=name, grid=(t // tm,),
        in_specs=[row, one, row, row],
        out_specs=[row, one],
        out_shape=[jax.ShapeDtypeStruct((t, d), F32), jax.ShapeDtypeStruct((1, d), F32)],
        compiler_params=_cp(),
    )(x, gain, dn, dres)


def _colsum(name, x):
    t, n = x.shape
    tm = min(TM, t)

    def body(x_ref, o_ref):
        @pl.when(pl.program_id(0) == 0)
        def _():
            o_ref[...] = jnp.zeros_like(o_ref)

        o_ref[...] += jnp.sum(x_ref[...].astype(F32), axis=0, keepdims=True)

    return _call(
        body, name=name, grid=(t // tm,),
        in_specs=[pl.BlockSpec((tm, n), lambda i: (i, 0))],
        out_specs=pl.BlockSpec((1, n), lambda i: (0, 0)),
        out_shape=jax.ShapeDtypeStruct((1, n), F32),
        compiler_params=_cp(),
    )(x)


def _lora_norm_fwd(p_mla, gq, gkv):
    t = p_mla.shape[0]
    tm = min(ROW_TM, t)

    def body(p_ref, gq_ref, gkv_ref, q_ref, kv_ref):
        cq = p_ref[:, 0:Q_LORA]
        ckv = p_ref[:, Q_LORA:Q_LORA + KV_LORA]
        rq = lax.rsqrt(jnp.mean(cq * cq, axis=-1, keepdims=True) + EPS)
        rkv = lax.rsqrt(jnp.mean(ckv * ckv, axis=-1, keepdims=True) + EPS)
        q_ref[...] = (cq * rq * gq_ref[...]).astype(BF16)
        kv_ref[...] = (ckv * rkv * gkv_ref[...]).astype(BF16)

    return _call(
        body, name="lora_norm_fwd", grid=(t // tm,),
        in_specs=[pl.BlockSpec((tm, MLA_COLS), lambda i: (i, 0)),
                  pl.BlockSpec((1, Q_LORA), lambda i: (0, 0)), pl.BlockSpec((1, KV_LORA), lambda i: (0, 0))],
        out_specs=[pl.BlockSpec((tm, Q_LORA), lambda i: (i, 0)), pl.BlockSpec((tm, KV_LORA), lambda i: (i, 0))],
        out_shape=[jax.ShapeDtypeStruct((t, Q_LORA), BF16), jax.ShapeDtypeStruct((t, KV_LORA), BF16)],
        compiler_params=_cp(),
    )(p_mla, gq, gkv)


def _lora_norm_bwd(p_mla, gq, gkv, dcqn, dckvn, dkpe):
    t = p_mla.shape[0]
    tm = min(ROW_TM, t)

    def body(p_ref, gq_ref, gkv_ref, dq_ref, dkv_ref, dkpe_ref, dp_ref, dgq_ref, dgkv_ref):
        @pl.when(pl.program_id(0) == 0)
        def _():
            dgq_ref[...] = jnp.zeros_like(dgq_ref)
            dgkv_ref[...] = jnp.zeros_like(dgkv_ref)

        dcq, dgq = _rms_bwd_vals(p_ref[:, 0:Q_LORA], gq_ref[...], dq_ref[...])
        dckv, dgkv = _rms_bwd_vals(p_ref[:, Q_LORA:Q_LORA + KV_LORA], gkv_ref[...], dkv_ref[...])
        dp_ref[:, 0:Q_LORA] = dcq.astype(BF16)
        dp_ref[:, Q_LORA:Q_LORA + KV_LORA] = dckv.astype(BF16)
        dp_ref[:, Q_LORA + KV_LORA:MLA_COLS] = dkpe_ref[...].astype(BF16)
        dgq_ref[...] += jnp.sum(dgq, axis=0, keepdims=True)
        dgkv_ref[...] += jnp.sum(dgkv, axis=0, keepdims=True)

    return _call(
        body, name="lora_norm_bwd", grid=(t // tm,),
        in_specs=[pl.BlockSpec((tm, MLA_COLS), lambda i: (i, 0)),
                  pl.BlockSpec((1, Q_LORA), lambda i: (0, 0)), pl.BlockSpec((1, KV_LORA), lambda i: (0, 0)),
                  pl.BlockSpec((tm, Q_LORA), lambda i: (i, 0)), pl.BlockSpec((tm, KV_LORA), lambda i: (i, 0)),
                  pl.BlockSpec((tm, HK), lambda i: (i, 0))],
        out_specs=[pl.BlockSpec((tm, MLA_COLS), lambda i: (i, 0)),
                   pl.BlockSpec((1, Q_LORA), lambda i: (0, 0)), pl.BlockSpec((1, KV_LORA), lambda i: (0, 0))],
        out_shape=[jax.ShapeDtypeStruct((t, MLA_COLS), BF16), jax.ShapeDtypeStruct((1, Q_LORA), F32),
                   jax.ShapeDtypeStruct((1, KV_LORA), F32)],
        compiler_params=_cp(),
    )(p_mla, gq, gkv, dcqn, dckvn, dkpe)


def _cumsum_rows(x, row):
    for s in (1, 2, 4, 8, 16, 32):
        x = x + jnp.where(row >= s, pltpu.roll(x, s, 0), 0.0)
    return x


def _rcumsum_rows(x, row):
    for s in (1, 2, 4, 8, 16, 32):
        x = x + jnp.where(row < CHUNK - s, pltpu.roll(x, CHUNK - s, 0), 0.0)
    return x


def _hg_gates(qr, z, lb, row):
    q = _silu(qr)
    sg = _sig(z)
    f = lb + (1.0 - lb) * sg
    lf = jnp.log(f)
    k = (1.0 - lb) * (1.0 - sg)
    cum = _cumsum_rows(lf, row)
    mid = jnp.sum(jnp.where(row < CHUNK // 2, lf, 0.0), axis=0, keepdims=True)
    last = jnp.sum(lf, axis=0, keepdims=True)
    e_q = jnp.exp(jnp.minimum(cum - mid, EXP_CLAMP))
    e_k = jnp.exp(jnp.minimum(mid - cum, EXP_CLAMP))
    e_a = jnp.exp(cum)
    e_l = jnp.exp(last - cum)
    return q, sg, f, k, last, e_q, e_k, e_a, e_l


def _hgrn_fwd(p_hg, tab, gain):
    t = p_hg.shape[0]
    bt = min(HG_BT, t)
    nb, nc = t // bt, bt // CHUNK

    hpb = HG_HPB
    wide = hpb * HK

    def body(q_ref, f_ref, i_ref, g_ref, tab_ref, gain_ref, o_ref, ho_ref, st_ref, state):
        @pl.when(pl.program_id(1) == 0)
        def _():
            state[...] = jnp.zeros_like(state)

        row = lax.broadcasted_iota(jnp.int32, (CHUNK, HK), 0)
        tril = lax.broadcasted_iota(jnp.int32, (CHUNK, CHUNK), 0) >= lax.broadcasted_iota(jnp.int32, (CHUNK, CHUNK), 1)
        gain_v = gain_ref[...]

        def chunk(c, carry):
            sl = pl.ds(pl.multiple_of(c * CHUNK, CHUNK), CHUNK)
            for hh in range(hpb):
                ln = slice(hh * HK, (hh + 1) * HK)
                lb = _sig(tab_ref[0:1, ln] - tab_ref[1:2, ln])
                v = i_ref[sl, ln].astype(BF16)
                q, _, _, k, last, e_q, e_k, e_a, e_l = _hg_gates(q_ref[sl, ln], f_ref[sl, ln], lb, row)
                st = state[hh]
                st_ref[hh, c] = st
                p = jnp.where(tril, _dot_nt((q * e_q).astype(BF16), (k * e_k).astype(BF16)), 0.0)
                o = _dot(p.astype(BF16), v) + _dot_nt((q * e_a).astype(BF16), st.astype(BF16))
                state[hh] = jnp.exp(last) * st + _dot_tn(v, (k * e_l).astype(BF16))
                o_ref[sl, ln] = o
                r = lax.rsqrt(jnp.mean(o * o, axis=-1, keepdims=True) + EPS)
                ho_ref[sl, ln] = (o * r * gain_v * _silu(g_ref[sl, ln])).astype(BF16)
            return carry

        lax.fori_loop(0, nc, chunk, 0)

    def col(k):
        return pl.BlockSpec((bt, wide), lambda h, j, k=k: (j, k * (HEADS // hpb) + h))

    return _call(
        body, name="hgrn_fwd", grid=(HEADS // hpb, nb),
        in_specs=[col(0), col(1), col(2), col(3),
                  pl.BlockSpec((2, wide), lambda h, j: (0, h)), pl.BlockSpec((1, HK), lambda h, j: (0, 0))],
        out_specs=[pl.BlockSpec((bt, wide), lambda h, j: (j, h)), pl.BlockSpec((bt, wide), lambda h, j: (j, h)),
                   pl.BlockSpec((hpb, nc, HK, HK), lambda h, j: (h, j, 0, 0))],
        out_shape=[jax.ShapeDtypeStruct((t, D), F32), jax.ShapeDtypeStruct((t, D), BF16),
                   jax.ShapeDtypeStruct((HEADS, t // CHUNK, HK, HK), F32)],
        scratch_shapes=[pltpu.VMEM((hpb, HK, HK), F32)],
        compiler_params=_cp(),
    )(p_hg, p_hg, p_hg, p_hg, tab, gain)


def _hgrn_bwd(p_hg, tab, gain, o_raw, states, dho):
    t = p_hg.shape[0]
    bt = min(HG_BT, t)
    nb, nc = t // bt, bt // CHUNK
    hpb = HG_HPB
    wide = hpb * HK

    def body(q_ref, f_ref, i_ref, g_ref, tab_ref, gain_ref, o_ref, st_ref, dho_ref,
             dq_ref, df_ref, di_ref, dg_ref, dtab_ref, dgain_ref, dstate, dlb):
        h, j = pl.program_id(0), pl.program_id(1)

        @pl.when(jnp.logical_and(h == 0, j == 0))
        def _():
            dgain_ref[...] = jnp.zeros_like(dgain_ref)

        @pl.when(j == 0)
        def _():
            dstate[...] = jnp.zeros_like(dstate)
            dlb[...] = jnp.zeros_like(dlb)

        row = lax.broadcasted_iota(jnp.int32, (CHUNK, HK), 0)
        tril = lax.broadcasted_iota(jnp.int32, (CHUNK, CHUNK), 0) >= lax.broadcasted_iota(jnp.int32, (CHUNK, CHUNK), 1)
        gain_v = gain_ref[...]

        def chunk(cc, carry):
            c = nc - 1 - cc
            sl = pl.ds(pl.multiple_of(c * CHUNK, CHUNK), CHUNK)
            dgain = jnp.zeros((1, HK), F32)
            for hh in range(hpb):
                ln = slice(hh * HK, (hh + 1) * HK)
                lb = _sig(tab_ref[0:1, ln] - tab_ref[1:2, ln])
                qr = q_ref[sl, ln]
                v = i_ref[sl, ln].astype(BF16)
                gr = g_ref[sl, ln]
                q, sg, f, k, last, e_q, e_k, e_a, e_l = _hg_gates(qr, f_ref[sl, ln], lb, row)
                o = o_ref[sl, ln]
                r = lax.rsqrt(jnp.mean(o * o, axis=-1, keepdims=True) + EPS)
                oh = o * r
                dh = dho_ref[sl, ln].astype(F32)
                dnorm = dh * _silu(gr)
                dg_ref[sl, ln] = (dh * oh * gain_v * _dsilu(gr)).astype(BF16)
                dgain = dgain + jnp.sum(dnorm * oh, axis=0, keepdims=True)
                dxh = dnorm * gain_v
                do = (r * (dxh - oh * jnp.mean(dxh * oh, axis=-1, keepdims=True))).astype(BF16)
                st0 = st_ref[hh, c]
                st0_b = st0.astype(BF16)
                ds1 = dstate[hh]
                ds1_b = ds1.astype(BF16)
                qt = (q * e_q).astype(BF16)
                kt = (k * e_k).astype(BF16)
                qd = (q * e_a).astype(BF16)
                kd = (k * e_l).astype(BF16)
                p = jnp.where(tril, _dot_nt(qt, kt), 0.0).astype(BF16)
                dp = jnp.where(tril, _dot_nt(do, v), 0.0).astype(BF16)
                dv = _dot_tn(p, do) + _dot_nt(kd, ds1_b)
                dqt = _dot(dp, kt)
                dkt = _dot_tn(dp, qt)
                dq_inter = _dot(do, st0_b) * e_a
                dk_inter = _dot(v, ds1_b) * e_l
                dq = dqt * e_q + dq_inter
                dk = dkt * e_k + dk_inter
                e_last = jnp.exp(last)
                dstate[hh] = _dot_tn(do, qd) + e_last * ds1
                dlast = (jnp.sum(k * dk_inter, axis=0, keepdims=True)
                         + e_last * jnp.sum(ds1 * st0, axis=0, keepdims=True))
                da = (qt.astype(F32) * dqt - kt.astype(F32) * dkt + q * dq_inter - k * dk_inter
                      + jnp.where(row == CHUNK - 1, dlast, 0.0))
                dlf = _rcumsum_rows(da, row)
                dfv = dlf / f - dk
                df_ref[sl, ln] = (dfv * (1.0 - lb) * sg * (1.0 - sg)).astype(BF16)
                dlb[:, ln] += jnp.sum(dfv * (1.0 - sg), axis=0, keepdims=True)
                dq_ref[sl, ln] = (dq * _dsilu(qr)).astype(BF16)
                di_ref[sl, ln] = dv.astype(BF16)
            dgain_ref[...] += dgain
            return carry

        lax.fori_loop(0, nc, chunk, 0)

        @pl.when(j == nb - 1)
        def _():
            lb = _sig(tab_ref[0:1, :] - tab_ref[1:2, :])
            d0 = dlb[...] * lb * (1.0 - lb)
            dtab_ref[0:1, :] = d0
            dtab_ref[1:2, :] = -d0

    def col(k):
        return pl.BlockSpec((bt, wide), lambda h, j, k=k: (nb - 1 - j, k * (HEADS // hpb) + h))

    tok = pl.BlockSpec((bt, wide), lambda h, j: (nb - 1 - j, h))
    return _call(
        body, name="hgrn_bwd", grid=(HEADS // hpb, nb),
        in_specs=[col(0), col(1), col(2), col(3),
                  pl.BlockSpec((2, wide), lambda h, j: (0, h)), pl.BlockSpec((1, HK), lambda h, j: (0, 0)),
                  tok, pl.BlockSpec((hpb, nc, HK, HK), lambda h, j: (h, nb - 1 - j, 0, 0)), tok],
        out_specs=[tok, tok, tok, tok,
                   pl.BlockSpec((2, wide), lambda h, j: (0, h)), pl.BlockSpec((1, HK), lambda h, j: (0, 0))],
        out_shape=[jax.ShapeDtypeStruct((t, D), BF16)] * 4
        + [jax.ShapeDtypeStruct((2, D), F32), jax.ShapeDtypeStruct((1, HK), F32)],
        scratch_shapes=[pltpu.VMEM((hpb, HK, HK), F32), pltpu.VMEM((1, wide), F32)],
        compiler_params=_cp(),
    )(p_hg, p_hg, p_hg, p_hg, tab, gain, o_raw, states, dho)


def _rope_tables(pos):
    t = pos.shape[0]
    tm = min(ROW_TM, t)
    inv = np.zeros((1, HK), np.float32)
    freq = (ROPE_THETA ** (-np.arange(0, ROPE, 2, dtype=np.float32) / ROPE)).astype(np.float32)
    inv[0, 0:ROPE // 2] = freq
    inv[0, ROPE // 2:ROPE] = freq
    sign = np.zeros((1, HK), np.float32)
    sign[0, 0:ROPE // 2] = -1.0
    sign[0, ROPE // 2:ROPE] = 1.0

    def body(pos_ref, inv_ref, sign_ref, cos_ref, sin_ref):
        ang = pos_ref[...].astype(F32) * inv_ref[...]
        cos_ref[...] = jnp.cos(ang)
        sin_ref[...] = jnp.sin(ang) * sign_ref[...]

    one = pl.BlockSpec((1, HK), lambda i: (0, 0))
    row = pl.BlockSpec((tm, HK), lambda i: (i, 0))
    return _call(
        body, name="rope_tables", grid=(t // tm,),
        in_specs=[pl.BlockSpec((tm, 1), lambda i: (i, 0)), one, one],
        out_specs=[row, row],
        out_shape=[jax.ShapeDtypeStruct((t, HK), F32)] * 2,
        compiler_params=_cp(),
    )(pos, jnp.asarray(inv), jnp.asarray(sign))


def _rope(x, cos, sin_signed):
    r = lax.broadcasted_iota(jnp.int32, (HK, HK), 0)
    c = lax.broadcasted_iota(jnp.int32, (HK, HK), 1)
    half = ROPE // 2
    swap = jnp.logical_or(jnp.logical_and(c < half, r == c + half),
                          jnp.logical_and(jnp.logical_and(c >= half, c < ROPE), r == c - half))
    return x * cos + _dot_split(x, swap.astype(BF16)) * sin_signed


def _dot_split(x, m):
    hi = x.astype(BF16)
    lo = (x - hi.astype(F32)).astype(BF16)
    return _dot(hi, m) + _dot(lo, m)


def _lane_sum(x):
    return _dot_split(x, jnp.ones((HK, HK), BF16))


def _head_norm(xn, xr):
    r = lax.rsqrt(_lane_sum(xn * xn + xr * xr) * (1.0 / QK) + EPS)
    return xn * r, xr * r, r


def _head_norm_bwd(xn, xr, g_n, g_r, dn, dr):
    hn, hr, r = _head_norm(xn, xr)
    dxn, dxr = dn * g_n, dr * g_r
    c = _lane_sum(dxn * hn + dxr * hr) * (1.0 / QK)
    return r * (dxn - hn * c), r * (dxr - hr * c), dn * hn, dr * hr


def _mla_prep_fwd(qf, kv, p_mla, cos, sin, gq, gk):
    t = qf.shape[0]
    tm = min(ROW_TM, t)

    def body(qf_ref, kv_ref, kpe_ref, cos_ref, sin_ref, gq_ref, gk_ref, q_ref, k_ref, v_ref):
        cos_v, sin_v = cos_ref[...], sin_ref[...]
        kpe = kpe_ref[...]
        for h in range(HEADS):
            lo, mid, hi = h * QKP, h * QKP + HK, (h + 1) * QKP
            qn, qr, _ = _head_norm(qf_ref[:, lo:mid], qf_ref[:, mid:hi])
            q_ref[h, :, 0:HK] = (qn * gq_ref[:, 0:HK] * (SCALE * LOG2E)).astype(BF16)
            q_ref[h, :, HK:QKP] = (_rope(qr * gq_ref[:, HK:QKP], cos_v, sin_v) * (SCALE * LOG2E)).astype(BF16)
            kn, kr, _ = _head_norm(kv_ref[:, lo:mid], kpe)
            k_ref[h, :, 0:HK] = (kn * gk_ref[:, 0:HK]).astype(BF16)
            k_ref[h, :, HK:QKP] = _rope(kr * gk_ref[:, HK:QKP], cos_v, sin_v).astype(BF16)
            v_ref[h, :, 0:HK] = kv_ref[:, mid:hi].astype(BF16)
            v_ref[h, :, HK:QKP] = jnp.full((tm, HK), -1.0, BF16)

    head = pl.BlockSpec((tm, HEADS * QKP), lambda i: (i, 0))
    tok = pl.BlockSpec((tm, HK), lambda i: (i, 0))
    gain = pl.BlockSpec((1, QKP), lambda i: (0, 0))
    return _call(
        body, name="mla_prep_fwd", grid=(t // tm,),
        in_specs=[head, head, pl.BlockSpec((tm, HK), lambda i: (i, MLA_COLS // HK - 1)), tok, tok, gain, gain],
        out_specs=[pl.BlockSpec((HEADS, tm, QKP), lambda i: (0, i, 0)),
                   pl.BlockSpec((HEADS, tm, QKP), lambda i: (0, i, 0)),
                   pl.BlockSpec((HEADS, tm, QKP), lambda i: (0, i, 0))],
        out_shape=[jax.ShapeDtypeStruct((HEADS, t, QKP), BF16), jax.ShapeDtypeStruct((HEADS, t, QKP), BF16),
                   jax.ShapeDtypeStruct((HEADS, t, QKP), BF16)],
        compiler_params=_cp(),
    )(qf, kv, p_mla, cos, sin, gq, gk)


def _mla_prep_bwd(qf, kv, p_mla, cos, sin, gq, gk, dq, dk, dv):
    t = qf.shape[0]
    tm = min(ROW_TM, t)

    def body(qf_ref, kv_ref, kpe_ref, cos_ref, sin_ref, gq_ref, gk_ref, dq_ref, dk_ref, dv_ref,
             dqf_ref, dkv_ref, dkpe_ref, dgq_ref, dgk_ref):
        @pl.when(pl.program_id(0) == 0)
        def _():
            dgq_ref[...] = jnp.zeros_like(dgq_ref)
            dgk_ref[...] = jnp.zeros_like(dgk_ref)

        cos_v, sin_v = cos_ref[...], -sin_ref[...]
        kpe = kpe_ref[...]
        gqn, gqr, gkn, gkr = gq_ref[:, 0:HK], gq_ref[:, HK:QKP], gk_ref[:, 0:HK], gk_ref[:, HK:QKP]
        dkpe = jnp.zeros((tm, HK), F32)
        dgq_n, dgq_r, dgk_n, dgk_r = [jnp.zeros((1, HK), F32) for _ in range(4)]
        for h in range(HEADS):
            lo, mid, hi = h * QKP, h * QKP + HK, (h + 1) * QKP
            dqn = dq_ref[h, :, 0:HK].astype(F32) * SCALE
            dqr = _rope(dq_ref[h, :, HK:QKP].astype(F32), cos_v, sin_v) * SCALE
            a, b, ga, gb = _head_norm_bwd(qf_ref[:, lo:mid], qf_ref[:, mid:hi], gqn, gqr, dqn, dqr)
            dqf_ref[:, lo:mid] = a.astype(BF16)
            dqf_ref[:, mid:hi] = b.astype(BF16)
            dgq_n = dgq_n + jnp.sum(ga, axis=0, keepdims=True)
            dgq_r = dgq_r + jnp.sum(gb, axis=0, keepdims=True)
            dkn = dk_ref[h, :, 0:HK].astype(F32) * LN2
            dkr = _rope(dk_ref[h, :, HK:QKP].astype(F32), cos_v, sin_v) * LN2
            a, b, ga, gb = _head_norm_bwd(kv_ref[:, lo:mid], kpe, gkn, gkr, dkn, dkr)
            dkv_ref[:, lo:mid] = a.astype(BF16)
            dkv_ref[:, mid:hi] = dv_ref[h].astype(BF16)
            dkpe = dkpe + b
            dgk_n = dgk_n + jnp.sum(ga, axis=0, keepdims=True)
            dgk_r = dgk_r + jnp.sum(gb, axis=0, keepdims=True)
        dkpe_ref[...] = dkpe
        dgq_ref[:, 0:HK] += dgq_n
        dgq_ref[:, HK:QKP] += dgq_r
        dgk_ref[:, 0:HK] += dgk_n
        dgk_ref[:, HK:QKP] += dgk_r

    head = pl.BlockSpec((tm, HEADS * QKP), lambda i: (i, 0))
    tok = pl.BlockSpec((tm, HK), lambda i: (i, 0))
    gain = pl.BlockSpec((1, QKP), lambda i: (0, 0))
    hq = pl.BlockSpec((HEADS, tm, QKP), lambda i: (0, i, 0))
    return _call(
        body, name="mla_prep_bwd", grid=(t // tm,),
        in_specs=[head, head, pl.BlockSpec((tm, HK), lambda i: (i, MLA_COLS // HK - 1)), tok, tok, gain, gain,
                  hq, hq, pl.BlockSpec((HEADS, tm, HK), lambda i: (0, i, 0))],
        out_specs=[head, head, tok, gain, gain],
        out_shape=[jax.ShapeDtypeStruct((t, HEADS * QKP), BF16), jax.ShapeDtypeStruct((t, HEADS * QKP), BF16),
                   jax.ShapeDtypeStruct((t, HK), F32), jax.ShapeDtypeStruct((1, QKP), F32),
                   jax.ShapeDtypeStruct((1, QKP), F32)],
        compiler_params=_cp(),
    )(qf, kv, p_mla, cos, sin, gq, gk, dq, dk, dv)


def _chunk_mask(row0, rows, cols):
    r = lax.broadcasted_iota(jnp.int32, (rows, cols), 0) + row0
    c = lax.broadcasted_iota(jnp.int32, (rows, cols), 1)
    return jnp.right_shift(r, 6) >= jnp.right_shift(c, 6)


def _flash_fwd(q, k, v):
    t = q.shape[1]
    tq = min(TQ, t)
    nq = t // tq
    sub = min(SUBQ, tq)
    pairs = [(i, j) for i in range(nq) for j in range(i + 1)]
    qi = jnp.asarray([p[0] for p in pairs], jnp.int32)
    kj = jnp.asarray([p[1] for p in pairs], jnp.int32)

    def body(qi_ref, kj_ref, q_ref, k_ref, v_ref, o_ref, lse_ref, m_s, acc_s):
        n = pl.program_id(1)
        i, j = qi_ref[n], kj_ref[n]

        @pl.when(j == 0)
        def _():
            m_s[...] = jnp.full_like(m_s, NEG)
            acc_s[...] = jnp.zeros_like(acc_s)

        def step(diag):
            subs = range(tq // sub)
            width = [(r + 1) * sub if diag else tq for r in subs]
            logits = [_dot_nt(q_ref[r * sub:(r + 1) * sub, :], k_ref[0:width[r], :]) for r in subs]
            for r in subs:
                rows = slice(r * sub, (r + 1) * sub)
                cols = width[r]
                s = logits[r]
                if diag:
                    s = jnp.where(_chunk_mask(r * sub, sub, cols), s, NEG)
                m_old = m_s[rows, :]
                m_new = jnp.maximum(m_old, jnp.max(s, axis=-1, keepdims=True))
                alpha = jnp.exp2(m_old - m_new)
                p = jnp.exp2((s - jnp.tile(m_new, (1, cols // HK))).astype(BF16))
                acc_s[rows, :] = jnp.tile(alpha, (1, 2)) * acc_s[rows, :] + _dot(p, v_ref[0:cols, :])
                m_s[rows, :] = m_new

        @pl.when(j < i)
        def _():
            step(False)

        @pl.when(j == i)
        def _():
            step(True)
            l = -acc_s[:, HK:QKP]
            o_ref[...] = (acc_s[:, 0:HK] / l).astype(BF16)
            lse_ref[...] = m_s[...] + jnp.log(l) * LOG2E

    grid_spec = pltpu.PrefetchScalarGridSpec(
        num_scalar_prefetch=2, grid=(HEADS, len(pairs)),
        in_specs=[pl.BlockSpec((None, tq, QKP), lambda h, n, qi, kj: (h, qi[n], 0)),
                  pl.BlockSpec((None, tq, QKP), lambda h, n, qi, kj: (h, kj[n], 0)),
                  pl.BlockSpec((None, tq, QKP), lambda h, n, qi, kj: (h, kj[n], 0))],
        out_specs=[pl.BlockSpec((tq, HK), lambda h, n, qi, kj: (qi[n], h)),
                   pl.BlockSpec((None, tq, HK), lambda h, n, qi, kj: (h, qi[n], 0))],
        scratch_shapes=[pltpu.VMEM((tq, HK), F32), pltpu.VMEM((tq, QKP), F32)],
    )
    return _call(
        body, name="flash_fwd", grid_spec=grid_spec,
        out_shape=[jax.ShapeDtypeStruct((t, D), BF16), jax.ShapeDtypeStruct((HEADS, t, HK), F32)],
        compiler_params=_cp(),
    )(qi, kj, q, k, v)


def _attn_do(do, o):
    t = do.shape[0]
    tm = min(TM, t)

    def body(do_ref, o_ref, d_ref):
        lane = lax.broadcasted_iota(jnp.int32, (tm, HK), 1)
        for h in range(HEADS):
            ln = slice(h * HK, (h + 1) * HK)
            dov = do_ref[:, ln]
            d = jnp.sum(dov.astype(F32) * o_ref[:, ln].astype(F32), axis=-1, keepdims=True)
            hi = d.astype(BF16).astype(F32)
            d_ref[h, :, 0:HK] = dov
            d_ref[h, :, HK:QKP] = jnp.where(lane == 0, hi, jnp.where(lane == 1, d - hi, 0.0)).astype(BF16)

    blk = pl.BlockSpec((tm, D), lambda i: (i, 0))
    return _call(
        body, name="attn_do", grid=(t // tm,),
        in_specs=[blk, blk],
        out_specs=pl.BlockSpec((HEADS, tm, QKP), lambda i: (0, i, 0)),
        out_shape=jax.ShapeDtypeStruct((HEADS, t, QKP), BF16),
        compiler_params=_cp(),
    )(do, o)


def _flash_bwd(q, k, v, lse, do):
    t = q.shape[1]
    tq = min(TQ, t)
    nq = t // tq
    sub = min(SUBQ, tq)
    pairs = [(i, j) for j in range(nq) for i in range(j, nq)]
    qi = jnp.asarray([p[0] for p in pairs], jnp.int32)
    kj = jnp.asarray([p[1] for p in pairs], jnp.int32)
    npairs = len(pairs)

    def body(qi_ref, kj_ref, q_ref, k_ref, v_ref, lse_ref, do_ref, dq_ref, dk_ref, dv_ref):
        n = pl.program_id(1)
        i, j = qi_ref[n], kj_ref[n]

        @pl.when(n == 0)
        def _():
            dq_ref[...] = jnp.zeros_like(dq_ref)

        @pl.when(i == j)
        def _():
            dk_ref[...] = jnp.zeros_like(dk_ref)
            dv_ref[...] = jnp.zeros_like(dv_ref)

        def step(diag):
            for r in range(tq // sub):
                rows = slice(r * sub, (r + 1) * sub)
                cols = (r + 1) * sub if diag else tq
                qv, kv_ = q_ref[rows, :], k_ref[0:cols, :]
                p = jnp.exp2(_dot_nt(qv, kv_) - jnp.tile(lse_ref[rows, :], (1, cols // HK)))
                if diag:
                    p = jnp.where(_chunk_mask(r * sub, sub, cols), p, 0.0)
                dp_less_delta = _dot_nt(do_ref[rows, :], v_ref[0:cols, :])
                ds = (p * dp_less_delta).astype(BF16)
                dv_ref[0:cols, :] += _dot_tn(p.astype(BF16), do_ref[rows, 0:HK])
                dk_ref[0:cols, :] += _dot_tn(ds, qv)
                dq_rows = pl.ds(pl.multiple_of(i * tq + r * sub, sub), sub)
                dq_ref[dq_rows, :] += _dot(ds, kv_)

        @pl.when(j < i)
        def _():
            step(False)

        @pl.when(j == i)
        def _():
            step(True)

    grid_spec = pltpu.PrefetchScalarGridSpec(
        num_scalar_prefetch=2, grid=(HEADS, npairs),
        in_specs=[pl.BlockSpec((None, tq, QKP), lambda h, n, qi, kj: (h, qi[n], 0)),
                  pl.BlockSpec((None, tq, QKP), lambda h, n, qi, kj: (h, kj[n], 0)),
                  pl.BlockSpec((None, tq, QKP), lambda h, n, qi, kj: (h, kj[n], 0)),
                  pl.BlockSpec((None, tq, HK), lambda h, n, qi, kj: (h, qi[n], 0)),
                  pl.BlockSpec((None, tq, QKP), lambda h, n, qi, kj: (h, qi[n], 0))],
        out_specs=[pl.BlockSpec((None, t, QKP), lambda h, n, qi, kj: (h, 0, 0)),
                   pl.BlockSpec((None, tq, QKP), lambda h, n, qi, kj: (h, kj[n], 0)),
                   pl.BlockSpec((None, tq, HK), lambda h, n, qi, kj: (h, kj[n], 0))],
    )
    return _call(
        body, name="flash_bwd", grid_spec=grid_spec,
        out_shape=[jax.ShapeDtypeStruct((HEADS, t, QKP), F32), jax.ShapeDtypeStruct((HEADS, t, QKP), F32),
                   jax.ShapeDtypeStruct((HEADS, t, HK), F32)],
        compiler_params=_cp(56),
    )(qi, kj, q, k, v, lse, do)


def _adamw(name, w, g, m, v):
    r, c = w.shape
    tr = r if r <= 256 else next(k for k in (256, 352, 384) if r % k == 0)

    def body(w_ref, g_ref, m_ref, v_ref, d_ref, nm_ref, nv_ref):
        gv = g_ref[...]
        nm = ADAM_B1 * m_ref[...] + (1.0 - ADAM_B1) * gv
        nv = ADAM_B2 * v_ref[...] + (1.0 - ADAM_B2) * (gv * gv)
        m_hat = nm / (1.0 - ADAM_B1 ** ADAM_STEP)
        v_hat = nv / (1.0 - ADAM_B2 ** ADAM_STEP)
        d_ref[...] = -ADAM_LR * (m_hat / (jnp.sqrt(v_hat) + ADAM_EPS) + ADAM_WD * w_ref[...])
        nm_ref[...] = nm
        nv_ref[...] = nv

    blk = pl.BlockSpec((tr, c), lambda i: (i, 0))
    return _call(
        body, name=name, grid=(r // tr,),
        in_specs=[blk] * 4, out_specs=[blk] * 3,
        out_shape=[jax.ShapeDtypeStruct((r, c), F32)] * 3,
        compiler_params=_cp(),
    )(w, g, m, v)


def _place():
    return lax.axis_index("x"), lax.axis_index("y"), lax.axis_index("c")


def _other_chips(x, y):
    return [(1 - x, y), (x, 1 - y), (1 - x, 1 - y)]


class _Exchange:
    inputs = ()
    out_shapes = ()
    scratch = ()

    def start(self, *refs):
        raise NotImplementedError

    def finish(self, *refs):
        raise NotImplementedError

    def alone(self, name):
        def body(*refs):
            self.start(*refs)
            self.finish(*refs)

        anywhere = pl.BlockSpec(memory_space=pl.ANY)
        return _call(
            body, name=name,
            in_specs=[anywhere] * len(self.inputs), out_specs=[anywhere] * len(self.out_shapes),
            out_shape=list(self.out_shapes), scratch_shapes=list(self.scratch),
        )(*self.inputs)


class _GatherWeights(_Exchange):
    def __init__(self, shard):
        self.r = shard.shape[0]
        self.inputs = (shard,)
        self.out_shapes = (jax.ShapeDtypeStruct((4, self.r, PACK_W), shard.dtype),)
        self.scratch = (pltpu.SemaphoreType.DMA((6,)), pltpu.SemaphoreType.DMA((6,)))

    def gathered(self, got, k):
        return lax.dynamic_update_slice(got, self.inputs[0][None], (k, 0, 0))

    def _copies(self, s_ref, g_ref, send_sems, recv_sems):
        half = self.r // 2
        x, y, c = _place()
        chips = _other_chips(x, y)

        def rows(px, py, pc):
            return g_ref.at[2 * px + py, pl.ds(pc * half, half), :]

        def copy(k, block, to, src=None):
            return pltpu.make_async_remote_copy(
                src_ref=rows(*block) if src is None else src, dst_ref=rows(*block),
                send_sem=send_sems.at[k], recv_sem=recv_sems.at[k], device_id=to, device_id_type=MESH)

        first = [copy(j, (x, y, c), (*chip, c), src=s_ref.at[pl.ds(c * half, half), :]) for j, chip in enumerate(chips)]
        passed = [copy(3 + j, (*chip, c), (x, y, 1 - c)) for j, chip in enumerate(chips)]
        landed = [copy(j, (*chip, c), (x, y, c)) for j, chip in enumerate(chips)]
        landed += [copy(3 + j, (*chip, 1 - c), (x, y, c)) for j, chip in enumerate(chips)]
        return first, passed, landed

    def start(self, *refs):
        first, _, _ = self._copies(*refs)
        for cp in first:
            cp.start()

    def finish(self, *refs):
        first, passed, landed = self._copies(*refs)
        for j in range(3):
            landed[j].wait_recv()
            passed[j].start()
        for j in range(3):
            landed[3 + j].wait_recv()
        for cp in first + passed:
            cp.wait_send()


def _swap_halves(name, gp):
    r = gp.shape[1]
    half = r // 2

    def body(g_ref, o_ref, send_sem, recv_sem):
        x, y, c = _place()
        cp = pltpu.make_async_remote_copy(
            src_ref=g_ref.at[:, pl.ds((1 - c) * half, half), :], dst_ref=o_ref,
            send_sem=send_sem, recv_sem=recv_sem, device_id=(x, y, 1 - c), device_id_type=MESH)
        cp.start()
        cp.wait()

    return _call(
        body, name=name,
        in_specs=[pl.BlockSpec(memory_space=pl.ANY)],
        out_specs=pl.BlockSpec(memory_space=pl.ANY),
        out_shape=jax.ShapeDtypeStruct((4, half, PACK_W), gp.dtype),
        scratch_shapes=[pltpu.SemaphoreType.DMA, pltpu.SemaphoreType.DMA],
    )(gp)


def _chip_sum(name, gp, got, c_arr):
    half = got.shape[1]
    tr = ADD_ROWS
    nb = half // tr

    def body(c_ref, a_ref, b_ref, o_ref, ob_ref):
        s = a_ref[...] + b_ref[...]
        o_ref[...] = s
        ob_ref[...] = s.astype(BF16)

    grid_spec = pltpu.PrefetchScalarGridSpec(
        num_scalar_prefetch=1, grid=(4, nb),
        in_specs=[pl.BlockSpec((None, tr, PACK_W), lambda s, i, c: (s, c[0] * nb + i, 0)),
                  pl.BlockSpec((None, tr, PACK_W), lambda s, i, c: (s, i, 0))],
        out_specs=[pl.BlockSpec((None, tr, PACK_W), lambda s, i, c: (s, i, 0)),
                   pl.BlockSpec((None, tr, PACK_W), lambda s, i, c: (s, i, 0))],
    )
    return _call(
        body, name=name, grid_spec=grid_spec,
        out_shape=[jax.ShapeDtypeStruct(got.shape, F32), jax.ShapeDtypeStruct(got.shape, BF16)],
        compiler_params=_cp(),
    )(c_arr, gp, got)


class _ScatterChipSums(_Exchange):
    def __init__(self, cs):
        self.inputs = (cs,)
        self.out_shapes = (jax.ShapeDtypeStruct((3,) + cs.shape[1:], cs.dtype),)
        self.scratch = (pltpu.SemaphoreType.DMA((3,)), pltpu.SemaphoreType.DMA((3,)))

    def _copies(self, s_ref, o_ref, send_sems, recv_sems):
        x, y, c = _place()
        return [pltpu.make_async_remote_copy(
            src_ref=s_ref.at[2 * px + py], dst_ref=o_ref.at[j],
            send_sem=send_sems.at[j], recv_sem=recv_sems.at[j], device_id=(px, py, c), device_id_type=MESH)
            for j, (px, py) in enumerate(_other_chips(x, y))]

    def start(self, *refs):
        for cp in self._copies(*refs):
            cp.start()

    def finish(self, *refs):
        for cp in self._copies(*refs):
            cp.wait()


def _shard_sum(name, cs, got, kc_arr):
    h = cs.shape[1]
    tr = ADD_ROWS
    nb = h // tr

    def body(k_ref, a_ref, b_ref, o_ref):
        o_ref[...] = ((a_ref[...] + b_ref[0].astype(F32)) + b_ref[1].astype(F32)) + b_ref[2].astype(F32)

    grid_spec = pltpu.PrefetchScalarGridSpec(
        num_scalar_prefetch=1, grid=(nb,),
        in_specs=[pl.BlockSpec((None, tr, PACK_W), lambda i, k: (k[0], i, 0)),
                  pl.BlockSpec((3, tr, PACK_W), lambda i, k: (0, i, 0))],
        out_specs=pl.BlockSpec((tr, PACK_W), lambda i, k: (k[1] * nb + i, 0)),
    )
    return _call(
        body, name=name, grid_spec=grid_spec,
        out_shape=jax.ShapeDtypeStruct((2 * h, PACK_W), F32),
        compiler_params=_cp(),
    )(kc_arr, cs, got)


def _join_halves(name, both):
    h = both.shape[0] // 2

    def body(m_ref, o_ref, send_sem, recv_sem):
        x, y, c = _place()
        cp = pltpu.make_async_remote_copy(
            src_ref=m_ref.at[pl.ds(c * h, h), :], dst_ref=o_ref.at[pl.ds(c * h, h), :],
            send_sem=send_sem, recv_sem=recv_sem, device_id=(x, y, 1 - c), device_id_type=MESH)
        cp.start()
        cp.wait_send()
        pltpu.make_async_remote_copy(
            src_ref=m_ref.at[pl.ds(c * h, h), :], dst_ref=o_ref.at[pl.ds((1 - c) * h, h), :],
            send_sem=send_sem, recv_sem=recv_sem, device_id=(x, y, 1 - c), device_id_type=MESH).wait_recv()

    return _call(
        body, name=name,
        in_specs=[pl.BlockSpec(memory_space=pl.ANY)],
        out_specs=pl.BlockSpec(memory_space=pl.ANY),
        out_shape=jax.ShapeDtypeStruct(both.shape, both.dtype),
        input_output_aliases={0: 0},
        scratch_shapes=[pltpu.SemaphoreType.DMA, pltpu.SemaphoreType.DMA],
    )(both)


def _all_reduce_small(v):
    r = v.shape[0]

    def body(v_ref, o_ref, buf, send_sems, recv_sems):
        x, y, c = _place()
        me = 4 * x + 2 * y + c
        buf[me] = v_ref[...]
        cps = []
        for k in range(1, 8):
            peer = (x ^ (k >> 2), y ^ ((k >> 1) & 1), c ^ (k & 1))
            cps.append(pltpu.make_async_remote_copy(
                src_ref=v_ref, dst_ref=buf.at[me],
                send_sem=send_sems.at[k - 1], recv_sem=recv_sems.at[k - 1], device_id=peer, device_id_type=MESH))
        for cp in cps:
            cp.start()
        for k in range(1, 8):
            pltpu.make_async_remote_copy(
                src_ref=v_ref, dst_ref=buf.at[me ^ k],
                send_sem=send_sems.at[k - 1], recv_sem=recv_sems.at[k - 1],
                device_id=(x, y, c), device_id_type=MESH).wait_recv()
        for cp in cps:
            cp.wait_send()
        acc = buf[0]
        for k in range(1, 8):
            acc = acc + buf[k]
        o_ref[...] = acc

    return _call(
        body, name="all_reduce_small",
        in_specs=[pl.BlockSpec(memory_space=pltpu.VMEM)],
        out_specs=pl.BlockSpec(memory_space=pltpu.VMEM),
        out_shape=jax.ShapeDtypeStruct((r, 128), F32),
        scratch_shapes=[pltpu.VMEM((8, r, 128), F32), pltpu.SemaphoreType.DMA((7,)), pltpu.SemaphoreType.DMA((7,))],
    )(v)


def _group(names):
    return tuple(e for e in BIG if e[0] in names)


def _pack(shards, dtype):
    return jnp.concatenate([s.astype(dtype).reshape(-1, PACK_W) for s in shards], axis=0)


def _unpack_full(g, group):
    out, at = {}, 0
    for name, rows, cols, axis in group:
        n = rows * cols // 4 // PACK_W
        blk = g[:, at:at + n, :]
        at += n
        if axis == 1:
            out[name] = blk.reshape(4, rows, cols // 4).transpose(1, 0, 2).reshape(rows, cols)
        else:
            out[name] = blk.reshape(rows, cols)
    return out


def _pack_grads(grads, group):
    parts = []
    for name, rows, cols, axis in group:
        g = grads[name]
        if axis == 1:
            g = g.reshape(rows, 4, cols // 4).transpose(1, 0, 2)
        parts.append(g.reshape(4, -1, PACK_W))
    rows_total = sum(p.shape[1] for p in parts)
    pad = -rows_total % PACK_ALIGN
    if pad:
        parts.append(jnp.zeros((4, pad, PACK_W), F32))
    return jnp.concatenate(parts, axis=1)


def _unpack_shard(s, group):
    out, at = {}, 0
    for name, rows, cols, axis in group:
        n = rows * cols // 4 // PACK_W
        shape = (rows, cols // 4) if axis == 1 else (rows // 4, cols)
        out[name] = s[at:at + n, :].reshape(shape)
        at += n
    return out


def _pack_small(parts):
    flat = jnp.concatenate([p.reshape(-1) for p in parts])
    pad = -flat.shape[0] % 1024
    return jnp.concatenate([flat, jnp.zeros((pad,), F32)]).reshape(-1, 128)


def _ffn_fwd(tag, h, gain, w_in, w_out, next_gain, target=None, side=None):
    t = h.shape[0]
    tm = min(FFN_TM, t)

    def compute_in(rows, weights, outs):
        hv, w_ref = rows[0][...], weights[0]
        r = lax.rsqrt(jnp.mean(hv * hv, axis=-1, keepdims=True) + EPS)
        a = (hv * r * weights[1][...]).astype(BF16)
        outs[0][...] = a
        for j in range(DFF // FFN_CHUNK):
            cols = slice(j * FFN_CHUNK, (j + 1) * FFN_CHUNK)
            gate = _dot(a, w_ref[:, cols])
            up = _dot(a, w_ref[:, DFF + j * FFN_CHUNK:DFF + (j + 1) * FFN_CHUNK])
            outs[1][:, cols] = gate.astype(BF16)
            outs[2][:, cols] = up.astype(BF16)
            outs[3][:, cols] = (_silu(gate) * up).astype(BF16)

    n, gate, up, act, *side_out = _rows_call(tag + "_in", [h], [w_in, gain], [(D, BF16)] + [(DFF, BF16)] * 3,
                                             compute_in, tm, side=side)

    def compute_out(rows, weights, outs):
        hn = rows[1][...] + 0.5 * _dot(rows[0][...], weights[0][...])
        g = weights[1][...]
        r = lax.rsqrt(jnp.mean(hn * hn, axis=-1, keepdims=True) + EPS)
        xh = hn * r
        if target is None:
            outs[0][...] = hn
            outs[1][...] = (xh * g).astype(BF16)
        else:
            err = xh * g - rows[2][...]
            dy = err * (1.0 / D)
            dxh = dy * g
            outs[0][...] = r * (dxh - xh * jnp.mean(dxh * xh, axis=-1, keepdims=True))
            outs[1][...] += jnp.sum(dy * xh, axis=0, keepdims=True)
            outs[2][...] += 0.5 * jnp.sum(jnp.mean(err * err, axis=-1, keepdims=True), axis=0, keepdims=True)

    if target is None:
        tail = _rows_call(tag + "_out", [act, h], [w_out, next_gain], [(D, F32), (D, BF16)], compute_out, tm)
    else:
        tail = _rows_call(tag + "_out", [act, h, target], [w_out, next_gain], [(D, F32)], compute_out, tm, sums=(D, 128))
    return tail, (n, gate, up, act), side_out


class _Reduction:
    def __init__(self, tag, group, c_arr, k_arr):
        self.tag, self.group, self.c_arr, self.k_arr = tag, group, c_arr, k_arr

    def begin(self, grads):
        gp = _pack_grads(grads, self.group)
        self.sums, sums_bf16 = _chip_sum("grad_chip_sum_" + self.tag, gp, _swap_halves("grad_swap_" + self.tag, gp), self.c_arr)
        return _ScatterChipSums(sums_bf16)

    def end(self, got):
        mine = _shard_sum("grad_shard_sum_" + self.tag, self.sums, got, self.k_arr)
        return _unpack_shard(_join_halves("grad_join_" + self.tag, mine), self.group)


def _ffn_bwd(tag, h, gain, w_in, w_out, saved, dout, side, reduction):
    t = h.shape[0]
    tm = min(TM, t)
    n, gate, up, act = saved

    def compute(rows, weights, outs):
        d = rows[0][...].astype(BF16)
        for j in range(DFF // FFN_CHUNK):
            cols = slice(j * FFN_CHUNK, (j + 1) * FFN_CHUNK)
            da = 0.5 * _dot_nt(d, weights[0][cols, :])
            g, u = rows[1][:, cols].astype(F32), rows[2][:, cols].astype(F32)
            s = _sig(g)
            silu = g * s
            outs[0][:, cols] = (da * u * (s + silu * (1.0 - s))).astype(BF16)
            outs[1][:, cols] = (da * silu).astype(BF16)

    dgate, dup, *side_out = _rows_call(tag + "_dact", [dout, gate, up], [w_out], [(DFF, BF16)] * 2, compute,
                                       min(FFN_TM, t), side=side)
    dw_out = _mm_tn(tag + "_dw_out", act, dout, scale=0.5, tm=DFF // 2, tn=D)
    dw_g = _mm_tn(tag + "_dw_gate", n, dgate, tm=D, tn=DFF // 2)
    dw_u = _mm_tn(tag + "_dw_up", n, dup, tm=D, tn=DFF // 2)
    sending = reduction.begin({tag + "_w_in": jnp.concatenate([dw_g, dw_u], axis=1), tag + "_w_out": dw_out})

    def compute_dn(rows, weights, outs):
        w_ref = weights[0]
        dn = _dot_nt(rows[0][...], w_ref[:, 0:DFF]) + _dot_nt(rows[1][...], w_ref[:, DFF:2 * DFF])
        dx, dg = _rms_bwd_vals(rows[2][...], weights[1][...], dn)
        outs[0][...] = rows[3][...] + dx
        outs[1][...] += jnp.sum(dg, axis=0, keepdims=True)

    dh, dgain, got = _rows_call(tag + "_dn", [dgate, dup, h, dout], [w_in, gain], [(D, F32)], compute_dn,
                                min(FFN_TM, t), side=sending, sums=(D,), vmem_mb=58)
    return dh, dgain, side_out, got


def kernel(x, positions, ffn1_norm, ffn1_w_in, ffn1_w_out, mix_norm, w_in, hg_lb_table, hg_out_norm, w_hg_branch, mla_q_lora_norm, w_q_up, mla_kv_lora_norm, w_kv_up, q_head_norm, k_head_norm, w_mla_branch, w_merge, b_merge, w_out, ffn2_norm, ffn2_w_in, ffn2_w_out, final_norm, loss_target, m_ffn1_norm, m_ffn1_w_in, m_ffn1_w_out, m_mix_norm, m_w_in, m_hg_lb_table, m_hg_out_norm, m_w_hg_branch, m_mla_q_lora_norm, m_w_q_up, m_mla_kv_lora_norm, m_w_kv_up, m_q_head_norm, m_k_head_norm, m_w_mla_branch, m_w_merge, m_b_merge, m_w_out, m_ffn2_norm, m_ffn2_w_in, m_ffn2_w_out, m_final_norm, v_ffn1_norm, v_ffn1_w_in, v_ffn1_w_out, v_mix_norm, v_w_in, v_hg_lb_table, v_hg_out_norm, v_w_hg_branch, v_mla_q_lora_norm, v_w_q_up, v_mla_kv_lora_norm, v_w_kv_up, v_q_head_norm, v_k_head_norm, v_w_mla_branch, v_w_merge, v_b_merge, v_w_out, v_ffn2_norm, v_ffn2_w_in, v_ffn2_w_out, v_final_norm):
    a = dict(locals())
    w = {n: a[n] for n in WEIGHT_ORDER}
    mom = {n: a["m_" + n] for n in WEIGHT_ORDER}
    var = {n: a["v_" + n] for n in WEIGHT_ORDER}
    t = x.shape[1]
    tm = min(TM, t)
    xt = x.reshape(t, D)
    target = loss_target.reshape(t, D)
    pos = positions.reshape(t, 1)
    x_i, y_i, c_i = _place()
    k_idx = (2 * x_i + y_i).astype(jnp.int32)
    c_arr = c_i.astype(jnp.int32).reshape(1)
    k_arr = jnp.stack([k_idx, c_i.astype(jnp.int32)])

    group_first = _group(("ffn1_w_in", "ffn1_w_out"))
    group_mid = _group(("w_in", "w_hg_branch", "w_q_up", "w_kv_up", "w_mla_branch", "w_merge", "w_out"))
    group_last = _group(("ffn2_w_in", "ffn2_w_out"))
    gather_first = _GatherWeights(_pack([w[e[0]][0] for e in group_first], BF16))
    gather_mid = _GatherWeights(_pack([w[e[0]][0] for e in group_mid], BF16))
    gather_last = _GatherWeights(_pack([w[e[0]][0] for e in group_last], BF16))
    (got,) = gather_first.alone("gather_first")
    full = _unpack_full(gather_first.gathered(got, k_idx), group_first)
    (h1, u), ffn1_saved, (got,) = _ffn_fwd(
        "ffn1", xt, w["ffn1_norm"], full["ffn1_w_in"], full["ffn1_w_out"], w["mix_norm"], side=gather_mid)
    full.update(_unpack_full(gather_mid.gathered(got, k_idx), group_mid))
    w_in_full = full["w_in"]
    w_in_hg = w_in_full[:, :4 * D]
    w_in_mla = jnp.pad(w_in_full[:, 4 * D:], ((0, 0), (0, MLA_COLS - (4800 - 4 * D))))
    w_q_pad = jnp.pad(full["w_q_up"].reshape(Q_LORA, HEADS, QK), ((0, 0), (0, 0), (0, QKP - QK))).reshape(Q_LORA, HEADS * QKP)
    w_kv = full["w_kv_up"]
    gq = jnp.pad(w["q_head_norm"], ((0, 0), (0, QKP - QK)))
    gk = jnp.pad(w["k_head_norm"], ((0, 0), (0, QKP - QK)))

    ident = lambda accs, ex: (accs[0],)
    p_hg, got = _mm("in_hg", [_a_spec(u, tm)], [_b_nn(w_in_hg, 512)], [(0, 0)], ident, [], [F32], t, 4 * D, tm, 512,
                    side=gather_last)
    full.update(_unpack_full(gather_last.gathered(got, k_idx), group_last))
    (p_mla,) = _mm("in_mla", [_a_spec(u, tm)], [_b_nn(w_in_mla, MLA_COLS)], [(0, 0)], ident, [], [F32], t, MLA_COLS, tm, MLA_COLS)
    o_raw, hg_o, states = _hgrn_fwd(p_hg, w["hg_lb_table"], w["hg_out_norm"])
    (y_hg,) = _mm("hg_branch", [_a_spec(hg_o, tm)], [_b_nn(full["w_hg_branch"], 512)], [(0, 0)], ident, [], [BF16], t, D, tm, 512)
    cqn, ckvn = _lora_norm_fwd(p_mla, w["mla_q_lora_norm"], w["mla_kv_lora_norm"])
    (qf,) = _mm("q_up", [_a_spec(cqn, tm)], [_b_nn(w_q_pad, 512)], [(0, 0)], ident, [], [F32], t, HEADS * QKP, tm, 512)
    (kvf,) = _mm("kv_up", [_a_spec(ckvn, tm)], [_b_nn(w_kv, 512)], [(0, 0)], ident, [], [F32], t, HEADS * QKP, tm, 512)
    cos, sin = _rope_tables(pos)
    qh, kh, vh = _mla_prep_fwd(qf, kvf, p_mla, cos, sin, gq, gk)
    o_mla, lse = _flash_fwd(qh, kh, vh)
    (y_mla,) = _mm("mla_branch", [_a_spec(o_mla, tm)], [_b_nn(full["w_mla_branch"], 512)], [(0, 0)], ident, [], [BF16], t, D, tm, 512)

    def merge_epi(accs, ex):
        g_hg = _sig(accs[0] + ex[2])
        g_mla = _sig(accs[1] + ex[3])
        return g_hg * ex[0].astype(F32) + g_mla * ex[1].astype(F32), g_hg, g_mla

    w_merge_f = full["w_merge"]
    mix, g_hg, g_mla = _mm(
        "merge", [_a_spec(u, tm)], [_b_nn(w_merge_f, 512), _b_nn(w_merge_f, 512, D // 512)], [(0, 0), (0, 1)], merge_epi,
        [_e_tile(y_hg, tm, 512), _e_tile(y_mla, tm, 512), _e_row(w["b_merge"], 512), _e_row(w["b_merge"], 512, D // 512)],
        [BF16, BF16, BF16], t, D, tm, 512)
    (h2,) = _mm("out_proj", [_a_spec(mix, tm)], [_b_nn(full["w_out"], 512)], [(0, 0)],
                lambda accs, ex: (ex[0] + accs[0],), [_e_tile(h1, tm, 512)], [F32], t, D, tm, 512)
    (dh3, d_final_norm, loss_part), ffn2_saved, _ = _ffn_fwd(
        "ffn2", h2, w["ffn2_norm"], full["ffn2_w_in"], full["ffn2_w_out"], w["final_norm"], target=target)

    grads, small = {}, {}
    small["final_norm"] = d_final_norm
    reduce_last = _Reduction("last", group_last, c_arr, k_arr)
    reduce_mid = _Reduction("mid", group_mid, c_arr, k_arr)
    reduce_first = _Reduction("first", group_first, c_arr, k_arr)
    dh2, small["ffn2_norm"], _, got_last = _ffn_bwd(
        "ffn2", h2, w["ffn2_norm"], full["ffn2_w_in"], full["ffn2_w_out"], ffn2_saved, dh3, None, reduce_last)

    def dmix_epi(accs, ex):
        dm = accs[0]
        ghg, gml, yhg, yml = [e.astype(F32) for e in ex]
        return dm * ghg, dm * gml, dm * yhg * ghg * (1.0 - ghg), dm * yml * gml * (1.0 - gml)

    dy_hg, dy_mla, dpre_hg, dpre_mla = _mm(
        "d_mix", [_a_spec(dh2, tm)], [_b_nt(full["w_out"], 512)], [(0, 0)], dmix_epi,
        [_e_tile(g_hg, tm, 512), _e_tile(g_mla, tm, 512), _e_tile(y_hg, tm, 512), _e_tile(y_mla, tm, 512)],
        [BF16, BF16, BF16, BF16], t, D, tm, 512, trans_b=True)
    grads["w_out"] = _mm_tn("dw_out", mix, dh2)
    small["b_merge"] = jnp.concatenate([_colsum("db_hg", dpre_hg), _colsum("db_mla", dpre_mla)], axis=1)
    grads["w_merge"] = jnp.concatenate([_mm_tn("dw_merge_hg", u, dpre_hg), _mm_tn("dw_merge_mla", u, dpre_mla)], axis=1)
    grads["w_hg_branch"] = _mm_tn("dw_hg_branch", hg_o, dy_hg)
    grads["w_mla_branch"] = _mm_tn("dw_mla_branch", o_mla, dy_mla)
    (dho,) = _mm("d_hg_o", [_a_spec(dy_hg, tm)], [_b_nt(full["w_hg_branch"], 512)], [(0, 0)], ident, [], [BF16], t, D, tm, 512, trans_b=True)
    (do_mla,) = _mm("d_o_mla", [_a_spec(dy_mla, tm)], [_b_nt(full["w_mla_branch"], 512)], [(0, 0)], ident, [], [BF16], t, D, tm, 512, trans_b=True)

    dq_raw, df_raw, di_raw, dg_raw, small["hg_lb_table"], small["hg_out_norm"] = _hgrn_bwd(
        p_hg, w["hg_lb_table"], w["hg_out_norm"], o_raw, states, dho)
    dp_hg = [dq_raw, df_raw, di_raw, dg_raw]

    dqh, dkh, dvh = _flash_bwd(qh, kh, vh, lse, _attn_do(do_mla, o_mla))
    dqf, dkvf, dkpe, dgq, dgk = _mla_prep_bwd(qf, kvf, p_mla, cos, sin, gq, gk, dqh, dkh, dvh)
    small["q_head_norm"] = dgq[:, :QK]
    small["k_head_norm"] = dgk[:, :QK]
    dwq_pad = _mm_tn("dw_q_up", cqn, dqf, tm=Q_LORA, tn=1024)
    grads["w_q_up"] = dwq_pad.reshape(Q_LORA, HEADS, QKP)[:, :, :QK].reshape(Q_LORA, HEADS * QK)
    grads["w_kv_up"] = _mm_tn("dw_kv_up", ckvn, dkvf, tm=KV_LORA, tn=1024)
    (dcqn,) = _mm("d_cq", [_a_spec(dqf, tm)], [_b_nt(w_q_pad, Q_LORA)], [(0, 0)], ident, [], [F32], t, Q_LORA, tm, Q_LORA, trans_b=True)
    (dckvn,) = _mm("d_ckv", [_a_spec(dkvf, tm)], [_b_nt(w_kv, KV_LORA)], [(0, 0)], ident, [], [F32], t, KV_LORA, tm, KV_LORA, trans_b=True)
    dp_mla, small["mla_q_lora_norm"], small["mla_kv_lora_norm"] = _lora_norm_bwd(
        p_mla, w["mla_q_lora_norm"], w["mla_kv_lora_norm"], dcqn, dckvn, dkpe)

    dw_in_hg = [_mm_tn("dw_in_hg%d" % k, u, dp_hg[k]) for k in range(4)]
    dw_in_mla = _mm_tn("dw_in_mla", u, dp_mla, tn=MLA_COLS)
    grads["w_in"] = jnp.concatenate(dw_in_hg + [dw_in_mla[:, :4800 - 4 * D]], axis=1)
    tm_du = min(TM // 2, t)
    du, got_mid = _mm(
        "d_u",
        [_a_spec(dpre_hg, tm_du), _a_spec(dpre_mla, tm_du)] + [_a_spec(d, tm_du) for d in dp_hg] + [_a_spec(dp_mla, tm_du)],
        [_b_nt(w_merge_f, 512, D, 0), _b_nt(w_merge_f, 512, D, 1)]
        + [_b_nt(w_in_hg, 512, D, k) for k in range(4)] + [_b_nt(w_in_mla, 512)],
        [(k, k) for k in range(7)],
        lambda accs, ex: (functools.reduce(lambda p, q: p + q, accs),), [], [F32], t, D, tm_du, 512, trans_b=True,
        side=reduce_mid.begin(grads))
    dh1, small["mix_norm"] = _rms_bwd("mix_dnorm", h1, w["mix_norm"], du, dh2)
    dx, small["ffn1_norm"], _, got_first = _ffn_bwd(
        "ffn1", xt, w["ffn1_norm"], full["ffn1_w_in"], full["ffn1_w_out"], ffn1_saved, dh1, None, reduce_first)

    g_shard = {**reduce_last.end(got_last), **reduce_mid.end(got_mid), **reduce_first.end(got_first)}
    small_sum = _all_reduce_small(_pack_small([small[n] for n, _ in SMALL] + [loss_part])).reshape(-1)
    g_small, at = {}, 0
    for n, shape in SMALL:
        size = shape[0] * shape[1]
        g_small[n] = small_sum[at:at + size].reshape(shape)
        at += size
    loss = small_sum[at]

    g_out, d_out, m_out, v_out = {}, {}, {}, {}
    for n in WEIGHT_ORDER:
        shape = w[n].shape
        g = g_shard[n] if n in g_shard else g_small[n]
        two = g.shape
        d_, m_, v_ = _adamw("adamw_" + n, w[n].reshape(two), g, mom[n].reshape(two), var[n].reshape(two))
        g_out[n], d_out[n], m_out[n], v_out[n] = g.reshape(shape), d_.reshape(shape), m_.reshape(shape), v_.reshape(shape)

    return (loss, dx.reshape(x.shape), *[g_out[n] for n in WEIGHT_ORDER], *[d_out[n] for n in WEIGHT_ORDER],
            *[m_out[n] for n in WEIGHT_ORDER], *[v_out[n] for n in WEIGHT_ORDER])
```

```python
import functools

import numpy as np
import jax
import jax.numpy as jnp
from jax import lax
from jax.experimental import pallas as pl
from jax.experimental.pallas import tpu as pltpu

F32 = jnp.float32
BF16 = jnp.bfloat16
MESH = pl.DeviceIdType.MESH

D = 1024
DFF = 2816
HEADS = 8
HK = 128
CHUNK = 64
ROPE = 64
QK = 192
QKP = 256
Q_LORA = 384
KV_LORA = 256
MLA_COLS = 768
EPS = 1e-6
ROPE_THETA = 10000.0
SCALE = QK ** -0.5
LOG2E = 1.4426950408889634
LN2 = 0.6931471805599453
NEG = -1e30
EXP_CLAMP = 80.0

ADAM_LR = 0.001
ADAM_B1 = 0.9
ADAM_B2 = 0.999
ADAM_EPS = 1e-08
ADAM_WD = 0.01
ADAM_STEP = 10

PACK_W = 1024
ADD_ROWS = 352
PACK_ALIGN = 2 * ADD_ROWS

TM = 1024
FFN_TM = 512
FFN_CHUNK = 256
TQ = 1024
SUBQ = 256
HG_BT = 512
HG_HPB = 4
TT = 1024
ROW_TM = 256

VMEM_MB = 48

BIG = (
    ("ffn1_w_in", D, 2 * DFF, 1),
    ("ffn1_w_out", DFF, D, 0),
    ("w_in", D, 4800, 1),
    ("w_hg_branch", D, D, 0),
    ("w_q_up", Q_LORA, HEADS * QK, 1),
    ("w_kv_up", KV_LORA, HEADS * 2 * HK, 1),
    ("w_mla_branch", D, D, 0),
    ("w_merge", D, 2 * D, 1),
    ("w_out", D, D, 0),
    ("ffn2_w_in", D, 2 * DFF, 1),
    ("ffn2_w_out", DFF, D, 0),
)
SMALL = (
    ("ffn1_norm", (1, D)),
    ("mix_norm", (1, D)),
    ("hg_lb_table", (2, D)),
    ("hg_out_norm", (1, HK)),
    ("mla_q_lora_norm", (1, Q_LORA)),
    ("mla_kv_lora_norm", (1, KV_LORA)),
    ("q_head_norm", (1, QK)),
    ("k_head_norm", (1, QK)),
    ("b_merge", (1, 2 * D)),
    ("ffn2_norm", (1, D)),
    ("final_norm", (1, D)),
)
WEIGHT_ORDER = ("ffn1_norm", "ffn1_w_in", "ffn1_w_out", "mix_norm", "w_in", "hg_lb_table", "hg_out_norm",
                "w_hg_branch", "mla_q_lora_norm", "w_q_up", "mla_kv_lora_norm", "w_kv_up", "q_head_norm",
                "k_head_norm", "w_mla_branch", "w_merge", "b_merge", "w_out", "ffn2_norm", "ffn2_w_in",
                "ffn2_w_out", "final_norm")


def _call(body, **kw):
    return pl.pallas_call(body, **kw)


def _cp(vmem_mb=VMEM_MB):
    return pltpu.CompilerParams(vmem_limit_bytes=vmem_mb << 20)


def _dot(a, b):
    return lax.dot_general(a, b, (((1,), (0,)), ((), ())), preferred_element_type=F32)


def _dot_nt(a, b):
    return lax.dot_general(a, b, (((1,), (1,)), ((), ())), preferred_element_type=F32)


def _dot_tn(a, b):
    return lax.dot_general(a, b, (((0,), (0,)), ((), ())), preferred_element_type=F32)


def _sig(x):
    return jax.nn.sigmoid(x)


def _silu(x):
    return x * _sig(x)


def _dsilu(x):
    s = _sig(x)
    return s * (1.0 + x * (1.0 - s))


def _a_spec(arr, tm, kblk=None, kidx=0):
    kb = arr.shape[1] if kblk is None else kblk
    return arr, pl.BlockSpec((tm, kb), lambda i, j, kidx=kidx: (i, kidx))


def _b_nn(arr, tn, off=0):
    return arr, pl.BlockSpec((arr.shape[0], tn), lambda i, j, off=off: (0, j + off))


def _b_nt(arr, tn, kblk=None, kidx=0):
    kb = arr.shape[1] if kblk is None else kblk
    return arr, pl.BlockSpec((tn, kb), lambda i, j, kidx=kidx: (j, kidx))


def _e_tile(arr, tm, tn, off=0):
    return arr, pl.BlockSpec((tm, tn), lambda i, j, off=off: (i, j + off))


def _e_row(arr, tn, off=0):
    return arr, pl.BlockSpec((1, tn), lambda i, j, off=off: (0, j + off))


def _mm(name, As, Bs, dots, epi, extras, out_dtypes, m, n, tm, tn, trans_b=False, side=None):
    na, nb, ne, no = len(As), len(Bs), len(extras), len(out_dtypes)
    ni, nj = m // tm, n // tn
    s_in = len(side.inputs) if side else 0
    s_out = len(side.out_shapes) if side else 0

    def body(*refs):
        a_refs = refs[:na]
        b_refs = refs[na:na + nb]
        e_refs = refs[na + nb:na + nb + ne]
        at = na + nb + ne
        side_refs = refs[at:at + s_in]
        o_refs = refs[at + s_in:at + s_in + no]
        side_refs = list(side_refs) + list(refs[at + s_in + no:])
        if side:
            i, j = pl.program_id(0), pl.program_id(1)

            @pl.when(jnp.logical_and(i == 0, j == 0))
            def _():
                side.start(*side_refs)

        a_vals = [r[...].astype(BF16) for r in a_refs]
        accs = []
        for ai, bi in dots:
            b = b_refs[bi][...]
            accs.append(_dot_nt(a_vals[ai], b) if trans_b else _dot(a_vals[ai], b))
        outs = epi(accs, [r[...] for r in e_refs])
        for o_ref, o in zip(o_refs, outs):
            o_ref[...] = o.astype(o_ref.dtype)
        if side:
            @pl.when(jnp.logical_and(i == ni - 1, j == nj - 1))
            def _():
                side.finish(*side_refs)

    ops = list(As) + list(Bs) + list(extras)
    anywhere = pl.BlockSpec(memory_space=pl.ANY)
    res = _call(
        body, name=name,
        grid=(ni, nj),
        in_specs=[s for _, s in ops] + [anywhere] * s_in,
        out_specs=[pl.BlockSpec((tm, tn), lambda i, j: (i, j)) for _ in out_dtypes] + [anywhere] * s_out,
        out_shape=[jax.ShapeDtypeStruct((m, n), dt) for dt in out_dtypes] + (list(side.out_shapes) if side else []),
        scratch_shapes=list(side.scratch) if side else [],
        compiler_params=_cp(),
    )(*[a for a, _ in ops], *(side.inputs if side else []))
    return res


def _rows_call(name, rows, weights, outs, compute, tm, side=None, sums=(), vmem_mb=VMEM_MB):
    t = rows[0].shape[0]
    nr, nw, no = len(rows), len(weights), len(outs) + len(sums)
    ni = t // tm
    s_in = len(side.inputs) if side else 0
    s_out = len(side.out_shapes) if side else 0

    def body(*refs):
        at = nr + nw
        side_refs = list(refs[at:at + s_in]) + list(refs[at + s_in + no:])
        if side:
            @pl.when(pl.program_id(0) == 0)
            def _():
                side.start(*side_refs)

        out_refs = refs[at + s_in:at + s_in + no]
        if sums:
            @pl.when(pl.program_id(0) == 0)
            def _():
                for r in out_refs[len(outs):]:
                    r[...] = jnp.zeros_like(r)

        compute(refs[:nr], refs[nr:at], out_refs)
        if side:
            @pl.when(pl.program_id(0) == ni - 1)
            def _():
                side.finish(*side_refs)

    anywhere = pl.BlockSpec(memory_space=pl.ANY)
    return _call(
        body, name=name, grid=(ni,),
        in_specs=[pl.BlockSpec((tm, r.shape[1]), lambda i: (i, 0)) for r in rows]
        + [pl.BlockSpec(wt.shape, lambda i: (0, 0)) for wt in weights] + [anywhere] * s_in,
        out_specs=[pl.BlockSpec((tm, width), lambda i: (i, 0)) for width, _ in outs]
        + [pl.BlockSpec((1, width), lambda i: (0, 0)) for width in sums] + [anywhere] * s_out,
        out_shape=[jax.ShapeDtypeStruct((t, width), dt) for width, dt in outs]
        + [jax.ShapeDtypeStruct((1, width), F32) for width in sums] + (list(side.out_shapes) if side else []),
        scratch_shapes=list(side.scratch) if side else [],
        compiler_params=_cp(vmem_mb),
    )(*rows, *weights, *(side.inputs if side else []))


def _mm_tn(name, a, b, scale=1.0, tm=1024, tn=1024):
    t, m = a.shape
    n = b.shape[1]
    tm, tn, tt = min(tm, m), min(tn, n), min(TT, t)
    nk = t // tt

    def body(a_ref, b_ref, o_ref):
        k = pl.program_id(2)

        @pl.when(k == 0)
        def _():
            o_ref[...] = jnp.zeros_like(o_ref)

        o_ref[...] += _dot_tn(a_ref[...].astype(BF16), b_ref[...].astype(BF16))
        if scale != 1.0:
            @pl.when(k == nk - 1)
            def _():
                o_ref[...] = o_ref[...] * scale

    return _call(
        body, name=name,
        grid=(m // tm, n // tn, nk),
        in_specs=[pl.BlockSpec((tt, tm), lambda i, j, k: (k, i)), pl.BlockSpec((tt, tn), lambda i, j, k: (k, j))],
        out_specs=pl.BlockSpec((tm, tn), lambda i, j, k: (i, j)),
        out_shape=jax.ShapeDtypeStruct((m, n), F32),
        compiler_params=_cp(),
    )(a, b)


def _rms_bwd_vals(xv, g, dn):
    r = lax.rsqrt(jnp.mean(xv * xv, axis=-1, keepdims=True) + EPS)
    xh = xv * r
    dxh = dn * g
    c = jnp.mean(dxh * xh, axis=-1, keepdims=True)
    return r * (dxh - xh * c), dn * xh


def _rms_bwd(name, x, gain, dn, dres):
    t, d = x.shape
    tm = min(ROW_TM, t)

    def body(x_ref, g_ref, dn_ref, dr_ref, dx_ref, dg_ref):
        @pl.when(pl.program_id(0) == 0)
        def _():
            dg_ref[...] = jnp.zeros_like(dg_ref)

        dx, dg = _rms_bwd_vals(x_ref[...], g_ref[...], dn_ref[...].astype(F32))
        dx_ref[...] = dr_ref[...] + dx
        dg_ref[...] += jnp.sum(dg, axis=0, keepdims=True)

    row = pl.BlockSpec((tm, d), lambda i: (i, 0))
    one = pl.BlockSpec((1, d), lambda i: (0, 0))
    return _call(
        body, name=name, grid=(t // tm,),
        in_specs=[row, one, row, row],
        out_specs=[row, one],
        out_shape=[jax.ShapeDtypeStruct((t, d), F32), jax.ShapeDtypeStruct((1, d), F32)],
        compiler_params=_cp(),
    )(x, gain, dn, dres)


def _colsum(name, x):
    t, n = x.shape
    tm = min(TM, t)

    def body(x_ref, o_ref):
        @pl.when(pl.program_id(0) == 0)
        def _():
            o_ref[...] = jnp.zeros_like(o_ref)

        o_ref[...] += jnp.sum(x_ref[...].astype(F32), axis=0, keepdims=True)

    return _call(
        body, name=name, grid=(t // tm,),
        in_specs=[pl.BlockSpec((tm, n), lambda i: (i, 0))],
        out_specs=pl.BlockSpec((1, n), lambda i: (0, 0)),
        out_shape=jax.ShapeDtypeStruct((1, n), F32),
        compiler_params=_cp(),
    )(x)


def _lora_norm_fwd(p_mla, gq, gkv):
    t = p_mla.shape[0]
    tm = min(ROW_TM, t)

    def body(p_ref, gq_ref, gkv_ref, q_ref, kv_ref):
        cq = p_ref[:, 0:Q_LORA]
        ckv = p_ref[:, Q_LORA:Q_LORA + KV_LORA]
        rq = lax.rsqrt(jnp.mean(cq * cq, axis=-1, keepdims=True) + EPS)
        rkv = lax.rsqrt(jnp.mean(ckv * ckv, axis=-1, keepdims=True) + EPS)
        q_ref[...] = (cq * rq * gq_ref[...]).astype(BF16)
        kv_ref[...] = (ckv * rkv * gkv_ref[...]).astype(BF16)

    return _call(
        body, name="lora_norm_fwd", grid=(t // tm,),
        in_specs=[pl.BlockSpec((tm, MLA_COLS), lambda i: (i, 0)),
                  pl.BlockSpec((1, Q_LORA), lambda i: (0, 0)), pl.BlockSpec((1, KV_LORA), lambda i: (0, 0))],
        out_specs=[pl.BlockSpec((tm, Q_LORA), lambda i: (i, 0)), pl.BlockSpec((tm, KV_LORA), lambda i: (i, 0))],
        out_shape=[jax.ShapeDtypeStruct((t, Q_LORA), BF16), jax.ShapeDtypeStruct((t, KV_LORA), BF16)],
        compiler_params=_cp(),
    )(p_mla, gq, gkv)


def _lora_norm_bwd(p_mla, gq, gkv, dcqn, dckvn, dkpe):
    t = p_mla.shape[0]
    tm = min(ROW_TM, t)

    def body(p_ref, gq_ref, gkv_ref, dq_ref, dkv_ref, dkpe_ref, dp_ref, dgq_ref, dgkv_ref):
        @pl.when(pl.program_id(0) == 0)
        def _():
            dgq_ref[...] = jnp.zeros_like(dgq_ref)
            dgkv_ref[...] = jnp.zeros_like(dgkv_ref)

        dcq, dgq = _rms_bwd_vals(p_ref[:, 0:Q_LORA], gq_ref[...], dq_ref[...])
        dckv, dgkv = _rms_bwd_vals(p_ref[:, Q_LORA:Q_LORA + KV_LORA], gkv_ref[...], dkv_ref[...])
        dp_ref[:, 0:Q_LORA] = dcq.astype(BF16)
        dp_ref[:, Q_LORA:Q_LORA + KV_LORA] = dckv.astype(BF16)
        dp_ref[:, Q_LORA + KV_LORA:MLA_COLS] = dkpe_ref[...].astype(BF16)
        dgq_ref[...] += jnp.sum(dgq, axis=0, keepdims=True)
        dgkv_ref[...] += jnp.sum(dgkv, axis=0, keepdims=True)

    return _call(
        body, name="lora_norm_bwd", grid=(t // tm,),
        in_specs=[pl.BlockSpec((tm, MLA_COLS), lambda i: (i, 0)),
                  pl.BlockSpec((1, Q_LORA), lambda i: (0, 0)), pl.BlockSpec((1, KV_LORA), lambda i: (0, 0)),
                  pl.BlockSpec((tm, Q_LORA), lambda i: (i, 0)), pl.BlockSpec((tm, KV_LORA), lambda i: (i, 0)),
                  pl.BlockSpec((tm, HK), lambda i: (i, 0))],
        out_specs=[pl.BlockSpec((tm, MLA_COLS), lambda i: (i, 0)),
                   pl.BlockSpec((1, Q_LORA), lambda i: (0, 0)), pl.BlockSpec((1, KV_LORA), lambda i: (0, 0))],
        out_shape=[jax.ShapeDtypeStruct((t, MLA_COLS), BF16), jax.ShapeDtypeStruct((1, Q_LORA), F32),
                   jax.ShapeDtypeStruct((1, KV_LORA), F32)],
        compiler_params=_cp(),
    )(p_mla, gq, gkv, dcqn, dckvn, dkpe)


def _cumsum_rows(x, row):
    for s in (1, 2, 4, 8, 16, 32):
        x = x + jnp.where(row >= s, pltpu.roll(x, s, 0), 0.0)
    return x


def _rcumsum_rows(x, row):
    for s in (1, 2, 4, 8, 16, 32):
        x = x + jnp.where(row < CHUNK - s, pltpu.roll(x, CHUNK - s, 0), 0.0)
    return x


def _hg_gates(qr, z, lb, row):
    q = _silu(qr)
    sg = _sig(z)
    f = lb + (1.0 - lb) * sg
    lf = jnp.log(f)
    k = (1.0 - lb) * (1.0 - sg)
    cum = _cumsum_rows(lf, row)
    mid = jnp.sum(jnp.where(row < CHUNK // 2, lf, 0.0), axis=0, keepdims=True)
    last = jnp.sum(lf, axis=0, keepdims=True)
    e_q = jnp.exp(jnp.minimum(cum - mid, EXP_CLAMP))
    e_k = jnp.exp(jnp.minimum(mid - cum, EXP_CLAMP))
    e_a = jnp.exp(cum)
    e_l = jnp.exp(last - cum)
    return q, sg, f, k, last, e_q, e_k, e_a, e_l


def _hgrn_fwd(p_hg, tab, gain):
    t = p_hg.shape[0]
    bt = min(HG_BT, t)
    nb, nc = t // bt, bt // CHUNK

    hpb = HG_HPB
    wide = hpb * HK

    def body(q_ref, f_ref, i_ref, g_ref, tab_ref, gain_ref, o_ref, ho_ref, st_ref, state):
        @pl.when(pl.program_id(1) == 0)
        def _():
            state[...] = jnp.zeros_like(state)

        row = lax.broadcasted_iota(jnp.int32, (CHUNK, HK), 0)
        tril = lax.broadcasted_iota(jnp.int32, (CHUNK, CHUNK), 0) >= lax.broadcasted_iota(jnp.int32, (CHUNK, CHUNK), 1)
        gain_v = gain_ref[...]

        def chunk(c, carry):
            sl = pl.ds(pl.multiple_of(c * CHUNK, CHUNK), CHUNK)
            for hh in range(hpb):
                ln = slice(hh * HK, (hh + 1) * HK)
                lb = _sig(tab_ref[0:1, ln] - tab_ref[1:2, ln])
                v = i_ref[sl, ln].astype(BF16)
                q, _, _, k, last, e_q, e_k, e_a, e_l = _hg_gates(q_ref[sl, ln], f_ref[sl, ln], lb, row)
                st = state[hh]
                st_ref[hh, c] = st
                p = jnp.where(tril, _dot_nt((q * e_q).astype(BF16), (k * e_k).astype(BF16)), 0.0)
                o = _dot(p.astype(BF16), v) + _dot_nt((q * e_a).astype(BF16), st.astype(BF16))
                state[hh] = jnp.exp(last) * st + _dot_tn(v, (k * e_l).astype(BF16))
                o_ref[sl, ln] = o
                r = lax.rsqrt(jnp.mean(o * o, axis=-1, keepdims=True) + EPS)
                ho_ref[sl, ln] = (o * r * gain_v * _silu(g_ref[sl, ln])).astype(BF16)
            return carry

        lax.fori_loop(0, nc, chunk, 0)

    def col(k):
        return pl.BlockSpec((bt, wide), lambda h, j, k=k: (j, k * (HEADS // hpb) + h))

    return _call(
        body, name="hgrn_fwd", grid=(HEADS // hpb, nb),
        in_specs=[col(0), col(1), col(2), col(3),
                  pl.BlockSpec((2, wide), lambda h, j: (0, h)), pl.BlockSpec((1, HK), lambda h, j: (0, 0))],
        out_specs=[pl.BlockSpec((bt, wide), lambda h, j: (j, h)), pl.BlockSpec((bt, wide), lambda h, j: (j, h)),
                   pl.BlockSpec((hpb, nc, HK, HK), lambda h, j: (h, j, 0, 0))],
        out_shape=[jax.ShapeDtypeStruct((t, D), F32), jax.ShapeDtypeStruct((t, D), BF16),
                   jax.ShapeDtypeStruct((HEADS, t // CHUNK, HK, HK), F32)],
        scratch_shapes=[pltpu.VMEM((hpb, HK, HK), F32)],
        compiler_params=_cp(),
    )(p_hg, p_hg, p_hg, p_hg, tab, gain)


def _hgrn_bwd(p_hg, tab, gain, o_raw, states, dho):
    t = p_hg.shape[0]
    bt = min(HG_BT, t)
    nb, nc = t // bt, bt // CHUNK
    hpb = HG_HPB
    wide = hpb * HK

    def body(q_ref, f_ref, i_ref, g_ref, tab_ref, gain_ref, o_ref, st_ref, dho_ref,
             dq_ref, df_ref, di_ref, dg_ref, dtab_ref, dgain_ref, dstate, dlb):
        h, j = pl.program_id(0), pl.program_id(1)

        @pl.when(jnp.logical_and(h == 0, j == 0))
        def _():
            dgain_ref[...] = jnp.zeros_like(dgain_ref)

        @pl.when(j == 0)
        def _():
            dstate[...] = jnp.zeros_like(dstate)
            dlb[...] = jnp.zeros_like(dlb)

        row = lax.broadcasted_iota(jnp.int32, (CHUNK, HK), 0)
        tril = lax.broadcasted_iota(jnp.int32, (CHUNK, CHUNK), 0) >= lax.broadcasted_iota(jnp.int32, (CHUNK, CHUNK), 1)
        gain_v = gain_ref[...]

        def chunk(cc, carry):
            c = nc - 1 - cc
            sl = pl.ds(pl.multiple_of(c * CHUNK, CHUNK), CHUNK)
            dgain = jnp.zeros((1, HK), F32)
            for hh in range(hpb):
                ln = slice(hh * HK, (hh + 1) * HK)
                lb = _sig(tab_ref[0:1, ln] - tab_ref[1:2, ln])
                qr = q_ref[sl, ln]
                v = i_ref[sl, ln].astype(BF16)
                gr = g_ref[sl, ln]
                q, sg, f, k, last, e_q, e_k, e_a, e_l = _hg_gates(qr, f_ref[sl, ln], lb, row)
                o = o_ref[sl, ln]
                r = lax.rsqrt(jnp.mean(o * o, axis=-1, keepdims=True) + EPS)
                oh = o * r
                dh = dho_ref[sl, ln].astype(F32)
                dnorm = dh * _silu(gr)
                dg_ref[sl, ln] = (dh * oh * gain_v * _dsilu(gr)).astype(BF16)
                dgain = dgain + jnp.sum(dnorm * oh, axis=0, keepdims=True)
                dxh = dnorm * gain_v
                do = (r * (dxh - oh * jnp.mean(dxh * oh, axis=-1, keepdims=True))).astype(BF16)
                st0 = st_ref[hh, c]
                st0_b = st0.astype(BF16)
                ds1 = dstate[hh]
                ds1_b = ds1.astype(BF16)
                qt = (q * e_q).astype(BF16)
                kt = (k * e_k).astype(BF16)
                qd = (q * e_a).astype(BF16)
                kd = (k * e_l).astype(BF16)
                p = jnp.where(tril, _dot_nt(qt, kt), 0.0).astype(BF16)
                dp = jnp.where(tril, _dot_nt(do, v), 0.0).astype(BF16)
                dv = _dot_tn(p, do) + _dot_nt(kd, ds1_b)
                dqt = _dot(dp, kt)
                dkt = _dot_tn(dp, qt)
                dq_inter = _dot(do, st0_b) * e_a
                dk_inter = _dot(v, ds1_b) * e_l
                dq = dqt * e_q + dq_inter
                dk = dkt * e_k + dk_inter
                e_last = jnp.exp(last)
                dstate[hh] = _dot_tn(do, qd) + e_last * ds1
                dlast = (jnp.sum(k * dk_inter, axis=0, keepdims=True)
                         + e_last * jnp.sum(ds1 * st0, axis=0, keepdims=True))
                da = (qt.astype(F32) * dqt - kt.astype(F32) * dkt + q * dq_inter - k * dk_inter
                      + jnp.where(row == CHUNK - 1, dlast, 0.0))
                dlf = _rcumsum_rows(da, row)
                dfv = dlf / f - dk
                df_ref[sl, ln] = (dfv * (1.0 - lb) * sg * (1.0 - sg)).astype(BF16)
                dlb[:, ln] += jnp.sum(dfv * (1.0 - sg), axis=0, keepdims=True)
                dq_ref[sl, ln] = (dq * _dsilu(qr)).astype(BF16)
                di_ref[sl, ln] = dv.astype(BF16)
            dgain_ref[...] += dgain
            return carry

        lax.fori_loop(0, nc, chunk, 0)

        @pl.when(j == nb - 1)
        def _():
            lb = _sig(tab_ref[0:1, :] - tab_ref[1:2, :])
            d0 = dlb[...] * lb * (1.0 - lb)
            dtab_ref[0:1, :] = d0
            dtab_ref[1:2, :] = -d0

    def col(k):
        return pl.BlockSpec((bt, wide), lambda h, j, k=k: (nb - 1 - j, k * (HEADS // hpb) + h))

    tok = pl.BlockSpec((bt, wide), lambda h, j: (nb - 1 - j, h))
    return _call(
        body, name="hgrn_bwd", grid=(HEADS // hpb, nb),
        in_specs=[col(0), col(1), col(2), col(3),
                  pl.BlockSpec((2, wide), lambda h, j: (0, h)), pl.BlockSpec((1, HK), lambda h, j: (0, 0)),
                  tok, pl.BlockSpec((hpb, nc, HK, HK), lambda h, j: (h, nb - 1 - j, 0, 0)), tok],
        out_specs=[tok, tok, tok, tok,
                   pl.BlockSpec((2, wide), lambda h, j: (0, h)), pl.BlockSpec((1, HK), lambda h, j: (0, 0))],
        out_shape=[jax.ShapeDtypeStruct((t, D), BF16)] * 4
        + [jax.ShapeDtypeStruct((2, D), F32), jax.ShapeDtypeStruct((1, HK), F32)],
        scratch_shapes=[pltpu.VMEM((hpb, HK, HK), F32), pltpu.VMEM((1, wide), F32)],
        compiler_params=_cp(),
    )(p_hg, p_hg, p_hg, p_hg, tab, gain, o_raw, states, dho)


def _rope_tables(pos):
    t = pos.shape[0]
    tm = min(ROW_TM, t)
    inv = np.zeros((1, HK), np.float32)
    freq = (ROPE_THETA ** (-np.arange(0, ROPE, 2, dtype=np.float32) / ROPE)).astype(np.float32)
    inv[0, 0:ROPE // 2] = freq
    inv[0, ROPE // 2:ROPE] = freq
    sign = np.zeros((1, HK), np.float32)
    sign[0, 0:ROPE // 2] = -1.0
    sign[0, ROPE // 2:ROPE] = 1.0

    def body(pos_ref, inv_ref, sign_ref, cos_ref, sin_ref):
        ang = pos_ref[...].astype(F32) * inv_ref[...]
        cos_ref[...] = jnp.cos(ang)
        sin_ref[...] = jnp.sin(ang) * sign_ref[...]

    one = pl.BlockSpec((1, HK), lambda i: (0, 0))
    row = pl.BlockSpec((tm, HK), lambda i: (i, 0))
    return _call(
        body, name="rope_tables", grid=(t // tm,),
        in_specs=[pl.BlockSpec((tm, 1), lambda i: (i, 0)), one, one],
        out_specs=[row, row],
        out_shape=[jax.ShapeDtypeStruct((t, HK), F32)] * 2,
        compiler_params=_cp(),
    )(pos, jnp.asarray(inv), jnp.asarray(sign))


def _rope(x, cos, sin_signed):
    r = lax.broadcasted_iota(jnp.int32, (HK, HK), 0)
    c = lax.broadcasted_iota(jnp.int32, (HK, HK), 1)
    half = ROPE // 2
    swap = jnp.logical_or(jnp.logical_and(c < half, r == c + half),
                          jnp.logical_and(jnp.logical_and(c >= half, c < ROPE), r == c - half))
    return x * cos + _dot_split(x, swap.astype(BF16)) * sin_signed


def _dot_split(x, m):
    hi = x.astype(BF16)
    lo = (x - hi.astype(F32)).astype(BF16)
    return _dot(hi, m) + _dot(lo, m)


def _lane_sum(x):
    return _dot_split(x, jnp.ones((HK, HK), BF16))


def _head_norm(xn, xr):
    r = lax.rsqrt(_lane_sum(xn * xn + xr * xr) * (1.0 / QK) + EPS)
    return xn * r, xr * r, r


def _head_norm_bwd(xn, xr, g_n, g_r, dn, dr):
    hn, hr, r = _head_norm(xn, xr)
    dxn, dxr = dn * g_n, dr * g_r
    c = _lane_sum(dxn * hn + dxr * hr) * (1.0 / QK)
    return r * (dxn - hn * c), r * (dxr - hr * c), dn * hn, dr * hr


def _mla_prep_fwd(qf, kv, p_mla, cos, sin, gq, gk):
    t = qf.shape[0]
    tm = min(ROW_TM, t)

    def body(qf_ref, kv_ref, kpe_ref, cos_ref, sin_ref, gq_ref, gk_ref, q_ref, k_ref, v_ref):
        cos_v, sin_v = cos_ref[...], sin_ref[...]
        kpe = kpe_ref[...]
        for h in range(HEADS):
            lo, mid, hi = h * QKP, h * QKP + HK, (h + 1) * QKP
            qn, qr, _ = _head_norm(qf_ref[:, lo:mid], qf_ref[:, mid:hi])
            q_ref[h, :, 0:HK] = (qn * gq_ref[:, 0:HK] * (SCALE * LOG2E)).astype(BF16)
            q_ref[h, :, HK:QKP] = (_rope(qr * gq_ref[:, HK:QKP], cos_v, sin_v) * (SCALE * LOG2E)).astype(BF16)
            kn, kr, _ = _head_norm(kv_ref[:, lo:mid], kpe)
            k_ref[h, :, 0:HK] = (kn * gk_ref[:, 0:HK]).astype(BF16)
            k_ref[h, :, HK:QKP] = _rope(kr * gk_ref[:, HK:QKP], cos_v, sin_v).astype(BF16)
            v_ref[h, :, 0:HK] = kv_ref[:, mid:hi].astype(BF16)
            v_ref[h, :, HK:QKP] = jnp.full((tm, HK), -1.0, BF16)

    head = pl.BlockSpec((tm, HEADS * QKP), lambda i: (i, 0))
    tok = pl.BlockSpec((tm, HK), lambda i: (i, 0))
    gain = pl.BlockSpec((1, QKP), lambda i: (0, 0))
    return _call(
        body, name="mla_prep_fwd", grid=(t // tm,),
        in_specs=[head, head, pl.BlockSpec((tm, HK), lambda i: (i, MLA_COLS // HK - 1)), tok, tok, gain, gain],
        out_specs=[pl.BlockSpec((HEADS, tm, QKP), lambda i: (0, i, 0)),
                   pl.BlockSpec((HEADS, tm, QKP), lambda i: (0, i, 0)),
                   pl.BlockSpec((HEADS, tm, QKP), lambda i: (0, i, 0))],
        out_shape=[jax.ShapeDtypeStruct((HEADS, t, QKP), BF16), jax.ShapeDtypeStruct((HEADS, t, QKP), BF16),
                   jax.ShapeDtypeStruct((HEADS, t, QKP), BF16)],
        compiler_params=_cp(),
    )(qf, kv, p_mla, cos, sin, gq, gk)


def _mla_prep_bwd(qf, kv, p_mla, cos, sin, gq, gk, dq, dk, dv):
    t = qf.shape[0]
    tm = min(ROW_TM, t)

    def body(qf_ref, kv_ref, kpe_ref, cos_ref, sin_ref, gq_ref, gk_ref, dq_ref, dk_ref, dv_ref,
             dqf_ref, dkv_ref, dkpe_ref, dgq_ref, dgk_ref):
        @pl.when(pl.program_id(0) == 0)
        def _():
            dgq_ref[...] = jnp.zeros_like(dgq_ref)
            dgk_ref[...] = jnp.zeros_like(dgk_ref)

        cos_v, sin_v = cos_ref[...], -sin_ref[...]
        kpe = kpe_ref[...]
        gqn, gqr, gkn, gkr = gq_ref[:, 0:HK], gq_ref[:, HK:QKP], gk_ref[:, 0:HK], gk_ref[:, HK:QKP]
        dkpe = jnp.zeros((tm, HK), F32)
        dgq_n, dgq_r, dgk_n, dgk_r = [jnp.zeros((1, HK), F32) for _ in range(4)]
        for h in range(HEADS):
            lo, mid, hi = h * QKP, h * QKP + HK, (h + 1) * QKP
            dqn = dq_ref[h, :, 0:HK].astype(F32) * SCALE
            dqr = _rope(dq_ref[h, :, HK:QKP].astype(F32), cos_v, sin_v) * SCALE
            a, b, ga, gb = _head_norm_bwd(qf_ref[:, lo:mid], qf_ref[:, mid:hi], gqn, gqr, dqn, dqr)
            dqf_ref[:, lo:mid] = a.astype(BF16)
            dqf_ref[:, mid:hi] = b.astype(BF16)
            dgq_n = dgq_n + jnp.sum(ga, axis=0, keepdims=True)
            dgq_r = dgq_r + jnp.sum(gb, axis=0, keepdims=True)
            dkn = dk_ref[h, :, 0:HK].astype(F32) * LN2
            dkr = _rope(dk_ref[h, :, HK:QKP].astype(F32), cos_v, sin_v) * LN2
            a, b, ga, gb = _head_norm_bwd(kv_ref[:, lo:mid], kpe, gkn, gkr, dkn, dkr)
            dkv_ref[:, lo:mid] = a.astype(BF16)
            dkv_ref[:, mid:hi] = dv_ref[h].astype(BF16)
            dkpe = dkpe + b
            dgk_n = dgk_n + jnp.sum(ga, axis=0, keepdims=True)
            dgk_r = dgk_r + jnp.sum(gb, axis=0, keepdims=True)
        dkpe_ref[...] = dkpe
        dgq_ref[:, 0:HK] += dgq_n
        dgq_ref[:, HK:QKP] += dgq_r
        dgk_ref[:, 0:HK] += dgk_n
        dgk_ref[:, HK:QKP] += dgk_r

    head = pl.BlockSpec((tm, HEADS * QKP), lambda i: (i, 0))
    tok = pl.BlockSpec((tm, HK), lambda i: (i, 0))
    gain = pl.BlockSpec((1, QKP), lambda i: (0, 0))
    hq = pl.BlockSpec((HEADS, tm, QKP), lambda i: (0, i, 0))
    return _call(
        body, name="mla_prep_bwd", grid=(t // tm,),
        in_specs=[head, head, pl.BlockSpec((tm, HK), lambda i: (i, MLA_COLS // HK - 1)), tok, tok, gain, gain,
                  hq, hq, pl.BlockSpec((HEADS, tm, HK), lambda i: (0, i, 0))],
        out_specs=[head, head, tok, gain, gain],
        out_shape=[jax.ShapeDtypeStruct((t, HEADS * QKP), BF16), jax.ShapeDtypeStruct((t, HEADS * QKP), BF16),
                   jax.ShapeDtypeStruct((t, HK), F32), jax.ShapeDtypeStruct((1, QKP), F32),
                   jax.ShapeDtypeStruct((1, QKP), F32)],
        compiler_params=_cp(),
    )(qf, kv, p_mla, cos, sin, gq, gk, dq, dk, dv)


def _chunk_mask(row0, rows, cols):
    r = lax.broadcasted_iota(jnp.int32, (rows, cols), 0) + row0
    c = lax.broadcasted_iota(jnp.int32, (rows, cols), 1)
    return jnp.right_shift(r, 6) >= jnp.right_shift(c, 6)


def _flash_fwd(q, k, v, side=None):
    t = q.shape[1]
    tq = min(TQ, t)
    nq = t // tq
    sub = min(SUBQ, tq)
    pairs = [(i, j) for i in range(nq) for j in range(i + 1)]
    qi = jnp.asarray([p[0] for p in pairs], jnp.int32)
    kj = jnp.asarray([p[1] for p in pairs], jnp.int32)
    s_in = len(side.inputs) if side else 0
    s_out = len(side.out_shapes) if side else 0

    def body(qi_ref, kj_ref, q_ref, k_ref, v_ref, *rest):
        o_ref, lse_ref = rest[s_in:s_in + 2]
        m_s, acc_s = rest[s_in + 2 + s_out:s_in + 4 + s_out]
        side_refs = list(rest[:s_in]) + list(rest[s_in + 2:s_in + 2 + s_out]) + list(rest[s_in + 4 + s_out:])
        n = pl.program_id(1)
        i, j = qi_ref[n], kj_ref[n]
        if side:
            @pl.when(jnp.logical_and(pl.program_id(0) == 0, n == 0))
            def _():
                side.start(*side_refs)

        @pl.when(j == 0)
        def _():
            m_s[...] = jnp.full_like(m_s, NEG)
            acc_s[...] = jnp.zeros_like(acc_s)

        def step(diag):
            subs = range(tq // sub)
            width = [(r + 1) * sub if diag else tq for r in subs]
            logits = [_dot_nt(q_ref[r * sub:(r + 1) * sub, :], k_ref[0:width[r], :]) for r in subs]
            for r in subs:
                rows = slice(r * sub, (r + 1) * sub)
                cols = width[r]
                s = logits[r]
                if diag:
                    s = jnp.where(_chunk_mask(r * sub, sub, cols), s, NEG)
                m_old = m_s[rows, :]
                m_new = jnp.maximum(m_old, jnp.max(s, axis=-1, keepdims=True))
                alpha = jnp.exp2(m_old - m_new)
                p = jnp.exp2((s - jnp.tile(m_new, (1, cols // HK))).astype(BF16))
                acc_s[rows, :] = jnp.tile(alpha, (1, 2)) * acc_s[rows, :] + _dot(p, v_ref[0:cols, :])
                m_s[rows, :] = m_new

        @pl.when(j < i)
        def _():
            step(False)

        @pl.when(j == i)
        def _():
            step(True)
            l = -acc_s[:, HK:QKP]
            o_ref[...] = (acc_s[:, 0:HK] / l).astype(BF16)
            lse_ref[...] = m_s[...] + jnp.log(l) * LOG2E

        if side:
            @pl.when(jnp.logical_and(pl.program_id(0) == HEADS - 1, n == len(pairs) - 1))
            def _():
                side.finish(*side_refs)

    anywhere = pl.BlockSpec(memory_space=pl.ANY)
    grid_spec = pltpu.PrefetchScalarGridSpec(
        num_scalar_prefetch=2, grid=(HEADS, len(pairs)),
        in_specs=[pl.BlockSpec((None, tq, QKP), lambda h, n, qi, kj: (h, qi[n], 0)),
                  pl.BlockSpec((None, tq, QKP), lambda h, n, qi, kj: (h, kj[n], 0)),
                  pl.BlockSpec((None, tq, QKP), lambda h, n, qi, kj: (h, kj[n], 0))] + [anywhere] * s_in,
        out_specs=[pl.BlockSpec((tq, HK), lambda h, n, qi, kj: (qi[n], h)),
                   pl.BlockSpec((None, tq, HK), lambda h, n, qi, kj: (h, qi[n], 0))] + [anywhere] * s_out,
        scratch_shapes=[pltpu.VMEM((tq, HK), F32), pltpu.VMEM((tq, QKP), F32)] + (list(side.scratch) if side else []),
    )
    return _call(
        body, name="flash_fwd", grid_spec=grid_spec,
        out_shape=[jax.ShapeDtypeStruct((t, D), BF16), jax.ShapeDtypeStruct((HEADS, t, HK), F32)]
        + (list(side.out_shapes) if side else []),
        compiler_params=_cp(),
    )(qi, kj, q, k, v, *(side.inputs if side else []))


def _attn_do(do, o):
    t = do.shape[0]
    tm = min(TM, t)

    def body(do_ref, o_ref, d_ref):
        lane = lax.broadcasted_iota(jnp.int32, (tm, HK), 1)
        for h in range(HEADS):
            ln = slice(h * HK, (h + 1) * HK)
            dov = do_ref[:, ln]
            d = jnp.sum(dov.astype(F32) * o_ref[:, ln].astype(F32), axis=-1, keepdims=True)
            hi = d.astype(BF16).astype(F32)
            d_ref[h, :, 0:HK] = dov
            d_ref[h, :, HK:QKP] = jnp.where(lane == 0, hi, jnp.where(lane == 1, d - hi, 0.0)).astype(BF16)

    blk = pl.BlockSpec((tm, D), lambda i: (i, 0))
    return _call(
        body, name="attn_do", grid=(t // tm,),
        in_specs=[blk, blk],
        out_specs=pl.BlockSpec((HEADS, tm, QKP), lambda i: (0, i, 0)),
        out_shape=jax.ShapeDtypeStruct((HEADS, t, QKP), BF16),
        compiler_params=_cp(),
    )(do, o)


def _flash_bwd(q, k, v, lse, do):
    t = q.shape[1]
    tq = min(TQ, t)
    nq = t // tq
    sub = min(SUBQ, tq)
    pairs = [(i, j) for j in range(nq) for i in range(j, nq)]
    qi = jnp.asarray([p[0] for p in pairs], jnp.int32)
    kj = jnp.asarray([p[1] for p in pairs], jnp.int32)
    npairs = len(pairs)

    def body(qi_ref, kj_ref, q_ref, k_ref, v_ref, lse_ref, do_ref, dq_ref, dk_ref, dv_ref):
        n = pl.program_id(1)
        i, j = qi_ref[n], kj_ref[n]

        @pl.when(n == 0)
        def _():
            dq_ref[...] = jnp.zeros_like(dq_ref)

        @pl.when(i == j)
        def _():
            dk_ref[...] = jnp.zeros_like(dk_ref)
            dv_ref[...] = jnp.zeros_like(dv_ref)

        def step(diag):
            for r in range(tq // sub):
                rows = slice(r * sub, (r + 1) * sub)
                cols = (r + 1) * sub if diag else tq
                qv, kv_ = q_ref[rows, :], k_ref[0:cols, :]
                p = jnp.exp2(_dot_nt(qv, kv_) - jnp.tile(lse_ref[rows, :], (1, cols // HK)))
                if diag:
                    p = jnp.where(_chunk_mask(r * sub, sub, cols), p, 0.0)
                dp_less_delta = _dot_nt(do_ref[rows, :], v_ref[0:cols, :])
                ds = (p * dp_less_delta).astype(BF16)
                dv_ref[0:cols, :] += _dot_tn(p.astype(BF16), do_ref[rows, 0:HK])
                dk_ref[0:cols, :] += _dot_tn(ds, qv)
                dq_rows = pl.ds(pl.multiple_of(i * tq + r * sub, sub), sub)
                dq_ref[dq_rows, :] += _dot(ds, kv_)

        @pl.when(j < i)
        def _():
            step(False)

        @pl.when(j == i)
        def _():
            step(True)

    grid_spec = pltpu.PrefetchScalarGridSpec(
        num_scalar_prefetch=2, grid=(HEADS, npairs),
        in_specs=[pl.BlockSpec((None, tq, QKP), lambda h, n, qi, kj: (h, qi[n], 0)),
                  pl.BlockSpec((None, tq, QKP), lambda h, n, qi, kj: (h, kj[n], 0)),
                  pl.BlockSpec((None, tq, QKP), lambda h, n, qi, kj: (h, kj[n], 0)),
                  pl.BlockSpec((None, tq, HK), lambda h, n, qi, kj: (h, qi[n], 0)),
                  pl.BlockSpec((None, tq, QKP), lambda h, n, qi, kj: (h, qi[n], 0))],
        out_specs=[pl.BlockSpec((None, t, QKP), lambda h, n, qi, kj: (h, 0, 0)),
                   pl.BlockSpec((None, tq, QKP), lambda h, n, qi, kj: (h, kj[n], 0)),
                   pl.BlockSpec((None, tq, HK), lambda h, n, qi, kj: (h, kj[n], 0))],
    )
    return _call(
        body, name="flash_bwd", grid_spec=grid_spec,
        out_shape=[jax.ShapeDtypeStruct((HEADS, t, QKP), F32), jax.ShapeDtypeStruct((HEADS, t, QKP), F32),
                   jax.ShapeDtypeStruct((HEADS, t, HK), F32)],
        compiler_params=_cp(56),
    )(qi, kj, q, k, v, lse, do)


def _adamw(name, w, g, m, v):
    r, c = w.shape
    tr = r if r <= 256 else next(k for k in (256, 352, 384) if r % k == 0)

    def body(w_ref, g_ref, m_ref, v_ref, d_ref, nm_ref, nv_ref):
        gv = g_ref[...]
        nm = ADAM_B1 * m_ref[...] + (1.0 - ADAM_B1) * gv
        nv = ADAM_B2 * v_ref[...] + (1.0 - ADAM_B2) * (gv * gv)
        m_hat = nm / (1.0 - ADAM_B1 ** ADAM_STEP)
        v_hat = nv / (1.0 - ADAM_B2 ** ADAM_STEP)
        d_ref[...] = -ADAM_LR * (m_hat / (jnp.sqrt(v_hat) + ADAM_EPS) + ADAM_WD * w_ref[...])
        nm_ref[...] = nm
        nv_ref[...] = nv

    blk = pl.BlockSpec((tr, c), lambda i: (i, 0))
    return _call(
        body, name=name, grid=(r // tr,),
        in_specs=[blk] * 4, out_specs=[blk] * 3,
        out_shape=[jax.ShapeDtypeStruct((r, c), F32)] * 3,
        compiler_params=_cp(),
    )(w, g, m, v)


def _place():
    return lax.axis_index("x"), lax.axis_index("y"), lax.axis_index("c")


def _other_chips(x, y):
    return [(1 - x, y), (x, 1 - y), (1 - x, 1 - y)]


class _Exchange:
    inputs = ()
    out_shapes = ()
    scratch = ()

    def start(self, *refs):
        raise NotImplementedError

    def finish(self, *refs):
        raise NotImplementedError

    def alone(self, name):
        def body(*refs):
            self.start(*refs)
            self.finish(*refs)

        anywhere = pl.BlockSpec(memory_space=pl.ANY)
        return _call(
            body, name=name,
            in_specs=[anywhere] * len(self.inputs), out_specs=[anywhere] * len(self.out_shapes),
            out_shape=list(self.out_shapes), scratch_shapes=list(self.scratch),
        )(*self.inputs)


class _GatherWeights(_Exchange):
    def __init__(self, shard):
        self.r = shard.shape[0]
        self.inputs = (shard,)
        self.out_shapes = (jax.ShapeDtypeStruct((4, self.r, PACK_W), shard.dtype),)
        self.scratch = (pltpu.SemaphoreType.DMA((6,)), pltpu.SemaphoreType.DMA((6,)))

    def gathered(self, got, k):
        return lax.dynamic_update_slice(got, self.inputs[0][None], (k, 0, 0))

    def _copies(self, s_ref, g_ref, send_sems, recv_sems):
        half = self.r // 2
        x, y, c = _place()
        chips = _other_chips(x, y)

        def rows(px, py, pc):
            return g_ref.at[2 * px + py, pl.ds(pc * half, half), :]

        def copy(k, block, to, src=None):
            return pltpu.make_async_remote_copy(
                src_ref=rows(*block) if src is None else src, dst_ref=rows(*block),
                send_sem=send_sems.at[k], recv_sem=recv_sems.at[k], device_id=to, device_id_type=MESH)

        first = [copy(j, (x, y, c), (*chip, c), src=s_ref.at[pl.ds(c * half, half), :]) for j, chip in enumerate(chips)]
        passed = [copy(3 + j, (*chip, c), (x, y, 1 - c)) for j, chip in enumerate(chips)]
        landed = [copy(j, (*chip, c), (x, y, c)) for j, chip in enumerate(chips)]
        landed += [copy(3 + j, (*chip, 1 - c), (x, y, c)) for j, chip in enumerate(chips)]
        return first, passed, landed

    def start(self, *refs):
        first, _, _ = self._copies(*refs)
        for cp in first:
            cp.start()

    def finish(self, *refs):
        first, passed, landed = self._copies(*refs)
        for j in range(3):
            landed[j].wait_recv()
            passed[j].start()
        for j in range(3):
            landed[3 + j].wait_recv()
        for cp in first + passed:
            cp.wait_send()


def _swap_halves(name, gp):
    r = gp.shape[1]
    half = r // 2

    def body(g_ref, o_ref, send_sem, recv_sem):
        x, y, c = _place()
        cp = pltpu.make_async_remote_copy(
            src_ref=g_ref.at[:, pl.ds((1 - c) * half, half), :], dst_ref=o_ref,
            send_sem=send_sem, recv_sem=recv_sem, device_id=(x, y, 1 - c), device_id_type=MESH)
        cp.start()
        cp.wait()

    return _call(
        body, name=name,
        in_specs=[pl.BlockSpec(memory_space=pl.ANY)],
        out_specs=pl.BlockSpec(memory_space=pl.ANY),
        out_shape=jax.ShapeDtypeStruct((4, half, PACK_W), gp.dtype),
        scratch_shapes=[pltpu.SemaphoreType.DMA, pltpu.SemaphoreType.DMA],
    )(gp)


def _chip_sum(name, gp, got, c_arr):
    half = got.shape[1]
    tr = ADD_ROWS
    nb = half // tr

    def body(c_ref, a_ref, b_ref, o_ref, ob_ref):
        s = a_ref[...] + b_ref[...]
        o_ref[...] = s
        ob_ref[...] = s.astype(BF16)

    grid_spec = pltpu.PrefetchScalarGridSpec(
        num_scalar_prefetch=1, grid=(4, nb),
        in_specs=[pl.BlockSpec((None, tr, PACK_W), lambda s, i, c: (s, c[0] * nb + i, 0)),
                  pl.BlockSpec((None, tr, PACK_W), lambda s, i, c: (s, i, 0))],
        out_specs=[pl.BlockSpec((None, tr, PACK_W), lambda s, i, c: (s, i, 0)),
                   pl.BlockSpec((None, tr, PACK_W), lambda s, i, c: (s, i, 0))],
    )
    return _call(
        body, name=name, grid_spec=grid_spec,
        out_shape=[jax.ShapeDtypeStruct(got.shape, F32), jax.ShapeDtypeStruct(got.shape, BF16)],
        compiler_params=_cp(),
    )(c_arr, gp, got)


class _ScatterChipSums(_Exchange):
    def __init__(self, cs):
        self.inputs = (cs,)
        self.out_shapes = (jax.ShapeDtypeStruct((3,) + cs.shape[1:], cs.dtype),)
        self.scratch = (pltpu.SemaphoreType.DMA((3,)), pltpu.SemaphoreType.DMA((3,)))

    def _copies(self, s_ref, o_ref, send_sems, recv_sems):
        x, y, c = _place()
        return [pltpu.make_async_remote_copy(
            src_ref=s_ref.at[2 * px + py], dst_ref=o_ref.at[j],
            send_sem=send_sems.at[j], recv_sem=recv_sems.at[j], device_id=(px, py, c), device_id_type=MESH)
            for j, (px, py) in enumerate(_other_chips(x, y))]

    def start(self, *refs):
        for cp in self._copies(*refs):
            cp.start()

    def finish(self, *refs):
        for cp in self._copies(*refs):
            cp.wait()


def _shard_sum(name, cs, got, kc_arr):
    h = cs.shape[1]
    tr = ADD_ROWS
    nb = h // tr

    def body(k_ref, a_ref, b_ref, o_ref):
        o_ref[...] = ((a_ref[...] + b_ref[0].astype(F32)) + b_ref[1].astype(F32)) + b_ref[2].astype(F32)

    grid_spec = pltpu.PrefetchScalarGridSpec(
        num_scalar_prefetch=1, grid=(nb,),
        in_specs=[pl.BlockSpec((None, tr, PACK_W), lambda i, k: (k[0], i, 0)),
                  pl.BlockSpec((3, tr, PACK_W), lambda i, k: (0, i, 0))],
        out_specs=pl.BlockSpec((tr, PACK_W), lambda i, k: (k[1] * nb + i, 0)),
    )
    return _call(
        body, name=name, grid_spec=grid_spec,
        out_shape=jax.ShapeDtypeStruct((2 * h, PACK_W), F32),
        compiler_params=_cp(),
    )(kc_arr, cs, got)


def _join_halves(name, both):
    h = both.shape[0] // 2

    def body(m_ref, o_ref, send_sem, recv_sem):
        x, y, c = _place()
        cp = pltpu.make_async_remote_copy(
            src_ref=m_ref.at[pl.ds(c * h, h), :], dst_ref=o_ref.at[pl.ds(c * h, h), :],
            send_sem=send_sem, recv_sem=recv_sem, device_id=(x, y, 1 - c), device_id_type=MESH)
        cp.start()
        cp.wait_send()
        pltpu.make_async_remote_copy(
            src_ref=m_ref.at[pl.ds(c * h, h), :], dst_ref=o_ref.at[pl.ds((1 - c) * h, h), :],
            send_sem=send_sem, recv_sem=recv_sem, device_id=(x, y, 1 - c), device_id_type=MESH).wait_recv()

    return _call(
        body, name=name,
        in_specs=[pl.BlockSpec(memory_space=pl.ANY)],
        out_specs=pl.BlockSpec(memory_space=pl.ANY),
        out_shape=jax.ShapeDtypeStruct(both.shape, both.dtype),
        input_output_aliases={0: 0},
        scratch_shapes=[pltpu.SemaphoreType.DMA, pltpu.SemaphoreType.DMA],
    )(both)


def _all_reduce_small(v):
    r = v.shape[0]

    def body(v_ref, o_ref, buf, send_sems, recv_sems):
        x, y, c = _place()
        me = 4 * x + 2 * y + c
        buf[me] = v_ref[...]
        cps = []
        for k in range(1, 8):
            peer = (x ^ (k >> 2), y ^ ((k >> 1) & 1), c ^ (k & 1))
            cps.append(pltpu.make_async_remote_copy(
                src_ref=v_ref, dst_ref=buf.at[me],
                send_sem=send_sems.at[k - 1], recv_sem=recv_sems.at[k - 1], device_id=peer, device_id_type=MESH))
        for cp in cps:
            cp.start()
        for k in range(1, 8):
            pltpu.make_async_remote_copy(
                src_ref=v_ref, dst_ref=buf.at[me ^ k],
                send_sem=send_sems.at[k - 1], recv_sem=recv_sems.at[k - 1],
                device_id=(x, y, c), device_id_type=MESH).wait_recv()
        for cp in cps:
            cp.wait_send()
        acc = buf[0]
        for k in range(1, 8):
            acc = acc + buf[k]
        o_ref[...] = acc

    return _call(
        body, name="all_reduce_small",
        in_specs=[pl.BlockSpec(memory_space=pltpu.VMEM)],
        out_specs=pl.BlockSpec(memory_space=pltpu.VMEM),
        out_shape=jax.ShapeDtypeStruct((r, 128), F32),
        scratch_shapes=[pltpu.VMEM((8, r, 128), F32), pltpu.SemaphoreType.DMA((7,)), pltpu.SemaphoreType.DMA((7,))],
    )(v)


def _group(names):
    return tuple(e for e in BIG if e[0] in names)


def _pack(shards, dtype):
    return jnp.concatenate([s.astype(dtype).reshape(-1, PACK_W) for s in shards], axis=0)


def _unpack_full(g, group):
    out, at = {}, 0
    for name, rows, cols, axis in group:
        n = rows * cols // 4 // PACK_W
        blk = g[:, at:at + n, :]
        at += n
        if axis == 1:
            out[name] = blk.reshape(4, rows, cols // 4).transpose(1, 0, 2).reshape(rows, cols)
        else:
            out[name] = blk.reshape(rows, cols)
    return out


def _pack_grads(grads, group):
    parts = []
    for name, rows, cols, axis in group:
        g = grads[name]
        if axis == 1:
            g = g.reshape(rows, 4, cols // 4).transpose(1, 0, 2)
        parts.append(g.reshape(4, -1, PACK_W))
    rows_total = sum(p.shape[1] for p in parts)
    pad = -rows_total % PACK_ALIGN
    if pad:
        parts.append(jnp.zeros((4, pad, PACK_W), F32))
    return jnp.concatenate(parts, axis=1)


def _unpack_shard(s, group):
    out, at = {}, 0
    for name, rows, cols, axis in group:
        n = rows * cols // 4 // PACK_W
        shape = (rows, cols // 4) if axis == 1 else (rows // 4, cols)
        out[name] = s[at:at + n, :].reshape(shape)
        at += n
    return out


def _pack_small(parts):
    flat = jnp.concatenate([p.reshape(-1) for p in parts])
    pad = -flat.shape[0] % 1024
    return jnp.concatenate([flat, jnp.zeros((pad,), F32)]).reshape(-1, 128)


def _ffn_in(tag, h, gain, w_in, side=None):
    t = h.shape[0]

    def compute_in(rows, weights, outs):
        hv, w_ref = rows[0][...], weights[0]
        r = lax.rsqrt(jnp.mean(hv * hv, axis=-1, keepdims=True) + EPS)
        a = (hv * r * weights[1][...]).astype(BF16)
        outs[0][...] = a
        for j in range(DFF // FFN_CHUNK):
            cols = slice(j * FFN_CHUNK, (j + 1) * FFN_CHUNK)
            gate = _dot(a, w_ref[:, cols])
            up = _dot(a, w_ref[:, DFF + j * FFN_CHUNK:DFF + (j + 1) * FFN_CHUNK])
            outs[1][:, cols] = gate.astype(BF16)
            outs[2][:, cols] = up.astype(BF16)
            outs[3][:, cols] = (_silu(gate) * up).astype(BF16)

    return _rows_call(tag + "_in", [h], [w_in, gain], [(D, BF16)] + [(DFF, BF16)] * 3, compute_in, min(FFN_TM, t),
                      side=side)


def _ffn_out(tag, act, h, w_out, next_gain, target=None):
    t = h.shape[0]
    tm = min(FFN_TM, t)

    def compute_out(rows, weights, outs):
        hn = rows[1][...] + 0.5 * _dot(rows[0][...], weights[0][...])
        g = weights[1][...]
        r = lax.rsqrt(jnp.mean(hn * hn, axis=-1, keepdims=True) + EPS)
        xh = hn * r
        if target is None:
            outs[0][...] = hn
            outs[1][...] = (xh * g).astype(BF16)
        else:
            err = xh * g - rows[2][...]
            dy = err * (1.0 / D)
            dxh = dy * g
            outs[0][...] = r * (dxh - xh * jnp.mean(dxh * xh, axis=-1, keepdims=True))
            outs[1][...] += jnp.sum(dy * xh, axis=0, keepdims=True)
            outs[2][...] += 0.5 * jnp.sum(jnp.mean(err * err, axis=-1, keepdims=True), axis=0, keepdims=True)

    if target is None:
        return _rows_call(tag + "_out", [act, h], [w_out, next_gain], [(D, F32), (D, BF16)], compute_out, tm)
    return _rows_call(tag + "_out", [act, h, target], [w_out, next_gain], [(D, F32)], compute_out, tm, sums=(D, 128))


class _Reduction:
    def __init__(self, tag, group, c_arr, k_arr):
        self.tag, self.group, self.c_arr, self.k_arr = tag, group, c_arr, k_arr

    def begin(self, grads):
        gp = _pack_grads(grads, self.group)
        self.sums, sums_bf16 = _chip_sum("grad_chip_sum_" + self.tag, gp, _swap_halves("grad_swap_" + self.tag, gp), self.c_arr)
        return _ScatterChipSums(sums_bf16)

    def end(self, got):
        mine = _shard_sum("grad_shard_sum_" + self.tag, self.sums, got, self.k_arr)
        return _unpack_shard(_join_halves("grad_join_" + self.tag, mine), self.group)


def _ffn_bwd(tag, h, gain, w_in, w_out, saved, dout, side, reduction):
    t = h.shape[0]
    tm = min(TM, t)
    n, gate, up, act = saved

    def compute(rows, weights, outs):
        d = rows[0][...].astype(BF16)
        for j in range(DFF // FFN_CHUNK):
            cols = slice(j * FFN_CHUNK, (j + 1) * FFN_CHUNK)
            da = 0.5 * _dot_nt(d, weights[0][cols, :])
            g, u = rows[1][:, cols].astype(F32), rows[2][:, cols].astype(F32)
            s = _sig(g)
            silu = g * s
            outs[0][:, cols] = (da * u * (s + silu * (1.0 - s))).astype(BF16)
            outs[1][:, cols] = (da * silu).astype(BF16)

    dgate, dup, *side_out = _rows_call(tag + "_dact", [dout, gate, up], [w_out], [(DFF, BF16)] * 2, compute,
                                       min(FFN_TM, t), side=side)
    dw_out = _mm_tn(tag + "_dw_out", act, dout, scale=0.5, tm=DFF // 2, tn=D)
    dw_g = _mm_tn(tag + "_dw_gate", n, dgate, tm=D, tn=DFF // 2)
    dw_u = _mm_tn(tag + "_dw_up", n, dup, tm=D, tn=DFF // 2)
    sending = reduction.begin({tag + "_w_in": jnp.concatenate([dw_g, dw_u], axis=1), tag + "_w_out": dw_out})

    def compute_dn(rows, weights, outs):
        w_ref = weights[0]
        dn = _dot_nt(rows[0][...], w_ref[:, 0:DFF]) + _dot_nt(rows[1][...], w_ref[:, DFF:2 * DFF])
        dx, dg = _rms_bwd_vals(rows[2][...], weights[1][...], dn)
        outs[0][...] = rows[3][...] + dx
        outs[1][...] += jnp.sum(dg, axis=0, keepdims=True)

    dh, dgain, got = _rows_call(tag + "_dn", [dgate, dup, h, dout], [w_in, gain], [(D, F32)], compute_dn,
                                min(FFN_TM, t), side=sending, sums=(D,), vmem_mb=58)
    return dh, dgain, side_out, got


def kernel(x, positions, ffn1_norm, ffn1_w_in, ffn1_w_out, mix_norm, w_in, hg_lb_table, hg_out_norm, w_hg_branch, mla_q_lora_norm, w_q_up, mla_kv_lora_norm, w_kv_up, q_head_norm, k_head_norm, w_mla_branch, w_merge, b_merge, w_out, ffn2_norm, ffn2_w_in, ffn2_w_out, final_norm, loss_target, m_ffn1_norm, m_ffn1_w_in, m_ffn1_w_out, m_mix_norm, m_w_in, m_hg_lb_table, m_hg_out_norm, m_w_hg_branch, m_mla_q_lora_norm, m_w_q_up, m_mla_kv_lora_norm, m_w_kv_up, m_q_head_norm, m_k_head_norm, m_w_mla_branch, m_w_merge, m_b_merge, m_w_out, m_ffn2_norm, m_ffn2_w_in, m_ffn2_w_out, m_final_norm, v_ffn1_norm, v_ffn1_w_in, v_ffn1_w_out, v_mix_norm, v_w_in, v_hg_lb_table, v_hg_out_norm, v_w_hg_branch, v_mla_q_lora_norm, v_w_q_up, v_mla_kv_lora_norm, v_w_kv_up, v_q_head_norm, v_k_head_norm, v_w_mla_branch, v_w_merge, v_b_merge, v_w_out, v_ffn2_norm, v_ffn2_w_in, v_ffn2_w_out, v_final_norm):
    a = dict(locals())
    w = {n: a[n] for n in WEIGHT_ORDER}
    mom = {n: a["m_" + n] for n in WEIGHT_ORDER}
    var = {n: a["v_" + n] for n in WEIGHT_ORDER}
    t = x.shape[1]
    tm = min(TM, t)
    xt = x.reshape(t, D)
    target = loss_target.reshape(t, D)
    pos = positions.reshape(t, 1)
    x_i, y_i, c_i = _place()
    k_idx = (2 * x_i + y_i).astype(jnp.int32)
    c_arr = c_i.astype(jnp.int32).reshape(1)
    k_arr = jnp.stack([k_idx, c_i.astype(jnp.int32)])

    group_first = _group(("ffn1_w_in", "ffn1_w_out"))
    group_mid = _group(("w_in", "w_hg_branch", "w_q_up", "w_kv_up", "w_mla_branch", "w_merge", "w_out"))
    group_last = _group(("ffn2_w_in", "ffn2_w_out"))
    use_first = _group(("ffn1_w_in",))
    use_early = _group(("ffn1_w_out", "w_in", "w_hg_branch", "w_q_up", "w_kv_up"))
    use_late = _group(("w_mla_branch", "w_merge", "w_out", "ffn2_w_in", "ffn2_w_out"))
    gather_first = _GatherWeights(_pack([w[e[0]][0] for e in use_first], BF16))
    gather_early = _GatherWeights(_pack([w[e[0]][0] for e in use_early], BF16))
    gather_late = _GatherWeights(_pack([w[e[0]][0] for e in use_late], BF16))
    (got,) = gather_first.alone("gather_first")
    full = _unpack_full(gather_first.gathered(got, k_idx), use_first)
    n1, gate1, up1, act1, got = _ffn_in("ffn1", xt, w["ffn1_norm"], full["ffn1_w_in"], gather_early)
    full.update(_unpack_full(gather_early.gathered(got, k_idx), use_early))
    h1, u = _ffn_out("ffn1", act1, xt, full["ffn1_w_out"], w["mix_norm"])
    ffn1_saved = (n1, gate1, up1, act1)
    w_in_full = full["w_in"]
    w_in_hg = w_in_full[:, :4 * D]
    w_in_mla = jnp.pad(w_in_full[:, 4 * D:], ((0, 0), (0, MLA_COLS - (4800 - 4 * D))))
    w_q_pad = jnp.pad(full["w_q_up"].reshape(Q_LORA, HEADS, QK), ((0, 0), (0, 0), (0, QKP - QK))).reshape(Q_LORA, HEADS * QKP)
    w_kv = full["w_kv_up"]
    gq = jnp.pad(w["q_head_norm"], ((0, 0), (0, QKP - QK)))
    gk = jnp.pad(w["k_head_norm"], ((0, 0), (0, QKP - QK)))

    ident = lambda accs, ex: (accs[0],)
    def in_hg(rows, weights, outs):
        a = rows[0][...]
        for j in range(4 * D // 512):
            cols = slice(j * 512, (j + 1) * 512)
            outs[0][:, cols] = _dot(a, weights[0][:, cols])

    (p_hg,) = _rows_call("in_hg", [u], [w_in_hg], [(4 * D, F32)], in_hg, min(FFN_TM, t))
    (p_mla,) = _mm("in_mla", [_a_spec(u, tm)], [_b_nn(w_in_mla, MLA_COLS)], [(0, 0)], ident, [], [F32], t, MLA_COLS, tm, MLA_COLS)
    o_raw, hg_o, states = _hgrn_fwd(p_hg, w["hg_lb_table"], w["hg_out_norm"])
    (y_hg,) = _mm("hg_branch", [_a_spec(hg_o, tm)], [_b_nn(full["w_hg_branch"], 512)], [(0, 0)], ident, [], [BF16], t, D, tm, 512)
    cqn, ckvn = _lora_norm_fwd(p_mla, w["mla_q_lora_norm"], w["mla_kv_lora_norm"])
    (qf,) = _mm("q_up", [_a_spec(cqn, tm)], [_b_nn(w_q_pad, 512)], [(0, 0)], ident, [], [F32], t, HEADS * QKP, tm, 512)
    (kvf,) = _mm("kv_up", [_a_spec(ckvn, tm)], [_b_nn(w_kv, 512)], [(0, 0)], ident, [], [F32], t, HEADS * QKP, tm, 512)
    cos, sin = _rope_tables(pos)
    qh, kh, vh = _mla_prep_fwd(qf, kvf, p_mla, cos, sin, gq, gk)
    o_mla, lse, got = _flash_fwd(qh, kh, vh, side=gather_late)
    full.update(_unpack_full(gather_late.gathered(got, k_idx), use_late))
    (y_mla,) = _mm("mla_branch", [_a_spec(o_mla, tm)], [_b_nn(full["w_mla_branch"], 512)], [(0, 0)], ident, [], [BF16], t, D, tm, 512)

    def merge_epi(accs, ex):
        g_hg = _sig(accs[0] + ex[2])
        g_mla = _sig(accs[1] + ex[3])
        return g_hg * ex[0].astype(F32) + g_mla * ex[1].astype(F32), g_hg, g_mla

    w_merge_f = full["w_merge"]
    mix, g_hg, g_mla = _mm(
        "merge", [_a_spec(u, tm)], [_b_nn(w_merge_f, 512), _b_nn(w_merge_f, 512, D // 512)], [(0, 0), (0, 1)], merge_epi,
        [_e_tile(y_hg, tm, 512), _e_tile(y_mla, tm, 512), _e_row(w["b_merge"], 512), _e_row(w["b_merge"], 512, D // 512)],
        [BF16, BF16, BF16], t, D, tm, 512)
    (h2,) = _mm("out_proj", [_a_spec(mix, tm)], [_b_nn(full["w_out"], 512)], [(0, 0)],
                lambda accs, ex: (ex[0] + accs[0],), [_e_tile(h1, tm, 512)], [F32], t, D, tm, 512)
    ffn2_saved = _ffn_in("ffn2", h2, w["ffn2_norm"], full["ffn2_w_in"])
    dh3, d_final_norm, loss_part = _ffn_out("ffn2", ffn2_saved[3], h2, full["ffn2_w_out"], w["final_norm"], target=target)

    grads, small = {}, {}
    small["final_norm"] = d_final_norm
    reduce_last = _Reduction("last", group_last, c_arr, k_arr)
    reduce_mid = _Reduction("mid", group_mid, c_arr, k_arr)
    reduce_first = _Reduction("first", group_first, c_arr, k_arr)
    dh2, small["ffn2_norm"], _, got_last = _ffn_bwd(
        "ffn2", h2, w["ffn2_norm"], full["ffn2_w_in"], full["ffn2_w_out"], ffn2_saved, dh3, None, reduce_last)

    def dmix_epi(accs, ex):
        dm = accs[0]
        ghg, gml, yhg, yml = [e.astype(F32) for e in ex]
        return dm * ghg, dm * gml, dm * yhg * ghg * (1.0 - ghg), dm * yml * gml * (1.0 - gml)

    dy_hg, dy_mla, dpre_hg, dpre_mla = _mm(
        "d_mix", [_a_spec(dh2, tm)], [_b_nt(full["w_out"], 512)], [(0, 0)], dmix_epi,
        [_e_tile(g_hg, tm, 512), _e_tile(g_mla, tm, 512), _e_tile(y_hg, tm, 512), _e_tile(y_mla, tm, 512)],
        [BF16, BF16, BF16, BF16], t, D, tm, 512, trans_b=True)
    grads["w_out"] = _mm_tn("dw_out", mix, dh2)
    small["b_merge"] = jnp.concatenate([_colsum("db_hg", dpre_hg), _colsum("db_mla", dpre_mla)], axis=1)
    grads["w_merge"] = jnp.concatenate([_mm_tn("dw_merge_hg", u, dpre_hg), _mm_tn("dw_merge_mla", u, dpre_mla)], axis=1)
    grads["w_hg_branch"] = _mm_tn("dw_hg_branch", hg_o, dy_hg)
    grads["w_mla_branch"] = _mm_tn("dw_mla_branch", o_mla, dy_mla)
    (dho,) = _mm("d_hg_o", [_a_spec(dy_hg, tm)], [_b_nt(full["w_hg_branch"], 512)], [(0, 0)], ident, [], [BF16], t, D, tm, 512, trans_b=True)
    (do_mla,) = _mm("d_o_mla", [_a_spec(dy_mla, tm)], [_b_nt(full["w_mla_branch"], 512)], [(0, 0)], ident, [], [BF16], t, D, tm, 512, trans_b=True)

    dq_raw, df_raw, di_raw, dg_raw, small["hg_lb_table"], small["hg_out_norm"] = _hgrn_bwd(
        p_hg, w["hg_lb_table"], w["hg_out_norm"], o_raw, states, dho)
    dp_hg = [dq_raw, df_raw, di_raw, dg_raw]

    dqh, dkh, dvh = _flash_bwd(qh, kh, vh, lse, _attn_do(do_mla, o_mla))
    dqf, dkvf, dkpe, dgq, dgk = _mla_prep_bwd(qf, kvf, p_mla, cos, sin, gq, gk, dqh, dkh, dvh)
    small["q_head_norm"] = dgq[:, :QK]
    small["k_head_norm"] = dgk[:, :QK]
    dwq_pad = _mm_tn("dw_q_up", cqn, dqf, tm=Q_LORA, tn=1024)
    grads["w_q_up"] = dwq_pad.reshape(Q_LORA, HEADS, QKP)[:, :, :QK].reshape(Q_LORA, HEADS * QK)
    grads["w_kv_up"] = _mm_tn("dw_kv_up", ckvn, dkvf, tm=KV_LORA, tn=1024)
    (dcqn,) = _mm("d_cq", [_a_spec(dqf, tm)], [_b_nt(w_q_pad, Q_LORA)], [(0, 0)], ident, [], [F32], t, Q_LORA, tm, Q_LORA, trans_b=True)
    (dckvn,) = _mm("d_ckv", [_a_spec(dkvf, tm)], [_b_nt(w_kv, KV_LORA)], [(0, 0)], ident, [], [F32], t, KV_LORA, tm, KV_LORA, trans_b=True)
    dp_mla, small["mla_q_lora_norm"], small["mla_kv_lora_norm"] = _lora_norm_bwd(
        p_mla, w["mla_q_lora_norm"], w["mla_kv_lora_norm"], dcqn, dckvn, dkpe)

    dw_in_hg = [_mm_tn("dw_in_hg%d" % k, u, dp_hg[k]) for k in range(4)]
    dw_in_mla = _mm_tn("dw_in_mla", u, dp_mla, tn=MLA_COLS)
    grads["w_in"] = jnp.concatenate(dw_in_hg + [dw_in_mla[:, :4800 - 4 * D]], axis=1)
    tm_du = min(TM // 2, t)
    du, got_mid = _mm(
        "d_u",
        [_a_spec(dpre_hg, tm_du), _a_spec(dpre_mla, tm_du)] + [_a_spec(d, tm_du) for d in dp_hg] + [_a_spec(dp_mla, tm_du)],
        [_b_nt(w_merge_f, 512, D, 0), _b_nt(w_merge_f, 512, D, 1)]
        + [_b_nt(w_in_hg, 512, D, k) for k in range(4)] + [_b_nt(w_in_mla, 512)],
        [(k, k) for k in range(7)],
        lambda accs, ex: (functools.reduce(lambda p, q: p + q, accs),), [], [F32], t, D, tm_du, 512, trans_b=True,
        side=reduce_mid.begin(grads))
    dh1, small["mix_norm"] = _rms_bwd("mix_dnorm", h1, w["mix_norm"], du, dh2)
    dx, small["ffn1_norm"], _, got_first = _ffn_bwd(
        "ffn1", xt, w["ffn1_norm"], full["ffn1_w_in"], full["ffn1_w_out"], ffn1_saved, dh1, None, reduce_first)

    g_shard = {**reduce_last.end(got_last), **reduce_mid.end(got_mid), **reduce_first.end(got_first)}
    small_sum = _all_reduce_small(_pack_small([small[n] for n, _ in SMALL] + [loss_part])).reshape(-1)
    g_small, at = {}, 0
    for n, shape in SMALL:
        size = shape[0] * shape[1]
        g_small[n] = small_sum[at:at + size].reshape(shape)
        at += size
    loss = small_sum[at]

    g_out, d_out, m_out, v_out = {}, {}, {}, {}
    for n in WEIGHT_ORDER:
        shape = w[n].shape
        g = g_shard[n] if n in g_shard else g_small[n]
        two = g.shape
        d_, m_, v_ = _adamw("adamw_" + n, w[n].reshape(two), g, mom[n].reshape(two), var[n].reshape(two))
        g_out[n], d_out[n], m_out[n], v_out[n] = g.reshape(shape), d_.reshape(shape), m_.reshape(shape), v_.reshape(shape)

    return (loss, dx.reshape(x.shape), *[g_out[n] for n in WEIGHT_ORDER], *[d_out[n] for n in WEIGHT_ORDER],
            *[m_out[n] for n in WEIGHT_ORDER], *[v_out[n] for n in WEIGHT_ORDER])
```

```python
import functools

import numpy as np
import jax
import jax.numpy as jnp
from jax import lax
from jax.experimental import pallas as pl
from jax.experimental.pallas import tpu as pltpu

F32 = jnp.float32
BF16 = jnp.bfloat16
MESH = pl.DeviceIdType.MESH

D = 1024
DFF = 2816
HEADS = 8
HK = 128
CHUNK = 64
ROPE = 64
QK = 192
QKP = 256
Q_LORA = 384
KV_LORA = 256
MLA_COLS = 768
EPS = 1e-6
ROPE_THETA = 10000.0
SCALE = QK ** -0.5
LOG2E = 1.4426950408889634
LN2 = 0.6931471805599453
NEG = -1e30
EXP_CLAMP = 80.0

ADAM_LR = 0.001
ADAM_B1 = 0.9
ADAM_B2 = 0.999
ADAM_EPS = 1e-08
ADAM_WD = 0.01
ADAM_STEP = 10

PACK_W = 1024
ADD_ROWS = 352
PACK_ALIGN = 2 * ADD_ROWS

TM = 1024
FFN_TM = 512
FFN_CHUNK = 256
TQ = 1024
SUBQ = 256
HG_BT = 512
HG_HPB = 8
TT = 1024
ROW_TM = 256

VMEM_MB = 48

BIG = (
    ("ffn1_w_in", D, 2 * DFF, 1),
    ("ffn1_w_out", DFF, D, 0),
    ("w_in", D, 4800, 1),
    ("w_hg_branch", D, D, 0),
    ("w_q_up", Q_LORA, HEADS * QK, 1),
    ("w_kv_up", KV_LORA, HEADS * 2 * HK, 1),
    ("w_mla_branch", D, D, 0),
    ("w_merge", D, 2 * D, 1),
    ("w_out", D, D, 0),
    ("ffn2_w_in", D, 2 * DFF, 1),
    ("ffn2_w_out", DFF, D, 0),
)
SMALL = (
    ("ffn1_norm", (1, D)),
    ("mix_norm", (1, D)),
    ("hg_lb_table", (2, D)),
    ("hg_out_norm", (1, HK)),
    ("mla_q_lora_norm", (1, Q_LORA)),
    ("mla_kv_lora_norm", (1, KV_LORA)),
    ("q_head_norm", (1, QK)),
    ("k_head_norm", (1, QK)),
    ("b_merge", (1, 2 * D)),
    ("ffn2_norm", (1, D)),
    ("final_norm", (1, D)),
)
WEIGHT_ORDER = ("ffn1_norm", "ffn1_w_in", "ffn1_w_out", "mix_norm", "w_in", "hg_lb_table", "hg_out_norm",
                "w_hg_branch", "mla_q_lora_norm", "w_q_up", "mla_kv_lora_norm", "w_kv_up", "q_head_norm",
                "k_head_norm", "w_mla_branch", "w_merge", "b_merge", "w_out", "ffn2_norm", "ffn2_w_in",
                "ffn2_w_out", "final_norm")


def _call(body, **kw):
    return pl.pallas_call(body, **kw)


def _cp(vmem_mb=VMEM_MB):
    return pltpu.CompilerParams(vmem_limit_bytes=vmem_mb << 20)


def _dot(a, b):
    return lax.dot_general(a, b, (((1,), (0,)), ((), ())), preferred_element_type=F32)


def _dot_nt(a, b):
    return lax.dot_general(a, b, (((1,), (1,)), ((), ())), preferred_element_type=F32)


def _dot_tn(a, b):
    return lax.dot_general(a, b, (((0,), (0,)), ((), ())), preferred_element_type=F32)


def _sig(x):
    return jax.nn.sigmoid(x)


def _silu(x):
    return x * _sig(x)


def _dsilu(x):
    s = _sig(x)
    return s * (1.0 + x * (1.0 - s))


def _a_spec(arr, tm, kblk=None, kidx=0):
    kb = arr.shape[1] if kblk is None else kblk
    return arr, pl.BlockSpec((tm, kb), lambda i, j, kidx=kidx: (i, kidx)), slice(kidx * kb, (kidx + 1) * kb)


def _b_nn(arr, tn, off=0):
    return arr, pl.BlockSpec((arr.shape[0], tn), lambda i, j, off=off: (0, j + off)), ("cols", off)


def _b_nt(arr, tn, kblk=None, kidx=0):
    kb = arr.shape[1] if kblk is None else kblk
    return arr, pl.BlockSpec((tn, kb), lambda i, j, kidx=kidx: (j, kidx)), ("rows", slice(kidx * kb, (kidx + 1) * kb))


def _e_tile(arr, tm, tn, off=0):
    return arr, pl.BlockSpec((tm, tn), lambda i, j, off=off: (i, j + off)), ("tile", off)


def _e_row(arr, tn, off=0):
    return arr, pl.BlockSpec((1, tn), lambda i, j, off=off: (0, j + off)), ("row", off)


def _mm_resident(name, As, Bs, dots, epi, extras, out_dtypes, m, n, tn):
    def unique(arrays):
        seen = []
        for a in arrays:
            if not any(a is s for s in seen):
                seen.append(a)
        return seen

    rows = unique([a for a, _, _ in As] + [e for e, _, where in extras if where[0] == "tile"])
    weights = unique([b for b, _, _ in Bs] + [e for e, _, where in extras if where[0] == "row"])

    def ref_of(arr, row_refs, weight_refs):
        for r, ref in zip(rows, row_refs):
            if r is arr:
                return ref
        for wt, ref in zip(weights, weight_refs):
            if wt is arr:
                return ref

    def compute(row_refs, weight_refs, out_refs):
        a_vals = [ref_of(a, row_refs, weight_refs)[:, ks].astype(BF16) for a, _, ks in As]
        for j in range(n // tn):
            accs = []
            for ai, bi in dots:
                b, _, where = Bs[bi]
                b_ref = ref_of(b, row_refs, weight_refs)
                if where[0] == "cols":
                    accs.append(_dot(a_vals[ai], b_ref[:, (j + where[1]) * tn:(j + where[1] + 1) * tn]))
                else:
                    accs.append(_dot_nt(a_vals[ai], b_ref[j * tn:(j + 1) * tn, where[1]]))
            ex = [ref_of(e, row_refs, weight_refs)[:, (j + where[1]) * tn:(j + where[1] + 1) * tn]
                  for e, _, where in extras]
            for o_ref, o in zip(out_refs, epi(accs, ex)):
                o_ref[:, j * tn:(j + 1) * tn] = o.astype(o_ref.dtype)

    return _rows_call(name, rows, weights, [(n, dt) for dt in out_dtypes], compute, min(FFN_TM, m))


def _mm(name, As, Bs, dots, epi, extras, out_dtypes, m, n, tm, tn, trans_b=False, side=None):
    if side is None:
        return _mm_resident(name, As, Bs, dots, epi, extras, out_dtypes, m, n, tn)
    na, nb, ne, no = len(As), len(Bs), len(extras), len(out_dtypes)
    ni, nj = m // tm, n // tn
    s_in = len(side.inputs) if side else 0
    s_out = len(side.out_shapes) if side else 0

    def body(*refs):
        a_refs = refs[:na]
        b_refs = refs[na:na + nb]
        e_refs = refs[na + nb:na + nb + ne]
        at = na + nb + ne
        side_refs = refs[at:at + s_in]
        o_refs = refs[at + s_in:at + s_in + no]
        side_refs = list(side_refs) + list(refs[at + s_in + no:])
        if side:
            i, j = pl.program_id(0), pl.program_id(1)

            @pl.when(jnp.logical_and(i == 0, j == 0))
            def _():
                side.start(*side_refs)

        a_vals = [r[...].astype(BF16) for r in a_refs]
        accs = []
        for ai, bi in dots:
            b = b_refs[bi][...]
            accs.append(_dot_nt(a_vals[ai], b) if trans_b else _dot(a_vals[ai], b))
        outs = epi(accs, [r[...] for r in e_refs])
        for o_ref, o in zip(o_refs, outs):
            o_ref[...] = o.astype(o_ref.dtype)
        if side:
            @pl.when(jnp.logical_and(i == ni - 1, j == nj - 1))
            def _():
                side.finish(*side_refs)

    ops = list(As) + list(Bs) + list(extras)
    anywhere = pl.BlockSpec(memory_space=pl.ANY)
    res = _call(
        body, name=name,
        grid=(ni, nj),
        in_specs=[op[1] for op in ops] + [anywhere] * s_in,
        out_specs=[pl.BlockSpec((tm, tn), lambda i, j: (i, j)) for _ in out_dtypes] + [anywhere] * s_out,
        out_shape=[jax.ShapeDtypeStruct((m, n), dt) for dt in out_dtypes] + (list(side.out_shapes) if side else []),
        scratch_shapes=list(side.scratch) if side else [],
        compiler_params=_cp(),
    )(*[op[0] for op in ops], *(side.inputs if side else []))
    return res


def _rows_call(name, rows, weights, outs, compute, tm, side=None, sums=(), vmem_mb=VMEM_MB):
    t = rows[0].shape[0]
    nr, nw, no = len(rows), len(weights), len(outs) + len(sums)
    ni = t // tm
    s_in = len(side.inputs) if side else 0
    s_out = len(side.out_shapes) if side else 0

    def body(*refs):
        at = nr + nw
        side_refs = list(refs[at:at + s_in]) + list(refs[at + s_in + no:])
        if side:
            @pl.when(pl.program_id(0) == 0)
            def _():
                side.start(*side_refs)

        out_refs = refs[at + s_in:at + s_in + no]
        if sums:
            @pl.when(pl.program_id(0) == 0)
            def _():
                for r in out_refs[len(outs):]:
                    r[...] = jnp.zeros_like(r)

        compute(refs[:nr], refs[nr:at], out_refs)
        if side:
            @pl.when(pl.program_id(0) == ni - 1)
            def _():
                side.finish(*side_refs)

    anywhere = pl.BlockSpec(memory_space=pl.ANY)
    return _call(
        body, name=name, grid=(ni,),
        in_specs=[pl.BlockSpec((tm, r.shape[1]), lambda i: (i, 0)) for r in rows]
        + [pl.BlockSpec(wt.shape, lambda i: (0, 0)) for wt in weights] + [anywhere] * s_in,
        out_specs=[pl.BlockSpec((tm, width), lambda i: (i, 0)) for width, _ in outs]
        + [pl.BlockSpec((1, width), lambda i: (0, 0)) for width in sums] + [anywhere] * s_out,
        out_shape=[jax.ShapeDtypeStruct((t, width), dt) for width, dt in outs]
        + [jax.ShapeDtypeStruct((1, width), F32) for width in sums] + (list(side.out_shapes) if side else []),
        scratch_shapes=list(side.scratch) if side else [],
        compiler_params=_cp(vmem_mb),
    )(*rows, *weights, *(side.inputs if side else []))


def _mm_tn(name, a, b, scale=1.0, tm=1024, tn=1024):
    t, m = a.shape
    n = b.shape[1]
    tm, tn, tt = min(tm, m), min(tn, n), min(TT, t)
    nk = t // tt

    def body(a_ref, b_ref, o_ref):
        k = pl.program_id(2)

        @pl.when(k == 0)
        def _():
            o_ref[...] = jnp.zeros_like(o_ref)

        o_ref[...] += _dot_tn(a_ref[...].astype(BF16), b_ref[...].astype(BF16))
        if scale != 1.0:
            @pl.when(k == nk - 1)
            def _():
                o_ref[...] = o_ref[...] * scale

    return _call(
        body, name=name,
        grid=(m // tm, n // tn, nk),
        in_specs=[pl.BlockSpec((tt, tm), lambda i, j, k: (k, i)), pl.BlockSpec((tt, tn), lambda i, j, k: (k, j))],
        out_specs=pl.BlockSpec((tm, tn), lambda i, j, k: (i, j)),
        out_shape=jax.ShapeDtypeStruct((m, n), F32),
        compiler_params=_cp(),
    )(a, b)


def _rms_bwd_vals(xv, g, dn):
    r = lax.rsqrt(jnp.mean(xv * xv, axis=-1, keepdims=True) + EPS)
    xh = xv * r
    dxh = dn * g
    c = jnp.mean(dxh * xh, axis=-1, keepdims=True)
    return r * (dxh - xh * c), dn * xh


def _rms_bwd(name, x, gain, dn, dres):
    t, d = x.shape
    tm = min(ROW_TM, t)

    def body(x_ref, g_ref, dn_ref, dr_ref, dx_ref, dg_ref):
        @pl.when(pl.program_id(0) == 0)
        def _():
            dg_ref[...] = jnp.zeros_like(dg_ref)

        dx, dg = _rms_bwd_vals(x_ref[...], g_ref[...], dn_ref[...].astype(F32))
        dx_ref[...] = dr_ref[...] + dx
        dg_ref[...] += jnp.sum(dg, axis=0, keepdims=True)

    row = pl.BlockSpec((tm, d), lambda i: (i, 0))
    one = pl.BlockSpec((1, d), lambda i: (0, 0))
    return _call(
        body, name=name, grid=(t // tm,),
        in_specs=[row, one, row, row],
        out_specs=[row, one],
        out_shape=[jax.ShapeDtypeStruct((t, d), F32), jax.ShapeDtypeStruct((1, d), F32)],
        compiler_params=_cp(),
    )(x, gain, dn, dres)


def _colsum(name, x):
    t, n = x.shape
    tm = min(TM, t)

    def body(x_ref, o_ref):
        @pl.when(pl.program_id(0) == 0)
        def _():
            o_ref[...] = jnp.zeros_like(o_ref)

        o_ref[...] += jnp.sum(x_ref[...].astype(F32), axis=0, keepdims=True)

    return _call(
        body, name=name, grid=(t // tm,),
        in_specs=[pl.BlockSpec((tm, n), lambda i: (i, 0))],
        out_specs=pl.BlockSpec((1, n), lambda i: (0, 0)),
        out_shape=jax.ShapeDtypeStruct((1, n), F32),
        compiler_params=_cp(),
    )(x)


def _lora_norm_fwd(p_mla, gq, gkv):
    t = p_mla.shape[0]
    tm = min(ROW_TM, t)

    def body(p_ref, gq_ref, gkv_ref, q_ref, kv_ref):
        cq = p_ref[:, 0:Q_LORA]
        ckv = p_ref[:, Q_LORA:Q_LORA + KV_LORA]
        rq = lax.rsqrt(jnp.mean(cq * cq, axis=-1, keepdims=True) + EPS)
        rkv = lax.rsqrt(jnp.mean(ckv * ckv, axis=-1, keepdims=True) + EPS)
        q_ref[...] = (cq * rq * gq_ref[...]).astype(BF16)
        kv_ref[...] = (ckv * rkv * gkv_ref[...]).astype(BF16)

    return _call(
        body, name="lora_norm_fwd", grid=(t // tm,),
        in_specs=[pl.BlockSpec((tm, MLA_COLS), lambda i: (i, 0)),
                  pl.BlockSpec((1, Q_LORA), lambda i: (0, 0)), pl.BlockSpec((1, KV_LORA), lambda i: (0, 0))],
        out_specs=[pl.BlockSpec((tm, Q_LORA), lambda i: (i, 0)), pl.BlockSpec((tm, KV_LORA), lambda i: (i, 0))],
        out_shape=[jax.ShapeDtypeStruct((t, Q_LORA), BF16), jax.ShapeDtypeStruct((t, KV_LORA), BF16)],
        compiler_params=_cp(),
    )(p_mla, gq, gkv)


def _lora_norm_bwd(p_mla, gq, gkv, dcqn, dckvn, dkpe):
    t = p_mla.shape[0]
    tm = min(ROW_TM, t)

    def body(p_ref, gq_ref, gkv_ref, dq_ref, dkv_ref, dkpe_ref, dp_ref, dgq_ref, dgkv_ref):
        @pl.when(pl.program_id(0) == 0)
        def _():
            dgq_ref[...] = jnp.zeros_like(dgq_ref)
            dgkv_ref[...] = jnp.zeros_like(dgkv_ref)

        dcq, dgq = _rms_bwd_vals(p_ref[:, 0:Q_LORA], gq_ref[...], dq_ref[...])
        dckv, dgkv = _rms_bwd_vals(p_ref[:, Q_LORA:Q_LORA + KV_LORA], gkv_ref[...], dkv_ref[...])
        dp_ref[:, 0:Q_LORA] = dcq.astype(BF16)
        dp_ref[:, Q_LORA:Q_LORA + KV_LORA] = dckv.astype(BF16)
        dp_ref[:, Q_LORA + KV_LORA:MLA_COLS] = dkpe_ref[...].astype(BF16)
        dgq_ref[...] += jnp.sum(dgq, axis=0, keepdims=True)
        dgkv_ref[...] += jnp.sum(dgkv, axis=0, keepdims=True)

    return _call(
        body, name="lora_norm_bwd", grid=(t // tm,),
        in_specs=[pl.BlockSpec((tm, MLA_COLS), lambda i: (i, 0)),
                  pl.BlockSpec((1, Q_LORA), lambda i: (0, 0)), pl.BlockSpec((1, KV_LORA), lambda i: (0, 0)),
                  pl.BlockSpec((tm, Q_LORA), lambda i: (i, 0)), pl.BlockSpec((tm, KV_LORA), lambda i: (i, 0)),
                  pl.BlockSpec((tm, HK), lambda i: (i, 0))],
        out_specs=[pl.BlockSpec((tm, MLA_COLS), lambda i: (i, 0)),
                   pl.BlockSpec((1, Q_LORA), lambda i: (0, 0)), pl.BlockSpec((1, KV_LORA), lambda i: (0, 0))],
        out_shape=[jax.ShapeDtypeStruct((t, MLA_COLS), BF16), jax.ShapeDtypeStruct((1, Q_LORA), F32),
                   jax.ShapeDtypeStruct((1, KV_LORA), F32)],
        compiler_params=_cp(),
    )(p_mla, gq, gkv, dcqn, dckvn, dkpe)


def _cumsum_rows(x, row):
    for s in (1, 2, 4, 8, 16, 32):
        x = x + jnp.where(row >= s, pltpu.roll(x, s, 0), 0.0)
    return x


def _rcumsum_rows(x, row):
    for s in (1, 2, 4, 8, 16, 32):
        x = x + jnp.where(row < CHUNK - s, pltpu.roll(x, CHUNK - s, 0), 0.0)
    return x


def _hg_gates(qr, z, lb, row):
    q = _silu(qr)
    sg = _sig(z)
    f = lb + (1.0 - lb) * sg
    lf = jnp.log(f)
    k = (1.0 - lb) * (1.0 - sg)
    cum = _cumsum_rows(lf, row)
    mid = jnp.sum(jnp.where(row < CHUNK // 2, lf, 0.0), axis=0, keepdims=True)
    last = jnp.sum(lf, axis=0, keepdims=True)
    e_q = jnp.exp(jnp.minimum(cum - mid, EXP_CLAMP))
    e_k = jnp.exp(jnp.minimum(mid - cum, EXP_CLAMP))
    e_a = jnp.exp(cum)
    e_l = jnp.exp(last - cum)
    return q, sg, f, k, last, e_q, e_k, e_a, e_l


def _hgrn_fwd(p_hg, tab, gain):
    t = p_hg.shape[0]
    bt = min(HG_BT, t)
    nb, nc = t // bt, bt // CHUNK

    hpb = HG_HPB
    wide = hpb * HK

    def body(q_ref, f_ref, i_ref, g_ref, tab_ref, gain_ref, o_ref, ho_ref, st_ref, state):
        @pl.when(pl.program_id(1) == 0)
        def _():
            state[...] = jnp.zeros_like(state)

        row = lax.broadcasted_iota(jnp.int32, (CHUNK, HK), 0)
        tril = lax.broadcasted_iota(jnp.int32, (CHUNK, CHUNK), 0) >= lax.broadcasted_iota(jnp.int32, (CHUNK, CHUNK), 1)
        gain_v = gain_ref[...]

        def chunk(c, carry):
            sl = pl.ds(pl.multiple_of(c * CHUNK, CHUNK), CHUNK)
            for hh in range(hpb):
                ln = slice(hh * HK, (hh + 1) * HK)
                lb = _sig(tab_ref[0:1, ln] - tab_ref[1:2, ln])
                v = i_ref[sl, ln].astype(BF16)
                q, _, _, k, last, e_q, e_k, e_a, e_l = _hg_gates(q_ref[sl, ln], f_ref[sl, ln], lb, row)
                st = state[hh]
                st_ref[hh, c] = st
                p = jnp.where(tril, _dot_nt((q * e_q).astype(BF16), (k * e_k).astype(BF16)), 0.0)
                o = _dot(p.astype(BF16), v) + _dot_nt((q * e_a).astype(BF16), st.astype(BF16))
                state[hh] = jnp.exp(last) * st + _dot_tn(v, (k * e_l).astype(BF16))
                o_ref[sl, ln] = o
                r = lax.rsqrt(jnp.mean(o * o, axis=-1, keepdims=True) + EPS)
                ho_ref[sl, ln] = (o * r * gain_v * _silu(g_ref[sl, ln])).astype(BF16)
            return carry

        lax.fori_loop(0, nc, chunk, 0)

    def col(k):
        return pl.BlockSpec((bt, wide), lambda h, j, k=k: (j, k * (HEADS // hpb) + h))

    return _call(
        body, name="hgrn_fwd", grid=(HEADS // hpb, nb),
        in_specs=[col(0), col(1), col(2), col(3),
                  pl.BlockSpec((2, wide), lambda h, j: (0, h)), pl.BlockSpec((1, HK), lambda h, j: (0, 0))],
        out_specs=[pl.BlockSpec((bt, wide), lambda h, j: (j, h)), pl.BlockSpec((bt, wide), lambda h, j: (j, h)),
                   pl.BlockSpec((hpb, nc, HK, HK), lambda h, j: (h, j, 0, 0))],
        out_shape=[jax.ShapeDtypeStruct((t, D), F32), jax.ShapeDtypeStruct((t, D), BF16),
                   jax.ShapeDtypeStruct((HEADS, t // CHUNK, HK, HK), F32)],
        scratch_shapes=[pltpu.VMEM((hpb, HK, HK), F32)],
        compiler_params=_cp(),
    )(p_hg, p_hg, p_hg, p_hg, tab, gain)


def _hgrn_bwd(p_hg, tab, gain, o_raw, states, dho):
    t = p_hg.shape[0]
    bt = min(HG_BT, t)
    nb, nc = t // bt, bt // CHUNK
    hpb = HG_HPB
    wide = hpb * HK

    def body(q_ref, f_ref, i_ref, g_ref, tab_ref, gain_ref, o_ref, st_ref, dho_ref,
             dq_ref, df_ref, di_ref, dg_ref, dtab_ref, dgain_ref, dstate, dlb):
        h, j = pl.program_id(0), pl.program_id(1)

        @pl.when(jnp.logical_and(h == 0, j == 0))
        def _():
            dgain_ref[...] = jnp.zeros_like(dgain_ref)

        @pl.when(j == 0)
        def _():
            dstate[...] = jnp.zeros_like(dstate)
            dlb[...] = jnp.zeros_like(dlb)

        row = lax.broadcasted_iota(jnp.int32, (CHUNK, HK), 0)
        tril = lax.broadcasted_iota(jnp.int32, (CHUNK, CHUNK), 0) >= lax.broadcasted_iota(jnp.int32, (CHUNK, CHUNK), 1)
        gain_v = gain_ref[...]

        def chunk(cc, carry):
            c = nc - 1 - cc
            sl = pl.ds(pl.multiple_of(c * CHUNK, CHUNK), CHUNK)
            dgain = jnp.zeros((1, HK), F32)
            for hh in range(hpb):
                ln = slice(hh * HK, (hh + 1) * HK)
                lb = _sig(tab_ref[0:1, ln] - tab_ref[1:2, ln])
                qr = q_ref[sl, ln]
                v = i_ref[sl, ln].astype(BF16)
                gr = g_ref[sl, ln]
                q, sg, f, k, last, e_q, e_k, e_a, e_l = _hg_gates(qr, f_ref[sl, ln], lb, row)
                o = o_ref[sl, ln]
                r = lax.rsqrt(jnp.mean(o * o, axis=-1, keepdims=True) + EPS)
                oh = o * r
                dh = dho_ref[sl, ln].astype(F32)
                dnorm = dh * _silu(gr)
                dg_ref[sl, ln] = (dh * oh * gain_v * _dsilu(gr)).astype(BF16)
                dgain = dgain + jnp.sum(dnorm * oh, axis=0, keepdims=True)
                dxh = dnorm * gain_v
                do = (r * (dxh - oh * jnp.mean(dxh * oh, axis=-1, keepdims=True))).astype(BF16)
                st0 = st_ref[hh, c]
                st0_b = st0.astype(BF16)
                ds1 = dstate[hh]
                ds1_b = ds1.astype(BF16)
                qt = (q * e_q).astype(BF16)
                kt = (k * e_k).astype(BF16)
                qd = (q * e_a).astype(BF16)
                kd = (k * e_l).astype(BF16)
                p = jnp.where(tril, _dot_nt(qt, kt), 0.0).astype(BF16)
                dp = jnp.where(tril, _dot_nt(do, v), 0.0).astype(BF16)
                dv = _dot_tn(p, do) + _dot_nt(kd, ds1_b)
                dqt = _dot(dp, kt)
                dkt = _dot_tn(dp, qt)
                dq_inter = _dot(do, st0_b) * e_a
                dk_inter = _dot(v, ds1_b) * e_l
                dq = dqt * e_q + dq_inter
                dk = dkt * e_k + dk_inter
                e_last = jnp.exp(last)
                dstate[hh] = _dot_tn(do, qd) + e_last * ds1
                dlast = (jnp.sum(k * dk_inter, axis=0, keepdims=True)
                         + e_last * jnp.sum(ds1 * st0, axis=0, keepdims=True))
                da = (qt.astype(F32) * dqt - kt.astype(F32) * dkt + q * dq_inter - k * dk_inter
                      + jnp.where(row == CHUNK - 1, dlast, 0.0))
                dlf = _rcumsum_rows(da, row)
                dfv = dlf / f - dk
                df_ref[sl, ln] = (dfv * (1.0 - lb) * sg * (1.0 - sg)).astype(BF16)
                dlb[:, ln] += jnp.sum(dfv * (1.0 - sg), axis=0, keepdims=True)
                dq_ref[sl, ln] = (dq * _dsilu(qr)).astype(BF16)
                di_ref[sl, ln] = dv.astype(BF16)
            dgain_ref[...] += dgain
            return carry

        lax.fori_loop(0, nc, chunk, 0)

        @pl.when(j == nb - 1)
        def _():
            lb = _sig(tab_ref[0:1, :] - tab_ref[1:2, :])
            d0 = dlb[...] * lb * (1.0 - lb)
            dtab_ref[0:1, :] = d0
            dtab_ref[1:2, :] = -d0

    def col(k):
        return pl.BlockSpec((bt, wide), lambda h, j, k=k: (nb - 1 - j, k * (HEADS // hpb) + h))

    tok = pl.BlockSpec((bt, wide), lambda h, j: (nb - 1 - j, h))
    return _call(
        body, name="hgrn_bwd", grid=(HEADS // hpb, nb),
        in_specs=[col(0), col(1), col(2), col(3),
                  pl.BlockSpec((2, wide), lambda h, j: (0, h)), pl.BlockSpec((1, HK), lambda h, j: (0, 0)),
                  tok, pl.BlockSpec((hpb, nc, HK, HK), lambda h, j: (h, nb - 1 - j, 0, 0)), tok],
        out_specs=[tok, tok, tok, tok,
                   pl.BlockSpec((2, wide), lambda h, j: (0, h)), pl.BlockSpec((1, HK), lambda h, j: (0, 0))],
        out_shape=[jax.ShapeDtypeStruct((t, D), BF16)] * 4
        + [jax.ShapeDtypeStruct((2, D), F32), jax.ShapeDtypeStruct((1, HK), F32)],
        scratch_shapes=[pltpu.VMEM((hpb, HK, HK), F32), pltpu.VMEM((1, wide), F32)],
        compiler_params=_cp(),
    )(p_hg, p_hg, p_hg, p_hg, tab, gain, o_raw, states, dho)


def _rope_tables(pos):
    t = pos.shape[0]
    tm = min(ROW_TM, t)
    inv = np.zeros((1, HK), np.float32)
    freq = (ROPE_THETA ** (-np.arange(0, ROPE, 2, dtype=np.float32) / ROPE)).astype(np.float32)
    inv[0, 0:ROPE // 2] = freq
    inv[0, ROPE // 2:ROPE] = freq
    sign = np.zeros((1, HK), np.float32)
    sign[0, 0:ROPE // 2] = -1.0
    sign[0, ROPE // 2:ROPE] = 1.0

    def body(pos_ref, inv_ref, sign_ref, cos_ref, sin_ref):
        ang = pos_ref[...].astype(F32) * inv_ref[...]
        cos_ref[...] = jnp.cos(ang)
        sin_ref[...] = jnp.sin(ang) * sign_ref[...]

    one = pl.BlockSpec((1, HK), lambda i: (0, 0))
    row = pl.BlockSpec((tm, HK), lambda i: (i, 0))
    return _call(
        body, name="rope_tables", grid=(t // tm,),
        in_specs=[pl.BlockSpec((tm, 1), lambda i: (i, 0)), one, one],
        out_specs=[row, row],
        out_shape=[jax.ShapeDtypeStruct((t, HK), F32)] * 2,
        compiler_params=_cp(),
    )(pos, jnp.asarray(inv), jnp.asarray(sign))


def _rope(x, cos, sin_signed):
    r = lax.broadcasted_iota(jnp.int32, (HK, HK), 0)
    c = lax.broadcasted_iota(jnp.int32, (HK, HK), 1)
    half = ROPE // 2
    swap = jnp.logical_or(jnp.logical_and(c < half, r == c + half),
                          jnp.logical_and(jnp.logical_and(c >= half, c < ROPE), r == c - half))
    return x * cos + _dot_split(x, swap.astype(BF16)) * sin_signed


def _dot_split(x, m):
    hi = x.astype(BF16)
    lo = (x - hi.astype(F32)).astype(BF16)
    return _dot(hi, m) + _dot(lo, m)


def _lane_sum(x):
    return _dot_split(x, jnp.ones((HK, HK), BF16))


def _head_norm(xn, xr):
    r = lax.rsqrt(_lane_sum(xn * xn + xr * xr) * (1.0 / QK) + EPS)
    return xn * r, xr * r, r


def _head_norm_bwd(xn, xr, g_n, g_r, dn, dr):
    hn, hr, r = _head_norm(xn, xr)
    dxn, dxr = dn * g_n, dr * g_r
    c = _lane_sum(dxn * hn + dxr * hr) * (1.0 / QK)
    return r * (dxn - hn * c), r * (dxr - hr * c), dn * hn, dr * hr


def _mla_prep_fwd(qf, kv, p_mla, cos, sin, gq, gk):
    t = qf.shape[0]
    tm = min(ROW_TM, t)

    def body(qf_ref, kv_ref, kpe_ref, cos_ref, sin_ref, gq_ref, gk_ref, q_ref, k_ref, v_ref):
        cos_v, sin_v = cos_ref[...], sin_ref[...]
        kpe = kpe_ref[...]
        for h in range(HEADS):
            lo, mid, hi = h * QKP, h * QKP + HK, (h + 1) * QKP
            qn, qr, _ = _head_norm(qf_ref[:, lo:mid], qf_ref[:, mid:hi])
            q_ref[h, :, 0:HK] = (qn * gq_ref[:, 0:HK] * (SCALE * LOG2E)).astype(BF16)
            q_ref[h, :, HK:QKP] = (_rope(qr * gq_ref[:, HK:QKP], cos_v, sin_v) * (SCALE * LOG2E)).astype(BF16)
            kn, kr, _ = _head_norm(kv_ref[:, lo:mid], kpe)
            k_ref[h, :, 0:HK] = (kn * gk_ref[:, 0:HK]).astype(BF16)
            k_ref[h, :, HK:QKP] = _rope(kr * gk_ref[:, HK:QKP], cos_v, sin_v).astype(BF16)
            v_ref[h, :, 0:HK] = kv_ref[:, mid:hi].astype(BF16)
            v_ref[h, :, HK:QKP] = jnp.full((tm, HK), -1.0, BF16)

    head = pl.BlockSpec((tm, HEADS * QKP), lambda i: (i, 0))
    tok = pl.BlockSpec((tm, HK), lambda i: (i, 0))
    gain = pl.BlockSpec((1, QKP), lambda i: (0, 0))
    return _call(
        body, name="mla_prep_fwd", grid=(t // tm,),
        in_specs=[head, head, pl.BlockSpec((tm, HK), lambda i: (i, MLA_COLS // HK - 1)), tok, tok, gain, gain],
        out_specs=[pl.BlockSpec((HEADS, tm, QKP), lambda i: (0, i, 0)),
                   pl.BlockSpec((HEADS, tm, QKP), lambda i: (0, i, 0)),
                   pl.BlockSpec((HEADS, tm, QKP), lambda i: (0, i, 0))],
        out_shape=[jax.ShapeDtypeStruct((HEADS, t, QKP), BF16), jax.ShapeDtypeStruct((HEADS, t, QKP), BF16),
                   jax.ShapeDtypeStruct((HEADS, t, QKP), BF16)],
        compiler_params=_cp(),
    )(qf, kv, p_mla, cos, sin, gq, gk)


def _mla_prep_bwd(qf, kv, p_mla, cos, sin, gq, gk, dq, dk, dv):
    t = qf.shape[0]
    tm = min(ROW_TM, t)

    def body(qf_ref, kv_ref, kpe_ref, cos_ref, sin_ref, gq_ref, gk_ref, dq_ref, dk_ref, dv_ref,
             dqf_ref, dkv_ref, dkpe_ref, dgq_ref, dgk_ref):
        @pl.when(pl.program_id(0) == 0)
        def _():
            dgq_ref[...] = jnp.zeros_like(dgq_ref)
            dgk_ref[...] = jnp.zeros_like(dgk_ref)

        cos_v, sin_v = cos_ref[...], -sin_ref[...]
        kpe = kpe_ref[...]
        gqn, gqr, gkn, gkr = gq_ref[:, 0:HK], gq_ref[:, HK:QKP], gk_ref[:, 0:HK], gk_ref[:, HK:QKP]
        dkpe = jnp.zeros((tm, HK), F32)
        dgq_n, dgq_r, dgk_n, dgk_r = [jnp.zeros((1, HK), F32) for _ in range(4)]
        for h in range(HEADS):
            lo, mid, hi = h * QKP, h * QKP + HK, (h + 1) * QKP
            dqn = dq_ref[h, :, 0:HK].astype(F32) * SCALE
            dqr = _rope(dq_ref[h, :, HK:QKP].astype(F32), cos_v, sin_v) * SCALE
            a, b, ga, gb = _head_norm_bwd(qf_ref[:, lo:mid], qf_ref[:, mid:hi], gqn, gqr, dqn, dqr)
            dqf_ref[:, lo:mid] = a.astype(BF16)
            dqf_ref[:, mid:hi] = b.astype(BF16)
            dgq_n = dgq_n + jnp.sum(ga, axis=0, keepdims=True)
            dgq_r = dgq_r + jnp.sum(gb, axis=0, keepdims=True)
            dkn = dk_ref[h, :, 0:HK].astype(F32) * LN2
            dkr = _rope(dk_ref[h, :, HK:QKP].astype(F32), cos_v, sin_v) * LN2
            a, b, ga, gb = _head_norm_bwd(kv_ref[:, lo:mid], kpe, gkn, gkr, dkn, dkr)
            dkv_ref[:, lo:mid] = a.astype(BF16)
            dkv_ref[:, mid:hi] = dv_ref[h].astype(BF16)
            dkpe = dkpe + b
            dgk_n = dgk_n + jnp.sum(ga, axis=0, keepdims=True)
            dgk_r = dgk_r + jnp.sum(gb, axis=0, keepdims=True)
        dkpe_ref[...] = dkpe
        dgq_ref[:, 0:HK] += dgq_n
        dgq_ref[:, HK:QKP] += dgq_r
        dgk_ref[:, 0:HK] += dgk_n
        dgk_ref[:, HK:QKP] += dgk_r

    head = pl.BlockSpec((tm, HEADS * QKP), lambda i: (i, 0))
    tok = pl.BlockSpec((tm, HK), lambda i: (i, 0))
    gain = pl.BlockSpec((1, QKP), lambda i: (0, 0))
    hq = pl.BlockSpec((HEADS, tm, QKP), lambda i: (0, i, 0))
    return _call(
        body, name="mla_prep_bwd", grid=(t // tm,),
        in_specs=[head, head, pl.BlockSpec((tm, HK), lambda i: (i, MLA_COLS // HK - 1)), tok, tok, gain, gain,
                  hq, hq, pl.BlockSpec((HEADS, tm, HK), lambda i: (0, i, 0))],
        out_specs=[head, head, tok, gain, gain],
        out_shape=[jax.ShapeDtypeStruct((t, HEADS * QKP), BF16), jax.ShapeDtypeStruct((t, HEADS * QKP), BF16),
                   jax.ShapeDtypeStruct((t, HK), F32), jax.ShapeDtypeStruct((1, QKP), F32),
                   jax.ShapeDtypeStruct((1, QKP), F32)],
        compiler_params=_cp(),
    )(qf, kv, p_mla, cos, sin, gq, gk, dq, dk, dv)


def _chunk_mask(row0, rows, cols):
    r = lax.broadcasted_iota(jnp.int32, (rows, cols), 0) + row0
    c = lax.broadcasted_iota(jnp.int32, (rows, cols), 1)
    return jnp.right_shift(r, 6) >= jnp.right_shift(c, 6)


def _flash_fwd(q, k, v, side=None):
    t = q.shape[1]
    tq = min(TQ, t)
    nq = t // tq
    sub = min(SUBQ, tq)
    pairs = [(i, j) for i in range(nq) for j in range(i + 1)]
    qi = jnp.asarray([p[0] for p in pairs], jnp.int32)
    kj = jnp.asarray([p[1] for p in pairs], jnp.int32)
    s_in = len(side.inputs) if side else 0
    s_out = len(side.out_shapes) if side else 0

    def body(qi_ref, kj_ref, q_ref, k_ref, v_ref, *rest):
        o_ref, lse_ref = rest[s_in:s_in + 2]
        m_s, acc_s = rest[s_in + 2 + s_out:s_in + 4 + s_out]
        side_refs = list(rest[:s_in]) + list(rest[s_in + 2:s_in + 2 + s_out]) + list(rest[s_in + 4 + s_out:])
        n = pl.program_id(1)
        i, j = qi_ref[n], kj_ref[n]
        if side:
            @pl.when(jnp.logical_and(pl.program_id(0) == 0, n == 0))
            def _():
                side.start(*side_refs)

        @pl.when(j == 0)
        def _():
            m_s[...] = jnp.full_like(m_s, NEG)
            acc_s[...] = jnp.zeros_like(acc_s)

        def step(diag):
            subs = range(tq // sub)
            width = [(r + 1) * sub if diag else tq for r in subs]
            logits = [_dot_nt(q_ref[r * sub:(r + 1) * sub, :], k_ref[0:width[r], :]) for r in subs]
            for r in subs:
                rows = slice(r * sub, (r + 1) * sub)
                cols = width[r]
                s = logits[r]
                if diag:
                    s = jnp.where(_chunk_mask(r * sub, sub, cols), s, NEG)
                m_old = m_s[rows, :]
                m_new = jnp.maximum(m_old, jnp.max(s, axis=-1, keepdims=True))
                alpha = jnp.exp2(m_old - m_new)
                p = jnp.exp2((s - jnp.tile(m_new, (1, cols // HK))).astype(BF16))
                acc_s[rows, :] = jnp.tile(alpha, (1, 2)) * acc_s[rows, :] + _dot(p, v_ref[0:cols, :])
                m_s[rows, :] = m_new

        @pl.when(j < i)
        def _():
            step(False)

        @pl.when(j == i)
        def _():
            step(True)
            l = -acc_s[:, HK:QKP]
            o_ref[...] = (acc_s[:, 0:HK] / l).astype(BF16)
            lse_ref[...] = m_s[...] + jnp.log(l) * LOG2E

        if side:
            @pl.when(jnp.logical_and(pl.program_id(0) == HEADS - 1, n == len(pairs) - 1))
            def _():
                side.finish(*side_refs)

    anywhere = pl.BlockSpec(memory_space=pl.ANY)
    grid_spec = pltpu.PrefetchScalarGridSpec(
        num_scalar_prefetch=2, grid=(HEADS, len(pairs)),
        in_specs=[pl.BlockSpec((None, tq, QKP), lambda h, n, qi, kj: (h, qi[n], 0)),
                  pl.BlockSpec((None, tq, QKP), lambda h, n, qi, kj: (h, kj[n], 0)),
                  pl.BlockSpec((None, tq, QKP), lambda h, n, qi, kj: (h, kj[n], 0))] + [anywhere] * s_in,
        out_specs=[pl.BlockSpec((tq, HK), lambda h, n, qi, kj: (qi[n], h)),
                   pl.BlockSpec((None, tq, HK), lambda h, n, qi, kj: (h, qi[n], 0))] + [anywhere] * s_out,
        scratch_shapes=[pltpu.VMEM((tq, HK), F32), pltpu.VMEM((tq, QKP), F32)] + (list(side.scratch) if side else []),
    )
    return _call(
        body, name="flash_fwd", grid_spec=grid_spec,
        out_shape=[jax.ShapeDtypeStruct((t, D), BF16), jax.ShapeDtypeStruct((HEADS, t, HK), F32)]
        + (list(side.out_shapes) if side else []),
        compiler_params=_cp(),
    )(qi, kj, q, k, v, *(side.inputs if side else []))


def _attn_do(do, o):
    t = do.shape[0]
    tm = min(TM, t)

    def body(do_ref, o_ref, d_ref):
        lane = lax.broadcasted_iota(jnp.int32, (tm, HK), 1)
        for h in range(HEADS):
            ln = slice(h * HK, (h + 1) * HK)
            dov = do_ref[:, ln]
            d = jnp.sum(dov.astype(F32) * o_ref[:, ln].astype(F32), axis=-1, keepdims=True)
            hi = d.astype(BF16).astype(F32)
            d_ref[h, :, 0:HK] = dov
            d_ref[h, :, HK:QKP] = jnp.where(lane == 0, hi, jnp.where(lane == 1, d - hi, 0.0)).astype(BF16)

    blk = pl.BlockSpec((tm, D), lambda i: (i, 0))
    return _call(
        body, name="attn_do", grid=(t // tm,),
        in_specs=[blk, blk],
        out_specs=pl.BlockSpec((HEADS, tm, QKP), lambda i: (0, i, 0)),
        out_shape=jax.ShapeDtypeStruct((HEADS, t, QKP), BF16),
        compiler_params=_cp(),
    )(do, o)


def _flash_bwd(q, k, v, lse, do):
    t = q.shape[1]
    tq = min(TQ, t)
    nq = t // tq
    sub = min(SUBQ, tq)
    pairs = [(i, j) for j in range(nq) for i in range(j, nq)]
    qi = jnp.asarray([p[0] for p in pairs], jnp.int32)
    kj = jnp.asarray([p[1] for p in pairs], jnp.int32)
    npairs = len(pairs)

    def body(qi_ref, kj_ref, q_ref, k_ref, v_ref, lse_ref, do_ref, dq_ref, dk_ref, dv_ref):
        n = pl.program_id(1)
        i, j = qi_ref[n], kj_ref[n]

        @pl.when(n == 0)
        def _():
            dq_ref[...] = jnp.zeros_like(dq_ref)

        @pl.when(i == j)
        def _():
            dk_ref[...] = jnp.zeros_like(dk_ref)
            dv_ref[...] = jnp.zeros_like(dv_ref)

        def step(diag):
            for r in range(tq // sub):
                rows = slice(r * sub, (r + 1) * sub)
                cols = (r + 1) * sub if diag else tq
                qv, kv_ = q_ref[rows, :], k_ref[0:cols, :]
                p = jnp.exp2(_dot_nt(qv, kv_) - jnp.tile(lse_ref[rows, :], (1, cols // HK)))
                if diag:
                    p = jnp.where(_chunk_mask(r * sub, sub, cols), p, 0.0)
                dp_less_delta = _dot_nt(do_ref[rows, :], v_ref[0:cols, :])
                ds = (p * dp_less_delta).astype(BF16)
                dv_ref[0:cols, :] += _dot_tn(p.astype(BF16), do_ref[rows, 0:HK])
                dk_ref[0:cols, :] += _dot_tn(ds, qv)
                dq_rows = pl.ds(pl.multiple_of(i * tq + r * sub, sub), sub)
                dq_ref[dq_rows, :] += _dot(ds, kv_)

        @pl.when(j < i)
        def _():
            step(False)

        @pl.when(j == i)
        def _():
            step(True)

    grid_spec = pltpu.PrefetchScalarGridSpec(
        num_scalar_prefetch=2, grid=(HEADS, npairs),
        in_specs=[pl.BlockSpec((None, tq, QKP), lambda h, n, qi, kj: (h, qi[n], 0)),
                  pl.BlockSpec((None, tq, QKP), lambda h, n, qi, kj: (h, kj[n], 0)),
                  pl.BlockSpec((None, tq, QKP), lambda h, n, qi, kj: (h, kj[n], 0)),
                  pl.BlockSpec((None, tq, HK), lambda h, n, qi, kj: (h, qi[n], 0)),
                  pl.BlockSpec((None, tq, QKP), lambda h, n, qi, kj: (h, qi[n], 0))],
        out_specs=[pl.BlockSpec((None, t, QKP), lambda h, n, qi, kj: (h, 0, 0)),
                   pl.BlockSpec((None, tq, QKP), lambda h, n, qi, kj: (h, kj[n], 0)),
                   pl.BlockSpec((None, tq, HK), lambda h, n, qi, kj: (h, kj[n], 0))],
    )
    return _call(
        body, name="flash_bwd", grid_spec=grid_spec,
        out_shape=[jax.ShapeDtypeStruct((HEADS, t, QKP), F32), jax.ShapeDtypeStruct((HEADS, t, QKP), F32),
                   jax.ShapeDtypeStruct((HEADS, t, HK), F32)],
        compiler_params=_cp(56),
    )(qi, kj, q, k, v, lse, do)


def _adamw(name, w, g, m, v):
    r, c = w.shape
    tr = r if r <= 256 else next(k for k in (256, 352, 384) if r % k == 0)

    def body(w_ref, g_ref, m_ref, v_ref, d_ref, nm_ref, nv_ref):
        gv = g_ref[...]
        nm = ADAM_B1 * m_ref[...] + (1.0 - ADAM_B1) * gv
        nv = ADAM_B2 * v_ref[...] + (1.0 - ADAM_B2) * (gv * gv)
        m_hat = nm / (1.0 - ADAM_B1 ** ADAM_STEP)
        v_hat = nv / (1.0 - ADAM_B2 ** ADAM_STEP)
        d_ref[...] = -ADAM_LR * (m_hat / (jnp.sqrt(v_hat) + ADAM_EPS) + ADAM_WD * w_ref[...])
        nm_ref[...] = nm
        nv_ref[...] = nv

    blk = pl.BlockSpec((tr, c), lambda i: (i, 0))
    return _call(
        body, name=name, grid=(r // tr,),
        in_specs=[blk] * 4, out_specs=[blk] * 3,
        out_shape=[jax.ShapeDtypeStruct((r, c), F32)] * 3,
        compiler_params=_cp(),
    )(w, g, m, v)


def _place():
    return lax.axis_index("x"), lax.axis_index("y"), lax.axis_index("c")


def _other_chips(x, y):
    return [(1 - x, y), (x, 1 - y), (1 - x, 1 - y)]


class _Exchange:
    inputs = ()
    out_shapes = ()
    scratch = ()

    def start(self, *refs):
        raise NotImplementedError

    def finish(self, *refs):
        raise NotImplementedError

    def alone(self, name):
        def body(*refs):
            self.start(*refs)
            self.finish(*refs)

        anywhere = pl.BlockSpec(memory_space=pl.ANY)
        return _call(
            body, name=name,
            in_specs=[anywhere] * len(self.inputs), out_specs=[anywhere] * len(self.out_shapes),
            out_shape=list(self.out_shapes), scratch_shapes=list(self.scratch),
        )(*self.inputs)


class _GatherWeights(_Exchange):
    def __init__(self, shard):
        self.r = shard.shape[0]
        self.inputs = (shard,)
        self.out_shapes = (jax.ShapeDtypeStruct((4, self.r, PACK_W), shard.dtype),)
        self.scratch = (pltpu.SemaphoreType.DMA((6,)), pltpu.SemaphoreType.DMA((6,)))

    def gathered(self, got, k):
        return lax.dynamic_update_slice(got, self.inputs[0][None], (k, 0, 0))

    def _copies(self, s_ref, g_ref, send_sems, recv_sems):
        half = self.r // 2
        x, y, c = _place()
        chips = _other_chips(x, y)

        def rows(px, py, pc):
            return g_ref.at[2 * px + py, pl.ds(pc * half, half), :]

        def copy(k, block, to, src=None):
            return pltpu.make_async_remote_copy(
                src_ref=rows(*block) if src is None else src, dst_ref=rows(*block),
                send_sem=send_sems.at[k], recv_sem=recv_sems.at[k], device_id=to, device_id_type=MESH)

        first = [copy(j, (x, y, c), (*chip, c), src=s_ref.at[pl.ds(c * half, half), :]) for j, chip in enumerate(chips)]
        passed = [copy(3 + j, (*chip, c), (x, y, 1 - c)) for j, chip in enumerate(chips)]
        landed = [copy(j, (*chip, c), (x, y, c)) for j, chip in enumerate(chips)]
        landed += [copy(3 + j, (*chip, 1 - c), (x, y, c)) for j, chip in enumerate(chips)]
        return first, passed, landed

    def start(self, *refs):
        first, _, _ = self._copies(*refs)
        for cp in first:
            cp.start()

    def finish(self, *refs):
        first, passed, landed = self._copies(*refs)
        for j in range(3):
            landed[j].wait_recv()
            passed[j].start()
        for j in range(3):
            landed[3 + j].wait_recv()
        for cp in first + passed:
            cp.wait_send()


def _swap_halves(name, gp):
    r = gp.shape[1]
    half = r // 2

    def body(g_ref, o_ref, send_sem, recv_sem):
        x, y, c = _place()
        cp = pltpu.make_async_remote_copy(
            src_ref=g_ref.at[:, pl.ds((1 - c) * half, half), :], dst_ref=o_ref,
            send_sem=send_sem, recv_sem=recv_sem, device_id=(x, y, 1 - c), device_id_type=MESH)
        cp.start()
        cp.wait()

    return _call(
        body, name=name,
        in_specs=[pl.BlockSpec(memory_space=pl.ANY)],
        out_specs=pl.BlockSpec(memory_space=pl.ANY),
        out_shape=jax.ShapeDtypeStruct((4, half, PACK_W), gp.dtype),
        scratch_shapes=[pltpu.SemaphoreType.DMA, pltpu.SemaphoreType.DMA],
    )(gp)


def _chip_sum(name, gp, got, c_arr):
    half = got.shape[1]
    tr = ADD_ROWS
    nb = half // tr

    def body(c_ref, a_ref, b_ref, o_ref, ob_ref):
        s = a_ref[...] + b_ref[...]
        o_ref[...] = s
        ob_ref[...] = s.astype(BF16)

    grid_spec = pltpu.PrefetchScalarGridSpec(
        num_scalar_prefetch=1, grid=(4, nb),
        in_specs=[pl.BlockSpec((None, tr, PACK_W), lambda s, i, c: (s, c[0] * nb + i, 0)),
                  pl.BlockSpec((None, tr, PACK_W), lambda s, i, c: (s, i, 0))],
        out_specs=[pl.BlockSpec((None, tr, PACK_W), lambda s, i, c: (s, i, 0)),
                   pl.BlockSpec((None, tr, PACK_W), lambda s, i, c: (s, i, 0))],
    )
    return _call(
        body, name=name, grid_spec=grid_spec,
        out_shape=[jax.ShapeDtypeStruct(got.shape, F32), jax.ShapeDtypeStruct(got.shape, BF16)],
        compiler_params=_cp(),
    )(c_arr, gp, got)


class _ScatterChipSums(_Exchange):
    def __init__(self, cs):
        self.inputs = (cs,)
        self.out_shapes = (jax.ShapeDtypeStruct((3,) + cs.shape[1:], cs.dtype),)
        self.scratch = (pltpu.SemaphoreType.DMA((3,)), pltpu.SemaphoreType.DMA((3,)))

    def _copies(self, s_ref, o_ref, send_sems, recv_sems):
        x, y, c = _place()
        return [pltpu.make_async_remote_copy(
            src_ref=s_ref.at[2 * px + py], dst_ref=o_ref.at[j],
            send_sem=send_sems.at[j], recv_sem=recv_sems.at[j], device_id=(px, py, c), device_id_type=MESH)
            for j, (px, py) in enumerate(_other_chips(x, y))]

    def start(self, *refs):
        for cp in self._copies(*refs):
            cp.start()

    def finish(self, *refs):
        for cp in self._copies(*refs):
            cp.wait()


def _shard_sum(name, cs, got, kc_arr):
    h = cs.shape[1]
    tr = ADD_ROWS
    nb = h // tr

    def body(k_ref, a_ref, b_ref, o_ref):
        o_ref[...] = ((a_ref[...] + b_ref[0].astype(F32)) + b_ref[1].astype(F32)) + b_ref[2].astype(F32)

    grid_spec = pltpu.PrefetchScalarGridSpec(
        num_scalar_prefetch=1, grid=(nb,),
        in_specs=[pl.BlockSpec((None, tr, PACK_W), lambda i, k: (k[0], i, 0)),
                  pl.BlockSpec((3, tr, PACK_W), lambda i, k: (0, i, 0))],
        out_specs=pl.BlockSpec((tr, PACK_W), lambda i, k: (k[1] * nb + i, 0)),
    )
    return _call(
        body, name=name, grid_spec=grid_spec,
        out_shape=jax.ShapeDtypeStruct((2 * h, PACK_W), F32),
        compiler_params=_cp(),
    )(kc_arr, cs, got)


def _join_halves(name, both):
    h = both.shape[0] // 2

    def body(m_ref, o_ref, send_sem, recv_sem):
        x, y, c = _place()
        cp = pltpu.make_async_remote_copy(
            src_ref=m_ref.at[pl.ds(c * h, h), :], dst_ref=o_ref.at[pl.ds(c * h, h), :],
            send_sem=send_sem, recv_sem=recv_sem, device_id=(x, y, 1 - c), device_id_type=MESH)
        cp.start()
        cp.wait_send()
        pltpu.make_async_remote_copy(
            src_ref=m_ref.at[pl.ds(c * h, h), :], dst_ref=o_ref.at[pl.ds((1 - c) * h, h), :],
            send_sem=send_sem, recv_sem=recv_sem, device_id=(x, y, 1 - c), device_id_type=MESH).wait_recv()

    return _call(
        body, name=name,
        in_specs=[pl.BlockSpec(memory_space=pl.ANY)],
        out_specs=pl.BlockSpec(memory_space=pl.ANY),
        out_shape=jax.ShapeDtypeStruct(both.shape, both.dtype),
        input_output_aliases={0: 0},
        scratch_shapes=[pltpu.SemaphoreType.DMA, pltpu.SemaphoreType.DMA],
    )(both)


def _all_reduce_small(v):
    r = v.shape[0]

    def body(v_ref, o_ref, buf, send_sems, recv_sems):
        x, y, c = _place()
        me = 4 * x + 2 * y + c
        buf[me] = v_ref[...]
        cps = []
        for k in range(1, 8):
            peer = (x ^ (k >> 2), y ^ ((k >> 1) & 1), c ^ (k & 1))
            cps.append(pltpu.make_async_remote_copy(
                src_ref=v_ref, dst_ref=buf.at[me],
                send_sem=send_sems.at[k - 1], recv_sem=recv_sems.at[k - 1], device_id=peer, device_id_type=MESH))
        for cp in cps:
            cp.start()
        for k in range(1, 8):
            pltpu.make_async_remote_copy(
                src_ref=v_ref, dst_ref=buf.at[me ^ k],
                send_sem=send_sems.at[k - 1], recv_sem=recv_sems.at[k - 1],
                device_id=(x, y, c), device_id_type=MESH).wait_recv()
        for cp in cps:
            cp.wait_send()
        acc = buf[0]
        for k in range(1, 8):
            acc = acc + buf[k]
        o_ref[...] = acc

    return _call(
        body, name="all_reduce_small",
        in_specs=[pl.BlockSpec(memory_space=pltpu.VMEM)],
        out_specs=pl.BlockSpec(memory_space=pltpu.VMEM),
        out_shape=jax.ShapeDtypeStruct((r, 128), F32),
        scratch_shapes=[pltpu.VMEM((8, r, 128), F32), pltpu.SemaphoreType.DMA((7,)), pltpu.SemaphoreType.DMA((7,))],
    )(v)


def _group(names):
    return tuple(e for e in BIG if e[0] in names)


def _pack(shards, dtype):
    return jnp.concatenate([s.astype(dtype).reshape(-1, PACK_W) for s in shards], axis=0)


def _unpack_full(g, group):
    out, at = {}, 0
    for name, rows, cols, axis in group:
        n = rows * cols // 4 // PACK_W
        blk = g[:, at:at + n, :]
        at += n
        if axis == 1:
            out[name] = blk.reshape(4, rows, cols // 4).transpose(1, 0, 2).reshape(rows, cols)
        else:
            out[name] = blk.reshape(rows, cols)
    return out


def _pack_grads(grads, group):
    parts = []
    for name, rows, cols, axis in group:
        g = grads[name]
        if axis == 1:
            g = g.reshape(rows, 4, cols // 4).transpose(1, 0, 2)
        parts.append(g.reshape(4, -1, PACK_W))
    rows_total = sum(p.shape[1] for p in parts)
    pad = -rows_total % PACK_ALIGN
    if pad:
        parts.append(jnp.zeros((4, pad, PACK_W), F32))
    return jnp.concatenate(parts, axis=1)


def _unpack_shard(s, group):
    out, at = {}, 0
    for name, rows, cols, axis in group:
        n = rows * cols // 4 // PACK_W
        shape = (rows, cols // 4) if axis == 1 else (rows // 4, cols)
        out[name] = s[at:at + n, :].reshape(shape)
        at += n
    return out


def _pack_small(parts):
    flat = jnp.concatenate([p.reshape(-1) for p in parts])
    pad = -flat.shape[0] % 1024
    return jnp.concatenate([flat, jnp.zeros((pad,), F32)]).reshape(-1, 128)


def _ffn_in(tag, h, gain, w_in, side=None):
    t = h.shape[0]

    def compute_in(rows, weights, outs):
        hv, w_ref = rows[0][...], weights[0]
        r = lax.rsqrt(jnp.mean(hv * hv, axis=-1, keepdims=True) + EPS)
        a = (hv * r * weights[1][...]).astype(BF16)
        outs[0][...] = a
        for j in range(DFF // FFN_CHUNK):
            cols = slice(j * FFN_CHUNK, (j + 1) * FFN_CHUNK)
            gate = _dot(a, w_ref[:, cols])
            up = _dot(a, w_ref[:, DFF + j * FFN_CHUNK:DFF + (j + 1) * FFN_CHUNK])
            outs[1][:, cols] = gate.astype(BF16)
            outs[2][:, cols] = up.astype(BF16)
            outs[3][:, cols] = (_silu(gate) * up).astype(BF16)

    return _rows_call(tag + "_in", [h], [w_in, gain], [(D, BF16)] + [(DFF, BF16)] * 3, compute_in, min(FFN_TM, t),
                      side=side)


def _ffn_out(tag, act, h, w_out, next_gain, target=None):
    t = h.shape[0]
    tm = min(FFN_TM, t)

    def compute_out(rows, weights, outs):
        hn = rows[1][...] + 0.5 * _dot(rows[0][...], weights[0][...])
        g = weights[1][...]
        r = lax.rsqrt(jnp.mean(hn * hn, axis=-1, keepdims=True) + EPS)
        xh = hn * r
        if target is None:
            outs[0][...] = hn
            outs[1][...] = (xh * g).astype(BF16)
        else:
            err = xh * g - rows[2][...]
            dy = err * (1.0 / D)
            dxh = dy * g
            outs[0][...] = r * (dxh - xh * jnp.mean(dxh * xh, axis=-1, keepdims=True))
            outs[1][...] += jnp.sum(dy * xh, axis=0, keepdims=True)
            outs[2][...] += 0.5 * jnp.sum(jnp.mean(err * err, axis=-1, keepdims=True), axis=0, keepdims=True)

    if target is None:
        return _rows_call(tag + "_out", [act, h], [w_out, next_gain], [(D, F32), (D, BF16)], compute_out, tm)
    return _rows_call(tag + "_out", [act, h, target], [w_out, next_gain], [(D, F32)], compute_out, tm, sums=(D, 128))


class _Reduction:
    def __init__(self, tag, group, c_arr, k_arr):
        self.tag, self.group, self.c_arr, self.k_arr = tag, group, c_arr, k_arr

    def begin(self, grads):
        gp = _pack_grads(grads, self.group)
        self.sums, sums_bf16 = _chip_sum("grad_chip_sum_" + self.tag, gp, _swap_halves("grad_swap_" + self.tag, gp), self.c_arr)
        return _ScatterChipSums(sums_bf16)

    def end(self, got):
        mine = _shard_sum("grad_shard_sum_" + self.tag, self.sums, got, self.k_arr)
        return _unpack_shard(_join_halves("grad_join_" + self.tag, mine), self.group)


def _ffn_bwd(tag, h, gain, w_in, w_out, saved, dout, side, reduction):
    t = h.shape[0]
    tm = min(TM, t)
    n, gate, up, act = saved

    def compute(rows, weights, outs):
        d = rows[0][...].astype(BF16)
        for j in range(DFF // FFN_CHUNK):
            cols = slice(j * FFN_CHUNK, (j + 1) * FFN_CHUNK)
            da = 0.5 * _dot_nt(d, weights[0][cols, :])
            g, u = rows[1][:, cols].astype(F32), rows[2][:, cols].astype(F32)
            s = _sig(g)
            silu = g * s
            outs[0][:, cols] = (da * u * (s + silu * (1.0 - s))).astype(BF16)
            outs[1][:, cols] = (da * silu).astype(BF16)

    dgate, dup, *side_out = _rows_call(tag + "_dact", [dout, gate, up], [w_out], [(DFF, BF16)] * 2, compute,
                                       min(FFN_TM, t), side=side)
    dw_out = _mm_tn(tag + "_dw_out", act, dout, scale=0.5, tm=DFF // 2, tn=D)
    dw_g = _mm_tn(tag + "_dw_gate", n, dgate, tm=D, tn=DFF // 2)
    dw_u = _mm_tn(tag + "_dw_up", n, dup, tm=D, tn=DFF // 2)
    sending = reduction.begin({tag + "_w_in": jnp.concatenate([dw_g, dw_u], axis=1), tag + "_w_out": dw_out})

    def compute_dn(rows, weights, outs):
        w_ref = weights[0]
        dn = _dot_nt(rows[0][...], w_ref[:, 0:DFF]) + _dot_nt(rows[1][...], w_ref[:, DFF:2 * DFF])
        dx, dg = _rms_bwd_vals(rows[2][...], weights[1][...], dn)
        outs[0][...] = rows[3][...] + dx
        outs[1][...] += jnp.sum(dg, axis=0, keepdims=True)

    dh, dgain, got = _rows_call(tag + "_dn", [dgate, dup, h, dout], [w_in, gain], [(D, F32)], compute_dn,
                                min(FFN_TM, t), side=sending, sums=(D,), vmem_mb=58)
    return dh, dgain, side_out, got


def kernel(x, positions, ffn1_norm, ffn1_w_in, ffn1_w_out, mix_norm, w_in, hg_lb_table, hg_out_norm, w_hg_branch, mla_q_lora_norm, w_q_up, mla_kv_lora_norm, w_kv_up, q_head_norm, k_head_norm, w_mla_branch, w_merge, b_merge, w_out, ffn2_norm, ffn2_w_in, ffn2_w_out, final_norm, loss_target, m_ffn1_norm, m_ffn1_w_in, m_ffn1_w_out, m_mix_norm, m_w_in, m_hg_lb_table, m_hg_out_norm, m_w_hg_branch, m_mla_q_lora_norm, m_w_q_up, m_mla_kv_lora_norm, m_w_kv_up, m_q_head_norm, m_k_head_norm, m_w_mla_branch, m_w_merge, m_b_merge, m_w_out, m_ffn2_norm, m_ffn2_w_in, m_ffn2_w_out, m_final_norm, v_ffn1_norm, v_ffn1_w_in, v_ffn1_w_out, v_mix_norm, v_w_in, v_hg_lb_table, v_hg_out_norm, v_w_hg_branch, v_mla_q_lora_norm, v_w_q_up, v_mla_kv_lora_norm, v_w_kv_up, v_q_head_norm, v_k_head_norm, v_w_mla_branch, v_w_merge, v_b_merge, v_w_out, v_ffn2_norm, v_ffn2_w_in, v_ffn2_w_out, v_final_norm):
    a = dict(locals())
    w = {n: a[n] for n in WEIGHT_ORDER}
    mom = {n: a["m_" + n] for n in WEIGHT_ORDER}
    var = {n: a["v_" + n] for n in WEIGHT_ORDER}
    t = x.shape[1]
    tm = min(TM, t)
    xt = x.reshape(t, D)
    target = loss_target.reshape(t, D)
    pos = positions.reshape(t, 1)
    x_i, y_i, c_i = _place()
    k_idx = (2 * x_i + y_i).astype(jnp.int32)
    c_arr = c_i.astype(jnp.int32).reshape(1)
    k_arr = jnp.stack([k_idx, c_i.astype(jnp.int32)])

    group_first = _group(("ffn1_w_in", "ffn1_w_out"))
    group_mid = _group(("w_in", "w_hg_branch", "w_q_up", "w_kv_up", "w_mla_branch", "w_merge", "w_out"))
    group_last = _group(("ffn2_w_in", "ffn2_w_out"))
    use_first = _group(("ffn1_w_in",))
    use_early = _group(("ffn1_w_out", "w_in", "w_hg_branch", "w_q_up", "w_kv_up"))
    use_late = _group(("w_mla_branch", "w_merge", "w_out", "ffn2_w_in", "ffn2_w_out"))
    gather_first = _GatherWeights(_pack([w[e[0]][0] for e in use_first], BF16))
    gather_early = _GatherWeights(_pack([w[e[0]][0] for e in use_early], BF16))
    gather_late = _GatherWeights(_pack([w[e[0]][0] for e in use_late], BF16))
    (got,) = gather_first.alone("gather_first")
    full = _unpack_full(gather_first.gathered(got, k_idx), use_first)
    n1, gate1, up1, act1, got = _ffn_in("ffn1", xt, w["ffn1_norm"], full["ffn1_w_in"], gather_early)
    full.update(_unpack_full(gather_early.gathered(got, k_idx), use_early))
    h1, u = _ffn_out("ffn1", act1, xt, full["ffn1_w_out"], w["mix_norm"])
    ffn1_saved = (n1, gate1, up1, act1)
    w_in_full = full["w_in"]
    w_in_hg = w_in_full[:, :4 * D]
    w_in_mla = jnp.pad(w_in_full[:, 4 * D:], ((0, 0), (0, MLA_COLS - (4800 - 4 * D))))
    w_q_pad = jnp.pad(full["w_q_up"].reshape(Q_LORA, HEADS, QK), ((0, 0), (0, 0), (0, QKP - QK))).reshape(Q_LORA, HEADS * QKP)
    w_kv = full["w_kv_up"]
    gq = jnp.pad(w["q_head_norm"], ((0, 0), (0, QKP - QK)))
    gk = jnp.pad(w["k_head_norm"], ((0, 0), (0, QKP - QK)))

    ident = lambda accs, ex: (accs[0],)
    def in_hg(rows, weights, outs):
        a = rows[0][...]
        for j in range(4 * D // 512):
            cols = slice(j * 512, (j + 1) * 512)
            outs[0][:, cols] = _dot(a, weights[0][:, cols])

    (p_hg,) = _rows_call("in_hg", [u], [w_in_hg], [(4 * D, F32)], in_hg, min(FFN_TM, t))
    (p_mla,) = _mm("in_mla", [_a_spec(u, tm)], [_b_nn(w_in_mla, MLA_COLS)], [(0, 0)], ident, [], [F32], t, MLA_COLS, tm, MLA_COLS)
    o_raw, hg_o, states = _hgrn_fwd(p_hg, w["hg_lb_table"], w["hg_out_norm"])
    (y_hg,) = _mm("hg_branch", [_a_spec(hg_o, tm)], [_b_nn(full["w_hg_branch"], 512)], [(0, 0)], ident, [], [BF16], t, D, tm, 512)
    cqn, ckvn = _lora_norm_fwd(p_mla, w["mla_q_lora_norm"], w["mla_kv_lora_norm"])
    (qf,) = _mm("q_up", [_a_spec(cqn, tm)], [_b_nn(w_q_pad, 512)], [(0, 0)], ident, [], [F32], t, HEADS * QKP, tm, 512)
    (kvf,) = _mm("kv_up", [_a_spec(ckvn, tm)], [_b_nn(w_kv, 512)], [(0, 0)], ident, [], [F32], t, HEADS * QKP, tm, 512)
    cos, sin = _rope_tables(pos)
    qh, kh, vh = _mla_prep_fwd(qf, kvf, p_mla, cos, sin, gq, gk)
    o_mla, lse, got = _flash_fwd(qh, kh, vh, side=gather_late)
    full.update(_unpack_full(gather_late.gathered(got, k_idx), use_late))
    (y_mla,) = _mm("mla_branch", [_a_spec(o_mla, tm)], [_b_nn(full["w_mla_branch"], 512)], [(0, 0)], ident, [], [BF16], t, D, tm, 512)

    def merge_epi(accs, ex):
        g_hg = _sig(accs[0] + ex[2])
        g_mla = _sig(accs[1] + ex[3])
        return g_hg * ex[0].astype(F32) + g_mla * ex[1].astype(F32), g_hg, g_mla

    w_merge_f = full["w_merge"]
    mix, g_hg, g_mla = _mm(
        "merge", [_a_spec(u, tm)], [_b_nn(w_merge_f, 512), _b_nn(w_merge_f, 512, D // 512)], [(0, 0), (0, 1)], merge_epi,
        [_e_tile(y_hg, tm, 512), _e_tile(y_mla, tm, 512), _e_row(w["b_merge"], 512), _e_row(w["b_merge"], 512, D // 512)],
        [BF16, BF16, BF16], t, D, tm, 512)
    (h2,) = _mm("out_proj", [_a_spec(mix, tm)], [_b_nn(full["w_out"], 512)], [(0, 0)],
                lambda accs, ex: (ex[0] + accs[0],), [_e_tile(h1, tm, 512)], [F32], t, D, tm, 512)
    ffn2_saved = _ffn_in("ffn2", h2, w["ffn2_norm"], full["ffn2_w_in"])
    dh3, d_final_norm, loss_part = _ffn_out("ffn2", ffn2_saved[3], h2, full["ffn2_w_out"], w["final_norm"], target=target)

    grads, small = {}, {}
    small["final_norm"] = d_final_norm
    reduce_last = _Reduction("last", group_last, c_arr, k_arr)
    reduce_mid = _Reduction("mid", group_mid, c_arr, k_arr)
    reduce_first = _Reduction("first", group_first, c_arr, k_arr)
    dh2, small["ffn2_norm"], _, got_last = _ffn_bwd(
        "ffn2", h2, w["ffn2_norm"], full["ffn2_w_in"], full["ffn2_w_out"], ffn2_saved, dh3, None, reduce_last)

    def dmix_epi(accs, ex):
        dm = accs[0]
        ghg, gml, yhg, yml = [e.astype(F32) for e in ex]
        return dm * ghg, dm * gml, dm * yhg * ghg * (1.0 - ghg), dm * yml * gml * (1.0 - gml)

    dy_hg, dy_mla, dpre_hg, dpre_mla = _mm(
        "d_mix", [_a_spec(dh2, tm)], [_b_nt(full["w_out"], 512)], [(0, 0)], dmix_epi,
        [_e_tile(g_hg, tm, 512), _e_tile(g_mla, tm, 512), _e_tile(y_hg, tm, 512), _e_tile(y_mla, tm, 512)],
        [BF16, BF16, BF16, BF16], t, D, tm, 512, trans_b=True)
    grads["w_out"] = _mm_tn("dw_out", mix, dh2)
    small["b_merge"] = jnp.concatenate([_colsum("db_hg", dpre_hg), _colsum("db_mla", dpre_mla)], axis=1)
    grads["w_merge"] = jnp.concatenate([_mm_tn("dw_merge_hg", u, dpre_hg), _mm_tn("dw_merge_mla", u, dpre_mla)], axis=1)
    grads["w_hg_branch"] = _mm_tn("dw_hg_branch", hg_o, dy_hg)
    grads["w_mla_branch"] = _mm_tn("dw_mla_branch", o_mla, dy_mla)
    (dho,) = _mm("d_hg_o", [_a_spec(dy_hg, tm)], [_b_nt(full["w_hg_branch"], 512)], [(0, 0)], ident, [], [BF16], t, D, tm, 512, trans_b=True)
    (do_mla,) = _mm("d_o_mla", [_a_spec(dy_mla, tm)], [_b_nt(full["w_mla_branch"], 512)], [(0, 0)], ident, [], [BF16], t, D, tm, 512, trans_b=True)

    dq_raw, df_raw, di_raw, dg_raw, small["hg_lb_table"], small["hg_out_norm"] = _hgrn_bwd(
        p_hg, w["hg_lb_table"], w["hg_out_norm"], o_raw, states, dho)
    dp_hg = [dq_raw, df_raw, di_raw, dg_raw]

    dqh, dkh, dvh = _flash_bwd(qh, kh, vh, lse, _attn_do(do_mla, o_mla))
    dqf, dkvf, dkpe, dgq, dgk = _mla_prep_bwd(qf, kvf, p_mla, cos, sin, gq, gk, dqh, dkh, dvh)
    small["q_head_norm"] = dgq[:, :QK]
    small["k_head_norm"] = dgk[:, :QK]
    dwq_pad = _mm_tn("dw_q_up", cqn, dqf, tm=Q_LORA, tn=1024)
    grads["w_q_up"] = dwq_pad.reshape(Q_LORA, HEADS, QKP)[:, :, :QK].reshape(Q_LORA, HEADS * QK)
    grads["w_kv_up"] = _mm_tn("dw_kv_up", ckvn, dkvf, tm=KV_LORA, tn=1024)
    (dcqn,) = _mm("d_cq", [_a_spec(dqf, tm)], [_b_nt(w_q_pad, Q_LORA)], [(0, 0)], ident, [], [F32], t, Q_LORA, tm, Q_LORA, trans_b=True)
    (dckvn,) = _mm("d_ckv", [_a_spec(dkvf, tm)], [_b_nt(w_kv, KV_LORA)], [(0, 0)], ident, [], [F32], t, KV_LORA, tm, KV_LORA, trans_b=True)
    dp_mla, small["mla_q_lora_norm"], small["mla_kv_lora_norm"] = _lora_norm_bwd(
        p_mla, w["mla_q_lora_norm"], w["mla_kv_lora_norm"], dcqn, dckvn, dkpe)

    dw_in_hg = [_mm_tn("dw_in_hg%d" % k, u, dp_hg[k]) for k in range(4)]
    dw_in_mla = _mm_tn("dw_in_mla", u, dp_mla, tn=MLA_COLS)
    grads["w_in"] = jnp.concatenate(dw_in_hg + [dw_in_mla[:, :4800 - 4 * D]], axis=1)
    tm_du = min(TM // 2, t)
    du, got_mid = _mm(
        "d_u",
        [_a_spec(dpre_hg, tm_du), _a_spec(dpre_mla, tm_du)] + [_a_spec(d, tm_du) for d in dp_hg] + [_a_spec(dp_mla, tm_du)],
        [_b_nt(w_merge_f, 512, D, 0), _b_nt(w_merge_f, 512, D, 1)]
        + [_b_nt(w_in_hg, 512, D, k) for k in range(4)] + [_b_nt(w_in_mla, 512)],
        [(k, k) for k in range(7)],
        lambda accs, ex: (functools.reduce(lambda p, q: p + q, accs),), [], [F32], t, D, tm_du, 512, trans_b=True,
        side=reduce_mid.begin(grads))
    dh1, small["mix_norm"] = _rms_bwd("mix_dnorm", h1, w["mix_norm"], du, dh2)
    dx, small["ffn1_norm"], _, got_first = _ffn_bwd(
        "ffn1", xt, w["ffn1_norm"], full["ffn1_w_in"], full["ffn1_w_out"], ffn1_saved, dh1, None, reduce_first)

    g_shard = {**reduce_last.end(got_last), **reduce_mid.end(got_mid), **reduce_first.end(got_first)}
    small_sum = _all_reduce_small(_pack_small([small[n] for n, _ in SMALL] + [loss_part])).reshape(-1)
    g_small, at = {}, 0
    for n, shape in SMALL:
        size = shape[0] * shape[1]
        g_small[n] = small_sum[at:at + size].reshape(shape)
        at += size
    loss = small_sum[at]

    g_out, d_out, m_out, v_out = {}, {}, {}, {}
    for n in WEIGHT_ORDER:
        shape = w[n].shape
        g = g_shard[n] if n in g_shard else g_small[n]
        two = g.shape
        d_, m_, v_ = _adamw("adamw_" + n, w[n].reshape(two), g, mom[n].reshape(two), var[n].reshape(two))
        g_out[n], d_out[n], m_out[n], v_out[n] = g.reshape(shape), d_.reshape(shape), m_.reshape(shape), v_.reshape(shape)

    return (loss, dx.reshape(x.shape), *[g_out[n] for n in WEIGHT_ORDER], *[d_out[n] for n in WEIGHT_ORDER],
            *[m_out[n] for n in WEIGHT_ORDER], *[v_out[n] for n in WEIGHT_ORDER])
```

```python
import functools

import numpy as np
import jax
import jax.numpy as jnp
from jax import lax
from jax.experimental import pallas as pl
from jax.experimental.pallas import tpu as pltpu

F32 = jnp.float32
BF16 = jnp.bfloat16
MESH = pl.DeviceIdType.MESH

D = 1024
DFF = 2816
HEADS = 8
HK = 128
CHUNK = 64
ROPE = 64
QK = 192
QKP = 256
Q_LORA = 384
KV_LORA = 256
MLA_COLS = 768
EPS = 1e-6
ROPE_THETA = 10000.0
SCALE = QK ** -0.5
LOG2E = 1.4426950408889634
LN2 = 0.6931471805599453
NEG = -1e30
EXP_CLAMP = 80.0

ADAM_LR = 0.001
ADAM_B1 = 0.9
ADAM_B2 = 0.999
ADAM_EPS = 1e-08
ADAM_WD = 0.01
ADAM_STEP = 10

PACK_W = 1024
ADD_ROWS = 352
PACK_ALIGN = 2 * ADD_ROWS

TM = 1024
FFN_TM = 512
FFN_CHUNK = 256
TQ = 1024
SUBQ = 256
HG_BT = 512
HG_HPB = 8
TT = 1024
ROW_TM = 256

VMEM_MB = 48

BIG = (
    ("ffn1_w_in", D, 2 * DFF, 1),
    ("ffn1_w_out", DFF, D, 0),
    ("w_in", D, 4800, 1),
    ("w_hg_branch", D, D, 0),
    ("w_q_up", Q_LORA, HEADS * QK, 1),
    ("w_kv_up", KV_LORA, HEADS * 2 * HK, 1),
    ("w_mla_branch", D, D, 0),
    ("w_merge", D, 2 * D, 1),
    ("w_out", D, D, 0),
    ("ffn2_w_in", D, 2 * DFF, 1),
    ("ffn2_w_out", DFF, D, 0),
)
SMALL = (
    ("ffn1_norm", (1, D)),
    ("mix_norm", (1, D)),
    ("hg_lb_table", (2, D)),
    ("hg_out_norm", (1, HK)),
    ("mla_q_lora_norm", (1, Q_LORA)),
    ("mla_kv_lora_norm", (1, KV_LORA)),
    ("q_head_norm", (1, QK)),
    ("k_head_norm", (1, QK)),
    ("b_merge", (1, 2 * D)),
    ("ffn2_norm", (1, D)),
    ("final_norm", (1, D)),
)
WEIGHT_ORDER = ("ffn1_norm", "ffn1_w_in", "ffn1_w_out", "mix_norm", "w_in", "hg_lb_table", "hg_out_norm",
                "w_hg_branch", "mla_q_lora_norm", "w_q_up", "mla_kv_lora_norm", "w_kv_up", "q_head_norm",
                "k_head_norm", "w_mla_branch", "w_merge", "b_merge", "w_out", "ffn2_norm", "ffn2_w_in",
                "ffn2_w_out", "final_norm")


def _call(body, **kw):
    return pl.pallas_call(body, **kw)


def _cp(vmem_mb=VMEM_MB):
    return pltpu.CompilerParams(vmem_limit_bytes=vmem_mb << 20)


def _dot(a, b):
    return lax.dot_general(a, b, (((1,), (0,)), ((), ())), preferred_element_type=F32)


def _dot_nt(a, b):
    return lax.dot_general(a, b, (((1,), (1,)), ((), ())), preferred_element_type=F32)


def _dot_tn(a, b):
    return lax.dot_general(a, b, (((0,), (0,)), ((), ())), preferred_element_type=F32)


def _sig(x):
    return jax.nn.sigmoid(x)


def _silu(x):
    return x * _sig(x)


def _dsilu(x):
    s = _sig(x)
    return s * (1.0 + x * (1.0 - s))


def _a_spec(arr, tm, kblk=None, kidx=0):
    kb = arr.shape[1] if kblk is None else kblk
    return arr, pl.BlockSpec((tm, kb), lambda i, j, kidx=kidx: (i, kidx)), slice(kidx * kb, (kidx + 1) * kb)


def _b_nn(arr, tn, off=0):
    return arr, pl.BlockSpec((arr.shape[0], tn), lambda i, j, off=off: (0, j + off)), ("cols", off)


def _b_nt(arr, tn, kblk=None, kidx=0):
    kb = arr.shape[1] if kblk is None else kblk
    return arr, pl.BlockSpec((tn, kb), lambda i, j, kidx=kidx: (j, kidx)), ("rows", slice(kidx * kb, (kidx + 1) * kb))


def _e_tile(arr, tm, tn, off=0):
    return arr, pl.BlockSpec((tm, tn), lambda i, j, off=off: (i, j + off)), ("tile", off)


def _e_row(arr, tn, off=0):
    return arr, pl.BlockSpec((1, tn), lambda i, j, off=off: (0, j + off)), ("row", off)


def _mm_resident(name, As, Bs, dots, epi, extras, out_dtypes, m, n, tn):
    def unique(arrays):
        seen = []
        for a in arrays:
            if not any(a is s for s in seen):
                seen.append(a)
        return seen

    rows = unique([a for a, _, _ in As] + [e for e, _, where in extras if where[0] == "tile"])
    weights = unique([b for b, _, _ in Bs] + [e for e, _, where in extras if where[0] == "row"])

    def ref_of(arr, row_refs, weight_refs):
        for r, ref in zip(rows, row_refs):
            if r is arr:
                return ref
        for wt, ref in zip(weights, weight_refs):
            if wt is arr:
                return ref

    def compute(row_refs, weight_refs, out_refs):
        a_vals = [ref_of(a, row_refs, weight_refs)[:, ks].astype(BF16) for a, _, ks in As]
        for j in range(n // tn):
            accs = []
            for ai, bi in dots:
                b, _, where = Bs[bi]
                b_ref = ref_of(b, row_refs, weight_refs)
                if where[0] == "cols":
                    accs.append(_dot(a_vals[ai], b_ref[:, (j + where[1]) * tn:(j + where[1] + 1) * tn]))
                else:
                    accs.append(_dot_nt(a_vals[ai], b_ref[j * tn:(j + 1) * tn, where[1]]))
            ex = [ref_of(e, row_refs, weight_refs)[:, (j + where[1]) * tn:(j + where[1] + 1) * tn]
                  for e, _, where in extras]
            for o_ref, o in zip(out_refs, epi(accs, ex)):
                o_ref[:, j * tn:(j + 1) * tn] = o.astype(o_ref.dtype)

    return _rows_call(name, rows, weights, [(n, dt) for dt in out_dtypes], compute, min(FFN_TM, m))


def _mm(name, As, Bs, dots, epi, extras, out_dtypes, m, n, tm, tn, trans_b=False, side=None):
    if side is None:
        return _mm_resident(name, As, Bs, dots, epi, extras, out_dtypes, m, n, tn)
    na, nb, ne, no = len(As), len(Bs), len(extras), len(out_dtypes)
    ni, nj = m // tm, n // tn
    s_in = len(side.inputs) if side else 0
    s_out = len(side.out_shapes) if side else 0

    def body(*refs):
        a_refs = refs[:na]
        b_refs = refs[na:na + nb]
        e_refs = refs[na + nb:na + nb + ne]
        at = na + nb + ne
        side_refs = refs[at:at + s_in]
        o_refs = refs[at + s_in:at + s_in + no]
        side_refs = list(side_refs) + list(refs[at + s_in + no:])
        if side:
            i, j = pl.program_id(0), pl.program_id(1)

            @pl.when(jnp.logical_and(i == 0, j == 0))
            def _():
                side.start(*side_refs)

        a_vals = [r[...].astype(BF16) for r in a_refs]
        accs = []
        for ai, bi in dots:
            b = b_refs[bi][...]
            accs.append(_dot_nt(a_vals[ai], b) if trans_b else _dot(a_vals[ai], b))
        outs = epi(accs, [r[...] for r in e_refs])
        for o_ref, o in zip(o_refs, outs):
            o_ref[...] = o.astype(o_ref.dtype)
        if side:
            @pl.when(jnp.logical_and(i == ni - 1, j == nj - 1))
            def _():
                side.finish(*side_refs)

    ops = list(As) + list(Bs) + list(extras)
    anywhere = pl.BlockSpec(memory_space=pl.ANY)
    res = _call(
        body, name=name,
        grid=(ni, nj),
        in_specs=[op[1] for op in ops] + [anywhere] * s_in,
        out_specs=[pl.BlockSpec((tm, tn), lambda i, j: (i, j)) for _ in out_dtypes] + [anywhere] * s_out,
        out_shape=[jax.ShapeDtypeStruct((m, n), dt) for dt in out_dtypes] + (list(side.out_shapes) if side else []),
        scratch_shapes=list(side.scratch) if side else [],
        compiler_params=_cp(),
    )(*[op[0] for op in ops], *(side.inputs if side else []))
    return res


def _rows_call(name, rows, weights, outs, compute, tm, side=None, sums=(), vmem_mb=VMEM_MB):
    t = rows[0].shape[0]
    nr, nw, no = len(rows), len(weights), len(outs) + len(sums)
    ni = t // tm
    s_in = len(side.inputs) if side else 0
    s_out = len(side.out_shapes) if side else 0

    def body(*refs):
        at = nr + nw
        side_refs = list(refs[at:at + s_in]) + list(refs[at + s_in + no:])
        if side:
            @pl.when(pl.program_id(0) == 0)
            def _():
                side.start(*side_refs)

        out_refs = refs[at + s_in:at + s_in + no]
        if sums:
            @pl.when(pl.program_id(0) == 0)
            def _():
                for r in out_refs[len(outs):]:
                    r[...] = jnp.zeros_like(r)

        compute(refs[:nr], refs[nr:at], out_refs)
        if side:
            @pl.when(pl.program_id(0) == ni - 1)
            def _():
                side.finish(*side_refs)

    anywhere = pl.BlockSpec(memory_space=pl.ANY)
    return _call(
        body, name=name, grid=(ni,),
        in_specs=[pl.BlockSpec((tm, r.shape[1]), lambda i: (i, 0)) for r in rows]
        + [pl.BlockSpec(wt.shape, lambda i, nd=wt.ndim: (0,) * nd) for wt in weights] + [anywhere] * s_in,
        out_specs=[pl.BlockSpec((tm, width), lambda i: (i, 0)) for width, _ in outs]
        + [pl.BlockSpec((1, width), lambda i: (0, 0)) for width in sums] + [anywhere] * s_out,
        out_shape=[jax.ShapeDtypeStruct((t, width), dt) for width, dt in outs]
        + [jax.ShapeDtypeStruct((1, width), F32) for width in sums] + (list(side.out_shapes) if side else []),
        scratch_shapes=list(side.scratch) if side else [],
        compiler_params=_cp(vmem_mb),
    )(*rows, *weights, *(side.inputs if side else []))


def _mm_tn(name, a, b, scale=1.0, tm=1024, tn=1024, column_shards=False):
    t, m = a.shape
    n = b.shape[1]
    tm, tn, tt = min(tm, m), min(tn, n), min(TT, t)
    nk = t // tt

    def body(a_ref, b_ref, o_ref):
        k = pl.program_id(2)

        @pl.when(k == 0)
        def _():
            o_ref[...] = jnp.zeros_like(o_ref)

        o_ref[...] += _dot_tn(a_ref[...].astype(BF16), b_ref[...].astype(BF16))
        if scale != 1.0:
            @pl.when(k == nk - 1)
            def _():
                o_ref[...] = o_ref[...] * scale

    return _call(
        body, name=name,
        grid=(m // tm, n // tn, nk),
        in_specs=[pl.BlockSpec((tt, tm), lambda i, j, k: (k, i)), pl.BlockSpec((tt, tn), lambda i, j, k: (k, j))],
        out_specs=(pl.BlockSpec((None, tm, tn), lambda i, j, k: (j, i, 0)) if column_shards
                   else pl.BlockSpec((tm, tn), lambda i, j, k: (i, j))),
        out_shape=jax.ShapeDtypeStruct((n // tn, m, tn) if column_shards else (m, n), F32),
        compiler_params=_cp(),
    )(a, b)


def _rms_bwd_vals(xv, g, dn):
    r = lax.rsqrt(jnp.mean(xv * xv, axis=-1, keepdims=True) + EPS)
    xh = xv * r
    dxh = dn * g
    c = jnp.mean(dxh * xh, axis=-1, keepdims=True)
    return r * (dxh - xh * c), dn * xh


def _rms_bwd(name, x, gain, dn, dres):
    t, d = x.shape
    tm = min(ROW_TM, t)

    def body(x_ref, g_ref, dn_ref, dr_ref, dx_ref, dg_ref):
        @pl.when(pl.program_id(0) == 0)
        def _():
            dg_ref[...] = jnp.zeros_like(dg_ref)

        dx, dg = _rms_bwd_vals(x_ref[...], g_ref[...], dn_ref[...].astype(F32))
        dx_ref[...] = dr_ref[...] + dx
        dg_ref[...] += jnp.sum(dg, axis=0, keepdims=True)

    row = pl.BlockSpec((tm, d), lambda i: (i, 0))
    one = pl.BlockSpec((1, d), lambda i: (0, 0))
    return _call(
        body, name=name, grid=(t // tm,),
        in_specs=[row, one, row, row],
        out_specs=[row, one],
        out_shape=[jax.ShapeDtypeStruct((t, d), F32), jax.ShapeDtypeStruct((1, d), F32)],
        compiler_params=_cp(),
    )(x, gain, dn, dres)


def _colsum(name, x):
    t, n = x.shape
    tm = min(TM, t)

    def body(x_ref, o_ref):
        @pl.when(pl.program_id(0) == 0)
        def _():
            o_ref[...] = jnp.zeros_like(o_ref)

        o_ref[...] += jnp.sum(x_ref[...].astype(F32), axis=0, keepdims=True)

    return _call(
        body, name=name, grid=(t // tm,),
        in_specs=[pl.BlockSpec((tm, n), lambda i: (i, 0))],
        out_specs=pl.BlockSpec((1, n), lambda i: (0, 0)),
        out_shape=jax.ShapeDtypeStruct((1, n), F32),
        compiler_params=_cp(),
    )(x)


def _lora_norm_fwd(p_mla, gq, gkv):
    t = p_mla.shape[0]
    tm = min(ROW_TM, t)

    def body(p_ref, gq_ref, gkv_ref, q_ref, kv_ref):
        cq = p_ref[:, 0:Q_LORA]
        ckv = p_ref[:, Q_LORA:Q_LORA + KV_LORA]
        rq = lax.rsqrt(jnp.mean(cq * cq, axis=-1, keepdims=True) + EPS)
        rkv = lax.rsqrt(jnp.mean(ckv * ckv, axis=-1, keepdims=True) + EPS)
        q_ref[...] = (cq * rq * gq_ref[...]).astype(BF16)
        kv_ref[...] = (ckv * rkv * gkv_ref[...]).astype(BF16)

    return _call(
        body, name="lora_norm_fwd", grid=(t // tm,),
        in_specs=[pl.BlockSpec((tm, MLA_COLS), lambda i: (i, 0)),
                  pl.BlockSpec((1, Q_LORA), lambda i: (0, 0)), pl.BlockSpec((1, KV_LORA), lambda i: (0, 0))],
        out_specs=[pl.BlockSpec((tm, Q_LORA), lambda i: (i, 0)), pl.BlockSpec((tm, KV_LORA), lambda i: (i, 0))],
        out_shape=[jax.ShapeDtypeStruct((t, Q_LORA), BF16), jax.ShapeDtypeStruct((t, KV_LORA), BF16)],
        compiler_params=_cp(),
    )(p_mla, gq, gkv)


def _lora_norm_bwd(p_mla, gq, gkv, dcqn, dckvn, dkpe):
    t = p_mla.shape[0]
    tm = min(ROW_TM, t)

    def body(p_ref, gq_ref, gkv_ref, dq_ref, dkv_ref, dkpe_ref, dp_ref, dgq_ref, dgkv_ref):
        @pl.when(pl.program_id(0) == 0)
        def _():
            dgq_ref[...] = jnp.zeros_like(dgq_ref)
            dgkv_ref[...] = jnp.zeros_like(dgkv_ref)

        dcq, dgq = _rms_bwd_vals(p_ref[:, 0:Q_LORA], gq_ref[...], dq_ref[...])
        dckv, dgkv = _rms_bwd_vals(p_ref[:, Q_LORA:Q_LORA + KV_LORA], gkv_ref[...], dkv_ref[...])
        dp_ref[:, 0:Q_LORA] = dcq.astype(BF16)
        dp_ref[:, Q_LORA:Q_LORA + KV_LORA] = dckv.astype(BF16)
        dp_ref[:, Q_LORA + KV_LORA:MLA_COLS] = dkpe_ref[...].astype(BF16)
        dgq_ref[...] += jnp.sum(dgq, axis=0, keepdims=True)
        dgkv_ref[...] += jnp.sum(dgkv, axis=0, keepdims=True)

    return _call(
        body, name="lora_norm_bwd", grid=(t // tm,),
        in_specs=[pl.BlockSpec((tm, MLA_COLS), lambda i: (i, 0)),
                  pl.BlockSpec((1, Q_LORA), lambda i: (0, 0)), pl.BlockSpec((1, KV_LORA), lambda i: (0, 0)),
                  pl.BlockSpec((tm, Q_LORA), lambda i: (i, 0)), pl.BlockSpec((tm, KV_LORA), lambda i: (i, 0)),
                  pl.BlockSpec((tm, HK), lambda i: (i, 0))],
        out_specs=[pl.BlockSpec((tm, MLA_COLS), lambda i: (i, 0)),
                   pl.BlockSpec((1, Q_LORA), lambda i: (0, 0)), pl.BlockSpec((1, KV_LORA), lambda i: (0, 0))],
        out_shape=[jax.ShapeDtypeStruct((t, MLA_COLS), BF16), jax.ShapeDtypeStruct((1, Q_LORA), F32),
                   jax.ShapeDtypeStruct((1, KV_LORA), F32)],
        compiler_params=_cp(),
    )(p_mla, gq, gkv, dcqn, dckvn, dkpe)


def _cumsum_rows(x, row):
    for s in (1, 2, 4, 8, 16, 32):
        x = x + jnp.where(row >= s, pltpu.roll(x, s, 0), 0.0)
    return x


def _rcumsum_rows(x, row):
    for s in (1, 2, 4, 8, 16, 32):
        x = x + jnp.where(row < CHUNK - s, pltpu.roll(x, CHUNK - s, 0), 0.0)
    return x


def _hg_gates(qr, z, lb, row):
    q = _silu(qr)
    sg = _sig(z)
    f = lb + (1.0 - lb) * sg
    lf = jnp.log(f)
    k = (1.0 - lb) * (1.0 - sg)
    cum = _cumsum_rows(lf, row)
    mid = jnp.sum(jnp.where(row < CHUNK // 2, lf, 0.0), axis=0, keepdims=True)
    last = jnp.sum(lf, axis=0, keepdims=True)
    e_q = jnp.exp(jnp.minimum(cum - mid, EXP_CLAMP))
    e_k = jnp.exp(jnp.minimum(mid - cum, EXP_CLAMP))
    e_a = jnp.exp(cum)
    e_l = jnp.exp(last - cum)
    return q, sg, f, k, last, e_q, e_k, e_a, e_l


def _hgrn_fwd(p_hg, tab, gain):
    t = p_hg.shape[0]
    bt = min(HG_BT, t)
    nb, nc = t // bt, bt // CHUNK

    hpb = HG_HPB
    wide = hpb * HK

    def body(q_ref, f_ref, i_ref, g_ref, tab_ref, gain_ref, o_ref, ho_ref, st_ref, state):
        @pl.when(pl.program_id(1) == 0)
        def _():
            state[...] = jnp.zeros_like(state)

        row = lax.broadcasted_iota(jnp.int32, (CHUNK, HK), 0)
        tril = lax.broadcasted_iota(jnp.int32, (CHUNK, CHUNK), 0) >= lax.broadcasted_iota(jnp.int32, (CHUNK, CHUNK), 1)
        gain_v = gain_ref[...]

        def chunk(c, carry):
            sl = pl.ds(pl.multiple_of(c * CHUNK, CHUNK), CHUNK)
            for hh in range(hpb):
                ln = slice(hh * HK, (hh + 1) * HK)
                lb = _sig(tab_ref[0:1, ln] - tab_ref[1:2, ln])
                v = i_ref[sl, ln].astype(BF16)
                q, _, _, k, last, e_q, e_k, e_a, e_l = _hg_gates(q_ref[sl, ln], f_ref[sl, ln], lb, row)
                st = state[hh]
                st_ref[hh, c] = st
                p = jnp.where(tril, _dot_nt((q * e_q).astype(BF16), (k * e_k).astype(BF16)), 0.0)
                o = _dot(p.astype(BF16), v) + _dot_nt((q * e_a).astype(BF16), st.astype(BF16))
                state[hh] = jnp.exp(last) * st + _dot_tn(v, (k * e_l).astype(BF16))
                o_ref[sl, ln] = o
                r = lax.rsqrt(jnp.mean(o * o, axis=-1, keepdims=True) + EPS)
                ho_ref[sl, ln] = (o * r * gain_v * _silu(g_ref[sl, ln])).astype(BF16)
            return carry

        lax.fori_loop(0, nc, chunk, 0)

    def col(k):
        return pl.BlockSpec((bt, wide), lambda h, j, k=k: (j, k * (HEADS // hpb) + h))

    return _call(
        body, name="hgrn_fwd", grid=(HEADS // hpb, nb),
        in_specs=[col(0), col(1), col(2), col(3),
                  pl.BlockSpec((2, wide), lambda h, j: (0, h)), pl.BlockSpec((1, HK), lambda h, j: (0, 0))],
        out_specs=[pl.BlockSpec((bt, wide), lambda h, j: (j, h)), pl.BlockSpec((bt, wide), lambda h, j: (j, h)),
                   pl.BlockSpec((hpb, nc, HK, HK), lambda h, j: (h, j, 0, 0))],
        out_shape=[jax.ShapeDtypeStruct((t, D), F32), jax.ShapeDtypeStruct((t, D), BF16),
                   jax.ShapeDtypeStruct((HEADS, t // CHUNK, HK, HK), F32)],
        scratch_shapes=[pltpu.VMEM((hpb, HK, HK), F32)],
        compiler_params=_cp(),
    )(p_hg, p_hg, p_hg, p_hg, tab, gain)


def _hgrn_bwd(p_hg, tab, gain, o_raw, states, dho):
    t = p_hg.shape[0]
    bt = min(HG_BT, t)
    nb, nc = t // bt, bt // CHUNK
    hpb = HG_HPB
    wide = hpb * HK

    def body(q_ref, f_ref, i_ref, g_ref, tab_ref, gain_ref, o_ref, st_ref, dho_ref,
             dq_ref, df_ref, di_ref, dg_ref, dtab_ref, dgain_ref, dstate, dlb):
        h, j = pl.program_id(0), pl.program_id(1)

        @pl.when(jnp.logical_and(h == 0, j == 0))
        def _():
            dgain_ref[...] = jnp.zeros_like(dgain_ref)

        @pl.when(j == 0)
        def _():
            dstate[...] = jnp.zeros_like(dstate)
            dlb[...] = jnp.zeros_like(dlb)

        row = lax.broadcasted_iota(jnp.int32, (CHUNK, HK), 0)
        tril = lax.broadcasted_iota(jnp.int32, (CHUNK, CHUNK), 0) >= lax.broadcasted_iota(jnp.int32, (CHUNK, CHUNK), 1)
        gain_v = gain_ref[...]

        def chunk(cc, carry):
            c = nc - 1 - cc
            sl = pl.ds(pl.multiple_of(c * CHUNK, CHUNK), CHUNK)
            dgain = jnp.zeros((1, HK), F32)
            for hh in range(hpb):
                ln = slice(hh * HK, (hh + 1) * HK)
                lb = _sig(tab_ref[0:1, ln] - tab_ref[1:2, ln])
                qr = q_ref[sl, ln]
                v = i_ref[sl, ln].astype(BF16)
                gr = g_ref[sl, ln]
                q, sg, f, k, last, e_q, e_k, e_a, e_l = _hg_gates(qr, f_ref[sl, ln], lb, row)
                o = o_ref[sl, ln]
                r = lax.rsqrt(jnp.mean(o * o, axis=-1, keepdims=True) + EPS)
                oh = o * r
                dh = dho_ref[sl, ln].astype(F32)
                dnorm = dh * _silu(gr)
                dg_ref[sl, ln] = (dh * oh * gain_v * _dsilu(gr)).astype(BF16)
                dgain = dgain + jnp.sum(dnorm * oh, axis=0, keepdims=True)
                dxh = dnorm * gain_v
                do = (r * (dxh - oh * jnp.mean(dxh * oh, axis=-1, keepdims=True))).astype(BF16)
                st0 = st_ref[hh, c]
                st0_b = st0.astype(BF16)
                ds1 = dstate[hh]
                ds1_b = ds1.astype(BF16)
                qt = (q * e_q).astype(BF16)
                kt = (k * e_k).astype(BF16)
                qd = (q * e_a).astype(BF16)
                kd = (k * e_l).astype(BF16)
                p = jnp.where(tril, _dot_nt(qt, kt), 0.0).astype(BF16)
                dp = jnp.where(tril, _dot_nt(do, v), 0.0).astype(BF16)
                dv = _dot_tn(p, do) + _dot_nt(kd, ds1_b)
                dqt = _dot(dp, kt)
                dkt = _dot_tn(dp, qt)
                dq_inter = _dot(do, st0_b) * e_a
                dk_inter = _dot(v, ds1_b) * e_l
                dq = dqt * e_q + dq_inter
                dk = dkt * e_k + dk_inter
                e_last = jnp.exp(last)
                dstate[hh] = _dot_tn(do, qd) + e_last * ds1
                dlast = (jnp.sum(k * dk_inter, axis=0, keepdims=True)
                         + e_last * jnp.sum(ds1 * st0, axis=0, keepdims=True))
                da = (qt.astype(F32) * dqt - kt.astype(F32) * dkt + q * dq_inter - k * dk_inter
                      + jnp.where(row == CHUNK - 1, dlast, 0.0))
                dlf = _rcumsum_rows(da, row)
                dfv = dlf / f - dk
                df_ref[sl, ln] = (dfv * (1.0 - lb) * sg * (1.0 - sg)).astype(BF16)
                dlb[:, ln] += jnp.sum(dfv * (1.0 - sg), axis=0, keepdims=True)
                dq_ref[sl, ln] = (dq * _dsilu(qr)).astype(BF16)
                di_ref[sl, ln] = dv.astype(BF16)
            dgain_ref[...] += dgain
            return carry

        lax.fori_loop(0, nc, chunk, 0)

        @pl.when(j == nb - 1)
        def _():
            lb = _sig(tab_ref[0:1, :] - tab_ref[1:2, :])
            d0 = dlb[...] * lb * (1.0 - lb)
            dtab_ref[0:1, :] = d0
            dtab_ref[1:2, :] = -d0

    def col(k):
        return pl.BlockSpec((bt, wide), lambda h, j, k=k: (nb - 1 - j, k * (HEADS // hpb) + h))

    tok = pl.BlockSpec((bt, wide), lambda h, j: (nb - 1 - j, h))
    return _call(
        body, name="hgrn_bwd", grid=(HEADS // hpb, nb),
        in_specs=[col(0), col(1), col(2), col(3),
                  pl.BlockSpec((2, wide), lambda h, j: (0, h)), pl.BlockSpec((1, HK), lambda h, j: (0, 0)),
                  tok, pl.BlockSpec((hpb, nc, HK, HK), lambda h, j: (h, nb - 1 - j, 0, 0)), tok],
        out_specs=[tok, tok, tok, tok,
                   pl.BlockSpec((2, wide), lambda h, j: (0, h)), pl.BlockSpec((1, HK), lambda h, j: (0, 0))],
        out_shape=[jax.ShapeDtypeStruct((t, D), BF16)] * 4
        + [jax.ShapeDtypeStruct((2, D), F32), jax.ShapeDtypeStruct((1, HK), F32)],
        scratch_shapes=[pltpu.VMEM((hpb, HK, HK), F32), pltpu.VMEM((1, wide), F32)],
        compiler_params=_cp(),
    )(p_hg, p_hg, p_hg, p_hg, tab, gain, o_raw, states, dho)


def _rope_tables(pos):
    t = pos.shape[0]
    tm = min(ROW_TM, t)
    inv = np.zeros((1, HK), np.float32)
    freq = (ROPE_THETA ** (-np.arange(0, ROPE, 2, dtype=np.float32) / ROPE)).astype(np.float32)
    inv[0, 0:ROPE // 2] = freq
    inv[0, ROPE // 2:ROPE] = freq
    sign = np.zeros((1, HK), np.float32)
    sign[0, 0:ROPE // 2] = -1.0
    sign[0, ROPE // 2:ROPE] = 1.0

    def body(pos_ref, inv_ref, sign_ref, cos_ref, sin_ref):
        ang = pos_ref[...].astype(F32) * inv_ref[...]
        cos_ref[...] = jnp.cos(ang)
        sin_ref[...] = jnp.sin(ang) * sign_ref[...]

    one = pl.BlockSpec((1, HK), lambda i: (0, 0))
    row = pl.BlockSpec((tm, HK), lambda i: (i, 0))
    return _call(
        body, name="rope_tables", grid=(t // tm,),
        in_specs=[pl.BlockSpec((tm, 1), lambda i: (i, 0)), one, one],
        out_specs=[row, row],
        out_shape=[jax.ShapeDtypeStruct((t, HK), F32)] * 2,
        compiler_params=_cp(),
    )(pos, jnp.asarray(inv), jnp.asarray(sign))


def _rope(x, cos, sin_signed):
    r = lax.broadcasted_iota(jnp.int32, (HK, HK), 0)
    c = lax.broadcasted_iota(jnp.int32, (HK, HK), 1)
    half = ROPE // 2
    swap = jnp.logical_or(jnp.logical_and(c < half, r == c + half),
                          jnp.logical_and(jnp.logical_and(c >= half, c < ROPE), r == c - half))
    return x * cos + _dot_split(x, swap.astype(BF16)) * sin_signed


def _dot_split(x, m):
    hi = x.astype(BF16)
    lo = (x - hi.astype(F32)).astype(BF16)
    return _dot(hi, m) + _dot(lo, m)


def _lane_sum(x):
    return _dot_split(x, jnp.ones((HK, HK), BF16))


def _head_norm(xn, xr):
    r = lax.rsqrt(_lane_sum(xn * xn + xr * xr) * (1.0 / QK) + EPS)
    return xn * r, xr * r, r


def _head_norm_bwd(xn, xr, g_n, g_r, dn, dr):
    hn, hr, r = _head_norm(xn, xr)
    dxn, dxr = dn * g_n, dr * g_r
    c = _lane_sum(dxn * hn + dxr * hr) * (1.0 / QK)
    return r * (dxn - hn * c), r * (dxr - hr * c), dn * hn, dr * hr


def _mla_prep_fwd(qf, kv, p_mla, cos, sin, gq, gk):
    t = qf.shape[0]
    tm = min(ROW_TM, t)

    def body(qf_ref, kv_ref, kpe_ref, cos_ref, sin_ref, gq_ref, gk_ref, q_ref, k_ref, v_ref):
        cos_v, sin_v = cos_ref[...], sin_ref[...]
        kpe = kpe_ref[...]
        for h in range(HEADS):
            lo, mid, hi = h * QKP, h * QKP + HK, (h + 1) * QKP
            qn, qr, _ = _head_norm(qf_ref[:, lo:mid], qf_ref[:, mid:hi])
            q_ref[h, :, 0:HK] = (qn * gq_ref[:, 0:HK] * (SCALE * LOG2E)).astype(BF16)
            q_ref[h, :, HK:QKP] = (_rope(qr * gq_ref[:, HK:QKP], cos_v, sin_v) * (SCALE * LOG2E)).astype(BF16)
            kn, kr, _ = _head_norm(kv_ref[:, lo:mid], kpe)
            k_ref[h, :, 0:HK] = (kn * gk_ref[:, 0:HK]).astype(BF16)
            k_ref[h, :, HK:QKP] = _rope(kr * gk_ref[:, HK:QKP], cos_v, sin_v).astype(BF16)
            v_ref[h, :, 0:HK] = kv_ref[:, mid:hi].astype(BF16)
            v_ref[h, :, HK:QKP] = jnp.full((tm, HK), -1.0, BF16)

    head = pl.BlockSpec((tm, HEADS * QKP), lambda i: (i, 0))
    tok = pl.BlockSpec((tm, HK), lambda i: (i, 0))
    gain = pl.BlockSpec((1, QKP), lambda i: (0, 0))
    return _call(
        body, name="mla_prep_fwd", grid=(t // tm,),
        in_specs=[head, head, pl.BlockSpec((tm, HK), lambda i: (i, MLA_COLS // HK - 1)), tok, tok, gain, gain],
        out_specs=[pl.BlockSpec((HEADS, tm, QKP), lambda i: (0, i, 0)),
                   pl.BlockSpec((HEADS, tm, QKP), lambda i: (0, i, 0)),
                   pl.BlockSpec((HEADS, tm, QKP), lambda i: (0, i, 0))],
        out_shape=[jax.ShapeDtypeStruct((HEADS, t, QKP), BF16), jax.ShapeDtypeStruct((HEADS, t, QKP), BF16),
                   jax.ShapeDtypeStruct((HEADS, t, QKP), BF16)],
        compiler_params=_cp(),
    )(qf, kv, p_mla, cos, sin, gq, gk)


def _mla_prep_bwd(qf, kv, p_mla, cos, sin, gq, gk, dq, dk, dv):
    t = qf.shape[0]
    tm = min(ROW_TM, t)

    def body(qf_ref, kv_ref, kpe_ref, cos_ref, sin_ref, gq_ref, gk_ref, dq_ref, dk_ref, dv_ref,
             dqf_ref, dkv_ref, dkpe_ref, dgq_ref, dgk_ref):
        @pl.when(pl.program_id(0) == 0)
        def _():
            dgq_ref[...] = jnp.zeros_like(dgq_ref)
            dgk_ref[...] = jnp.zeros_like(dgk_ref)

        cos_v, sin_v = cos_ref[...], -sin_ref[...]
        kpe = kpe_ref[...]
        gqn, gqr, gkn, gkr = gq_ref[:, 0:HK], gq_ref[:, HK:QKP], gk_ref[:, 0:HK], gk_ref[:, HK:QKP]
        dkpe = jnp.zeros((tm, HK), F32)
        dgq_n, dgq_r, dgk_n, dgk_r = [jnp.zeros((1, HK), F32) for _ in range(4)]
        for h in range(HEADS):
            lo, mid, hi = h * QKP, h * QKP + HK, (h + 1) * QKP
            dqn = dq_ref[h, :, 0:HK].astype(F32) * SCALE
            dqr = _rope(dq_ref[h, :, HK:QKP].astype(F32), cos_v, sin_v) * SCALE
            a, b, ga, gb = _head_norm_bwd(qf_ref[:, lo:mid], qf_ref[:, mid:hi], gqn, gqr, dqn, dqr)
            dqf_ref[:, lo:mid] = a.astype(BF16)
            dqf_ref[:, mid:hi] = b.astype(BF16)
            dgq_n = dgq_n + jnp.sum(ga, axis=0, keepdims=True)
            dgq_r = dgq_r + jnp.sum(gb, axis=0, keepdims=True)
            dkn = dk_ref[h, :, 0:HK].astype(F32) * LN2
            dkr = _rope(dk_ref[h, :, HK:QKP].astype(F32), cos_v, sin_v) * LN2
            a, b, ga, gb = _head_norm_bwd(kv_ref[:, lo:mid], kpe, gkn, gkr, dkn, dkr)
            dkv_ref[:, lo:mid] = a.astype(BF16)
            dkv_ref[:, mid:hi] = dv_ref[h].astype(BF16)
            dkpe = dkpe + b
            dgk_n = dgk_n + jnp.sum(ga, axis=0, keepdims=True)
            dgk_r = dgk_r + jnp.sum(gb, axis=0, keepdims=True)
        dkpe_ref[...] = dkpe
        dgq_ref[:, 0:HK] += dgq_n
        dgq_ref[:, HK:QKP] += dgq_r
        dgk_ref[:, 0:HK] += dgk_n
        dgk_ref[:, HK:QKP] += dgk_r

    head = pl.BlockSpec((tm, HEADS * QKP), lambda i: (i, 0))
    tok = pl.BlockSpec((tm, HK), lambda i: (i, 0))
    gain = pl.BlockSpec((1, QKP), lambda i: (0, 0))
    hq = pl.BlockSpec((HEADS, tm, QKP), lambda i: (0, i, 0))
    return _call(
        body, name="mla_prep_bwd", grid=(t // tm,),
        in_specs=[head, head, pl.BlockSpec((tm, HK), lambda i: (i, MLA_COLS // HK - 1)), tok, tok, gain, gain,
                  hq, hq, pl.BlockSpec((HEADS, tm, HK), lambda i: (0, i, 0))],
        out_specs=[head, head, tok, gain, gain],
        out_shape=[jax.ShapeDtypeStruct((t, HEADS * QKP), BF16), jax.ShapeDtypeStruct((t, HEADS * QKP), BF16),
                   jax.ShapeDtypeStruct((t, HK), F32), jax.ShapeDtypeStruct((1, QKP), F32),
                   jax.ShapeDtypeStruct((1, QKP), F32)],
        compiler_params=_cp(),
    )(qf, kv, p_mla, cos, sin, gq, gk, dq, dk, dv)


def _chunk_mask(row0, rows, cols):
    r = lax.broadcasted_iota(jnp.int32, (rows, cols), 0) + row0
    c = lax.broadcasted_iota(jnp.int32, (rows, cols), 1)
    return jnp.right_shift(r, 6) >= jnp.right_shift(c, 6)


def _flash_fwd(q, k, v, side=None):
    t = q.shape[1]
    tq = min(TQ, t)
    nq = t // tq
    sub = min(SUBQ, tq)
    pairs = [(i, j) for i in range(nq) for j in range(i + 1)]
    qi = jnp.asarray([p[0] for p in pairs], jnp.int32)
    kj = jnp.asarray([p[1] for p in pairs], jnp.int32)
    s_in = len(side.inputs) if side else 0
    s_out = len(side.out_shapes) if side else 0

    def body(qi_ref, kj_ref, q_ref, k_ref, v_ref, *rest):
        o_ref, lse_ref = rest[s_in:s_in + 2]
        m_s, acc_s = rest[s_in + 2 + s_out:s_in + 4 + s_out]
        side_refs = list(rest[:s_in]) + list(rest[s_in + 2:s_in + 2 + s_out]) + list(rest[s_in + 4 + s_out:])
        n = pl.program_id(1)
        i, j = qi_ref[n], kj_ref[n]
        if side:
            @pl.when(jnp.logical_and(pl.program_id(0) == 0, n == 0))
            def _():
                side.start(*side_refs)

        @pl.when(j == 0)
        def _():
            m_s[...] = jnp.full_like(m_s, NEG)
            acc_s[...] = jnp.zeros_like(acc_s)

        def step(diag):
            subs = range(tq // sub)
            width = [(r + 1) * sub if diag else tq for r in subs]
            logits = [_dot_nt(q_ref[r * sub:(r + 1) * sub, :], k_ref[0:width[r], :]) for r in subs]
            for r in subs:
                rows = slice(r * sub, (r + 1) * sub)
                cols = width[r]
                s = logits[r]
                if diag:
                    s = jnp.where(_chunk_mask(r * sub, sub, cols), s, NEG)
                m_old = m_s[rows, :]
                m_new = jnp.maximum(m_old, jnp.max(s, axis=-1, keepdims=True))
                alpha = jnp.exp2(m_old - m_new)
                p = jnp.exp2((s - jnp.tile(m_new, (1, cols // HK))).astype(BF16))
                acc_s[rows, :] = jnp.tile(alpha, (1, 2)) * acc_s[rows, :] + _dot(p, v_ref[0:cols, :])
                m_s[rows, :] = m_new

        @pl.when(j < i)
        def _():
            step(False)

        @pl.when(j == i)
        def _():
            step(True)
            l = -acc_s[:, HK:QKP]
            o_ref[...] = (acc_s[:, 0:HK] / l).astype(BF16)
            lse_ref[...] = m_s[...] + jnp.log(l) * LOG2E

        if side:
            @pl.when(jnp.logical_and(pl.program_id(0) == HEADS - 1, n == len(pairs) - 1))
            def _():
                side.finish(*side_refs)

    anywhere = pl.BlockSpec(memory_space=pl.ANY)
    grid_spec = pltpu.PrefetchScalarGridSpec(
        num_scalar_prefetch=2, grid=(HEADS, len(pairs)),
        in_specs=[pl.BlockSpec((None, tq, QKP), lambda h, n, qi, kj: (h, qi[n], 0)),
                  pl.BlockSpec((None, tq, QKP), lambda h, n, qi, kj: (h, kj[n], 0)),
                  pl.BlockSpec((None, tq, QKP), lambda h, n, qi, kj: (h, kj[n], 0))] + [anywhere] * s_in,
        out_specs=[pl.BlockSpec((tq, HK), lambda h, n, qi, kj: (qi[n], h)),
                   pl.BlockSpec((None, tq, HK), lambda h, n, qi, kj: (h, qi[n], 0))] + [anywhere] * s_out,
        scratch_shapes=[pltpu.VMEM((tq, HK), F32), pltpu.VMEM((tq, QKP), F32)] + (list(side.scratch) if side else []),
    )
    return _call(
        body, name="flash_fwd", grid_spec=grid_spec,
        out_shape=[jax.ShapeDtypeStruct((t, D), BF16), jax.ShapeDtypeStruct((HEADS, t, HK), F32)]
        + (list(side.out_shapes) if side else []),
        compiler_params=_cp(),
    )(qi, kj, q, k, v, *(side.inputs if side else []))


def _attn_do(do, o):
    t = do.shape[0]
    tm = min(TM, t)

    def body(do_ref, o_ref, d_ref):
        lane = lax.broadcasted_iota(jnp.int32, (tm, HK), 1)
        for h in range(HEADS):
            ln = slice(h * HK, (h + 1) * HK)
            dov = do_ref[:, ln]
            d = jnp.sum(dov.astype(F32) * o_ref[:, ln].astype(F32), axis=-1, keepdims=True)
            hi = d.astype(BF16).astype(F32)
            d_ref[h, :, 0:HK] = dov
            d_ref[h, :, HK:QKP] = jnp.where(lane == 0, hi, jnp.where(lane == 1, d - hi, 0.0)).astype(BF16)

    blk = pl.BlockSpec((tm, D), lambda i: (i, 0))
    return _call(
        body, name="attn_do", grid=(t // tm,),
        in_specs=[blk, blk],
        out_specs=pl.BlockSpec((HEADS, tm, QKP), lambda i: (0, i, 0)),
        out_shape=jax.ShapeDtypeStruct((HEADS, t, QKP), BF16),
        compiler_params=_cp(),
    )(do, o)


def _flash_bwd(q, k, v, lse, do):
    t = q.shape[1]
    tq = min(TQ, t)
    nq = t // tq
    sub = min(SUBQ, tq)
    pairs = [(i, j) for j in range(nq) for i in range(j, nq)]
    qi = jnp.asarray([p[0] for p in pairs], jnp.int32)
    kj = jnp.asarray([p[1] for p in pairs], jnp.int32)
    npairs = len(pairs)

    def body(qi_ref, kj_ref, q_ref, k_ref, v_ref, lse_ref, do_ref, dq_ref, dk_ref, dv_ref):
        n = pl.program_id(1)
        i, j = qi_ref[n], kj_ref[n]

        @pl.when(n == 0)
        def _():
            dq_ref[...] = jnp.zeros_like(dq_ref)

        @pl.when(i == j)
        def _():
            dk_ref[...] = jnp.zeros_like(dk_ref)
            dv_ref[...] = jnp.zeros_like(dv_ref)

        def step(diag):
            for r in range(tq // sub):
                rows = slice(r * sub, (r + 1) * sub)
                cols = (r + 1) * sub if diag else tq
                qv, kv_ = q_ref[rows, :], k_ref[0:cols, :]
                p = jnp.exp2(_dot_nt(qv, kv_) - jnp.tile(lse_ref[rows, :], (1, cols // HK)))
                if diag:
                    p = jnp.where(_chunk_mask(r * sub, sub, cols), p, 0.0)
                dp_less_delta = _dot_nt(do_ref[rows, :], v_ref[0:cols, :])
                ds = (p * dp_less_delta).astype(BF16)
                dv_ref[0:cols, :] += _dot_tn(p.astype(BF16), do_ref[rows, 0:HK])
                dk_ref[0:cols, :] += _dot_tn(ds, qv)
                dq_rows = pl.ds(pl.multiple_of(i * tq + r * sub, sub), sub)
                dq_ref[dq_rows, :] += _dot(ds, kv_)

        @pl.when(j < i)
        def _():
            step(False)

        @pl.when(j == i)
        def _():
            step(True)

    grid_spec = pltpu.PrefetchScalarGridSpec(
        num_scalar_prefetch=2, grid=(HEADS, npairs),
        in_specs=[pl.BlockSpec((None, tq, QKP), lambda h, n, qi, kj: (h, qi[n], 0)),
                  pl.BlockSpec((None, tq, QKP), lambda h, n, qi, kj: (h, kj[n], 0)),
                  pl.BlockSpec((None, tq, QKP), lambda h, n, qi, kj: (h, kj[n], 0)),
                  pl.BlockSpec((None, tq, HK), lambda h, n, qi, kj: (h, qi[n], 0)),
                  pl.BlockSpec((None, tq, QKP), lambda h, n, qi, kj: (h, qi[n], 0))],
        out_specs=[pl.BlockSpec((None, t, QKP), lambda h, n, qi, kj: (h, 0, 0)),
                   pl.BlockSpec((None, tq, QKP), lambda h, n, qi, kj: (h, kj[n], 0)),
                   pl.BlockSpec((None, tq, HK), lambda h, n, qi, kj: (h, kj[n], 0))],
    )
    return _call(
        body, name="flash_bwd", grid_spec=grid_spec,
        out_shape=[jax.ShapeDtypeStruct((HEADS, t, QKP), F32), jax.ShapeDtypeStruct((HEADS, t, QKP), F32),
                   jax.ShapeDtypeStruct((HEADS, t, HK), F32)],
        compiler_params=_cp(56),
    )(qi, kj, q, k, v, lse, do)


def _adamw(name, w, g, m, v):
    r, c = w.shape
    tr = r if r <= 256 else next(k for k in (256, 352, 384) if r % k == 0)

    def body(w_ref, g_ref, m_ref, v_ref, d_ref, nm_ref, nv_ref):
        gv = g_ref[...]
        nm = ADAM_B1 * m_ref[...] + (1.0 - ADAM_B1) * gv
        nv = ADAM_B2 * v_ref[...] + (1.0 - ADAM_B2) * (gv * gv)
        m_hat = nm / (1.0 - ADAM_B1 ** ADAM_STEP)
        v_hat = nv / (1.0 - ADAM_B2 ** ADAM_STEP)
        d_ref[...] = -ADAM_LR * (m_hat / (jnp.sqrt(v_hat) + ADAM_EPS) + ADAM_WD * w_ref[...])
        nm_ref[...] = nm
        nv_ref[...] = nv

    blk = pl.BlockSpec((tr, c), lambda i: (i, 0))
    return _call(
        body, name=name, grid=(r // tr,),
        in_specs=[blk] * 4, out_specs=[blk] * 3,
        out_shape=[jax.ShapeDtypeStruct((r, c), F32)] * 3,
        compiler_params=_cp(),
    )(w, g, m, v)


def _place():
    return lax.axis_index("x"), lax.axis_index("y"), lax.axis_index("c")


def _other_chips(x, y):
    return [(1 - x, y), (x, 1 - y), (1 - x, 1 - y)]


class _Exchange:
    inputs = ()
    out_shapes = ()
    scratch = ()

    def start(self, *refs):
        raise NotImplementedError

    def finish(self, *refs):
        raise NotImplementedError

    def alone(self, name):
        def body(*refs):
            self.start(*refs)
            self.finish(*refs)

        anywhere = pl.BlockSpec(memory_space=pl.ANY)
        return _call(
            body, name=name,
            in_specs=[anywhere] * len(self.inputs), out_specs=[anywhere] * len(self.out_shapes),
            out_shape=list(self.out_shapes), scratch_shapes=list(self.scratch),
        )(*self.inputs)


class _GatherWeights(_Exchange):
    def __init__(self, shards):
        self.inputs = tuple(shards)
        self.out_shapes = tuple(jax.ShapeDtypeStruct((4,) + s.shape, s.dtype) for s in shards)
        self.scratch = (pltpu.SemaphoreType.DMA((6 * len(shards),)), pltpu.SemaphoreType.DMA((6 * len(shards),)))

    def gathered(self, got, k):
        return [lax.dynamic_update_slice(g, s[None], (k, 0, 0)) for g, s in zip(got, self.inputs)]

    def _copies(self, *refs):
        nbuf = len(self.inputs)
        send_sems, recv_sems = refs[2 * nbuf:]
        x, y, c = _place()
        chips = _other_chips(x, y)
        first, passed, landed, relayed = [], [], [], []
        for b, (s_ref, g_ref) in enumerate(zip(refs[:nbuf], refs[nbuf:2 * nbuf])):
            half = self.inputs[b].shape[0] // 2

            def rows(px, py, pc, g_ref=g_ref, half=half):
                return g_ref.at[2 * px + py, pl.ds(pc * half, half), :]

            def copy(k, block, to, src=None, rows=rows, b=b):
                return pltpu.make_async_remote_copy(
                    src_ref=rows(*block) if src is None else src, dst_ref=rows(*block),
                    send_sem=send_sems.at[6 * b + k], recv_sem=recv_sems.at[6 * b + k], device_id=to, device_id_type=MESH)

            mine = s_ref.at[pl.ds(c * half, half), :]
            first += [copy(j, (x, y, c), (*chip, c), src=mine) for j, chip in enumerate(chips)]
            passed += [copy(3 + j, (*chip, c), (x, y, 1 - c)) for j, chip in enumerate(chips)]
            landed += [copy(j, (*chip, c), (x, y, c)) for j, chip in enumerate(chips)]
            relayed += [copy(3 + j, (*chip, 1 - c), (x, y, c)) for j, chip in enumerate(chips)]
        return first, passed, landed, relayed

    def start(self, *refs):
        for cp in self._copies(*refs)[0]:
            cp.start()

    def finish(self, *refs):
        first, passed, landed, relayed = self._copies(*refs)
        for arrived, onward in zip(landed, passed):
            arrived.wait_recv()
            onward.start()
        for cp in relayed:
            cp.wait_recv()
        for cp in first + passed:
            cp.wait_send()


def _swap_halves(name, bufs):
    nbuf = len(bufs)

    def body(*refs):
        send_sems, recv_sems = refs[2 * nbuf:]
        x, y, c = _place()
        cps = []
        for b, (g_ref, o_ref) in enumerate(zip(refs[:nbuf], refs[nbuf:2 * nbuf])):
            half = bufs[b].shape[1] // 2
            cps.append(pltpu.make_async_remote_copy(
                src_ref=g_ref.at[:, pl.ds((1 - c) * half, half), :], dst_ref=o_ref,
                send_sem=send_sems.at[b], recv_sem=recv_sems.at[b], device_id=(x, y, 1 - c), device_id_type=MESH))
        for cp in cps:
            cp.start()
        for cp in cps:
            cp.wait()

    anywhere = pl.BlockSpec(memory_space=pl.ANY)
    return _call(
        body, name=name,
        in_specs=[anywhere] * nbuf, out_specs=[anywhere] * nbuf,
        out_shape=[jax.ShapeDtypeStruct((4, g.shape[1] // 2, g.shape[2]), g.dtype) for g in bufs],
        scratch_shapes=[pltpu.SemaphoreType.DMA((nbuf,)), pltpu.SemaphoreType.DMA((nbuf,))],
    )(*bufs)


def _add_rows(half):
    return next(tr for tr in range(512, 15, -16) if half % tr == 0)


def _chip_sum(name, gp, got, c_arr):
    half, width = got.shape[1], got.shape[2]
    tr = _add_rows(half)
    nb = half // tr

    def body(c_ref, a_ref, b_ref, o_ref, ob_ref):
        s = a_ref[...] + b_ref[...]
        o_ref[...] = s
        ob_ref[...] = s.astype(BF16)

    grid_spec = pltpu.PrefetchScalarGridSpec(
        num_scalar_prefetch=1, grid=(4, nb),
        in_specs=[pl.BlockSpec((None, tr, width), lambda s, i, c: (s, c[0] * nb + i, 0)),
                  pl.BlockSpec((None, tr, width), lambda s, i, c: (s, i, 0))],
        out_specs=[pl.BlockSpec((None, tr, width), lambda s, i, c: (s, i, 0)),
                   pl.BlockSpec((None, tr, width), lambda s, i, c: (s, i, 0))],
    )
    return _call(
        body, name=name, grid_spec=grid_spec,
        out_shape=[jax.ShapeDtypeStruct(got.shape, F32), jax.ShapeDtypeStruct(got.shape, BF16)],
        compiler_params=_cp(),
    )(c_arr, gp, got)


class _ScatterChipSums(_Exchange):
    def __init__(self, sums):
        self.inputs = tuple(sums)
        self.out_shapes = tuple(jax.ShapeDtypeStruct((3,) + cs.shape[1:], cs.dtype) for cs in sums)
        self.scratch = (pltpu.SemaphoreType.DMA((3 * len(sums),)), pltpu.SemaphoreType.DMA((3 * len(sums),)))

    def _copies(self, *refs):
        nbuf = len(self.inputs)
        send_sems, recv_sems = refs[2 * nbuf:]
        x, y, c = _place()
        return [pltpu.make_async_remote_copy(
            src_ref=s_ref.at[2 * px + py], dst_ref=o_ref.at[j],
            send_sem=send_sems.at[3 * b + j], recv_sem=recv_sems.at[3 * b + j], device_id=(px, py, c), device_id_type=MESH)
            for b, (s_ref, o_ref) in enumerate(zip(refs[:nbuf], refs[nbuf:2 * nbuf]))
            for j, (px, py) in enumerate(_other_chips(x, y))]

    def start(self, *refs):
        for cp in self._copies(*refs):
            cp.start()

    def finish(self, *refs):
        for cp in self._copies(*refs):
            cp.wait()


def _shard_sum(name, cs, got, kc_arr):
    h, width = cs.shape[1], cs.shape[2]
    tr = _add_rows(h)
    nb = h // tr

    def body(k_ref, a_ref, b_ref, o_ref):
        o_ref[...] = ((a_ref[...] + b_ref[0].astype(F32)) + b_ref[1].astype(F32)) + b_ref[2].astype(F32)

    grid_spec = pltpu.PrefetchScalarGridSpec(
        num_scalar_prefetch=1, grid=(nb,),
        in_specs=[pl.BlockSpec((None, tr, width), lambda i, k: (k[0], i, 0)),
                  pl.BlockSpec((3, tr, width), lambda i, k: (0, i, 0))],
        out_specs=pl.BlockSpec((tr, width), lambda i, k: (k[1] * nb + i, 0)),
    )
    return _call(
        body, name=name, grid_spec=grid_spec,
        out_shape=jax.ShapeDtypeStruct((2 * h, width), F32),
        compiler_params=_cp(),
    )(kc_arr, cs, got)


def _join_halves(name, boths):
    nbuf = len(boths)

    def body(*refs):
        send_sems, recv_sems = refs[2 * nbuf:]
        x, y, c = _place()
        sent, landing = [], []
        for b, (m_ref, o_ref) in enumerate(zip(refs[:nbuf], refs[nbuf:2 * nbuf])):
            h = boths[b].shape[0] // 2
            mine = m_ref.at[pl.ds(c * h, h), :]
            sent.append(pltpu.make_async_remote_copy(
                src_ref=mine, dst_ref=o_ref.at[pl.ds(c * h, h), :],
                send_sem=send_sems.at[b], recv_sem=recv_sems.at[b], device_id=(x, y, 1 - c), device_id_type=MESH))
            landing.append(pltpu.make_async_remote_copy(
                src_ref=mine, dst_ref=o_ref.at[pl.ds((1 - c) * h, h), :],
                send_sem=send_sems.at[b], recv_sem=recv_sems.at[b], device_id=(x, y, 1 - c), device_id_type=MESH))
        for cp in sent:
            cp.start()
        for cp in sent:
            cp.wait_send()
        for cp in landing:
            cp.wait_recv()

    anywhere = pl.BlockSpec(memory_space=pl.ANY)
    return _call(
        body, name=name,
        in_specs=[anywhere] * nbuf, out_specs=[anywhere] * nbuf,
        out_shape=[jax.ShapeDtypeStruct(g.shape, g.dtype) for g in boths],
        input_output_aliases={b: b for b in range(nbuf)},
        scratch_shapes=[pltpu.SemaphoreType.DMA((nbuf,)), pltpu.SemaphoreType.DMA((nbuf,))],
    )(*boths)


def _all_reduce_small(v):
    r = v.shape[0]

    def body(v_ref, o_ref, buf, send_sems, recv_sems):
        x, y, c = _place()
        me = 4 * x + 2 * y + c
        buf[me] = v_ref[...]
        cps = []
        for k in range(1, 8):
            peer = (x ^ (k >> 2), y ^ ((k >> 1) & 1), c ^ (k & 1))
            cps.append(pltpu.make_async_remote_copy(
                src_ref=v_ref, dst_ref=buf.at[me],
                send_sem=send_sems.at[k - 1], recv_sem=recv_sems.at[k - 1], device_id=peer, device_id_type=MESH))
        for cp in cps:
            cp.start()
        for k in range(1, 8):
            pltpu.make_async_remote_copy(
                src_ref=v_ref, dst_ref=buf.at[me ^ k],
                send_sem=send_sems.at[k - 1], recv_sem=recv_sems.at[k - 1],
                device_id=(x, y, c), device_id_type=MESH).wait_recv()
        for cp in cps:
            cp.wait_send()
        acc = buf[0]
        for k in range(1, 8):
            acc = acc + buf[k]
        o_ref[...] = acc

    return _call(
        body, name="all_reduce_small",
        in_specs=[pl.BlockSpec(memory_space=pltpu.VMEM)],
        out_specs=pl.BlockSpec(memory_space=pltpu.VMEM),
        out_shape=jax.ShapeDtypeStruct((r, 128), F32),
        scratch_shapes=[pltpu.VMEM((8, r, 128), F32), pltpu.SemaphoreType.DMA((7,)), pltpu.SemaphoreType.DMA((7,))],
    )(v)


def _group(names):
    return tuple(e for e in BIG if e[0] in names)


def _pack(shards, dtype):
    return jnp.concatenate([s.astype(dtype).reshape(-1, PACK_W) for s in shards], axis=0)


def _unpack_full(g, group):
    out, at = {}, 0
    for name, rows, cols, axis in group:
        n = rows * cols // 4 // PACK_W
        blk = g[:, at:at + n, :]
        at += n
        if axis == 1:
            out[name] = blk.reshape(4, rows, cols // 4).transpose(1, 0, 2).reshape(rows, cols)
        else:
            out[name] = blk.reshape(rows, cols)
    return out


def _pack_grads(grads, group):
    parts = []
    for name, rows, cols, axis in group:
        g = grads[name]
        if axis == 1:
            g = g.reshape(rows, 4, cols // 4).transpose(1, 0, 2)
        parts.append(g.reshape(4, -1, PACK_W))
    rows_total = sum(p.shape[1] for p in parts)
    pad = -rows_total % PACK_ALIGN
    if pad:
        parts.append(jnp.zeros((4, pad, PACK_W), F32))
    return jnp.concatenate(parts, axis=1)


def _unpack_shard(s, group):
    out, at = {}, 0
    for name, rows, cols, axis in group:
        n = rows * cols // 4 // PACK_W
        shape = (rows, cols // 4) if axis == 1 else (rows // 4, cols)
        out[name] = s[at:at + n, :].reshape(shape)
        at += n
    return out


def _pack_small(parts):
    flat = jnp.concatenate([p.reshape(-1) for p in parts])
    pad = -flat.shape[0] % 1024
    return jnp.concatenate([flat, jnp.zeros((pad,), F32)]).reshape(-1, 128)


def _ffn_in(tag, h, gain, w_in, side=None):
    t = h.shape[0]
    wide = DFF // 2

    def compute_in(rows, weights, outs):
        hv, w_ref = rows[0][...], weights[0]
        r = lax.rsqrt(jnp.mean(hv * hv, axis=-1, keepdims=True) + EPS)
        a = (hv * r * weights[1][...]).astype(BF16)
        outs[0][...] = a
        for s in range(2):
            cols = slice(s * wide, (s + 1) * wide)
            gate = _dot(a, w_ref[s])
            up = _dot(a, w_ref[2 + s])
            outs[1][:, cols] = gate.astype(BF16)
            outs[2][:, cols] = up.astype(BF16)
            outs[3][:, cols] = (_silu(gate) * up).astype(BF16)

    return _rows_call(tag + "_in", [h], [w_in, gain], [(D, BF16)] + [(DFF, BF16)] * 3, compute_in, min(FFN_TM, t),
                      side=side)


def _ffn_out(tag, act, h, w_out, next_gain, target=None):
    t = h.shape[0]
    tm = min(FFN_TM, t)

    def compute_out(rows, weights, outs):
        hn = rows[1][...] + 0.5 * _dot(rows[0][...], weights[0][...])
        g = weights[1][...]
        r = lax.rsqrt(jnp.mean(hn * hn, axis=-1, keepdims=True) + EPS)
        xh = hn * r
        if target is None:
            outs[0][...] = hn
            outs[1][...] = (xh * g).astype(BF16)
        else:
            err = xh * g - rows[2][...]
            dy = err * (1.0 / D)
            dxh = dy * g
            outs[0][...] = r * (dxh - xh * jnp.mean(dxh * xh, axis=-1, keepdims=True))
            outs[1][...] += jnp.sum(dy * xh, axis=0, keepdims=True)
            outs[2][...] += 0.5 * jnp.sum(jnp.mean(err * err, axis=-1, keepdims=True), axis=0, keepdims=True)

    if target is None:
        return _rows_call(tag + "_out", [act, h], [w_out, next_gain], [(D, F32), (D, BF16)], compute_out, tm)
    return _rows_call(tag + "_out", [act, h, target], [w_out, next_gain], [(D, F32)], compute_out, tm, sums=(D, 128))


class _Reduction:
    def __init__(self, tag, c_arr, k_arr):
        self.tag, self.c_arr, self.k_arr = tag, c_arr, k_arr

    def begin(self, bufs):
        swapped = _swap_halves("grad_swap_" + self.tag, bufs)
        sums = [_chip_sum("grad_chip_sum_%s%d" % (self.tag, b), gp, got, self.c_arr)
                for b, (gp, got) in enumerate(zip(bufs, swapped))]
        self.sums = [s[0] for s in sums]
        return _ScatterChipSums([s[1] for s in sums])

    def end(self, got):
        mine = [_shard_sum("grad_shard_sum_%s%d" % (self.tag, b), cs, g, self.k_arr)
                for b, (cs, g) in enumerate(zip(self.sums, got))]
        return _join_halves("grad_join_" + self.tag, mine)


def _ffn_bwd(tag, h, gain, w_in, w_out, saved, dout, side, reduction):
    t = h.shape[0]
    tm = min(TM, t)
    n, gate, up, act = saved

    def compute(rows, weights, outs):
        d = rows[0][...].astype(BF16)
        for j in range(DFF // FFN_CHUNK):
            cols = slice(j * FFN_CHUNK, (j + 1) * FFN_CHUNK)
            da = 0.5 * _dot_nt(d, weights[0][cols, :])
            g, u = rows[1][:, cols].astype(F32), rows[2][:, cols].astype(F32)
            s = _sig(g)
            silu = g * s
            outs[0][:, cols] = (da * u * (s + silu * (1.0 - s))).astype(BF16)
            outs[1][:, cols] = (da * silu).astype(BF16)

    dgate, dup, *side_out = _rows_call(tag + "_dact", [dout, gate, up], [w_out], [(DFF, BF16)] * 2, compute,
                                       min(FFN_TM, t), side=side)
    dw_out = _mm_tn(tag + "_dw_out", act, dout, scale=0.5, tm=DFF // 2, tn=D)
    dw_g = _mm_tn(tag + "_dw_gate", n, dgate, tm=D, tn=DFF // 2, column_shards=True)
    dw_u = _mm_tn(tag + "_dw_up", n, dup, tm=D, tn=DFF // 2, column_shards=True)
    sending = reduction.begin([jnp.concatenate([dw_g, dw_u], axis=0), dw_out.reshape(4, DFF // 4, D)])

    def compute_dn(rows, weights, outs):
        w_ref = weights[0]
        wide = DFF // 2
        dn = jnp.zeros((rows[0].shape[0], D), F32)
        for s in range(2):
            cols = slice(s * wide, (s + 1) * wide)
            dn = dn + _dot_nt(rows[0][:, cols], w_ref[s]) + _dot_nt(rows[1][:, cols], w_ref[2 + s])
        dx, dg = _rms_bwd_vals(rows[2][...], weights[1][...], dn)
        outs[0][...] = rows[3][...] + dx
        outs[1][...] += jnp.sum(dg, axis=0, keepdims=True)

    dh, dgain, *got = _rows_call(tag + "_dn", [dgate, dup, h, dout], [w_in, gain], [(D, F32)], compute_dn,
                                 min(FFN_TM, t), side=sending, sums=(D,), vmem_mb=58)
    return dh, dgain, side_out, got


def kernel(x, positions, ffn1_norm, ffn1_w_in, ffn1_w_out, mix_norm, w_in, hg_lb_table, hg_out_norm, w_hg_branch, mla_q_lora_norm, w_q_up, mla_kv_lora_norm, w_kv_up, q_head_norm, k_head_norm, w_mla_branch, w_merge, b_merge, w_out, ffn2_norm, ffn2_w_in, ffn2_w_out, final_norm, loss_target, m_ffn1_norm, m_ffn1_w_in, m_ffn1_w_out, m_mix_norm, m_w_in, m_hg_lb_table, m_hg_out_norm, m_w_hg_branch, m_mla_q_lora_norm, m_w_q_up, m_mla_kv_lora_norm, m_w_kv_up, m_q_head_norm, m_k_head_norm, m_w_mla_branch, m_w_merge, m_b_merge, m_w_out, m_ffn2_norm, m_ffn2_w_in, m_ffn2_w_out, m_final_norm, v_ffn1_norm, v_ffn1_w_in, v_ffn1_w_out, v_mix_norm, v_w_in, v_hg_lb_table, v_hg_out_norm, v_w_hg_branch, v_mla_q_lora_norm, v_w_q_up, v_mla_kv_lora_norm, v_w_kv_up, v_q_head_norm, v_k_head_norm, v_w_mla_branch, v_w_merge, v_b_merge, v_w_out, v_ffn2_norm, v_ffn2_w_in, v_ffn2_w_out, v_final_norm):
    a = dict(locals())
    w = {n: a[n] for n in WEIGHT_ORDER}
    mom = {n: a["m_" + n] for n in WEIGHT_ORDER}
    var = {n: a["v_" + n] for n in WEIGHT_ORDER}
    t = x.shape[1]
    tm = min(TM, t)
    xt = x.reshape(t, D)
    target = loss_target.reshape(t, D)
    pos = positions.reshape(t, 1)
    x_i, y_i, c_i = _place()
    k_idx = (2 * x_i + y_i).astype(jnp.int32)
    c_arr = c_i.astype(jnp.int32).reshape(1)
    k_arr = jnp.stack([k_idx, c_i.astype(jnp.int32)])

    group_first = _group(("ffn1_w_in", "ffn1_w_out"))
    group_mid = _group(("w_in", "w_hg_branch", "w_q_up", "w_kv_up", "w_mla_branch", "w_merge", "w_out"))
    group_last = _group(("ffn2_w_in", "ffn2_w_out"))
    use_early = _group(("ffn1_w_out", "w_in", "w_hg_branch", "w_q_up", "w_kv_up"))
    use_late = _group(("w_mla_branch", "w_merge", "w_out", "ffn2_w_out"))
    gather_first = _GatherWeights([w["ffn1_w_in"][0].astype(BF16)])
    gather_early = _GatherWeights([_pack([w[e[0]][0] for e in use_early], BF16)])
    gather_late = _GatherWeights([_pack([w[e[0]][0] for e in use_late], BF16), w["ffn2_w_in"][0].astype(BF16)])
    (ffn1_w_in_g,) = gather_first.gathered(gather_first.alone("gather_first"), k_idx)
    n1, gate1, up1, act1, got = _ffn_in("ffn1", xt, w["ffn1_norm"], ffn1_w_in_g, gather_early)
    full = _unpack_full(gather_early.gathered([got], k_idx)[0], use_early)
    h1, u = _ffn_out("ffn1", act1, xt, full["ffn1_w_out"], w["mix_norm"])
    ffn1_saved = (n1, gate1, up1, act1)
    w_in_full = full["w_in"]
    w_in_hg = w_in_full[:, :4 * D]
    w_in_mla = jnp.pad(w_in_full[:, 4 * D:], ((0, 0), (0, MLA_COLS - (4800 - 4 * D))))
    w_q_pad = jnp.pad(full["w_q_up"].reshape(Q_LORA, HEADS, QK), ((0, 0), (0, 0), (0, QKP - QK))).reshape(Q_LORA, HEADS * QKP)
    w_kv = full["w_kv_up"]
    gq = jnp.pad(w["q_head_norm"], ((0, 0), (0, QKP - QK)))
    gk = jnp.pad(w["k_head_norm"], ((0, 0), (0, QKP - QK)))

    ident = lambda accs, ex: (accs[0],)
    def in_hg(rows, weights, outs):
        a = rows[0][...]
        for j in range(4 * D // 512):
            cols = slice(j * 512, (j + 1) * 512)
            outs[0][:, cols] = _dot(a, weights[0][:, cols])

    (p_hg,) = _rows_call("in_hg", [u], [w_in_hg], [(4 * D, F32)], in_hg, min(FFN_TM, t))
    (p_mla,) = _mm("in_mla", [_a_spec(u, tm)], [_b_nn(w_in_mla, MLA_COLS)], [(0, 0)], ident, [], [F32], t, MLA_COLS, tm, MLA_COLS)
    o_raw, hg_o, states = _hgrn_fwd(p_hg, w["hg_lb_table"], w["hg_out_norm"])
    (y_hg,) = _mm("hg_branch", [_a_spec(hg_o, tm)], [_b_nn(full["w_hg_branch"], 512)], [(0, 0)], ident, [], [BF16], t, D, tm, 512)
    cqn, ckvn = _lora_norm_fwd(p_mla, w["mla_q_lora_norm"], w["mla_kv_lora_norm"])
    (qf,) = _mm("q_up", [_a_spec(cqn, tm)], [_b_nn(w_q_pad, 512)], [(0, 0)], ident, [], [F32], t, HEADS * QKP, tm, 512)
    (kvf,) = _mm("kv_up", [_a_spec(ckvn, tm)], [_b_nn(w_kv, 512)], [(0, 0)], ident, [], [F32], t, HEADS * QKP, tm, 512)
    cos, sin = _rope_tables(pos)
    qh, kh, vh = _mla_prep_fwd(qf, kvf, p_mla, cos, sin, gq, gk)
    o_mla, lse, *got = _flash_fwd(qh, kh, vh, side=gather_late)
    late, ffn2_w_in_g = gather_late.gathered(got, k_idx)
    full.update(_unpack_full(late, use_late))
    (y_mla,) = _mm("mla_branch", [_a_spec(o_mla, tm)], [_b_nn(full["w_mla_branch"], 512)], [(0, 0)], ident, [], [BF16], t, D, tm, 512)

    def merge_epi(accs, ex):
        g_hg = _sig(accs[0] + ex[2])
        g_mla = _sig(accs[1] + ex[3])
        return g_hg * ex[0].astype(F32) + g_mla * ex[1].astype(F32), g_hg, g_mla

    w_merge_f = full["w_merge"]
    mix, g_hg, g_mla = _mm(
        "merge", [_a_spec(u, tm)], [_b_nn(w_merge_f, 512), _b_nn(w_merge_f, 512, D // 512)], [(0, 0), (0, 1)], merge_epi,
        [_e_tile(y_hg, tm, 512), _e_tile(y_mla, tm, 512), _e_row(w["b_merge"], 512), _e_row(w["b_merge"], 512, D // 512)],
        [BF16, BF16, BF16], t, D, tm, 512)
    (h2,) = _mm("out_proj", [_a_spec(mix, tm)], [_b_nn(full["w_out"], 512)], [(0, 0)],
                lambda accs, ex: (ex[0] + accs[0],), [_e_tile(h1, tm, 512)], [F32], t, D, tm, 512)
    ffn2_saved = _ffn_in("ffn2", h2, w["ffn2_norm"], ffn2_w_in_g)
    dh3, d_final_norm, loss_part = _ffn_out("ffn2", ffn2_saved[3], h2, full["ffn2_w_out"], w["final_norm"], target=target)

    grads, small = {}, {}
    small["final_norm"] = d_final_norm
    reduce_last = _Reduction("last", c_arr, k_arr)
    reduce_mid = _Reduction("mid", c_arr, k_arr)
    reduce_first = _Reduction("first", c_arr, k_arr)
    dh2, small["ffn2_norm"], _, got_last = _ffn_bwd(
        "ffn2", h2, w["ffn2_norm"], ffn2_w_in_g, full["ffn2_w_out"], ffn2_saved, dh3, None, reduce_last)

    def dmix_epi(accs, ex):
        dm = accs[0]
        ghg, gml, yhg, yml = [e.astype(F32) for e in ex]
        return dm * ghg, dm * gml, dm * yhg * ghg * (1.0 - ghg), dm * yml * gml * (1.0 - gml)

    dy_hg, dy_mla, dpre_hg, dpre_mla = _mm(
        "d_mix", [_a_spec(dh2, tm)], [_b_nt(full["w_out"], 512)], [(0, 0)], dmix_epi,
        [_e_tile(g_hg, tm, 512), _e_tile(g_mla, tm, 512), _e_tile(y_hg, tm, 512), _e_tile(y_mla, tm, 512)],
        [BF16, BF16, BF16, BF16], t, D, tm, 512, trans_b=True)
    grads["w_out"] = _mm_tn("dw_out", mix, dh2)
    small["b_merge"] = jnp.concatenate([_colsum("db_hg", dpre_hg), _colsum("db_mla", dpre_mla)], axis=1)
    grads["w_merge"] = jnp.concatenate([_mm_tn("dw_merge_hg", u, dpre_hg), _mm_tn("dw_merge_mla", u, dpre_mla)], axis=1)
    grads["w_hg_branch"] = _mm_tn("dw_hg_branch", hg_o, dy_hg)
    grads["w_mla_branch"] = _mm_tn("dw_mla_branch", o_mla, dy_mla)
    (dho,) = _mm("d_hg_o", [_a_spec(dy_hg, tm)], [_b_nt(full["w_hg_branch"], 512)], [(0, 0)], ident, [], [BF16], t, D, tm, 512, trans_b=True)
    (do_mla,) = _mm("d_o_mla", [_a_spec(dy_mla, tm)], [_b_nt(full["w_mla_branch"], 512)], [(0, 0)], ident, [], [BF16], t, D, tm, 512, trans_b=True)

    dq_raw, df_raw, di_raw, dg_raw, small["hg_lb_table"], small["hg_out_norm"] = _hgrn_bwd(
        p_hg, w["hg_lb_table"], w["hg_out_norm"], o_raw, states, dho)
    dp_hg = [dq_raw, df_raw, di_raw, dg_raw]

    dqh, dkh, dvh = _flash_bwd(qh, kh, vh, lse, _attn_do(do_mla, o_mla))
    dqf, dkvf, dkpe, dgq, dgk = _mla_prep_bwd(qf, kvf, p_mla, cos, sin, gq, gk, dqh, dkh, dvh)
    small["q_head_norm"] = dgq[:, :QK]
    small["k_head_norm"] = dgk[:, :QK]
    dwq_pad = _mm_tn("dw_q_up", cqn, dqf, tm=Q_LORA, tn=1024)
    grads["w_q_up"] = dwq_pad.reshape(Q_LORA, HEADS, QKP)[:, :, :QK].reshape(Q_LORA, HEADS * QK)
    grads["w_kv_up"] = _mm_tn("dw_kv_up", ckvn, dkvf, tm=KV_LORA, tn=1024)
    (dcqn,) = _mm("d_cq", [_a_spec(dqf, tm)], [_b_nt(w_q_pad, Q_LORA)], [(0, 0)], ident, [], [F32], t, Q_LORA, tm, Q_LORA, trans_b=True)
    (dckvn,) = _mm("d_ckv", [_a_spec(dkvf, tm)], [_b_nt(w_kv, KV_LORA)], [(0, 0)], ident, [], [F32], t, KV_LORA, tm, KV_LORA, trans_b=True)
    dp_mla, small["mla_q_lora_norm"], small["mla_kv_lora_norm"] = _lora_norm_bwd(
        p_mla, w["mla_q_lora_norm"], w["mla_kv_lora_norm"], dcqn, dckvn, dkpe)

    dw_in_hg = [_mm_tn("dw_in_hg%d" % k, u, dp_hg[k]) for k in range(4)]
    dw_in_mla = _mm_tn("dw_in_mla", u, dp_mla, tn=MLA_COLS)
    grads["w_in"] = jnp.concatenate(dw_in_hg + [dw_in_mla[:, :4800 - 4 * D]], axis=1)
    tm_du = min(TM // 2, t)
    du, *got_mid = _mm(
        "d_u",
        [_a_spec(dpre_hg, tm_du), _a_spec(dpre_mla, tm_du)] + [_a_spec(d, tm_du) for d in dp_hg] + [_a_spec(dp_mla, tm_du)],
        [_b_nt(w_merge_f, 512, D, 0), _b_nt(w_merge_f, 512, D, 1)]
        + [_b_nt(w_in_hg, 512, D, k) for k in range(4)] + [_b_nt(w_in_mla, 512)],
        [(k, k) for k in range(7)],
        lambda accs, ex: (functools.reduce(lambda p, q: p + q, accs),), [], [F32], t, D, tm_du, 512, trans_b=True,
        side=reduce_mid.begin([_pack_grads(grads, group_mid)]))
    dh1, small["mix_norm"] = _rms_bwd("mix_dnorm", h1, w["mix_norm"], du, dh2)
    dx, small["ffn1_norm"], _, got_first = _ffn_bwd(
        "ffn1", xt, w["ffn1_norm"], ffn1_w_in_g, full["ffn1_w_out"], ffn1_saved, dh1, None, reduce_first)

    g_shard = _unpack_shard(reduce_mid.end(got_mid)[0], group_mid)
    g_shard["ffn2_w_in"], g_shard["ffn2_w_out"] = reduce_last.end(got_last)
    g_shard["ffn1_w_in"], g_shard["ffn1_w_out"] = reduce_first.end(got_first)
    small_sum = _all_reduce_small(_pack_small([small[n] for n, _ in SMALL] + [loss_part])).reshape(-1)
    g_small, at = {}, 0
    for n, shape in SMALL:
        size = shape[0] * shape[1]
        g_small[n] = small_sum[at:at + size].reshape(shape)
        at += size
    loss = small_sum[at]

    g_out, d_out, m_out, v_out = {}, {}, {}, {}
    for n in WEIGHT_ORDER:
        shape = w[n].shape
        g = g_shard[n] if n in g_shard else g_small[n]
        two = g.shape
        d_, m_, v_ = _adamw("adamw_" + n, w[n].reshape(two), g, mom[n].reshape(two), var[n].reshape(two))
        g_out[n], d_out[n], m_out[n], v_out[n] = g.reshape(shape), d_.reshape(shape), m_.reshape(shape), v_.reshape(shape)

    return (loss, dx.reshape(x.shape), *[g_out[n] for n in WEIGHT_ORDER], *[d_out[n] for n in WEIGHT_ORDER],
            *[m_out[n] for n in WEIGHT_ORDER], *[v_out[n] for n in WEIGHT_ORDER])
```

```python
import functools

import numpy as np
import jax
import jax.numpy as jnp
from jax import lax
from jax.experimental import pallas as pl
from jax.experimental.pallas import tpu as pltpu

F32 = jnp.float32
BF16 = jnp.bfloat16
MESH = pl.DeviceIdType.MESH

D = 1024
DFF = 2816
HEADS = 8
HK = 128
CHUNK = 64
ROPE = 64
QK = 192
QKP = 256
Q_LORA = 384
KV_LORA = 256
MLA_COLS = 768
EPS = 1e-6
ROPE_THETA = 10000.0
SCALE = QK ** -0.5
LOG2E = 1.4426950408889634
LN2 = 0.6931471805599453
NEG = -1e30
EXP_CLAMP = 80.0

ADAM_LR = 0.001
ADAM_B1 = 0.9
ADAM_B2 = 0.999
ADAM_EPS = 1e-08
ADAM_WD = 0.01
ADAM_STEP = 10

PACK_W = 1024
ADD_ROWS = 352
PACK_ALIGN = 2 * ADD_ROWS

TM = 1024
FFN_TM = 512
FFN_CHUNK = 256
FFN_MAIN = 1280
TQ = 1024
SUBQ = 256
HG_BT = 512
HG_HPB = 8
TT = 1024
ROW_TM = 256

VMEM_MB = 48

BIG = (
    ("ffn1_w_in", D, 2 * DFF, 1),
    ("ffn1_w_out", DFF, D, 0),
    ("w_in", D, 4800, 1),
    ("w_hg_branch", D, D, 0),
    ("w_q_up", Q_LORA, HEADS * QK, 1),
    ("w_kv_up", KV_LORA, HEADS * 2 * HK, 1),
    ("w_mla_branch", D, D, 0),
    ("w_merge", D, 2 * D, 1),
    ("w_out", D, D, 0),
    ("ffn2_w_in", D, 2 * DFF, 1),
    ("ffn2_w_out", DFF, D, 0),
)
SMALL = (
    ("ffn1_norm", (1, D)),
    ("mix_norm", (1, D)),
    ("hg_lb_table", (2, D)),
    ("hg_out_norm", (1, HK)),
    ("mla_q_lora_norm", (1, Q_LORA)),
    ("mla_kv_lora_norm", (1, KV_LORA)),
    ("q_head_norm", (1, QK)),
    ("k_head_norm", (1, QK)),
    ("b_merge", (1, 2 * D)),
    ("ffn2_norm", (1, D)),
    ("final_norm", (1, D)),
)
WEIGHT_ORDER = ("ffn1_norm", "ffn1_w_in", "ffn1_w_out", "mix_norm", "w_in", "hg_lb_table", "hg_out_norm",
                "w_hg_branch", "mla_q_lora_norm", "w_q_up", "mla_kv_lora_norm", "w_kv_up", "q_head_norm",
                "k_head_norm", "w_mla_branch", "w_merge", "b_merge", "w_out", "ffn2_norm", "ffn2_w_in",
                "ffn2_w_out", "final_norm")


def _call(body, **kw):
    return pl.pallas_call(body, **kw)


def _cp(vmem_mb=VMEM_MB):
    return pltpu.CompilerParams(vmem_limit_bytes=vmem_mb << 20)


def _dot(a, b):
    return lax.dot_general(a, b, (((1,), (0,)), ((), ())), preferred_element_type=F32)


def _dot_nt(a, b):
    return lax.dot_general(a, b, (((1,), (1,)), ((), ())), preferred_element_type=F32)


def _dot_tn(a, b):
    return lax.dot_general(a, b, (((0,), (0,)), ((), ())), preferred_element_type=F32)


def _sig(x):
    return jax.nn.sigmoid(x)


def _silu(x):
    return x * _sig(x)


def _dsilu(x):
    s = _sig(x)
    return s * (1.0 + x * (1.0 - s))


def _a_spec(arr, tm, kblk=None, kidx=0):
    kb = arr.shape[1] if kblk is None else kblk
    return arr, pl.BlockSpec((tm, kb), lambda i, j, kidx=kidx: (i, kidx)), slice(kidx * kb, (kidx + 1) * kb)


def _b_nn(arr, tn, off=0):
    return arr, pl.BlockSpec((arr.shape[0], tn), lambda i, j, off=off: (0, j + off)), ("cols", off)


def _b_nt(arr, tn, kblk=None, kidx=0):
    kb = arr.shape[1] if kblk is None else kblk
    return arr, pl.BlockSpec((tn, kb), lambda i, j, kidx=kidx: (j, kidx)), ("rows", slice(kidx * kb, (kidx + 1) * kb))


def _e_tile(arr, tm, tn, off=0):
    return arr, pl.BlockSpec((tm, tn), lambda i, j, off=off: (i, j + off)), ("tile", off)


def _e_row(arr, tn, off=0):
    return arr, pl.BlockSpec((1, tn), lambda i, j, off=off: (0, j + off)), ("row", off)


def _mm_resident(name, As, Bs, dots, epi, extras, out_dtypes, m, n, tn):
    def unique(arrays):
        seen = []
        for a in arrays:
            if not any(a is s for s in seen):
                seen.append(a)
        return seen

    rows = unique([a for a, _, _ in As] + [e for e, _, where in extras if where[0] == "tile"])
    weights = unique([b for b, _, _ in Bs] + [e for e, _, where in extras if where[0] == "row"])

    def ref_of(arr, row_refs, weight_refs):
        for r, ref in zip(rows, row_refs):
            if r is arr:
                return ref
        for wt, ref in zip(weights, weight_refs):
            if wt is arr:
                return ref

    def compute(row_refs, weight_refs, out_refs):
        a_vals = [ref_of(a, row_refs, weight_refs)[:, ks].astype(BF16) for a, _, ks in As]
        for j in range(n // tn):
            accs = []
            for ai, bi in dots:
                b, _, where = Bs[bi]
                b_ref = ref_of(b, row_refs, weight_refs)
                if where[0] == "cols":
                    accs.append(_dot(a_vals[ai], b_ref[:, (j + where[1]) * tn:(j + where[1] + 1) * tn]))
                else:
                    accs.append(_dot_nt(a_vals[ai], b_ref[j * tn:(j + 1) * tn, where[1]]))
            ex = [ref_of(e, row_refs, weight_refs)[:, (j + where[1]) * tn:(j + where[1] + 1) * tn]
                  for e, _, where in extras]
            for o_ref, o in zip(out_refs, epi(accs, ex)):
                o_ref[:, j * tn:(j + 1) * tn] = o.astype(o_ref.dtype)

    return _rows_call(name, rows, weights, [(n, dt) for dt in out_dtypes], compute, min(FFN_TM, m))


def _mm(name, As, Bs, dots, epi, extras, out_dtypes, m, n, tm, tn, trans_b=False, side=None):
    if side is None:
        return _mm_resident(name, As, Bs, dots, epi, extras, out_dtypes, m, n, tn)
    na, nb, ne, no = len(As), len(Bs), len(extras), len(out_dtypes)
    ni, nj = m // tm, n // tn
    s_in = len(side.inputs) if side else 0
    s_out = len(side.out_shapes) if side else 0

    def body(*refs):
        a_refs = refs[:na]
        b_refs = refs[na:na + nb]
        e_refs = refs[na + nb:na + nb + ne]
        at = na + nb + ne
        side_refs = refs[at:at + s_in]
        o_refs = refs[at + s_in:at + s_in + no]
        side_refs = list(side_refs) + list(refs[at + s_in + no:])
        if side:
            i, j = pl.program_id(0), pl.program_id(1)

            @pl.when(jnp.logical_and(i == 0, j == 0))
            def _():
                side.start(*side_refs)

        a_vals = [r[...].astype(BF16) for r in a_refs]
        accs = []
        for ai, bi in dots:
            b = b_refs[bi][...]
            accs.append(_dot_nt(a_vals[ai], b) if trans_b else _dot(a_vals[ai], b))
        outs = epi(accs, [r[...] for r in e_refs])
        for o_ref, o in zip(o_refs, outs):
            o_ref[...] = o.astype(o_ref.dtype)
        if side:
            @pl.when(jnp.logical_and(i == ni - 1, j == nj - 1))
            def _():
                side.finish(*side_refs)

    ops = list(As) + list(Bs) + list(extras)
    anywhere = pl.BlockSpec(memory_space=pl.ANY)
    res = _call(
        body, name=name,
        grid=(ni, nj),
        in_specs=[op[1] for op in ops] + [anywhere] * s_in,
        out_specs=[pl.BlockSpec((tm, tn), lambda i, j: (i, j)) for _ in out_dtypes] + [anywhere] * s_out,
        out_shape=[jax.ShapeDtypeStruct((m, n), dt) for dt in out_dtypes] + (list(side.out_shapes) if side else []),
        scratch_shapes=list(side.scratch) if side else [],
        compiler_params=_cp(),
    )(*[op[0] for op in ops], *(side.inputs if side else []))
    return res


def _rows_call(name, rows, weights, outs, compute, tm, side=None, sums=(), vmem_mb=VMEM_MB):
    t = rows[0].shape[0]
    nr, nw, no = len(rows), len(weights), len(outs) + len(sums)
    ni = t // tm
    s_in = len(side.inputs) if side else 0
    s_out = len(side.out_shapes) if side else 0

    def body(*refs):
        at = nr + nw
        side_refs = list(refs[at:at + s_in]) + list(refs[at + s_in + no:])
        if side:
            @pl.when(pl.program_id(0) == 0)
            def _():
                side.start(*side_refs)

        out_refs = refs[at + s_in:at + s_in + no]
        if sums:
            @pl.when(pl.program_id(0) == 0)
            def _():
                for r in out_refs[len(outs):]:
                    r[...] = jnp.zeros_like(r)

        compute(refs[:nr], refs[nr:at], out_refs)
        if side:
            @pl.when(pl.program_id(0) == ni - 1)
            def _():
                side.finish(*side_refs)

    anywhere = pl.BlockSpec(memory_space=pl.ANY)
    return _call(
        body, name=name, grid=(ni,),
        in_specs=[pl.BlockSpec((tm, r.shape[1]), lambda i: (i, 0)) for r in rows]
        + [pl.BlockSpec(wt.shape, lambda i, nd=wt.ndim: (0,) * nd) for wt in weights] + [anywhere] * s_in,
        out_specs=[pl.BlockSpec((tm, width), lambda i: (i, 0)) for width, _ in outs]
        + [pl.BlockSpec((1, width), lambda i: (0, 0)) for width in sums] + [anywhere] * s_out,
        out_shape=[jax.ShapeDtypeStruct((t, width), dt) for width, dt in outs]
        + [jax.ShapeDtypeStruct((1, width), F32) for width in sums] + (list(side.out_shapes) if side else []),
        scratch_shapes=list(side.scratch) if side else [],
        compiler_params=_cp(vmem_mb),
    )(*rows, *weights, *(side.inputs if side else []))


def _mm_tn(name, a, b, scale=1.0, tm=1024, tn=1024, stacked=None, into=None):
    t, m = a.shape
    n = b.shape[1]
    tm, tn, tt = min(tm, m), min(tn, n), min(TT, t)
    nk = t // tt

    def body(a_ref, b_ref, *rest):
        o_ref = rest[-1]
        k = pl.program_id(2)

        @pl.when(k == 0)
        def _():
            o_ref[...] = jnp.zeros_like(o_ref)

        o_ref[...] += _dot_tn(a_ref[...].astype(BF16), b_ref[...].astype(BF16))
        if scale != 1.0:
            @pl.when(k == nk - 1)
            def _():
                o_ref[...] = o_ref[...] * scale

    return _call(
        body, name=name,
        grid=(m // tm, n // tn, nk),
        in_specs=[pl.BlockSpec((tt, tm), lambda i, j, k: (k, i)), pl.BlockSpec((tt, tn), lambda i, j, k: (k, j))]
        + ([pl.BlockSpec(memory_space=pl.ANY)] if into is not None else []),
        out_specs=(pl.BlockSpec((None, tm, tn), lambda i, j, k: (stacked[1] + j, i, 0)) if stacked
                   else pl.BlockSpec((tm, tn), lambda i, j, k: (i, j))),
        out_shape=jax.ShapeDtypeStruct((stacked[0], m, tn) if stacked else (m, n), F32),
        input_output_aliases={2: 0} if into is not None else {},
        compiler_params=_cp(),
    )(a, b, *([into] if into is not None else []))


def _rms_bwd_vals(xv, g, dn):
    r = lax.rsqrt(jnp.mean(xv * xv, axis=-1, keepdims=True) + EPS)
    xh = xv * r
    dxh = dn * g
    c = jnp.mean(dxh * xh, axis=-1, keepdims=True)
    return r * (dxh - xh * c), dn * xh


def _rms_bwd(name, x, gain, dn, dres):
    t, d = x.shape
    tm = min(ROW_TM, t)

    def body(x_ref, g_ref, dn_ref, dr_ref, dx_ref, dg_ref):
        @pl.when(pl.program_id(0) == 0)
        def _():
            dg_ref[...] = jnp.zeros_like(dg_ref)

        dx, dg = _rms_bwd_vals(x_ref[...], g_ref[...], dn_ref[...].astype(F32))
        dx_ref[...] = dr_ref[...] + dx
        dg_ref[...] += jnp.sum(dg, axis=0, keepdims=True)

    row = pl.BlockSpec((tm, d), lambda i: (i, 0))
    one = pl.BlockSpec((1, d), lambda i: (0, 0))
    return _call(
        body, name=name, grid=(t // tm,),
        in_specs=[row, one, row, row],
        out_specs=[row, one],
        out_shape=[jax.ShapeDtypeStruct((t, d), F32), jax.ShapeDtypeStruct((1, d), F32)],
        compiler_params=_cp(),
    )(x, gain, dn, dres)


def _colsum(name, x):
    t, n = x.shape
    tm = min(TM, t)

    def body(x_ref, o_ref):
        @pl.when(pl.program_id(0) == 0)
        def _():
            o_ref[...] = jnp.zeros_like(o_ref)

        o_ref[...] += jnp.sum(x_ref[...].astype(F32), axis=0, keepdims=True)

    return _call(
        body, name=name, grid=(t // tm,),
        in_specs=[pl.BlockSpec((tm, n), lambda i: (i, 0))],
        out_specs=pl.BlockSpec((1, n), lambda i: (0, 0)),
        out_shape=jax.ShapeDtypeStruct((1, n), F32),
        compiler_params=_cp(),
    )(x)


def _lora_norm_fwd(p_mla, gq, gkv):
    t = p_mla.shape[0]
    tm = min(ROW_TM, t)

    def body(p_ref, gq_ref, gkv_ref, q_ref, kv_ref):
        cq = p_ref[:, 0:Q_LORA]
        ckv = p_ref[:, Q_LORA:Q_LORA + KV_LORA]
        rq = lax.rsqrt(jnp.mean(cq * cq, axis=-1, keepdims=True) + EPS)
        rkv = lax.rsqrt(jnp.mean(ckv * ckv, axis=-1, keepdims=True) + EPS)
        q_ref[...] = (cq * rq * gq_ref[...]).astype(BF16)
        kv_ref[...] = (ckv * rkv * gkv_ref[...]).astype(BF16)

    return _call(
        body, name="lora_norm_fwd", grid=(t // tm,),
        in_specs=[pl.BlockSpec((tm, MLA_COLS), lambda i: (i, 0)),
                  pl.BlockSpec((1, Q_LORA), lambda i: (0, 0)), pl.BlockSpec((1, KV_LORA), lambda i: (0, 0))],
        out_specs=[pl.BlockSpec((tm, Q_LORA), lambda i: (i, 0)), pl.BlockSpec((tm, KV_LORA), lambda i: (i, 0))],
        out_shape=[jax.ShapeDtypeStruct((t, Q_LORA), BF16), jax.ShapeDtypeStruct((t, KV_LORA), BF16)],
        compiler_params=_cp(),
    )(p_mla, gq, gkv)


def _lora_norm_bwd(p_mla, gq, gkv, dcqn, dckvn, dkpe):
    t = p_mla.shape[0]
    tm = min(ROW_TM, t)

    def body(p_ref, gq_ref, gkv_ref, dq_ref, dkv_ref, dkpe_ref, dp_ref, dgq_ref, dgkv_ref):
        @pl.when(pl.program_id(0) == 0)
        def _():
            dgq_ref[...] = jnp.zeros_like(dgq_ref)
            dgkv_ref[...] = jnp.zeros_like(dgkv_ref)

        dcq, dgq = _rms_bwd_vals(p_ref[:, 0:Q_LORA], gq_ref[...], dq_ref[...])
        dckv, dgkv = _rms_bwd_vals(p_ref[:, Q_LORA:Q_LORA + KV_LORA], gkv_ref[...], dkv_ref[...])
        dp_ref[:, 0:Q_LORA] = dcq.astype(BF16)
        dp_ref[:, Q_LORA:Q_LORA + KV_LORA] = dckv.astype(BF16)
        dp_ref[:, Q_LORA + KV_LORA:MLA_COLS] = dkpe_ref[...].astype(BF16)
        dgq_ref[...] += jnp.sum(dgq, axis=0, keepdims=True)
        dgkv_ref[...] += jnp.sum(dgkv, axis=0, keepdims=True)

    return _call(
        body, name="lora_norm_bwd", grid=(t // tm,),
        in_specs=[pl.BlockSpec((tm, MLA_COLS), lambda i: (i, 0)),
                  pl.BlockSpec((1, Q_LORA), lambda i: (0, 0)), pl.BlockSpec((1, KV_LORA), lambda i: (0, 0)),
                  pl.BlockSpec((tm, Q_LORA), lambda i: (i, 0)), pl.BlockSpec((tm, KV_LORA), lambda i: (i, 0)),
                  pl.BlockSpec((tm, HK), lambda i: (i, 0))],
        out_specs=[pl.BlockSpec((tm, MLA_COLS), lambda i: (i, 0)),
                   pl.BlockSpec((1, Q_LORA), lambda i: (0, 0)), pl.BlockSpec((1, KV_LORA), lambda i: (0, 0))],
        out_shape=[jax.ShapeDtypeStruct((t, MLA_COLS), BF16), jax.ShapeDtypeStruct((1, Q_LORA), F32),
                   jax.ShapeDtypeStruct((1, KV_LORA), F32)],
        compiler_params=_cp(),
    )(p_mla, gq, gkv, dcqn, dckvn, dkpe)


def _cumsum_rows(x, row):
    for s in (1, 2, 4, 8, 16, 32):
        x = x + jnp.where(row >= s, pltpu.roll(x, s, 0), 0.0)
    return x


def _rcumsum_rows(x, row):
    for s in (1, 2, 4, 8, 16, 32):
        x = x + jnp.where(row < CHUNK - s, pltpu.roll(x, CHUNK - s, 0), 0.0)
    return x


def _hg_gates(qr, z, lb, row):
    q = _silu(qr)
    sg = _sig(z)
    f = lb + (1.0 - lb) * sg
    lf = jnp.log(f)
    k = (1.0 - lb) * (1.0 - sg)
    cum = _cumsum_rows(lf, row)
    mid = jnp.sum(jnp.where(row < CHUNK // 2, lf, 0.0), axis=0, keepdims=True)
    last = jnp.sum(lf, axis=0, keepdims=True)
    e_q = jnp.exp(jnp.minimum(cum - mid, EXP_CLAMP))
    e_k = jnp.exp(jnp.minimum(mid - cum, EXP_CLAMP))
    e_a = jnp.exp(cum)
    e_l = jnp.exp(last - cum)
    return q, sg, f, k, last, e_q, e_k, e_a, e_l


def _hgrn_fwd(p_hg, tab, gain):
    t = p_hg.shape[0]
    bt = min(HG_BT, t)
    nb, nc = t // bt, bt // CHUNK

    hpb = HG_HPB
    wide = hpb * HK

    def body(q_ref, f_ref, i_ref, g_ref, tab_ref, gain_ref, o_ref, ho_ref, st_ref, state):
        @pl.when(pl.program_id(1) == 0)
        def _():
            state[...] = jnp.zeros_like(state)

        row = lax.broadcasted_iota(jnp.int32, (CHUNK, HK), 0)
        tril = lax.broadcasted_iota(jnp.int32, (CHUNK, CHUNK), 0) >= lax.broadcasted_iota(jnp.int32, (CHUNK, CHUNK), 1)
        gain_v = gain_ref[...]

        def chunk(c, carry):
            sl = pl.ds(pl.multiple_of(c * CHUNK, CHUNK), CHUNK)
            for hh in range(hpb):
                ln = slice(hh * HK, (hh + 1) * HK)
                lb = _sig(tab_ref[0:1, ln] - tab_ref[1:2, ln])
                v = i_ref[sl, ln].astype(BF16)
                q, _, _, k, last, e_q, e_k, e_a, e_l = _hg_gates(q_ref[sl, ln], f_ref[sl, ln], lb, row)
                st = state[hh]
                st_ref[hh, c] = st
                p = jnp.where(tril, _dot_nt((q * e_q).astype(BF16), (k * e_k).astype(BF16)), 0.0)
                o = _dot(p.astype(BF16), v) + _dot_nt((q * e_a).astype(BF16), st.astype(BF16))
                state[hh] = jnp.exp(last) * st + _dot_tn(v, (k * e_l).astype(BF16))
                o_ref[sl, ln] = o
                r = lax.rsqrt(jnp.mean(o * o, axis=-1, keepdims=True) + EPS)
                ho_ref[sl, ln] = (o * r * gain_v * _silu(g_ref[sl, ln])).astype(BF16)
            return carry

        lax.fori_loop(0, nc, chunk, 0)

    def col(k):
        return pl.BlockSpec((bt, wide), lambda h, j, k=k: (j, k * (HEADS // hpb) + h))

    return _call(
        body, name="hgrn_fwd", grid=(HEADS // hpb, nb),
        in_specs=[col(0), col(1), col(2), col(3),
                  pl.BlockSpec((2, wide), lambda h, j: (0, h)), pl.BlockSpec((1, HK), lambda h, j: (0, 0))],
        out_specs=[pl.BlockSpec((bt, wide), lambda h, j: (j, h)), pl.BlockSpec((bt, wide), lambda h, j: (j, h)),
                   pl.BlockSpec((hpb, nc, HK, HK), lambda h, j: (h, j, 0, 0))],
        out_shape=[jax.ShapeDtypeStruct((t, D), F32), jax.ShapeDtypeStruct((t, D), BF16),
                   jax.ShapeDtypeStruct((HEADS, t // CHUNK, HK, HK), F32)],
        scratch_shapes=[pltpu.VMEM((hpb, HK, HK), F32)],
        compiler_params=_cp(),
    )(p_hg, p_hg, p_hg, p_hg, tab, gain)


def _hgrn_bwd(p_hg, tab, gain, o_raw, states, dho):
    t = p_hg.shape[0]
    bt = min(HG_BT, t)
    nb, nc = t // bt, bt // CHUNK
    hpb = HG_HPB
    wide = hpb * HK

    def body(q_ref, f_ref, i_ref, g_ref, tab_ref, gain_ref, o_ref, st_ref, dho_ref,
             dq_ref, df_ref, di_ref, dg_ref, dtab_ref, dgain_ref, dstate, dlb):
        h, j = pl.program_id(0), pl.program_id(1)

        @pl.when(jnp.logical_and(h == 0, j == 0))
        def _():
            dgain_ref[...] = jnp.zeros_like(dgain_ref)

        @pl.when(j == 0)
        def _():
            dstate[...] = jnp.zeros_like(dstate)
            dlb[...] = jnp.zeros_like(dlb)

        row = lax.broadcasted_iota(jnp.int32, (CHUNK, HK), 0)
        tril = lax.broadcasted_iota(jnp.int32, (CHUNK, CHUNK), 0) >= lax.broadcasted_iota(jnp.int32, (CHUNK, CHUNK), 1)
        gain_v = gain_ref[...]

        def chunk(cc, carry):
            c = nc - 1 - cc
            sl = pl.ds(pl.multiple_of(c * CHUNK, CHUNK), CHUNK)
            dgain = jnp.zeros((1, HK), F32)
            for hh in range(hpb):
                ln = slice(hh * HK, (hh + 1) * HK)
                lb = _sig(tab_ref[0:1, ln] - tab_ref[1:2, ln])
                qr = q_ref[sl, ln]
                v = i_ref[sl, ln].astype(BF16)
                gr = g_ref[sl, ln]
                q, sg, f, k, last, e_q, e_k, e_a, e_l = _hg_gates(qr, f_ref[sl, ln], lb, row)
                o = o_ref[sl, ln]
                r = lax.rsqrt(jnp.mean(o * o, axis=-1, keepdims=True) + EPS)
                oh = o * r
                dh = dho_ref[sl, ln].astype(F32)
                dnorm = dh * _silu(gr)
                dg_ref[sl, ln] = (dh * oh * gain_v * _dsilu(gr)).astype(BF16)
                dgain = dgain + jnp.sum(dnorm * oh, axis=0, keepdims=True)
                dxh = dnorm * gain_v
                do = (r * (dxh - oh * jnp.mean(dxh * oh, axis=-1, keepdims=True))).astype(BF16)
                st0 = st_ref[hh, c]
                st0_b = st0.astype(BF16)
                ds1 = dstate[hh]
                ds1_b = ds1.astype(BF16)
                qt = (q * e_q).astype(BF16)
                kt = (k * e_k).astype(BF16)
                qd = (q * e_a).astype(BF16)
                kd = (k * e_l).astype(BF16)
                p = jnp.where(tril, _dot_nt(qt, kt), 0.0).astype(BF16)
                dp = jnp.where(tril, _dot_nt(do, v), 0.0).astype(BF16)
                dv = _dot_tn(p, do) + _dot_nt(kd, ds1_b)
                dqt = _dot(dp, kt)
                dkt = _dot_tn(dp, qt)
                dq_inter = _dot(do, st0_b) * e_a
                dk_inter = _dot(v, ds1_b) * e_l
                dq = dqt * e_q + dq_inter
                dk = dkt * e_k + dk_inter
                e_last = jnp.exp(last)
                dstate[hh] = _dot_tn(do, qd) + e_last * ds1
                dlast = (jnp.sum(k * dk_inter, axis=0, keepdims=True)
                         + e_last * jnp.sum(ds1 * st0, axis=0, keepdims=True))
                da = (qt.astype(F32) * dqt - kt.astype(F32) * dkt + q * dq_inter - k * dk_inter
                      + jnp.where(row == CHUNK - 1, dlast, 0.0))
                dlf = _rcumsum_rows(da, row)
                dfv = dlf / f - dk
                df_ref[sl, ln] = (dfv * (1.0 - lb) * sg * (1.0 - sg)).astype(BF16)
                dlb[:, ln] += jnp.sum(dfv * (1.0 - sg), axis=0, keepdims=True)
                dq_ref[sl, ln] = (dq * _dsilu(qr)).astype(BF16)
                di_ref[sl, ln] = dv.astype(BF16)
            dgain_ref[...] += dgain
            return carry

        lax.fori_loop(0, nc, chunk, 0)

        @pl.when(j == nb - 1)
        def _():
            lb = _sig(tab_ref[0:1, :] - tab_ref[1:2, :])
            d0 = dlb[...] * lb * (1.0 - lb)
            dtab_ref[0:1, :] = d0
            dtab_ref[1:2, :] = -d0

    def col(k):
        return pl.BlockSpec((bt, wide), lambda h, j, k=k: (nb - 1 - j, k * (HEADS // hpb) + h))

    tok = pl.BlockSpec((bt, wide), lambda h, j: (nb - 1 - j, h))
    return _call(
        body, name="hgrn_bwd", grid=(HEADS // hpb, nb),
        in_specs=[col(0), col(1), col(2), col(3),
                  pl.BlockSpec((2, wide), lambda h, j: (0, h)), pl.BlockSpec((1, HK), lambda h, j: (0, 0)),
                  tok, pl.BlockSpec((hpb, nc, HK, HK), lambda h, j: (h, nb - 1 - j, 0, 0)), tok],
        out_specs=[tok, tok, tok, tok,
                   pl.BlockSpec((2, wide), lambda h, j: (0, h)), pl.BlockSpec((1, HK), lambda h, j: (0, 0))],
        out_shape=[jax.ShapeDtypeStruct((t, D), BF16)] * 4
        + [jax.ShapeDtypeStruct((2, D), F32), jax.ShapeDtypeStruct((1, HK), F32)],
        scratch_shapes=[pltpu.VMEM((hpb, HK, HK), F32), pltpu.VMEM((1, wide), F32)],
        compiler_params=_cp(),
    )(p_hg, p_hg, p_hg, p_hg, tab, gain, o_raw, states, dho)


def _rope_tables(pos):
    t = pos.shape[0]
    tm = min(ROW_TM, t)
    inv = np.zeros((1, HK), np.float32)
    freq = (ROPE_THETA ** (-np.arange(0, ROPE, 2, dtype=np.float32) / ROPE)).astype(np.float32)
    inv[0, 0:ROPE // 2] = freq
    inv[0, ROPE // 2:ROPE] = freq
    sign = np.zeros((1, HK), np.float32)
    sign[0, 0:ROPE // 2] = -1.0
    sign[0, ROPE // 2:ROPE] = 1.0

    def body(pos_ref, inv_ref, sign_ref, cos_ref, sin_ref):
        ang = pos_ref[...].astype(F32) * inv_ref[...]
        cos_ref[...] = jnp.cos(ang)
        sin_ref[...] = jnp.sin(ang) * sign_ref[...]

    one = pl.BlockSpec((1, HK), lambda i: (0, 0))
    row = pl.BlockSpec((tm, HK), lambda i: (i, 0))
    return _call(
        body, name="rope_tables", grid=(t // tm,),
        in_specs=[pl.BlockSpec((tm, 1), lambda i: (i, 0)), one, one],
        out_specs=[row, row],
        out_shape=[jax.ShapeDtypeStruct((t, HK), F32)] * 2,
        compiler_params=_cp(),
    )(pos, jnp.asarray(inv), jnp.asarray(sign))


def _rope(x, cos, sin_signed):
    r = lax.broadcasted_iota(jnp.int32, (HK, HK), 0)
    c = lax.broadcasted_iota(jnp.int32, (HK, HK), 1)
    half = ROPE // 2
    swap = jnp.logical_or(jnp.logical_and(c < half, r == c + half),
                          jnp.logical_and(jnp.logical_and(c >= half, c < ROPE), r == c - half))
    return x * cos + _dot_split(x, swap.astype(BF16)) * sin_signed


def _dot_split(x, m):
    hi = x.astype(BF16)
    lo = (x - hi.astype(F32)).astype(BF16)
    return _dot(hi, m) + _dot(lo, m)


def _lane_sum(x):
    return _dot_split(x, jnp.ones((HK, HK), BF16))


def _head_norm(xn, xr):
    r = lax.rsqrt(_lane_sum(xn * xn + xr * xr) * (1.0 / QK) + EPS)
    return xn * r, xr * r, r


def _head_norm_bwd(xn, xr, g_n, g_r, dn, dr):
    hn, hr, r = _head_norm(xn, xr)
    dxn, dxr = dn * g_n, dr * g_r
    c = _lane_sum(dxn * hn + dxr * hr) * (1.0 / QK)
    return r * (dxn - hn * c), r * (dxr - hr * c), dn * hn, dr * hr


def _mla_prep_fwd(qf, kv, p_mla, cos, sin, gq, gk):
    t = qf.shape[0]
    tm = min(ROW_TM, t)

    def body(qf_ref, kv_ref, kpe_ref, cos_ref, sin_ref, gq_ref, gk_ref, q_ref, k_ref, v_ref):
        cos_v, sin_v = cos_ref[...], sin_ref[...]
        kpe = kpe_ref[...]
        for h in range(HEADS):
            lo, mid, hi = h * QKP, h * QKP + HK, (h + 1) * QKP
            qn, qr, _ = _head_norm(qf_ref[:, lo:mid], qf_ref[:, mid:hi])
            q_ref[h, :, 0:HK] = (qn * gq_ref[:, 0:HK] * (SCALE * LOG2E)).astype(BF16)
            q_ref[h, :, HK:QKP] = (_rope(qr * gq_ref[:, HK:QKP], cos_v, sin_v) * (SCALE * LOG2E)).astype(BF16)
            kn, kr, _ = _head_norm(kv_ref[:, lo:mid], kpe)
            k_ref[h, :, 0:HK] = (kn * gk_ref[:, 0:HK]).astype(BF16)
            k_ref[h, :, HK:QKP] = _rope(kr * gk_ref[:, HK:QKP], cos_v, sin_v).astype(BF16)
            v_ref[h, :, 0:HK] = kv_ref[:, mid:hi].astype(BF16)
            v_ref[h, :, HK:QKP] = jnp.full((tm, HK), -1.0, BF16)

    head = pl.BlockSpec((tm, HEADS * QKP), lambda i: (i, 0))
    tok = pl.BlockSpec((tm, HK), lambda i: (i, 0))
    gain = pl.BlockSpec((1, QKP), lambda i: (0, 0))
    return _call(
        body, name="mla_prep_fwd", grid=(t // tm,),
        in_specs=[head, head, pl.BlockSpec((tm, HK), lambda i: (i, MLA_COLS // HK - 1)), tok, tok, gain, gain],
        out_specs=[pl.BlockSpec((HEADS, tm, QKP), lambda i: (0, i, 0)),
                   pl.BlockSpec((HEADS, tm, QKP), lambda i: (0, i, 0)),
                   pl.BlockSpec((HEADS, tm, QKP), lambda i: (0, i, 0))],
        out_shape=[jax.ShapeDtypeStruct((HEADS, t, QKP), BF16), jax.ShapeDtypeStruct((HEADS, t, QKP), BF16),
                   jax.ShapeDtypeStruct((HEADS, t, QKP), BF16)],
        compiler_params=_cp(),
    )(qf, kv, p_mla, cos, sin, gq, gk)


def _mla_prep_bwd(qf, kv, p_mla, cos, sin, gq, gk, dq, dk, dv):
    t = qf.shape[0]
    tm = min(ROW_TM, t)

    def body(qf_ref, kv_ref, kpe_ref, cos_ref, sin_ref, gq_ref, gk_ref, dq_ref, dk_ref, dv_ref,
             dqf_ref, dkv_ref, dkpe_ref, dgq_ref, dgk_ref):
        @pl.when(pl.program_id(0) == 0)
        def _():
            dgq_ref[...] = jnp.zeros_like(dgq_ref)
            dgk_ref[...] = jnp.zeros_like(dgk_ref)

        cos_v, sin_v = cos_ref[...], -sin_ref[...]
        kpe = kpe_ref[...]
        gqn, gqr, gkn, gkr = gq_ref[:, 0:HK], gq_ref[:, HK:QKP], gk_ref[:, 0:HK], gk_ref[:, HK:QKP]
        dkpe = jnp.zeros((tm, HK), F32)
        dgq_n, dgq_r, dgk_n, dgk_r = [jnp.zeros((1, HK), F32) for _ in range(4)]
        for h in range(HEADS):
            lo, mid, hi = h * QKP, h * QKP + HK, (h + 1) * QKP
            dqn = dq_ref[h, :, 0:HK].astype(F32) * SCALE
            dqr = _rope(dq_ref[h, :, HK:QKP].astype(F32), cos_v, sin_v) * SCALE
            a, b, ga, gb = _head_norm_bwd(qf_ref[:, lo:mid], qf_ref[:, mid:hi], gqn, gqr, dqn, dqr)
            dqf_ref[:, lo:mid] = a.astype(BF16)
            dqf_ref[:, mid:hi] = b.astype(BF16)
            dgq_n = dgq_n + jnp.sum(ga, axis=0, keepdims=True)
            dgq_r = dgq_r + jnp.sum(gb, axis=0, keepdims=True)
            dkn = dk_ref[h, :, 0:HK].astype(F32) * LN2
            dkr = _rope(dk_ref[h, :, HK:QKP].astype(F32), cos_v, sin_v) * LN2
            a, b, ga, gb = _head_norm_bwd(kv_ref[:, lo:mid], kpe, gkn, gkr, dkn, dkr)
            dkv_ref[:, lo:mid] = a.astype(BF16)
            dkv_ref[:, mid:hi] = dv_ref[h].astype(BF16)
            dkpe = dkpe + b
            dgk_n = dgk_n + jnp.sum(ga, axis=0, keepdims=True)
            dgk_r = dgk_r + jnp.sum(gb, axis=0, keepdims=True)
        dkpe_ref[...] = dkpe
        dgq_ref[:, 0:HK] += dgq_n
        dgq_ref[:, HK:QKP] += dgq_r
        dgk_ref[:, 0:HK] += dgk_n
        dgk_ref[:, HK:QKP] += dgk_r

    head = pl.BlockSpec((tm, HEADS * QKP), lambda i: (i, 0))
    tok = pl.BlockSpec((tm, HK), lambda i: (i, 0))
    gain = pl.BlockSpec((1, QKP), lambda i: (0, 0))
    hq = pl.BlockSpec((HEADS, tm, QKP), lambda i: (0, i, 0))
    return _call(
        body, name="mla_prep_bwd", grid=(t // tm,),
        in_specs=[head, head, pl.BlockSpec((tm, HK), lambda i: (i, MLA_COLS // HK - 1)), tok, tok, gain, gain,
                  hq, hq, pl.BlockSpec((HEADS, tm, HK), lambda i: (0, i, 0))],
        out_specs=[head, head, tok, gain, gain],
        out_shape=[jax.ShapeDtypeStruct((t, HEADS * QKP), BF16), jax.ShapeDtypeStruct((t, HEADS * QKP), BF16),
                   jax.ShapeDtypeStruct((t, HK), F32), jax.ShapeDtypeStruct((1, QKP), F32),
                   jax.ShapeDtypeStruct((1, QKP), F32)],
        compiler_params=_cp(),
    )(qf, kv, p_mla, cos, sin, gq, gk, dq, dk, dv)


def _chunk_mask(row0, rows, cols):
    r = lax.broadcasted_iota(jnp.int32, (rows, cols), 0) + row0
    c = lax.broadcasted_iota(jnp.int32, (rows, cols), 1)
    return jnp.right_shift(r, 6) >= jnp.right_shift(c, 6)


def _flash_fwd(q, k, v, side=None):
    t = q.shape[1]
    tq = min(TQ, t)
    nq = t // tq
    sub = min(SUBQ, tq)
    pairs = [(i, j) for i in range(nq) for j in range(i + 1)]
    qi = jnp.asarray([p[0] for p in pairs], jnp.int32)
    kj = jnp.asarray([p[1] for p in pairs], jnp.int32)
    s_in = len(side.inputs) if side else 0
    s_out = len(side.out_shapes) if side else 0

    def body(qi_ref, kj_ref, q_ref, k_ref, v_ref, *rest):
        o_ref, lse_ref = rest[s_in:s_in + 2]
        m_s, acc_s = rest[s_in + 2 + s_out:s_in + 4 + s_out]
        side_refs = list(rest[:s_in]) + list(rest[s_in + 2:s_in + 2 + s_out]) + list(rest[s_in + 4 + s_out:])
        n = pl.program_id(1)
        i, j = qi_ref[n], kj_ref[n]
        if side:
            @pl.when(jnp.logical_and(pl.program_id(0) == 0, n == 0))
            def _():
                side.start(*side_refs)

        @pl.when(j == 0)
        def _():
            m_s[...] = jnp.full_like(m_s, NEG)
            acc_s[...] = jnp.zeros_like(acc_s)

        def step(diag):
            subs = range(tq // sub)
            width = [(r + 1) * sub if diag else tq for r in subs]
            logits = [_dot_nt(q_ref[r * sub:(r + 1) * sub, :], k_ref[0:width[r], :]) for r in subs]
            for r in subs:
                rows = slice(r * sub, (r + 1) * sub)
                cols = width[r]
                s = logits[r]
                if diag:
                    s = jnp.where(_chunk_mask(r * sub, sub, cols), s, NEG)
                m_old = m_s[rows, :]
                m_new = jnp.maximum(m_old, jnp.max(s, axis=-1, keepdims=True))
                alpha = jnp.exp2(m_old - m_new)
                p = jnp.exp2((s - jnp.tile(m_new, (1, cols // HK))).astype(BF16))
                acc_s[rows, :] = jnp.tile(alpha, (1, 2)) * acc_s[rows, :] + _dot(p, v_ref[0:cols, :])
                m_s[rows, :] = m_new

        @pl.when(j < i)
        def _():
            step(False)

        @pl.when(j == i)
        def _():
            step(True)
            l = -acc_s[:, HK:QKP]
            o_ref[...] = (acc_s[:, 0:HK] / l).astype(BF16)
            lse_ref[...] = m_s[...] + jnp.log(l) * LOG2E

        if side:
            @pl.when(jnp.logical_and(pl.program_id(0) == HEADS - 1, n == len(pairs) - 1))
            def _():
                side.finish(*side_refs)

    anywhere = pl.BlockSpec(memory_space=pl.ANY)
    grid_spec = pltpu.PrefetchScalarGridSpec(
        num_scalar_prefetch=2, grid=(HEADS, len(pairs)),
        in_specs=[pl.BlockSpec((None, tq, QKP), lambda h, n, qi, kj: (h, qi[n], 0)),
                  pl.BlockSpec((None, tq, QKP), lambda h, n, qi, kj: (h, kj[n], 0)),
                  pl.BlockSpec((None, tq, QKP), lambda h, n, qi, kj: (h, kj[n], 0))] + [anywhere] * s_in,
        out_specs=[pl.BlockSpec((tq, HK), lambda h, n, qi, kj: (qi[n], h)),
                   pl.BlockSpec((None, tq, HK), lambda h, n, qi, kj: (h, qi[n], 0))] + [anywhere] * s_out,
        scratch_shapes=[pltpu.VMEM((tq, HK), F32), pltpu.VMEM((tq, QKP), F32)] + (list(side.scratch) if side else []),
    )
    return _call(
        body, name="flash_fwd", grid_spec=grid_spec,
        out_shape=[jax.ShapeDtypeStruct((t, D), BF16), jax.ShapeDtypeStruct((HEADS, t, HK), F32)]
        + (list(side.out_shapes) if side else []),
        compiler_params=_cp(),
    )(qi, kj, q, k, v, *(side.inputs if side else []))


def _attn_do(do, o):
    t = do.shape[0]
    tm = min(TM, t)

    def body(do_ref, o_ref, d_ref):
        lane = lax.broadcasted_iota(jnp.int32, (tm, HK), 1)
        for h in range(HEADS):
            ln = slice(h * HK, (h + 1) * HK)
            dov = do_ref[:, ln]
            d = jnp.sum(dov.astype(F32) * o_ref[:, ln].astype(F32), axis=-1, keepdims=True)
            hi = d.astype(BF16).astype(F32)
            d_ref[h, :, 0:HK] = dov
            d_ref[h, :, HK:QKP] = jnp.where(lane == 0, hi, jnp.where(lane == 1, d - hi, 0.0)).astype(BF16)

    blk = pl.BlockSpec((tm, D), lambda i: (i, 0))
    return _call(
        body, name="attn_do", grid=(t // tm,),
        in_specs=[blk, blk],
        out_specs=pl.BlockSpec((HEADS, tm, QKP), lambda i: (0, i, 0)),
        out_shape=jax.ShapeDtypeStruct((HEADS, t, QKP), BF16),
        compiler_params=_cp(),
    )(do, o)


def _flash_bwd(q, k, v, lse, do):
    t = q.shape[1]
    tq = min(TQ, t)
    nq = t // tq
    sub = min(SUBQ, tq)
    pairs = [(i, j) for j in range(nq) for i in range(j, nq)]
    qi = jnp.asarray([p[0] for p in pairs], jnp.int32)
    kj = jnp.asarray([p[1] for p in pairs], jnp.int32)
    npairs = len(pairs)

    def body(qi_ref, kj_ref, q_ref, k_ref, v_ref, lse_ref, do_ref, dq_ref, dk_ref, dv_ref):
        n = pl.program_id(1)
        i, j = qi_ref[n], kj_ref[n]

        @pl.when(n == 0)
        def _():
            dq_ref[...] = jnp.zeros_like(dq_ref)

        @pl.when(i == j)
        def _():
            dk_ref[...] = jnp.zeros_like(dk_ref)
            dv_ref[...] = jnp.zeros_like(dv_ref)

        def step(diag):
            for r in range(tq // sub):
                rows = slice(r * sub, (r + 1) * sub)
                cols = (r + 1) * sub if diag else tq
                qv, kv_ = q_ref[rows, :], k_ref[0:cols, :]
                p = jnp.exp2(_dot_nt(qv, kv_) - jnp.tile(lse_ref[rows, :], (1, cols // HK)))
                if diag:
                    p = jnp.where(_chunk_mask(r * sub, sub, cols), p, 0.0)
                dp_less_delta = _dot_nt(do_ref[rows, :], v_ref[0:cols, :])
                ds = (p * dp_less_delta).astype(BF16)
                dv_ref[0:cols, :] += _dot_tn(p.astype(BF16), do_ref[rows, 0:HK])
                dk_ref[0:cols, :] += _dot_tn(ds, qv)
                dq_rows = pl.ds(pl.multiple_of(i * tq + r * sub, sub), sub)
                dq_ref[dq_rows, :] += _dot(ds, kv_)

        @pl.when(j < i)
        def _():
            step(False)

        @pl.when(j == i)
        def _():
            step(True)

    grid_spec = pltpu.PrefetchScalarGridSpec(
        num_scalar_prefetch=2, grid=(HEADS, npairs),
        in_specs=[pl.BlockSpec((None, tq, QKP), lambda h, n, qi, kj: (h, qi[n], 0)),
                  pl.BlockSpec((None, tq, QKP), lambda h, n, qi, kj: (h, kj[n], 0)),
                  pl.BlockSpec((None, tq, QKP), lambda h, n, qi, kj: (h, kj[n], 0)),
                  pl.BlockSpec((None, tq, HK), lambda h, n, qi, kj: (h, qi[n], 0)),
                  pl.BlockSpec((None, tq, QKP), lambda h, n, qi, kj: (h, qi[n], 0))],
        out_specs=[pl.BlockSpec((None, t, QKP), lambda h, n, qi, kj: (h, 0, 0)),
                   pl.BlockSpec((None, tq, QKP), lambda h, n, qi, kj: (h, kj[n], 0)),
                   pl.BlockSpec((None, tq, HK), lambda h, n, qi, kj: (h, kj[n], 0))],
    )
    return _call(
        body, name="flash_bwd", grid_spec=grid_spec,
        out_shape=[jax.ShapeDtypeStruct((HEADS, t, QKP), F32), jax.ShapeDtypeStruct((HEADS, t, QKP), F32),
                   jax.ShapeDtypeStruct((HEADS, t, HK), F32)],
        compiler_params=_cp(56),
    )(qi, kj, q, k, v, lse, do)


def _adamw(name, w, g, m, v):
    r, c = w.shape
    tr = r if r <= 256 else next(k for k in (256, 352, 384) if r % k == 0)

    def body(w_ref, g_ref, m_ref, v_ref, d_ref, nm_ref, nv_ref):
        gv = g_ref[...]
        nm = ADAM_B1 * m_ref[...] + (1.0 - ADAM_B1) * gv
        nv = ADAM_B2 * v_ref[...] + (1.0 - ADAM_B2) * (gv * gv)
        m_hat = nm / (1.0 - ADAM_B1 ** ADAM_STEP)
        v_hat = nv / (1.0 - ADAM_B2 ** ADAM_STEP)
        d_ref[...] = -ADAM_LR * (m_hat / (jnp.sqrt(v_hat) + ADAM_EPS) + ADAM_WD * w_ref[...])
        nm_ref[...] = nm
        nv_ref[...] = nv

    blk = pl.BlockSpec((tr, c), lambda i: (i, 0))
    return _call(
        body, name=name, grid=(r // tr,),
        in_specs=[blk] * 4, out_specs=[blk] * 3,
        out_shape=[jax.ShapeDtypeStruct((r, c), F32)] * 3,
        compiler_params=_cp(),
    )(w, g, m, v)


def _place():
    return lax.axis_index("x"), lax.axis_index("y"), lax.axis_index("c")


def _other_chips(x, y):
    return [(1 - x, y), (x, 1 - y), (1 - x, 1 - y)]


class _Exchange:
    inputs = ()
    out_shapes = ()
    scratch = ()

    def start(self, *refs):
        raise NotImplementedError

    def finish(self, *refs):
        raise NotImplementedError

    def alone(self, name):
        def body(*refs):
            self.start(*refs)
            self.finish(*refs)

        anywhere = pl.BlockSpec(memory_space=pl.ANY)
        return _call(
            body, name=name,
            in_specs=[anywhere] * len(self.inputs), out_specs=[anywhere] * len(self.out_shapes),
            out_shape=list(self.out_shapes), scratch_shapes=list(self.scratch),
        )(*self.inputs)


class _GatherWeights(_Exchange):
    def __init__(self, shards):
        self.inputs = tuple(shards)
        self.out_shapes = tuple(jax.ShapeDtypeStruct((4,) + s.shape, s.dtype) for s in shards)
        self.scratch = (pltpu.SemaphoreType.DMA((6 * len(shards),)), pltpu.SemaphoreType.DMA((6 * len(shards),)))

    def gathered(self, got, k):
        return [lax.dynamic_update_slice(g, s[None], (k, 0, 0)) for g, s in zip(got, self.inputs)]

    def _copies(self, *refs):
        nbuf = len(self.inputs)
        send_sems, recv_sems = refs[2 * nbuf:]
        x, y, c = _place()
        chips = _other_chips(x, y)
        first, passed, landed, relayed = [], [], [], []
        for b, (s_ref, g_ref) in enumerate(zip(refs[:nbuf], refs[nbuf:2 * nbuf])):
            half = self.inputs[b].shape[0] // 2

            def rows(px, py, pc, g_ref=g_ref, half=half):
                return g_ref.at[2 * px + py, pl.ds(pc * half, half), :]

            def copy(k, block, to, src=None, rows=rows, b=b):
                return pltpu.make_async_remote_copy(
                    src_ref=rows(*block) if src is None else src, dst_ref=rows(*block),
                    send_sem=send_sems.at[6 * b + k], recv_sem=recv_sems.at[6 * b + k], device_id=to, device_id_type=MESH)

            mine = s_ref.at[pl.ds(c * half, half), :]
            first += [copy(j, (x, y, c), (*chip, c), src=mine) for j, chip in enumerate(chips)]
            passed += [copy(3 + j, (*chip, c), (x, y, 1 - c)) for j, chip in enumerate(chips)]
            landed += [copy(j, (*chip, c), (x, y, c)) for j, chip in enumerate(chips)]
            relayed += [copy(3 + j, (*chip, 1 - c), (x, y, c)) for j, chip in enumerate(chips)]
        return first, passed, landed, relayed

    def start(self, *refs):
        for cp in self._copies(*refs)[0]:
            cp.start()

    def finish(self, *refs):
        first, passed, landed, relayed = self._copies(*refs)
        for arrived, onward in zip(landed, passed):
            arrived.wait_recv()
            onward.start()
        for cp in relayed:
            cp.wait_recv()
        for cp in first + passed:
            cp.wait_send()


def _swap_halves(name, bufs):
    nbuf = len(bufs)

    def body(*refs):
        send_sems, recv_sems = refs[2 * nbuf:]
        x, y, c = _place()
        cps = []
        for b, (g_ref, o_ref) in enumerate(zip(refs[:nbuf], refs[nbuf:2 * nbuf])):
            half = bufs[b].shape[1] // 2
            cps.append(pltpu.make_async_remote_copy(
                src_ref=g_ref.at[:, pl.ds((1 - c) * half, half), :], dst_ref=o_ref,
                send_sem=send_sems.at[b], recv_sem=recv_sems.at[b], device_id=(x, y, 1 - c), device_id_type=MESH))
        for cp in cps:
            cp.start()
        for cp in cps:
            cp.wait()

    anywhere = pl.BlockSpec(memory_space=pl.ANY)
    return _call(
        body, name=name,
        in_specs=[anywhere] * nbuf, out_specs=[anywhere] * nbuf,
        out_shape=[jax.ShapeDtypeStruct((4, g.shape[1] // 2, g.shape[2]), g.dtype) for g in bufs],
        scratch_shapes=[pltpu.SemaphoreType.DMA((nbuf,)), pltpu.SemaphoreType.DMA((nbuf,))],
    )(*bufs)


def _add_rows(half):
    return next(tr for tr in range(512, 15, -16) if half % tr == 0)


def _chip_sum(name, gp, got, c_arr):
    half, width = got.shape[1], got.shape[2]
    tr = _add_rows(half)
    nb = half // tr

    def body(c_ref, a_ref, b_ref, o_ref, ob_ref):
        s = a_ref[...] + b_ref[...]
        o_ref[...] = s
        ob_ref[...] = s.astype(BF16)

    grid_spec = pltpu.PrefetchScalarGridSpec(
        num_scalar_prefetch=1, grid=(4, nb),
        in_specs=[pl.BlockSpec((None, tr, width), lambda s, i, c: (s, c[0] * nb + i, 0)),
                  pl.BlockSpec((None, tr, width), lambda s, i, c: (s, i, 0))],
        out_specs=[pl.BlockSpec((None, tr, width), lambda s, i, c: (s, i, 0)),
                   pl.BlockSpec((None, tr, width), lambda s, i, c: (s, i, 0))],
    )
    return _call(
        body, name=name, grid_spec=grid_spec,
        out_shape=[jax.ShapeDtypeStruct(got.shape, F32), jax.ShapeDtypeStruct(got.shape, BF16)],
        compiler_params=_cp(),
    )(c_arr, gp, got)


class _ScatterChipSums(_Exchange):
    def __init__(self, sums):
        self.inputs = tuple(sums)
        self.out_shapes = tuple(jax.ShapeDtypeStruct((3,) + cs.shape[1:], cs.dtype) for cs in sums)
        self.scratch = (pltpu.SemaphoreType.DMA((3 * len(sums),)), pltpu.SemaphoreType.DMA((3 * len(sums),)))

    def _copies(self, *refs):
        nbuf = len(self.inputs)
        send_sems, recv_sems = refs[2 * nbuf:]
        x, y, c = _place()
        return [pltpu.make_async_remote_copy(
            src_ref=s_ref.at[2 * px + py], dst_ref=o_ref.at[j],
            send_sem=send_sems.at[3 * b + j], recv_sem=recv_sems.at[3 * b + j], device_id=(px, py, c), device_id_type=MESH)
            for b, (s_ref, o_ref) in enumerate(zip(refs[:nbuf], refs[nbuf:2 * nbuf]))
            for j, (px, py) in enumerate(_other_chips(x, y))]

    def start(self, *refs):
        for cp in self._copies(*refs):
            cp.start()

    def finish(self, *refs):
        for cp in self._copies(*refs):
            cp.wait()


def _shard_sum(name, cs, got, kc_arr):
    h, width = cs.shape[1], cs.shape[2]
    tr = _add_rows(h)
    nb = h // tr

    def body(k_ref, a_ref, b_ref, o_ref):
        o_ref[...] = ((a_ref[...] + b_ref[0].astype(F32)) + b_ref[1].astype(F32)) + b_ref[2].astype(F32)

    grid_spec = pltpu.PrefetchScalarGridSpec(
        num_scalar_prefetch=1, grid=(nb,),
        in_specs=[pl.BlockSpec((None, tr, width), lambda i, k: (k[0], i, 0)),
                  pl.BlockSpec((3, tr, width), lambda i, k: (0, i, 0))],
        out_specs=pl.BlockSpec((tr, width), lambda i, k: (k[1] * nb + i, 0)),
    )
    return _call(
        body, name=name, grid_spec=grid_spec,
        out_shape=jax.ShapeDtypeStruct((2 * h, width), F32),
        compiler_params=_cp(),
    )(kc_arr, cs, got)


def _join_halves(name, boths):
    nbuf = len(boths)

    def body(*refs):
        send_sems, recv_sems = refs[2 * nbuf:]
        x, y, c = _place()
        sent, landing = [], []
        for b, (m_ref, o_ref) in enumerate(zip(refs[:nbuf], refs[nbuf:2 * nbuf])):
            h = boths[b].shape[0] // 2
            mine = m_ref.at[pl.ds(c * h, h), :]
            sent.append(pltpu.make_async_remote_copy(
                src_ref=mine, dst_ref=o_ref.at[pl.ds(c * h, h), :],
                send_sem=send_sems.at[b], recv_sem=recv_sems.at[b], device_id=(x, y, 1 - c), device_id_type=MESH))
            landing.append(pltpu.make_async_remote_copy(
                src_ref=mine, dst_ref=o_ref.at[pl.ds((1 - c) * h, h), :],
                send_sem=send_sems.at[b], recv_sem=recv_sems.at[b], device_id=(x, y, 1 - c), device_id_type=MESH))
        for cp in sent:
            cp.start()
        for cp in sent:
            cp.wait_send()
        for cp in landing:
            cp.wait_recv()

    anywhere = pl.BlockSpec(memory_space=pl.ANY)
    return _call(
        body, name=name,
        in_specs=[anywhere] * nbuf, out_specs=[anywhere] * nbuf,
        out_shape=[jax.ShapeDtypeStruct(g.shape, g.dtype) for g in boths],
        input_output_aliases={b: b for b in range(nbuf)},
        scratch_shapes=[pltpu.SemaphoreType.DMA((nbuf,)), pltpu.SemaphoreType.DMA((nbuf,))],
    )(*boths)


def _all_reduce_small(v):
    r = v.shape[0]

    def body(v_ref, o_ref, buf, send_sems, recv_sems):
        x, y, c = _place()
        me = 4 * x + 2 * y + c
        buf[me] = v_ref[...]
        cps = []
        for k in range(1, 8):
            peer = (x ^ (k >> 2), y ^ ((k >> 1) & 1), c ^ (k & 1))
            cps.append(pltpu.make_async_remote_copy(
                src_ref=v_ref, dst_ref=buf.at[me],
                send_sem=send_sems.at[k - 1], recv_sem=recv_sems.at[k - 1], device_id=peer, device_id_type=MESH))
        for cp in cps:
            cp.start()
        for k in range(1, 8):
            pltpu.make_async_remote_copy(
                src_ref=v_ref, dst_ref=buf.at[me ^ k],
                send_sem=send_sems.at[k - 1], recv_sem=recv_sems.at[k - 1],
                device_id=(x, y, c), device_id_type=MESH).wait_recv()
        for cp in cps:
            cp.wait_send()
        acc = buf[0]
        for k in range(1, 8):
            acc = acc + buf[k]
        o_ref[...] = acc

    return _call(
        body, name="all_reduce_small",
        in_specs=[pl.BlockSpec(memory_space=pltpu.VMEM)],
        out_specs=pl.BlockSpec(memory_space=pltpu.VMEM),
        out_shape=jax.ShapeDtypeStruct((r, 128), F32),
        scratch_shapes=[pltpu.VMEM((8, r, 128), F32), pltpu.SemaphoreType.DMA((7,)), pltpu.SemaphoreType.DMA((7,))],
    )(v)


def _group(names):
    return tuple(e for e in BIG if e[0] in names)


def _pack(shards, dtype):
    return jnp.concatenate([s.astype(dtype).reshape(-1, PACK_W) for s in shards], axis=0)


def _unpack_full(g, group):
    out, at = {}, 0
    for name, rows, cols, axis in group:
        n = rows * cols // 4 // PACK_W
        blk = g[:, at:at + n, :]
        at += n
        if axis == 1:
            out[name] = blk.reshape(4, rows, cols // 4).transpose(1, 0, 2).reshape(rows, cols)
        else:
            out[name] = blk.reshape(rows, cols)
    return out


def _pack_grads(grads, group):
    parts = []
    for name, rows, cols, axis in group:
        g = grads[name]
        if axis == 1:
            g = g.reshape(rows, 4, cols // 4).transpose(1, 0, 2)
        parts.append(g.reshape(4, -1, PACK_W))
    rows_total = sum(p.shape[1] for p in parts)
    pad = -rows_total % PACK_ALIGN
    if pad:
        parts.append(jnp.zeros((4, pad, PACK_W), F32))
    return jnp.concatenate(parts, axis=1)


def _unpack_shard(s, group):
    out, at = {}, 0
    for name, rows, cols, axis in group:
        n = rows * cols // 4 // PACK_W
        shape = (rows, cols // 4) if axis == 1 else (rows // 4, cols)
        out[name] = s[at:at + n, :].reshape(shape)
        at += n
    return out


def _pack_small(parts):
    flat = jnp.concatenate([p.reshape(-1) for p in parts])
    pad = -flat.shape[0] % 1024
    return jnp.concatenate([flat, jnp.zeros((pad,), F32)]).reshape(-1, 128)


def _ffn_in(tag, h, gain, w_in, side=None):
    t = h.shape[0]
    wide = DFF // 2

    def compute_in(rows, weights, outs):
        hv, w_ref = rows[0][...], weights[0]
        r = lax.rsqrt(jnp.mean(hv * hv, axis=-1, keepdims=True) + EPS)
        a = (hv * r * weights[1][...]).astype(BF16)
        outs[0][...] = a

        def emit(gate, up, cols):
            outs[1][:, cols] = gate.astype(BF16)
            outs[2][:, cols] = up.astype(BF16)
            outs[3][:, cols] = (_silu(gate) * up).astype(BF16)

        for s in range(2):
            emit(_dot(a, w_ref[s, :, 0:FFN_MAIN]), _dot(a, w_ref[2 + s, :, 0:FFN_MAIN]),
                 slice(s * wide, s * wide + FFN_MAIN))
        gate = _dot(a, jnp.concatenate([w_ref[0, :, FFN_MAIN:wide], w_ref[1, :, FFN_MAIN:wide]], axis=1))
        up = _dot(a, jnp.concatenate([w_ref[2, :, FFN_MAIN:wide], w_ref[3, :, FFN_MAIN:wide]], axis=1))
        rest = wide - FFN_MAIN
        for s in range(2):
            emit(gate[:, s * rest:(s + 1) * rest], up[:, s * rest:(s + 1) * rest],
                 slice(s * wide + FFN_MAIN, (s + 1) * wide))

    return _rows_call(tag + "_in", [h], [w_in, gain], [(D, BF16)] + [(DFF, BF16)] * 3, compute_in, min(FFN_TM, t),
                      side=side)


def _ffn_out(tag, act, h, w_out, next_gain, target=None):
    t = h.shape[0]
    tm = min(FFN_TM, t)

    def compute_out(rows, weights, outs):
        hn = rows[1][...] + 0.5 * _dot(rows[0][...], weights[0][...])
        g = weights[1][...]
        r = lax.rsqrt(jnp.mean(hn * hn, axis=-1, keepdims=True) + EPS)
        xh = hn * r
        if target is None:
            outs[0][...] = hn
            outs[1][...] = (xh * g).astype(BF16)
        else:
            err = xh * g - rows[2][...]
            dy = err * (1.0 / D)
            dxh = dy * g
            outs[0][...] = r * (dxh - xh * jnp.mean(dxh * xh, axis=-1, keepdims=True))
            outs[1][...] += jnp.sum(dy * xh, axis=0, keepdims=True)
            outs[2][...] += 0.5 * jnp.sum(jnp.mean(err * err, axis=-1, keepdims=True), axis=0, keepdims=True)

    if target is None:
        return _rows_call(tag + "_out", [act, h], [w_out, next_gain], [(D, F32), (D, BF16)], compute_out, tm)
    return _rows_call(tag + "_out", [act, h, target], [w_out, next_gain], [(D, F32)], compute_out, tm, sums=(D, 128))


class _Reduction:
    def __init__(self, tag, c_arr, k_arr):
        self.tag, self.c_arr, self.k_arr = tag, c_arr, k_arr

    def begin(self, bufs):
        swapped = _swap_halves("grad_swap_" + self.tag, bufs)
        sums = [_chip_sum("grad_chip_sum_%s%d" % (self.tag, b), gp, got, self.c_arr)
                for b, (gp, got) in enumerate(zip(bufs, swapped))]
        self.sums = [s[0] for s in sums]
        return _ScatterChipSums([s[1] for s in sums])

    def end(self, got):
        mine = [_shard_sum("grad_shard_sum_%s%d" % (self.tag, b), cs, g, self.k_arr)
                for b, (cs, g) in enumerate(zip(self.sums, got))]
        return _join_halves("grad_join_" + self.tag, mine)


def _ffn_bwd(tag, h, gain, w_in, w_out, saved, dout, side, reduction):
    t = h.shape[0]
    tm = min(TM, t)
    n, gate, up, act = saved

    def compute(rows, weights, outs):
        d = rows[0][...].astype(BF16)
        for j in range(DFF // FFN_CHUNK):
            cols = slice(j * FFN_CHUNK, (j + 1) * FFN_CHUNK)
            da = 0.5 * _dot_nt(d, weights[0][cols, :])
            g, u = rows[1][:, cols].astype(F32), rows[2][:, cols].astype(F32)
            s = _sig(g)
            silu = g * s
            outs[0][:, cols] = (da * u * (s + silu * (1.0 - s))).astype(BF16)
            outs[1][:, cols] = (da * silu).astype(BF16)

    dgate, dup, *side_out = _rows_call(tag + "_dact", [dout, gate, up], [w_out], [(DFF, BF16)] * 2, compute,
                                       min(FFN_TM, t), side=side)
    dw_out = _mm_tn(tag + "_dw_out", act, dout, scale=0.5, tm=DFF // 2, tn=D)
    dw_in = _mm_tn(tag + "_dw_gate", n, dgate, tm=D, tn=DFF // 2, stacked=(4, 0))
    dw_in = _mm_tn(tag + "_dw_up", n, dup, tm=D, tn=DFF // 2, stacked=(4, 2), into=dw_in)
    sending = reduction.begin([dw_in, dw_out.reshape(4, DFF // 4, D)])

    def compute_dn(rows, weights, outs):
        w_ref = weights[0]
        wide = DFF // 2
        dn = jnp.zeros((rows[0].shape[0], D), F32)
        for s in range(2):
            cols = slice(s * wide, s * wide + FFN_MAIN)
            dn = (dn + _dot_nt(rows[0][:, cols], w_ref[s, :, 0:FFN_MAIN])
                  + _dot_nt(rows[1][:, cols], w_ref[2 + s, :, 0:FFN_MAIN]))
        for r, first in ((0, 0), (1, 2)):
            x = jnp.concatenate([rows[r][:, FFN_MAIN:wide], rows[r][:, wide + FFN_MAIN:2 * wide]], axis=1)
            wt = jnp.concatenate([w_ref[first, :, FFN_MAIN:wide], w_ref[first + 1, :, FFN_MAIN:wide]], axis=1)
            dn = dn + _dot_nt(x, wt)
        dx, dg = _rms_bwd_vals(rows[2][...], weights[1][...], dn)
        outs[0][...] = rows[3][...] + dx
        outs[1][...] += jnp.sum(dg, axis=0, keepdims=True)

    dh, dgain, *got = _rows_call(tag + "_dn", [dgate, dup, h, dout], [w_in, gain], [(D, F32)], compute_dn,
                                 min(FFN_TM, t), side=sending, sums=(D,), vmem_mb=58)
    return dh, dgain, side_out, got


def kernel(x, positions, ffn1_norm, ffn1_w_in, ffn1_w_out, mix_norm, w_in, hg_lb_table, hg_out_norm, w_hg_branch, mla_q_lora_norm, w_q_up, mla_kv_lora_norm, w_kv_up, q_head_norm, k_head_norm, w_mla_branch, w_merge, b_merge, w_out, ffn2_norm, ffn2_w_in, ffn2_w_out, final_norm, loss_target, m_ffn1_norm, m_ffn1_w_in, m_ffn1_w_out, m_mix_norm, m_w_in, m_hg_lb_table, m_hg_out_norm, m_w_hg_branch, m_mla_q_lora_norm, m_w_q_up, m_mla_kv_lora_norm, m_w_kv_up, m_q_head_norm, m_k_head_norm, m_w_mla_branch, m_w_merge, m_b_merge, m_w_out, m_ffn2_norm, m_ffn2_w_in, m_ffn2_w_out, m_final_norm, v_ffn1_norm, v_ffn1_w_in, v_ffn1_w_out, v_mix_norm, v_w_in, v_hg_lb_table, v_hg_out_norm, v_w_hg_branch, v_mla_q_lora_norm, v_w_q_up, v_mla_kv_lora_norm, v_w_kv_up, v_q_head_norm, v_k_head_norm, v_w_mla_branch, v_w_merge, v_b_merge, v_w_out, v_ffn2_norm, v_ffn2_w_in, v_ffn2_w_out, v_final_norm):
    a = dict(locals())
    w = {n: a[n] for n in WEIGHT_ORDER}
    mom = {n: a["m_" + n] for n in WEIGHT_ORDER}
    var = {n: a["v_" + n] for n in WEIGHT_ORDER}
    t = x.shape[1]
    tm = min(TM, t)
    xt = x.reshape(t, D)
    target = loss_target.reshape(t, D)
    pos = positions.reshape(t, 1)
    x_i, y_i, c_i = _place()
    k_idx = (2 * x_i + y_i).astype(jnp.int32)
    c_arr = c_i.astype(jnp.int32).reshape(1)
    k_arr = jnp.stack([k_idx, c_i.astype(jnp.int32)])

    group_mid = _group(("w_in", "w_hg_branch", "w_q_up", "w_kv_up", "w_mla_branch", "w_merge", "w_out"))
    use_early = _group(("ffn1_w_out", "w_in", "w_hg_branch", "w_q_up", "w_kv_up"))
    use_late = _group(("w_mla_branch", "w_merge", "w_out", "ffn2_w_out"))
    gather_first = _GatherWeights([w["ffn1_w_in"][0].astype(BF16)])
    gather_early = _GatherWeights([_pack([w[e[0]][0] for e in use_early], BF16)])
    gather_late = _GatherWeights([_pack([w[e[0]][0] for e in use_late], BF16), w["ffn2_w_in"][0].astype(BF16)])
    (ffn1_w_in_g,) = gather_first.gathered(gather_first.alone("gather_first"), k_idx)
    n1, gate1, up1, act1, got = _ffn_in("ffn1", xt, w["ffn1_norm"], ffn1_w_in_g, gather_early)
    full = _unpack_full(gather_early.gathered([got], k_idx)[0], use_early)
    h1, u = _ffn_out("ffn1", act1, xt, full["ffn1_w_out"], w["mix_norm"])
    ffn1_saved = (n1, gate1, up1, act1)
    w_in_full = full["w_in"]
    w_in_hg = w_in_full[:, :4 * D]
    w_in_mla = jnp.pad(w_in_full[:, 4 * D:], ((0, 0), (0, MLA_COLS - (4800 - 4 * D))))
    w_q_pad = jnp.pad(full["w_q_up"].reshape(Q_LORA, HEADS, QK), ((0, 0), (0, 0), (0, QKP - QK))).reshape(Q_LORA, HEADS * QKP)
    w_kv = full["w_kv_up"]
    gq = jnp.pad(w["q_head_norm"], ((0, 0), (0, QKP - QK)))
    gk = jnp.pad(w["k_head_norm"], ((0, 0), (0, QKP - QK)))

    ident = lambda accs, ex: (accs[0],)
    def in_hg(rows, weights, outs):
        a = rows[0][...]
        for j in range(4 * D // 512):
            cols = slice(j * 512, (j + 1) * 512)
            outs[0][:, cols] = _dot(a, weights[0][:, cols])

    (p_hg,) = _rows_call("in_hg", [u], [w_in_hg], [(4 * D, F32)], in_hg, min(FFN_TM, t))
    (p_mla,) = _mm("in_mla", [_a_spec(u, tm)], [_b_nn(w_in_mla, MLA_COLS)], [(0, 0)], ident, [], [F32], t, MLA_COLS, tm, MLA_COLS)
    o_raw, hg_o, states = _hgrn_fwd(p_hg, w["hg_lb_table"], w["hg_out_norm"])
    (y_hg,) = _mm("hg_branch", [_a_spec(hg_o, tm)], [_b_nn(full["w_hg_branch"], 512)], [(0, 0)], ident, [], [BF16], t, D, tm, 512)
    cqn, ckvn = _lora_norm_fwd(p_mla, w["mla_q_lora_norm"], w["mla_kv_lora_norm"])
    (qf,) = _mm("q_up", [_a_spec(cqn, tm)], [_b_nn(w_q_pad, 512)], [(0, 0)], ident, [], [F32], t, HEADS * QKP, tm, 512)
    (kvf,) = _mm("kv_up", [_a_spec(ckvn, tm)], [_b_nn(w_kv, 512)], [(0, 0)], ident, [], [F32], t, HEADS * QKP, tm, 512)
    cos, sin = _rope_tables(pos)
    qh, kh, vh = _mla_prep_fwd(qf, kvf, p_mla, cos, sin, gq, gk)
    o_mla, lse, *got = _flash_fwd(qh, kh, vh, side=gather_late)
    late, ffn2_w_in_g = gather_late.gathered(got, k_idx)
    full.update(_unpack_full(late, use_late))
    (y_mla,) = _mm("mla_branch", [_a_spec(o_mla, tm)], [_b_nn(full["w_mla_branch"], 512)], [(0, 0)], ident, [], [BF16], t, D, tm, 512)

    def merge_epi(accs, ex):
        g_hg = _sig(accs[0] + ex[2])
        g_mla = _sig(accs[1] + ex[3])
        return g_hg * ex[0].astype(F32) + g_mla * ex[1].astype(F32), g_hg, g_mla

    w_merge_f = full["w_merge"]
    mix, g_hg, g_mla = _mm(
        "merge", [_a_spec(u, tm)], [_b_nn(w_merge_f, 512), _b_nn(w_merge_f, 512, D // 512)], [(0, 0), (0, 1)], merge_epi,
        [_e_tile(y_hg, tm, 512), _e_tile(y_mla, tm, 512), _e_row(w["b_merge"], 512), _e_row(w["b_merge"], 512, D // 512)],
        [BF16, BF16, BF16], t, D, tm, 512)
    (h2,) = _mm("out_proj", [_a_spec(mix, tm)], [_b_nn(full["w_out"], 512)], [(0, 0)],
                lambda accs, ex: (ex[0] + accs[0],), [_e_tile(h1, tm, 512)], [F32], t, D, tm, 512)
    ffn2_saved = _ffn_in("ffn2", h2, w["ffn2_norm"], ffn2_w_in_g)
    dh3, d_final_norm, loss_part = _ffn_out("ffn2", ffn2_saved[3], h2, full["ffn2_w_out"], w["final_norm"], target=target)

    grads, small = {}, {}
    small["final_norm"] = d_final_norm
    reduce_last = _Reduction("last", c_arr, k_arr)
    reduce_mid = _Reduction("mid", c_arr, k_arr)
    reduce_first = _Reduction("first", c_arr, k_arr)
    dh2, small["ffn2_norm"], _, got_last = _ffn_bwd(
        "ffn2", h2, w["ffn2_norm"], ffn2_w_in_g, full["ffn2_w_out"], ffn2_saved, dh3, None, reduce_last)

    def dmix_epi(accs, ex):
        dm = accs[0]
        ghg, gml, yhg, yml = [e.astype(F32) for e in ex]
        return dm * ghg, dm * gml, dm * yhg * ghg * (1.0 - ghg), dm * yml * gml * (1.0 - gml)

    dy_hg, dy_mla, dpre_hg, dpre_mla = _mm(
        "d_mix", [_a_spec(dh2, tm)], [_b_nt(full["w_out"], 512)], [(0, 0)], dmix_epi,
        [_e_tile(g_hg, tm, 512), _e_tile(g_mla, tm, 512), _e_tile(y_hg, tm, 512), _e_tile(y_mla, tm, 512)],
        [BF16, BF16, BF16, BF16], t, D, tm, 512, trans_b=True)
    grads["w_out"] = _mm_tn("dw_out", mix, dh2)
    small["b_merge"] = jnp.concatenate([_colsum("db_hg", dpre_hg), _colsum("db_mla", dpre_mla)], axis=1)
    grads["w_merge"] = jnp.concatenate([_mm_tn("dw_merge_hg", u, dpre_hg), _mm_tn("dw_merge_mla", u, dpre_mla)], axis=1)
    grads["w_hg_branch"] = _mm_tn("dw_hg_branch", hg_o, dy_hg)
    grads["w_mla_branch"] = _mm_tn("dw_mla_branch", o_mla, dy_mla)
    (dho,) = _mm("d_hg_o", [_a_spec(dy_hg, tm)], [_b_nt(full["w_hg_branch"], 512)], [(0, 0)], ident, [], [BF16], t, D, tm, 512, trans_b=True)
    (do_mla,) = _mm("d_o_mla", [_a_spec(dy_mla, tm)], [_b_nt(full["w_mla_branch"], 512)], [(0, 0)], ident, [], [BF16], t, D, tm, 512, trans_b=True)

    dq_raw, df_raw, di_raw, dg_raw, small["hg_lb_table"], small["hg_out_norm"] = _hgrn_bwd(
        p_hg, w["hg_lb_table"], w["hg_out_norm"], o_raw, states, dho)
    dp_hg = [dq_raw, df_raw, di_raw, dg_raw]

    dqh, dkh, dvh = _flash_bwd(qh, kh, vh, lse, _attn_do(do_mla, o_mla))
    dqf, dkvf, dkpe, dgq, dgk = _mla_prep_bwd(qf, kvf, p_mla, cos, sin, gq, gk, dqh, dkh, dvh)
    small["q_head_norm"] = dgq[:, :QK]
    small["k_head_norm"] = dgk[:, :QK]
    dwq_pad = _mm_tn("dw_q_up", cqn, dqf, tm=Q_LORA, tn=1024)
    grads["w_q_up"] = dwq_pad.reshape(Q_LORA, HEADS, QKP)[:, :, :QK].reshape(Q_LORA, HEADS * QK)
    grads["w_kv_up"] = _mm_tn("dw_kv_up", ckvn, dkvf, tm=KV_LORA, tn=1024)
    (dcqn,) = _mm("d_cq", [_a_spec(dqf, tm)], [_b_nt(w_q_pad, Q_LORA)], [(0, 0)], ident, [], [F32], t, Q_LORA, tm, Q_LORA, trans_b=True)
    (dckvn,) = _mm("d_ckv", [_a_spec(dkvf, tm)], [_b_nt(w_kv, KV_LORA)], [(0, 0)], ident, [], [F32], t, KV_LORA, tm, KV_LORA, trans_b=True)
    dp_mla, small["mla_q_lora_norm"], small["mla_kv_lora_norm"] = _lora_norm_bwd(
        p_mla, w["mla_q_lora_norm"], w["mla_kv_lora_norm"], dcqn, dckvn, dkpe)

    dw_in_hg = [_mm_tn("dw_in_hg%d" % k, u, dp_hg[k]) for k in range(4)]
    dw_in_mla = _mm_tn("dw_in_mla", u, dp_mla, tn=MLA_COLS)
    grads["w_in"] = jnp.concatenate(dw_in_hg + [dw_in_mla[:, :4800 - 4 * D]], axis=1)
    tm_du = min(TM // 2, t)
    du, *got_mid = _mm(
        "d_u",
        [_a_spec(dpre_hg, tm_du), _a_spec(dpre_mla, tm_du)] + [_a_spec(d, tm_du) for d in dp_hg] + [_a_spec(dp_mla, tm_du)],
        [_b_nt(w_merge_f, 512, D, 0), _b_nt(w_merge_f, 512, D, 1)]
        + [_b_nt(w_in_hg, 512, D, k) for k in range(4)] + [_b_nt(w_in_mla, 512)],
        [(k, k) for k in range(7)],
        lambda accs, ex: (functools.reduce(lambda p, q: p + q, accs),), [], [F32], t, D, tm_du, 512, trans_b=True,
        side=reduce_mid.begin([_pack_grads(grads, group_mid)]))
    dh1, small["mix_norm"] = _rms_bwd("mix_dnorm", h1, w["mix_norm"], du, dh2)
    dx, small["ffn1_norm"], _, got_first = _ffn_bwd(
        "ffn1", xt, w["ffn1_norm"], ffn1_w_in_g, full["ffn1_w_out"], ffn1_saved, dh1, None, reduce_first)

    g_shard = _unpack_shard(reduce_mid.end(got_mid)[0], group_mid)
    g_shard["ffn2_w_in"], g_shard["ffn2_w_out"] = reduce_last.end(got_last)
    g_shard["ffn1_w_in"], g_shard["ffn1_w_out"] = reduce_first.end(got_first)
    small_sum = _all_reduce_small(_pack_small([small[n] for n, _ in SMALL] + [loss_part])).reshape(-1)
    g_small, at = {}, 0
    for n, shape in SMALL:
        size = shape[0] * shape[1]
        g_small[n] = small_sum[at:at + size].reshape(shape)
        at += size
    loss = small_sum[at]

    g_out, d_out, m_out, v_out = {}, {}, {}, {}
    for n in WEIGHT_ORDER:
        shape = w[n].shape
        g = g_shard[n] if n in g_shard else g_small[n]
        two = g.shape
        d_, m_, v_ = _adamw("adamw_" + n, w[n].reshape(two), g, mom[n].reshape(two), var[n].reshape(two))
        g_out[n], d_out[n], m_out[n], v_out[n] = g.reshape(shape), d_.reshape(shape), m_.reshape(shape), v_.reshape(shape)

    return (loss, dx.reshape(x.shape), *[g_out[n] for n in WEIGHT_ORDER], *[d_out[n] for n in WEIGHT_ORDER],
            *[m_out[n] for n in WEIGHT_ORDER], *[v_out[n] for n in WEIGHT_ORDER])
```

```python
import functools

import numpy as np
import jax
import jax.numpy as jnp
from jax import lax
from jax.experimental import pallas as pl
from jax.experimental.pallas import tpu as pltpu

F32 = jnp.float32
BF16 = jnp.bfloat16
MESH = pl.DeviceIdType.MESH

D = 1024
DFF = 2816
HEADS = 8
HK = 128
CHUNK = 64
ROPE = 64
QK = 192
QKP = 256
Q_LORA = 384
KV_LORA = 256
MLA_COLS = 768
EPS = 1e-6
ROPE_THETA = 10000.0
SCALE = QK ** -0.5
LOG2E = 1.4426950408889634
LN2 = 0.6931471805599453
NEG = -1e30
EXP_CLAMP = 80.0

ADAM_LR = 0.001
ADAM_B1 = 0.9
ADAM_B2 = 0.999
ADAM_EPS = 1e-08
ADAM_WD = 0.01
ADAM_STEP = 10

PACK_W = 1024
ADD_ROWS = 352
PACK_ALIGN = 2 * ADD_ROWS

TM = 1024
FFN_TM = 512
FFN_CHUNK = 256
FFN_MAIN = 1280
TQ = 2048
SUBQ = 256
HG_BT = 512
HG_HPB = 8
TT = 1024
ROW_TM = 256

VMEM_MB = 48

BIG = (
    ("ffn1_w_in", D, 2 * DFF, 1),
    ("ffn1_w_out", DFF, D, 0),
    ("w_in", D, 4800, 1),
    ("w_hg_branch", D, D, 0),
    ("w_q_up", Q_LORA, HEADS * QK, 1),
    ("w_kv_up", KV_LORA, HEADS * 2 * HK, 1),
    ("w_mla_branch", D, D, 0),
    ("w_merge", D, 2 * D, 1),
    ("w_out", D, D, 0),
    ("ffn2_w_in", D, 2 * DFF, 1),
    ("ffn2_w_out", DFF, D, 0),
)
SMALL = (
    ("ffn1_norm", (1, D)),
    ("mix_norm", (1, D)),
    ("hg_lb_table", (2, D)),
    ("hg_out_norm", (1, HK)),
    ("mla_q_lora_norm", (1, Q_LORA)),
    ("mla_kv_lora_norm", (1, KV_LORA)),
    ("q_head_norm", (1, QK)),
    ("k_head_norm", (1, QK)),
    ("b_merge", (1, 2 * D)),
    ("ffn2_norm", (1, D)),
    ("final_norm", (1, D)),
)
WEIGHT_ORDER = ("ffn1_norm", "ffn1_w_in", "ffn1_w_out", "mix_norm", "w_in", "hg_lb_table", "hg_out_norm",
                "w_hg_branch", "mla_q_lora_norm", "w_q_up", "mla_kv_lora_norm", "w_kv_up", "q_head_norm",
                "k_head_norm", "w_mla_branch", "w_merge", "b_merge", "w_out", "ffn2_norm", "ffn2_w_in",
                "ffn2_w_out", "final_norm")


def _call(body, **kw):
    return pl.pallas_call(body, **kw)


def _cp(vmem_mb=VMEM_MB):
    return pltpu.CompilerParams(vmem_limit_bytes=vmem_mb << 20)


def _dot(a, b):
    return lax.dot_general(a, b, (((1,), (0,)), ((), ())), preferred_element_type=F32)


def _dot_nt(a, b):
    return lax.dot_general(a, b, (((1,), (1,)), ((), ())), preferred_element_type=F32)


def _dot_tn(a, b):
    return lax.dot_general(a, b, (((0,), (0,)), ((), ())), preferred_element_type=F32)


def _sig(x):
    return jax.nn.sigmoid(x)


def _silu(x):
    return x * _sig(x)


def _dsilu(x):
    s = _sig(x)
    return s * (1.0 + x * (1.0 - s))


def _a_spec(arr, tm, kblk=None, kidx=0):
    kb = arr.shape[1] if kblk is None else kblk
    return arr, pl.BlockSpec((tm, kb), lambda i, j, kidx=kidx: (i, kidx)), slice(kidx * kb, (kidx + 1) * kb)


def _b_nn(arr, tn, off=0):
    return arr, pl.BlockSpec((arr.shape[0], tn), lambda i, j, off=off: (0, j + off)), ("cols", off)


def _b_nt(arr, tn, kblk=None, kidx=0):
    kb = arr.shape[1] if kblk is None else kblk
    return arr, pl.BlockSpec((tn, kb), lambda i, j, kidx=kidx: (j, kidx)), ("rows", slice(kidx * kb, (kidx + 1) * kb))


def _e_tile(arr, tm, tn, off=0):
    return arr, pl.BlockSpec((tm, tn), lambda i, j, off=off: (i, j + off)), ("tile", off)


def _e_row(arr, tn, off=0):
    return arr, pl.BlockSpec((1, tn), lambda i, j, off=off: (0, j + off)), ("row", off)


def _mm_resident(name, As, Bs, dots, epi, extras, out_dtypes, m, n, tn):
    def unique(arrays):
        seen = []
        for a in arrays:
            if not any(a is s for s in seen):
                seen.append(a)
        return seen

    rows = unique([a for a, _, _ in As] + [e for e, _, where in extras if where[0] == "tile"])
    weights = unique([b for b, _, _ in Bs] + [e for e, _, where in extras if where[0] == "row"])

    def ref_of(arr, row_refs, weight_refs):
        for r, ref in zip(rows, row_refs):
            if r is arr:
                return ref
        for wt, ref in zip(weights, weight_refs):
            if wt is arr:
                return ref

    def compute(row_refs, weight_refs, out_refs):
        a_vals = [ref_of(a, row_refs, weight_refs)[:, ks].astype(BF16) for a, _, ks in As]
        for j in range(n // tn):
            accs = []
            for ai, bi in dots:
                b, _, where = Bs[bi]
                b_ref = ref_of(b, row_refs, weight_refs)
                if where[0] == "cols":
                    accs.append(_dot(a_vals[ai], b_ref[:, (j + where[1]) * tn:(j + where[1] + 1) * tn]))
                else:
                    accs.append(_dot_nt(a_vals[ai], b_ref[j * tn:(j + 1) * tn, where[1]]))
            ex = [ref_of(e, row_refs, weight_refs)[:, (j + where[1]) * tn:(j + where[1] + 1) * tn]
                  for e, _, where in extras]
            for o_ref, o in zip(out_refs, epi(accs, ex)):
                o_ref[:, j * tn:(j + 1) * tn] = o.astype(o_ref.dtype)

    return _rows_call(name, rows, weights, [(n, dt) for dt in out_dtypes], compute, min(FFN_TM, m))


def _mm(name, As, Bs, dots, epi, extras, out_dtypes, m, n, tm, tn, trans_b=False, side=None):
    if side is None:
        return _mm_resident(name, As, Bs, dots, epi, extras, out_dtypes, m, n, tn)
    na, nb, ne, no = len(As), len(Bs), len(extras), len(out_dtypes)
    ni, nj = m // tm, n // tn
    s_in = len(side.inputs) if side else 0
    s_out = len(side.out_shapes) if side else 0

    def body(*refs):
        a_refs = refs[:na]
        b_refs = refs[na:na + nb]
        e_refs = refs[na + nb:na + nb + ne]
        at = na + nb + ne
        side_refs = refs[at:at + s_in]
        o_refs = refs[at + s_in:at + s_in + no]
        side_refs = list(side_refs) + list(refs[at + s_in + no:])
        if side:
            i, j = pl.program_id(0), pl.program_id(1)

            @pl.when(jnp.logical_and(i == 0, j == 0))
            def _():
                side.start(*side_refs)

        a_vals = [r[...].astype(BF16) for r in a_refs]
        accs = []
        for ai, bi in dots:
            b = b_refs[bi][...]
            accs.append(_dot_nt(a_vals[ai], b) if trans_b else _dot(a_vals[ai], b))
        outs = epi(accs, [r[...] for r in e_refs])
        for o_ref, o in zip(o_refs, outs):
            o_ref[...] = o.astype(o_ref.dtype)
        if side:
            @pl.when(jnp.logical_and(i == ni - 1, j == nj - 1))
            def _():
                side.finish(*side_refs)

    ops = list(As) + list(Bs) + list(extras)
    anywhere = pl.BlockSpec(memory_space=pl.ANY)
    res = _call(
        body, name=name,
        grid=(ni, nj),
        in_specs=[op[1] for op in ops] + [anywhere] * s_in,
        out_specs=[pl.BlockSpec((tm, tn), lambda i, j: (i, j)) for _ in out_dtypes] + [anywhere] * s_out,
        out_shape=[jax.ShapeDtypeStruct((m, n), dt) for dt in out_dtypes] + (list(side.out_shapes) if side else []),
        scratch_shapes=list(side.scratch) if side else [],
        compiler_params=_cp(),
    )(*[op[0] for op in ops], *(side.inputs if side else []))
    return res


def _rows_call(name, rows, weights, outs, compute, tm, side=None, sums=(), vmem_mb=VMEM_MB):
    t = rows[0].shape[0]
    nr, nw, no = len(rows), len(weights), len(outs) + len(sums)
    ni = t // tm
    s_in = len(side.inputs) if side else 0
    s_out = len(side.out_shapes) if side else 0

    def body(*refs):
        at = nr + nw
        side_refs = list(refs[at:at + s_in]) + list(refs[at + s_in + no:])
        if side:
            @pl.when(pl.program_id(0) == 0)
            def _():
                side.start(*side_refs)

        out_refs = refs[at + s_in:at + s_in + no]
        if sums:
            @pl.when(pl.program_id(0) == 0)
            def _():
                for r in out_refs[len(outs):]:
                    r[...] = jnp.zeros_like(r)

        compute(refs[:nr], refs[nr:at], out_refs)
        if side:
            @pl.when(pl.program_id(0) == ni - 1)
            def _():
                side.finish(*side_refs)

    anywhere = pl.BlockSpec(memory_space=pl.ANY)
    return _call(
        body, name=name, grid=(ni,),
        in_specs=[pl.BlockSpec((tm, r.shape[1]), lambda i: (i, 0)) for r in rows]
        + [pl.BlockSpec(wt.shape, lambda i, nd=wt.ndim: (0,) * nd) for wt in weights] + [anywhere] * s_in,
        out_specs=[pl.BlockSpec((tm, width), lambda i: (i, 0)) for width, _ in outs]
        + [pl.BlockSpec((1, width), lambda i: (0, 0)) for width in sums] + [anywhere] * s_out,
        out_shape=[jax.ShapeDtypeStruct((t, width), dt) for width, dt in outs]
        + [jax.ShapeDtypeStruct((1, width), F32) for width in sums] + (list(side.out_shapes) if side else []),
        scratch_shapes=list(side.scratch) if side else [],
        compiler_params=_cp(vmem_mb),
    )(*rows, *weights, *(side.inputs if side else []))


def _mm_tn(name, a, b, scale=1.0, tm=1024, tn=1024, stacked=None, into=None):
    t, m = a.shape
    n = b.shape[1]
    tm, tn, tt = min(tm, m), min(tn, n), min(TT, t)
    nk = t // tt

    def body(a_ref, b_ref, *rest):
        o_ref = rest[-1]
        k = pl.program_id(2)

        @pl.when(k == 0)
        def _():
            o_ref[...] = jnp.zeros_like(o_ref)

        o_ref[...] += _dot_tn(a_ref[...].astype(BF16), b_ref[...].astype(BF16))
        if scale != 1.0:
            @pl.when(k == nk - 1)
            def _():
                o_ref[...] = o_ref[...] * scale

    return _call(
        body, name=name,
        grid=(m // tm, n // tn, nk),
        in_specs=[pl.BlockSpec((tt, tm), lambda i, j, k: (k, i)), pl.BlockSpec((tt, tn), lambda i, j, k: (k, j))]
        + ([pl.BlockSpec(memory_space=pl.ANY)] if into is not None else []),
        out_specs=(pl.BlockSpec((None, tm, tn), lambda i, j, k: (stacked[1] + j, i, 0)) if stacked
                   else pl.BlockSpec((tm, tn), lambda i, j, k: (i, j))),
        out_shape=jax.ShapeDtypeStruct((stacked[0], m, tn) if stacked else (m, n), F32),
        input_output_aliases={2: 0} if into is not None else {},
        compiler_params=_cp(),
    )(a, b, *([into] if into is not None else []))


def _rms_bwd_vals(xv, g, dn):
    r = lax.rsqrt(jnp.mean(xv * xv, axis=-1, keepdims=True) + EPS)
    xh = xv * r
    dxh = dn * g
    c = jnp.mean(dxh * xh, axis=-1, keepdims=True)
    return r * (dxh - xh * c), dn * xh


def _rms_bwd(name, x, gain, dn, dres):
    t, d = x.shape
    tm = min(ROW_TM, t)

    def body(x_ref, g_ref, dn_ref, dr_ref, dx_ref, dg_ref):
        @pl.when(pl.program_id(0) == 0)
        def _():
            dg_ref[...] = jnp.zeros_like(dg_ref)

        dx, dg = _rms_bwd_vals(x_ref[...], g_ref[...], dn_ref[...].astype(F32))
        dx_ref[...] = dr_ref[...] + dx
        dg_ref[...] += jnp.sum(dg, axis=0, keepdims=True)

    row = pl.BlockSpec((tm, d), lambda i: (i, 0))
    one = pl.BlockSpec((1, d), lambda i: (0, 0))
    return _call(
        body, name=name, grid=(t // tm,),
        in_specs=[row, one, row, row],
        out_specs=[row, one],
        out_shape=[jax.ShapeDtypeStruct((t, d), F32), jax.ShapeDtypeStruct((1, d), F32)],
        compiler_params=_cp(),
    )(x, gain, dn, dres)


def _colsum(name, x):
    t, n = x.shape
    tm = min(TM, t)

    def body(x_ref, o_ref):
        @pl.when(pl.program_id(0) == 0)
        def _():
            o_ref[...] = jnp.zeros_like(o_ref)

        o_ref[...] += jnp.sum(x_ref[...].astype(F32), axis=0, keepdims=True)

    return _call(
        body, name=name, grid=(t // tm,),
        in_specs=[pl.BlockSpec((tm, n), lambda i: (i, 0))],
        out_specs=pl.BlockSpec((1, n), lambda i: (0, 0)),
        out_shape=jax.ShapeDtypeStruct((1, n), F32),
        compiler_params=_cp(),
    )(x)


def _in_mla(u, w_in_mla, gq, gkv):
    t = u.shape[0]

    def compute(rows, weights, outs):
        p = _dot(rows[0][...], weights[0][...])
        outs[0][...] = p
        cq = p[:, 0:Q_LORA]
        ckv = p[:, Q_LORA:Q_LORA + KV_LORA]
        rq = lax.rsqrt(jnp.mean(cq * cq, axis=-1, keepdims=True) + EPS)
        rkv = lax.rsqrt(jnp.mean(ckv * ckv, axis=-1, keepdims=True) + EPS)
        outs[1][...] = (cq * rq * weights[1][...]).astype(BF16)
        outs[2][...] = (ckv * rkv * weights[2][...]).astype(BF16)

    return _rows_call("in_mla", [u], [w_in_mla, gq, gkv], [(MLA_COLS, F32), (Q_LORA, BF16), (KV_LORA, BF16)], compute,
                      min(FFN_TM, t))


def _lora_norm_bwd(p_mla, gq, gkv, dcqn, dckvn, dkpe):
    t = p_mla.shape[0]
    tm = min(ROW_TM, t)

    def body(p_ref, gq_ref, gkv_ref, dq_ref, dkv_ref, dkpe_ref, dp_ref, dgq_ref, dgkv_ref):
        @pl.when(pl.program_id(0) == 0)
        def _():
            dgq_ref[...] = jnp.zeros_like(dgq_ref)
            dgkv_ref[...] = jnp.zeros_like(dgkv_ref)

        dcq, dgq = _rms_bwd_vals(p_ref[:, 0:Q_LORA], gq_ref[...], dq_ref[...])
        dckv, dgkv = _rms_bwd_vals(p_ref[:, Q_LORA:Q_LORA + KV_LORA], gkv_ref[...], dkv_ref[...])
        dp_ref[:, 0:Q_LORA] = dcq.astype(BF16)
        dp_ref[:, Q_LORA:Q_LORA + KV_LORA] = dckv.astype(BF16)
        dp_ref[:, Q_LORA + KV_LORA:MLA_COLS] = dkpe_ref[...].astype(BF16)
        dgq_ref[...] += jnp.sum(dgq, axis=0, keepdims=True)
        dgkv_ref[...] += jnp.sum(dgkv, axis=0, keepdims=True)

    return _call(
        body, name="lora_norm_bwd", grid=(t // tm,),
        in_specs=[pl.BlockSpec((tm, MLA_COLS), lambda i: (i, 0)),
                  pl.BlockSpec((1, Q_LORA), lambda i: (0, 0)), pl.BlockSpec((1, KV_LORA), lambda i: (0, 0)),
                  pl.BlockSpec((tm, Q_LORA), lambda i: (i, 0)), pl.BlockSpec((tm, KV_LORA), lambda i: (i, 0)),
                  pl.BlockSpec((tm, HK), lambda i: (i, 0))],
        out_specs=[pl.BlockSpec((tm, MLA_COLS), lambda i: (i, 0)),
                   pl.BlockSpec((1, Q_LORA), lambda i: (0, 0)), pl.BlockSpec((1, KV_LORA), lambda i: (0, 0))],
        out_shape=[jax.ShapeDtypeStruct((t, MLA_COLS), BF16), jax.ShapeDtypeStruct((1, Q_LORA), F32),
                   jax.ShapeDtypeStruct((1, KV_LORA), F32)],
        compiler_params=_cp(),
    )(p_mla, gq, gkv, dcqn, dckvn, dkpe)


def _cumsum_rows(x, row):
    for s in (1, 2, 4, 8, 16, 32):
        x = x + jnp.where(row >= s, pltpu.roll(x, s, 0), 0.0)
    return x


def _rcumsum_rows(x, row):
    for s in (1, 2, 4, 8, 16, 32):
        x = x + jnp.where(row < CHUNK - s, pltpu.roll(x, CHUNK - s, 0), 0.0)
    return x


def _hg_gates(qr, z, lb, row):
    q = _silu(qr)
    sg = _sig(z)
    f = lb + (1.0 - lb) * sg
    lf = jnp.log(f)
    k = (1.0 - lb) * (1.0 - sg)
    cum = _cumsum_rows(lf, row)
    mid = jnp.sum(jnp.where(row < CHUNK // 2, lf, 0.0), axis=0, keepdims=True)
    last = jnp.sum(lf, axis=0, keepdims=True)
    e_q = jnp.exp(jnp.minimum(cum - mid, EXP_CLAMP))
    e_k = jnp.exp(jnp.minimum(mid - cum, EXP_CLAMP))
    e_a = jnp.exp(cum)
    e_l = jnp.exp(last - cum)
    return q, sg, f, k, last, e_q, e_k, e_a, e_l


def _hgrn_fwd(p_hg, tab, gain):
    t = p_hg.shape[0]
    bt = min(HG_BT, t)
    nb, nc = t // bt, bt // CHUNK

    hpb = HG_HPB
    wide = hpb * HK

    def body(q_ref, f_ref, i_ref, g_ref, tab_ref, gain_ref, o_ref, ho_ref, st_ref, state):
        @pl.when(pl.program_id(1) == 0)
        def _():
            state[...] = jnp.zeros_like(state)

        row = lax.broadcasted_iota(jnp.int32, (CHUNK, HK), 0)
        tril = lax.broadcasted_iota(jnp.int32, (CHUNK, CHUNK), 0) >= lax.broadcasted_iota(jnp.int32, (CHUNK, CHUNK), 1)
        gain_v = gain_ref[...]

        def chunk(c, carry):
            sl = pl.ds(pl.multiple_of(c * CHUNK, CHUNK), CHUNK)
            for hh in range(hpb):
                ln = slice(hh * HK, (hh + 1) * HK)
                lb = _sig(tab_ref[0:1, ln] - tab_ref[1:2, ln])
                v = i_ref[sl, ln].astype(BF16)
                q, _, _, k, last, e_q, e_k, e_a, e_l = _hg_gates(q_ref[sl, ln], f_ref[sl, ln], lb, row)
                st = state[hh]
                st_ref[hh, c] = st
                p = jnp.where(tril, _dot_nt((q * e_q).astype(BF16), (k * e_k).astype(BF16)), 0.0)
                o = _dot(p.astype(BF16), v) + _dot_nt((q * e_a).astype(BF16), st.astype(BF16))
                state[hh] = jnp.exp(last) * st + _dot_tn(v, (k * e_l).astype(BF16))
                o_ref[sl, ln] = o
                r = lax.rsqrt(jnp.mean(o * o, axis=-1, keepdims=True) + EPS)
                ho_ref[sl, ln] = (o * r * gain_v * _silu(g_ref[sl, ln])).astype(BF16)
            return carry

        lax.fori_loop(0, nc, chunk, 0)

    def col(k):
        return pl.BlockSpec((bt, wide), lambda h, j, k=k: (j, k * (HEADS // hpb) + h))

    return _call(
        body, name="hgrn_fwd", grid=(HEADS // hpb, nb),
        in_specs=[col(0), col(1), col(2), col(3),
                  pl.BlockSpec((2, wide), lambda h, j: (0, h)), pl.BlockSpec((1, HK), lambda h, j: (0, 0))],
        out_specs=[pl.BlockSpec((bt, wide), lambda h, j: (j, h)), pl.BlockSpec((bt, wide), lambda h, j: (j, h)),
                   pl.BlockSpec((hpb, nc, HK, HK), lambda h, j: (h, j, 0, 0))],
        out_shape=[jax.ShapeDtypeStruct((t, D), F32), jax.ShapeDtypeStruct((t, D), BF16),
                   jax.ShapeDtypeStruct((HEADS, t // CHUNK, HK, HK), F32)],
        scratch_shapes=[pltpu.VMEM((hpb, HK, HK), F32)],
        compiler_params=_cp(),
    )(p_hg, p_hg, p_hg, p_hg, tab, gain)


def _hgrn_bwd(p_hg, tab, gain, o_raw, states, dho):
    t = p_hg.shape[0]
    bt = min(HG_BT, t)
    nb, nc = t // bt, bt // CHUNK
    hpb = HG_HPB
    wide = hpb * HK

    def body(q_ref, f_ref, i_ref, g_ref, tab_ref, gain_ref, o_ref, st_ref, dho_ref,
             dq_ref, df_ref, di_ref, dg_ref, dtab_ref, dgain_ref, dstate, dlb):
        h, j = pl.program_id(0), pl.program_id(1)

        @pl.when(jnp.logical_and(h == 0, j == 0))
        def _():
            dgain_ref[...] = jnp.zeros_like(dgain_ref)

        @pl.when(j == 0)
        def _():
            dstate[...] = jnp.zeros_like(dstate)
            dlb[...] = jnp.zeros_like(dlb)

        row = lax.broadcasted_iota(jnp.int32, (CHUNK, HK), 0)
        tril = lax.broadcasted_iota(jnp.int32, (CHUNK, CHUNK), 0) >= lax.broadcasted_iota(jnp.int32, (CHUNK, CHUNK), 1)
        gain_v = gain_ref[...]

        def chunk(cc, carry):
            c = nc - 1 - cc
            sl = pl.ds(pl.multiple_of(c * CHUNK, CHUNK), CHUNK)
            dgain = jnp.zeros((1, HK), F32)
            for hh in range(hpb):
                ln = slice(hh * HK, (hh + 1) * HK)
                lb = _sig(tab_ref[0:1, ln] - tab_ref[1:2, ln])
                qr = q_ref[sl, ln]
                v = i_ref[sl, ln].astype(BF16)
                gr = g_ref[sl, ln]
                q, sg, f, k, last, e_q, e_k, e_a, e_l = _hg_gates(qr, f_ref[sl, ln], lb, row)
                o = o_ref[sl, ln]
                r = lax.rsqrt(jnp.mean(o * o, axis=-1, keepdims=True) + EPS)
                oh = o * r
                dh = dho_ref[sl, ln].astype(F32)
                dnorm = dh * _silu(gr)
                dg_ref[sl, ln] = (dh * oh * gain_v * _dsilu(gr)).astype(BF16)
                dgain = dgain + jnp.sum(dnorm * oh, axis=0, keepdims=True)
                dxh = dnorm * gain_v
                do = (r * (dxh - oh * jnp.mean(dxh * oh, axis=-1, keepdims=True))).astype(BF16)
                st0 = st_ref[hh, c]
                st0_b = st0.astype(BF16)
                ds1 = dstate[hh]
                ds1_b = ds1.astype(BF16)
                qt = (q * e_q).astype(BF16)
                kt = (k * e_k).astype(BF16)
                qd = (q * e_a).astype(BF16)
                kd = (k * e_l).astype(BF16)
                p = jnp.where(tril, _dot_nt(qt, kt), 0.0).astype(BF16)
                dp = jnp.where(tril, _dot_nt(do, v), 0.0).astype(BF16)
                dv = _dot_tn(p, do) + _dot_nt(kd, ds1_b)
                dqt = _dot(dp, kt)
                dkt = _dot_tn(dp, qt)
                dq_inter = _dot(do, st0_b) * e_a
                dk_inter = _dot(v, ds1_b) * e_l
                dq = dqt * e_q + dq_inter
                dk = dkt * e_k + dk_inter
                e_last = jnp.exp(last)
                dstate[hh] = _dot_tn(do, qd) + e_last * ds1
                dlast = (jnp.sum(k * dk_inter, axis=0, keepdims=True)
                         + e_last * jnp.sum(ds1 * st0, axis=0, keepdims=True))
                da = (qt.astype(F32) * dqt - kt.astype(F32) * dkt + q * dq_inter - k * dk_inter
                      + jnp.where(row == CHUNK - 1, dlast, 0.0))
                dlf = _rcumsum_rows(da, row)
                dfv = dlf / f - dk
                df_ref[sl, ln] = (dfv * (1.0 - lb) * sg * (1.0 - sg)).astype(BF16)
                dlb[:, ln] += jnp.sum(dfv * (1.0 - sg), axis=0, keepdims=True)
                dq_ref[sl, ln] = (dq * _dsilu(qr)).astype(BF16)
                di_ref[sl, ln] = dv.astype(BF16)
            dgain_ref[...] += dgain
            return carry

        lax.fori_loop(0, nc, chunk, 0)

        @pl.when(j == nb - 1)
        def _():
            lb = _sig(tab_ref[0:1, :] - tab_ref[1:2, :])
            d0 = dlb[...] * lb * (1.0 - lb)
            dtab_ref[0:1, :] = d0
            dtab_ref[1:2, :] = -d0

    def col(k):
        return pl.BlockSpec((bt, wide), lambda h, j, k=k: (nb - 1 - j, k * (HEADS // hpb) + h))

    tok = pl.BlockSpec((bt, wide), lambda h, j: (nb - 1 - j, h))
    return _call(
        body, name="hgrn_bwd", grid=(HEADS // hpb, nb),
        in_specs=[col(0), col(1), col(2), col(3),
                  pl.BlockSpec((2, wide), lambda h, j: (0, h)), pl.BlockSpec((1, HK), lambda h, j: (0, 0)),
                  tok, pl.BlockSpec((hpb, nc, HK, HK), lambda h, j: (h, nb - 1 - j, 0, 0)), tok],
        out_specs=[tok, tok, tok, tok,
                   pl.BlockSpec((2, wide), lambda h, j: (0, h)), pl.BlockSpec((1, HK), lambda h, j: (0, 0))],
        out_shape=[jax.ShapeDtypeStruct((t, D), BF16)] * 4
        + [jax.ShapeDtypeStruct((2, D), F32), jax.ShapeDtypeStruct((1, HK), F32)],
        scratch_shapes=[pltpu.VMEM((hpb, HK, HK), F32), pltpu.VMEM((1, wide), F32)],
        compiler_params=_cp(),
    )(p_hg, p_hg, p_hg, p_hg, tab, gain, o_raw, states, dho)


def _rope_tables(pos):
    t = pos.shape[0]
    tm = min(ROW_TM, t)
    inv = np.zeros((1, HK), np.float32)
    freq = (ROPE_THETA ** (-np.arange(0, ROPE, 2, dtype=np.float32) / ROPE)).astype(np.float32)
    inv[0, 0:ROPE // 2] = freq
    inv[0, ROPE // 2:ROPE] = freq
    sign = np.zeros((1, HK), np.float32)
    sign[0, 0:ROPE // 2] = -1.0
    sign[0, ROPE // 2:ROPE] = 1.0

    def body(pos_ref, inv_ref, sign_ref, cos_ref, sin_ref):
        ang = pos_ref[...].astype(F32) * inv_ref[...]
        cos_ref[...] = jnp.cos(ang)
        sin_ref[...] = jnp.sin(ang) * sign_ref[...]

    one = pl.BlockSpec((1, HK), lambda i: (0, 0))
    row = pl.BlockSpec((tm, HK), lambda i: (i, 0))
    return _call(
        body, name="rope_tables", grid=(t // tm,),
        in_specs=[pl.BlockSpec((tm, 1), lambda i: (i, 0)), one, one],
        out_specs=[row, row],
        out_shape=[jax.ShapeDtypeStruct((t, HK), F32)] * 2,
        compiler_params=_cp(),
    )(pos, jnp.asarray(inv), jnp.asarray(sign))


def _rope(x, cos, sin_signed):
    r = lax.broadcasted_iota(jnp.int32, (HK, HK), 0)
    c = lax.broadcasted_iota(jnp.int32, (HK, HK), 1)
    half = ROPE // 2
    swap = jnp.logical_or(jnp.logical_and(c < half, r == c + half),
                          jnp.logical_and(jnp.logical_and(c >= half, c < ROPE), r == c - half))
    return x * cos + _dot_split(x, swap.astype(BF16)) * sin_signed


def _dot_split(x, m):
    hi = x.astype(BF16)
    lo = (x - hi.astype(F32)).astype(BF16)
    return _dot(hi, m) + _dot(lo, m)


def _lane_sum(x):
    return _dot_split(x, jnp.ones((HK, HK), BF16))


def _head_norm(xn, xr):
    r = lax.rsqrt(_lane_sum(xn * xn + xr * xr) * (1.0 / QK) + EPS)
    return xn * r, xr * r, r


def _head_norm_bwd(xn, xr, g_n, g_r, dn, dr):
    hn, hr, r = _head_norm(xn, xr)
    dxn, dxr = dn * g_n, dr * g_r
    c = _lane_sum(dxn * hn + dxr * hr) * (1.0 / QK)
    return r * (dxn - hn * c), r * (dxr - hr * c), dn * hn, dr * hr


def _mla_prep_fwd(qf, kv, p_mla, cos, sin, gq, gk):
    t = qf.shape[0]
    tm = min(ROW_TM, t)

    def body(qf_ref, kv_ref, kpe_ref, cos_ref, sin_ref, gq_ref, gk_ref, q_ref, k_ref, v_ref):
        cos_v, sin_v = cos_ref[...], sin_ref[...]
        kpe = kpe_ref[...]
        for h in range(HEADS):
            lo, mid, hi = h * QKP, h * QKP + HK, (h + 1) * QKP
            qn, qr, _ = _head_norm(qf_ref[:, lo:mid], qf_ref[:, mid:hi])
            q_ref[h, :, 0:HK] = (qn * gq_ref[:, 0:HK] * (SCALE * LOG2E)).astype(BF16)
            q_ref[h, :, HK:QKP] = (_rope(qr * gq_ref[:, HK:QKP], cos_v, sin_v) * (SCALE * LOG2E)).astype(BF16)
            kn, kr, _ = _head_norm(kv_ref[:, lo:mid], kpe)
            k_ref[h, :, 0:HK] = (kn * gk_ref[:, 0:HK]).astype(BF16)
            k_ref[h, :, HK:QKP] = _rope(kr * gk_ref[:, HK:QKP], cos_v, sin_v).astype(BF16)
            v_ref[h, :, 0:HK] = kv_ref[:, mid:hi].astype(BF16)
            v_ref[h, :, HK:QKP] = jnp.full((tm, HK), -1.0, BF16)

    head = pl.BlockSpec((tm, HEADS * QKP), lambda i: (i, 0))
    tok = pl.BlockSpec((tm, HK), lambda i: (i, 0))
    gain = pl.BlockSpec((1, QKP), lambda i: (0, 0))
    return _call(
        body, name="mla_prep_fwd", grid=(t // tm,),
        in_specs=[head, head, pl.BlockSpec((tm, HK), lambda i: (i, MLA_COLS // HK - 1)), tok, tok, gain, gain],
        out_specs=[pl.BlockSpec((HEADS, tm, QKP), lambda i: (0, i, 0)),
                   pl.BlockSpec((HEADS, tm, QKP), lambda i: (0, i, 0)),
                   pl.BlockSpec((HEADS, tm, QKP), lambda i: (0, i, 0))],
        out_shape=[jax.ShapeDtypeStruct((HEADS, t, QKP), BF16), jax.ShapeDtypeStruct((HEADS, t, QKP), BF16),
                   jax.ShapeDtypeStruct((HEADS, t, QKP), BF16)],
        compiler_params=_cp(),
    )(qf, kv, p_mla, cos, sin, gq, gk)


def _mla_prep_bwd(qf, kv, p_mla, cos, sin, gq, gk, dq, dk, dv):
    t = qf.shape[0]
    tm = min(ROW_TM, t)

    def body(qf_ref, kv_ref, kpe_ref, cos_ref, sin_ref, gq_ref, gk_ref, dq_ref, dk_ref, dv_ref,
             dqf_ref, dkv_ref, dkpe_ref, dgq_ref, dgk_ref):
        @pl.when(pl.program_id(0) == 0)
        def _():
            dgq_ref[...] = jnp.zeros_like(dgq_ref)
            dgk_ref[...] = jnp.zeros_like(dgk_ref)

        cos_v, sin_v = cos_ref[...], -sin_ref[...]
        kpe = kpe_ref[...]
        gqn, gqr, gkn, gkr = gq_ref[:, 0:HK], gq_ref[:, HK:QKP], gk_ref[:, 0:HK], gk_ref[:, HK:QKP]
        dkpe = jnp.zeros((tm, HK), F32)
        dgq_n, dgq_r, dgk_n, dgk_r = [jnp.zeros((1, HK), F32) for _ in range(4)]
        for h in range(HEADS):
            lo, mid, hi = h * QKP, h * QKP + HK, (h + 1) * QKP
            dqn = dq_ref[h, :, 0:HK].astype(F32) * SCALE
            dqr = _rope(dq_ref[h, :, HK:QKP].astype(F32), cos_v, sin_v) * SCALE
            a, b, ga, gb = _head_norm_bwd(qf_ref[:, lo:mid], qf_ref[:, mid:hi], gqn, gqr, dqn, dqr)
            dqf_ref[:, lo:mid] = a.astype(BF16)
            dqf_ref[:, mid:hi] = b.astype(BF16)
            dgq_n = dgq_n + jnp.sum(ga, axis=0, keepdims=True)
            dgq_r = dgq_r + jnp.sum(gb, axis=0, keepdims=True)
            dkn = dk_ref[h, :, 0:HK].astype(F32) * LN2
            dkr = _rope(dk_ref[h, :, HK:QKP].astype(F32), cos_v, sin_v) * LN2
            a, b, ga, gb = _head_norm_bwd(kv_ref[:, lo:mid], kpe, gkn, gkr, dkn, dkr)
            dkv_ref[:, lo:mid] = a.astype(BF16)
            dkv_ref[:, mid:hi] = dv_ref[h].astype(BF16)
            dkpe = dkpe + b
            dgk_n = dgk_n + jnp.sum(ga, axis=0, keepdims=True)
            dgk_r = dgk_r + jnp.sum(gb, axis=0, keepdims=True)
        dkpe_ref[...] = dkpe
        dgq_ref[:, 0:HK] += dgq_n
        dgq_ref[:, HK:QKP] += dgq_r
        dgk_ref[:, 0:HK] += dgk_n
        dgk_ref[:, HK:QKP] += dgk_r

    head = pl.BlockSpec((tm, HEADS * QKP), lambda i: (i, 0))
    tok = pl.BlockSpec((tm, HK), lambda i: (i, 0))
    gain = pl.BlockSpec((1, QKP), lambda i: (0, 0))
    hq = pl.BlockSpec((HEADS, tm, QKP), lambda i: (0, i, 0))
    return _call(
        body, name="mla_prep_bwd", grid=(t // tm,),
        in_specs=[head, head, pl.BlockSpec((tm, HK), lambda i: (i, MLA_COLS // HK - 1)), tok, tok, gain, gain,
                  hq, hq, pl.BlockSpec((HEADS, tm, HK), lambda i: (0, i, 0))],
        out_specs=[head, head, tok, gain, gain],
        out_shape=[jax.ShapeDtypeStruct((t, HEADS * QKP), BF16), jax.ShapeDtypeStruct((t, HEADS * QKP), BF16),
                   jax.ShapeDtypeStruct((t, HK), F32), jax.ShapeDtypeStruct((1, QKP), F32),
                   jax.ShapeDtypeStruct((1, QKP), F32)],
        compiler_params=_cp(),
    )(qf, kv, p_mla, cos, sin, gq, gk, dq, dk, dv)


def _chunk_mask(row0, rows, cols):
    r = lax.broadcasted_iota(jnp.int32, (rows, cols), 0) + row0
    c = lax.broadcasted_iota(jnp.int32, (rows, cols), 1)
    return jnp.right_shift(r, 6) >= jnp.right_shift(c, 6)


def _flash_fwd(q, k, v, side=None):
    t = q.shape[1]
    tq = min(TQ, t)
    nq = t // tq
    sub = min(SUBQ, tq)
    pairs = [(i, j) for i in range(nq) for j in range(i + 1)]
    qi = jnp.asarray([p[0] for p in pairs], jnp.int32)
    kj = jnp.asarray([p[1] for p in pairs], jnp.int32)
    s_in = len(side.inputs) if side else 0
    s_out = len(side.out_shapes) if side else 0

    def body(qi_ref, kj_ref, q_ref, k_ref, v_ref, *rest):
        o_ref, lse_ref = rest[s_in:s_in + 2]
        m_s, acc_s = rest[s_in + 2 + s_out:s_in + 4 + s_out]
        side_refs = list(rest[:s_in]) + list(rest[s_in + 2:s_in + 2 + s_out]) + list(rest[s_in + 4 + s_out:])
        n = pl.program_id(1)
        i, j = qi_ref[n], kj_ref[n]
        if side:
            @pl.when(jnp.logical_and(pl.program_id(0) == 0, n == 0))
            def _():
                side.start(*side_refs)

        @pl.when(j == 0)
        def _():
            m_s[...] = jnp.full_like(m_s, NEG)
            acc_s[...] = jnp.zeros_like(acc_s)

        def step(diag):
            subs = range(tq // sub)
            width = [(r + 1) * sub if diag else tq for r in subs]
            logits = [_dot_nt(q_ref[r * sub:(r + 1) * sub, :], k_ref[0:width[r], :]) for r in subs]
            for r in subs:
                rows = slice(r * sub, (r + 1) * sub)
                cols = width[r]
                s = logits[r]
                if diag:
                    s = jnp.where(_chunk_mask(r * sub, sub, cols), s, NEG)
                m_old = m_s[rows, :]
                m_new = jnp.maximum(m_old, jnp.max(s, axis=-1, keepdims=True))
                alpha = jnp.exp2(m_old - m_new)
                p = jnp.exp2((s - jnp.tile(m_new, (1, cols // HK))).astype(BF16))
                acc_s[rows, :] = jnp.tile(alpha, (1, 2)) * acc_s[rows, :] + _dot(p, v_ref[0:cols, :])
                m_s[rows, :] = m_new

        @pl.when(j < i)
        def _():
            step(False)

        @pl.when(j == i)
        def _():
            step(True)
            l = -acc_s[:, HK:QKP]
            o_ref[...] = (acc_s[:, 0:HK] / l).astype(BF16)
            lse_ref[...] = m_s[...] + jnp.log(l) * LOG2E

        if side:
            @pl.when(jnp.logical_and(pl.program_id(0) == HEADS - 1, n == len(pairs) - 1))
            def _():
                side.finish(*side_refs)

    anywhere = pl.BlockSpec(memory_space=pl.ANY)
    grid_spec = pltpu.PrefetchScalarGridSpec(
        num_scalar_prefetch=2, grid=(HEADS, len(pairs)),
        in_specs=[pl.BlockSpec((None, tq, QKP), lambda h, n, qi, kj: (h, qi[n], 0)),
                  pl.BlockSpec((None, tq, QKP), lambda h, n, qi, kj: (h, kj[n], 0)),
                  pl.BlockSpec((None, tq, QKP), lambda h, n, qi, kj: (h, kj[n], 0))] + [anywhere] * s_in,
        out_specs=[pl.BlockSpec((tq, HK), lambda h, n, qi, kj: (qi[n], h)),
                   pl.BlockSpec((None, tq, HK), lambda h, n, qi, kj: (h, qi[n], 0))] + [anywhere] * s_out,
        scratch_shapes=[pltpu.VMEM((tq, HK), F32), pltpu.VMEM((tq, QKP), F32)] + (list(side.scratch) if side else []),
    )
    return _call(
        body, name="flash_fwd", grid_spec=grid_spec,
        out_shape=[jax.ShapeDtypeStruct((t, D), BF16), jax.ShapeDtypeStruct((HEADS, t, HK), F32)]
        + (list(side.out_shapes) if side else []),
        compiler_params=_cp(),
    )(qi, kj, q, k, v, *(side.inputs if side else []))


def _attn_do(do, o):
    t = do.shape[0]
    tm = min(TM, t)

    def body(do_ref, o_ref, d_ref):
        lane = lax.broadcasted_iota(jnp.int32, (tm, HK), 1)
        for h in range(HEADS):
            ln = slice(h * HK, (h + 1) * HK)
            dov = do_ref[:, ln]
            d = jnp.sum(dov.astype(F32) * o_ref[:, ln].astype(F32), axis=-1, keepdims=True)
            hi = d.astype(BF16).astype(F32)
            d_ref[h, :, 0:HK] = dov
            d_ref[h, :, HK:QKP] = jnp.where(lane == 0, hi, jnp.where(lane == 1, d - hi, 0.0)).astype(BF16)

    blk = pl.BlockSpec((tm, D), lambda i: (i, 0))
    return _call(
        body, name="attn_do", grid=(t // tm,),
        in_specs=[blk, blk],
        out_specs=pl.BlockSpec((HEADS, tm, QKP), lambda i: (0, i, 0)),
        out_shape=jax.ShapeDtypeStruct((HEADS, t, QKP), BF16),
        compiler_params=_cp(),
    )(do, o)


def _flash_bwd(q, k, v, lse, do):
    t = q.shape[1]
    tq = min(TQ, t)
    nq = t // tq
    sub = min(SUBQ, tq)
    pairs = [(i, j) for j in range(nq) for i in range(j, nq)]
    qi = jnp.asarray([p[0] for p in pairs], jnp.int32)
    kj = jnp.asarray([p[1] for p in pairs], jnp.int32)
    npairs = len(pairs)

    def body(qi_ref, kj_ref, q_ref, k_ref, v_ref, lse_ref, do_ref, dq_ref, dk_ref, dv_ref):
        n = pl.program_id(1)
        i, j = qi_ref[n], kj_ref[n]

        @pl.when(n == 0)
        def _():
            dq_ref[...] = jnp.zeros_like(dq_ref)

        @pl.when(i == j)
        def _():
            dk_ref[...] = jnp.zeros_like(dk_ref)
            dv_ref[...] = jnp.zeros_like(dv_ref)

        def step(diag):
            for r in range(tq // sub):
                rows = slice(r * sub, (r + 1) * sub)
                cols = (r + 1) * sub if diag else tq
                qv, kv_ = q_ref[rows, :], k_ref[0:cols, :]
                p = jnp.exp2(_dot_nt(qv, kv_) - jnp.tile(lse_ref[rows, :], (1, cols // HK)))
                if diag:
                    p = jnp.where(_chunk_mask(r * sub, sub, cols), p, 0.0)
                dp_less_delta = _dot_nt(do_ref[rows, :], v_ref[0:cols, :])
                ds = (p * dp_less_delta).astype(BF16)
                dv_ref[0:cols, :] += _dot_tn(p.astype(BF16), do_ref[rows, 0:HK])
                dk_ref[0:cols, :] += _dot_tn(ds, qv)
                dq_rows = pl.ds(pl.multiple_of(i * tq + r * sub, sub), sub)
                dq_ref[dq_rows, :] += _dot(ds, kv_)

        @pl.when(j < i)
        def _():
            step(False)

        @pl.when(j == i)
        def _():
            step(True)

    grid_spec = pltpu.PrefetchScalarGridSpec(
        num_scalar_prefetch=2, grid=(HEADS, npairs),
        in_specs=[pl.BlockSpec((None, tq, QKP), lambda h, n, qi, kj: (h, qi[n], 0)),
                  pl.BlockSpec((None, tq, QKP), lambda h, n, qi, kj: (h, kj[n], 0)),
                  pl.BlockSpec((None, tq, QKP), lambda h, n, qi, kj: (h, kj[n], 0)),
                  pl.BlockSpec((None, tq, HK), lambda h, n, qi, kj: (h, qi[n], 0)),
                  pl.BlockSpec((None, tq, QKP), lambda h, n, qi, kj: (h, qi[n], 0))],
        out_specs=[pl.BlockSpec((None, t, QKP), lambda h, n, qi, kj: (h, 0, 0)),
                   pl.BlockSpec((None, tq, QKP), lambda h, n, qi, kj: (h, kj[n], 0)),
                   pl.BlockSpec((None, tq, HK), lambda h, n, qi, kj: (h, kj[n], 0))],
    )
    return _call(
        body, name="flash_bwd", grid_spec=grid_spec,
        out_shape=[jax.ShapeDtypeStruct((HEADS, t, QKP), F32), jax.ShapeDtypeStruct((HEADS, t, QKP), F32),
                   jax.ShapeDtypeStruct((HEADS, t, HK), F32)],
        compiler_params=_cp(56),
    )(qi, kj, q, k, v, lse, do)


def _adamw(name, w, g, m, v):
    r, c = w.shape
    tr = r if r <= 256 else next(k for k in (256, 352, 384) if r % k == 0)

    def body(w_ref, g_ref, m_ref, v_ref, d_ref, nm_ref, nv_ref):
        gv = g_ref[...]
        nm = ADAM_B1 * m_ref[...] + (1.0 - ADAM_B1) * gv
        nv = ADAM_B2 * v_ref[...] + (1.0 - ADAM_B2) * (gv * gv)
        m_hat = nm / (1.0 - ADAM_B1 ** ADAM_STEP)
        v_hat = nv / (1.0 - ADAM_B2 ** ADAM_STEP)
        d_ref[...] = -ADAM_LR * (m_hat / (jnp.sqrt(v_hat) + ADAM_EPS) + ADAM_WD * w_ref[...])
        nm_ref[...] = nm
        nv_ref[...] = nv

    blk = pl.BlockSpec((tr, c), lambda i: (i, 0))
    return _call(
        body, name=name, grid=(r // tr,),
        in_specs=[blk] * 4, out_specs=[blk] * 3,
        out_shape=[jax.ShapeDtypeStruct((r, c), F32)] * 3,
        compiler_params=_cp(),
    )(w, g, m, v)


def _place():
    return lax.axis_index("x"), lax.axis_index("y"), lax.axis_index("c")


def _other_chips(x, y):
    return [(1 - x, y), (x, 1 - y), (1 - x, 1 - y)]


class _Exchange:
    inputs = ()
    out_shapes = ()
    scratch = ()

    def start(self, *refs):
        raise NotImplementedError

    def finish(self, *refs):
        raise NotImplementedError

    def alone(self, name):
        def body(*refs):
            self.start(*refs)
            self.finish(*refs)

        anywhere = pl.BlockSpec(memory_space=pl.ANY)
        return _call(
            body, name=name,
            in_specs=[anywhere] * len(self.inputs), out_specs=[anywhere] * len(self.out_shapes),
            out_shape=list(self.out_shapes), scratch_shapes=list(self.scratch),
        )(*self.inputs)


class _GatherWeights(_Exchange):
    def __init__(self, shards):
        self.inputs = tuple(shards)
        self.out_shapes = tuple(jax.ShapeDtypeStruct((4,) + s.shape, s.dtype) for s in shards)
        self.scratch = (pltpu.SemaphoreType.DMA((6 * len(shards),)), pltpu.SemaphoreType.DMA((6 * len(shards),)))

    def gathered(self, got, k):
        return [lax.dynamic_update_slice(g, s[None], (k, 0, 0)) for g, s in zip(got, self.inputs)]

    def _copies(self, *refs):
        nbuf = len(self.inputs)
        send_sems, recv_sems = refs[2 * nbuf:]
        x, y, c = _place()
        chips = _other_chips(x, y)
        first, passed, landed, relayed = [], [], [], []
        for b, (s_ref, g_ref) in enumerate(zip(refs[:nbuf], refs[nbuf:2 * nbuf])):
            half = self.inputs[b].shape[0] // 2

            def rows(px, py, pc, g_ref=g_ref, half=half):
                return g_ref.at[2 * px + py, pl.ds(pc * half, half), :]

            def copy(k, block, to, src=None, rows=rows, b=b):
                return pltpu.make_async_remote_copy(
                    src_ref=rows(*block) if src is None else src, dst_ref=rows(*block),
                    send_sem=send_sems.at[6 * b + k], recv_sem=recv_sems.at[6 * b + k], device_id=to, device_id_type=MESH)

            mine = s_ref.at[pl.ds(c * half, half), :]
            first += [copy(j, (x, y, c), (*chip, c), src=mine) for j, chip in enumerate(chips)]
            passed += [copy(3 + j, (*chip, c), (x, y, 1 - c)) for j, chip in enumerate(chips)]
            landed += [copy(j, (*chip, c), (x, y, c)) for j, chip in enumerate(chips)]
            relayed += [copy(3 + j, (*chip, 1 - c), (x, y, c)) for j, chip in enumerate(chips)]
        return first, passed, landed, relayed

    def start(self, *refs):
        for cp in self._copies(*refs)[0]:
            cp.start()

    def finish(self, *refs):
        first, passed, landed, relayed = self._copies(*refs)
        for arrived, onward in zip(landed, passed):
            arrived.wait_recv()
            onward.start()
        for cp in relayed:
            cp.wait_recv()
        for cp in first + passed:
            cp.wait_send()


def _swap_halves(name, bufs):
    nbuf = len(bufs)

    def body(*refs):
        send_sems, recv_sems = refs[2 * nbuf:]
        x, y, c = _place()
        cps = []
        for b, (g_ref, o_ref) in enumerate(zip(refs[:nbuf], refs[nbuf:2 * nbuf])):
            half = bufs[b].shape[1] // 2
            cps.append(pltpu.make_async_remote_copy(
                src_ref=g_ref.at[:, pl.ds((1 - c) * half, half), :], dst_ref=o_ref,
                send_sem=send_sems.at[b], recv_sem=recv_sems.at[b], device_id=(x, y, 1 - c), device_id_type=MESH))
        for cp in cps:
            cp.start()
        for cp in cps:
            cp.wait()

    anywhere = pl.BlockSpec(memory_space=pl.ANY)
    return _call(
        body, name=name,
        in_specs=[anywhere] * nbuf, out_specs=[anywhere] * nbuf,
        out_shape=[jax.ShapeDtypeStruct((4, g.shape[1] // 2, g.shape[2]), g.dtype) for g in bufs],
        scratch_shapes=[pltpu.SemaphoreType.DMA((nbuf,)), pltpu.SemaphoreType.DMA((nbuf,))],
    )(*bufs)


def _add_rows(half):
    return next(tr for tr in range(512, 15, -16) if half % tr == 0)


def _chip_sum(name, gp, got, c_arr):
    half, width = got.shape[1], got.shape[2]
    tr = _add_rows(half)
    nb = half // tr

    def body(c_ref, a_ref, b_ref, o_ref, ob_ref):
        s = a_ref[...] + b_ref[...]
        o_ref[...] = s
        ob_ref[...] = s.astype(BF16)

    grid_spec = pltpu.PrefetchScalarGridSpec(
        num_scalar_prefetch=1, grid=(4, nb),
        in_specs=[pl.BlockSpec((None, tr, width), lambda s, i, c: (s, c[0] * nb + i, 0)),
                  pl.BlockSpec((None, tr, width), lambda s, i, c: (s, i, 0))],
        out_specs=[pl.BlockSpec((None, tr, width), lambda s, i, c: (s, i, 0)),
                   pl.BlockSpec((None, tr, width), lambda s, i, c: (s, i, 0))],
    )
    return _call(
        body, name=name, grid_spec=grid_spec,
        out_shape=[jax.ShapeDtypeStruct(got.shape, F32), jax.ShapeDtypeStruct(got.shape, BF16)],
        compiler_params=_cp(),
    )(c_arr, gp, got)


class _ScatterChipSums(_Exchange):
    def __init__(self, sums):
        self.inputs = tuple(sums)
        self.out_shapes = tuple(jax.ShapeDtypeStruct((3,) + cs.shape[1:], cs.dtype) for cs in sums)
        self.scratch = (pltpu.SemaphoreType.DMA((3 * len(sums),)), pltpu.SemaphoreType.DMA((3 * len(sums),)))

    def _copies(self, *refs):
        nbuf = len(self.inputs)
        send_sems, recv_sems = refs[2 * nbuf:]
        x, y, c = _place()
        return [pltpu.make_async_remote_copy(
            src_ref=s_ref.at[2 * px + py], dst_ref=o_ref.at[j],
            send_sem=send_sems.at[3 * b + j], recv_sem=recv_sems.at[3 * b + j], device_id=(px, py, c), device_id_type=MESH)
            for b, (s_ref, o_ref) in enumerate(zip(refs[:nbuf], refs[nbuf:2 * nbuf]))
            for j, (px, py) in enumerate(_other_chips(x, y))]

    def start(self, *refs):
        for cp in self._copies(*refs):
            cp.start()

    def finish(self, *refs):
        for cp in self._copies(*refs):
            cp.wait()


def _shard_sum(name, cs, got, kc_arr):
    h, width = cs.shape[1], cs.shape[2]
    tr = _add_rows(h)
    nb = h // tr

    def body(k_ref, a_ref, b_ref, o_ref):
        o_ref[...] = ((a_ref[...] + b_ref[0].astype(F32)) + b_ref[1].astype(F32)) + b_ref[2].astype(F32)

    grid_spec = pltpu.PrefetchScalarGridSpec(
        num_scalar_prefetch=1, grid=(nb,),
        in_specs=[pl.BlockSpec((None, tr, width), lambda i, k: (k[0], i, 0)),
                  pl.BlockSpec((3, tr, width), lambda i, k: (0, i, 0))],
        out_specs=pl.BlockSpec((tr, width), lambda i, k: (k[1] * nb + i, 0)),
    )
    return _call(
        body, name=name, grid_spec=grid_spec,
        out_shape=jax.ShapeDtypeStruct((2 * h, width), F32),
        compiler_params=_cp(),
    )(kc_arr, cs, got)


def _join_halves(name, boths):
    nbuf = len(boths)

    def body(*refs):
        send_sems, recv_sems = refs[2 * nbuf:]
        x, y, c = _place()
        sent, landing = [], []
        for b, (m_ref, o_ref) in enumerate(zip(refs[:nbuf], refs[nbuf:2 * nbuf])):
            h = boths[b].shape[0] // 2
            mine = m_ref.at[pl.ds(c * h, h), :]
            sent.append(pltpu.make_async_remote_copy(
                src_ref=mine, dst_ref=o_ref.at[pl.ds(c * h, h), :],
                send_sem=send_sems.at[b], recv_sem=recv_sems.at[b], device_id=(x, y, 1 - c), device_id_type=MESH))
            landing.append(pltpu.make_async_remote_copy(
                src_ref=mine, dst_ref=o_ref.at[pl.ds((1 - c) * h, h), :],
                send_sem=send_sems.at[b], recv_sem=recv_sems.at[b], device_id=(x, y, 1 - c), device_id_type=MESH))
        for cp in sent:
            cp.start()
        for cp in sent:
            cp.wait_send()
        for cp in landing:
            cp.wait_recv()

    anywhere = pl.BlockSpec(memory_space=pl.ANY)
    return _call(
        body, name=name,
        in_specs=[anywhere] * nbuf, out_specs=[anywhere] * nbuf,
        out_shape=[jax.ShapeDtypeStruct(g.shape, g.dtype) for g in boths],
        input_output_aliases={b: b for b in range(nbuf)},
        scratch_shapes=[pltpu.SemaphoreType.DMA((nbuf,)), pltpu.SemaphoreType.DMA((nbuf,))],
    )(*boths)


def _all_reduce_small(v):
    r = v.shape[0]

    def body(v_ref, o_ref, buf, send_sems, recv_sems):
        x, y, c = _place()
        me = 4 * x + 2 * y + c
        buf[me] = v_ref[...]
        cps = []
        for k in range(1, 8):
            peer = (x ^ (k >> 2), y ^ ((k >> 1) & 1), c ^ (k & 1))
            cps.append(pltpu.make_async_remote_copy(
                src_ref=v_ref, dst_ref=buf.at[me],
                send_sem=send_sems.at[k - 1], recv_sem=recv_sems.at[k - 1], device_id=peer, device_id_type=MESH))
        for cp in cps:
            cp.start()
        for k in range(1, 8):
            pltpu.make_async_remote_copy(
                src_ref=v_ref, dst_ref=buf.at[me ^ k],
                send_sem=send_sems.at[k - 1], recv_sem=recv_sems.at[k - 1],
                device_id=(x, y, c), device_id_type=MESH).wait_recv()
        for cp in cps:
            cp.wait_send()
        acc = buf[0]
        for k in range(1, 8):
            acc = acc + buf[k]
        o_ref[...] = acc

    return _call(
        body, name="all_reduce_small",
        in_specs=[pl.BlockSpec(memory_space=pltpu.VMEM)],
        out_specs=pl.BlockSpec(memory_space=pltpu.VMEM),
        out_shape=jax.ShapeDtypeStruct((r, 128), F32),
        scratch_shapes=[pltpu.VMEM((8, r, 128), F32), pltpu.SemaphoreType.DMA((7,)), pltpu.SemaphoreType.DMA((7,))],
    )(v)


def _group(names):
    return tuple(e for e in BIG if e[0] in names)


def _pack(shards, dtype):
    return jnp.concatenate([s.astype(dtype).reshape(-1, PACK_W) for s in shards], axis=0)


def _unpack_full(g, group):
    out, at = {}, 0
    for name, rows, cols, axis in group:
        n = rows * cols // 4 // PACK_W
        blk = g[:, at:at + n, :]
        at += n
        if axis == 1:
            out[name] = blk.reshape(4, rows, cols // 4).transpose(1, 0, 2).reshape(rows, cols)
        else:
            out[name] = blk.reshape(rows, cols)
    return out


def _pack_grads(grads, group):
    parts = []
    for name, rows, cols, axis in group:
        g = grads[name]
        if axis == 1:
            g = g.reshape(rows, 4, cols // 4).transpose(1, 0, 2)
        parts.append(g.reshape(4, -1, PACK_W))
    rows_total = sum(p.shape[1] for p in parts)
    pad = -rows_total % PACK_ALIGN
    if pad:
        parts.append(jnp.zeros((4, pad, PACK_W), F32))
    return jnp.concatenate(parts, axis=1)


def _unpack_shard(s, group):
    out, at = {}, 0
    for name, rows, cols, axis in group:
        n = rows * cols // 4 // PACK_W
        shape = (rows, cols // 4) if axis == 1 else (rows // 4, cols)
        out[name] = s[at:at + n, :].reshape(shape)
        at += n
    return out


def _pack_small(parts):
    flat = jnp.concatenate([p.reshape(-1) for p in parts])
    pad = -flat.shape[0] % 1024
    return jnp.concatenate([flat, jnp.zeros((pad,), F32)]).reshape(-1, 128)


def _ffn_in(tag, h, gain, w_in, side=None):
    t = h.shape[0]
    wide = DFF // 2

    def compute_in(rows, weights, outs):
        hv, w_ref = rows[0][...], weights[0]
        r = lax.rsqrt(jnp.mean(hv * hv, axis=-1, keepdims=True) + EPS)
        a = (hv * r * weights[1][...]).astype(BF16)
        outs[0][...] = a

        def emit(gate, up, cols):
            outs[1][:, cols] = gate.astype(BF16)
            outs[2][:, cols] = up.astype(BF16)
            outs[3][:, cols] = (_silu(gate) * up).astype(BF16)

        for s in range(2):
            emit(_dot(a, w_ref[s, :, 0:FFN_MAIN]), _dot(a, w_ref[2 + s, :, 0:FFN_MAIN]),
                 slice(s * wide, s * wide + FFN_MAIN))
        gate = _dot(a, jnp.concatenate([w_ref[0, :, FFN_MAIN:wide], w_ref[1, :, FFN_MAIN:wide]], axis=1))
        up = _dot(a, jnp.concatenate([w_ref[2, :, FFN_MAIN:wide], w_ref[3, :, FFN_MAIN:wide]], axis=1))
        rest = wide - FFN_MAIN
        for s in range(2):
            emit(gate[:, s * rest:(s + 1) * rest], up[:, s * rest:(s + 1) * rest],
                 slice(s * wide + FFN_MAIN, (s + 1) * wide))

    return _rows_call(tag + "_in", [h], [w_in, gain], [(D, BF16)] + [(DFF, BF16)] * 3, compute_in, min(FFN_TM, t),
                      side=side)


def _ffn_out(tag, act, h, w_out, next_gain, target=None):
    t = h.shape[0]
    tm = min(FFN_TM, t)

    def compute_out(rows, weights, outs):
        hn = rows[1][...] + 0.5 * _dot(rows[0][...], weights[0][...])
        g = weights[1][...]
        r = lax.rsqrt(jnp.mean(hn * hn, axis=-1, keepdims=True) + EPS)
        xh = hn * r
        if target is None:
            outs[0][...] = hn
            outs[1][...] = (xh * g).astype(BF16)
        else:
            err = xh * g - rows[2][...]
            dy = err * (1.0 / D)
            dxh = dy * g
            outs[0][...] = r * (dxh - xh * jnp.mean(dxh * xh, axis=-1, keepdims=True))
            outs[1][...] += jnp.sum(dy * xh, axis=0, keepdims=True)
            outs[2][...] += 0.5 * jnp.sum(jnp.mean(err * err, axis=-1, keepdims=True), axis=0, keepdims=True)

    if target is None:
        return _rows_call(tag + "_out", [act, h], [w_out, next_gain], [(D, F32), (D, BF16)], compute_out, tm)
    return _rows_call(tag + "_out", [act, h, target], [w_out, next_gain], [(D, F32)], compute_out, tm, sums=(D, 128))


class _Reduction:
    def __init__(self, tag, c_arr, k_arr):
        self.tag, self.c_arr, self.k_arr = tag, c_arr, k_arr

    def begin(self, bufs):
        swapped = _swap_halves("grad_swap_" + self.tag, bufs)
        sums = [_chip_sum("grad_chip_sum_%s%d" % (self.tag, b), gp, got, self.c_arr)
                for b, (gp, got) in enumerate(zip(bufs, swapped))]
        self.sums = [s[0] for s in sums]
        return _ScatterChipSums([s[1] for s in sums])

    def end(self, got):
        mine = [_shard_sum("grad_shard_sum_%s%d" % (self.tag, b), cs, g, self.k_arr)
                for b, (cs, g) in enumerate(zip(self.sums, got))]
        return _join_halves("grad_join_" + self.tag, mine)


def _ffn_bwd(tag, h, gain, w_in, w_out, saved, dout, side, reduction):
    t = h.shape[0]
    tm = min(TM, t)
    n, gate, up, act = saved

    def compute(rows, weights, outs):
        d = rows[0][...].astype(BF16)
        for j in range(DFF // FFN_CHUNK):
            cols = slice(j * FFN_CHUNK, (j + 1) * FFN_CHUNK)
            da = 0.5 * _dot_nt(d, weights[0][cols, :])
            g, u = rows[1][:, cols].astype(F32), rows[2][:, cols].astype(F32)
            s = _sig(g)
            silu = g * s
            outs[0][:, cols] = (da * u * (s + silu * (1.0 - s))).astype(BF16)
            outs[1][:, cols] = (da * silu).astype(BF16)

    dgate, dup, *side_out = _rows_call(tag + "_dact", [dout, gate, up], [w_out], [(DFF, BF16)] * 2, compute,
                                       min(FFN_TM, t), side=side)
    dw_out = _mm_tn(tag + "_dw_out", act, dout, scale=0.5, tm=DFF // 2, tn=D)
    dw_in = _mm_tn(tag + "_dw_gate", n, dgate, tm=D, tn=DFF // 2, stacked=(4, 0))
    dw_in = _mm_tn(tag + "_dw_up", n, dup, tm=D, tn=DFF // 2, stacked=(4, 2), into=dw_in)
    sending = reduction.begin([dw_in, dw_out.reshape(4, DFF // 4, D)])

    def compute_dn(rows, weights, outs):
        w_ref = weights[0]
        wide = DFF // 2
        dn = jnp.zeros((rows[0].shape[0], D), F32)
        for s in range(2):
            cols = slice(s * wide, s * wide + FFN_MAIN)
            dn = (dn + _dot_nt(rows[0][:, cols], w_ref[s, :, 0:FFN_MAIN])
                  + _dot_nt(rows[1][:, cols], w_ref[2 + s, :, 0:FFN_MAIN]))
        for r, first in ((0, 0), (1, 2)):
            x = jnp.concatenate([rows[r][:, FFN_MAIN:wide], rows[r][:, wide + FFN_MAIN:2 * wide]], axis=1)
            wt = jnp.concatenate([w_ref[first, :, FFN_MAIN:wide], w_ref[first + 1, :, FFN_MAIN:wide]], axis=1)
            dn = dn + _dot_nt(x, wt)
        dx, dg = _rms_bwd_vals(rows[2][...], weights[1][...], dn)
        outs[0][...] = rows[3][...] + dx
        outs[1][...] += jnp.sum(dg, axis=0, keepdims=True)

    dh, dgain, *got = _rows_call(tag + "_dn", [dgate, dup, h, dout], [w_in, gain], [(D, F32)], compute_dn,
                                 min(FFN_TM, t), side=sending, sums=(D,), vmem_mb=58)
    return dh, dgain, side_out, got


def kernel(x, positions, ffn1_norm, ffn1_w_in, ffn1_w_out, mix_norm, w_in, hg_lb_table, hg_out_norm, w_hg_branch, mla_q_lora_norm, w_q_up, mla_kv_lora_norm, w_kv_up, q_head_norm, k_head_norm, w_mla_branch, w_merge, b_merge, w_out, ffn2_norm, ffn2_w_in, ffn2_w_out, final_norm, loss_target, m_ffn1_norm, m_ffn1_w_in, m_ffn1_w_out, m_mix_norm, m_w_in, m_hg_lb_table, m_hg_out_norm, m_w_hg_branch, m_mla_q_lora_norm, m_w_q_up, m_mla_kv_lora_norm, m_w_kv_up, m_q_head_norm, m_k_head_norm, m_w_mla_branch, m_w_merge, m_b_merge, m_w_out, m_ffn2_norm, m_ffn2_w_in, m_ffn2_w_out, m_final_norm, v_ffn1_norm, v_ffn1_w_in, v_ffn1_w_out, v_mix_norm, v_w_in, v_hg_lb_table, v_hg_out_norm, v_w_hg_branch, v_mla_q_lora_norm, v_w_q_up, v_mla_kv_lora_norm, v_w_kv_up, v_q_head_norm, v_k_head_norm, v_w_mla_branch, v_w_merge, v_b_merge, v_w_out, v_ffn2_norm, v_ffn2_w_in, v_ffn2_w_out, v_final_norm):
    a = dict(locals())
    w = {n: a[n] for n in WEIGHT_ORDER}
    mom = {n: a["m_" + n] for n in WEIGHT_ORDER}
    var = {n: a["v_" + n] for n in WEIGHT_ORDER}
    t = x.shape[1]
    tm = min(TM, t)
    xt = x.reshape(t, D)
    target = loss_target.reshape(t, D)
    pos = positions.reshape(t, 1)
    x_i, y_i, c_i = _place()
    k_idx = (2 * x_i + y_i).astype(jnp.int32)
    c_arr = c_i.astype(jnp.int32).reshape(1)
    k_arr = jnp.stack([k_idx, c_i.astype(jnp.int32)])

    group_mid = _group(("w_in", "w_hg_branch", "w_q_up", "w_kv_up", "w_mla_branch", "w_merge", "w_out"))
    use_early = _group(("ffn1_w_out", "w_in", "w_hg_branch", "w_q_up", "w_kv_up"))
    use_late = _group(("w_mla_branch", "w_merge", "w_out", "ffn2_w_out"))
    gather_first = _GatherWeights([w["ffn1_w_in"][0].astype(BF16)])
    gather_early = _GatherWeights([_pack([w[e[0]][0] for e in use_early], BF16)])
    gather_late = _GatherWeights([_pack([w[e[0]][0] for e in use_late], BF16), w["ffn2_w_in"][0].astype(BF16)])
    (ffn1_w_in_g,) = gather_first.gathered(gather_first.alone("gather_first"), k_idx)
    n1, gate1, up1, act1, got = _ffn_in("ffn1", xt, w["ffn1_norm"], ffn1_w_in_g, gather_early)
    full = _unpack_full(gather_early.gathered([got], k_idx)[0], use_early)
    h1, u = _ffn_out("ffn1", act1, xt, full["ffn1_w_out"], w["mix_norm"])
    ffn1_saved = (n1, gate1, up1, act1)
    w_in_full = full["w_in"]
    w_in_hg = w_in_full[:, :4 * D]
    w_in_mla = jnp.pad(w_in_full[:, 4 * D:], ((0, 0), (0, MLA_COLS - (4800 - 4 * D))))
    w_q_pad = jnp.pad(full["w_q_up"].reshape(Q_LORA, HEADS, QK), ((0, 0), (0, 0), (0, QKP - QK))).reshape(Q_LORA, HEADS * QKP)
    w_kv = full["w_kv_up"]
    gq = jnp.pad(w["q_head_norm"], ((0, 0), (0, QKP - QK)))
    gk = jnp.pad(w["k_head_norm"], ((0, 0), (0, QKP - QK)))

    ident = lambda accs, ex: (accs[0],)
    def in_hg(rows, weights, outs):
        a = rows[0][...]
        for j in range(4 * D // 512):
            cols = slice(j * 512, (j + 1) * 512)
            outs[0][:, cols] = _dot(a, weights[0][:, cols])

    (p_hg,) = _rows_call("in_hg", [u], [w_in_hg], [(4 * D, F32)], in_hg, min(FFN_TM, t))
    p_mla, cqn, ckvn = _in_mla(u, w_in_mla, w["mla_q_lora_norm"], w["mla_kv_lora_norm"])
    o_raw, hg_o, states = _hgrn_fwd(p_hg, w["hg_lb_table"], w["hg_out_norm"])
    (y_hg,) = _mm("hg_branch", [_a_spec(hg_o, tm)], [_b_nn(full["w_hg_branch"], 512)], [(0, 0)], ident, [], [BF16], t, D, tm, 512)
    (qf,) = _mm("q_up", [_a_spec(cqn, tm)], [_b_nn(w_q_pad, 512)], [(0, 0)], ident, [], [F32], t, HEADS * QKP, tm, 512)
    (kvf,) = _mm("kv_up", [_a_spec(ckvn, tm)], [_b_nn(w_kv, 512)], [(0, 0)], ident, [], [F32], t, HEADS * QKP, tm, 512)
    cos, sin = _rope_tables(pos)
    qh, kh, vh = _mla_prep_fwd(qf, kvf, p_mla, cos, sin, gq, gk)
    o_mla, lse, *got = _flash_fwd(qh, kh, vh, side=gather_late)
    late, ffn2_w_in_g = gather_late.gathered(got, k_idx)
    full.update(_unpack_full(late, use_late))
    (y_mla,) = _mm("mla_branch", [_a_spec(o_mla, tm)], [_b_nn(full["w_mla_branch"], 512)], [(0, 0)], ident, [], [BF16], t, D, tm, 512)

    def merge_epi(accs, ex):
        g_hg = _sig(accs[0] + ex[2])
        g_mla = _sig(accs[1] + ex[3])
        return g_hg * ex[0].astype(F32) + g_mla * ex[1].astype(F32), g_hg, g_mla

    w_merge_f = full["w_merge"]
    mix, g_hg, g_mla = _mm(
        "merge", [_a_spec(u, tm)], [_b_nn(w_merge_f, 512), _b_nn(w_merge_f, 512, D // 512)], [(0, 0), (0, 1)], merge_epi,
        [_e_tile(y_hg, tm, 512), _e_tile(y_mla, tm, 512), _e_row(w["b_merge"], 512), _e_row(w["b_merge"], 512, D // 512)],
        [BF16, BF16, BF16], t, D, tm, 512)
    (h2,) = _mm("out_proj", [_a_spec(mix, tm)], [_b_nn(full["w_out"], 512)], [(0, 0)],
                lambda accs, ex: (ex[0] + accs[0],), [_e_tile(h1, tm, 512)], [F32], t, D, tm, 512)
    ffn2_saved = _ffn_in("ffn2", h2, w["ffn2_norm"], ffn2_w_in_g)
    dh3, d_final_norm, loss_part = _ffn_out("ffn2", ffn2_saved[3], h2, full["ffn2_w_out"], w["final_norm"], target=target)

    grads, small = {}, {}
    small["final_norm"] = d_final_norm
    reduce_last = _Reduction("last", c_arr, k_arr)
    reduce_mid = _Reduction("mid", c_arr, k_arr)
    reduce_first = _Reduction("first", c_arr, k_arr)
    dh2, small["ffn2_norm"], _, got_last = _ffn_bwd(
        "ffn2", h2, w["ffn2_norm"], ffn2_w_in_g, full["ffn2_w_out"], ffn2_saved, dh3, None, reduce_last)

    def dmix_epi(accs, ex):
        dm = accs[0]
        ghg, gml, yhg, yml = [e.astype(F32) for e in ex]
        return dm * ghg, dm * gml, dm * yhg * ghg * (1.0 - ghg), dm * yml * gml * (1.0 - gml)

    dy_hg, dy_mla, dpre_hg, dpre_mla = _mm(
        "d_mix", [_a_spec(dh2, tm)], [_b_nt(full["w_out"], 512)], [(0, 0)], dmix_epi,
        [_e_tile(g_hg, tm, 512), _e_tile(g_mla, tm, 512), _e_tile(y_hg, tm, 512), _e_tile(y_mla, tm, 512)],
        [BF16, BF16, BF16, BF16], t, D, tm, 512, trans_b=True)
    grads["w_out"] = _mm_tn("dw_out", mix, dh2)
    small["b_merge"] = jnp.concatenate([_colsum("db_hg", dpre_hg), _colsum("db_mla", dpre_mla)], axis=1)
    grads["w_merge"] = jnp.concatenate([_mm_tn("dw_merge_hg", u, dpre_hg), _mm_tn("dw_merge_mla", u, dpre_mla)], axis=1)
    grads["w_hg_branch"] = _mm_tn("dw_hg_branch", hg_o, dy_hg)
    grads["w_mla_branch"] = _mm_tn("dw_mla_branch", o_mla, dy_mla)
    (dho,) = _mm("d_hg_o", [_a_spec(dy_hg, tm)], [_b_nt(full["w_hg_branch"], 512)], [(0, 0)], ident, [], [BF16], t, D, tm, 512, trans_b=True)
    (do_mla,) = _mm("d_o_mla", [_a_spec(dy_mla, tm)], [_b_nt(full["w_mla_branch"], 512)], [(0, 0)], ident, [], [BF16], t, D, tm, 512, trans_b=True)

    dq_raw, df_raw, di_raw, dg_raw, small["hg_lb_table"], small["hg_out_norm"] = _hgrn_bwd(
        p_hg, w["hg_lb_table"], w["hg_out_norm"], o_raw, states, dho)
    dp_hg = [dq_raw, df_raw, di_raw, dg_raw]

    dqh, dkh, dvh = _flash_bwd(qh, kh, vh, lse, _attn_do(do_mla, o_mla))
    dqf, dkvf, dkpe, dgq, dgk = _mla_prep_bwd(qf, kvf, p_mla, cos, sin, gq, gk, dqh, dkh, dvh)
    small["q_head_norm"] = dgq[:, :QK]
    small["k_head_norm"] = dgk[:, :QK]
    dwq_pad = _mm_tn("dw_q_up", cqn, dqf, tm=Q_LORA, tn=1024)
    grads["w_q_up"] = dwq_pad.reshape(Q_LORA, HEADS, QKP)[:, :, :QK].reshape(Q_LORA, HEADS * QK)
    grads["w_kv_up"] = _mm_tn("dw_kv_up", ckvn, dkvf, tm=KV_LORA, tn=1024)
    (dcqn,) = _mm("d_cq", [_a_spec(dqf, tm)], [_b_nt(w_q_pad, Q_LORA)], [(0, 0)], ident, [], [F32], t, Q_LORA, tm, Q_LORA, trans_b=True)
    (dckvn,) = _mm("d_ckv", [_a_spec(dkvf, tm)], [_b_nt(w_kv, KV_LORA)], [(0, 0)], ident, [], [F32], t, KV_LORA, tm, KV_LORA, trans_b=True)
    dp_mla, small["mla_q_lora_norm"], small["mla_kv_lora_norm"] = _lora_norm_bwd(
        p_mla, w["mla_q_lora_norm"], w["mla_kv_lora_norm"], dcqn, dckvn, dkpe)

    dw_in_hg = [_mm_tn("dw_in_hg%d" % k, u, dp_hg[k]) for k in range(4)]
    dw_in_mla = _mm_tn("dw_in_mla", u, dp_mla, tn=MLA_COLS)
    grads["w_in"] = jnp.concatenate(dw_in_hg + [dw_in_mla[:, :4800 - 4 * D]], axis=1)
    tm_du = min(TM // 2, t)
    du, *got_mid = _mm(
        "d_u",
        [_a_spec(dpre_hg, tm_du), _a_spec(dpre_mla, tm_du)] + [_a_spec(d, tm_du) for d in dp_hg] + [_a_spec(dp_mla, tm_du)],
        [_b_nt(w_merge_f, 512, D, 0), _b_nt(w_merge_f, 512, D, 1)]
        + [_b_nt(w_in_hg, 512, D, k) for k in range(4)] + [_b_nt(w_in_mla, 512)],
        [(k, k) for k in range(7)],
        lambda accs, ex: (functools.reduce(lambda p, q: p + q, accs),), [], [F32], t, D, tm_du, 512, trans_b=True,
        side=reduce_mid.begin([_pack_grads(grads, group_mid)]))
    dh1, small["mix_norm"] = _rms_bwd("mix_dnorm", h1, w["mix_norm"], du, dh2)
    dx, small["ffn1_norm"], _, got_first = _ffn_bwd(
        "ffn1", xt, w["ffn1_norm"], ffn1_w_in_g, full["ffn1_w_out"], ffn1_saved, dh1, None, reduce_first)

    g_shard = _unpack_shard(reduce_mid.end(got_mid)[0], group_mid)
    g_shard["ffn2_w_in"], g_shard["ffn2_w_out"] = reduce_last.end(got_last)
    g_shard["ffn1_w_in"], g_shard["ffn1_w_out"] = reduce_first.end(got_first)
    small_sum = _all_reduce_small(_pack_small([small[n] for n, _ in SMALL] + [loss_part])).reshape(-1)
    g_small, at = {}, 0
    for n, shape in SMALL:
        size = shape[0] * shape[1]
        g_small[n] = small_sum[at:at + size].reshape(shape)
        at += size
    loss = small_sum[at]

    g_out, d_out, m_out, v_out = {}, {}, {}, {}
    for n in WEIGHT_ORDER:
        shape = w[n].shape
        g = g_shard[n] if n in g_shard else g_small[n]
        two = g.shape
        d_, m_, v_ = _adamw("adamw_" + n, w[n].reshape(two), g, mom[n].reshape(two), var[n].reshape(two))
        g_out[n], d_out[n], m_out[n], v_out[n] = g.reshape(shape), d_.reshape(shape), m_.reshape(shape), v_.reshape(shape)

    return (loss, dx.reshape(x.shape), *[g_out[n] for n in WEIGHT_ORDER], *[d_out[n] for n in WEIGHT_ORDER],
            *[m_out[n] for n in WEIGHT_ORDER], *[v_out[n] for n in WEIGHT_ORDER])
```

```python
import functools

import numpy as np
import jax
import jax.numpy as jnp
from jax import lax
from jax.experimental import pallas as pl
from jax.experimental.pallas import tpu as pltpu

F32 = jnp.float32
BF16 = jnp.bfloat16
MESH = pl.DeviceIdType.MESH

D = 1024
DFF = 2816
HEADS = 8
HK = 128
CHUNK = 64
ROPE = 64
QK = 192
QKP = 256
Q_LORA = 384
KV_LORA = 256
MLA_COLS = 768
EPS = 1e-6
ROPE_THETA = 10000.0
SCALE = QK ** -0.5
LOG2E = 1.4426950408889634
LN2 = 0.6931471805599453
NEG = -1e30
EXP_CLAMP = 80.0

ADAM_LR = 0.001
ADAM_B1 = 0.9
ADAM_B2 = 0.999
ADAM_EPS = 1e-08
ADAM_WD = 0.01
ADAM_STEP = 10

PACK_W = 1024
ADD_ROWS = 352
PACK_ALIGN = 2 * ADD_ROWS

TM = 1024
FFN_TM = 512
FFN_CHUNK = 256
FFN_MAIN = 1280
TQ = 2048
SUBQ = 256
HG_BT = 512
HG_HPB = 8
TT = 2048
ROW_TM = 256

VMEM_MB = 48

BIG = (
    ("ffn1_w_in", D, 2 * DFF, 1),
    ("ffn1_w_out", DFF, D, 0),
    ("w_in", D, 4800, 1),
    ("w_hg_branch", D, D, 0),
    ("w_q_up", Q_LORA, HEADS * QK, 1),
    ("w_kv_up", KV_LORA, HEADS * 2 * HK, 1),
    ("w_mla_branch", D, D, 0),
    ("w_merge", D, 2 * D, 1),
    ("w_out", D, D, 0),
    ("ffn2_w_in", D, 2 * DFF, 1),
    ("ffn2_w_out", DFF, D, 0),
)
SMALL = (
    ("ffn1_norm", (1, D)),
    ("mix_norm", (1, D)),
    ("hg_lb_table", (2, D)),
    ("hg_out_norm", (1, HK)),
    ("mla_q_lora_norm", (1, Q_LORA)),
    ("mla_kv_lora_norm", (1, KV_LORA)),
    ("q_head_norm", (1, QK)),
    ("k_head_norm", (1, QK)),
    ("b_merge", (1, 2 * D)),
    ("ffn2_norm", (1, D)),
    ("final_norm", (1, D)),
)
WEIGHT_ORDER = ("ffn1_norm", "ffn1_w_in", "ffn1_w_out", "mix_norm", "w_in", "hg_lb_table", "hg_out_norm",
                "w_hg_branch", "mla_q_lora_norm", "w_q_up", "mla_kv_lora_norm", "w_kv_up", "q_head_norm",
                "k_head_norm", "w_mla_branch", "w_merge", "b_merge", "w_out", "ffn2_norm", "ffn2_w_in",
                "ffn2_w_out", "final_norm")


def _call(body, **kw):
    return pl.pallas_call(body, **kw)


def _cp(vmem_mb=VMEM_MB):
    return pltpu.CompilerParams(vmem_limit_bytes=vmem_mb << 20)


def _dot(a, b):
    return lax.dot_general(a, b, (((1,), (0,)), ((), ())), preferred_element_type=F32)


def _dot_nt(a, b):
    return lax.dot_general(a, b, (((1,), (1,)), ((), ())), preferred_element_type=F32)


def _dot_tn(a, b):
    return lax.dot_general(a, b, (((0,), (0,)), ((), ())), preferred_element_type=F32)


def _sig(x):
    return jax.nn.sigmoid(x)


def _silu(x):
    return x * _sig(x)


def _dsilu(x):
    s = _sig(x)
    return s * (1.0 + x * (1.0 - s))


def _a_spec(arr, tm, kblk=None, kidx=0):
    kb = arr.shape[1] if kblk is None else kblk
    return arr, pl.BlockSpec((tm, kb), lambda i, j, kidx=kidx: (i, kidx)), slice(kidx * kb, (kidx + 1) * kb)


def _b_nn(arr, tn, off=0):
    return arr, pl.BlockSpec((arr.shape[0], tn), lambda i, j, off=off: (0, j + off)), ("cols", off)


def _b_nt(arr, tn, kblk=None, kidx=0):
    kb = arr.shape[1] if kblk is None else kblk
    return arr, pl.BlockSpec((tn, kb), lambda i, j, kidx=kidx: (j, kidx)), ("rows", slice(kidx * kb, (kidx + 1) * kb))


def _e_tile(arr, tm, tn, off=0):
    return arr, pl.BlockSpec((tm, tn), lambda i, j, off=off: (i, j + off)), ("tile", off)


def _e_row(arr, tn, off=0):
    return arr, pl.BlockSpec((1, tn), lambda i, j, off=off: (0, j + off)), ("row", off)


def _mm_resident(name, As, Bs, dots, epi, extras, out_dtypes, m, n, tn):
    def unique(arrays):
        seen = []
        for a in arrays:
            if not any(a is s for s in seen):
                seen.append(a)
        return seen

    rows = unique([a for a, _, _ in As] + [e for e, _, where in extras if where[0] == "tile"])
    weights = unique([b for b, _, _ in Bs] + [e for e, _, where in extras if where[0] == "row"])

    def ref_of(arr, row_refs, weight_refs):
        for r, ref in zip(rows, row_refs):
            if r is arr:
                return ref
        for wt, ref in zip(weights, weight_refs):
            if wt is arr:
                return ref

    def compute(row_refs, weight_refs, out_refs):
        a_vals = [ref_of(a, row_refs, weight_refs)[:, ks].astype(BF16) for a, _, ks in As]
        for j in range(n // tn):
            accs = []
            for ai, bi in dots:
                b, _, where = Bs[bi]
                b_ref = ref_of(b, row_refs, weight_refs)
                if where[0] == "cols":
                    accs.append(_dot(a_vals[ai], b_ref[:, (j + where[1]) * tn:(j + where[1] + 1) * tn]))
                else:
                    accs.append(_dot_nt(a_vals[ai], b_ref[j * tn:(j + 1) * tn, where[1]]))
            ex = [ref_of(e, row_refs, weight_refs)[:, (j + where[1]) * tn:(j + where[1] + 1) * tn]
                  for e, _, where in extras]
            for o_ref, o in zip(out_refs, epi(accs, ex)):
                o_ref[:, j * tn:(j + 1) * tn] = o.astype(o_ref.dtype)

    return _rows_call(name, rows, weights, [(n, dt) for dt in out_dtypes], compute, min(FFN_TM, m))


def _mm(name, As, Bs, dots, epi, extras, out_dtypes, m, n, tm, tn, trans_b=False, side=None):
    if side is None:
        return _mm_resident(name, As, Bs, dots, epi, extras, out_dtypes, m, n, tn)
    na, nb, ne, no = len(As), len(Bs), len(extras), len(out_dtypes)
    ni, nj = m // tm, n // tn
    s_in = len(side.inputs) if side else 0
    s_out = len(side.out_shapes) if side else 0

    def body(*refs):
        a_refs = refs[:na]
        b_refs = refs[na:na + nb]
        e_refs = refs[na + nb:na + nb + ne]
        at = na + nb + ne
        side_refs = refs[at:at + s_in]
        o_refs = refs[at + s_in:at + s_in + no]
        side_refs = list(side_refs) + list(refs[at + s_in + no:])
        if side:
            i, j = pl.program_id(0), pl.program_id(1)

            @pl.when(jnp.logical_and(i == 0, j == 0))
            def _():
                side.start(*side_refs)

        a_vals = [r[...].astype(BF16) for r in a_refs]
        accs = []
        for ai, bi in dots:
            b = b_refs[bi][...]
            accs.append(_dot_nt(a_vals[ai], b) if trans_b else _dot(a_vals[ai], b))
        outs = epi(accs, [r[...] for r in e_refs])
        for o_ref, o in zip(o_refs, outs):
            o_ref[...] = o.astype(o_ref.dtype)
        if side:
            @pl.when(jnp.logical_and(i == ni - 1, j == nj - 1))
            def _():
                side.finish(*side_refs)

    ops = list(As) + list(Bs) + list(extras)
    anywhere = pl.BlockSpec(memory_space=pl.ANY)
    res = _call(
        body, name=name,
        grid=(ni, nj),
        in_specs=[op[1] for op in ops] + [anywhere] * s_in,
        out_specs=[pl.BlockSpec((tm, tn), lambda i, j: (i, j)) for _ in out_dtypes] + [anywhere] * s_out,
        out_shape=[jax.ShapeDtypeStruct((m, n), dt) for dt in out_dtypes] + (list(side.out_shapes) if side else []),
        scratch_shapes=list(side.scratch) if side else [],
        compiler_params=_cp(),
    )(*[op[0] for op in ops], *(side.inputs if side else []))
    return res


def _rows_call(name, rows, weights, outs, compute, tm, side=None, sums=(), vmem_mb=VMEM_MB):
    t = rows[0].shape[0]
    nr, nw, no = len(rows), len(weights), len(outs) + len(sums)
    ni = t // tm
    s_in = len(side.inputs) if side else 0
    s_out = len(side.out_shapes) if side else 0

    def body(*refs):
        at = nr + nw
        side_refs = list(refs[at:at + s_in]) + list(refs[at + s_in + no:])
        if side:
            @pl.when(pl.program_id(0) == 0)
            def _():
                side.start(*side_refs)

        out_refs = refs[at + s_in:at + s_in + no]
        if sums:
            @pl.when(pl.program_id(0) == 0)
            def _():
                for r in out_refs[len(outs):]:
                    r[...] = jnp.zeros_like(r)

        compute(refs[:nr], refs[nr:at], out_refs)
        if side:
            @pl.when(pl.program_id(0) == ni - 1)
            def _():
                side.finish(*side_refs)

    anywhere = pl.BlockSpec(memory_space=pl.ANY)
    return _call(
        body, name=name, grid=(ni,),
        in_specs=[pl.BlockSpec((tm, r.shape[1]), lambda i: (i, 0)) for r in rows]
        + [pl.BlockSpec(wt.shape, lambda i, nd=wt.ndim: (0,) * nd) for wt in weights] + [anywhere] * s_in,
        out_specs=[pl.BlockSpec((tm, width), lambda i: (i, 0)) for width, _ in outs]
        + [pl.BlockSpec((1, width), lambda i: (0, 0)) for width in sums] + [anywhere] * s_out,
        out_shape=[jax.ShapeDtypeStruct((t, width), dt) for width, dt in outs]
        + [jax.ShapeDtypeStruct((1, width), F32) for width in sums] + (list(side.out_shapes) if side else []),
        scratch_shapes=list(side.scratch) if side else [],
        compiler_params=_cp(vmem_mb),
    )(*rows, *weights, *(side.inputs if side else []))


def _mm_tn(name, a, b, scale=1.0, tm=1024, tn=1024, stacked=None, into=None):
    t, m = a.shape
    n = b.shape[1]
    tm, tn, tt = min(tm, m), min(tn, n), min(TT, t)
    nk = t // tt

    def body(a_ref, b_ref, *rest):
        o_ref = rest[-1]
        k = pl.program_id(2)

        @pl.when(k == 0)
        def _():
            o_ref[...] = jnp.zeros_like(o_ref)

        o_ref[...] += _dot_tn(a_ref[...].astype(BF16), b_ref[...].astype(BF16))
        if scale != 1.0:
            @pl.when(k == nk - 1)
            def _():
                o_ref[...] = o_ref[...] * scale

    return _call(
        body, name=name,
        grid=(m // tm, n // tn, nk),
        in_specs=[pl.BlockSpec((tt, tm), lambda i, j, k: (k, i)), pl.BlockSpec((tt, tn), lambda i, j, k: (k, j))]
        + ([pl.BlockSpec(memory_space=pl.ANY)] if into is not None else []),
        out_specs=(pl.BlockSpec((None, tm, tn), lambda i, j, k: (stacked[1] + j, i, 0)) if stacked
                   else pl.BlockSpec((tm, tn), lambda i, j, k: (i, j))),
        out_shape=jax.ShapeDtypeStruct((stacked[0], m, tn) if stacked else (m, n), F32),
        input_output_aliases={2: 0} if into is not None else {},
        compiler_params=_cp(),
    )(a, b, *([into] if into is not None else []))


def _rms_bwd_vals(xv, g, dn):
    r = lax.rsqrt(jnp.mean(xv * xv, axis=-1, keepdims=True) + EPS)
    xh = xv * r
    dxh = dn * g
    c = jnp.mean(dxh * xh, axis=-1, keepdims=True)
    return r * (dxh - xh * c), dn * xh


def _rms_bwd(name, x, gain, dn, dres):
    t, d = x.shape
    tm = min(ROW_TM, t)

    def body(x_ref, g_ref, dn_ref, dr_ref, dx_ref, dg_ref):
        @pl.when(pl.program_id(0) == 0)
        def _():
            dg_ref[...] = jnp.zeros_like(dg_ref)

        dx, dg = _rms_bwd_vals(x_ref[...], g_ref[...], dn_ref[...].astype(F32))
        dx_ref[...] = dr_ref[...] + dx
        dg_ref[...] += jnp.sum(dg, axis=0, keepdims=True)

    row = pl.BlockSpec((tm, d), lambda i: (i, 0))
    one = pl.BlockSpec((1, d), lambda i: (0, 0))
    return _call(
        body, name=name, grid=(t // tm,),
        in_specs=[row, one, row, row],
        out_specs=[row, one],
        out_shape=[jax.ShapeDtypeStruct((t, d), F32), jax.ShapeDtypeStruct((1, d), F32)],
        compiler_params=_cp(),
    )(x, gain, dn, dres)


def _colsum(name, x):
    t, n = x.shape
    tm = min(TM, t)

    def body(x_ref, o_ref):
        @pl.when(pl.program_id(0) == 0)
        def _():
            o_ref[...] = jnp.zeros_like(o_ref)

        o_ref[...] += jnp.sum(x_ref[...].astype(F32), axis=0, keepdims=True)

    return _call(
        body, name=name, grid=(t // tm,),
        in_specs=[pl.BlockSpec((tm, n), lambda i: (i, 0))],
        out_specs=pl.BlockSpec((1, n), lambda i: (0, 0)),
        out_shape=jax.ShapeDtypeStruct((1, n), F32),
        compiler_params=_cp(),
    )(x)


def _in_mla(u, w_in_mla, gq, gkv):
    t = u.shape[0]

    def compute(rows, weights, outs):
        p = _dot(rows[0][...], weights[0][...])
        outs[0][...] = p
        cq = p[:, 0:Q_LORA]
        ckv = p[:, Q_LORA:Q_LORA + KV_LORA]
        rq = lax.rsqrt(jnp.mean(cq * cq, axis=-1, keepdims=True) + EPS)
        rkv = lax.rsqrt(jnp.mean(ckv * ckv, axis=-1, keepdims=True) + EPS)
        outs[1][...] = (cq * rq * weights[1][...]).astype(BF16)
        outs[2][...] = (ckv * rkv * weights[2][...]).astype(BF16)

    return _rows_call("in_mla", [u], [w_in_mla, gq, gkv], [(MLA_COLS, F32), (Q_LORA, BF16), (KV_LORA, BF16)], compute,
                      min(FFN_TM, t))


def _lora_norm_bwd(p_mla, gq, gkv, dcqn, dckvn, dkpe):
    t = p_mla.shape[0]
    tm = min(ROW_TM, t)

    def body(p_ref, gq_ref, gkv_ref, dq_ref, dkv_ref, dkpe_ref, dp_ref, dgq_ref, dgkv_ref):
        @pl.when(pl.program_id(0) == 0)
        def _():
            dgq_ref[...] = jnp.zeros_like(dgq_ref)
            dgkv_ref[...] = jnp.zeros_like(dgkv_ref)

        dcq, dgq = _rms_bwd_vals(p_ref[:, 0:Q_LORA], gq_ref[...], dq_ref[...])
        dckv, dgkv = _rms_bwd_vals(p_ref[:, Q_LORA:Q_LORA + KV_LORA], gkv_ref[...], dkv_ref[...])
        dp_ref[:, 0:Q_LORA] = dcq.astype(BF16)
        dp_ref[:, Q_LORA:Q_LORA + KV_LORA] = dckv.astype(BF16)
        dp_ref[:, Q_LORA + KV_LORA:MLA_COLS] = dkpe_ref[...].astype(BF16)
        dgq_ref[...] += jnp.sum(dgq, axis=0, keepdims=True)
        dgkv_ref[...] += jnp.sum(dgkv, axis=0, keepdims=True)

    return _call(
        body, name="lora_norm_bwd", grid=(t // tm,),
        in_specs=[pl.BlockSpec((tm, MLA_COLS), lambda i: (i, 0)),
                  pl.BlockSpec((1, Q_LORA), lambda i: (0, 0)), pl.BlockSpec((1, KV_LORA), lambda i: (0, 0)),
                  pl.BlockSpec((tm, Q_LORA), lambda i: (i, 0)), pl.BlockSpec((tm, KV_LORA), lambda i: (i, 0)),
                  pl.BlockSpec((tm, HK), lambda i: (i, 0))],
        out_specs=[pl.BlockSpec((tm, MLA_COLS), lambda i: (i, 0)),
                   pl.BlockSpec((1, Q_LORA), lambda i: (0, 0)), pl.BlockSpec((1, KV_LORA), lambda i: (0, 0))],
        out_shape=[jax.ShapeDtypeStruct((t, MLA_COLS), BF16), jax.ShapeDtypeStruct((1, Q_LORA), F32),
                   jax.ShapeDtypeStruct((1, KV_LORA), F32)],
        compiler_params=_cp(),
    )(p_mla, gq, gkv, dcqn, dckvn, dkpe)


def _cumsum_rows(x, row):
    for s in (1, 2, 4, 8, 16, 32):
        x = x + jnp.where(row >= s, pltpu.roll(x, s, 0), 0.0)
    return x


def _rcumsum_rows(x, row):
    for s in (1, 2, 4, 8, 16, 32):
        x = x + jnp.where(row < CHUNK - s, pltpu.roll(x, CHUNK - s, 0), 0.0)
    return x


def _hg_gates(qr, z, lb, row):
    q = _silu(qr)
    sg = _sig(z)
    f = lb + (1.0 - lb) * sg
    lf = jnp.log(f)
    k = (1.0 - lb) * (1.0 - sg)
    cum = _cumsum_rows(lf, row)
    mid = jnp.sum(jnp.where(row < CHUNK // 2, lf, 0.0), axis=0, keepdims=True)
    last = jnp.sum(lf, axis=0, keepdims=True)
    e_q = jnp.exp(jnp.minimum(cum - mid, EXP_CLAMP))
    e_k = jnp.exp(jnp.minimum(mid - cum, EXP_CLAMP))
    e_a = jnp.exp(cum)
    e_l = jnp.exp(last - cum)
    return q, sg, f, k, last, e_q, e_k, e_a, e_l


def _hgrn_fwd(p_hg, tab, gain):
    t = p_hg.shape[0]
    bt = min(HG_BT, t)
    nb, nc = t // bt, bt // CHUNK

    hpb = HG_HPB
    wide = hpb * HK

    def body(q_ref, f_ref, i_ref, g_ref, tab_ref, gain_ref, o_ref, ho_ref, st_ref, state):
        @pl.when(pl.program_id(1) == 0)
        def _():
            state[...] = jnp.zeros_like(state)

        row = lax.broadcasted_iota(jnp.int32, (CHUNK, HK), 0)
        tril = lax.broadcasted_iota(jnp.int32, (CHUNK, CHUNK), 0) >= lax.broadcasted_iota(jnp.int32, (CHUNK, CHUNK), 1)
        gain_v = gain_ref[...]

        def chunk(c, carry):
            sl = pl.ds(pl.multiple_of(c * CHUNK, CHUNK), CHUNK)
            for hh in range(hpb):
                ln = slice(hh * HK, (hh + 1) * HK)
                lb = _sig(tab_ref[0:1, ln] - tab_ref[1:2, ln])
                v = i_ref[sl, ln].astype(BF16)
                q, _, _, k, last, e_q, e_k, e_a, e_l = _hg_gates(q_ref[sl, ln], f_ref[sl, ln], lb, row)
                st = state[hh]
                st_ref[hh, c] = st
                p = jnp.where(tril, _dot_nt((q * e_q).astype(BF16), (k * e_k).astype(BF16)), 0.0)
                o = _dot(p.astype(BF16), v) + _dot_nt((q * e_a).astype(BF16), st.astype(BF16))
                state[hh] = jnp.exp(last) * st + _dot_tn(v, (k * e_l).astype(BF16))
                o_ref[sl, ln] = o
                r = lax.rsqrt(jnp.mean(o * o, axis=-1, keepdims=True) + EPS)
                ho_ref[sl, ln] = (o * r * gain_v * _silu(g_ref[sl, ln])).astype(BF16)
            return carry

        lax.fori_loop(0, nc, chunk, 0)

    def col(k):
        return pl.BlockSpec((bt, wide), lambda h, j, k=k: (j, k * (HEADS // hpb) + h))

    return _call(
        body, name="hgrn_fwd", grid=(HEADS // hpb, nb),
        in_specs=[col(0), col(1), col(2), col(3),
                  pl.BlockSpec((2, wide), lambda h, j: (0, h)), pl.BlockSpec((1, HK), lambda h, j: (0, 0))],
        out_specs=[pl.BlockSpec((bt, wide), lambda h, j: (j, h)), pl.BlockSpec((bt, wide), lambda h, j: (j, h)),
                   pl.BlockSpec((hpb, nc, HK, HK), lambda h, j: (h, j, 0, 0))],
        out_shape=[jax.ShapeDtypeStruct((t, D), F32), jax.ShapeDtypeStruct((t, D), BF16),
                   jax.ShapeDtypeStruct((HEADS, t // CHUNK, HK, HK), F32)],
        scratch_shapes=[pltpu.VMEM((hpb, HK, HK), F32)],
        compiler_params=_cp(),
    )(p_hg, p_hg, p_hg, p_hg, tab, gain)


def _hgrn_bwd(p_hg, tab, gain, o_raw, states, dho):
    t = p_hg.shape[0]
    bt = min(HG_BT, t)
    nb, nc = t // bt, bt // CHUNK
    hpb = HG_HPB
    wide = hpb * HK

    def body(q_ref, f_ref, i_ref, g_ref, tab_ref, gain_ref, o_ref, st_ref, dho_ref,
             dq_ref, df_ref, di_ref, dg_ref, dtab_ref, dgain_ref, dstate, dlb):
        h, j = pl.program_id(0), pl.program_id(1)

        @pl.when(jnp.logical_and(h == 0, j == 0))
        def _():
            dgain_ref[...] = jnp.zeros_like(dgain_ref)

        @pl.when(j == 0)
        def _():
            dstate[...] = jnp.zeros_like(dstate)
            dlb[...] = jnp.zeros_like(dlb)

        row = lax.broadcasted_iota(jnp.int32, (CHUNK, HK), 0)
        tril = lax.broadcasted_iota(jnp.int32, (CHUNK, CHUNK), 0) >= lax.broadcasted_iota(jnp.int32, (CHUNK, CHUNK), 1)
        gain_v = gain_ref[...]

        def chunk(cc, carry):
            c = nc - 1 - cc
            sl = pl.ds(pl.multiple_of(c * CHUNK, CHUNK), CHUNK)
            dgain = jnp.zeros((1, HK), F32)
            for hh in range(hpb):
                ln = slice(hh * HK, (hh + 1) * HK)
                lb = _sig(tab_ref[0:1, ln] - tab_ref[1:2, ln])
                qr = q_ref[sl, ln]
                v = i_ref[sl, ln].astype(BF16)
                gr = g_ref[sl, ln]
                q, sg, f, k, last, e_q, e_k, e_a, e_l = _hg_gates(qr, f_ref[sl, ln], lb, row)
                o = o_ref[sl, ln]
                r = lax.rsqrt(jnp.mean(o * o, axis=-1, keepdims=True) + EPS)
                oh = o * r
                dh = dho_ref[sl, ln].astype(F32)
                dnorm = dh * _silu(gr)
                dg_ref[sl, ln] = (dh * oh * gain_v * _dsilu(gr)).astype(BF16)
                dgain = dgain + jnp.sum(dnorm * oh, axis=0, keepdims=True)
                dxh = dnorm * gain_v
                do = (r * (dxh - oh * jnp.mean(dxh * oh, axis=-1, keepdims=True))).astype(BF16)
                st0 = st_ref[hh, c]
                st0_b = st0.astype(BF16)
                ds1 = dstate[hh]
                ds1_b = ds1.astype(BF16)
                qt = (q * e_q).astype(BF16)
                kt = (k * e_k).astype(BF16)
                qd = (q * e_a).astype(BF16)
                kd = (k * e_l).astype(BF16)
                p = jnp.where(tril, _dot_nt(qt, kt), 0.0).astype(BF16)
                dp = jnp.where(tril, _dot_nt(do, v), 0.0).astype(BF16)
                dv = _dot_tn(p, do) + _dot_nt(kd, ds1_b)
                dqt = _dot(dp, kt)
                dkt = _dot_tn(dp, qt)
                dq_inter = _dot(do, st0_b) * e_a
                dk_inter = _dot(v, ds1_b) * e_l
                dq = dqt * e_q + dq_inter
                dk = dkt * e_k + dk_inter
                e_last = jnp.exp(last)
                dstate[hh] = _dot_tn(do, qd) + e_last * ds1
                dlast = (jnp.sum(k * dk_inter, axis=0, keepdims=True)
                         + e_last * jnp.sum(ds1 * st0, axis=0, keepdims=True))
                da = (qt.astype(F32) * dqt - kt.astype(F32) * dkt + q * dq_inter - k * dk_inter
                      + jnp.where(row == CHUNK - 1, dlast, 0.0))
                dlf = _rcumsum_rows(da, row)
                dfv = dlf / f - dk
                df_ref[sl, ln] = (dfv * (1.0 - lb) * sg * (1.0 - sg)).astype(BF16)
                dlb[:, ln] += jnp.sum(dfv * (1.0 - sg), axis=0, keepdims=True)
                dq_ref[sl, ln] = (dq * _dsilu(qr)).astype(BF16)
                di_ref[sl, ln] = dv.astype(BF16)
            dgain_ref[...] += dgain
            return carry

        lax.fori_loop(0, nc, chunk, 0)

        @pl.when(j == nb - 1)
        def _():
            lb = _sig(tab_ref[0:1, :] - tab_ref[1:2, :])
            d0 = dlb[...] * lb * (1.0 - lb)
            dtab_ref[0:1, :] = d0
            dtab_ref[1:2, :] = -d0

    def col(k):
        return pl.BlockSpec((bt, wide), lambda h, j, k=k: (nb - 1 - j, k * (HEADS // hpb) + h))

    tok = pl.BlockSpec((bt, wide), lambda h, j: (nb - 1 - j, h))
    return _call(
        body, name="hgrn_bwd", grid=(HEADS // hpb, nb),
        in_specs=[col(0), col(1), col(2), col(3),
                  pl.BlockSpec((2, wide), lambda h, j: (0, h)), pl.BlockSpec((1, HK), lambda h, j: (0, 0)),
                  tok, pl.BlockSpec((hpb, nc, HK, HK), lambda h, j: (h, nb - 1 - j, 0, 0)), tok],
        out_specs=[tok, tok, tok, tok,
                   pl.BlockSpec((2, wide), lambda h, j: (0, h)), pl.BlockSpec((1, HK), lambda h, j: (0, 0))],
        out_shape=[jax.ShapeDtypeStruct((t, D), BF16)] * 4
        + [jax.ShapeDtypeStruct((2, D), F32), jax.ShapeDtypeStruct((1, HK), F32)],
        scratch_shapes=[pltpu.VMEM((hpb, HK, HK), F32), pltpu.VMEM((1, wide), F32)],
        compiler_params=_cp(),
    )(p_hg, p_hg, p_hg, p_hg, tab, gain, o_raw, states, dho)


def _rope_tables(pos):
    t = pos.shape[0]
    tm = min(ROW_TM, t)
    inv = np.zeros((1, HK), np.float32)
    freq = (ROPE_THETA ** (-np.arange(0, ROPE, 2, dtype=np.float32) / ROPE)).astype(np.float32)
    inv[0, 0:ROPE // 2] = freq
    inv[0, ROPE // 2:ROPE] = freq
    sign = np.zeros((1, HK), np.float32)
    sign[0, 0:ROPE // 2] = -1.0
    sign[0, ROPE // 2:ROPE] = 1.0

    def body(pos_ref, inv_ref, sign_ref, cos_ref, sin_ref):
        ang = pos_ref[...].astype(F32) * inv_ref[...]
        cos_ref[...] = jnp.cos(ang)
        sin_ref[...] = jnp.sin(ang) * sign_ref[...]

    one = pl.BlockSpec((1, HK), lambda i: (0, 0))
    row = pl.BlockSpec((tm, HK), lambda i: (i, 0))
    return _call(
        body, name="rope_tables", grid=(t // tm,),
        in_specs=[pl.BlockSpec((tm, 1), lambda i: (i, 0)), one, one],
        out_specs=[row, row],
        out_shape=[jax.ShapeDtypeStruct((t, HK), F32)] * 2,
        compiler_params=_cp(),
    )(pos, jnp.asarray(inv), jnp.asarray(sign))


def _rope(x, cos, sin_signed):
    r = lax.broadcasted_iota(jnp.int32, (HK, HK), 0)
    c = lax.broadcasted_iota(jnp.int32, (HK, HK), 1)
    half = ROPE // 2
    swap = jnp.logical_or(jnp.logical_and(c < half, r == c + half),
                          jnp.logical_and(jnp.logical_and(c >= half, c < ROPE), r == c - half))
    return x * cos + _dot_split(x, swap.astype(BF16)) * sin_signed


def _dot_split(x, m):
    hi = x.astype(BF16)
    lo = (x - hi.astype(F32)).astype(BF16)
    return _dot(hi, m) + _dot(lo, m)


def _lane_sum(x):
    return _dot_split(x, jnp.ones((HK, HK), BF16))


def _head_norm(xn, xr):
    r = lax.rsqrt(_lane_sum(xn * xn + xr * xr) * (1.0 / QK) + EPS)
    return xn * r, xr * r, r


def _head_norm_bwd(xn, xr, g_n, g_r, dn, dr):
    hn, hr, r = _head_norm(xn, xr)
    dxn, dxr = dn * g_n, dr * g_r
    c = _lane_sum(dxn * hn + dxr * hr) * (1.0 / QK)
    return r * (dxn - hn * c), r * (dxr - hr * c), dn * hn, dr * hr


def _mla_prep_fwd(qf, kv, p_mla, cos, sin, gq, gk):
    t = qf.shape[0]
    tm = min(ROW_TM, t)

    def body(qf_ref, kv_ref, kpe_ref, cos_ref, sin_ref, gq_ref, gk_ref, q_ref, k_ref, v_ref):
        cos_v, sin_v = cos_ref[...], sin_ref[...]
        kpe = kpe_ref[...]
        for h in range(HEADS):
            lo, mid, hi = h * QKP, h * QKP + HK, (h + 1) * QKP
            qn, qr, _ = _head_norm(qf_ref[:, lo:mid], qf_ref[:, mid:hi])
            q_ref[h, :, 0:HK] = (qn * gq_ref[:, 0:HK] * (SCALE * LOG2E)).astype(BF16)
            q_ref[h, :, HK:QKP] = (_rope(qr * gq_ref[:, HK:QKP], cos_v, sin_v) * (SCALE * LOG2E)).astype(BF16)
            kn, kr, _ = _head_norm(kv_ref[:, lo:mid], kpe)
            k_ref[h, :, 0:HK] = (kn * gk_ref[:, 0:HK]).astype(BF16)
            k_ref[h, :, HK:QKP] = _rope(kr * gk_ref[:, HK:QKP], cos_v, sin_v).astype(BF16)
            v_ref[h, :, 0:HK] = kv_ref[:, mid:hi].astype(BF16)
            v_ref[h, :, HK:QKP] = jnp.full((tm, HK), -1.0, BF16)

    head = pl.BlockSpec((tm, HEADS * QKP), lambda i: (i, 0))
    tok = pl.BlockSpec((tm, HK), lambda i: (i, 0))
    gain = pl.BlockSpec((1, QKP), lambda i: (0, 0))
    return _call(
        body, name="mla_prep_fwd", grid=(t // tm,),
        in_specs=[head, head, pl.BlockSpec((tm, HK), lambda i: (i, MLA_COLS // HK - 1)), tok, tok, gain, gain],
        out_specs=[pl.BlockSpec((HEADS, tm, QKP), lambda i: (0, i, 0)),
                   pl.BlockSpec((HEADS, tm, QKP), lambda i: (0, i, 0)),
                   pl.BlockSpec((HEADS, tm, QKP), lambda i: (0, i, 0))],
        out_shape=[jax.ShapeDtypeStruct((HEADS, t, QKP), BF16), jax.ShapeDtypeStruct((HEADS, t, QKP), BF16),
                   jax.ShapeDtypeStruct((HEADS, t, QKP), BF16)],
        compiler_params=_cp(),
    )(qf, kv, p_mla, cos, sin, gq, gk)


def _mla_prep_bwd(qf, kv, p_mla, cos, sin, gq, gk, dq, dk, dv):
    t = qf.shape[0]
    tm = min(ROW_TM, t)

    def body(qf_ref, kv_ref, kpe_ref, cos_ref, sin_ref, gq_ref, gk_ref, dq_ref, dk_ref, dv_ref,
             dqf_ref, dkv_ref, dkpe_ref, dgq_ref, dgk_ref):
        @pl.when(pl.program_id(0) == 0)
        def _():
            dgq_ref[...] = jnp.zeros_like(dgq_ref)
            dgk_ref[...] = jnp.zeros_like(dgk_ref)

        cos_v, sin_v = cos_ref[...], -sin_ref[...]
        kpe = kpe_ref[...]
        gqn, gqr, gkn, gkr = gq_ref[:, 0:HK], gq_ref[:, HK:QKP], gk_ref[:, 0:HK], gk_ref[:, HK:QKP]
        dkpe = jnp.zeros((tm, HK), F32)
        dgq_n, dgq_r, dgk_n, dgk_r = [jnp.zeros((1, HK), F32) for _ in range(4)]
        for h in range(HEADS):
            lo, mid, hi = h * QKP, h * QKP + HK, (h + 1) * QKP
            dqn = dq_ref[h, :, 0:HK].astype(F32) * SCALE
            dqr = _rope(dq_ref[h, :, HK:QKP].astype(F32), cos_v, sin_v) * SCALE
            a, b, ga, gb = _head_norm_bwd(qf_ref[:, lo:mid], qf_ref[:, mid:hi], gqn, gqr, dqn, dqr)
            dqf_ref[:, lo:mid] = a.astype(BF16)
            dqf_ref[:, mid:hi] = b.astype(BF16)
            dgq_n = dgq_n + jnp.sum(ga, axis=0, keepdims=True)
            dgq_r = dgq_r + jnp.sum(gb, axis=0, keepdims=True)
            dkn = dk_ref[h, :, 0:HK].astype(F32) * LN2
            dkr = _rope(dk_ref[h, :, HK:QKP].astype(F32), cos_v, sin_v) * LN2
            a, b, ga, gb = _head_norm_bwd(kv_ref[:, lo:mid], kpe, gkn, gkr, dkn, dkr)
            dkv_ref[:, lo:mid] = a.astype(BF16)
            dkv_ref[:, mid:hi] = dv_ref[h].astype(BF16)
            dkpe = dkpe + b
            dgk_n = dgk_n + jnp.sum(ga, axis=0, keepdims=True)
            dgk_r = dgk_r + jnp.sum(gb, axis=0, keepdims=True)
        dkpe_ref[...] = dkpe
        dgq_ref[:, 0:HK] += dgq_n
        dgq_ref[:, HK:QKP] += dgq_r
        dgk_ref[:, 0:HK] += dgk_n
        dgk_ref[:, HK:QKP] += dgk_r

    head = pl.BlockSpec((tm, HEADS * QKP), lambda i: (i, 0))
    tok = pl.BlockSpec((tm, HK), lambda i: (i, 0))
    gain = pl.BlockSpec((1, QKP), lambda i: (0, 0))
    hq = pl.BlockSpec((HEADS, tm, QKP), lambda i: (0, i, 0))
    return _call(
        body, name="mla_prep_bwd", grid=(t // tm,),
        in_specs=[head, head, pl.BlockSpec((tm, HK), lambda i: (i, MLA_COLS // HK - 1)), tok, tok, gain, gain,
                  hq, hq, pl.BlockSpec((HEADS, tm, HK), lambda i: (0, i, 0))],
        out_specs=[head, head, tok, gain, gain],
        out_shape=[jax.ShapeDtypeStruct((t, HEADS * QKP), BF16), jax.ShapeDtypeStruct((t, HEADS * QKP), BF16),
                   jax.ShapeDtypeStruct((t, HK), F32), jax.ShapeDtypeStruct((1, QKP), F32),
                   jax.ShapeDtypeStruct((1, QKP), F32)],
        compiler_params=_cp(),
    )(qf, kv, p_mla, cos, sin, gq, gk, dq, dk, dv)


def _chunk_mask(row0, rows, cols):
    r = lax.broadcasted_iota(jnp.int32, (rows, cols), 0) + row0
    c = lax.broadcasted_iota(jnp.int32, (rows, cols), 1)
    return jnp.right_shift(r, 6) >= jnp.right_shift(c, 6)


def _flash_fwd(q, k, v, side=None):
    t = q.shape[1]
    tq = min(TQ, t)
    nq = t // tq
    sub = min(SUBQ, tq)
    pairs = [(i, j) for i in range(nq) for j in range(i + 1)]
    qi = jnp.asarray([p[0] for p in pairs], jnp.int32)
    kj = jnp.asarray([p[1] for p in pairs], jnp.int32)
    s_in = len(side.inputs) if side else 0
    s_out = len(side.out_shapes) if side else 0

    def body(qi_ref, kj_ref, q_ref, k_ref, v_ref, *rest):
        o_ref, lse_ref = rest[s_in:s_in + 2]
        m_s, acc_s = rest[s_in + 2 + s_out:s_in + 4 + s_out]
        side_refs = list(rest[:s_in]) + list(rest[s_in + 2:s_in + 2 + s_out]) + list(rest[s_in + 4 + s_out:])
        n = pl.program_id(1)
        i, j = qi_ref[n], kj_ref[n]
        if side:
            @pl.when(jnp.logical_and(pl.program_id(0) == 0, n == 0))
            def _():
                side.start(*side_refs)

        @pl.when(j == 0)
        def _():
            m_s[...] = jnp.full_like(m_s, NEG)
            acc_s[...] = jnp.zeros_like(acc_s)

        def step(diag):
            subs = range(tq // sub)
            width = [(r + 1) * sub if diag else tq for r in subs]
            logits = [_dot_nt(q_ref[r * sub:(r + 1) * sub, :], k_ref[0:width[r], :]) for r in subs]
            for r in subs:
                rows = slice(r * sub, (r + 1) * sub)
                cols = width[r]
                s = logits[r]
                if diag:
                    s = jnp.where(_chunk_mask(r * sub, sub, cols), s, NEG)
                m_old = m_s[rows, :]
                m_new = jnp.maximum(m_old, jnp.max(s, axis=-1, keepdims=True))
                alpha = jnp.exp2(m_old - m_new)
                p = jnp.exp2((s - jnp.tile(m_new, (1, cols // HK))).astype(BF16))
                acc_s[rows, :] = jnp.tile(alpha, (1, 2)) * acc_s[rows, :] + _dot(p, v_ref[0:cols, :])
                m_s[rows, :] = m_new

        @pl.when(j < i)
        def _():
            step(False)

        @pl.when(j == i)
        def _():
            step(True)
            l = -acc_s[:, HK:QKP]
            o_ref[...] = (acc_s[:, 0:HK] / l).astype(BF16)
            lse_ref[...] = m_s[...] + jnp.log(l) * LOG2E

        if side:
            @pl.when(jnp.logical_and(pl.program_id(0) == HEADS - 1, n == len(pairs) - 1))
            def _():
                side.finish(*side_refs)

    anywhere = pl.BlockSpec(memory_space=pl.ANY)
    grid_spec = pltpu.PrefetchScalarGridSpec(
        num_scalar_prefetch=2, grid=(HEADS, len(pairs)),
        in_specs=[pl.BlockSpec((None, tq, QKP), lambda h, n, qi, kj: (h, qi[n], 0)),
                  pl.BlockSpec((None, tq, QKP), lambda h, n, qi, kj: (h, kj[n], 0)),
                  pl.BlockSpec((None, tq, QKP), lambda h, n, qi, kj: (h, kj[n], 0))] + [anywhere] * s_in,
        out_specs=[pl.BlockSpec((tq, HK), lambda h, n, qi, kj: (qi[n], h)),
                   pl.BlockSpec((None, tq, HK), lambda h, n, qi, kj: (h, qi[n], 0))] + [anywhere] * s_out,
        scratch_shapes=[pltpu.VMEM((tq, HK), F32), pltpu.VMEM((tq, QKP), F32)] + (list(side.scratch) if side else []),
    )
    return _call(
        body, name="flash_fwd", grid_spec=grid_spec,
        out_shape=[jax.ShapeDtypeStruct((t, D), BF16), jax.ShapeDtypeStruct((HEADS, t, HK), F32)]
        + (list(side.out_shapes) if side else []),
        compiler_params=_cp(),
    )(qi, kj, q, k, v, *(side.inputs if side else []))


def _attn_do(do, o):
    t = do.shape[0]
    tm = min(TM, t)

    def body(do_ref, o_ref, d_ref):
        lane = lax.broadcasted_iota(jnp.int32, (tm, HK), 1)
        for h in range(HEADS):
            ln = slice(h * HK, (h + 1) * HK)
            dov = do_ref[:, ln]
            d = jnp.sum(dov.astype(F32) * o_ref[:, ln].astype(F32), axis=-1, keepdims=True)
            hi = d.astype(BF16).astype(F32)
            d_ref[h, :, 0:HK] = dov
            d_ref[h, :, HK:QKP] = jnp.where(lane == 0, hi, jnp.where(lane == 1, d - hi, 0.0)).astype(BF16)

    blk = pl.BlockSpec((tm, D), lambda i: (i, 0))
    return _call(
        body, name="attn_do", grid=(t // tm,),
        in_specs=[blk, blk],
        out_specs=pl.BlockSpec((HEADS, tm, QKP), lambda i: (0, i, 0)),
        out_shape=jax.ShapeDtypeStruct((HEADS, t, QKP), BF16),
        compiler_params=_cp(),
    )(do, o)


def _flash_bwd(q, k, v, lse, do):
    t = q.shape[1]
    tq = min(TQ, t)
    nq = t // tq
    sub = min(SUBQ, tq)
    pairs = [(i, j) for j in range(nq) for i in range(j, nq)]
    qi = jnp.asarray([p[0] for p in pairs], jnp.int32)
    kj = jnp.asarray([p[1] for p in pairs], jnp.int32)
    npairs = len(pairs)

    def body(qi_ref, kj_ref, q_ref, k_ref, v_ref, lse_ref, do_ref, dq_ref, dk_ref, dv_ref):
        n = pl.program_id(1)
        i, j = qi_ref[n], kj_ref[n]

        @pl.when(n == 0)
        def _():
            dq_ref[...] = jnp.zeros_like(dq_ref)

        @pl.when(i == j)
        def _():
            dk_ref[...] = jnp.zeros_like(dk_ref)
            dv_ref[...] = jnp.zeros_like(dv_ref)

        def step(diag):
            for r in range(tq // sub):
                rows = slice(r * sub, (r + 1) * sub)
                cols = (r + 1) * sub if diag else tq
                qv, kv_ = q_ref[rows, :], k_ref[0:cols, :]
                p = jnp.exp2(_dot_nt(qv, kv_) - jnp.tile(lse_ref[rows, :], (1, cols // HK)))
                if diag:
                    p = jnp.where(_chunk_mask(r * sub, sub, cols), p, 0.0)
                dp_less_delta = _dot_nt(do_ref[rows, :], v_ref[0:cols, :])
                ds = (p * dp_less_delta).astype(BF16)
                dv_ref[0:cols, :] += _dot_tn(p.astype(BF16), do_ref[rows, 0:HK])
                dk_ref[0:cols, :] += _dot_tn(ds, qv)
                dq_rows = pl.ds(pl.multiple_of(i * tq + r * sub, sub), sub)
                dq_ref[dq_rows, :] += _dot(ds, kv_)

        @pl.when(j < i)
        def _():
            step(False)

        @pl.when(j == i)
        def _():
            step(True)

    grid_spec = pltpu.PrefetchScalarGridSpec(
        num_scalar_prefetch=2, grid=(HEADS, npairs),
        in_specs=[pl.BlockSpec((None, tq, QKP), lambda h, n, qi, kj: (h, qi[n], 0)),
                  pl.BlockSpec((None, tq, QKP), lambda h, n, qi, kj: (h, kj[n], 0)),
                  pl.BlockSpec((None, tq, QKP), lambda h, n, qi, kj: (h, kj[n], 0)),
                  pl.BlockSpec((None, tq, HK), lambda h, n, qi, kj: (h, qi[n], 0)),
                  pl.BlockSpec((None, tq, QKP), lambda h, n, qi, kj: (h, qi[n], 0))],
        out_specs=[pl.BlockSpec((None, t, QKP), lambda h, n, qi, kj: (h, 0, 0)),
                   pl.BlockSpec((None, tq, QKP), lambda h, n, qi, kj: (h, kj[n], 0)),
                   pl.BlockSpec((None, tq, HK), lambda h, n, qi, kj: (h, kj[n], 0))],
    )
    return _call(
        body, name="flash_bwd", grid_spec=grid_spec,
        out_shape=[jax.ShapeDtypeStruct((HEADS, t, QKP), F32), jax.ShapeDtypeStruct((HEADS, t, QKP), F32),
                   jax.ShapeDtypeStruct((HEADS, t, HK), F32)],
        compiler_params=_cp(56),
    )(qi, kj, q, k, v, lse, do)


def _adamw(name, w, g, m, v):
    r, c = w.shape
    tr = r if r <= 256 else next(k for k in (256, 352, 384) if r % k == 0)

    def body(w_ref, g_ref, m_ref, v_ref, d_ref, nm_ref, nv_ref):
        gv = g_ref[...]
        nm = ADAM_B1 * m_ref[...] + (1.0 - ADAM_B1) * gv
        nv = ADAM_B2 * v_ref[...] + (1.0 - ADAM_B2) * (gv * gv)
        m_hat = nm / (1.0 - ADAM_B1 ** ADAM_STEP)
        v_hat = nv / (1.0 - ADAM_B2 ** ADAM_STEP)
        d_ref[...] = -ADAM_LR * (m_hat / (jnp.sqrt(v_hat) + ADAM_EPS) + ADAM_WD * w_ref[...])
        nm_ref[...] = nm
        nv_ref[...] = nv

    blk = pl.BlockSpec((tr, c), lambda i: (i, 0))
    return _call(
        body, name=name, grid=(r // tr,),
        in_specs=[blk] * 4, out_specs=[blk] * 3,
        out_shape=[jax.ShapeDtypeStruct((r, c), F32)] * 3,
        compiler_params=_cp(),
    )(w, g, m, v)


def _place():
    return lax.axis_index("x"), lax.axis_index("y"), lax.axis_index("c")


def _other_chips(x, y):
    return [(1 - x, y), (x, 1 - y), (1 - x, 1 - y)]


class _Exchange:
    inputs = ()
    out_shapes = ()
    scratch = ()

    def start(self, *refs):
        raise NotImplementedError

    def finish(self, *refs):
        raise NotImplementedError

    def alone(self, name):
        def body(*refs):
            self.start(*refs)
            self.finish(*refs)

        anywhere = pl.BlockSpec(memory_space=pl.ANY)
        return _call(
            body, name=name,
            in_specs=[anywhere] * len(self.inputs), out_specs=[anywhere] * len(self.out_shapes),
            out_shape=list(self.out_shapes), scratch_shapes=list(self.scratch),
        )(*self.inputs)


class _GatherWeights(_Exchange):
    def __init__(self, shards):
        self.inputs = tuple(shards)
        self.out_shapes = tuple(jax.ShapeDtypeStruct((4,) + s.shape, s.dtype) for s in shards)
        self.scratch = (pltpu.SemaphoreType.DMA((6 * len(shards),)), pltpu.SemaphoreType.DMA((6 * len(shards),)))

    def gathered(self, got, k):
        return [lax.dynamic_update_slice(g, s[None], (k, 0, 0)) for g, s in zip(got, self.inputs)]

    def _copies(self, *refs):
        nbuf = len(self.inputs)
        send_sems, recv_sems = refs[2 * nbuf:]
        x, y, c = _place()
        chips = _other_chips(x, y)
        first, passed, landed, relayed = [], [], [], []
        for b, (s_ref, g_ref) in enumerate(zip(refs[:nbuf], refs[nbuf:2 * nbuf])):
            half = self.inputs[b].shape[0] // 2

            def rows(px, py, pc, g_ref=g_ref, half=half):
                return g_ref.at[2 * px + py, pl.ds(pc * half, half), :]

            def copy(k, block, to, src=None, rows=rows, b=b):
                return pltpu.make_async_remote_copy(
                    src_ref=rows(*block) if src is None else src, dst_ref=rows(*block),
                    send_sem=send_sems.at[6 * b + k], recv_sem=recv_sems.at[6 * b + k], device_id=to, device_id_type=MESH)

            mine = s_ref.at[pl.ds(c * half, half), :]
            first += [copy(j, (x, y, c), (*chip, c), src=mine) for j, chip in enumerate(chips)]
            passed += [copy(3 + j, (*chip, c), (x, y, 1 - c)) for j, chip in enumerate(chips)]
            landed += [copy(j, (*chip, c), (x, y, c)) for j, chip in enumerate(chips)]
            relayed += [copy(3 + j, (*chip, 1 - c), (x, y, c)) for j, chip in enumerate(chips)]
        return first, passed, landed, relayed

    def start(self, *refs):
        for cp in self._copies(*refs)[0]:
            cp.start()

    def finish(self, *refs):
        first, passed, landed, relayed = self._copies(*refs)
        for arrived, onward in zip(landed, passed):
            arrived.wait_recv()
            onward.start()
        for cp in relayed:
            cp.wait_recv()
        for cp in first + passed:
            cp.wait_send()


def _swap_halves(name, bufs):
    nbuf = len(bufs)

    def body(*refs):
        send_sems, recv_sems = refs[2 * nbuf:]
        x, y, c = _place()
        cps = []
        for b, (g_ref, o_ref) in enumerate(zip(refs[:nbuf], refs[nbuf:2 * nbuf])):
            half = bufs[b].shape[1] // 2
            cps.append(pltpu.make_async_remote_copy(
                src_ref=g_ref.at[:, pl.ds((1 - c) * half, half), :], dst_ref=o_ref,
                send_sem=send_sems.at[b], recv_sem=recv_sems.at[b], device_id=(x, y, 1 - c), device_id_type=MESH))
        for cp in cps:
            cp.start()
        for cp in cps:
            cp.wait()

    anywhere = pl.BlockSpec(memory_space=pl.ANY)
    return _call(
        body, name=name,
        in_specs=[anywhere] * nbuf, out_specs=[anywhere] * nbuf,
        out_shape=[jax.ShapeDtypeStruct((4, g.shape[1] // 2, g.shape[2]), g.dtype) for g in bufs],
        scratch_shapes=[pltpu.SemaphoreType.DMA((nbuf,)), pltpu.SemaphoreType.DMA((nbuf,))],
    )(*bufs)


def _add_rows(half):
    return next(tr for tr in range(512, 15, -16) if half % tr == 0)


def _chip_sum(name, gp, got, c_arr):
    half, width = got.shape[1], got.shape[2]
    tr = _add_rows(half)
    nb = half // tr

    def body(c_ref, a_ref, b_ref, o_ref, ob_ref):
        s = a_ref[...] + b_ref[...]
        o_ref[...] = s
        ob_ref[...] = s.astype(BF16)

    grid_spec = pltpu.PrefetchScalarGridSpec(
        num_scalar_prefetch=1, grid=(4, nb),
        in_specs=[pl.BlockSpec((None, tr, width), lambda s, i, c: (s, c[0] * nb + i, 0)),
                  pl.BlockSpec((None, tr, width), lambda s, i, c: (s, i, 0))],
        out_specs=[pl.BlockSpec((None, tr, width), lambda s, i, c: (s, i, 0)),
                   pl.BlockSpec((None, tr, width), lambda s, i, c: (s, i, 0))],
    )
    return _call(
        body, name=name, grid_spec=grid_spec,
        out_shape=[jax.ShapeDtypeStruct(got.shape, F32), jax.ShapeDtypeStruct(got.shape, BF16)],
        compiler_params=_cp(),
    )(c_arr, gp, got)


class _ScatterChipSums(_Exchange):
    def __init__(self, sums):
        self.inputs = tuple(sums)
        self.out_shapes = tuple(jax.ShapeDtypeStruct((3,) + cs.shape[1:], cs.dtype) for cs in sums)
        self.scratch = (pltpu.SemaphoreType.DMA((3 * len(sums),)), pltpu.SemaphoreType.DMA((3 * len(sums),)))

    def _copies(self, *refs):
        nbuf = len(self.inputs)
        send_sems, recv_sems = refs[2 * nbuf:]
        x, y, c = _place()
        return [pltpu.make_async_remote_copy(
            src_ref=s_ref.at[2 * px + py], dst_ref=o_ref.at[j],
            send_sem=send_sems.at[3 * b + j], recv_sem=recv_sems.at[3 * b + j], device_id=(px, py, c), device_id_type=MESH)
            for b, (s_ref, o_ref) in enumerate(zip(refs[:nbuf], refs[nbuf:2 * nbuf]))
            for j, (px, py) in enumerate(_other_chips(x, y))]

    def start(self, *refs):
        for cp in self._copies(*refs):
            cp.start()

    def finish(self, *refs):
        for cp in self._copies(*refs):
            cp.wait()


def _shard_sum(name, cs, got, kc_arr):
    h, width = cs.shape[1], cs.shape[2]
    tr = _add_rows(h)
    nb = h // tr

    def body(k_ref, a_ref, b_ref, o_ref):
        o_ref[...] = ((a_ref[...] + b_ref[0].astype(F32)) + b_ref[1].astype(F32)) + b_ref[2].astype(F32)

    grid_spec = pltpu.PrefetchScalarGridSpec(
        num_scalar_prefetch=1, grid=(nb,),
        in_specs=[pl.BlockSpec((None, tr, width), lambda i, k: (k[0], i, 0)),
                  pl.BlockSpec((3, tr, width), lambda i, k: (0, i, 0))],
        out_specs=pl.BlockSpec((tr, width), lambda i, k: (k[1] * nb + i, 0)),
    )
    return _call(
        body, name=name, grid_spec=grid_spec,
        out_shape=jax.ShapeDtypeStruct((2 * h, width), F32),
        compiler_params=_cp(),
    )(kc_arr, cs, got)


def _join_halves(name, boths):
    nbuf = len(boths)

    def body(*refs):
        send_sems, recv_sems = refs[2 * nbuf:]
        x, y, c = _place()
        sent, landing = [], []
        for b, (m_ref, o_ref) in enumerate(zip(refs[:nbuf], refs[nbuf:2 * nbuf])):
            h = boths[b].shape[0] // 2
            mine = m_ref.at[pl.ds(c * h, h), :]
            sent.append(pltpu.make_async_remote_copy(
                src_ref=mine, dst_ref=o_ref.at[pl.ds(c * h, h), :],
                send_sem=send_sems.at[b], recv_sem=recv_sems.at[b], device_id=(x, y, 1 - c), device_id_type=MESH))
            landing.append(pltpu.make_async_remote_copy(
                src_ref=mine, dst_ref=o_ref.at[pl.ds((1 - c) * h, h), :],
                send_sem=send_sems.at[b], recv_sem=recv_sems.at[b], device_id=(x, y, 1 - c), device_id_type=MESH))
        for cp in sent:
            cp.start()
        for cp in sent:
            cp.wait_send()
        for cp in landing:
            cp.wait_recv()

    anywhere = pl.BlockSpec(memory_space=pl.ANY)
    return _call(
        body, name=name,
        in_specs=[anywhere] * nbuf, out_specs=[anywhere] * nbuf,
        out_shape=[jax.ShapeDtypeStruct(g.shape, g.dtype) for g in boths],
        input_output_aliases={b: b for b in range(nbuf)},
        scratch_shapes=[pltpu.SemaphoreType.DMA((nbuf,)), pltpu.SemaphoreType.DMA((nbuf,))],
    )(*boths)


def _all_reduce_small(v):
    r = v.shape[0]

    def body(v_ref, o_ref, buf, send_sems, recv_sems):
        x, y, c = _place()
        me = 4 * x + 2 * y + c
        buf[me] = v_ref[...]
        cps = []
        for k in range(1, 8):
            peer = (x ^ (k >> 2), y ^ ((k >> 1) & 1), c ^ (k & 1))
            cps.append(pltpu.make_async_remote_copy(
                src_ref=v_ref, dst_ref=buf.at[me],
                send_sem=send_sems.at[k - 1], recv_sem=recv_sems.at[k - 1], device_id=peer, device_id_type=MESH))
        for cp in cps:
            cp.start()
        for k in range(1, 8):
            pltpu.make_async_remote_copy(
                src_ref=v_ref, dst_ref=buf.at[me ^ k],
                send_sem=send_sems.at[k - 1], recv_sem=recv_sems.at[k - 1],
                device_id=(x, y, c), device_id_type=MESH).wait_recv()
        for cp in cps:
            cp.wait_send()
        acc = buf[0]
        for k in range(1, 8):
            acc = acc + buf[k]
        o_ref[...] = acc

    return _call(
        body, name="all_reduce_small",
        in_specs=[pl.BlockSpec(memory_space=pltpu.VMEM)],
        out_specs=pl.BlockSpec(memory_space=pltpu.VMEM),
        out_shape=jax.ShapeDtypeStruct((r, 128), F32),
        scratch_shapes=[pltpu.VMEM((8, r, 128), F32), pltpu.SemaphoreType.DMA((7,)), pltpu.SemaphoreType.DMA((7,))],
    )(v)


def _group(names):
    return tuple(e for e in BIG if e[0] in names)


def _pack(shards, dtype):
    return jnp.concatenate([s.astype(dtype).reshape(-1, PACK_W) for s in shards], axis=0)


def _unpack_full(g, group):
    out, at = {}, 0
    for name, rows, cols, axis in group:
        n = rows * cols // 4 // PACK_W
        blk = g[:, at:at + n, :]
        at += n
        if axis == 1:
            out[name] = blk.reshape(4, rows, cols // 4).transpose(1, 0, 2).reshape(rows, cols)
        else:
            out[name] = blk.reshape(rows, cols)
    return out


def _pack_grads(grads, group):
    parts = []
    for name, rows, cols, axis in group:
        g = grads[name]
        if axis == 1:
            g = g.reshape(rows, 4, cols // 4).transpose(1, 0, 2)
        parts.append(g.reshape(4, -1, PACK_W))
    rows_total = sum(p.shape[1] for p in parts)
    pad = -rows_total % PACK_ALIGN
    if pad:
        parts.append(jnp.zeros((4, pad, PACK_W), F32))
    return jnp.concatenate(parts, axis=1)


def _unpack_shard(s, group):
    out, at = {}, 0
    for name, rows, cols, axis in group:
        n = rows * cols // 4 // PACK_W
        shape = (rows, cols // 4) if axis == 1 else (rows // 4, cols)
        out[name] = s[at:at + n, :].reshape(shape)
        at += n
    return out


def _pack_small(parts):
    flat = jnp.concatenate([p.reshape(-1) for p in parts])
    pad = -flat.shape[0] % 1024
    return jnp.concatenate([flat, jnp.zeros((pad,), F32)]).reshape(-1, 128)


def _ffn_in(tag, h, gain, w_in, side=None):
    t = h.shape[0]
    wide = DFF // 2

    def compute_in(rows, weights, outs):
        hv, w_ref = rows[0][...], weights[0]
        r = lax.rsqrt(jnp.mean(hv * hv, axis=-1, keepdims=True) + EPS)
        a = (hv * r * weights[1][...]).astype(BF16)
        outs[0][...] = a

        def emit(gate, up, cols):
            outs[1][:, cols] = gate.astype(BF16)
            outs[2][:, cols] = up.astype(BF16)
            outs[3][:, cols] = (_silu(gate) * up).astype(BF16)

        for s in range(2):
            emit(_dot(a, w_ref[s, :, 0:FFN_MAIN]), _dot(a, w_ref[2 + s, :, 0:FFN_MAIN]),
                 slice(s * wide, s * wide + FFN_MAIN))
        gate = _dot(a, jnp.concatenate([w_ref[0, :, FFN_MAIN:wide], w_ref[1, :, FFN_MAIN:wide]], axis=1))
        up = _dot(a, jnp.concatenate([w_ref[2, :, FFN_MAIN:wide], w_ref[3, :, FFN_MAIN:wide]], axis=1))
        rest = wide - FFN_MAIN
        for s in range(2):
            emit(gate[:, s * rest:(s + 1) * rest], up[:, s * rest:(s + 1) * rest],
                 slice(s * wide + FFN_MAIN, (s + 1) * wide))

    return _rows_call(tag + "_in", [h], [w_in, gain], [(D, BF16)] + [(DFF, BF16)] * 3, compute_in, min(FFN_TM, t),
                      side=side)


def _ffn_out(tag, act, h, w_out, next_gain, target=None):
    t = h.shape[0]
    tm = min(FFN_TM, t)

    def compute_out(rows, weights, outs):
        hn = rows[1][...] + 0.5 * _dot(rows[0][...], weights[0][...])
        g = weights[1][...]
        r = lax.rsqrt(jnp.mean(hn * hn, axis=-1, keepdims=True) + EPS)
        xh = hn * r
        if target is None:
            outs[0][...] = hn
            outs[1][...] = (xh * g).astype(BF16)
        else:
            err = xh * g - rows[2][...]
            dy = err * (1.0 / D)
            dxh = dy * g
            outs[0][...] = r * (dxh - xh * jnp.mean(dxh * xh, axis=-1, keepdims=True))
            outs[1][...] += jnp.sum(dy * xh, axis=0, keepdims=True)
            outs[2][...] += 0.5 * jnp.sum(jnp.mean(err * err, axis=-1, keepdims=True), axis=0, keepdims=True)

    if target is None:
        return _rows_call(tag + "_out", [act, h], [w_out, next_gain], [(D, F32), (D, BF16)], compute_out, tm)
    return _rows_call(tag + "_out", [act, h, target], [w_out, next_gain], [(D, F32)], compute_out, tm, sums=(D, 128))


class _Reduction:
    def __init__(self, tag, c_arr, k_arr):
        self.tag, self.c_arr, self.k_arr = tag, c_arr, k_arr

    def begin(self, bufs):
        swapped = _swap_halves("grad_swap_" + self.tag, bufs)
        sums = [_chip_sum("grad_chip_sum_%s%d" % (self.tag, b), gp, got, self.c_arr)
                for b, (gp, got) in enumerate(zip(bufs, swapped))]
        self.sums = [s[0] for s in sums]
        return _ScatterChipSums([s[1] for s in sums])

    def end(self, got):
        mine = [_shard_sum("grad_shard_sum_%s%d" % (self.tag, b), cs, g, self.k_arr)
                for b, (cs, g) in enumerate(zip(self.sums, got))]
        return _join_halves("grad_join_" + self.tag, mine)


def _ffn_bwd(tag, h, gain, w_in, w_out, saved, dout, side, reduction):
    t = h.shape[0]
    tm = min(TM, t)
    n, gate, up, act = saved

    def compute(rows, weights, outs):
        d = rows[0][...].astype(BF16)
        for j in range(DFF // FFN_CHUNK):
            cols = slice(j * FFN_CHUNK, (j + 1) * FFN_CHUNK)
            da = 0.5 * _dot_nt(d, weights[0][cols, :])
            g, u = rows[1][:, cols].astype(F32), rows[2][:, cols].astype(F32)
            s = _sig(g)
            silu = g * s
            outs[0][:, cols] = (da * u * (s + silu * (1.0 - s))).astype(BF16)
            outs[1][:, cols] = (da * silu).astype(BF16)

    dgate, dup, *side_out = _rows_call(tag + "_dact", [dout, gate, up], [w_out], [(DFF, BF16)] * 2, compute,
                                       min(FFN_TM, t), side=side)
    dw_out = _mm_tn(tag + "_dw_out", act, dout, scale=0.5, tm=DFF // 2, tn=D)
    dw_in = _mm_tn(tag + "_dw_gate", n, dgate, tm=D, tn=DFF // 2, stacked=(4, 0))
    dw_in = _mm_tn(tag + "_dw_up", n, dup, tm=D, tn=DFF // 2, stacked=(4, 2), into=dw_in)
    sending = reduction.begin([dw_in, dw_out.reshape(4, DFF // 4, D)])

    def compute_dn(rows, weights, outs):
        w_ref = weights[0]
        wide = DFF // 2
        dn = jnp.zeros((rows[0].shape[0], D), F32)
        for s in range(2):
            cols = slice(s * wide, s * wide + FFN_MAIN)
            dn = (dn + _dot_nt(rows[0][:, cols], w_ref[s, :, 0:FFN_MAIN])
                  + _dot_nt(rows[1][:, cols], w_ref[2 + s, :, 0:FFN_MAIN]))
        for r, first in ((0, 0), (1, 2)):
            x = jnp.concatenate([rows[r][:, FFN_MAIN:wide], rows[r][:, wide + FFN_MAIN:2 * wide]], axis=1)
            wt = jnp.concatenate([w_ref[first, :, FFN_MAIN:wide], w_ref[first + 1, :, FFN_MAIN:wide]], axis=1)
            dn = dn + _dot_nt(x, wt)
        dx, dg = _rms_bwd_vals(rows[2][...], weights[1][...], dn)
        outs[0][...] = rows[3][...] + dx
        outs[1][...] += jnp.sum(dg, axis=0, keepdims=True)

    dh, dgain, *got = _rows_call(tag + "_dn", [dgate, dup, h, dout], [w_in, gain], [(D, F32)], compute_dn,
                                 min(FFN_TM, t), side=sending, sums=(D,), vmem_mb=58)
    return dh, dgain, side_out, got


def kernel(x, positions, ffn1_norm, ffn1_w_in, ffn1_w_out, mix_norm, w_in, hg_lb_table, hg_out_norm, w_hg_branch, mla_q_lora_norm, w_q_up, mla_kv_lora_norm, w_kv_up, q_head_norm, k_head_norm, w_mla_branch, w_merge, b_merge, w_out, ffn2_norm, ffn2_w_in, ffn2_w_out, final_norm, loss_target, m_ffn1_norm, m_ffn1_w_in, m_ffn1_w_out, m_mix_norm, m_w_in, m_hg_lb_table, m_hg_out_norm, m_w_hg_branch, m_mla_q_lora_norm, m_w_q_up, m_mla_kv_lora_norm, m_w_kv_up, m_q_head_norm, m_k_head_norm, m_w_mla_branch, m_w_merge, m_b_merge, m_w_out, m_ffn2_norm, m_ffn2_w_in, m_ffn2_w_out, m_final_norm, v_ffn1_norm, v_ffn1_w_in, v_ffn1_w_out, v_mix_norm, v_w_in, v_hg_lb_table, v_hg_out_norm, v_w_hg_branch, v_mla_q_lora_norm, v_w_q_up, v_mla_kv_lora_norm, v_w_kv_up, v_q_head_norm, v_k_head_norm, v_w_mla_branch, v_w_merge, v_b_merge, v_w_out, v_ffn2_norm, v_ffn2_w_in, v_ffn2_w_out, v_final_norm):
    a = dict(locals())
    w = {n: a[n] for n in WEIGHT_ORDER}
    mom = {n: a["m_" + n] for n in WEIGHT_ORDER}
    var = {n: a["v_" + n] for n in WEIGHT_ORDER}
    t = x.shape[1]
    tm = min(TM, t)
    xt = x.reshape(t, D)
    target = loss_target.reshape(t, D)
    pos = positions.reshape(t, 1)
    x_i, y_i, c_i = _place()
    k_idx = (2 * x_i + y_i).astype(jnp.int32)
    c_arr = c_i.astype(jnp.int32).reshape(1)
    k_arr = jnp.stack([k_idx, c_i.astype(jnp.int32)])

    group_mid = _group(("w_in", "w_hg_branch", "w_q_up", "w_kv_up", "w_mla_branch", "w_merge", "w_out"))
    use_early = _group(("ffn1_w_out", "w_in", "w_hg_branch", "w_q_up", "w_kv_up"))
    use_late = _group(("w_mla_branch", "w_merge", "w_out", "ffn2_w_out"))
    gather_first = _GatherWeights([w["ffn1_w_in"][0].astype(BF16)])
    gather_early = _GatherWeights([_pack([w[e[0]][0] for e in use_early], BF16)])
    gather_late = _GatherWeights([_pack([w[e[0]][0] for e in use_late], BF16), w["ffn2_w_in"][0].astype(BF16)])
    (ffn1_w_in_g,) = gather_first.gathered(gather_first.alone("gather_first"), k_idx)
    n1, gate1, up1, act1, got = _ffn_in("ffn1", xt, w["ffn1_norm"], ffn1_w_in_g, gather_early)
    full = _unpack_full(gather_early.gathered([got], k_idx)[0], use_early)
    h1, u = _ffn_out("ffn1", act1, xt, full["ffn1_w_out"], w["mix_norm"])
    ffn1_saved = (n1, gate1, up1, act1)
    w_in_full = full["w_in"]
    w_in_hg = w_in_full[:, :4 * D]
    w_in_mla = jnp.pad(w_in_full[:, 4 * D:], ((0, 0), (0, MLA_COLS - (4800 - 4 * D))))
    w_q_pad = jnp.pad(full["w_q_up"].reshape(Q_LORA, HEADS, QK), ((0, 0), (0, 0), (0, QKP - QK))).reshape(Q_LORA, HEADS * QKP)
    w_kv = full["w_kv_up"]
    gq = jnp.pad(w["q_head_norm"], ((0, 0), (0, QKP - QK)))
    gk = jnp.pad(w["k_head_norm"], ((0, 0), (0, QKP - QK)))

    ident = lambda accs, ex: (accs[0],)
    def in_hg(rows, weights, outs):
        a = rows[0][...]
        for j in range(4 * D // 512):
            cols = slice(j * 512, (j + 1) * 512)
            outs[0][:, cols] = _dot(a, weights[0][:, cols])

    (p_hg,) = _rows_call("in_hg", [u], [w_in_hg], [(4 * D, F32)], in_hg, min(FFN_TM, t))
    p_mla, cqn, ckvn = _in_mla(u, w_in_mla, w["mla_q_lora_norm"], w["mla_kv_lora_norm"])
    o_raw, hg_o, states = _hgrn_fwd(p_hg, w["hg_lb_table"], w["hg_out_norm"])
    (y_hg,) = _mm("hg_branch", [_a_spec(hg_o, tm)], [_b_nn(full["w_hg_branch"], 512)], [(0, 0)], ident, [], [BF16], t, D, tm, 512)
    (qf,) = _mm("q_up", [_a_spec(cqn, tm)], [_b_nn(w_q_pad, 512)], [(0, 0)], ident, [], [F32], t, HEADS * QKP, tm, 512)
    (kvf,) = _mm("kv_up", [_a_spec(ckvn, tm)], [_b_nn(w_kv, 512)], [(0, 0)], ident, [], [F32], t, HEADS * QKP, tm, 512)
    cos, sin = _rope_tables(pos)
    qh, kh, vh = _mla_prep_fwd(qf, kvf, p_mla, cos, sin, gq, gk)
    o_mla, lse, *got = _flash_fwd(qh, kh, vh, side=gather_late)
    late, ffn2_w_in_g = gather_late.gathered(got, k_idx)
    full.update(_unpack_full(late, use_late))
    (y_mla,) = _mm("mla_branch", [_a_spec(o_mla, tm)], [_b_nn(full["w_mla_branch"], 512)], [(0, 0)], ident, [], [BF16], t, D, tm, 512)

    def merge_epi(accs, ex):
        g_hg = _sig(accs[0] + ex[2])
        g_mla = _sig(accs[1] + ex[3])
        return g_hg * ex[0].astype(F32) + g_mla * ex[1].astype(F32), g_hg, g_mla

    w_merge_f = full["w_merge"]
    mix, g_hg, g_mla = _mm(
        "merge", [_a_spec(u, tm)], [_b_nn(w_merge_f, 512), _b_nn(w_merge_f, 512, D // 512)], [(0, 0), (0, 1)], merge_epi,
        [_e_tile(y_hg, tm, 512), _e_tile(y_mla, tm, 512), _e_row(w["b_merge"], 512), _e_row(w["b_merge"], 512, D // 512)],
        [BF16, BF16, BF16], t, D, tm, 512)
    (h2,) = _mm("out_proj", [_a_spec(mix, tm)], [_b_nn(full["w_out"], 512)], [(0, 0)],
                lambda accs, ex: (ex[0] + accs[0],), [_e_tile(h1, tm, 512)], [F32], t, D, tm, 512)
    ffn2_saved = _ffn_in("ffn2", h2, w["ffn2_norm"], ffn2_w_in_g)
    dh3, d_final_norm, loss_part = _ffn_out("ffn2", ffn2_saved[3], h2, full["ffn2_w_out"], w["final_norm"], target=target)

    grads, small = {}, {}
    small["final_norm"] = d_final_norm
    reduce_last = _Reduction("last", c_arr, k_arr)
    reduce_mid = _Reduction("mid", c_arr, k_arr)
    reduce_first = _Reduction("first", c_arr, k_arr)
    dh2, small["ffn2_norm"], _, got_last = _ffn_bwd(
        "ffn2", h2, w["ffn2_norm"], ffn2_w_in_g, full["ffn2_w_out"], ffn2_saved, dh3, None, reduce_last)

    def dmix_epi(accs, ex):
        dm = accs[0]
        ghg, gml, yhg, yml = [e.astype(F32) for e in ex]
        return dm * ghg, dm * gml, dm * yhg * ghg * (1.0 - ghg), dm * yml * gml * (1.0 - gml)

    dy_hg, dy_mla, dpre_hg, dpre_mla = _mm(
        "d_mix", [_a_spec(dh2, tm)], [_b_nt(full["w_out"], 512)], [(0, 0)], dmix_epi,
        [_e_tile(g_hg, tm, 512), _e_tile(g_mla, tm, 512), _e_tile(y_hg, tm, 512), _e_tile(y_mla, tm, 512)],
        [BF16, BF16, BF16, BF16], t, D, tm, 512, trans_b=True)
    grads["w_out"] = _mm_tn("dw_out", mix, dh2)
    small["b_merge"] = jnp.concatenate([_colsum("db_hg", dpre_hg), _colsum("db_mla", dpre_mla)], axis=1)
    grads["w_merge"] = jnp.concatenate([_mm_tn("dw_merge_hg", u, dpre_hg), _mm_tn("dw_merge_mla", u, dpre_mla)], axis=1)
    grads["w_hg_branch"] = _mm_tn("dw_hg_branch", hg_o, dy_hg)
    grads["w_mla_branch"] = _mm_tn("dw_mla_branch", o_mla, dy_mla)
    (dho,) = _mm("d_hg_o", [_a_spec(dy_hg, tm)], [_b_nt(full["w_hg_branch"], 512)], [(0, 0)], ident, [], [BF16], t, D, tm, 512, trans_b=True)
    (do_mla,) = _mm("d_o_mla", [_a_spec(dy_mla, tm)], [_b_nt(full["w_mla_branch"], 512)], [(0, 0)], ident, [], [BF16], t, D, tm, 512, trans_b=True)

    dq_raw, df_raw, di_raw, dg_raw, small["hg_lb_table"], small["hg_out_norm"] = _hgrn_bwd(
        p_hg, w["hg_lb_table"], w["hg_out_norm"], o_raw, states, dho)
    dp_hg = [dq_raw, df_raw, di_raw, dg_raw]

    dqh, dkh, dvh = _flash_bwd(qh, kh, vh, lse, _attn_do(do_mla, o_mla))
    dqf, dkvf, dkpe, dgq, dgk = _mla_prep_bwd(qf, kvf, p_mla, cos, sin, gq, gk, dqh, dkh, dvh)
    small["q_head_norm"] = dgq[:, :QK]
    small["k_head_norm"] = dgk[:, :QK]
    dwq_pad = _mm_tn("dw_q_up", cqn, dqf, tm=Q_LORA, tn=1024)
    grads["w_q_up"] = dwq_pad.reshape(Q_LORA, HEADS, QKP)[:, :, :QK].reshape(Q_LORA, HEADS * QK)
    grads["w_kv_up"] = _mm_tn("dw_kv_up", ckvn, dkvf, tm=KV_LORA, tn=1024)
    (dcqn,) = _mm("d_cq", [_a_spec(dqf, tm)], [_b_nt(w_q_pad, Q_LORA)], [(0, 0)], ident, [], [F32], t, Q_LORA, tm, Q_LORA, trans_b=True)
    (dckvn,) = _mm("d_ckv", [_a_spec(dkvf, tm)], [_b_nt(w_kv, KV_LORA)], [(0, 0)], ident, [], [F32], t, KV_LORA, tm, KV_LORA, trans_b=True)
    dp_mla, small["mla_q_lora_norm"], small["mla_kv_lora_norm"] = _lora_norm_bwd(
        p_mla, w["mla_q_lora_norm"], w["mla_kv_lora_norm"], dcqn, dckvn, dkpe)

    dw_in_hg = [_mm_tn("dw_in_hg%d" % k, u, dp_hg[k]) for k in range(4)]
    dw_in_mla = _mm_tn("dw_in_mla", u, dp_mla, tn=MLA_COLS)
    grads["w_in"] = jnp.concatenate(dw_in_hg + [dw_in_mla[:, :4800 - 4 * D]], axis=1)
    tm_du = min(TM // 2, t)
    du, *got_mid = _mm(
        "d_u",
        [_a_spec(dpre_hg, tm_du), _a_spec(dpre_mla, tm_du)] + [_a_spec(d, tm_du) for d in dp_hg] + [_a_spec(dp_mla, tm_du)],
        [_b_nt(w_merge_f, 512, D, 0), _b_nt(w_merge_f, 512, D, 1)]
        + [_b_nt(w_in_hg, 512, D, k) for k in range(4)] + [_b_nt(w_in_mla, 512)],
        [(k, k) for k in range(7)],
        lambda accs, ex: (functools.reduce(lambda p, q: p + q, accs),), [], [F32], t, D, tm_du, 512, trans_b=True,
        side=reduce_mid.begin([_pack_grads(grads, group_mid)]))
    dh1, small["mix_norm"] = _rms_bwd("mix_dnorm", h1, w["mix_norm"], du, dh2)
    dx, small["ffn1_norm"], _, got_first = _ffn_bwd(
        "ffn1", xt, w["ffn1_norm"], ffn1_w_in_g, full["ffn1_w_out"], ffn1_saved, dh1, None, reduce_first)

    g_shard = _unpack_shard(reduce_mid.end(got_mid)[0], group_mid)
    g_shard["ffn2_w_in"], g_shard["ffn2_w_out"] = reduce_last.end(got_last)
    g_shard["ffn1_w_in"], g_shard["ffn1_w_out"] = reduce_first.end(got_first)
    small_sum = _all_reduce_small(_pack_small([small[n] for n, _ in SMALL] + [loss_part])).reshape(-1)
    g_small, at = {}, 0
    for n, shape in SMALL:
        size = shape[0] * shape[1]
        g_small[n] = small_sum[at:at + size].reshape(shape)
        at += size
    loss = small_sum[at]

    g_out, d_out, m_out, v_out = {}, {}, {}, {}
    for n in WEIGHT_ORDER:
        shape = w[n].shape
        g = g_shard[n] if n in g_shard else g_small[n]
        two = g.shape
        d_, m_, v_ = _adamw("adamw_" + n, w[n].reshape(two), g, mom[n].reshape(two), var[n].reshape(two))
        g_out[n], d_out[n], m_out[n], v_out[n] = g.reshape(shape), d_.reshape(shape), m_.reshape(shape), v_.reshape(shape)

    return (loss, dx.reshape(x.shape), *[g_out[n] for n in WEIGHT_ORDER], *[d_out[n] for n in WEIGHT_ORDER],
            *[m_out[n] for n in WEIGHT_ORDER], *[v_out[n] for n in WEIGHT_ORDER])
```

```python
import functools

import numpy as np
import jax
import jax.numpy as jnp
from jax import lax
from jax.experimental import pallas as pl
from jax.experimental.pallas import tpu as pltpu

F32 = jnp.float32
BF16 = jnp.bfloat16
MESH = pl.DeviceIdType.MESH

D = 1024
DFF = 2816
HEADS = 8
HK = 128
CHUNK = 64
ROPE = 64
QK = 192
QKP = 256
Q_LORA = 384
KV_LORA = 256
MLA_COLS = 768
EPS = 1e-6
ROPE_THETA = 10000.0
SCALE = QK ** -0.5
LOG2E = 1.4426950408889634
LN2 = 0.6931471805599453
NEG = -1e30
EXP_CLAMP = 80.0

ADAM_LR = 0.001
ADAM_B1 = 0.9
ADAM_B2 = 0.999
ADAM_EPS = 1e-08
ADAM_WD = 0.01
ADAM_STEP = 10

PACK_W = 1024
ADD_ROWS = 352
PACK_ALIGN = 2 * ADD_ROWS

TM = 1024
FFN_TM = 512
FFN_CHUNK = 256
FFN_MAIN = 1280
TQ = 2048
SUBQ = 256
HG_BT = 512
HG_HPB = 8
TT = 2048
ROW_TM = 256

VMEM_MB = 48

BIG = (
    ("ffn1_w_in", D, 2 * DFF, 1),
    ("ffn1_w_out", DFF, D, 0),
    ("w_in", D, 4800, 1),
    ("w_hg_branch", D, D, 0),
    ("w_q_up", Q_LORA, HEADS * QK, 1),
    ("w_kv_up", KV_LORA, HEADS * 2 * HK, 1),
    ("w_mla_branch", D, D, 0),
    ("w_merge", D, 2 * D, 1),
    ("w_out", D, D, 0),
    ("ffn2_w_in", D, 2 * DFF, 1),
    ("ffn2_w_out", DFF, D, 0),
)
SMALL = (
    ("ffn1_norm", (1, D)),
    ("mix_norm", (1, D)),
    ("hg_lb_table", (2, D)),
    ("hg_out_norm", (1, HK)),
    ("mla_q_lora_norm", (1, Q_LORA)),
    ("mla_kv_lora_norm", (1, KV_LORA)),
    ("q_head_norm", (1, QK)),
    ("k_head_norm", (1, QK)),
    ("b_merge", (1, 2 * D)),
    ("ffn2_norm", (1, D)),
    ("final_norm", (1, D)),
)
WEIGHT_ORDER = ("ffn1_norm", "ffn1_w_in", "ffn1_w_out", "mix_norm", "w_in", "hg_lb_table", "hg_out_norm",
                "w_hg_branch", "mla_q_lora_norm", "w_q_up", "mla_kv_lora_norm", "w_kv_up", "q_head_norm",
                "k_head_norm", "w_mla_branch", "w_merge", "b_merge", "w_out", "ffn2_norm", "ffn2_w_in",
                "ffn2_w_out", "final_norm")


def _call(body, **kw):
    return pl.pallas_call(body, **kw)


def _cp(vmem_mb=VMEM_MB):
    return pltpu.CompilerParams(vmem_limit_bytes=vmem_mb << 20)


def _dot(a, b):
    return lax.dot_general(a, b, (((1,), (0,)), ((), ())), preferred_element_type=F32)


def _dot_nt(a, b):
    return lax.dot_general(a, b, (((1,), (1,)), ((), ())), preferred_element_type=F32)


def _dot_tn(a, b):
    return lax.dot_general(a, b, (((0,), (0,)), ((), ())), preferred_element_type=F32)


def _sig(x):
    return jax.nn.sigmoid(x)


def _silu(x):
    return x * _sig(x)


def _dsilu(x):
    s = _sig(x)
    return s * (1.0 + x * (1.0 - s))


def _a_spec(arr, tm, kblk=None, kidx=0):
    kb = arr.shape[1] if kblk is None else kblk
    return arr, pl.BlockSpec((tm, kb), lambda i, j, kidx=kidx: (i, kidx)), slice(kidx * kb, (kidx + 1) * kb)


def _b_nn(arr, tn, off=0):
    return arr, pl.BlockSpec((arr.shape[0], tn), lambda i, j, off=off: (0, j + off)), ("cols", off)


def _b_nt(arr, tn, kblk=None, kidx=0):
    kb = arr.shape[1] if kblk is None else kblk
    return arr, pl.BlockSpec((tn, kb), lambda i, j, kidx=kidx: (j, kidx)), ("rows", slice(kidx * kb, (kidx + 1) * kb))


def _e_tile(arr, tm, tn, off=0):
    return arr, pl.BlockSpec((tm, tn), lambda i, j, off=off: (i, j + off)), ("tile", off)


def _e_row(arr, tn, off=0):
    return arr, pl.BlockSpec((1, tn), lambda i, j, off=off: (0, j + off)), ("row", off)


def _mm_resident(name, As, Bs, dots, epi, extras, out_dtypes, m, n, tn):
    def unique(arrays):
        seen = []
        for a in arrays:
            if not any(a is s for s in seen):
                seen.append(a)
        return seen

    rows = unique([a for a, _, _ in As] + [e for e, _, where in extras if where[0] == "tile"])
    weights = unique([b for b, _, _ in Bs] + [e for e, _, where in extras if where[0] == "row"])

    def ref_of(arr, row_refs, weight_refs):
        for r, ref in zip(rows, row_refs):
            if r is arr:
                return ref
        for wt, ref in zip(weights, weight_refs):
            if wt is arr:
                return ref

    def compute(row_refs, weight_refs, out_refs):
        a_vals = [ref_of(a, row_refs, weight_refs)[:, ks].astype(BF16) for a, _, ks in As]
        for j in range(n // tn):
            accs = []
            for ai, bi in dots:
                b, _, where = Bs[bi]
                b_ref = ref_of(b, row_refs, weight_refs)
                if where[0] == "cols":
                    accs.append(_dot(a_vals[ai], b_ref[:, (j + where[1]) * tn:(j + where[1] + 1) * tn]))
                else:
                    accs.append(_dot_nt(a_vals[ai], b_ref[j * tn:(j + 1) * tn, where[1]]))
            ex = [ref_of(e, row_refs, weight_refs)[:, (j + where[1]) * tn:(j + where[1] + 1) * tn]
                  for e, _, where in extras]
            for o_ref, o in zip(out_refs, epi(accs, ex)):
                o_ref[:, j * tn:(j + 1) * tn] = o.astype(o_ref.dtype)

    return _rows_call(name, rows, weights, [(n, dt) for dt in out_dtypes], compute, min(FFN_TM, m))


def _mm(name, As, Bs, dots, epi, extras, out_dtypes, m, n, tm, tn, trans_b=False, side=None):
    if side is None:
        return _mm_resident(name, As, Bs, dots, epi, extras, out_dtypes, m, n, tn)
    na, nb, ne, no = len(As), len(Bs), len(extras), len(out_dtypes)
    ni, nj = m // tm, n // tn
    s_in = len(side.inputs) if side else 0
    s_out = len(side.out_shapes) if side else 0

    def body(*refs):
        a_refs = refs[:na]
        b_refs = refs[na:na + nb]
        e_refs = refs[na + nb:na + nb + ne]
        at = na + nb + ne
        side_refs = refs[at:at + s_in]
        o_refs = refs[at + s_in:at + s_in + no]
        side_refs = list(side_refs) + list(refs[at + s_in + no:])
        if side:
            i, j = pl.program_id(0), pl.program_id(1)

            @pl.when(jnp.logical_and(i == 0, j == 0))
            def _():
                side.start(*side_refs)

        a_vals = [r[...].astype(BF16) for r in a_refs]
        accs = []
        for ai, bi in dots:
            b = b_refs[bi][...]
            accs.append(_dot_nt(a_vals[ai], b) if trans_b else _dot(a_vals[ai], b))
        outs = epi(accs, [r[...] for r in e_refs])
        for o_ref, o in zip(o_refs, outs):
            o_ref[...] = o.astype(o_ref.dtype)
        if side:
            @pl.when(jnp.logical_and(i == ni - 1, j == nj - 1))
            def _():
                side.finish(*side_refs)

    ops = list(As) + list(Bs) + list(extras)
    anywhere = pl.BlockSpec(memory_space=pl.ANY)
    res = _call(
        body, name=name,
        grid=(ni, nj),
        in_specs=[op[1] for op in ops] + [anywhere] * s_in,
        out_specs=[pl.BlockSpec((tm, tn), lambda i, j: (i, j)) for _ in out_dtypes] + [anywhere] * s_out,
        out_shape=[jax.ShapeDtypeStruct((m, n), dt) for dt in out_dtypes] + (list(side.out_shapes) if side else []),
        scratch_shapes=list(side.scratch) if side else [],
        compiler_params=_cp(),
    )(*[op[0] for op in ops], *(side.inputs if side else []))
    return res


def _rows_call(name, rows, weights, outs, compute, tm, side=None, sums=(), vmem_mb=VMEM_MB):
    t = rows[0].shape[0]
    nr, nw, no = len(rows), len(weights), len(outs) + len(sums)
    ni = t // tm
    s_in = len(side.inputs) if side else 0
    s_out = len(side.out_shapes) if side else 0

    def body(*refs):
        at = nr + nw
        side_refs = list(refs[at:at + s_in]) + list(refs[at + s_in + no:])
        if side:
            @pl.when(pl.program_id(0) == 0)
            def _():
                side.start(*side_refs)

        out_refs = refs[at + s_in:at + s_in + no]
        if sums:
            @pl.when(pl.program_id(0) == 0)
            def _():
                for r in out_refs[len(outs):]:
                    r[...] = jnp.zeros_like(r)

        compute(refs[:nr], refs[nr:at], out_refs)
        if side:
            @pl.when(pl.program_id(0) == ni - 1)
            def _():
                side.finish(*side_refs)

    anywhere = pl.BlockSpec(memory_space=pl.ANY)
    return _call(
        body, name=name, grid=(ni,),
        in_specs=[pl.BlockSpec((tm, r.shape[1]), lambda i: (i, 0)) for r in rows]
        + [pl.BlockSpec(wt.shape, lambda i, nd=wt.ndim: (0,) * nd) for wt in weights] + [anywhere] * s_in,
        out_specs=[pl.BlockSpec((tm, width), lambda i: (i, 0)) for width, _ in outs]
        + [pl.BlockSpec((1, width), lambda i: (0, 0)) for width in sums] + [anywhere] * s_out,
        out_shape=[jax.ShapeDtypeStruct((t, width), dt) for width, dt in outs]
        + [jax.ShapeDtypeStruct((1, width), F32) for width in sums] + (list(side.out_shapes) if side else []),
        scratch_shapes=list(side.scratch) if side else [],
        compiler_params=_cp(vmem_mb),
    )(*rows, *weights, *(side.inputs if side else []))


def _mm_tn(name, a, b, scale=1.0, tm=1024, tn=1024, stacked=None, into=None, side=None):
    t, m = a.shape
    n = b.shape[1]
    tm, tn, tt = min(tm, m), min(tn, n), min(TT, t)
    ni, nj, nk = m // tm, n // tn, t // tt
    extra_in = [into] if into is not None else list(side.inputs) if side else []
    s_out = len(side.out_shapes) if side else 0

    def body(a_ref, b_ref, *rest):
        o_ref = rest[len(extra_in)]
        i, j, k = pl.program_id(0), pl.program_id(1), pl.program_id(2)
        if side:
            side_refs = list(rest[:len(extra_in)]) + list(rest[len(extra_in) + 1:])

            @pl.when(jnp.logical_and(jnp.logical_and(i == 0, j == 0), k == 0))
            def _():
                side.start(*side_refs)

        @pl.when(k == 0)
        def _():
            o_ref[...] = jnp.zeros_like(o_ref)

        o_ref[...] += _dot_tn(a_ref[...].astype(BF16), b_ref[...].astype(BF16))
        if scale != 1.0:
            @pl.when(k == nk - 1)
            def _():
                o_ref[...] = o_ref[...] * scale
        if side:
            @pl.when(jnp.logical_and(jnp.logical_and(i == ni - 1, j == nj - 1), k == nk - 1))
            def _():
                side.finish(*side_refs)

    anywhere = pl.BlockSpec(memory_space=pl.ANY)
    product = jax.ShapeDtypeStruct((stacked[0], m, tn) if stacked else (m, n), F32)
    res = _call(
        body, name=name,
        grid=(ni, nj, nk),
        in_specs=[pl.BlockSpec((tt, tm), lambda i, j, k: (k, i)), pl.BlockSpec((tt, tn), lambda i, j, k: (k, j))]
        + [anywhere] * len(extra_in),
        out_specs=[pl.BlockSpec((None, tm, tn), lambda i, j, k: (stacked[1] + j, i, 0)) if stacked
                   else pl.BlockSpec((tm, tn), lambda i, j, k: (i, j))] + [anywhere] * s_out,
        out_shape=[product] + (list(side.out_shapes) if side else []),
        input_output_aliases={2: 0} if into is not None else {},
        scratch_shapes=list(side.scratch) if side else [],
        compiler_params=_cp(),
    )(a, b, *extra_in)
    return res if side else res[0]


def _rms_bwd_vals(xv, g, dn):
    r = lax.rsqrt(jnp.mean(xv * xv, axis=-1, keepdims=True) + EPS)
    xh = xv * r
    dxh = dn * g
    c = jnp.mean(dxh * xh, axis=-1, keepdims=True)
    return r * (dxh - xh * c), dn * xh


def _rms_bwd(name, x, gain, dn, dres):
    t, d = x.shape
    tm = min(ROW_TM, t)

    def body(x_ref, g_ref, dn_ref, dr_ref, dx_ref, dg_ref):
        @pl.when(pl.program_id(0) == 0)
        def _():
            dg_ref[...] = jnp.zeros_like(dg_ref)

        dx, dg = _rms_bwd_vals(x_ref[...], g_ref[...], dn_ref[...].astype(F32))
        dx_ref[...] = dr_ref[...] + dx
        dg_ref[...] += jnp.sum(dg, axis=0, keepdims=True)

    row = pl.BlockSpec((tm, d), lambda i: (i, 0))
    one = pl.BlockSpec((1, d), lambda i: (0, 0))
    return _call(
        body, name=name, grid=(t // tm,),
        in_specs=[row, one, row, row],
        out_specs=[row, one],
        out_shape=[jax.ShapeDtypeStruct((t, d), F32), jax.ShapeDtypeStruct((1, d), F32)],
        compiler_params=_cp(),
    )(x, gain, dn, dres)


def _colsum(name, x):
    t, n = x.shape
    tm = min(TM, t)

    def body(x_ref, o_ref):
        @pl.when(pl.program_id(0) == 0)
        def _():
            o_ref[...] = jnp.zeros_like(o_ref)

        o_ref[...] += jnp.sum(x_ref[...].astype(F32), axis=0, keepdims=True)

    return _call(
        body, name=name, grid=(t // tm,),
        in_specs=[pl.BlockSpec((tm, n), lambda i: (i, 0))],
        out_specs=pl.BlockSpec((1, n), lambda i: (0, 0)),
        out_shape=jax.ShapeDtypeStruct((1, n), F32),
        compiler_params=_cp(),
    )(x)


def _in_mla(u, w_in_mla, gq, gkv):
    t = u.shape[0]

    def compute(rows, weights, outs):
        p = _dot(rows[0][...], weights[0][...])
        outs[0][...] = p
        cq = p[:, 0:Q_LORA]
        ckv = p[:, Q_LORA:Q_LORA + KV_LORA]
        rq = lax.rsqrt(jnp.mean(cq * cq, axis=-1, keepdims=True) + EPS)
        rkv = lax.rsqrt(jnp.mean(ckv * ckv, axis=-1, keepdims=True) + EPS)
        outs[1][...] = (cq * rq * weights[1][...]).astype(BF16)
        outs[2][...] = (ckv * rkv * weights[2][...]).astype(BF16)

    return _rows_call("in_mla", [u], [w_in_mla, gq, gkv], [(MLA_COLS, F32), (Q_LORA, BF16), (KV_LORA, BF16)], compute,
                      min(FFN_TM, t))


def _lora_norm_bwd(p_mla, gq, gkv, dcqn, dckvn, dkpe):
    t = p_mla.shape[0]
    tm = min(ROW_TM, t)

    def body(p_ref, gq_ref, gkv_ref, dq_ref, dkv_ref, dkpe_ref, dp_ref, dgq_ref, dgkv_ref):
        @pl.when(pl.program_id(0) == 0)
        def _():
            dgq_ref[...] = jnp.zeros_like(dgq_ref)
            dgkv_ref[...] = jnp.zeros_like(dgkv_ref)

        dcq, dgq = _rms_bwd_vals(p_ref[:, 0:Q_LORA], gq_ref[...], dq_ref[...])
        dckv, dgkv = _rms_bwd_vals(p_ref[:, Q_LORA:Q_LORA + KV_LORA], gkv_ref[...], dkv_ref[...])
        dp_ref[:, 0:Q_LORA] = dcq.astype(BF16)
        dp_ref[:, Q_LORA:Q_LORA + KV_LORA] = dckv.astype(BF16)
        dp_ref[:, Q_LORA + KV_LORA:MLA_COLS] = dkpe_ref[...].astype(BF16)
        dgq_ref[...] += jnp.sum(dgq, axis=0, keepdims=True)
        dgkv_ref[...] += jnp.sum(dgkv, axis=0, keepdims=True)

    return _call(
        body, name="lora_norm_bwd", grid=(t // tm,),
        in_specs=[pl.BlockSpec((tm, MLA_COLS), lambda i: (i, 0)),
                  pl.BlockSpec((1, Q_LORA), lambda i: (0, 0)), pl.BlockSpec((1, KV_LORA), lambda i: (0, 0)),
                  pl.BlockSpec((tm, Q_LORA), lambda i: (i, 0)), pl.BlockSpec((tm, KV_LORA), lambda i: (i, 0)),
                  pl.BlockSpec((tm, HK), lambda i: (i, 0))],
        out_specs=[pl.BlockSpec((tm, MLA_COLS), lambda i: (i, 0)),
                   pl.BlockSpec((1, Q_LORA), lambda i: (0, 0)), pl.BlockSpec((1, KV_LORA), lambda i: (0, 0))],
        out_shape=[jax.ShapeDtypeStruct((t, MLA_COLS), BF16), jax.ShapeDtypeStruct((1, Q_LORA), F32),
                   jax.ShapeDtypeStruct((1, KV_LORA), F32)],
        compiler_params=_cp(),
    )(p_mla, gq, gkv, dcqn, dckvn, dkpe)


def _cumsum_rows(x, row):
    for s in (1, 2, 4, 8, 16, 32):
        x = x + jnp.where(row >= s, pltpu.roll(x, s, 0), 0.0)
    return x


def _rcumsum_rows(x, row):
    for s in (1, 2, 4, 8, 16, 32):
        x = x + jnp.where(row < CHUNK - s, pltpu.roll(x, CHUNK - s, 0), 0.0)
    return x


def _hg_gates(qr, z, lb, row):
    q = _silu(qr)
    sg = _sig(z)
    f = lb + (1.0 - lb) * sg
    lf = jnp.log(f)
    k = (1.0 - lb) * (1.0 - sg)
    cum = _cumsum_rows(lf, row)
    mid = jnp.sum(jnp.where(row < CHUNK // 2, lf, 0.0), axis=0, keepdims=True)
    last = jnp.sum(lf, axis=0, keepdims=True)
    e_q = jnp.exp(jnp.minimum(cum - mid, EXP_CLAMP))
    e_k = jnp.exp(jnp.minimum(mid - cum, EXP_CLAMP))
    e_a = jnp.exp(cum)
    e_l = jnp.exp(last - cum)
    return q, sg, f, k, last, e_q, e_k, e_a, e_l


def _hgrn_fwd(p_hg, tab, gain):
    t = p_hg.shape[0]
    bt = min(HG_BT, t)
    nb, nc = t // bt, bt // CHUNK

    hpb = HG_HPB
    wide = hpb * HK

    def body(q_ref, f_ref, i_ref, g_ref, tab_ref, gain_ref, o_ref, ho_ref, st_ref, state):
        @pl.when(pl.program_id(1) == 0)
        def _():
            state[...] = jnp.zeros_like(state)

        row = lax.broadcasted_iota(jnp.int32, (CHUNK, HK), 0)
        tril = lax.broadcasted_iota(jnp.int32, (CHUNK, CHUNK), 0) >= lax.broadcasted_iota(jnp.int32, (CHUNK, CHUNK), 1)
        gain_v = gain_ref[...]

        def chunk(c, carry):
            sl = pl.ds(pl.multiple_of(c * CHUNK, CHUNK), CHUNK)
            for hh in range(hpb):
                ln = slice(hh * HK, (hh + 1) * HK)
                lb = _sig(tab_ref[0:1, ln] - tab_ref[1:2, ln])
                v = i_ref[sl, ln].astype(BF16)
                q, _, _, k, last, e_q, e_k, e_a, e_l = _hg_gates(q_ref[sl, ln], f_ref[sl, ln], lb, row)
                st = state[hh]
                st_ref[hh, c] = st
                p = jnp.where(tril, _dot_nt((q * e_q).astype(BF16), (k * e_k).astype(BF16)), 0.0)
                o = _dot(p.astype(BF16), v) + _dot_nt((q * e_a).astype(BF16), st.astype(BF16))
                state[hh] = jnp.exp(last) * st + _dot_tn(v, (k * e_l).astype(BF16))
                o_ref[sl, ln] = o
                r = lax.rsqrt(jnp.mean(o * o, axis=-1, keepdims=True) + EPS)
                ho_ref[sl, ln] = (o * r * gain_v * _silu(g_ref[sl, ln])).astype(BF16)
            return carry

        lax.fori_loop(0, nc, chunk, 0)

    def col(k):
        return pl.BlockSpec((bt, wide), lambda h, j, k=k: (j, k * (HEADS // hpb) + h))

    return _call(
        body, name="hgrn_fwd", grid=(HEADS // hpb, nb),
        in_specs=[col(0), col(1), col(2), col(3),
                  pl.BlockSpec((2, wide), lambda h, j: (0, h)), pl.BlockSpec((1, HK), lambda h, j: (0, 0))],
        out_specs=[pl.BlockSpec((bt, wide), lambda h, j: (j, h)), pl.BlockSpec((bt, wide), lambda h, j: (j, h)),
                   pl.BlockSpec((hpb, nc, HK, HK), lambda h, j: (h, j, 0, 0))],
        out_shape=[jax.ShapeDtypeStruct((t, D), F32), jax.ShapeDtypeStruct((t, D), BF16),
                   jax.ShapeDtypeStruct((HEADS, t // CHUNK, HK, HK), F32)],
        scratch_shapes=[pltpu.VMEM((hpb, HK, HK), F32)],
        compiler_params=_cp(),
    )(p_hg, p_hg, p_hg, p_hg, tab, gain)


def _hgrn_bwd(p_hg, tab, gain, o_raw, states, dho):
    t = p_hg.shape[0]
    bt = min(HG_BT, t)
    nb, nc = t // bt, bt // CHUNK
    hpb = HG_HPB
    wide = hpb * HK

    def body(q_ref, f_ref, i_ref, g_ref, tab_ref, gain_ref, o_ref, st_ref, dho_ref,
             dq_ref, df_ref, di_ref, dg_ref, dtab_ref, dgain_ref, dstate, dlb):
        h, j = pl.program_id(0), pl.program_id(1)

        @pl.when(jnp.logical_and(h == 0, j == 0))
        def _():
            dgain_ref[...] = jnp.zeros_like(dgain_ref)

        @pl.when(j == 0)
        def _():
            dstate[...] = jnp.zeros_like(dstate)
            dlb[...] = jnp.zeros_like(dlb)

        row = lax.broadcasted_iota(jnp.int32, (CHUNK, HK), 0)
        tril = lax.broadcasted_iota(jnp.int32, (CHUNK, CHUNK), 0) >= lax.broadcasted_iota(jnp.int32, (CHUNK, CHUNK), 1)
        gain_v = gain_ref[...]

        def chunk(cc, carry):
            c = nc - 1 - cc
            sl = pl.ds(pl.multiple_of(c * CHUNK, CHUNK), CHUNK)
            dgain = jnp.zeros((1, HK), F32)
            for hh in range(hpb):
                ln = slice(hh * HK, (hh + 1) * HK)
                lb = _sig(tab_ref[0:1, ln] - tab_ref[1:2, ln])
                qr = q_ref[sl, ln]
                v = i_ref[sl, ln].astype(BF16)
                gr = g_ref[sl, ln]
                q, sg, f, k, last, e_q, e_k, e_a, e_l = _hg_gates(qr, f_ref[sl, ln], lb, row)
                o = o_ref[sl, ln]
                r = lax.rsqrt(jnp.mean(o * o, axis=-1, keepdims=True) + EPS)
                oh = o * r
                dh = dho_ref[sl, ln].astype(F32)
                dnorm = dh * _silu(gr)
                dg_ref[sl, ln] = (dh * oh * gain_v * _dsilu(gr)).astype(BF16)
                dgain = dgain + jnp.sum(dnorm * oh, axis=0, keepdims=True)
                dxh = dnorm * gain_v
                do = (r * (dxh - oh * jnp.mean(dxh * oh, axis=-1, keepdims=True))).astype(BF16)
                st0 = st_ref[hh, c]
                st0_b = st0.astype(BF16)
                ds1 = dstate[hh]
                ds1_b = ds1.astype(BF16)
                qt = (q * e_q).astype(BF16)
                kt = (k * e_k).astype(BF16)
                qd = (q * e_a).astype(BF16)
                kd = (k * e_l).astype(BF16)
                p = jnp.where(tril, _dot_nt(qt, kt), 0.0).astype(BF16)
                dp = jnp.where(tril, _dot_nt(do, v), 0.0).astype(BF16)
                dv = _dot_tn(p, do) + _dot_nt(kd, ds1_b)
                dqt = _dot(dp, kt)
                dkt = _dot_tn(dp, qt)
                dq_inter = _dot(do, st0_b) * e_a
                dk_inter = _dot(v, ds1_b) * e_l
                dq = dqt * e_q + dq_inter
                dk = dkt * e_k + dk_inter
                e_last = jnp.exp(last)
                dstate[hh] = _dot_tn(do, qd) + e_last * ds1
                dlast = (jnp.sum(k * dk_inter, axis=0, keepdims=True)
                         + e_last * jnp.sum(ds1 * st0, axis=0, keepdims=True))
                da = (qt.astype(F32) * dqt - kt.astype(F32) * dkt + q * dq_inter - k * dk_inter
                      + jnp.where(row == CHUNK - 1, dlast, 0.0))
                dlf = _rcumsum_rows(da, row)
                dfv = dlf / f - dk
                df_ref[sl, ln] = (dfv * (1.0 - lb) * sg * (1.0 - sg)).astype(BF16)
                dlb[:, ln] += jnp.sum(dfv * (1.0 - sg), axis=0, keepdims=True)
                dq_ref[sl, ln] = (dq * _dsilu(qr)).astype(BF16)
                di_ref[sl, ln] = dv.astype(BF16)
            dgain_ref[...] += dgain
            return carry

        lax.fori_loop(0, nc, chunk, 0)

        @pl.when(j == nb - 1)
        def _():
            lb = _sig(tab_ref[0:1, :] - tab_ref[1:2, :])
            d0 = dlb[...] * lb * (1.0 - lb)
            dtab_ref[0:1, :] = d0
            dtab_ref[1:2, :] = -d0

    def col(k):
        return pl.BlockSpec((bt, wide), lambda h, j, k=k: (nb - 1 - j, k * (HEADS // hpb) + h))

    tok = pl.BlockSpec((bt, wide), lambda h, j: (nb - 1 - j, h))
    return _call(
        body, name="hgrn_bwd", grid=(HEADS // hpb, nb),
        in_specs=[col(0), col(1), col(2), col(3),
                  pl.BlockSpec((2, wide), lambda h, j: (0, h)), pl.BlockSpec((1, HK), lambda h, j: (0, 0)),
                  tok, pl.BlockSpec((hpb, nc, HK, HK), lambda h, j: (h, nb - 1 - j, 0, 0)), tok],
        out_specs=[tok, tok, tok, tok,
                   pl.BlockSpec((2, wide), lambda h, j: (0, h)), pl.BlockSpec((1, HK), lambda h, j: (0, 0))],
        out_shape=[jax.ShapeDtypeStruct((t, D), BF16)] * 4
        + [jax.ShapeDtypeStruct((2, D), F32), jax.ShapeDtypeStruct((1, HK), F32)],
        scratch_shapes=[pltpu.VMEM((hpb, HK, HK), F32), pltpu.VMEM((1, wide), F32)],
        compiler_params=_cp(),
    )(p_hg, p_hg, p_hg, p_hg, tab, gain, o_raw, states, dho)


def _rope_tables(pos):
    t = pos.shape[0]
    tm = min(ROW_TM, t)
    inv = np.zeros((1, HK), np.float32)
    freq = (ROPE_THETA ** (-np.arange(0, ROPE, 2, dtype=np.float32) / ROPE)).astype(np.float32)
    inv[0, 0:ROPE // 2] = freq
    inv[0, ROPE // 2:ROPE] = freq
    sign = np.zeros((1, HK), np.float32)
    sign[0, 0:ROPE // 2] = -1.0
    sign[0, ROPE // 2:ROPE] = 1.0

    def body(pos_ref, inv_ref, sign_ref, cos_ref, sin_ref):
        ang = pos_ref[...].astype(F32) * inv_ref[...]
        cos_ref[...] = jnp.cos(ang)
        sin_ref[...] = jnp.sin(ang) * sign_ref[...]

    one = pl.BlockSpec((1, HK), lambda i: (0, 0))
    row = pl.BlockSpec((tm, HK), lambda i: (i, 0))
    return _call(
        body, name="rope_tables", grid=(t // tm,),
        in_specs=[pl.BlockSpec((tm, 1), lambda i: (i, 0)), one, one],
        out_specs=[row, row],
        out_shape=[jax.ShapeDtypeStruct((t, HK), F32)] * 2,
        compiler_params=_cp(),
    )(pos, jnp.asarray(inv), jnp.asarray(sign))


def _rope(x, cos, sin_signed):
    r = lax.broadcasted_iota(jnp.int32, (HK, HK), 0)
    c = lax.broadcasted_iota(jnp.int32, (HK, HK), 1)
    half = ROPE // 2
    swap = jnp.logical_or(jnp.logical_and(c < half, r == c + half),
                          jnp.logical_and(jnp.logical_and(c >= half, c < ROPE), r == c - half))
    return x * cos + _dot_split(x, swap.astype(BF16)) * sin_signed


def _dot_split(x, m):
    hi = x.astype(BF16)
    lo = (x - hi.astype(F32)).astype(BF16)
    return _dot(hi, m) + _dot(lo, m)


def _lane_sum(x):
    return _dot_split(x, jnp.ones((HK, HK), BF16))


def _head_norm(xn, xr):
    r = lax.rsqrt(_lane_sum(xn * xn + xr * xr) * (1.0 / QK) + EPS)
    return xn * r, xr * r, r


def _head_norm_bwd(xn, xr, g_n, g_r, dn, dr):
    hn, hr, r = _head_norm(xn, xr)
    dxn, dxr = dn * g_n, dr * g_r
    c = _lane_sum(dxn * hn + dxr * hr) * (1.0 / QK)
    return r * (dxn - hn * c), r * (dxr - hr * c), dn * hn, dr * hr


def _mla_prep_fwd(qf, kv, p_mla, cos, sin, gq, gk):
    t = qf.shape[0]
    tm = min(ROW_TM, t)

    def body(qf_ref, kv_ref, kpe_ref, cos_ref, sin_ref, gq_ref, gk_ref, q_ref, k_ref, v_ref):
        cos_v, sin_v = cos_ref[...], sin_ref[...]
        kpe = kpe_ref[...]
        for h in range(HEADS):
            lo, mid, hi = h * QKP, h * QKP + HK, (h + 1) * QKP
            qn, qr, _ = _head_norm(qf_ref[:, lo:mid], qf_ref[:, mid:hi])
            q_ref[h, :, 0:HK] = (qn * gq_ref[:, 0:HK] * (SCALE * LOG2E)).astype(BF16)
            q_ref[h, :, HK:QKP] = (_rope(qr * gq_ref[:, HK:QKP], cos_v, sin_v) * (SCALE * LOG2E)).astype(BF16)
            kn, kr, _ = _head_norm(kv_ref[:, lo:mid], kpe)
            k_ref[h, :, 0:HK] = (kn * gk_ref[:, 0:HK]).astype(BF16)
            k_ref[h, :, HK:QKP] = _rope(kr * gk_ref[:, HK:QKP], cos_v, sin_v).astype(BF16)
            v_ref[h, :, 0:HK] = kv_ref[:, mid:hi].astype(BF16)
            v_ref[h, :, HK:QKP] = jnp.full((tm, HK), -1.0, BF16)

    head = pl.BlockSpec((tm, HEADS * QKP), lambda i: (i, 0))
    tok = pl.BlockSpec((tm, HK), lambda i: (i, 0))
    gain = pl.BlockSpec((1, QKP), lambda i: (0, 0))
    return _call(
        body, name="mla_prep_fwd", grid=(t // tm,),
        in_specs=[head, head, pl.BlockSpec((tm, HK), lambda i: (i, MLA_COLS // HK - 1)), tok, tok, gain, gain],
        out_specs=[pl.BlockSpec((HEADS, tm, QKP), lambda i: (0, i, 0)),
                   pl.BlockSpec((HEADS, tm, QKP), lambda i: (0, i, 0)),
                   pl.BlockSpec((HEADS, tm, QKP), lambda i: (0, i, 0))],
        out_shape=[jax.ShapeDtypeStruct((HEADS, t, QKP), BF16), jax.ShapeDtypeStruct((HEADS, t, QKP), BF16),
                   jax.ShapeDtypeStruct((HEADS, t, QKP), BF16)],
        compiler_params=_cp(),
    )(qf, kv, p_mla, cos, sin, gq, gk)


def _mla_prep_bwd(qf, kv, p_mla, cos, sin, gq, gk, dq, dk, dv):
    t = qf.shape[0]
    tm = min(ROW_TM, t)

    def body(qf_ref, kv_ref, kpe_ref, cos_ref, sin_ref, gq_ref, gk_ref, dq_ref, dk_ref, dv_ref,
             dqf_ref, dkv_ref, dkpe_ref, dgq_ref, dgk_ref):
        @pl.when(pl.program_id(0) == 0)
        def _():
            dgq_ref[...] = jnp.zeros_like(dgq_ref)
            dgk_ref[...] = jnp.zeros_like(dgk_ref)

        cos_v, sin_v = cos_ref[...], -sin_ref[...]
        kpe = kpe_ref[...]
        gqn, gqr, gkn, gkr = gq_ref[:, 0:HK], gq_ref[:, HK:QKP], gk_ref[:, 0:HK], gk_ref[:, HK:QKP]
        dkpe = jnp.zeros((tm, HK), F32)
        dgq_n, dgq_r, dgk_n, dgk_r = [jnp.zeros((1, HK), F32) for _ in range(4)]
        for h in range(HEADS):
            lo, mid, hi = h * QKP, h * QKP + HK, (h + 1) * QKP
            dqn = dq_ref[h, :, 0:HK].astype(F32) * SCALE
            dqr = _rope(dq_ref[h, :, HK:QKP].astype(F32), cos_v, sin_v) * SCALE
            a, b, ga, gb = _head_norm_bwd(qf_ref[:, lo:mid], qf_ref[:, mid:hi], gqn, gqr, dqn, dqr)
            dqf_ref[:, lo:mid] = a.astype(BF16)
            dqf_ref[:, mid:hi] = b.astype(BF16)
            dgq_n = dgq_n + jnp.sum(ga, axis=0, keepdims=True)
            dgq_r = dgq_r + jnp.sum(gb, axis=0, keepdims=True)
            dkn = dk_ref[h, :, 0:HK].astype(F32) * LN2
            dkr = _rope(dk_ref[h, :, HK:QKP].astype(F32), cos_v, sin_v) * LN2
            a, b, ga, gb = _head_norm_bwd(kv_ref[:, lo:mid], kpe, gkn, gkr, dkn, dkr)
            dkv_ref[:, lo:mid] = a.astype(BF16)
            dkv_ref[:, mid:hi] = dv_ref[h].astype(BF16)
            dkpe = dkpe + b
            dgk_n = dgk_n + jnp.sum(ga, axis=0, keepdims=True)
            dgk_r = dgk_r + jnp.sum(gb, axis=0, keepdims=True)
        dkpe_ref[...] = dkpe
        dgq_ref[:, 0:HK] += dgq_n
        dgq_ref[:, HK:QKP] += dgq_r
        dgk_ref[:, 0:HK] += dgk_n
        dgk_ref[:, HK:QKP] += dgk_r

    head = pl.BlockSpec((tm, HEADS * QKP), lambda i: (i, 0))
    tok = pl.BlockSpec((tm, HK), lambda i: (i, 0))
    gain = pl.BlockSpec((1, QKP), lambda i: (0, 0))
    hq = pl.BlockSpec((HEADS, tm, QKP), lambda i: (0, i, 0))
    return _call(
        body, name="mla_prep_bwd", grid=(t // tm,),
        in_specs=[head, head, pl.BlockSpec((tm, HK), lambda i: (i, MLA_COLS // HK - 1)), tok, tok, gain, gain,
                  hq, hq, pl.BlockSpec((HEADS, tm, HK), lambda i: (0, i, 0))],
        out_specs=[head, head, tok, gain, gain],
        out_shape=[jax.ShapeDtypeStruct((t, HEADS * QKP), BF16), jax.ShapeDtypeStruct((t, HEADS * QKP), BF16),
                   jax.ShapeDtypeStruct((t, HK), F32), jax.ShapeDtypeStruct((1, QKP), F32),
                   jax.ShapeDtypeStruct((1, QKP), F32)],
        compiler_params=_cp(),
    )(qf, kv, p_mla, cos, sin, gq, gk, dq, dk, dv)


def _chunk_mask(row0, rows, cols):
    r = lax.broadcasted_iota(jnp.int32, (rows, cols), 0) + row0
    c = lax.broadcasted_iota(jnp.int32, (rows, cols), 1)
    return jnp.right_shift(r, 6) >= jnp.right_shift(c, 6)


def _flash_fwd(q, k, v, side=None):
    t = q.shape[1]
    tq = min(TQ, t)
    nq = t // tq
    sub = min(SUBQ, tq)
    pairs = [(i, j) for i in range(nq) for j in range(i + 1)]
    qi = jnp.asarray([p[0] for p in pairs], jnp.int32)
    kj = jnp.asarray([p[1] for p in pairs], jnp.int32)
    s_in = len(side.inputs) if side else 0
    s_out = len(side.out_shapes) if side else 0

    def body(qi_ref, kj_ref, q_ref, k_ref, v_ref, *rest):
        o_ref, lse_ref = rest[s_in:s_in + 2]
        m_s, acc_s = rest[s_in + 2 + s_out:s_in + 4 + s_out]
        side_refs = list(rest[:s_in]) + list(rest[s_in + 2:s_in + 2 + s_out]) + list(rest[s_in + 4 + s_out:])
        n = pl.program_id(1)
        i, j = qi_ref[n], kj_ref[n]
        if side:
            @pl.when(jnp.logical_and(pl.program_id(0) == 0, n == 0))
            def _():
                side.start(*side_refs)

        @pl.when(j == 0)
        def _():
            m_s[...] = jnp.full_like(m_s, NEG)
            acc_s[...] = jnp.zeros_like(acc_s)

        def step(diag):
            subs = range(tq // sub)
            width = [(r + 1) * sub if diag else tq for r in subs]
            logits = [_dot_nt(q_ref[r * sub:(r + 1) * sub, :], k_ref[0:width[r], :]) for r in subs]
            for r in subs:
                rows = slice(r * sub, (r + 1) * sub)
                cols = width[r]
                s = logits[r]
                if diag:
                    s = jnp.where(_chunk_mask(r * sub, sub, cols), s, NEG)
                m_old = m_s[rows, :]
                m_new = jnp.maximum(m_old, jnp.max(s, axis=-1, keepdims=True))
                alpha = jnp.exp2(m_old - m_new)
                p = jnp.exp2((s - jnp.tile(m_new, (1, cols // HK))).astype(BF16))
                acc_s[rows, :] = jnp.tile(alpha, (1, 2)) * acc_s[rows, :] + _dot(p, v_ref[0:cols, :])
                m_s[rows, :] = m_new

        @pl.when(j < i)
        def _():
            step(False)

        @pl.when(j == i)
        def _():
            step(True)
            l = -acc_s[:, HK:QKP]
            o_ref[...] = (acc_s[:, 0:HK] / l).astype(BF16)
            lse_ref[...] = m_s[...] + jnp.log(l) * LOG2E

        if side:
            @pl.when(jnp.logical_and(pl.program_id(0) == HEADS - 1, n == len(pairs) - 1))
            def _():
                side.finish(*side_refs)

    anywhere = pl.BlockSpec(memory_space=pl.ANY)
    grid_spec = pltpu.PrefetchScalarGridSpec(
        num_scalar_prefetch=2, grid=(HEADS, len(pairs)),
        in_specs=[pl.BlockSpec((None, tq, QKP), lambda h, n, qi, kj: (h, qi[n], 0)),
                  pl.BlockSpec((None, tq, QKP), lambda h, n, qi, kj: (h, kj[n], 0)),
                  pl.BlockSpec((None, tq, QKP), lambda h, n, qi, kj: (h, kj[n], 0))] + [anywhere] * s_in,
        out_specs=[pl.BlockSpec((tq, HK), lambda h, n, qi, kj: (qi[n], h)),
                   pl.BlockSpec((None, tq, HK), lambda h, n, qi, kj: (h, qi[n], 0))] + [anywhere] * s_out,
        scratch_shapes=[pltpu.VMEM((tq, HK), F32), pltpu.VMEM((tq, QKP), F32)] + (list(side.scratch) if side else []),
    )
    return _call(
        body, name="flash_fwd", grid_spec=grid_spec,
        out_shape=[jax.ShapeDtypeStruct((t, D), BF16), jax.ShapeDtypeStruct((HEADS, t, HK), F32)]
        + (list(side.out_shapes) if side else []),
        compiler_params=_cp(),
    )(qi, kj, q, k, v, *(side.inputs if side else []))


def _attn_do(do, o):
    t = do.shape[0]
    tm = min(TM, t)

    def body(do_ref, o_ref, d_ref):
        lane = lax.broadcasted_iota(jnp.int32, (tm, HK), 1)
        for h in range(HEADS):
            ln = slice(h * HK, (h + 1) * HK)
            dov = do_ref[:, ln]
            d = jnp.sum(dov.astype(F32) * o_ref[:, ln].astype(F32), axis=-1, keepdims=True)
            hi = d.astype(BF16).astype(F32)
            d_ref[h, :, 0:HK] = dov
            d_ref[h, :, HK:QKP] = jnp.where(lane == 0, hi, jnp.where(lane == 1, d - hi, 0.0)).astype(BF16)

    blk = pl.BlockSpec((tm, D), lambda i: (i, 0))
    return _call(
        body, name="attn_do", grid=(t // tm,),
        in_specs=[blk, blk],
        out_specs=pl.BlockSpec((HEADS, tm, QKP), lambda i: (0, i, 0)),
        out_shape=jax.ShapeDtypeStruct((HEADS, t, QKP), BF16),
        compiler_params=_cp(),
    )(do, o)


def _flash_bwd(q, k, v, lse, do):
    t = q.shape[1]
    tq = min(TQ, t)
    nq = t // tq
    sub = min(SUBQ, tq)
    pairs = [(i, j) for j in range(nq) for i in range(j, nq)]
    qi = jnp.asarray([p[0] for p in pairs], jnp.int32)
    kj = jnp.asarray([p[1] for p in pairs], jnp.int32)
    npairs = len(pairs)

    def body(qi_ref, kj_ref, q_ref, k_ref, v_ref, lse_ref, do_ref, dq_ref, dk_ref, dv_ref):
        n = pl.program_id(1)
        i, j = qi_ref[n], kj_ref[n]

        @pl.when(n == 0)
        def _():
            dq_ref[...] = jnp.zeros_like(dq_ref)

        @pl.when(i == j)
        def _():
            dk_ref[...] = jnp.zeros_like(dk_ref)
            dv_ref[...] = jnp.zeros_like(dv_ref)

        def step(diag):
            for r in range(tq // sub):
                rows = slice(r * sub, (r + 1) * sub)
                cols = (r + 1) * sub if diag else tq
                qv, kv_ = q_ref[rows, :], k_ref[0:cols, :]
                p = jnp.exp2(_dot_nt(qv, kv_) - jnp.tile(lse_ref[rows, :], (1, cols // HK)))
                if diag:
                    p = jnp.where(_chunk_mask(r * sub, sub, cols), p, 0.0)
                dp_less_delta = _dot_nt(do_ref[rows, :], v_ref[0:cols, :])
                ds = (p * dp_less_delta).astype(BF16)
                dv_ref[0:cols, :] += _dot_tn(p.astype(BF16), do_ref[rows, 0:HK])
                dk_ref[0:cols, :] += _dot_tn(ds, qv)
                dq_rows = pl.ds(pl.multiple_of(i * tq + r * sub, sub), sub)
                dq_ref[dq_rows, :] += _dot(ds, kv_)

        @pl.when(j < i)
        def _():
            step(False)

        @pl.when(j == i)
        def _():
            step(True)

    grid_spec = pltpu.PrefetchScalarGridSpec(
        num_scalar_prefetch=2, grid=(HEADS, npairs),
        in_specs=[pl.BlockSpec((None, tq, QKP), lambda h, n, qi, kj: (h, qi[n], 0)),
                  pl.BlockSpec((None, tq, QKP), lambda h, n, qi, kj: (h, kj[n], 0)),
                  pl.BlockSpec((None, tq, QKP), lambda h, n, qi, kj: (h, kj[n], 0)),
                  pl.BlockSpec((None, tq, HK), lambda h, n, qi, kj: (h, qi[n], 0)),
                  pl.BlockSpec((None, tq, QKP), lambda h, n, qi, kj: (h, qi[n], 0))],
        out_specs=[pl.BlockSpec((None, t, QKP), lambda h, n, qi, kj: (h, 0, 0)),
                   pl.BlockSpec((None, tq, QKP), lambda h, n, qi, kj: (h, kj[n], 0)),
                   pl.BlockSpec((None, tq, HK), lambda h, n, qi, kj: (h, kj[n], 0))],
    )
    return _call(
        body, name="flash_bwd", grid_spec=grid_spec,
        out_shape=[jax.ShapeDtypeStruct((HEADS, t, QKP), F32), jax.ShapeDtypeStruct((HEADS, t, QKP), F32),
                   jax.ShapeDtypeStruct((HEADS, t, HK), F32)],
        compiler_params=_cp(56),
    )(qi, kj, q, k, v, lse, do)


def _adamw(name, w, g, m, v):
    r, c = w.shape
    tr = r if r <= 256 else next(k for k in (256, 352, 384) if r % k == 0)

    def body(w_ref, g_ref, m_ref, v_ref, d_ref, nm_ref, nv_ref):
        gv = g_ref[...]
        nm = ADAM_B1 * m_ref[...] + (1.0 - ADAM_B1) * gv
        nv = ADAM_B2 * v_ref[...] + (1.0 - ADAM_B2) * (gv * gv)
        m_hat = nm / (1.0 - ADAM_B1 ** ADAM_STEP)
        v_hat = nv / (1.0 - ADAM_B2 ** ADAM_STEP)
        d_ref[...] = -ADAM_LR * (m_hat / (jnp.sqrt(v_hat) + ADAM_EPS) + ADAM_WD * w_ref[...])
        nm_ref[...] = nm
        nv_ref[...] = nv

    blk = pl.BlockSpec((tr, c), lambda i: (i, 0))
    return _call(
        body, name=name, grid=(r // tr,),
        in_specs=[blk] * 4, out_specs=[blk] * 3,
        out_shape=[jax.ShapeDtypeStruct((r, c), F32)] * 3,
        compiler_params=_cp(),
    )(w, g, m, v)


def _place():
    return lax.axis_index("x"), lax.axis_index("y"), lax.axis_index("c")


def _other_chips(x, y):
    return [(1 - x, y), (x, 1 - y), (1 - x, 1 - y)]


class _Exchange:
    inputs = ()
    out_shapes = ()
    scratch = ()

    def start(self, *refs):
        raise NotImplementedError

    def finish(self, *refs):
        raise NotImplementedError

    def alone(self, name):
        def body(*refs):
            self.start(*refs)
            self.finish(*refs)

        anywhere = pl.BlockSpec(memory_space=pl.ANY)
        return _call(
            body, name=name,
            in_specs=[anywhere] * len(self.inputs), out_specs=[anywhere] * len(self.out_shapes),
            out_shape=list(self.out_shapes), scratch_shapes=list(self.scratch),
        )(*self.inputs)


class _GatherWeights(_Exchange):
    def __init__(self, shards):
        self.inputs = tuple(shards)
        self.out_shapes = tuple(jax.ShapeDtypeStruct((4,) + s.shape, s.dtype) for s in shards)
        self.scratch = (pltpu.SemaphoreType.DMA((6 * len(shards),)), pltpu.SemaphoreType.DMA((6 * len(shards),)))

    def gathered(self, got, k):
        return [lax.dynamic_update_slice(g, s[None], (k, 0, 0)) for g, s in zip(got, self.inputs)]

    def _copies(self, *refs):
        nbuf = len(self.inputs)
        send_sems, recv_sems = refs[2 * nbuf:]
        x, y, c = _place()
        chips = _other_chips(x, y)
        first, passed, landed, relayed = [], [], [], []
        for b, (s_ref, g_ref) in enumerate(zip(refs[:nbuf], refs[nbuf:2 * nbuf])):
            half = self.inputs[b].shape[0] // 2

            def rows(px, py, pc, g_ref=g_ref, half=half):
                return g_ref.at[2 * px + py, pl.ds(pc * half, half), :]

            def copy(k, block, to, src=None, rows=rows, b=b):
                return pltpu.make_async_remote_copy(
                    src_ref=rows(*block) if src is None else src, dst_ref=rows(*block),
                    send_sem=send_sems.at[6 * b + k], recv_sem=recv_sems.at[6 * b + k], device_id=to, device_id_type=MESH)

            mine = s_ref.at[pl.ds(c * half, half), :]
            first += [copy(j, (x, y, c), (*chip, c), src=mine) for j, chip in enumerate(chips)]
            passed += [copy(3 + j, (*chip, c), (x, y, 1 - c)) for j, chip in enumerate(chips)]
            landed += [copy(j, (*chip, c), (x, y, c)) for j, chip in enumerate(chips)]
            relayed += [copy(3 + j, (*chip, 1 - c), (x, y, c)) for j, chip in enumerate(chips)]
        return first, passed, landed, relayed

    def start(self, *refs):
        for cp in self._copies(*refs)[0]:
            cp.start()

    def finish(self, *refs):
        first, passed, landed, relayed = self._copies(*refs)
        for arrived, onward in zip(landed, passed):
            arrived.wait_recv()
            onward.start()
        for cp in relayed:
            cp.wait_recv()
        for cp in first + passed:
            cp.wait_send()


class _SwapHalves(_Exchange):
    def __init__(self, bufs):
        self.inputs = tuple(bufs)
        self.out_shapes = tuple(jax.ShapeDtypeStruct((4, g.shape[1] // 2, g.shape[2]), g.dtype) for g in bufs)
        self.scratch = (pltpu.SemaphoreType.DMA((len(bufs),)), pltpu.SemaphoreType.DMA((len(bufs),)))

    def _copies(self, *refs):
        nbuf = len(self.inputs)
        send_sems, recv_sems = refs[2 * nbuf:]
        x, y, c = _place()
        cps = []
        for b, (g_ref, o_ref) in enumerate(zip(refs[:nbuf], refs[nbuf:2 * nbuf])):
            half = self.inputs[b].shape[1] // 2
            cps.append(pltpu.make_async_remote_copy(
                src_ref=g_ref.at[:, pl.ds((1 - c) * half, half), :], dst_ref=o_ref,
                send_sem=send_sems.at[b], recv_sem=recv_sems.at[b], device_id=(x, y, 1 - c), device_id_type=MESH))
        return cps

    def start(self, *refs):
        for cp in self._copies(*refs):
            cp.start()

    def finish(self, *refs):
        for cp in self._copies(*refs):
            cp.wait()


def _add_rows(half):
    return next(tr for tr in range(512, 15, -16) if half % tr == 0)


def _chip_sum(name, gp, got, c_arr):
    half, width = got.shape[1], got.shape[2]
    tr = _add_rows(half)
    nb = half // tr

    def body(c_ref, a_ref, b_ref, o_ref, ob_ref):
        s = a_ref[...] + b_ref[...]
        o_ref[...] = s
        ob_ref[...] = s.astype(BF16)

    grid_spec = pltpu.PrefetchScalarGridSpec(
        num_scalar_prefetch=1, grid=(4, nb),
        in_specs=[pl.BlockSpec((None, tr, width), lambda s, i, c: (s, c[0] * nb + i, 0)),
                  pl.BlockSpec((None, tr, width), lambda s, i, c: (s, i, 0))],
        out_specs=[pl.BlockSpec((None, tr, width), lambda s, i, c: (s, i, 0)),
                   pl.BlockSpec((None, tr, width), lambda s, i, c: (s, i, 0))],
    )
    return _call(
        body, name=name, grid_spec=grid_spec,
        out_shape=[jax.ShapeDtypeStruct(got.shape, F32), jax.ShapeDtypeStruct(got.shape, BF16)],
        compiler_params=_cp(),
    )(c_arr, gp, got)


class _ScatterChipSums(_Exchange):
    def __init__(self, sums):
        self.inputs = tuple(sums)
        self.out_shapes = tuple(jax.ShapeDtypeStruct((3,) + cs.shape[1:], cs.dtype) for cs in sums)
        self.scratch = (pltpu.SemaphoreType.DMA((3 * len(sums),)), pltpu.SemaphoreType.DMA((3 * len(sums),)))

    def _copies(self, *refs):
        nbuf = len(self.inputs)
        send_sems, recv_sems = refs[2 * nbuf:]
        x, y, c = _place()
        return [pltpu.make_async_remote_copy(
            src_ref=s_ref.at[2 * px + py], dst_ref=o_ref.at[j],
            send_sem=send_sems.at[3 * b + j], recv_sem=recv_sems.at[3 * b + j], device_id=(px, py, c), device_id_type=MESH)
            for b, (s_ref, o_ref) in enumerate(zip(refs[:nbuf], refs[nbuf:2 * nbuf]))
            for j, (px, py) in enumerate(_other_chips(x, y))]

    def start(self, *refs):
        for cp in self._copies(*refs):
            cp.start()

    def finish(self, *refs):
        for cp in self._copies(*refs):
            cp.wait()


def _shard_sum(name, cs, got, kc_arr):
    h, width = cs.shape[1], cs.shape[2]
    tr = _add_rows(h)
    nb = h // tr

    def body(k_ref, a_ref, b_ref, o_ref):
        o_ref[...] = ((a_ref[...] + b_ref[0].astype(F32)) + b_ref[1].astype(F32)) + b_ref[2].astype(F32)

    grid_spec = pltpu.PrefetchScalarGridSpec(
        num_scalar_prefetch=1, grid=(nb,),
        in_specs=[pl.BlockSpec((None, tr, width), lambda i, k: (k[0], i, 0)),
                  pl.BlockSpec((3, tr, width), lambda i, k: (0, i, 0))],
        out_specs=pl.BlockSpec((tr, width), lambda i, k: (k[1] * nb + i, 0)),
    )
    return _call(
        body, name=name, grid_spec=grid_spec,
        out_shape=jax.ShapeDtypeStruct((2 * h, width), F32),
        compiler_params=_cp(),
    )(kc_arr, cs, got)


def _join_halves(name, boths):
    nbuf = len(boths)

    def body(*refs):
        send_sems, recv_sems = refs[2 * nbuf:]
        x, y, c = _place()
        sent, landing = [], []
        for b, (m_ref, o_ref) in enumerate(zip(refs[:nbuf], refs[nbuf:2 * nbuf])):
            h = boths[b].shape[0] // 2
            mine = m_ref.at[pl.ds(c * h, h), :]
            sent.append(pltpu.make_async_remote_copy(
                src_ref=mine, dst_ref=o_ref.at[pl.ds(c * h, h), :],
                send_sem=send_sems.at[b], recv_sem=recv_sems.at[b], device_id=(x, y, 1 - c), device_id_type=MESH))
            landing.append(pltpu.make_async_remote_copy(
                src_ref=mine, dst_ref=o_ref.at[pl.ds((1 - c) * h, h), :],
                send_sem=send_sems.at[b], recv_sem=recv_sems.at[b], device_id=(x, y, 1 - c), device_id_type=MESH))
        for cp in sent:
            cp.start()
        for cp in sent:
            cp.wait_send()
        for cp in landing:
            cp.wait_recv()

    anywhere = pl.BlockSpec(memory_space=pl.ANY)
    return _call(
        body, name=name,
        in_specs=[anywhere] * nbuf, out_specs=[anywhere] * nbuf,
        out_shape=[jax.ShapeDtypeStruct(g.shape, g.dtype) for g in boths],
        input_output_aliases={b: b for b in range(nbuf)},
        scratch_shapes=[pltpu.SemaphoreType.DMA((nbuf,)), pltpu.SemaphoreType.DMA((nbuf,))],
    )(*boths)


def _all_reduce_small(v):
    r = v.shape[0]

    def body(v_ref, o_ref, buf, send_sems, recv_sems):
        x, y, c = _place()
        me = 4 * x + 2 * y + c
        buf[me] = v_ref[...]
        cps = []
        for k in range(1, 8):
            peer = (x ^ (k >> 2), y ^ ((k >> 1) & 1), c ^ (k & 1))
            cps.append(pltpu.make_async_remote_copy(
                src_ref=v_ref, dst_ref=buf.at[me],
                send_sem=send_sems.at[k - 1], recv_sem=recv_sems.at[k - 1], device_id=peer, device_id_type=MESH))
        for cp in cps:
            cp.start()
        for k in range(1, 8):
            pltpu.make_async_remote_copy(
                src_ref=v_ref, dst_ref=buf.at[me ^ k],
                send_sem=send_sems.at[k - 1], recv_sem=recv_sems.at[k - 1],
                device_id=(x, y, c), device_id_type=MESH).wait_recv()
        for cp in cps:
            cp.wait_send()
        acc = buf[0]
        for k in range(1, 8):
            acc = acc + buf[k]
        o_ref[...] = acc

    return _call(
        body, name="all_reduce_small",
        in_specs=[pl.BlockSpec(memory_space=pltpu.VMEM)],
        out_specs=pl.BlockSpec(memory_space=pltpu.VMEM),
        out_shape=jax.ShapeDtypeStruct((r, 128), F32),
        scratch_shapes=[pltpu.VMEM((8, r, 128), F32), pltpu.SemaphoreType.DMA((7,)), pltpu.SemaphoreType.DMA((7,))],
    )(v)


def _group(names):
    return tuple(e for e in BIG if e[0] in names)


def _pack(shards, dtype):
    return jnp.concatenate([s.astype(dtype).reshape(-1, PACK_W) for s in shards], axis=0)


def _unpack_full(g, group):
    out, at = {}, 0
    for name, rows, cols, axis in group:
        n = rows * cols // 4 // PACK_W
        blk = g[:, at:at + n, :]
        at += n
        if axis == 1:
            out[name] = blk.reshape(4, rows, cols // 4).transpose(1, 0, 2).reshape(rows, cols)
        else:
            out[name] = blk.reshape(rows, cols)
    return out


def _pack_grads(grads, group):
    parts = []
    for name, rows, cols, axis in group:
        g = grads[name]
        if axis == 1:
            g = g.reshape(rows, 4, cols // 4).transpose(1, 0, 2)
        parts.append(g.reshape(4, -1, PACK_W))
    rows_total = sum(p.shape[1] for p in parts)
    pad = -rows_total % PACK_ALIGN
    if pad:
        parts.append(jnp.zeros((4, pad, PACK_W), F32))
    return jnp.concatenate(parts, axis=1)


def _unpack_shard(s, group):
    out, at = {}, 0
    for name, rows, cols, axis in group:
        n = rows * cols // 4 // PACK_W
        shape = (rows, cols // 4) if axis == 1 else (rows // 4, cols)
        out[name] = s[at:at + n, :].reshape(shape)
        at += n
    return out


def _pack_small(parts):
    flat = jnp.concatenate([p.reshape(-1) for p in parts])
    pad = -flat.shape[0] % 1024
    return jnp.concatenate([flat, jnp.zeros((pad,), F32)]).reshape(-1, 128)


def _ffn_in(tag, h, gain, w_in, side=None):
    t = h.shape[0]
    wide = DFF // 2

    def compute_in(rows, weights, outs):
        hv, w_ref = rows[0][...], weights[0]
        r = lax.rsqrt(jnp.mean(hv * hv, axis=-1, keepdims=True) + EPS)
        a = (hv * r * weights[1][...]).astype(BF16)
        outs[0][...] = a

        def emit(gate, up, cols):
            outs[1][:, cols] = gate.astype(BF16)
            outs[2][:, cols] = up.astype(BF16)
            outs[3][:, cols] = (_silu(gate) * up).astype(BF16)

        for s in range(2):
            emit(_dot(a, w_ref[s, :, 0:FFN_MAIN]), _dot(a, w_ref[2 + s, :, 0:FFN_MAIN]),
                 slice(s * wide, s * wide + FFN_MAIN))
        gate = _dot(a, jnp.concatenate([w_ref[0, :, FFN_MAIN:wide], w_ref[1, :, FFN_MAIN:wide]], axis=1))
        up = _dot(a, jnp.concatenate([w_ref[2, :, FFN_MAIN:wide], w_ref[3, :, FFN_MAIN:wide]], axis=1))
        rest = wide - FFN_MAIN
        for s in range(2):
            emit(gate[:, s * rest:(s + 1) * rest], up[:, s * rest:(s + 1) * rest],
                 slice(s * wide + FFN_MAIN, (s + 1) * wide))

    return _rows_call(tag + "_in", [h], [w_in, gain], [(D, BF16)] + [(DFF, BF16)] * 3, compute_in, min(FFN_TM, t),
                      side=side)


def _ffn_out(tag, act, h, w_out, next_gain, target=None):
    t = h.shape[0]
    tm = min(FFN_TM, t)

    def compute_out(rows, weights, outs):
        hn = rows[1][...] + 0.5 * _dot(rows[0][...], weights[0][...])
        g = weights[1][...]
        r = lax.rsqrt(jnp.mean(hn * hn, axis=-1, keepdims=True) + EPS)
        xh = hn * r
        if target is None:
            outs[0][...] = hn
            outs[1][...] = (xh * g).astype(BF16)
        else:
            err = xh * g - rows[2][...]
            dy = err * (1.0 / D)
            dxh = dy * g
            outs[0][...] = r * (dxh - xh * jnp.mean(dxh * xh, axis=-1, keepdims=True))
            outs[1][...] += jnp.sum(dy * xh, axis=0, keepdims=True)
            outs[2][...] += 0.5 * jnp.sum(jnp.mean(err * err, axis=-1, keepdims=True), axis=0, keepdims=True)

    if target is None:
        return _rows_call(tag + "_out", [act, h], [w_out, next_gain], [(D, F32), (D, BF16)], compute_out, tm)
    return _rows_call(tag + "_out", [act, h, target], [w_out, next_gain], [(D, F32)], compute_out, tm, sums=(D, 128))


class _Reduction:
    def __init__(self, tag, c_arr, k_arr):
        self.tag, self.c_arr, self.k_arr = tag, c_arr, k_arr

    def begin(self, bufs, swapped=None):
        if swapped is None:
            swapped = _SwapHalves(bufs).alone("grad_swap_" + self.tag)
        sums = [_chip_sum("grad_chip_sum_%s%d" % (self.tag, b), gp, got, self.c_arr)
                for b, (gp, got) in enumerate(zip(bufs, swapped))]
        self.sums = [s[0] for s in sums]
        return _ScatterChipSums([s[1] for s in sums])

    def end(self, got):
        mine = [_shard_sum("grad_shard_sum_%s%d" % (self.tag, b), cs, g, self.k_arr)
                for b, (cs, g) in enumerate(zip(self.sums, got))]
        return _join_halves("grad_join_" + self.tag, mine)


def _ffn_bwd(tag, h, gain, w_in, w_out, saved, dout, side, reduction):
    t = h.shape[0]
    tm = min(TM, t)
    n, gate, up, act = saved

    def compute(rows, weights, outs):
        d = rows[0][...].astype(BF16)
        for j in range(DFF // FFN_CHUNK):
            cols = slice(j * FFN_CHUNK, (j + 1) * FFN_CHUNK)
            da = 0.5 * _dot_nt(d, weights[0][cols, :])
            g, u = rows[1][:, cols].astype(F32), rows[2][:, cols].astype(F32)
            s = _sig(g)
            silu = g * s
            outs[0][:, cols] = (da * u * (s + silu * (1.0 - s))).astype(BF16)
            outs[1][:, cols] = (da * silu).astype(BF16)

    dgate, dup, *side_out = _rows_call(tag + "_dact", [dout, gate, up], [w_out], [(DFF, BF16)] * 2, compute,
                                       min(FFN_TM, t), side=side)
    dw_in = _mm_tn(tag + "_dw_gate", n, dgate, tm=D, tn=DFF // 2, stacked=(4, 0))
    dw_in = _mm_tn(tag + "_dw_up", n, dup, tm=D, tn=DFF // 2, stacked=(4, 2), into=dw_in)
    dw_out, dw_in_swapped = _mm_tn(tag + "_dw_out", act, dout, scale=0.5, tm=DFF // 2, tn=D, side=_SwapHalves([dw_in]))
    dw_out = dw_out.reshape(4, DFF // 4, D)
    (dw_out_swapped,) = _SwapHalves([dw_out]).alone("grad_swap_" + tag)
    sending = reduction.begin([dw_in, dw_out], [dw_in_swapped, dw_out_swapped])

    def compute_dn(rows, weights, outs):
        w_ref = weights[0]
        wide = DFF // 2
        dn = jnp.zeros((rows[0].shape[0], D), F32)
        for s in range(2):
            cols = slice(s * wide, s * wide + FFN_MAIN)
            dn = (dn + _dot_nt(rows[0][:, cols], w_ref[s, :, 0:FFN_MAIN])
                  + _dot_nt(rows[1][:, cols], w_ref[2 + s, :, 0:FFN_MAIN]))
        for r, first in ((0, 0), (1, 2)):
            x = jnp.concatenate([rows[r][:, FFN_MAIN:wide], rows[r][:, wide + FFN_MAIN:2 * wide]], axis=1)
            wt = jnp.concatenate([w_ref[first, :, FFN_MAIN:wide], w_ref[first + 1, :, FFN_MAIN:wide]], axis=1)
            dn = dn + _dot_nt(x, wt)
        dx, dg = _rms_bwd_vals(rows[2][...], weights[1][...], dn)
        outs[0][...] = rows[3][...] + dx
        outs[1][...] += jnp.sum(dg, axis=0, keepdims=True)

    dh, dgain, *got = _rows_call(tag + "_dn", [dgate, dup, h, dout], [w_in, gain], [(D, F32)], compute_dn,
                                 min(FFN_TM, t), side=sending, sums=(D,), vmem_mb=58)
    return dh, dgain, side_out, got


def kernel(x, positions, ffn1_norm, ffn1_w_in, ffn1_w_out, mix_norm, w_in, hg_lb_table, hg_out_norm, w_hg_branch, mla_q_lora_norm, w_q_up, mla_kv_lora_norm, w_kv_up, q_head_norm, k_head_norm, w_mla_branch, w_merge, b_merge, w_out, ffn2_norm, ffn2_w_in, ffn2_w_out, final_norm, loss_target, m_ffn1_norm, m_ffn1_w_in, m_ffn1_w_out, m_mix_norm, m_w_in, m_hg_lb_table, m_hg_out_norm, m_w_hg_branch, m_mla_q_lora_norm, m_w_q_up, m_mla_kv_lora_norm, m_w_kv_up, m_q_head_norm, m_k_head_norm, m_w_mla_branch, m_w_merge, m_b_merge, m_w_out, m_ffn2_norm, m_ffn2_w_in, m_ffn2_w_out, m_final_norm, v_ffn1_norm, v_ffn1_w_in, v_ffn1_w_out, v_mix_norm, v_w_in, v_hg_lb_table, v_hg_out_norm, v_w_hg_branch, v_mla_q_lora_norm, v_w_q_up, v_mla_kv_lora_norm, v_w_kv_up, v_q_head_norm, v_k_head_norm, v_w_mla_branch, v_w_merge, v_b_merge, v_w_out, v_ffn2_norm, v_ffn2_w_in, v_ffn2_w_out, v_final_norm):
    a = dict(locals())
    w = {n: a[n] for n in WEIGHT_ORDER}
    mom = {n: a["m_" + n] for n in WEIGHT_ORDER}
    var = {n: a["v_" + n] for n in WEIGHT_ORDER}
    t = x.shape[1]
    tm = min(TM, t)
    xt = x.reshape(t, D)
    target = loss_target.reshape(t, D)
    pos = positions.reshape(t, 1)
    x_i, y_i, c_i = _place()
    k_idx = (2 * x_i + y_i).astype(jnp.int32)
    c_arr = c_i.astype(jnp.int32).reshape(1)
    k_arr = jnp.stack([k_idx, c_i.astype(jnp.int32)])

    group_mid = _group(("w_in", "w_hg_branch", "w_q_up", "w_kv_up", "w_mla_branch", "w_merge", "w_out"))
    use_early = _group(("ffn1_w_out", "w_in", "w_hg_branch", "w_q_up", "w_kv_up"))
    use_late = _group(("w_mla_branch", "w_merge", "w_out", "ffn2_w_out"))
    gather_first = _GatherWeights([w["ffn1_w_in"][0].astype(BF16)])
    gather_early = _GatherWeights([_pack([w[e[0]][0] for e in use_early], BF16)])
    gather_late = _GatherWeights([_pack([w[e[0]][0] for e in use_late], BF16), w["ffn2_w_in"][0].astype(BF16)])
    (ffn1_w_in_g,) = gather_first.gathered(gather_first.alone("gather_first"), k_idx)
    n1, gate1, up1, act1, got = _ffn_in("ffn1", xt, w["ffn1_norm"], ffn1_w_in_g, gather_early)
    full = _unpack_full(gather_early.gathered([got], k_idx)[0], use_early)
    h1, u = _ffn_out("ffn1", act1, xt, full["ffn1_w_out"], w["mix_norm"])
    ffn1_saved = (n1, gate1, up1, act1)
    w_in_full = full["w_in"]
    w_in_hg = w_in_full[:, :4 * D]
    w_in_mla = jnp.pad(w_in_full[:, 4 * D:], ((0, 0), (0, MLA_COLS - (4800 - 4 * D))))
    w_q_pad = jnp.pad(full["w_q_up"].reshape(Q_LORA, HEADS, QK), ((0, 0), (0, 0), (0, QKP - QK))).reshape(Q_LORA, HEADS * QKP)
    w_kv = full["w_kv_up"]
    gq = jnp.pad(w["q_head_norm"], ((0, 0), (0, QKP - QK)))
    gk = jnp.pad(w["k_head_norm"], ((0, 0), (0, QKP - QK)))

    ident = lambda accs, ex: (accs[0],)
    def in_hg(rows, weights, outs):
        a = rows[0][...]
        for j in range(4 * D // 512):
            cols = slice(j * 512, (j + 1) * 512)
            outs[0][:, cols] = _dot(a, weights[0][:, cols])

    (p_hg,) = _rows_call("in_hg", [u], [w_in_hg], [(4 * D, F32)], in_hg, min(FFN_TM, t))
    p_mla, cqn, ckvn = _in_mla(u, w_in_mla, w["mla_q_lora_norm"], w["mla_kv_lora_norm"])
    o_raw, hg_o, states = _hgrn_fwd(p_hg, w["hg_lb_table"], w["hg_out_norm"])
    (y_hg,) = _mm("hg_branch", [_a_spec(hg_o, tm)], [_b_nn(full["w_hg_branch"], 512)], [(0, 0)], ident, [], [BF16], t, D, tm, 512)
    (qf,) = _mm("q_up", [_a_spec(cqn, tm)], [_b_nn(w_q_pad, 512)], [(0, 0)], ident, [], [F32], t, HEADS * QKP, tm, 512)
    (kvf,) = _mm("kv_up", [_a_spec(ckvn, tm)], [_b_nn(w_kv, 512)], [(0, 0)], ident, [], [F32], t, HEADS * QKP, tm, 512)
    cos, sin = _rope_tables(pos)
    qh, kh, vh = _mla_prep_fwd(qf, kvf, p_mla, cos, sin, gq, gk)
    o_mla, lse, *got = _flash_fwd(qh, kh, vh, side=gather_late)
    late, ffn2_w_in_g = gather_late.gathered(got, k_idx)
    full.update(_unpack_full(late, use_late))
    (y_mla,) = _mm("mla_branch", [_a_spec(o_mla, tm)], [_b_nn(full["w_mla_branch"], 512)], [(0, 0)], ident, [], [BF16], t, D, tm, 512)

    def merge_epi(accs, ex):
        g_hg = _sig(accs[0] + ex[2])
        g_mla = _sig(accs[1] + ex[3])
        return g_hg * ex[0].astype(F32) + g_mla * ex[1].astype(F32), g_hg, g_mla

    w_merge_f = full["w_merge"]
    mix, g_hg, g_mla = _mm(
        "merge", [_a_spec(u, tm)], [_b_nn(w_merge_f, 512), _b_nn(w_merge_f, 512, D // 512)], [(0, 0), (0, 1)], merge_epi,
        [_e_tile(y_hg, tm, 512), _e_tile(y_mla, tm, 512), _e_row(w["b_merge"], 512), _e_row(w["b_merge"], 512, D // 512)],
        [BF16, BF16, BF16], t, D, tm, 512)
    (h2,) = _mm("out_proj", [_a_spec(mix, tm)], [_b_nn(full["w_out"], 512)], [(0, 0)],
                lambda accs, ex: (ex[0] + accs[0],), [_e_tile(h1, tm, 512)], [F32], t, D, tm, 512)
    ffn2_saved = _ffn_in("ffn2", h2, w["ffn2_norm"], ffn2_w_in_g)
    dh3, d_final_norm, loss_part = _ffn_out("ffn2", ffn2_saved[3], h2, full["ffn2_w_out"], w["final_norm"], target=target)

    grads, small = {}, {}
    small["final_norm"] = d_final_norm
    reduce_last = _Reduction("last", c_arr, k_arr)
    reduce_mid = _Reduction("mid", c_arr, k_arr)
    reduce_first = _Reduction("first", c_arr, k_arr)
    dh2, small["ffn2_norm"], _, got_last = _ffn_bwd(
        "ffn2", h2, w["ffn2_norm"], ffn2_w_in_g, full["ffn2_w_out"], ffn2_saved, dh3, None, reduce_last)

    def dmix_epi(accs, ex):
        dm = accs[0]
        ghg, gml, yhg, yml = [e.astype(F32) for e in ex]
        return dm * ghg, dm * gml, dm * yhg * ghg * (1.0 - ghg), dm * yml * gml * (1.0 - gml)

    dy_hg, dy_mla, dpre_hg, dpre_mla = _mm(
        "d_mix", [_a_spec(dh2, tm)], [_b_nt(full["w_out"], 512)], [(0, 0)], dmix_epi,
        [_e_tile(g_hg, tm, 512), _e_tile(g_mla, tm, 512), _e_tile(y_hg, tm, 512), _e_tile(y_mla, tm, 512)],
        [BF16, BF16, BF16, BF16], t, D, tm, 512, trans_b=True)
    grads["w_out"] = _mm_tn("dw_out", mix, dh2)
    small["b_merge"] = jnp.concatenate([_colsum("db_hg", dpre_hg), _colsum("db_mla", dpre_mla)], axis=1)
    grads["w_merge"] = jnp.concatenate([_mm_tn("dw_merge_hg", u, dpre_hg), _mm_tn("dw_merge_mla", u, dpre_mla)], axis=1)
    grads["w_hg_branch"] = _mm_tn("dw_hg_branch", hg_o, dy_hg)
    grads["w_mla_branch"] = _mm_tn("dw_mla_branch", o_mla, dy_mla)
    (dho,) = _mm("d_hg_o", [_a_spec(dy_hg, tm)], [_b_nt(full["w_hg_branch"], 512)], [(0, 0)], ident, [], [BF16], t, D, tm, 512, trans_b=True)
    (do_mla,) = _mm("d_o_mla", [_a_spec(dy_mla, tm)], [_b_nt(full["w_mla_branch"], 512)], [(0, 0)], ident, [], [BF16], t, D, tm, 512, trans_b=True)

    dq_raw, df_raw, di_raw, dg_raw, small["hg_lb_table"], small["hg_out_norm"] = _hgrn_bwd(
        p_hg, w["hg_lb_table"], w["hg_out_norm"], o_raw, states, dho)
    dp_hg = [dq_raw, df_raw, di_raw, dg_raw]

    dqh, dkh, dvh = _flash_bwd(qh, kh, vh, lse, _attn_do(do_mla, o_mla))
    dqf, dkvf, dkpe, dgq, dgk = _mla_prep_bwd(qf, kvf, p_mla, cos, sin, gq, gk, dqh, dkh, dvh)
    small["q_head_norm"] = dgq[:, :QK]
    small["k_head_norm"] = dgk[:, :QK]
    dwq_pad = _mm_tn("dw_q_up", cqn, dqf, tm=Q_LORA, tn=1024)
    grads["w_q_up"] = dwq_pad.reshape(Q_LORA, HEADS, QKP)[:, :, :QK].reshape(Q_LORA, HEADS * QK)
    grads["w_kv_up"] = _mm_tn("dw_kv_up", ckvn, dkvf, tm=KV_LORA, tn=1024)
    (dcqn,) = _mm("d_cq", [_a_spec(dqf, tm)], [_b_nt(w_q_pad, Q_LORA)], [(0, 0)], ident, [], [F32], t, Q_LORA, tm, Q_LORA, trans_b=True)
    (dckvn,) = _mm("d_ckv", [_a_spec(dkvf, tm)], [_b_nt(w_kv, KV_LORA)], [(0, 0)], ident, [], [F32], t, KV_LORA, tm, KV_LORA, trans_b=True)
    dp_mla, small["mla_q_lora_norm"], small["mla_kv_lora_norm"] = _lora_norm_bwd(
        p_mla, w["mla_q_lora_norm"], w["mla_kv_lora_norm"], dcqn, dckvn, dkpe)

    dw_in_hg = [_mm_tn("dw_in_hg%d" % k, u, dp_hg[k]) for k in range(4)]
    dw_in_mla = _mm_tn("dw_in_mla", u, dp_mla, tn=MLA_COLS)
    grads["w_in"] = jnp.concatenate(dw_in_hg + [dw_in_mla[:, :4800 - 4 * D]], axis=1)
    tm_du = min(TM // 2, t)
    du, *got_mid = _mm(
        "d_u",
        [_a_spec(dpre_hg, tm_du), _a_spec(dpre_mla, tm_du)] + [_a_spec(d, tm_du) for d in dp_hg] + [_a_spec(dp_mla, tm_du)],
        [_b_nt(w_merge_f, 512, D, 0), _b_nt(w_merge_f, 512, D, 1)]
        + [_b_nt(w_in_hg, 512, D, k) for k in range(4)] + [_b_nt(w_in_mla, 512)],
        [(k, k) for k in range(7)],
        lambda accs, ex: (functools.reduce(lambda p, q: p + q, accs),), [], [F32], t, D, tm_du, 512, trans_b=True,
        side=reduce_mid.begin([_pack_grads(grads, group_mid)]))
    dh1, small["mix_norm"] = _rms_bwd("mix_dnorm", h1, w["mix_norm"], du, dh2)
    dx, small["ffn1_norm"], _, got_first = _ffn_bwd(
        "ffn1", xt, w["ffn1_norm"], ffn1_w_in_g, full["ffn1_w_out"], ffn1_saved, dh1, None, reduce_first)

    g_shard = _unpack_shard(reduce_mid.end(got_mid)[0], group_mid)
    g_shard["ffn2_w_in"], g_shard["ffn2_w_out"] = reduce_last.end(got_last)
    g_shard["ffn1_w_in"], g_shard["ffn1_w_out"] = reduce_first.end(got_first)
    small_sum = _all_reduce_small(_pack_small([small[n] for n, _ in SMALL] + [loss_part])).reshape(-1)
    g_small, at = {}, 0
    for n, shape in SMALL:
        size = shape[0] * shape[1]
        g_small[n] = small_sum[at:at + size].reshape(shape)
        at += size
    loss = small_sum[at]

    g_out, d_out, m_out, v_out = {}, {}, {}, {}
    for n in WEIGHT_ORDER:
        shape = w[n].shape
        g = g_shard[n] if n in g_shard else g_small[n]
        two = g.shape
        d_, m_, v_ = _adamw("adamw_" + n, w[n].reshape(two), g, mom[n].reshape(two), var[n].reshape(two))
        g_out[n], d_out[n], m_out[n], v_out[n] = g.reshape(shape), d_.reshape(shape), m_.reshape(shape), v_.reshape(shape)

    return (loss, dx.reshape(x.shape), *[g_out[n] for n in WEIGHT_ORDER], *[d_out[n] for n in WEIGHT_ORDER],
            *[m_out[n] for n in WEIGHT_ORDER], *[v_out[n] for n in WEIGHT_ORDER])
```

```python
import functools

import numpy as np
import jax
import jax.numpy as jnp
from jax import lax
from jax.experimental import pallas as pl
from jax.experimental.pallas import tpu as pltpu

F32 = jnp.float32
BF16 = jnp.bfloat16
MESH = pl.DeviceIdType.MESH

D = 1024
DFF = 2816
HEADS = 8
HK = 128
CHUNK = 64
ROPE = 64
QK = 192
QKP = 256
Q_LORA = 384
KV_LORA = 256
MLA_COLS = 768
EPS = 1e-6
ROPE_THETA = 10000.0
SCALE = QK ** -0.5
LOG2E = 1.4426950408889634
LN2 = 0.6931471805599453
NEG = -1e30
EXP_CLAMP = 80.0

ADAM_LR = 0.001
ADAM_B1 = 0.9
ADAM_B2 = 0.999
ADAM_EPS = 1e-08
ADAM_WD = 0.01
ADAM_STEP = 10

PACK_W = 1024
ADD_ROWS = 352
PACK_ALIGN = 2 * ADD_ROWS

TM = 1024
FFN_TM = 512
FFN_CHUNK = 256
FFN_MAIN = 1280
TQ = 2048
SUBQ = 256
HG_BT = 512
HG_HPB = 8
TT = 2048
ROW_TM = 256

VMEM_MB = 48

BIG = (
    ("ffn1_w_in", D, 2 * DFF, 1),
    ("ffn1_w_out", DFF, D, 0),
    ("w_in", D, 4800, 1),
    ("w_hg_branch", D, D, 0),
    ("w_q_up", Q_LORA, HEADS * QK, 1),
    ("w_kv_up", KV_LORA, HEADS * 2 * HK, 1),
    ("w_mla_branch", D, D, 0),
    ("w_merge", D, 2 * D, 1),
    ("w_out", D, D, 0),
    ("ffn2_w_in", D, 2 * DFF, 1),
    ("ffn2_w_out", DFF, D, 0),
)
SMALL = (
    ("ffn1_norm", (1, D)),
    ("mix_norm", (1, D)),
    ("hg_lb_table", (2, D)),
    ("hg_out_norm", (1, HK)),
    ("mla_q_lora_norm", (1, Q_LORA)),
    ("mla_kv_lora_norm", (1, KV_LORA)),
    ("q_head_norm", (1, QK)),
    ("k_head_norm", (1, QK)),
    ("b_merge", (1, 2 * D)),
    ("ffn2_norm", (1, D)),
    ("final_norm", (1, D)),
)
WEIGHT_ORDER = ("ffn1_norm", "ffn1_w_in", "ffn1_w_out", "mix_norm", "w_in", "hg_lb_table", "hg_out_norm",
                "w_hg_branch", "mla_q_lora_norm", "w_q_up", "mla_kv_lora_norm", "w_kv_up", "q_head_norm",
                "k_head_norm", "w_mla_branch", "w_merge", "b_merge", "w_out", "ffn2_norm", "ffn2_w_in",
                "ffn2_w_out", "final_norm")


def _call(body, **kw):
    return pl.pallas_call(body, **kw)


def _cp(vmem_mb=VMEM_MB):
    return pltpu.CompilerParams(vmem_limit_bytes=vmem_mb << 20)


def _dot(a, b):
    return lax.dot_general(a, b, (((1,), (0,)), ((), ())), preferred_element_type=F32)


def _dot_nt(a, b):
    return lax.dot_general(a, b, (((1,), (1,)), ((), ())), preferred_element_type=F32)


def _dot_tn(a, b):
    return lax.dot_general(a, b, (((0,), (0,)), ((), ())), preferred_element_type=F32)


def _sig(x):
    return jax.nn.sigmoid(x)


def _silu(x):
    return x * _sig(x)


def _dsilu(x):
    s = _sig(x)
    return s * (1.0 + x * (1.0 - s))


def _a_spec(arr, tm, kblk=None, kidx=0):
    kb = arr.shape[1] if kblk is None else kblk
    return arr, pl.BlockSpec((tm, kb), lambda i, j, kidx=kidx: (i, kidx)), slice(kidx * kb, (kidx + 1) * kb)


def _b_nn(arr, tn, off=0):
    return arr, pl.BlockSpec((arr.shape[0], tn), lambda i, j, off=off: (0, j + off)), ("cols", off)


def _b_nt(arr, tn, kblk=None, kidx=0):
    kb = arr.shape[1] if kblk is None else kblk
    return arr, pl.BlockSpec((tn, kb), lambda i, j, kidx=kidx: (j, kidx)), ("rows", slice(kidx * kb, (kidx + 1) * kb))


def _e_tile(arr, tm, tn, off=0):
    return arr, pl.BlockSpec((tm, tn), lambda i, j, off=off: (i, j + off)), ("tile", off)


def _e_row(arr, tn, off=0):
    return arr, pl.BlockSpec((1, tn), lambda i, j, off=off: (0, j + off)), ("row", off)


def _mm_resident(name, As, Bs, dots, epi, extras, out_dtypes, m, n, tn):
    def unique(arrays):
        seen = []
        for a in arrays:
            if not any(a is s for s in seen):
                seen.append(a)
        return seen

    rows = unique([a for a, _, _ in As] + [e for e, _, where in extras if where[0] == "tile"])
    weights = unique([b for b, _, _ in Bs] + [e for e, _, where in extras if where[0] == "row"])

    def ref_of(arr, row_refs, weight_refs):
        for r, ref in zip(rows, row_refs):
            if r is arr:
                return ref
        for wt, ref in zip(weights, weight_refs):
            if wt is arr:
                return ref

    def compute(row_refs, weight_refs, out_refs):
        a_vals = [ref_of(a, row_refs, weight_refs)[:, ks].astype(BF16) for a, _, ks in As]
        for j in range(n // tn):
            accs = []
            for ai, bi in dots:
                b, _, where = Bs[bi]
                b_ref = ref_of(b, row_refs, weight_refs)
                if where[0] == "cols":
                    accs.append(_dot(a_vals[ai], b_ref[:, (j + where[1]) * tn:(j + where[1] + 1) * tn]))
                else:
                    accs.append(_dot_nt(a_vals[ai], b_ref[j * tn:(j + 1) * tn, where[1]]))
            ex = [ref_of(e, row_refs, weight_refs)[:, (j + where[1]) * tn:(j + where[1] + 1) * tn]
                  for e, _, where in extras]
            for o_ref, o in zip(out_refs, epi(accs, ex)):
                o_ref[:, j * tn:(j + 1) * tn] = o.astype(o_ref.dtype)

    return _rows_call(name, rows, weights, [(n, dt) for dt in out_dtypes], compute, min(FFN_TM, m))


def _mm(name, As, Bs, dots, epi, extras, out_dtypes, m, n, tm, tn, trans_b=False, side=None):
    if side is None:
        return _mm_resident(name, As, Bs, dots, epi, extras, out_dtypes, m, n, tn)
    na, nb, ne, no = len(As), len(Bs), len(extras), len(out_dtypes)
    ni, nj = m // tm, n // tn
    s_in = len(side.inputs) if side else 0
    s_out = len(side.out_shapes) if side else 0

    def body(*refs):
        a_refs = refs[:na]
        b_refs = refs[na:na + nb]
        e_refs = refs[na + nb:na + nb + ne]
        at = na + nb + ne
        side_refs = refs[at:at + s_in]
        o_refs = refs[at + s_in:at + s_in + no]
        side_refs = list(side_refs) + list(refs[at + s_in + no:])
        if side:
            i, j = pl.program_id(0), pl.program_id(1)

            @pl.when(jnp.logical_and(i == 0, j == 0))
            def _():
                side.start(*side_refs)

        a_vals = [r[...].astype(BF16) for r in a_refs]
        accs = []
        for ai, bi in dots:
            b = b_refs[bi][...]
            accs.append(_dot_nt(a_vals[ai], b) if trans_b else _dot(a_vals[ai], b))
        outs = epi(accs, [r[...] for r in e_refs])
        for o_ref, o in zip(o_refs, outs):
            o_ref[...] = o.astype(o_ref.dtype)
        if side:
            @pl.when(jnp.logical_and(i == ni - 1, j == nj - 1))
            def _():
                side.finish(*side_refs)

    ops = list(As) + list(Bs) + list(extras)
    anywhere = pl.BlockSpec(memory_space=pl.ANY)
    res = _call(
        body, name=name,
        grid=(ni, nj),
        in_specs=[op[1] for op in ops] + [anywhere] * s_in,
        out_specs=[pl.BlockSpec((tm, tn), lambda i, j: (i, j)) for _ in out_dtypes] + [anywhere] * s_out,
        out_shape=[jax.ShapeDtypeStruct((m, n), dt) for dt in out_dtypes] + (list(side.out_shapes) if side else []),
        scratch_shapes=list(side.scratch) if side else [],
        compiler_params=_cp(),
    )(*[op[0] for op in ops], *(side.inputs if side else []))
    return res


def _rows_call(name, rows, weights, outs, compute, tm, side=None, sums=(), vmem_mb=VMEM_MB):
    t = rows[0].shape[0]
    nr, nw, no = len(rows), len(weights), len(outs) + len(sums)
    ni = t // tm
    s_in = len(side.inputs) if side else 0
    s_out = len(side.out_shapes) if side else 0

    def body(*refs):
        at = nr + nw
        side_refs = list(refs[at:at + s_in]) + list(refs[at + s_in + no:])
        if side:
            @pl.when(pl.program_id(0) == 0)
            def _():
                side.start(*side_refs)

        out_refs = refs[at + s_in:at + s_in + no]
        if sums:
            @pl.when(pl.program_id(0) == 0)
            def _():
                for r in out_refs[len(outs):]:
                    r[...] = jnp.zeros_like(r)

        compute(refs[:nr], refs[nr:at], out_refs)
        if side:
            @pl.when(pl.program_id(0) == ni - 1)
            def _():
                side.finish(*side_refs)

    anywhere = pl.BlockSpec(memory_space=pl.ANY)
    return _call(
        body, name=name, grid=(ni,),
        in_specs=[pl.BlockSpec((tm, r.shape[1]), lambda i: (i, 0)) for r in rows]
        + [pl.BlockSpec(wt.shape, lambda i, nd=wt.ndim: (0,) * nd) for wt in weights] + [anywhere] * s_in,
        out_specs=[pl.BlockSpec((tm, width), lambda i: (i, 0)) for width, _ in outs]
        + [pl.BlockSpec((1, width), lambda i: (0, 0)) for width in sums] + [anywhere] * s_out,
        out_shape=[jax.ShapeDtypeStruct((t, width), dt) for width, dt in outs]
        + [jax.ShapeDtypeStruct((1, width), F32) for width in sums] + (list(side.out_shapes) if side else []),
        scratch_shapes=list(side.scratch) if side else [],
        compiler_params=_cp(vmem_mb),
    )(*rows, *weights, *(side.inputs if side else []))


def _mm_tn(name, a, b, scale=1.0, tm=1024, tn=1024, stacked=None, into=None, side=None):
    t, m = a.shape
    n = b.shape[1]
    tm, tn, tt = min(tm, m), min(tn, n), min(TT, t)
    ni, nj, nk = m // tm, n // tn, t // tt
    extra_in = [into] if into is not None else list(side.inputs) if side else []
    s_out = len(side.out_shapes) if side else 0

    def body(a_ref, b_ref, *rest):
        o_ref = rest[len(extra_in)]
        i, j, k = pl.program_id(0), pl.program_id(1), pl.program_id(2)
        if side:
            side_refs = list(rest[:len(extra_in)]) + list(rest[len(extra_in) + 1:])

            @pl.when(jnp.logical_and(jnp.logical_and(i == 0, j == 0), k == 0))
            def _():
                side.start(*side_refs)

        @pl.when(k == 0)
        def _():
            o_ref[...] = jnp.zeros_like(o_ref)

        o_ref[...] += _dot_tn(a_ref[...].astype(BF16), b_ref[...].astype(BF16))
        if scale != 1.0:
            @pl.when(k == nk - 1)
            def _():
                o_ref[...] = o_ref[...] * scale
        if side:
            @pl.when(jnp.logical_and(jnp.logical_and(i == ni - 1, j == nj - 1), k == nk - 1))
            def _():
                side.finish(*side_refs)

    anywhere = pl.BlockSpec(memory_space=pl.ANY)
    product = jax.ShapeDtypeStruct((stacked[0], m, tn) if stacked else (m, n), F32)
    res = _call(
        body, name=name,
        grid=(ni, nj, nk),
        in_specs=[pl.BlockSpec((tt, tm), lambda i, j, k: (k, i)), pl.BlockSpec((tt, tn), lambda i, j, k: (k, j))]
        + [anywhere] * len(extra_in),
        out_specs=[pl.BlockSpec((None, tm, tn), lambda i, j, k: (stacked[1] + j, i, 0)) if stacked
                   else pl.BlockSpec((tm, tn), lambda i, j, k: (i, j))] + [anywhere] * s_out,
        out_shape=[product] + (list(side.out_shapes) if side else []),
        input_output_aliases={2: 0} if into is not None else {},
        scratch_shapes=list(side.scratch) if side else [],
        compiler_params=_cp(),
    )(a, b, *extra_in)
    return res if side else res[0]


def _rms_bwd_vals(xv, g, dn):
    r = lax.rsqrt(jnp.mean(xv * xv, axis=-1, keepdims=True) + EPS)
    xh = xv * r
    dxh = dn * g
    c = jnp.mean(dxh * xh, axis=-1, keepdims=True)
    return r * (dxh - xh * c), dn * xh


def _rms_bwd(name, x, gain, dn, dres):
    t, d = x.shape
    tm = min(ROW_TM, t)

    def body(x_ref, g_ref, dn_ref, dr_ref, dx_ref, dg_ref):
        @pl.when(pl.program_id(0) == 0)
        def _():
            dg_ref[...] = jnp.zeros_like(dg_ref)

        dx, dg = _rms_bwd_vals(x_ref[...], g_ref[...], dn_ref[...].astype(F32))
        dx_ref[...] = dr_ref[...] + dx
        dg_ref[...] += jnp.sum(dg, axis=0, keepdims=True)

    row = pl.BlockSpec((tm, d), lambda i: (i, 0))
    one = pl.BlockSpec((1, d), lambda i: (0, 0))
    return _call(
        body, name=name, grid=(t // tm,),
        in_specs=[row, one, row, row],
        out_specs=[row, one],
        out_shape=[jax.ShapeDtypeStruct((t, d), F32), jax.ShapeDtypeStruct((1, d), F32)],
        compiler_params=_cp(),
    )(x, gain, dn, dres)


def _colsum(name, x):
    t, n = x.shape
    tm = min(TM, t)

    def body(x_ref, o_ref):
        @pl.when(pl.program_id(0) == 0)
        def _():
            o_ref[...] = jnp.zeros_like(o_ref)

        o_ref[...] += jnp.sum(x_ref[...].astype(F32), axis=0, keepdims=True)

    return _call(
        body, name=name, grid=(t // tm,),
        in_specs=[pl.BlockSpec((tm, n), lambda i: (i, 0))],
        out_specs=pl.BlockSpec((1, n), lambda i: (0, 0)),
        out_shape=jax.ShapeDtypeStruct((1, n), F32),
        compiler_params=_cp(),
    )(x)


def _in_mla(u, w_in_mla, gq, gkv):
    t = u.shape[0]

    def compute(rows, weights, outs):
        p = _dot(rows[0][...], weights[0][...])
        outs[0][...] = p
        cq = p[:, 0:Q_LORA]
        ckv = p[:, Q_LORA:Q_LORA + KV_LORA]
        rq = lax.rsqrt(jnp.mean(cq * cq, axis=-1, keepdims=True) + EPS)
        rkv = lax.rsqrt(jnp.mean(ckv * ckv, axis=-1, keepdims=True) + EPS)
        outs[1][...] = (cq * rq * weights[1][...]).astype(BF16)
        outs[2][...] = (ckv * rkv * weights[2][...]).astype(BF16)

    return _rows_call("in_mla", [u], [w_in_mla, gq, gkv], [(MLA_COLS, F32), (Q_LORA, BF16), (KV_LORA, BF16)], compute,
                      min(FFN_TM, t))


def _lora_norm_bwd(p_mla, gq, gkv, dcqn, dckvn, dkpe):
    t = p_mla.shape[0]
    tm = min(ROW_TM, t)

    def body(p_ref, gq_ref, gkv_ref, dq_ref, dkv_ref, dkpe_ref, dp_ref, dgq_ref, dgkv_ref):
        @pl.when(pl.program_id(0) == 0)
        def _():
            dgq_ref[...] = jnp.zeros_like(dgq_ref)
            dgkv_ref[...] = jnp.zeros_like(dgkv_ref)

        dcq, dgq = _rms_bwd_vals(p_ref[:, 0:Q_LORA], gq_ref[...], dq_ref[...])
        dckv, dgkv = _rms_bwd_vals(p_ref[:, Q_LORA:Q_LORA + KV_LORA], gkv_ref[...], dkv_ref[...])
        dp_ref[:, 0:Q_LORA] = dcq.astype(BF16)
        dp_ref[:, Q_LORA:Q_LORA + KV_LORA] = dckv.astype(BF16)
        dp_ref[:, Q_LORA + KV_LORA:MLA_COLS] = dkpe_ref[...].astype(BF16)
        dgq_ref[...] += jnp.sum(dgq, axis=0, keepdims=True)
        dgkv_ref[...] += jnp.sum(dgkv, axis=0, keepdims=True)

    return _call(
        body, name="lora_norm_bwd", grid=(t // tm,),
        in_specs=[pl.BlockSpec((tm, MLA_COLS), lambda i: (i, 0)),
                  pl.BlockSpec((1, Q_LORA), lambda i: (0, 0)), pl.BlockSpec((1, KV_LORA), lambda i: (0, 0)),
                  pl.BlockSpec((tm, Q_LORA), lambda i: (i, 0)), pl.BlockSpec((tm, KV_LORA), lambda i: (i, 0)),
                  pl.BlockSpec((tm, HK), lambda i: (i, 0))],
        out_specs=[pl.BlockSpec((tm, MLA_COLS), lambda i: (i, 0)),
                   pl.BlockSpec((1, Q_LORA), lambda i: (0, 0)), pl.BlockSpec((1, KV_LORA), lambda i: (0, 0))],
        out_shape=[jax.ShapeDtypeStruct((t, MLA_COLS), BF16), jax.ShapeDtypeStruct((1, Q_LORA), F32),
                   jax.ShapeDtypeStruct((1, KV_LORA), F32)],
        compiler_params=_cp(),
    )(p_mla, gq, gkv, dcqn, dckvn, dkpe)


def _cumsum_rows(x, row):
    for s in (1, 2, 4, 8, 16, 32):
        x = x + jnp.where(row >= s, pltpu.roll(x, s, 0), 0.0)
    return x


def _rcumsum_rows(x, row):
    for s in (1, 2, 4, 8, 16, 32):
        x = x + jnp.where(row < CHUNK - s, pltpu.roll(x, CHUNK - s, 0), 0.0)
    return x


def _hg_gates(qr, z, lb, row):
    q = _silu(qr)
    sg = _sig(z)
    f = lb + (1.0 - lb) * sg
    lf = jnp.log(f)
    k = (1.0 - lb) * (1.0 - sg)
    cum = _cumsum_rows(lf, row)
    mid = jnp.sum(jnp.where(row < CHUNK // 2, lf, 0.0), axis=0, keepdims=True)
    last = jnp.sum(lf, axis=0, keepdims=True)
    e_q = jnp.exp(jnp.minimum(cum - mid, EXP_CLAMP))
    e_k = jnp.exp(jnp.minimum(mid - cum, EXP_CLAMP))
    e_a = jnp.exp(cum)
    e_l = jnp.exp(last - cum)
    return q, sg, f, k, last, e_q, e_k, e_a, e_l


def _hgrn_fwd(p_hg, tab, gain):
    t = p_hg.shape[0]
    bt = min(HG_BT, t)
    nb, nc = t // bt, bt // CHUNK

    hpb = HG_HPB
    wide = hpb * HK

    def body(q_ref, f_ref, i_ref, g_ref, tab_ref, gain_ref, o_ref, ho_ref, st_ref, state):
        @pl.when(pl.program_id(1) == 0)
        def _():
            state[...] = jnp.zeros_like(state)

        row = lax.broadcasted_iota(jnp.int32, (CHUNK, HK), 0)
        tril = lax.broadcasted_iota(jnp.int32, (CHUNK, CHUNK), 0) >= lax.broadcasted_iota(jnp.int32, (CHUNK, CHUNK), 1)
        gain_v = gain_ref[...]

        def chunk(c, carry):
            sl = pl.ds(pl.multiple_of(c * CHUNK, CHUNK), CHUNK)
            for hh in range(hpb):
                ln = slice(hh * HK, (hh + 1) * HK)
                lb = _sig(tab_ref[0:1, ln] - tab_ref[1:2, ln])
                v = i_ref[sl, ln].astype(BF16)
                q, _, _, k, last, e_q, e_k, e_a, e_l = _hg_gates(q_ref[sl, ln], f_ref[sl, ln], lb, row)
                st = state[hh]
                st_ref[hh, c] = st
                p = jnp.where(tril, _dot_nt((q * e_q).astype(BF16), (k * e_k).astype(BF16)), 0.0)
                o = _dot(p.astype(BF16), v) + _dot_nt((q * e_a).astype(BF16), st.astype(BF16))
                state[hh] = jnp.exp(last) * st + _dot_tn(v, (k * e_l).astype(BF16))
                o_ref[sl, ln] = o
                r = lax.rsqrt(jnp.mean(o * o, axis=-1, keepdims=True) + EPS)
                ho_ref[sl, ln] = (o * r * gain_v * _silu(g_ref[sl, ln])).astype(BF16)
            return carry

        lax.fori_loop(0, nc, chunk, 0)

    def col(k):
        return pl.BlockSpec((bt, wide), lambda h, j, k=k: (j, k * (HEADS // hpb) + h))

    return _call(
        body, name="hgrn_fwd", grid=(HEADS // hpb, nb),
        in_specs=[col(0), col(1), col(2), col(3),
                  pl.BlockSpec((2, wide), lambda h, j: (0, h)), pl.BlockSpec((1, HK), lambda h, j: (0, 0))],
        out_specs=[pl.BlockSpec((bt, wide), lambda h, j: (j, h)), pl.BlockSpec((bt, wide), lambda h, j: (j, h)),
                   pl.BlockSpec((hpb, nc, HK, HK), lambda h, j: (h, j, 0, 0))],
        out_shape=[jax.ShapeDtypeStruct((t, D), F32), jax.ShapeDtypeStruct((t, D), BF16),
                   jax.ShapeDtypeStruct((HEADS, t // CHUNK, HK, HK), F32)],
        scratch_shapes=[pltpu.VMEM((hpb, HK, HK), F32)],
        compiler_params=_cp(),
    )(p_hg, p_hg, p_hg, p_hg, tab, gain)


def _hgrn_bwd(p_hg, tab, gain, o_raw, states, dho):
    t = p_hg.shape[0]
    bt = min(HG_BT, t)
    nb, nc = t // bt, bt // CHUNK
    hpb = HG_HPB
    wide = hpb * HK

    def body(q_ref, f_ref, i_ref, g_ref, tab_ref, gain_ref, o_ref, st_ref, dho_ref,
             dq_ref, df_ref, di_ref, dg_ref, dtab_ref, dgain_ref, dstate, dlb):
        h, j = pl.program_id(0), pl.program_id(1)

        @pl.when(jnp.logical_and(h == 0, j == 0))
        def _():
            dgain_ref[...] = jnp.zeros_like(dgain_ref)

        @pl.when(j == 0)
        def _():
            dstate[...] = jnp.zeros_like(dstate)
            dlb[...] = jnp.zeros_like(dlb)

        row = lax.broadcasted_iota(jnp.int32, (CHUNK, HK), 0)
        tril = lax.broadcasted_iota(jnp.int32, (CHUNK, CHUNK), 0) >= lax.broadcasted_iota(jnp.int32, (CHUNK, CHUNK), 1)
        gain_v = gain_ref[...]

        def chunk(cc, carry):
            c = nc - 1 - cc
            sl = pl.ds(pl.multiple_of(c * CHUNK, CHUNK), CHUNK)
            dgain = jnp.zeros((1, HK), F32)
            for hh in range(hpb):
                ln = slice(hh * HK, (hh + 1) * HK)
                lb = _sig(tab_ref[0:1, ln] - tab_ref[1:2, ln])
                qr = q_ref[sl, ln]
                v = i_ref[sl, ln].astype(BF16)
                gr = g_ref[sl, ln]
                q, sg, f, k, last, e_q, e_k, e_a, e_l = _hg_gates(qr, f_ref[sl, ln], lb, row)
                o = o_ref[sl, ln]
                r = lax.rsqrt(jnp.mean(o * o, axis=-1, keepdims=True) + EPS)
                oh = o * r
                dh = dho_ref[sl, ln].astype(F32)
                dnorm = dh * _silu(gr)
                dg_ref[sl, ln] = (dh * oh * gain_v * _dsilu(gr)).astype(BF16)
                dgain = dgain + jnp.sum(dnorm * oh, axis=0, keepdims=True)
                dxh = dnorm * gain_v
                do = (r * (dxh - oh * jnp.mean(dxh * oh, axis=-1, keepdims=True))).astype(BF16)
                st0 = st_ref[hh, c]
                st0_b = st0.astype(BF16)
                ds1 = dstate[hh]
                ds1_b = ds1.astype(BF16)
                qt = (q * e_q).astype(BF16)
                kt = (k * e_k).astype(BF16)
                qd = (q * e_a).astype(BF16)
                kd = (k * e_l).astype(BF16)
                p = jnp.where(tril, _dot_nt(qt, kt), 0.0).astype(BF16)
                dp = jnp.where(tril, _dot_nt(do, v), 0.0).astype(BF16)
                dv = _dot_tn(p, do) + _dot_nt(kd, ds1_b)
                dqt = _dot(dp, kt)
                dkt = _dot_tn(dp, qt)
                dq_inter = _dot(do, st0_b) * e_a
                dk_inter = _dot(v, ds1_b) * e_l
                dq = dqt * e_q + dq_inter
                dk = dkt * e_k + dk_inter
                e_last = jnp.exp(last)
                dstate[hh] = _dot_tn(do, qd) + e_last * ds1
                dlast = (jnp.sum(k * dk_inter, axis=0, keepdims=True)
                         + e_last * jnp.sum(ds1 * st0, axis=0, keepdims=True))
                da = (qt.astype(F32) * dqt - kt.astype(F32) * dkt + q * dq_inter - k * dk_inter
                      + jnp.where(row == CHUNK - 1, dlast, 0.0))
                dlf = _rcumsum_rows(da, row)
                dfv = dlf / f - dk
                df_ref[sl, ln] = (dfv * (1.0 - lb) * sg * (1.0 - sg)).astype(BF16)
                dlb[:, ln] += jnp.sum(dfv * (1.0 - sg), axis=0, keepdims=True)
                dq_ref[sl, ln] = (dq * _dsilu(qr)).astype(BF16)
                di_ref[sl, ln] = dv.astype(BF16)
            dgain_ref[...] += dgain
            return carry

        lax.fori_loop(0, nc, chunk, 0)

        @pl.when(j == nb - 1)
        def _():
            lb = _sig(tab_ref[0:1, :] - tab_ref[1:2, :])
            d0 = dlb[...] * lb * (1.0 - lb)
            dtab_ref[0:1, :] = d0
            dtab_ref[1:2, :] = -d0

    def col(k):
        return pl.BlockSpec((bt, wide), lambda h, j, k=k: (nb - 1 - j, k * (HEADS // hpb) + h))

    tok = pl.BlockSpec((bt, wide), lambda h, j: (nb - 1 - j, h))
    return _call(
        body, name="hgrn_bwd", grid=(HEADS // hpb, nb),
        in_specs=[col(0), col(1), col(2), col(3),
                  pl.BlockSpec((2, wide), lambda h, j: (0, h)), pl.BlockSpec((1, HK), lambda h, j: (0, 0)),
                  tok, pl.BlockSpec((hpb, nc, HK, HK), lambda h, j: (h, nb - 1 - j, 0, 0)), tok],
        out_specs=[tok, tok, tok, tok,
                   pl.BlockSpec((2, wide), lambda h, j: (0, h)), pl.BlockSpec((1, HK), lambda h, j: (0, 0))],
        out_shape=[jax.ShapeDtypeStruct((t, D), BF16)] * 4
        + [jax.ShapeDtypeStruct((2, D), F32), jax.ShapeDtypeStruct((1, HK), F32)],
        scratch_shapes=[pltpu.VMEM((hpb, HK, HK), F32), pltpu.VMEM((1, wide), F32)],
        compiler_params=_cp(),
    )(p_hg, p_hg, p_hg, p_hg, tab, gain, o_raw, states, dho)


def _rope_tables(pos):
    t = pos.shape[0]
    tm = min(ROW_TM, t)
    inv = np.zeros((1, HK), np.float32)
    freq = (ROPE_THETA ** (-np.arange(0, ROPE, 2, dtype=np.float32) / ROPE)).astype(np.float32)
    inv[0, 0:ROPE // 2] = freq
    inv[0, ROPE // 2:ROPE] = freq
    sign = np.zeros((1, HK), np.float32)
    sign[0, 0:ROPE // 2] = -1.0
    sign[0, ROPE // 2:ROPE] = 1.0

    def body(pos_ref, inv_ref, sign_ref, cos_ref, sin_ref):
        ang = pos_ref[...].astype(F32) * inv_ref[...]
        cos_ref[...] = jnp.cos(ang)
        sin_ref[...] = jnp.sin(ang) * sign_ref[...]

    one = pl.BlockSpec((1, HK), lambda i: (0, 0))
    row = pl.BlockSpec((tm, HK), lambda i: (i, 0))
    return _call(
        body, name="rope_tables", grid=(t // tm,),
        in_specs=[pl.BlockSpec((tm, 1), lambda i: (i, 0)), one, one],
        out_specs=[row, row],
        out_shape=[jax.ShapeDtypeStruct((t, HK), F32)] * 2,
        compiler_params=_cp(),
    )(pos, jnp.asarray(inv), jnp.asarray(sign))


def _rope(x, cos, sin_signed):
    r = lax.broadcasted_iota(jnp.int32, (HK, HK), 0)
    c = lax.broadcasted_iota(jnp.int32, (HK, HK), 1)
    half = ROPE // 2
    swap = jnp.logical_or(jnp.logical_and(c < half, r == c + half),
                          jnp.logical_and(jnp.logical_and(c >= half, c < ROPE), r == c - half))
    return x * cos + _dot_split(x, swap.astype(BF16)) * sin_signed


def _dot_split(x, m):
    hi = x.astype(BF16)
    lo = (x - hi.astype(F32)).astype(BF16)
    return _dot(hi, m) + _dot(lo, m)


def _lane_sum(x):
    return _dot_split(x, jnp.ones((HK, HK), BF16))


def _head_norm(xn, xr):
    r = lax.rsqrt(_lane_sum(xn * xn + xr * xr) * (1.0 / QK) + EPS)
    return xn * r, xr * r, r


def _head_norm_bwd(xn, xr, g_n, g_r, dn, dr):
    hn, hr, r = _head_norm(xn, xr)
    dxn, dxr = dn * g_n, dr * g_r
    c = _lane_sum(dxn * hn + dxr * hr) * (1.0 / QK)
    return r * (dxn - hn * c), r * (dxr - hr * c), dn * hn, dr * hr


def _mla_prep_fwd(qf, kv, p_mla, cos, sin, gq, gk):
    t = qf.shape[0]
    tm = min(ROW_TM, t)

    def body(qf_ref, kv_ref, kpe_ref, cos_ref, sin_ref, gq_ref, gk_ref, q_ref, k_ref, v_ref):
        cos_v, sin_v = cos_ref[...], sin_ref[...]
        kpe = kpe_ref[...]
        for h in range(HEADS):
            lo, mid, hi = h * QKP, h * QKP + HK, (h + 1) * QKP
            qn, qr, _ = _head_norm(qf_ref[:, lo:mid], qf_ref[:, mid:hi])
            q_ref[h, :, 0:HK] = (qn * gq_ref[:, 0:HK] * (SCALE * LOG2E)).astype(BF16)
            q_ref[h, :, HK:QKP] = (_rope(qr * gq_ref[:, HK:QKP], cos_v, sin_v) * (SCALE * LOG2E)).astype(BF16)
            kn, kr, _ = _head_norm(kv_ref[:, lo:mid], kpe)
            k_ref[h, :, 0:HK] = (kn * gk_ref[:, 0:HK]).astype(BF16)
            k_ref[h, :, HK:QKP] = _rope(kr * gk_ref[:, HK:QKP], cos_v, sin_v).astype(BF16)
            v_ref[h, :, 0:HK] = kv_ref[:, mid:hi].astype(BF16)
            v_ref[h, :, HK:QKP] = jnp.full((tm, HK), -1.0, BF16)

    head = pl.BlockSpec((tm, HEADS * QKP), lambda i: (i, 0))
    tok = pl.BlockSpec((tm, HK), lambda i: (i, 0))
    gain = pl.BlockSpec((1, QKP), lambda i: (0, 0))
    return _call(
        body, name="mla_prep_fwd", grid=(t // tm,),
        in_specs=[head, head, pl.BlockSpec((tm, HK), lambda i: (i, MLA_COLS // HK - 1)), tok, tok, gain, gain],
        out_specs=[pl.BlockSpec((HEADS, tm, QKP), lambda i: (0, i, 0)),
                   pl.BlockSpec((HEADS, tm, QKP), lambda i: (0, i, 0)),
                   pl.BlockSpec((HEADS, tm, QKP), lambda i: (0, i, 0))],
        out_shape=[jax.ShapeDtypeStruct((HEADS, t, QKP), BF16), jax.ShapeDtypeStruct((HEADS, t, QKP), BF16),
                   jax.ShapeDtypeStruct((HEADS, t, QKP), BF16)],
        compiler_params=_cp(),
    )(qf, kv, p_mla, cos, sin, gq, gk)


def _mla_prep_bwd(qf, kv, p_mla, cos, sin, gq, gk, dq, dk, dv):
    t = qf.shape[0]
    tm = min(ROW_TM, t)

    def body(qf_ref, kv_ref, kpe_ref, cos_ref, sin_ref, gq_ref, gk_ref, dq_ref, dk_ref, dv_ref,
             dqf_ref, dkv_ref, dkpe_ref, dgq_ref, dgk_ref):
        @pl.when(pl.program_id(0) == 0)
        def _():
            dgq_ref[...] = jnp.zeros_like(dgq_ref)
            dgk_ref[...] = jnp.zeros_like(dgk_ref)

        cos_v, sin_v = cos_ref[...], -sin_ref[...]
        kpe = kpe_ref[...]
        gqn, gqr, gkn, gkr = gq_ref[:, 0:HK], gq_ref[:, HK:QKP], gk_ref[:, 0:HK], gk_ref[:, HK:QKP]
        dkpe = jnp.zeros((tm, HK), F32)
        dgq_n, dgq_r, dgk_n, dgk_r = [jnp.zeros((1, HK), F32) for _ in range(4)]
        for h in range(HEADS):
            lo, mid, hi = h * QKP, h * QKP + HK, (h + 1) * QKP
            dqn = dq_ref[h, :, 0:HK].astype(F32) * SCALE
            dqr = _rope(dq_ref[h, :, HK:QKP].astype(F32), cos_v, sin_v) * SCALE
            a, b, ga, gb = _head_norm_bwd(qf_ref[:, lo:mid], qf_ref[:, mid:hi], gqn, gqr, dqn, dqr)
            dqf_ref[:, lo:mid] = a.astype(BF16)
            dqf_ref[:, mid:hi] = b.astype(BF16)
            dgq_n = dgq_n + jnp.sum(ga, axis=0, keepdims=True)
            dgq_r = dgq_r + jnp.sum(gb, axis=0, keepdims=True)
            dkn = dk_ref[h, :, 0:HK].astype(F32) * LN2
            dkr = _rope(dk_ref[h, :, HK:QKP].astype(F32), cos_v, sin_v) * LN2
            a, b, ga, gb = _head_norm_bwd(kv_ref[:, lo:mid], kpe, gkn, gkr, dkn, dkr)
            dkv_ref[:, lo:mid] = a.astype(BF16)
            dkv_ref[:, mid:hi] = dv_ref[h].astype(BF16)
            dkpe = dkpe + b
            dgk_n = dgk_n + jnp.sum(ga, axis=0, keepdims=True)
            dgk_r = dgk_r + jnp.sum(gb, axis=0, keepdims=True)
        dkpe_ref[...] = dkpe
        dgq_ref[:, 0:HK] += dgq_n
        dgq_ref[:, HK:QKP] += dgq_r
        dgk_ref[:, 0:HK] += dgk_n
        dgk_ref[:, HK:QKP] += dgk_r

    head = pl.BlockSpec((tm, HEADS * QKP), lambda i: (i, 0))
    tok = pl.BlockSpec((tm, HK), lambda i: (i, 0))
    gain = pl.BlockSpec((1, QKP), lambda i: (0, 0))
    hq = pl.BlockSpec((HEADS, tm, QKP), lambda i: (0, i, 0))
    return _call(
        body, name="mla_prep_bwd", grid=(t // tm,),
        in_specs=[head, head, pl.BlockSpec((tm, HK), lambda i: (i, MLA_COLS // HK - 1)), tok, tok, gain, gain,
                  hq, hq, pl.BlockSpec((HEADS, tm, HK), lambda i: (0, i, 0))],
        out_specs=[head, head, tok, gain, gain],
        out_shape=[jax.ShapeDtypeStruct((t, HEADS * QKP), BF16), jax.ShapeDtypeStruct((t, HEADS * QKP), BF16),
                   jax.ShapeDtypeStruct((t, HK), F32), jax.ShapeDtypeStruct((1, QKP), F32),
                   jax.ShapeDtypeStruct((1, QKP), F32)],
        compiler_params=_cp(),
    )(qf, kv, p_mla, cos, sin, gq, gk, dq, dk, dv)


def _chunk_mask(row0, rows, cols):
    r = lax.broadcasted_iota(jnp.int32, (rows, cols), 0) + row0
    c = lax.broadcasted_iota(jnp.int32, (rows, cols), 1)
    return jnp.right_shift(r, 6) >= jnp.right_shift(c, 6)


def _flash_fwd(q, k, v, side=None):
    t = q.shape[1]
    tq = min(TQ, t)
    nq = t // tq
    sub = min(SUBQ, tq)
    pairs = [(i, j) for i in range(nq) for j in range(i + 1)]
    qi = jnp.asarray([p[0] for p in pairs], jnp.int32)
    kj = jnp.asarray([p[1] for p in pairs], jnp.int32)
    s_in = len(side.inputs) if side else 0
    s_out = len(side.out_shapes) if side else 0

    def body(qi_ref, kj_ref, q_ref, k_ref, v_ref, *rest):
        o_ref, lse_ref = rest[s_in:s_in + 2]
        m_s, acc_s = rest[s_in + 2 + s_out:s_in + 4 + s_out]
        side_refs = list(rest[:s_in]) + list(rest[s_in + 2:s_in + 2 + s_out]) + list(rest[s_in + 4 + s_out:])
        n = pl.program_id(1)
        i, j = qi_ref[n], kj_ref[n]
        if side:
            @pl.when(jnp.logical_and(pl.program_id(0) == 0, n == 0))
            def _():
                side.start(*side_refs)

        @pl.when(j == 0)
        def _():
            m_s[...] = jnp.full_like(m_s, NEG)
            acc_s[...] = jnp.zeros_like(acc_s)

        def step(diag):
            subs = range(tq // sub)
            width = [(r + 1) * sub if diag else tq for r in subs]
            logits = [_dot_nt(q_ref[r * sub:(r + 1) * sub, :], k_ref[0:width[r], :]) for r in subs]
            for r in subs:
                rows = slice(r * sub, (r + 1) * sub)
                cols = width[r]
                s = logits[r]
                if diag:
                    s = jnp.where(_chunk_mask(r * sub, sub, cols), s, NEG)
                m_old = m_s[rows, :]
                m_new = jnp.maximum(m_old, jnp.max(s, axis=-1, keepdims=True))
                alpha = jnp.exp2(m_old - m_new)
                p = jnp.exp2((s - jnp.tile(m_new, (1, cols // HK))).astype(BF16))
                acc_s[rows, :] = jnp.tile(alpha, (1, 2)) * acc_s[rows, :] + _dot(p, v_ref[0:cols, :])
                m_s[rows, :] = m_new

        @pl.when(j < i)
        def _():
            step(False)

        @pl.when(j == i)
        def _():
            step(True)
            l = -acc_s[:, HK:QKP]
            o_ref[...] = (acc_s[:, 0:HK] / l).astype(BF16)
            lse_ref[...] = m_s[...] + jnp.log(l) * LOG2E

        if side:
            @pl.when(jnp.logical_and(pl.program_id(0) == HEADS - 1, n == len(pairs) - 1))
            def _():
                side.finish(*side_refs)

    anywhere = pl.BlockSpec(memory_space=pl.ANY)
    grid_spec = pltpu.PrefetchScalarGridSpec(
        num_scalar_prefetch=2, grid=(HEADS, len(pairs)),
        in_specs=[pl.BlockSpec((None, tq, QKP), lambda h, n, qi, kj: (h, qi[n], 0)),
                  pl.BlockSpec((None, tq, QKP), lambda h, n, qi, kj: (h, kj[n], 0)),
                  pl.BlockSpec((None, tq, QKP), lambda h, n, qi, kj: (h, kj[n], 0))] + [anywhere] * s_in,
        out_specs=[pl.BlockSpec((tq, HK), lambda h, n, qi, kj: (qi[n], h)),
                   pl.BlockSpec((None, tq, HK), lambda h, n, qi, kj: (h, qi[n], 0))] + [anywhere] * s_out,
        scratch_shapes=[pltpu.VMEM((tq, HK), F32), pltpu.VMEM((tq, QKP), F32)] + (list(side.scratch) if side else []),
    )
    return _call(
        body, name="flash_fwd", grid_spec=grid_spec,
        out_shape=[jax.ShapeDtypeStruct((t, D), BF16), jax.ShapeDtypeStruct((HEADS, t, HK), F32)]
        + (list(side.out_shapes) if side else []),
        compiler_params=_cp(),
    )(qi, kj, q, k, v, *(side.inputs if side else []))


def _attn_do(do, o):
    t = do.shape[0]
    tm = min(TM, t)

    def body(do_ref, o_ref, d_ref):
        lane = lax.broadcasted_iota(jnp.int32, (tm, HK), 1)
        for h in range(HEADS):
            ln = slice(h * HK, (h + 1) * HK)
            dov = do_ref[:, ln]
            d = jnp.sum(dov.astype(F32) * o_ref[:, ln].astype(F32), axis=-1, keepdims=True)
            hi = d.astype(BF16).astype(F32)
            d_ref[h, :, 0:HK] = dov
            d_ref[h, :, HK:QKP] = jnp.where(lane == 0, hi, jnp.where(lane == 1, d - hi, 0.0)).astype(BF16)

    blk = pl.BlockSpec((tm, D), lambda i: (i, 0))
    return _call(
        body, name="attn_do", grid=(t // tm,),
        in_specs=[blk, blk],
        out_specs=pl.BlockSpec((HEADS, tm, QKP), lambda i: (0, i, 0)),
        out_shape=jax.ShapeDtypeStruct((HEADS, t, QKP), BF16),
        compiler_params=_cp(),
    )(do, o)


def _flash_bwd(q, k, v, lse, do):
    t = q.shape[1]
    tq = min(TQ, t)
    nq = t // tq
    sub = min(SUBQ, tq)
    pairs = [(i, j) for j in range(nq) for i in range(j, nq)]
    qi = jnp.asarray([p[0] for p in pairs], jnp.int32)
    kj = jnp.asarray([p[1] for p in pairs], jnp.int32)
    npairs = len(pairs)

    def body(qi_ref, kj_ref, q_ref, k_ref, v_ref, lse_ref, do_ref, dq_ref, dk_ref, dv_ref):
        n = pl.program_id(1)
        i, j = qi_ref[n], kj_ref[n]

        @pl.when(n == 0)
        def _():
            dq_ref[...] = jnp.zeros_like(dq_ref)

        @pl.when(i == j)
        def _():
            dk_ref[...] = jnp.zeros_like(dk_ref)
            dv_ref[...] = jnp.zeros_like(dv_ref)

        def step(diag):
            for r in range(tq // sub):
                rows = slice(r * sub, (r + 1) * sub)
                cols = (r + 1) * sub if diag else tq
                qv, kv_ = q_ref[rows, :], k_ref[0:cols, :]
                p = jnp.exp2(_dot_nt(qv, kv_) - jnp.tile(lse_ref[rows, :], (1, cols // HK)))
                if diag:
                    p = jnp.where(_chunk_mask(r * sub, sub, cols), p, 0.0)
                dp_less_delta = _dot_nt(do_ref[rows, :], v_ref[0:cols, :])
                ds = (p * dp_less_delta).astype(BF16)
                dv_ref[0:cols, :] += _dot_tn(p.astype(BF16), do_ref[rows, 0:HK])
                dk_ref[0:cols, :] += _dot_tn(ds, qv)
                dq_rows = pl.ds(pl.multiple_of(i * tq + r * sub, sub), sub)
                dq_ref[dq_rows, :] += _dot(ds, kv_)

        @pl.when(j < i)
        def _():
            step(False)

        @pl.when(j == i)
        def _():
            step(True)

    grid_spec = pltpu.PrefetchScalarGridSpec(
        num_scalar_prefetch=2, grid=(HEADS, npairs),
        in_specs=[pl.BlockSpec((None, tq, QKP), lambda h, n, qi, kj: (h, qi[n], 0)),
                  pl.BlockSpec((None, tq, QKP), lambda h, n, qi, kj: (h, kj[n], 0)),
                  pl.BlockSpec((None, tq, QKP), lambda h, n, qi, kj: (h, kj[n], 0)),
                  pl.BlockSpec((None, tq, HK), lambda h, n, qi, kj: (h, qi[n], 0)),
                  pl.BlockSpec((None, tq, QKP), lambda h, n, qi, kj: (h, qi[n], 0))],
        out_specs=[pl.BlockSpec((None, t, QKP), lambda h, n, qi, kj: (h, 0, 0)),
                   pl.BlockSpec((None, tq, QKP), lambda h, n, qi, kj: (h, kj[n], 0)),
                   pl.BlockSpec((None, tq, HK), lambda h, n, qi, kj: (h, kj[n], 0))],
    )
    return _call(
        body, name="flash_bwd", grid_spec=grid_spec,
        out_shape=[jax.ShapeDtypeStruct((HEADS, t, QKP), F32), jax.ShapeDtypeStruct((HEADS, t, QKP), F32),
                   jax.ShapeDtypeStruct((HEADS, t, HK), F32)],
        compiler_params=_cp(56),
    )(qi, kj, q, k, v, lse, do)


def _adamw(name, w, g, m, v):
    r, c = w.shape
    tr = r if r <= 256 else next(k for k in (256, 352, 384) if r % k == 0)

    def body(w_ref, g_ref, m_ref, v_ref, d_ref, nm_ref, nv_ref):
        gv = g_ref[...]
        nm = ADAM_B1 * m_ref[...] + (1.0 - ADAM_B1) * gv
        nv = ADAM_B2 * v_ref[...] + (1.0 - ADAM_B2) * (gv * gv)
        m_hat = nm / (1.0 - ADAM_B1 ** ADAM_STEP)
        v_hat = nv / (1.0 - ADAM_B2 ** ADAM_STEP)
        d_ref[...] = -ADAM_LR * (m_hat / (jnp.sqrt(v_hat) + ADAM_EPS) + ADAM_WD * w_ref[...])
        nm_ref[...] = nm
        nv_ref[...] = nv

    blk = pl.BlockSpec((tr, c), lambda i: (i, 0))
    return _call(
        body, name=name, grid=(r // tr,),
        in_specs=[blk] * 4, out_specs=[blk] * 3,
        out_shape=[jax.ShapeDtypeStruct((r, c), F32)] * 3,
        compiler_params=_cp(),
    )(w, g, m, v)


def _place():
    return lax.axis_index("x"), lax.axis_index("y"), lax.axis_index("c")


def _other_chips(x, y):
    return [(1 - x, y), (x, 1 - y), (1 - x, 1 - y)]


class _Exchange:
    inputs = ()
    out_shapes = ()
    scratch = ()

    def start(self, *refs):
        raise NotImplementedError

    def finish(self, *refs):
        raise NotImplementedError

    def alone(self, name):
        def body(*refs):
            self.start(*refs)
            self.finish(*refs)

        anywhere = pl.BlockSpec(memory_space=pl.ANY)
        return _call(
            body, name=name,
            in_specs=[anywhere] * len(self.inputs), out_specs=[anywhere] * len(self.out_shapes),
            out_shape=list(self.out_shapes), scratch_shapes=list(self.scratch),
        )(*self.inputs)


class _GatherWeights(_Exchange):
    def __init__(self, shards):
        self.inputs = tuple(shards)
        self.out_shapes = tuple(jax.ShapeDtypeStruct((4,) + s.shape, s.dtype) for s in shards)
        self.scratch = (pltpu.SemaphoreType.DMA((6 * len(shards),)), pltpu.SemaphoreType.DMA((6 * len(shards),)))

    def gathered(self, got, k):
        return [lax.dynamic_update_slice(g, s[None], (k, 0, 0)) for g, s in zip(got, self.inputs)]

    def _copies(self, *refs):
        nbuf = len(self.inputs)
        send_sems, recv_sems = refs[2 * nbuf:]
        x, y, c = _place()
        chips = _other_chips(x, y)
        first, passed, landed, relayed = [], [], [], []
        for b, (s_ref, g_ref) in enumerate(zip(refs[:nbuf], refs[nbuf:2 * nbuf])):
            half = self.inputs[b].shape[0] // 2

            def rows(px, py, pc, g_ref=g_ref, half=half):
                return g_ref.at[2 * px + py, pl.ds(pc * half, half), :]

            def copy(k, block, to, src=None, rows=rows, b=b):
                return pltpu.make_async_remote_copy(
                    src_ref=rows(*block) if src is None else src, dst_ref=rows(*block),
                    send_sem=send_sems.at[6 * b + k], recv_sem=recv_sems.at[6 * b + k], device_id=to, device_id_type=MESH)

            mine = s_ref.at[pl.ds(c * half, half), :]
            first += [copy(j, (x, y, c), (*chip, c), src=mine) for j, chip in enumerate(chips)]
            passed += [copy(3 + j, (*chip, c), (x, y, 1 - c)) for j, chip in enumerate(chips)]
            landed += [copy(j, (*chip, c), (x, y, c)) for j, chip in enumerate(chips)]
            relayed += [copy(3 + j, (*chip, 1 - c), (x, y, c)) for j, chip in enumerate(chips)]
        return first, passed, landed, relayed

    def start(self, *refs):
        for cp in self._copies(*refs)[0]:
            cp.start()

    def finish(self, *refs):
        first, passed, landed, relayed = self._copies(*refs)
        for arrived, onward in zip(landed, passed):
            arrived.wait_recv()
            onward.start()
        for cp in relayed:
            cp.wait_recv()
        for cp in first + passed:
            cp.wait_send()


class _SwapHalves(_Exchange):
    def __init__(self, bufs):
        self.inputs = tuple(bufs)
        self.out_shapes = tuple(jax.ShapeDtypeStruct((4, g.shape[1] // 2, g.shape[2]), g.dtype) for g in bufs)
        self.scratch = (pltpu.SemaphoreType.DMA((len(bufs),)), pltpu.SemaphoreType.DMA((len(bufs),)))

    def _copies(self, *refs):
        nbuf = len(self.inputs)
        send_sems, recv_sems = refs[2 * nbuf:]
        x, y, c = _place()
        cps = []
        for b, (g_ref, o_ref) in enumerate(zip(refs[:nbuf], refs[nbuf:2 * nbuf])):
            half = self.inputs[b].shape[1] // 2
            cps.append(pltpu.make_async_remote_copy(
                src_ref=g_ref.at[:, pl.ds((1 - c) * half, half), :], dst_ref=o_ref,
                send_sem=send_sems.at[b], recv_sem=recv_sems.at[b], device_id=(x, y, 1 - c), device_id_type=MESH))
        return cps

    def start(self, *refs):
        for cp in self._copies(*refs):
            cp.start()

    def finish(self, *refs):
        for cp in self._copies(*refs):
            cp.wait()


def _add_rows(half):
    return next(tr for tr in range(512, 15, -16) if half % tr == 0)


def _chip_sum(name, gp, got, c_arr):
    half, width = got.shape[1], got.shape[2]
    tr = _add_rows(half)
    nb = half // tr

    def body(c_ref, a_ref, b_ref, o_ref, ob_ref):
        s = a_ref[...] + b_ref[...]
        o_ref[...] = s
        ob_ref[...] = s.astype(BF16)

    grid_spec = pltpu.PrefetchScalarGridSpec(
        num_scalar_prefetch=1, grid=(4, nb),
        in_specs=[pl.BlockSpec((None, tr, width), lambda s, i, c: (s, c[0] * nb + i, 0)),
                  pl.BlockSpec((None, tr, width), lambda s, i, c: (s, i, 0))],
        out_specs=[pl.BlockSpec((None, tr, width), lambda s, i, c: (s, i, 0)),
                   pl.BlockSpec((None, tr, width), lambda s, i, c: (s, i, 0))],
    )
    return _call(
        body, name=name, grid_spec=grid_spec,
        out_shape=[jax.ShapeDtypeStruct(got.shape, F32), jax.ShapeDtypeStruct(got.shape, BF16)],
        compiler_params=_cp(),
    )(c_arr, gp, got)


class _ScatterChipSums(_Exchange):
    def __init__(self, sums):
        self.inputs = tuple(sums)
        self.out_shapes = tuple(jax.ShapeDtypeStruct((3,) + cs.shape[1:], cs.dtype) for cs in sums)
        self.scratch = (pltpu.SemaphoreType.DMA((3 * len(sums),)), pltpu.SemaphoreType.DMA((3 * len(sums),)))

    def _copies(self, *refs):
        nbuf = len(self.inputs)
        send_sems, recv_sems = refs[2 * nbuf:]
        x, y, c = _place()
        return [pltpu.make_async_remote_copy(
            src_ref=s_ref.at[2 * px + py], dst_ref=o_ref.at[j],
            send_sem=send_sems.at[3 * b + j], recv_sem=recv_sems.at[3 * b + j], device_id=(px, py, c), device_id_type=MESH)
            for b, (s_ref, o_ref) in enumerate(zip(refs[:nbuf], refs[nbuf:2 * nbuf]))
            for j, (px, py) in enumerate(_other_chips(x, y))]

    def start(self, *refs):
        for cp in self._copies(*refs):
            cp.start()

    def finish(self, *refs):
        for cp in self._copies(*refs):
            cp.wait()


def _shard_sum(name, cs, got, kc_arr):
    h, width = cs.shape[1], cs.shape[2]
    tr = _add_rows(h)
    nb = h // tr

    def body(k_ref, a_ref, b_ref, o_ref):
        o_ref[...] = ((a_ref[...] + b_ref[0].astype(F32)) + b_ref[1].astype(F32)) + b_ref[2].astype(F32)

    grid_spec = pltpu.PrefetchScalarGridSpec(
        num_scalar_prefetch=1, grid=(nb,),
        in_specs=[pl.BlockSpec((None, tr, width), lambda i, k: (k[0], i, 0)),
                  pl.BlockSpec((3, tr, width), lambda i, k: (0, i, 0))],
        out_specs=pl.BlockSpec((tr, width), lambda i, k: (k[1] * nb + i, 0)),
    )
    return _call(
        body, name=name, grid_spec=grid_spec,
        out_shape=jax.ShapeDtypeStruct((2 * h, width), F32),
        compiler_params=_cp(),
    )(kc_arr, cs, got)


def _join_halves(name, boths):
    nbuf = len(boths)

    def body(*refs):
        send_sems, recv_sems = refs[2 * nbuf:]
        x, y, c = _place()
        sent, landing = [], []
        for b, (m_ref, o_ref) in enumerate(zip(refs[:nbuf], refs[nbuf:2 * nbuf])):
            h = boths[b].shape[0] // 2
            mine = m_ref.at[pl.ds(c * h, h), :]
            sent.append(pltpu.make_async_remote_copy(
                src_ref=mine, dst_ref=o_ref.at[pl.ds(c * h, h), :],
                send_sem=send_sems.at[b], recv_sem=recv_sems.at[b], device_id=(x, y, 1 - c), device_id_type=MESH))
            landing.append(pltpu.make_async_remote_copy(
                src_ref=mine, dst_ref=o_ref.at[pl.ds((1 - c) * h, h), :],
                send_sem=send_sems.at[b], recv_sem=recv_sems.at[b], device_id=(x, y, 1 - c), device_id_type=MESH))
        for cp in sent:
            cp.start()
        for cp in sent:
            cp.wait_send()
        for cp in landing:
            cp.wait_recv()

    anywhere = pl.BlockSpec(memory_space=pl.ANY)
    return _call(
        body, name=name,
        in_specs=[anywhere] * nbuf, out_specs=[anywhere] * nbuf,
        out_shape=[jax.ShapeDtypeStruct(g.shape, g.dtype) for g in boths],
        input_output_aliases={b: b for b in range(nbuf)},
        scratch_shapes=[pltpu.SemaphoreType.DMA((nbuf,)), pltpu.SemaphoreType.DMA((nbuf,))],
    )(*boths)


def _all_reduce_small(v):
    r = v.shape[0]

    def body(v_ref, o_ref, buf, send_sems, recv_sems):
        x, y, c = _place()
        me = 4 * x + 2 * y + c
        buf[me] = v_ref[...]
        cps = []
        for k in range(1, 8):
            peer = (x ^ (k >> 2), y ^ ((k >> 1) & 1), c ^ (k & 1))
            cps.append(pltpu.make_async_remote_copy(
                src_ref=v_ref, dst_ref=buf.at[me],
                send_sem=send_sems.at[k - 1], recv_sem=recv_sems.at[k - 1], device_id=peer, device_id_type=MESH))
        for cp in cps:
            cp.start()
        for k in range(1, 8):
            pltpu.make_async_remote_copy(
                src_ref=v_ref, dst_ref=buf.at[me ^ k],
                send_sem=send_sems.at[k - 1], recv_sem=recv_sems.at[k - 1],
                device_id=(x, y, c), device_id_type=MESH).wait_recv()
        for cp in cps:
            cp.wait_send()
        acc = buf[0]
        for k in range(1, 8):
            acc = acc + buf[k]
        o_ref[...] = acc

    return _call(
        body, name="all_reduce_small",
        in_specs=[pl.BlockSpec(memory_space=pltpu.VMEM)],
        out_specs=pl.BlockSpec(memory_space=pltpu.VMEM),
        out_shape=jax.ShapeDtypeStruct((r, 128), F32),
        scratch_shapes=[pltpu.VMEM((8, r, 128), F32), pltpu.SemaphoreType.DMA((7,)), pltpu.SemaphoreType.DMA((7,))],
    )(v)


def _group(names):
    return tuple(e for e in BIG if e[0] in names)


def _pack(shards, dtype):
    return jnp.concatenate([s.astype(dtype).reshape(-1, PACK_W) for s in shards], axis=0)


def _unpack_full(g, group):
    out, at = {}, 0
    for name, rows, cols, axis in group:
        n = rows * cols // 4 // PACK_W
        blk = g[:, at:at + n, :]
        at += n
        if axis == 1:
            out[name] = blk.reshape(4, rows, cols // 4).transpose(1, 0, 2).reshape(rows, cols)
        else:
            out[name] = blk.reshape(rows, cols)
    return out


def _pack_grads(grads, group):
    parts = []
    for name, rows, cols, axis in group:
        g = grads[name]
        if axis == 1:
            g = g.reshape(rows, 4, cols // 4).transpose(1, 0, 2)
        parts.append(g.reshape(4, -1, PACK_W))
    rows_total = sum(p.shape[1] for p in parts)
    pad = -rows_total % PACK_ALIGN
    if pad:
        parts.append(jnp.zeros((4, pad, PACK_W), F32))
    return jnp.concatenate(parts, axis=1)


def _unpack_shard(s, group):
    out, at = {}, 0
    for name, rows, cols, axis in group:
        n = rows * cols // 4 // PACK_W
        shape = (rows, cols // 4) if axis == 1 else (rows // 4, cols)
        out[name] = s[at:at + n, :].reshape(shape)
        at += n
    return out


def _pack_small(parts):
    flat = jnp.concatenate([p.reshape(-1) for p in parts])
    pad = -flat.shape[0] % 1024
    return jnp.concatenate([flat, jnp.zeros((pad,), F32)]).reshape(-1, 128)


def _ffn_in(tag, h, gain, w_in, side=None):
    t = h.shape[0]
    wide = DFF // 2

    def compute_in(rows, weights, outs):
        hv, w_ref = rows[0][...], weights[0]
        r = lax.rsqrt(jnp.mean(hv * hv, axis=-1, keepdims=True) + EPS)
        a = (hv * r * weights[1][...]).astype(BF16)
        outs[0][...] = a

        def emit(gate, up, cols):
            outs[1][:, cols] = gate.astype(BF16)
            outs[2][:, cols] = up.astype(BF16)
            outs[3][:, cols] = (_silu(gate) * up).astype(BF16)

        for s in range(2):
            emit(_dot(a, w_ref[s, :, 0:FFN_MAIN]), _dot(a, w_ref[2 + s, :, 0:FFN_MAIN]),
                 slice(s * wide, s * wide + FFN_MAIN))
        gate = _dot(a, jnp.concatenate([w_ref[0, :, FFN_MAIN:wide], w_ref[1, :, FFN_MAIN:wide]], axis=1))
        up = _dot(a, jnp.concatenate([w_ref[2, :, FFN_MAIN:wide], w_ref[3, :, FFN_MAIN:wide]], axis=1))
        rest = wide - FFN_MAIN
        for s in range(2):
            emit(gate[:, s * rest:(s + 1) * rest], up[:, s * rest:(s + 1) * rest],
                 slice(s * wide + FFN_MAIN, (s + 1) * wide))

    return _rows_call(tag + "_in", [h], [w_in, gain], [(D, BF16)] + [(DFF, BF16)] * 3, compute_in, min(FFN_TM, t),
                      side=side)


def _ffn_out(tag, act, h, w_out, next_gain, target=None):
    t = h.shape[0]
    tm = min(FFN_TM, t)

    def compute_out(rows, weights, outs):
        hn = rows[1][...] + 0.5 * _dot(rows[0][...], weights[0][...])
        g = weights[1][...]
        r = lax.rsqrt(jnp.mean(hn * hn, axis=-1, keepdims=True) + EPS)
        xh = hn * r
        if target is None:
            outs[0][...] = hn
            outs[1][...] = (xh * g).astype(BF16)
        else:
            err = xh * g - rows[2][...]
            dy = err * (1.0 / D)
            dxh = dy * g
            outs[0][...] = r * (dxh - xh * jnp.mean(dxh * xh, axis=-1, keepdims=True))
            outs[1][...] += jnp.sum(dy * xh, axis=0, keepdims=True)
            outs[2][...] += 0.5 * jnp.sum(jnp.mean(err * err, axis=-1, keepdims=True), axis=0, keepdims=True)

    if target is None:
        return _rows_call(tag + "_out", [act, h], [w_out, next_gain], [(D, F32), (D, BF16)], compute_out, tm)
    return _rows_call(tag + "_out", [act, h, target], [w_out, next_gain], [(D, F32)], compute_out, tm, sums=(D, 128))


class _Reduction:
    def __init__(self, tag, c_arr, k_arr):
        self.tag, self.c_arr, self.k_arr = tag, c_arr, k_arr

    def begin(self, bufs, swapped=None):
        if swapped is None:
            swapped = _SwapHalves(bufs).alone("grad_swap_" + self.tag)
        sums = [_chip_sum("grad_chip_sum_%s%d" % (self.tag, b), gp, got, self.c_arr)
                for b, (gp, got) in enumerate(zip(bufs, swapped))]
        self.sums = [s[0] for s in sums]
        return _ScatterChipSums([s[1] for s in sums])

    def end(self, got):
        return [_shard_sum("grad_shard_sum_%s%d" % (self.tag, b), cs, g, self.k_arr)
                for b, (cs, g) in enumerate(zip(self.sums, got))]


def _ffn_bwd(tag, h, gain, w_in, w_out, saved, dout, side, reduction):
    t = h.shape[0]
    tm = min(TM, t)
    n, gate, up, act = saved

    def compute(rows, weights, outs):
        d = rows[0][...].astype(BF16)
        for j in range(DFF // FFN_CHUNK):
            cols = slice(j * FFN_CHUNK, (j + 1) * FFN_CHUNK)
            da = 0.5 * _dot_nt(d, weights[0][cols, :])
            g, u = rows[1][:, cols].astype(F32), rows[2][:, cols].astype(F32)
            s = _sig(g)
            silu = g * s
            outs[0][:, cols] = (da * u * (s + silu * (1.0 - s))).astype(BF16)
            outs[1][:, cols] = (da * silu).astype(BF16)

    dgate, dup, *side_out = _rows_call(tag + "_dact", [dout, gate, up], [w_out], [(DFF, BF16)] * 2, compute,
                                       min(FFN_TM, t), side=side)
    dw_in = _mm_tn(tag + "_dw_gate", n, dgate, tm=D, tn=DFF // 2, stacked=(4, 0))
    dw_in = _mm_tn(tag + "_dw_up", n, dup, tm=D, tn=DFF // 2, stacked=(4, 2), into=dw_in)
    dw_out, dw_in_swapped = _mm_tn(tag + "_dw_out", act, dout, scale=0.5, tm=DFF // 2, tn=D, side=_SwapHalves([dw_in]))
    dw_out = dw_out.reshape(4, DFF // 4, D)
    (dw_out_swapped,) = _SwapHalves([dw_out]).alone("grad_swap_" + tag)
    sending = reduction.begin([dw_in, dw_out], [dw_in_swapped, dw_out_swapped])

    def compute_dn(rows, weights, outs):
        w_ref = weights[0]
        wide = DFF // 2
        dn = jnp.zeros((rows[0].shape[0], D), F32)
        for s in range(2):
            cols = slice(s * wide, s * wide + FFN_MAIN)
            dn = (dn + _dot_nt(rows[0][:, cols], w_ref[s, :, 0:FFN_MAIN])
                  + _dot_nt(rows[1][:, cols], w_ref[2 + s, :, 0:FFN_MAIN]))
        for r, first in ((0, 0), (1, 2)):
            x = jnp.concatenate([rows[r][:, FFN_MAIN:wide], rows[r][:, wide + FFN_MAIN:2 * wide]], axis=1)
            wt = jnp.concatenate([w_ref[first, :, FFN_MAIN:wide], w_ref[first + 1, :, FFN_MAIN:wide]], axis=1)
            dn = dn + _dot_nt(x, wt)
        dx, dg = _rms_bwd_vals(rows[2][...], weights[1][...], dn)
        outs[0][...] = rows[3][...] + dx
        outs[1][...] += jnp.sum(dg, axis=0, keepdims=True)

    dh, dgain, *got = _rows_call(tag + "_dn", [dgate, dup, h, dout], [w_in, gain], [(D, F32)], compute_dn,
                                 min(FFN_TM, t), side=sending, sums=(D,), vmem_mb=58)
    return dh, dgain, side_out, got


def kernel(x, positions, ffn1_norm, ffn1_w_in, ffn1_w_out, mix_norm, w_in, hg_lb_table, hg_out_norm, w_hg_branch, mla_q_lora_norm, w_q_up, mla_kv_lora_norm, w_kv_up, q_head_norm, k_head_norm, w_mla_branch, w_merge, b_merge, w_out, ffn2_norm, ffn2_w_in, ffn2_w_out, final_norm, loss_target, m_ffn1_norm, m_ffn1_w_in, m_ffn1_w_out, m_mix_norm, m_w_in, m_hg_lb_table, m_hg_out_norm, m_w_hg_branch, m_mla_q_lora_norm, m_w_q_up, m_mla_kv_lora_norm, m_w_kv_up, m_q_head_norm, m_k_head_norm, m_w_mla_branch, m_w_merge, m_b_merge, m_w_out, m_ffn2_norm, m_ffn2_w_in, m_ffn2_w_out, m_final_norm, v_ffn1_norm, v_ffn1_w_in, v_ffn1_w_out, v_mix_norm, v_w_in, v_hg_lb_table, v_hg_out_norm, v_w_hg_branch, v_mla_q_lora_norm, v_w_q_up, v_mla_kv_lora_norm, v_w_kv_up, v_q_head_norm, v_k_head_norm, v_w_mla_branch, v_w_merge, v_b_merge, v_w_out, v_ffn2_norm, v_ffn2_w_in, v_ffn2_w_out, v_final_norm):
    a = dict(locals())
    w = {n: a[n] for n in WEIGHT_ORDER}
    mom = {n: a["m_" + n] for n in WEIGHT_ORDER}
    var = {n: a["v_" + n] for n in WEIGHT_ORDER}
    t = x.shape[1]
    tm = min(TM, t)
    xt = x.reshape(t, D)
    target = loss_target.reshape(t, D)
    pos = positions.reshape(t, 1)
    x_i, y_i, c_i = _place()
    k_idx = (2 * x_i + y_i).astype(jnp.int32)
    c_arr = c_i.astype(jnp.int32).reshape(1)
    k_arr = jnp.stack([k_idx, c_i.astype(jnp.int32)])

    group_mid = _group(("w_in", "w_hg_branch", "w_q_up", "w_kv_up", "w_mla_branch", "w_merge", "w_out"))
    use_early = _group(("ffn1_w_out", "w_in", "w_hg_branch", "w_q_up", "w_kv_up"))
    use_late = _group(("w_mla_branch", "w_merge", "w_out", "ffn2_w_out"))
    gather_first = _GatherWeights([w["ffn1_w_in"][0].astype(BF16)])
    gather_early = _GatherWeights([_pack([w[e[0]][0] for e in use_early], BF16)])
    gather_late = _GatherWeights([_pack([w[e[0]][0] for e in use_late], BF16), w["ffn2_w_in"][0].astype(BF16)])
    (ffn1_w_in_g,) = gather_first.gathered(gather_first.alone("gather_first"), k_idx)
    n1, gate1, up1, act1, got = _ffn_in("ffn1", xt, w["ffn1_norm"], ffn1_w_in_g, gather_early)
    full = _unpack_full(gather_early.gathered([got], k_idx)[0], use_early)
    h1, u = _ffn_out("ffn1", act1, xt, full["ffn1_w_out"], w["mix_norm"])
    ffn1_saved = (n1, gate1, up1, act1)
    w_in_full = full["w_in"]
    w_in_hg = w_in_full[:, :4 * D]
    w_in_mla = jnp.pad(w_in_full[:, 4 * D:], ((0, 0), (0, MLA_COLS - (4800 - 4 * D))))
    w_q_pad = jnp.pad(full["w_q_up"].reshape(Q_LORA, HEADS, QK), ((0, 0), (0, 0), (0, QKP - QK))).reshape(Q_LORA, HEADS * QKP)
    w_kv = full["w_kv_up"]
    gq = jnp.pad(w["q_head_norm"], ((0, 0), (0, QKP - QK)))
    gk = jnp.pad(w["k_head_norm"], ((0, 0), (0, QKP - QK)))

    ident = lambda accs, ex: (accs[0],)
    def in_hg(rows, weights, outs):
        a = rows[0][...]
        for j in range(4 * D // 512):
            cols = slice(j * 512, (j + 1) * 512)
            outs[0][:, cols] = _dot(a, weights[0][:, cols])

    (p_hg,) = _rows_call("in_hg", [u], [w_in_hg], [(4 * D, F32)], in_hg, min(FFN_TM, t))
    p_mla, cqn, ckvn = _in_mla(u, w_in_mla, w["mla_q_lora_norm"], w["mla_kv_lora_norm"])
    o_raw, hg_o, states = _hgrn_fwd(p_hg, w["hg_lb_table"], w["hg_out_norm"])
    (y_hg,) = _mm("hg_branch", [_a_spec(hg_o, tm)], [_b_nn(full["w_hg_branch"], 512)], [(0, 0)], ident, [], [BF16], t, D, tm, 512)
    (qf,) = _mm("q_up", [_a_spec(cqn, tm)], [_b_nn(w_q_pad, 512)], [(0, 0)], ident, [], [F32], t, HEADS * QKP, tm, 512)
    (kvf,) = _mm("kv_up", [_a_spec(ckvn, tm)], [_b_nn(w_kv, 512)], [(0, 0)], ident, [], [F32], t, HEADS * QKP, tm, 512)
    cos, sin = _rope_tables(pos)
    qh, kh, vh = _mla_prep_fwd(qf, kvf, p_mla, cos, sin, gq, gk)
    o_mla, lse, *got = _flash_fwd(qh, kh, vh, side=gather_late)
    late, ffn2_w_in_g = gather_late.gathered(got, k_idx)
    full.update(_unpack_full(late, use_late))
    (y_mla,) = _mm("mla_branch", [_a_spec(o_mla, tm)], [_b_nn(full["w_mla_branch"], 512)], [(0, 0)], ident, [], [BF16], t, D, tm, 512)

    def merge_epi(accs, ex):
        g_hg = _sig(accs[0] + ex[2])
        g_mla = _sig(accs[1] + ex[3])
        return g_hg * ex[0].astype(F32) + g_mla * ex[1].astype(F32), g_hg, g_mla

    w_merge_f = full["w_merge"]
    mix, g_hg, g_mla = _mm(
        "merge", [_a_spec(u, tm)], [_b_nn(w_merge_f, 512), _b_nn(w_merge_f, 512, D // 512)], [(0, 0), (0, 1)], merge_epi,
        [_e_tile(y_hg, tm, 512), _e_tile(y_mla, tm, 512), _e_row(w["b_merge"], 512), _e_row(w["b_merge"], 512, D // 512)],
        [BF16, BF16, BF16], t, D, tm, 512)
    (h2,) = _mm("out_proj", [_a_spec(mix, tm)], [_b_nn(full["w_out"], 512)], [(0, 0)],
                lambda accs, ex: (ex[0] + accs[0],), [_e_tile(h1, tm, 512)], [F32], t, D, tm, 512)
    ffn2_saved = _ffn_in("ffn2", h2, w["ffn2_norm"], ffn2_w_in_g)
    dh3, d_final_norm, loss_part = _ffn_out("ffn2", ffn2_saved[3], h2, full["ffn2_w_out"], w["final_norm"], target=target)

    grads, small = {}, {}
    small["final_norm"] = d_final_norm
    reduce_last = _Reduction("last", c_arr, k_arr)
    reduce_mid = _Reduction("mid", c_arr, k_arr)
    reduce_first = _Reduction("first", c_arr, k_arr)
    dh2, small["ffn2_norm"], _, got_last = _ffn_bwd(
        "ffn2", h2, w["ffn2_norm"], ffn2_w_in_g, full["ffn2_w_out"], ffn2_saved, dh3, None, reduce_last)

    def dmix_epi(accs, ex):
        dm = accs[0]
        ghg, gml, yhg, yml = [e.astype(F32) for e in ex]
        return dm * ghg, dm * gml, dm * yhg * ghg * (1.0 - ghg), dm * yml * gml * (1.0 - gml)

    dy_hg, dy_mla, dpre_hg, dpre_mla = _mm(
        "d_mix", [_a_spec(dh2, tm)], [_b_nt(full["w_out"], 512)], [(0, 0)], dmix_epi,
        [_e_tile(g_hg, tm, 512), _e_tile(g_mla, tm, 512), _e_tile(y_hg, tm, 512), _e_tile(y_mla, tm, 512)],
        [BF16, BF16, BF16, BF16], t, D, tm, 512, trans_b=True)
    grads["w_out"] = _mm_tn("dw_out", mix, dh2)
    small["b_merge"] = jnp.concatenate([_colsum("db_hg", dpre_hg), _colsum("db_mla", dpre_mla)], axis=1)
    grads["w_merge"] = jnp.concatenate([_mm_tn("dw_merge_hg", u, dpre_hg), _mm_tn("dw_merge_mla", u, dpre_mla)], axis=1)
    grads["w_hg_branch"] = _mm_tn("dw_hg_branch", hg_o, dy_hg)
    grads["w_mla_branch"] = _mm_tn("dw_mla_branch", o_mla, dy_mla)
    (dho,) = _mm("d_hg_o", [_a_spec(dy_hg, tm)], [_b_nt(full["w_hg_branch"], 512)], [(0, 0)], ident, [], [BF16], t, D, tm, 512, trans_b=True)
    (do_mla,) = _mm("d_o_mla", [_a_spec(dy_mla, tm)], [_b_nt(full["w_mla_branch"], 512)], [(0, 0)], ident, [], [BF16], t, D, tm, 512, trans_b=True)

    dq_raw, df_raw, di_raw, dg_raw, small["hg_lb_table"], small["hg_out_norm"] = _hgrn_bwd(
        p_hg, w["hg_lb_table"], w["hg_out_norm"], o_raw, states, dho)
    dp_hg = [dq_raw, df_raw, di_raw, dg_raw]

    dqh, dkh, dvh = _flash_bwd(qh, kh, vh, lse, _attn_do(do_mla, o_mla))
    dqf, dkvf, dkpe, dgq, dgk = _mla_prep_bwd(qf, kvf, p_mla, cos, sin, gq, gk, dqh, dkh, dvh)
    small["q_head_norm"] = dgq[:, :QK]
    small["k_head_norm"] = dgk[:, :QK]
    dwq_pad = _mm_tn("dw_q_up", cqn, dqf, tm=Q_LORA, tn=1024)
    grads["w_q_up"] = dwq_pad.reshape(Q_LORA, HEADS, QKP)[:, :, :QK].reshape(Q_LORA, HEADS * QK)
    grads["w_kv_up"] = _mm_tn("dw_kv_up", ckvn, dkvf, tm=KV_LORA, tn=1024)
    (dcqn,) = _mm("d_cq", [_a_spec(dqf, tm)], [_b_nt(w_q_pad, Q_LORA)], [(0, 0)], ident, [], [F32], t, Q_LORA, tm, Q_LORA, trans_b=True)
    (dckvn,) = _mm("d_ckv", [_a_spec(dkvf, tm)], [_b_nt(w_kv, KV_LORA)], [(0, 0)], ident, [], [F32], t, KV_LORA, tm, KV_LORA, trans_b=True)
    dp_mla, small["mla_q_lora_norm"], small["mla_kv_lora_norm"] = _lora_norm_bwd(
        p_mla, w["mla_q_lora_norm"], w["mla_kv_lora_norm"], dcqn, dckvn, dkpe)

    dw_in_hg = [_mm_tn("dw_in_hg%d" % k, u, dp_hg[k]) for k in range(4)]
    dw_in_mla = _mm_tn("dw_in_mla", u, dp_mla, tn=MLA_COLS)
    grads["w_in"] = jnp.concatenate(dw_in_hg + [dw_in_mla[:, :4800 - 4 * D]], axis=1)
    tm_du = min(TM // 2, t)
    du, *got_mid = _mm(
        "d_u",
        [_a_spec(dpre_hg, tm_du), _a_spec(dpre_mla, tm_du)] + [_a_spec(d, tm_du) for d in dp_hg] + [_a_spec(dp_mla, tm_du)],
        [_b_nt(w_merge_f, 512, D, 0), _b_nt(w_merge_f, 512, D, 1)]
        + [_b_nt(w_in_hg, 512, D, k) for k in range(4)] + [_b_nt(w_in_mla, 512)],
        [(k, k) for k in range(7)],
        lambda accs, ex: (functools.reduce(lambda p, q: p + q, accs),), [], [F32], t, D, tm_du, 512, trans_b=True,
        side=reduce_mid.begin([_pack_grads(grads, group_mid)]))
    dh1, small["mix_norm"] = _rms_bwd("mix_dnorm", h1, w["mix_norm"], du, dh2)
    dx, small["ffn1_norm"], _, got_first = _ffn_bwd(
        "ffn1", xt, w["ffn1_norm"], ffn1_w_in_g, full["ffn1_w_out"], ffn1_saved, dh1, None, reduce_first)

    mid, ffn2_in, ffn2_out, ffn1_in, ffn1_out = _join_halves(
        "grad_join", reduce_mid.end(got_mid) + reduce_last.end(got_last) + reduce_first.end(got_first))
    g_shard = _unpack_shard(mid, group_mid)
    g_shard.update(ffn2_w_in=ffn2_in, ffn2_w_out=ffn2_out, ffn1_w_in=ffn1_in, ffn1_w_out=ffn1_out)
    small_sum = _all_reduce_small(_pack_small([small[n] for n, _ in SMALL] + [loss_part])).reshape(-1)
    g_small, at = {}, 0
    for n, shape in SMALL:
        size = shape[0] * shape[1]
        g_small[n] = small_sum[at:at + size].reshape(shape)
        at += size
    loss = small_sum[at]

    g_out, d_out, m_out, v_out = {}, {}, {}, {}
    for n in WEIGHT_ORDER:
        shape = w[n].shape
        g = g_shard[n] if n in g_shard else g_small[n]
        two = g.shape
        d_, m_, v_ = _adamw("adamw_" + n, w[n].reshape(two), g, mom[n].reshape(two), var[n].reshape(two))
        g_out[n], d_out[n], m_out[n], v_out[n] = g.reshape(shape), d_.reshape(shape), m_.reshape(shape), v_.reshape(shape)

    return (loss, dx.reshape(x.shape), *[g_out[n] for n in WEIGHT_ORDER], *[d_out[n] for n in WEIGHT_ORDER],
            *[m_out[n] for n in WEIGHT_ORDER], *[v_out[n] for n in WEIGHT_ORDER])
```

```python
import functools

import numpy as np
import jax
import jax.numpy as jnp
from jax import lax
from jax.experimental import pallas as pl
from jax.experimental.pallas import tpu as pltpu

F32 = jnp.float32
BF16 = jnp.bfloat16
MESH = pl.DeviceIdType.MESH

D = 1024
DFF = 2816
HEADS = 8
HK = 128
CHUNK = 64
ROPE = 64
QK = 192
QKP = 256
Q_LORA = 384
KV_LORA = 256
MLA_COLS = 768
EPS = 1e-6
ROPE_THETA = 10000.0
SCALE = QK ** -0.5
LOG2E = 1.4426950408889634
LN2 = 0.6931471805599453
NEG = -1e30
EXP_CLAMP = 80.0

ADAM_LR = 0.001
ADAM_B1 = 0.9
ADAM_B2 = 0.999
ADAM_EPS = 1e-08
ADAM_WD = 0.01
ADAM_STEP = 10

PACK_W = 1024
ADD_ROWS = 352
PACK_ALIGN = 2 * ADD_ROWS

TM = 1024
FFN_TM = 512
FFN_CHUNK = 256
FFN_MAIN = 1280
TQ = 2048
SUBQ = 256
HG_BT = 512
HG_HPB = 8
TT = 2048
ROW_TM = 512
PREP_TM = 256

VMEM_MB = 48

BIG = (
    ("ffn1_w_in", D, 2 * DFF, 1),
    ("ffn1_w_out", DFF, D, 0),
    ("w_in", D, 4800, 1),
    ("w_hg_branch", D, D, 0),
    ("w_q_up", Q_LORA, HEADS * QK, 1),
    ("w_kv_up", KV_LORA, HEADS * 2 * HK, 1),
    ("w_mla_branch", D, D, 0),
    ("w_merge", D, 2 * D, 1),
    ("w_out", D, D, 0),
    ("ffn2_w_in", D, 2 * DFF, 1),
    ("ffn2_w_out", DFF, D, 0),
)
SMALL = (
    ("ffn1_norm", (1, D)),
    ("mix_norm", (1, D)),
    ("hg_lb_table", (2, D)),
    ("hg_out_norm", (1, HK)),
    ("mla_q_lora_norm", (1, Q_LORA)),
    ("mla_kv_lora_norm", (1, KV_LORA)),
    ("q_head_norm", (1, QK)),
    ("k_head_norm", (1, QK)),
    ("b_merge", (1, 2 * D)),
    ("ffn2_norm", (1, D)),
    ("final_norm", (1, D)),
)
WEIGHT_ORDER = ("ffn1_norm", "ffn1_w_in", "ffn1_w_out", "mix_norm", "w_in", "hg_lb_table", "hg_out_norm",
                "w_hg_branch", "mla_q_lora_norm", "w_q_up", "mla_kv_lora_norm", "w_kv_up", "q_head_norm",
                "k_head_norm", "w_mla_branch", "w_merge", "b_merge", "w_out", "ffn2_norm", "ffn2_w_in",
                "ffn2_w_out", "final_norm")


def _call(body, **kw):
    return pl.pallas_call(body, **kw)


def _cp(vmem_mb=VMEM_MB):
    return pltpu.CompilerParams(vmem_limit_bytes=vmem_mb << 20)


def _dot(a, b):
    return lax.dot_general(a, b, (((1,), (0,)), ((), ())), preferred_element_type=F32)


def _dot_nt(a, b):
    return lax.dot_general(a, b, (((1,), (1,)), ((), ())), preferred_element_type=F32)


def _dot_tn(a, b):
    return lax.dot_general(a, b, (((0,), (0,)), ((), ())), preferred_element_type=F32)


def _sig(x):
    return jax.nn.sigmoid(x)


def _silu(x):
    return x * _sig(x)


def _dsilu(x):
    s = _sig(x)
    return s * (1.0 + x * (1.0 - s))


def _a_spec(arr, tm, kblk=None, kidx=0):
    kb = arr.shape[1] if kblk is None else kblk
    return arr, pl.BlockSpec((tm, kb), lambda i, j, kidx=kidx: (i, kidx)), slice(kidx * kb, (kidx + 1) * kb)


def _b_nn(arr, tn, off=0):
    return arr, pl.BlockSpec((arr.shape[0], tn), lambda i, j, off=off: (0, j + off)), ("cols", off)


def _b_nt(arr, tn, kblk=None, kidx=0):
    kb = arr.shape[1] if kblk is None else kblk
    return arr, pl.BlockSpec((tn, kb), lambda i, j, kidx=kidx: (j, kidx)), ("rows", slice(kidx * kb, (kidx + 1) * kb))


def _e_tile(arr, tm, tn, off=0):
    return arr, pl.BlockSpec((tm, tn), lambda i, j, off=off: (i, j + off)), ("tile", off)


def _e_row(arr, tn, off=0):
    return arr, pl.BlockSpec((1, tn), lambda i, j, off=off: (0, j + off)), ("row", off)


def _mm_resident(name, As, Bs, dots, epi, extras, out_dtypes, m, n, tn):
    def unique(arrays):
        seen = []
        for a in arrays:
            if not any(a is s for s in seen):
                seen.append(a)
        return seen

    rows = unique([a for a, _, _ in As] + [e for e, _, where in extras if where[0] == "tile"])
    weights = unique([b for b, _, _ in Bs] + [e for e, _, where in extras if where[0] == "row"])

    def ref_of(arr, row_refs, weight_refs):
        for r, ref in zip(rows, row_refs):
            if r is arr:
                return ref
        for wt, ref in zip(weights, weight_refs):
            if wt is arr:
                return ref

    def compute(row_refs, weight_refs, out_refs):
        a_vals = [ref_of(a, row_refs, weight_refs)[:, ks].astype(BF16) for a, _, ks in As]
        for j in range(n // tn):
            accs = []
            for ai, bi in dots:
                b, _, where = Bs[bi]
                b_ref = ref_of(b, row_refs, weight_refs)
                if where[0] == "cols":
                    accs.append(_dot(a_vals[ai], b_ref[:, (j + where[1]) * tn:(j + where[1] + 1) * tn]))
                else:
                    accs.append(_dot_nt(a_vals[ai], b_ref[j * tn:(j + 1) * tn, where[1]]))
            ex = [ref_of(e, row_refs, weight_refs)[:, (j + where[1]) * tn:(j + where[1] + 1) * tn]
                  for e, _, where in extras]
            for o_ref, o in zip(out_refs, epi(accs, ex)):
                o_ref[:, j * tn:(j + 1) * tn] = o.astype(o_ref.dtype)

    return _rows_call(name, rows, weights, [(n, dt) for dt in out_dtypes], compute, min(FFN_TM, m))


def _mm(name, As, Bs, dots, epi, extras, out_dtypes, m, n, tm, tn, trans_b=False, side=None):
    if side is None:
        return _mm_resident(name, As, Bs, dots, epi, extras, out_dtypes, m, n, tn)
    na, nb, ne, no = len(As), len(Bs), len(extras), len(out_dtypes)
    ni, nj = m // tm, n // tn
    s_in = len(side.inputs) if side else 0
    s_out = len(side.out_shapes) if side else 0

    def body(*refs):
        a_refs = refs[:na]
        b_refs = refs[na:na + nb]
        e_refs = refs[na + nb:na + nb + ne]
        at = na + nb + ne
        side_refs = refs[at:at + s_in]
        o_refs = refs[at + s_in:at + s_in + no]
        side_refs = list(side_refs) + list(refs[at + s_in + no:])
        if side:
            i, j = pl.program_id(0), pl.program_id(1)

            @pl.when(jnp.logical_and(i == 0, j == 0))
            def _():
                side.start(*side_refs)

        a_vals = [r[...].astype(BF16) for r in a_refs]
        accs = []
        for ai, bi in dots:
            b = b_refs[bi][...]
            accs.append(_dot_nt(a_vals[ai], b) if trans_b else _dot(a_vals[ai], b))
        outs = epi(accs, [r[...] for r in e_refs])
        for o_ref, o in zip(o_refs, outs):
            o_ref[...] = o.astype(o_ref.dtype)
        if side:
            @pl.when(jnp.logical_and(i == ni - 1, j == nj - 1))
            def _():
                side.finish(*side_refs)

    ops = list(As) + list(Bs) + list(extras)
    anywhere = pl.BlockSpec(memory_space=pl.ANY)
    res = _call(
        body, name=name,
        grid=(ni, nj),
        in_specs=[op[1] for op in ops] + [anywhere] * s_in,
        out_specs=[pl.BlockSpec((tm, tn), lambda i, j: (i, j)) for _ in out_dtypes] + [anywhere] * s_out,
        out_shape=[jax.ShapeDtypeStruct((m, n), dt) for dt in out_dtypes] + (list(side.out_shapes) if side else []),
        scratch_shapes=list(side.scratch) if side else [],
        compiler_params=_cp(),
    )(*[op[0] for op in ops], *(side.inputs if side else []))
    return res


def _rows_call(name, rows, weights, outs, compute, tm, side=None, sums=(), vmem_mb=VMEM_MB):
    t = rows[0].shape[0]
    nr, nw, no = len(rows), len(weights), len(outs) + len(sums)
    ni = t // tm
    s_in = len(side.inputs) if side else 0
    s_out = len(side.out_shapes) if side else 0

    def body(*refs):
        at = nr + nw
        side_refs = list(refs[at:at + s_in]) + list(refs[at + s_in + no:])
        if side:
            @pl.when(pl.program_id(0) == 0)
            def _():
                side.start(*side_refs)

        out_refs = refs[at + s_in:at + s_in + no]
        if sums:
            @pl.when(pl.program_id(0) == 0)
            def _():
                for r in out_refs[len(outs):]:
                    r[...] = jnp.zeros_like(r)

        compute(refs[:nr], refs[nr:at], out_refs)
        if side:
            @pl.when(pl.program_id(0) == ni - 1)
            def _():
                side.finish(*side_refs)

    anywhere = pl.BlockSpec(memory_space=pl.ANY)
    return _call(
        body, name=name, grid=(ni,),
        in_specs=[pl.BlockSpec((tm, r.shape[1]), lambda i: (i, 0)) for r in rows]
        + [pl.BlockSpec(wt.shape, lambda i, nd=wt.ndim: (0,) * nd) for wt in weights] + [anywhere] * s_in,
        out_specs=[pl.BlockSpec((tm, width), lambda i: (i, 0)) for width, _ in outs]
        + [pl.BlockSpec((1, width), lambda i: (0, 0)) for width in sums] + [anywhere] * s_out,
        out_shape=[jax.ShapeDtypeStruct((t, width), dt) for width, dt in outs]
        + [jax.ShapeDtypeStruct((1, width), F32) for width in sums] + (list(side.out_shapes) if side else []),
        scratch_shapes=list(side.scratch) if side else [],
        compiler_params=_cp(vmem_mb),
    )(*rows, *weights, *(side.inputs if side else []))


def _mm_tn(name, a, b, scale=1.0, tm=1024, tn=1024, stacked=None, into=None, side=None):
    t, m = a.shape
    n = b.shape[1]
    tm, tn, tt = min(tm, m), min(tn, n), min(TT, t)
    ni, nj, nk = m // tm, n // tn, t // tt
    extra_in = [into] if into is not None else list(side.inputs) if side else []
    s_out = len(side.out_shapes) if side else 0

    def body(a_ref, b_ref, *rest):
        o_ref = rest[len(extra_in)]
        i, j, k = pl.program_id(0), pl.program_id(1), pl.program_id(2)
        if side:
            side_refs = list(rest[:len(extra_in)]) + list(rest[len(extra_in) + 1:])

            @pl.when(jnp.logical_and(jnp.logical_and(i == 0, j == 0), k == 0))
            def _():
                side.start(*side_refs)

        @pl.when(k == 0)
        def _():
            o_ref[...] = jnp.zeros_like(o_ref)

        o_ref[...] += _dot_tn(a_ref[...].astype(BF16), b_ref[...].astype(BF16))
        if scale != 1.0:
            @pl.when(k == nk - 1)
            def _():
                o_ref[...] = o_ref[...] * scale
        if side:
            @pl.when(jnp.logical_and(jnp.logical_and(i == ni - 1, j == nj - 1), k == nk - 1))
            def _():
                side.finish(*side_refs)

    anywhere = pl.BlockSpec(memory_space=pl.ANY)
    product = jax.ShapeDtypeStruct((stacked[0], m, tn) if stacked else (m, n), F32)
    res = _call(
        body, name=name,
        grid=(ni, nj, nk),
        in_specs=[pl.BlockSpec((tt, tm), lambda i, j, k: (k, i)), pl.BlockSpec((tt, tn), lambda i, j, k: (k, j))]
        + [anywhere] * len(extra_in),
        out_specs=[pl.BlockSpec((None, tm, tn), lambda i, j, k: (stacked[1] + j, i, 0)) if stacked
                   else pl.BlockSpec((tm, tn), lambda i, j, k: (i, j))] + [anywhere] * s_out,
        out_shape=[product] + (list(side.out_shapes) if side else []),
        input_output_aliases={2: 0} if into is not None else {},
        scratch_shapes=list(side.scratch) if side else [],
        compiler_params=_cp(),
    )(a, b, *extra_in)
    return res if side else res[0]


def _rms_bwd_vals(xv, g, dn):
    r = lax.rsqrt(jnp.mean(xv * xv, axis=-1, keepdims=True) + EPS)
    xh = xv * r
    dxh = dn * g
    c = jnp.mean(dxh * xh, axis=-1, keepdims=True)
    return r * (dxh - xh * c), dn * xh


def _rms_bwd(name, x, gain, dn, dres):
    t, d = x.shape
    tm = min(ROW_TM, t)

    def body(x_ref, g_ref, dn_ref, dr_ref, dx_ref, dg_ref):
        @pl.when(pl.program_id(0) == 0)
        def _():
            dg_ref[...] = jnp.zeros_like(dg_ref)

        dx, dg = _rms_bwd_vals(x_ref[...], g_ref[...], dn_ref[...].astype(F32))
        dx_ref[...] = dr_ref[...] + dx
        dg_ref[...] += jnp.sum(dg, axis=0, keepdims=True)

    row = pl.BlockSpec((tm, d), lambda i: (i, 0))
    one = pl.BlockSpec((1, d), lambda i: (0, 0))
    return _call(
        body, name=name, grid=(t // tm,),
        in_specs=[row, one, row, row],
        out_specs=[row, one],
        out_shape=[jax.ShapeDtypeStruct((t, d), F32), jax.ShapeDtypeStruct((1, d), F32)],
        compiler_params=_cp(),
    )(x, gain, dn, dres)


def _colsum(name, x):
    t, n = x.shape
    tm = min(TM, t)

    def body(x_ref, o_ref):
        @pl.when(pl.program_id(0) == 0)
        def _():
            o_ref[...] = jnp.zeros_like(o_ref)

        o_ref[...] += jnp.sum(x_ref[...].astype(F32), axis=0, keepdims=True)

    return _call(
        body, name=name, grid=(t // tm,),
        in_specs=[pl.BlockSpec((tm, n), lambda i: (i, 0))],
        out_specs=pl.BlockSpec((1, n), lambda i: (0, 0)),
        out_shape=jax.ShapeDtypeStruct((1, n), F32),
        compiler_params=_cp(),
    )(x)


def _in_mla(u, w_in_mla, gq, gkv):
    t = u.shape[0]

    def compute(rows, weights, outs):
        p = _dot(rows[0][...], weights[0][...])
        outs[0][...] = p
        cq = p[:, 0:Q_LORA]
        ckv = p[:, Q_LORA:Q_LORA + KV_LORA]
        rq = lax.rsqrt(jnp.mean(cq * cq, axis=-1, keepdims=True) + EPS)
        rkv = lax.rsqrt(jnp.mean(ckv * ckv, axis=-1, keepdims=True) + EPS)
        outs[1][...] = (cq * rq * weights[1][...]).astype(BF16)
        outs[2][...] = (ckv * rkv * weights[2][...]).astype(BF16)

    return _rows_call("in_mla", [u], [w_in_mla, gq, gkv], [(MLA_COLS, F32), (Q_LORA, BF16), (KV_LORA, BF16)], compute,
                      min(FFN_TM, t))


def _lora_norm_bwd(p_mla, gq, gkv, dcqn, dckvn, dkpe):
    t = p_mla.shape[0]
    tm = min(ROW_TM, t)

    def body(p_ref, gq_ref, gkv_ref, dq_ref, dkv_ref, dkpe_ref, dp_ref, dgq_ref, dgkv_ref):
        @pl.when(pl.program_id(0) == 0)
        def _():
            dgq_ref[...] = jnp.zeros_like(dgq_ref)
            dgkv_ref[...] = jnp.zeros_like(dgkv_ref)

        dcq, dgq = _rms_bwd_vals(p_ref[:, 0:Q_LORA], gq_ref[...], dq_ref[...])
        dckv, dgkv = _rms_bwd_vals(p_ref[:, Q_LORA:Q_LORA + KV_LORA], gkv_ref[...], dkv_ref[...])
        dp_ref[:, 0:Q_LORA] = dcq.astype(BF16)
        dp_ref[:, Q_LORA:Q_LORA + KV_LORA] = dckv.astype(BF16)
        dp_ref[:, Q_LORA + KV_LORA:MLA_COLS] = dkpe_ref[...].astype(BF16)
        dgq_ref[...] += jnp.sum(dgq, axis=0, keepdims=True)
        dgkv_ref[...] += jnp.sum(dgkv, axis=0, keepdims=True)

    return _call(
        body, name="lora_norm_bwd", grid=(t // tm,),
        in_specs=[pl.BlockSpec((tm, MLA_COLS), lambda i: (i, 0)),
                  pl.BlockSpec((1, Q_LORA), lambda i: (0, 0)), pl.BlockSpec((1, KV_LORA), lambda i: (0, 0)),
                  pl.BlockSpec((tm, Q_LORA), lambda i: (i, 0)), pl.BlockSpec((tm, KV_LORA), lambda i: (i, 0)),
                  pl.BlockSpec((tm, HK), lambda i: (i, 0))],
        out_specs=[pl.BlockSpec((tm, MLA_COLS), lambda i: (i, 0)),
                   pl.BlockSpec((1, Q_LORA), lambda i: (0, 0)), pl.BlockSpec((1, KV_LORA), lambda i: (0, 0))],
        out_shape=[jax.ShapeDtypeStruct((t, MLA_COLS), BF16), jax.ShapeDtypeStruct((1, Q_LORA), F32),
                   jax.ShapeDtypeStruct((1, KV_LORA), F32)],
        compiler_params=_cp(),
    )(p_mla, gq, gkv, dcqn, dckvn, dkpe)


def _cumsum_rows(x, row):
    for s in (1, 2, 4, 8, 16, 32):
        x = x + jnp.where(row >= s, pltpu.roll(x, s, 0), 0.0)
    return x


def _rcumsum_rows(x, row):
    for s in (1, 2, 4, 8, 16, 32):
        x = x + jnp.where(row < CHUNK - s, pltpu.roll(x, CHUNK - s, 0), 0.0)
    return x


def _hg_gates(qr, z, lb, row):
    q = _silu(qr)
    sg = _sig(z)
    f = lb + (1.0 - lb) * sg
    lf = jnp.log(f)
    k = (1.0 - lb) * (1.0 - sg)
    cum = _cumsum_rows(lf, row)
    mid = jnp.sum(jnp.where(row < CHUNK // 2, lf, 0.0), axis=0, keepdims=True)
    last = jnp.sum(lf, axis=0, keepdims=True)
    e_q = jnp.exp(jnp.minimum(cum - mid, EXP_CLAMP))
    e_k = jnp.exp(jnp.minimum(mid - cum, EXP_CLAMP))
    e_a = jnp.exp(cum)
    e_l = jnp.exp(last - cum)
    return q, sg, f, k, last, e_q, e_k, e_a, e_l


def _hgrn_fwd(p_hg, tab, gain):
    t = p_hg.shape[0]
    bt = min(HG_BT, t)
    nb, nc = t // bt, bt // CHUNK

    hpb = HG_HPB
    wide = hpb * HK

    def body(q_ref, f_ref, i_ref, g_ref, tab_ref, gain_ref, o_ref, ho_ref, st_ref, state):
        @pl.when(pl.program_id(1) == 0)
        def _():
            state[...] = jnp.zeros_like(state)

        row = lax.broadcasted_iota(jnp.int32, (CHUNK, HK), 0)
        tril = lax.broadcasted_iota(jnp.int32, (CHUNK, CHUNK), 0) >= lax.broadcasted_iota(jnp.int32, (CHUNK, CHUNK), 1)
        gain_v = gain_ref[...]

        def chunk(c, carry):
            sl = pl.ds(pl.multiple_of(c * CHUNK, CHUNK), CHUNK)
            for hh in range(hpb):
                ln = slice(hh * HK, (hh + 1) * HK)
                lb = _sig(tab_ref[0:1, ln] - tab_ref[1:2, ln])
                v = i_ref[sl, ln].astype(BF16)
                q, _, _, k, last, e_q, e_k, e_a, e_l = _hg_gates(q_ref[sl, ln], f_ref[sl, ln], lb, row)
                st = state[hh]
                st_ref[hh, c] = st
                p = jnp.where(tril, _dot_nt((q * e_q).astype(BF16), (k * e_k).astype(BF16)), 0.0)
                o = _dot(p.astype(BF16), v) + _dot_nt((q * e_a).astype(BF16), st.astype(BF16))
                state[hh] = jnp.exp(last) * st + _dot_tn(v, (k * e_l).astype(BF16))
                o_ref[sl, ln] = o
                r = lax.rsqrt(jnp.mean(o * o, axis=-1, keepdims=True) + EPS)
                ho_ref[sl, ln] = (o * r * gain_v * _silu(g_ref[sl, ln])).astype(BF16)
            return carry

        lax.fori_loop(0, nc, chunk, 0)

    def col(k):
        return pl.BlockSpec((bt, wide), lambda h, j, k=k: (j, k * (HEADS // hpb) + h))

    return _call(
        body, name="hgrn_fwd", grid=(HEADS // hpb, nb),
        in_specs=[col(0), col(1), col(2), col(3),
                  pl.BlockSpec((2, wide), lambda h, j: (0, h)), pl.BlockSpec((1, HK), lambda h, j: (0, 0))],
        out_specs=[pl.BlockSpec((bt, wide), lambda h, j: (j, h)), pl.BlockSpec((bt, wide), lambda h, j: (j, h)),
                   pl.BlockSpec((hpb, nc, HK, HK), lambda h, j: (h, j, 0, 0))],
        out_shape=[jax.ShapeDtypeStruct((t, D), F32), jax.ShapeDtypeStruct((t, D), BF16),
                   jax.ShapeDtypeStruct((HEADS, t // CHUNK, HK, HK), F32)],
        scratch_shapes=[pltpu.VMEM((hpb, HK, HK), F32)],
        compiler_params=_cp(),
    )(p_hg, p_hg, p_hg, p_hg, tab, gain)


def _hgrn_bwd(p_hg, tab, gain, o_raw, states, dho):
    t = p_hg.shape[0]
    bt = min(HG_BT, t)
    nb, nc = t // bt, bt // CHUNK
    hpb = HG_HPB
    wide = hpb * HK

    def body(q_ref, f_ref, i_ref, g_ref, tab_ref, gain_ref, o_ref, st_ref, dho_ref,
             dq_ref, df_ref, di_ref, dg_ref, dtab_ref, dgain_ref, dstate, dlb):
        h, j = pl.program_id(0), pl.program_id(1)

        @pl.when(jnp.logical_and(h == 0, j == 0))
        def _():
            dgain_ref[...] = jnp.zeros_like(dgain_ref)

        @pl.when(j == 0)
        def _():
            dstate[...] = jnp.zeros_like(dstate)
            dlb[...] = jnp.zeros_like(dlb)

        row = lax.broadcasted_iota(jnp.int32, (CHUNK, HK), 0)
        tril = lax.broadcasted_iota(jnp.int32, (CHUNK, CHUNK), 0) >= lax.broadcasted_iota(jnp.int32, (CHUNK, CHUNK), 1)
        gain_v = gain_ref[...]

        def chunk(cc, carry):
            c = nc - 1 - cc
            sl = pl.ds(pl.multiple_of(c * CHUNK, CHUNK), CHUNK)
            dgain = jnp.zeros((1, HK), F32)
            for hh in range(hpb):
                ln = slice(hh * HK, (hh + 1) * HK)
                lb = _sig(tab_ref[0:1, ln] - tab_ref[1:2, ln])
                qr = q_ref[sl, ln]
                v = i_ref[sl, ln].astype(BF16)
                gr = g_ref[sl, ln]
                q, sg, f, k, last, e_q, e_k, e_a, e_l = _hg_gates(qr, f_ref[sl, ln], lb, row)
                o = o_ref[sl, ln]
                r = lax.rsqrt(jnp.mean(o * o, axis=-1, keepdims=True) + EPS)
                oh = o * r
                dh = dho_ref[sl, ln].astype(F32)
                dnorm = dh * _silu(gr)
                dg_ref[sl, ln] = (dh * oh * gain_v * _dsilu(gr)).astype(BF16)
                dgain = dgain + jnp.sum(dnorm * oh, axis=0, keepdims=True)
                dxh = dnorm * gain_v
                do = (r * (dxh - oh * jnp.mean(dxh * oh, axis=-1, keepdims=True))).astype(BF16)
                st0 = st_ref[hh, c]
                st0_b = st0.astype(BF16)
                ds1 = dstate[hh]
                ds1_b = ds1.astype(BF16)
                qt = (q * e_q).astype(BF16)
                kt = (k * e_k).astype(BF16)
                qd = (q * e_a).astype(BF16)
                kd = (k * e_l).astype(BF16)
                p = jnp.where(tril, _dot_nt(qt, kt), 0.0).astype(BF16)
                dp = jnp.where(tril, _dot_nt(do, v), 0.0).astype(BF16)
                dv = _dot_tn(p, do) + _dot_nt(kd, ds1_b)
                dqt = _dot(dp, kt)
                dkt = _dot_tn(dp, qt)
                dq_inter = _dot(do, st0_b) * e_a
                dk_inter = _dot(v, ds1_b) * e_l
                dq = dqt * e_q + dq_inter
                dk = dkt * e_k + dk_inter
                e_last = jnp.exp(last)
                dstate[hh] = _dot_tn(do, qd) + e_last * ds1
                dlast = (jnp.sum(k * dk_inter, axis=0, keepdims=True)
                         + e_last * jnp.sum(ds1 * st0, axis=0, keepdims=True))
                da = (qt.astype(F32) * dqt - kt.astype(F32) * dkt + q * dq_inter - k * dk_inter
                      + jnp.where(row == CHUNK - 1, dlast, 0.0))
                dlf = _rcumsum_rows(da, row)
                dfv = dlf / f - dk
                df_ref[sl, ln] = (dfv * (1.0 - lb) * sg * (1.0 - sg)).astype(BF16)
                dlb[:, ln] += jnp.sum(dfv * (1.0 - sg), axis=0, keepdims=True)
                dq_ref[sl, ln] = (dq * _dsilu(qr)).astype(BF16)
                di_ref[sl, ln] = dv.astype(BF16)
            dgain_ref[...] += dgain
            return carry

        lax.fori_loop(0, nc, chunk, 0)

        @pl.when(j == nb - 1)
        def _():
            lb = _sig(tab_ref[0:1, :] - tab_ref[1:2, :])
            d0 = dlb[...] * lb * (1.0 - lb)
            dtab_ref[0:1, :] = d0
            dtab_ref[1:2, :] = -d0

    def col(k):
        return pl.BlockSpec((bt, wide), lambda h, j, k=k: (nb - 1 - j, k * (HEADS // hpb) + h))

    tok = pl.BlockSpec((bt, wide), lambda h, j: (nb - 1 - j, h))
    return _call(
        body, name="hgrn_bwd", grid=(HEADS // hpb, nb),
        in_specs=[col(0), col(1), col(2), col(3),
                  pl.BlockSpec((2, wide), lambda h, j: (0, h)), pl.BlockSpec((1, HK), lambda h, j: (0, 0)),
                  tok, pl.BlockSpec((hpb, nc, HK, HK), lambda h, j: (h, nb - 1 - j, 0, 0)), tok],
        out_specs=[tok, tok, tok, tok,
                   pl.BlockSpec((2, wide), lambda h, j: (0, h)), pl.BlockSpec((1, HK), lambda h, j: (0, 0))],
        out_shape=[jax.ShapeDtypeStruct((t, D), BF16)] * 4
        + [jax.ShapeDtypeStruct((2, D), F32), jax.ShapeDtypeStruct((1, HK), F32)],
        scratch_shapes=[pltpu.VMEM((hpb, HK, HK), F32), pltpu.VMEM((1, wide), F32)],
        compiler_params=_cp(),
    )(p_hg, p_hg, p_hg, p_hg, tab, gain, o_raw, states, dho)


def _rope_tables(pos):
    t = pos.shape[0]
    tm = min(ROW_TM, t)
    inv = np.zeros((1, HK), np.float32)
    freq = (ROPE_THETA ** (-np.arange(0, ROPE, 2, dtype=np.float32) / ROPE)).astype(np.float32)
    inv[0, 0:ROPE // 2] = freq
    inv[0, ROPE // 2:ROPE] = freq
    sign = np.zeros((1, HK), np.float32)
    sign[0, 0:ROPE // 2] = -1.0
    sign[0, ROPE // 2:ROPE] = 1.0

    def body(pos_ref, inv_ref, sign_ref, cos_ref, sin_ref):
        ang = pos_ref[...].astype(F32) * inv_ref[...]
        cos_ref[...] = jnp.cos(ang)
        sin_ref[...] = jnp.sin(ang) * sign_ref[...]

    one = pl.BlockSpec((1, HK), lambda i: (0, 0))
    row = pl.BlockSpec((tm, HK), lambda i: (i, 0))
    return _call(
        body, name="rope_tables", grid=(t // tm,),
        in_specs=[pl.BlockSpec((tm, 1), lambda i: (i, 0)), one, one],
        out_specs=[row, row],
        out_shape=[jax.ShapeDtypeStruct((t, HK), F32)] * 2,
        compiler_params=_cp(),
    )(pos, jnp.asarray(inv), jnp.asarray(sign))


def _rope(x, cos, sin_signed):
    r = lax.broadcasted_iota(jnp.int32, (HK, HK), 0)
    c = lax.broadcasted_iota(jnp.int32, (HK, HK), 1)
    half = ROPE // 2
    swap = jnp.logical_or(jnp.logical_and(c < half, r == c + half),
                          jnp.logical_and(jnp.logical_and(c >= half, c < ROPE), r == c - half))
    return x * cos + _dot_split(x, swap.astype(BF16)) * sin_signed


def _dot_split(x, m):
    hi = x.astype(BF16)
    lo = (x - hi.astype(F32)).astype(BF16)
    return _dot(hi, m) + _dot(lo, m)


def _lane_sum(x):
    return _dot_split(x, jnp.ones((HK, HK), BF16))


def _head_norm(xn, xr):
    r = lax.rsqrt(_lane_sum(xn * xn + xr * xr) * (1.0 / QK) + EPS)
    return xn * r, xr * r, r


def _head_norm_bwd(xn, xr, g_n, g_r, dn, dr):
    hn, hr, r = _head_norm(xn, xr)
    dxn, dxr = dn * g_n, dr * g_r
    c = _lane_sum(dxn * hn + dxr * hr) * (1.0 / QK)
    return r * (dxn - hn * c), r * (dxr - hr * c), dn * hn, dr * hr


def _mla_prep_fwd(qf, kv, p_mla, cos, sin, gq, gk):
    t = qf.shape[0]
    tm = min(PREP_TM, t)

    def body(qf_ref, kv_ref, kpe_ref, cos_ref, sin_ref, gq_ref, gk_ref, q_ref, k_ref, v_ref):
        cos_v, sin_v = cos_ref[...], sin_ref[...]
        kpe = kpe_ref[...]
        for h in range(HEADS):
            lo, mid, hi = h * QKP, h * QKP + HK, (h + 1) * QKP
            qn, qr, _ = _head_norm(qf_ref[:, lo:mid], qf_ref[:, mid:hi])
            q_ref[h, :, 0:HK] = (qn * gq_ref[:, 0:HK] * (SCALE * LOG2E)).astype(BF16)
            q_ref[h, :, HK:QKP] = (_rope(qr * gq_ref[:, HK:QKP], cos_v, sin_v) * (SCALE * LOG2E)).astype(BF16)
            kn, kr, _ = _head_norm(kv_ref[:, lo:mid], kpe)
            k_ref[h, :, 0:HK] = (kn * gk_ref[:, 0:HK]).astype(BF16)
            k_ref[h, :, HK:QKP] = _rope(kr * gk_ref[:, HK:QKP], cos_v, sin_v).astype(BF16)
            v_ref[h, :, 0:HK] = kv_ref[:, mid:hi].astype(BF16)
            v_ref[h, :, HK:QKP] = jnp.full((tm, HK), -1.0, BF16)

    head = pl.BlockSpec((tm, HEADS * QKP), lambda i: (i, 0))
    tok = pl.BlockSpec((tm, HK), lambda i: (i, 0))
    gain = pl.BlockSpec((1, QKP), lambda i: (0, 0))
    return _call(
        body, name="mla_prep_fwd", grid=(t // tm,),
        in_specs=[head, head, pl.BlockSpec((tm, HK), lambda i: (i, MLA_COLS // HK - 1)), tok, tok, gain, gain],
        out_specs=[pl.BlockSpec((HEADS, tm, QKP), lambda i: (0, i, 0)),
                   pl.BlockSpec((HEADS, tm, QKP), lambda i: (0, i, 0)),
                   pl.BlockSpec((HEADS, tm, QKP), lambda i: (0, i, 0))],
        out_shape=[jax.ShapeDtypeStruct((HEADS, t, QKP), BF16), jax.ShapeDtypeStruct((HEADS, t, QKP), BF16),
                   jax.ShapeDtypeStruct((HEADS, t, QKP), BF16)],
        compiler_params=_cp(),
    )(qf, kv, p_mla, cos, sin, gq, gk)


def _mla_prep_bwd(qf, kv, p_mla, cos, sin, gq, gk, dq, dk, dv):
    t = qf.shape[0]
    tm = min(PREP_TM, t)

    def body(qf_ref, kv_ref, kpe_ref, cos_ref, sin_ref, gq_ref, gk_ref, dq_ref, dk_ref, dv_ref,
             dqf_ref, dkv_ref, dkpe_ref, dgq_ref, dgk_ref):
        @pl.when(pl.program_id(0) == 0)
        def _():
            dgq_ref[...] = jnp.zeros_like(dgq_ref)
            dgk_ref[...] = jnp.zeros_like(dgk_ref)

        cos_v, sin_v = cos_ref[...], -sin_ref[...]
        kpe = kpe_ref[...]
        gqn, gqr, gkn, gkr = gq_ref[:, 0:HK], gq_ref[:, HK:QKP], gk_ref[:, 0:HK], gk_ref[:, HK:QKP]
        dkpe = jnp.zeros((tm, HK), F32)
        dgq_n, dgq_r, dgk_n, dgk_r = [jnp.zeros((1, HK), F32) for _ in range(4)]
        for h in range(HEADS):
            lo, mid, hi = h * QKP, h * QKP + HK, (h + 1) * QKP
            dqn = dq_ref[h, :, 0:HK].astype(F32) * SCALE
            dqr = _rope(dq_ref[h, :, HK:QKP].astype(F32), cos_v, sin_v) * SCALE
            a, b, ga, gb = _head_norm_bwd(qf_ref[:, lo:mid], qf_ref[:, mid:hi], gqn, gqr, dqn, dqr)
            dqf_ref[:, lo:mid] = a.astype(BF16)
            dqf_ref[:, mid:hi] = b.astype(BF16)
            dgq_n = dgq_n + jnp.sum(ga, axis=0, keepdims=True)
            dgq_r = dgq_r + jnp.sum(gb, axis=0, keepdims=True)
            dkn = dk_ref[h, :, 0:HK].astype(F32) * LN2
            dkr = _rope(dk_ref[h, :, HK:QKP].astype(F32), cos_v, sin_v) * LN2
            a, b, ga, gb = _head_norm_bwd(kv_ref[:, lo:mid], kpe, gkn, gkr, dkn, dkr)
            dkv_ref[:, lo:mid] = a.astype(BF16)
            dkv_ref[:, mid:hi] = dv_ref[h].astype(BF16)
            dkpe = dkpe + b
            dgk_n = dgk_n + jnp.sum(ga, axis=0, keepdims=True)
            dgk_r = dgk_r + jnp.sum(gb, axis=0, keepdims=True)
        dkpe_ref[...] = dkpe
        dgq_ref[:, 0:HK] += dgq_n
        dgq_ref[:, HK:QKP] += dgq_r
        dgk_ref[:, 0:HK] += dgk_n
        dgk_ref[:, HK:QKP] += dgk_r

    head = pl.BlockSpec((tm, HEADS * QKP), lambda i: (i, 0))
    tok = pl.BlockSpec((tm, HK), lambda i: (i, 0))
    gain = pl.BlockSpec((1, QKP), lambda i: (0, 0))
    hq = pl.BlockSpec((HEADS, tm, QKP), lambda i: (0, i, 0))
    return _call(
        body, name="mla_prep_bwd", grid=(t // tm,),
        in_specs=[head, head, pl.BlockSpec((tm, HK), lambda i: (i, MLA_COLS // HK - 1)), tok, tok, gain, gain,
                  hq, hq, pl.BlockSpec((HEADS, tm, HK), lambda i: (0, i, 0))],
        out_specs=[head, head, tok, gain, gain],
        out_shape=[jax.ShapeDtypeStruct((t, HEADS * QKP), BF16), jax.ShapeDtypeStruct((t, HEADS * QKP), BF16),
                   jax.ShapeDtypeStruct((t, HK), F32), jax.ShapeDtypeStruct((1, QKP), F32),
                   jax.ShapeDtypeStruct((1, QKP), F32)],
        compiler_params=_cp(),
    )(qf, kv, p_mla, cos, sin, gq, gk, dq, dk, dv)


def _chunk_mask(row0, rows, cols):
    r = lax.broadcasted_iota(jnp.int32, (rows, cols), 0) + row0
    c = lax.broadcasted_iota(jnp.int32, (rows, cols), 1)
    return jnp.right_shift(r, 6) >= jnp.right_shift(c, 6)


def _flash_fwd(q, k, v, side=None):
    t = q.shape[1]
    tq = min(TQ, t)
    nq = t // tq
    sub = min(SUBQ, tq)
    pairs = [(i, j) for i in range(nq) for j in range(i + 1)]
    qi = jnp.asarray([p[0] for p in pairs], jnp.int32)
    kj = jnp.asarray([p[1] for p in pairs], jnp.int32)
    s_in = len(side.inputs) if side else 0
    s_out = len(side.out_shapes) if side else 0

    def body(qi_ref, kj_ref, q_ref, k_ref, v_ref, *rest):
        o_ref, lse_ref = rest[s_in:s_in + 2]
        m_s, acc_s = rest[s_in + 2 + s_out:s_in + 4 + s_out]
        side_refs = list(rest[:s_in]) + list(rest[s_in + 2:s_in + 2 + s_out]) + list(rest[s_in + 4 + s_out:])
        n = pl.program_id(1)
        i, j = qi_ref[n], kj_ref[n]
        if side:
            @pl.when(jnp.logical_and(pl.program_id(0) == 0, n == 0))
            def _():
                side.start(*side_refs)

        @pl.when(j == 0)
        def _():
            m_s[...] = jnp.full_like(m_s, NEG)
            acc_s[...] = jnp.zeros_like(acc_s)

        def step(diag):
            subs = range(tq // sub)
            width = [(r + 1) * sub if diag else tq for r in subs]
            logits = [_dot_nt(q_ref[r * sub:(r + 1) * sub, :], k_ref[0:width[r], :]) for r in subs]
            for r in subs:
                rows = slice(r * sub, (r + 1) * sub)
                cols = width[r]
                s = logits[r]
                if diag:
                    s = jnp.where(_chunk_mask(r * sub, sub, cols), s, NEG)
                m_old = m_s[rows, :]
                m_new = jnp.maximum(m_old, jnp.max(s, axis=-1, keepdims=True))
                alpha = jnp.exp2(m_old - m_new)
                p = jnp.exp2((s - jnp.tile(m_new, (1, cols // HK))).astype(BF16))
                acc_s[rows, :] = jnp.tile(alpha, (1, 2)) * acc_s[rows, :] + _dot(p, v_ref[0:cols, :])
                m_s[rows, :] = m_new

        @pl.when(j < i)
        def _():
            step(False)

        @pl.when(j == i)
        def _():
            step(True)
            l = -acc_s[:, HK:QKP]
            o_ref[...] = (acc_s[:, 0:HK] / l).astype(BF16)
            lse_ref[...] = m_s[...] + jnp.log(l) * LOG2E

        if side:
            @pl.when(jnp.logical_and(pl.program_id(0) == HEADS - 1, n == len(pairs) - 1))
            def _():
                side.finish(*side_refs)

    anywhere = pl.BlockSpec(memory_space=pl.ANY)
    grid_spec = pltpu.PrefetchScalarGridSpec(
        num_scalar_prefetch=2, grid=(HEADS, len(pairs)),
        in_specs=[pl.BlockSpec((None, tq, QKP), lambda h, n, qi, kj: (h, qi[n], 0)),
                  pl.BlockSpec((None, tq, QKP), lambda h, n, qi, kj: (h, kj[n], 0)),
                  pl.BlockSpec((None, tq, QKP), lambda h, n, qi, kj: (h, kj[n], 0))] + [anywhere] * s_in,
        out_specs=[pl.BlockSpec((tq, HK), lambda h, n, qi, kj: (qi[n], h)),
                   pl.BlockSpec((None, tq, HK), lambda h, n, qi, kj: (h, qi[n], 0))] + [anywhere] * s_out,
        scratch_shapes=[pltpu.VMEM((tq, HK), F32), pltpu.VMEM((tq, QKP), F32)] + (list(side.scratch) if side else []),
    )
    return _call(
        body, name="flash_fwd", grid_spec=grid_spec,
        out_shape=[jax.ShapeDtypeStruct((t, D), BF16), jax.ShapeDtypeStruct((HEADS, t, HK), F32)]
        + (list(side.out_shapes) if side else []),
        compiler_params=_cp(),
    )(qi, kj, q, k, v, *(side.inputs if side else []))


def _attn_do(do, o):
    t = do.shape[0]
    tm = min(TM, t)

    def body(do_ref, o_ref, d_ref):
        lane = lax.broadcasted_iota(jnp.int32, (tm, HK), 1)
        for h in range(HEADS):
            ln = slice(h * HK, (h + 1) * HK)
            dov = do_ref[:, ln]
            d = jnp.sum(dov.astype(F32) * o_ref[:, ln].astype(F32), axis=-1, keepdims=True)
            hi = d.astype(BF16).astype(F32)
            d_ref[h, :, 0:HK] = dov
            d_ref[h, :, HK:QKP] = jnp.where(lane == 0, hi, jnp.where(lane == 1, d - hi, 0.0)).astype(BF16)

    blk = pl.BlockSpec((tm, D), lambda i: (i, 0))
    return _call(
        body, name="attn_do", grid=(t // tm,),
        in_specs=[blk, blk],
        out_specs=pl.BlockSpec((HEADS, tm, QKP), lambda i: (0, i, 0)),
        out_shape=jax.ShapeDtypeStruct((HEADS, t, QKP), BF16),
        compiler_params=_cp(),
    )(do, o)


def _flash_bwd(q, k, v, lse, do):
    t = q.shape[1]
    tq = min(TQ, t)
    nq = t // tq
    sub = min(SUBQ, tq)
    pairs = [(i, j) for j in range(nq) for i in range(j, nq)]
    qi = jnp.asarray([p[0] for p in pairs], jnp.int32)
    kj = jnp.asarray([p[1] for p in pairs], jnp.int32)
    npairs = len(pairs)

    def body(qi_ref, kj_ref, q_ref, k_ref, v_ref, lse_ref, do_ref, dq_ref, dk_ref, dv_ref):
        n = pl.program_id(1)
        i, j = qi_ref[n], kj_ref[n]

        @pl.when(n == 0)
        def _():
            dq_ref[...] = jnp.zeros_like(dq_ref)

        @pl.when(i == j)
        def _():
            dk_ref[...] = jnp.zeros_like(dk_ref)
            dv_ref[...] = jnp.zeros_like(dv_ref)

        def step(diag):
            for r in range(tq // sub):
                rows = slice(r * sub, (r + 1) * sub)
                cols = (r + 1) * sub if diag else tq
                qv, kv_ = q_ref[rows, :], k_ref[0:cols, :]
                p = jnp.exp2(_dot_nt(qv, kv_) - jnp.tile(lse_ref[rows, :], (1, cols // HK)))
                if diag:
                    p = jnp.where(_chunk_mask(r * sub, sub, cols), p, 0.0)
                dp_less_delta = _dot_nt(do_ref[rows, :], v_ref[0:cols, :])
                ds = (p * dp_less_delta).astype(BF16)
                dv_ref[0:cols, :] += _dot_tn(p.astype(BF16), do_ref[rows, 0:HK])
                dk_ref[0:cols, :] += _dot_tn(ds, qv)
                dq_rows = pl.ds(pl.multiple_of(i * tq + r * sub, sub), sub)
                dq_ref[dq_rows, :] += _dot(ds, kv_)

        @pl.when(j < i)
        def _():
            step(False)

        @pl.when(j == i)
        def _():
            step(True)

    grid_spec = pltpu.PrefetchScalarGridSpec(
        num_scalar_prefetch=2, grid=(HEADS, npairs),
        in_specs=[pl.BlockSpec((None, tq, QKP), lambda h, n, qi, kj: (h, qi[n], 0)),
                  pl.BlockSpec((None, tq, QKP), lambda h, n, qi, kj: (h, kj[n], 0)),
                  pl.BlockSpec((None, tq, QKP), lambda h, n, qi, kj: (h, kj[n], 0)),
                  pl.BlockSpec((None, tq, HK), lambda h, n, qi, kj: (h, qi[n], 0)),
                  pl.BlockSpec((None, tq, QKP), lambda h, n, qi, kj: (h, qi[n], 0))],
        out_specs=[pl.BlockSpec((None, t, QKP), lambda h, n, qi, kj: (h, 0, 0)),
                   pl.BlockSpec((None, tq, QKP), lambda h, n, qi, kj: (h, kj[n], 0)),
                   pl.BlockSpec((None, tq, HK), lambda h, n, qi, kj: (h, kj[n], 0))],
    )
    return _call(
        body, name="flash_bwd", grid_spec=grid_spec,
        out_shape=[jax.ShapeDtypeStruct((HEADS, t, QKP), F32), jax.ShapeDtypeStruct((HEADS, t, QKP), F32),
                   jax.ShapeDtypeStruct((HEADS, t, HK), F32)],
        compiler_params=_cp(56),
    )(qi, kj, q, k, v, lse, do)


def _adamw(name, w, g, m, v):
    r, c = w.shape
    tr = r if r <= 256 else next(k for k in (256, 352, 384) if r % k == 0)

    def body(w_ref, g_ref, m_ref, v_ref, d_ref, nm_ref, nv_ref):
        gv = g_ref[...]
        nm = ADAM_B1 * m_ref[...] + (1.0 - ADAM_B1) * gv
        nv = ADAM_B2 * v_ref[...] + (1.0 - ADAM_B2) * (gv * gv)
        m_hat = nm / (1.0 - ADAM_B1 ** ADAM_STEP)
        v_hat = nv / (1.0 - ADAM_B2 ** ADAM_STEP)
        d_ref[...] = -ADAM_LR * (m_hat / (jnp.sqrt(v_hat) + ADAM_EPS) + ADAM_WD * w_ref[...])
        nm_ref[...] = nm
        nv_ref[...] = nv

    blk = pl.BlockSpec((tr, c), lambda i: (i, 0))
    return _call(
        body, name=name, grid=(r // tr,),
        in_specs=[blk] * 4, out_specs=[blk] * 3,
        out_shape=[jax.ShapeDtypeStruct((r, c), F32)] * 3,
        compiler_params=_cp(),
    )(w, g, m, v)


def _place():
    return lax.axis_index("x"), lax.axis_index("y"), lax.axis_index("c")


def _other_chips(x, y):
    return [(1 - x, y), (x, 1 - y), (1 - x, 1 - y)]


class _Exchange:
    inputs = ()
    out_shapes = ()
    scratch = ()

    def start(self, *refs):
        raise NotImplementedError

    def finish(self, *refs):
        raise NotImplementedError

    def alone(self, name):
        def body(*refs):
            self.start(*refs)
            self.finish(*refs)

        anywhere = pl.BlockSpec(memory_space=pl.ANY)
        return _call(
            body, name=name,
            in_specs=[anywhere] * len(self.inputs), out_specs=[anywhere] * len(self.out_shapes),
            out_shape=list(self.out_shapes), scratch_shapes=list(self.scratch),
        )(*self.inputs)


class _GatherWeights(_Exchange):
    def __init__(self, shards):
        self.inputs = tuple(shards)
        self.out_shapes = tuple(jax.ShapeDtypeStruct((4,) + s.shape, s.dtype) for s in shards)
        self.scratch = (pltpu.SemaphoreType.DMA((6 * len(shards),)), pltpu.SemaphoreType.DMA((6 * len(shards),)))

    def gathered(self, got, k):
        return [lax.dynamic_update_slice(g, s[None], (k, 0, 0)) for g, s in zip(got, self.inputs)]

    def _copies(self, *refs):
        nbuf = len(self.inputs)
        send_sems, recv_sems = refs[2 * nbuf:]
        x, y, c = _place()
        chips = _other_chips(x, y)
        first, passed, landed, relayed = [], [], [], []
        for b, (s_ref, g_ref) in enumerate(zip(refs[:nbuf], refs[nbuf:2 * nbuf])):
            half = self.inputs[b].shape[0] // 2

            def rows(px, py, pc, g_ref=g_ref, half=half):
                return g_ref.at[2 * px + py, pl.ds(pc * half, half), :]

            def copy(k, block, to, src=None, rows=rows, b=b):
                return pltpu.make_async_remote_copy(
                    src_ref=rows(*block) if src is None else src, dst_ref=rows(*block),
                    send_sem=send_sems.at[6 * b + k], recv_sem=recv_sems.at[6 * b + k], device_id=to, device_id_type=MESH)

            mine = s_ref.at[pl.ds(c * half, half), :]
            first += [copy(j, (x, y, c), (*chip, c), src=mine) for j, chip in enumerate(chips)]
            passed += [copy(3 + j, (*chip, c), (x, y, 1 - c)) for j, chip in enumerate(chips)]
            landed += [copy(j, (*chip, c), (x, y, c)) for j, chip in enumerate(chips)]
            relayed += [copy(3 + j, (*chip, 1 - c), (x, y, c)) for j, chip in enumerate(chips)]
        return first, passed, landed, relayed

    def start(self, *refs):
        for cp in self._copies(*refs)[0]:
            cp.start()

    def finish(self, *refs):
        first, passed, landed, relayed = self._copies(*refs)
        for arrived, onward in zip(landed, passed):
            arrived.wait_recv()
            onward.start()
        for cp in relayed:
            cp.wait_recv()
        for cp in first + passed:
            cp.wait_send()


class _SwapHalves(_Exchange):
    def __init__(self, bufs):
        self.inputs = tuple(bufs)
        self.out_shapes = tuple(jax.ShapeDtypeStruct((4, g.shape[1] // 2, g.shape[2]), g.dtype) for g in bufs)
        self.scratch = (pltpu.SemaphoreType.DMA((len(bufs),)), pltpu.SemaphoreType.DMA((len(bufs),)))

    def _copies(self, *refs):
        nbuf = len(self.inputs)
        send_sems, recv_sems = refs[2 * nbuf:]
        x, y, c = _place()
        cps = []
        for b, (g_ref, o_ref) in enumerate(zip(refs[:nbuf], refs[nbuf:2 * nbuf])):
            half = self.inputs[b].shape[1] // 2
            cps.append(pltpu.make_async_remote_copy(
                src_ref=g_ref.at[:, pl.ds((1 - c) * half, half), :], dst_ref=o_ref,
                send_sem=send_sems.at[b], recv_sem=recv_sems.at[b], device_id=(x, y, 1 - c), device_id_type=MESH))
        return cps

    def start(self, *refs):
        for cp in self._copies(*refs):
            cp.start()

    def finish(self, *refs):
        for cp in self._copies(*refs):
            cp.wait()


def _add_rows(half):
    return next(tr for tr in range(512, 15, -16) if half % tr == 0)


def _chip_sum(name, gp, got, c_arr):
    half, width = got.shape[1], got.shape[2]
    tr = _add_rows(half)
    nb = half // tr

    def body(c_ref, a_ref, b_ref, o_ref, ob_ref):
        s = a_ref[...] + b_ref[...]
        o_ref[...] = s
        ob_ref[...] = s.astype(BF16)

    grid_spec = pltpu.PrefetchScalarGridSpec(
        num_scalar_prefetch=1, grid=(4, nb),
        in_specs=[pl.BlockSpec((None, tr, width), lambda s, i, c: (s, c[0] * nb + i, 0)),
                  pl.BlockSpec((None, tr, width), lambda s, i, c: (s, i, 0))],
        out_specs=[pl.BlockSpec((None, tr, width), lambda s, i, c: (s, i, 0)),
                   pl.BlockSpec((None, tr, width), lambda s, i, c: (s, i, 0))],
    )
    return _call(
        body, name=name, grid_spec=grid_spec,
        out_shape=[jax.ShapeDtypeStruct(got.shape, F32), jax.ShapeDtypeStruct(got.shape, BF16)],
        compiler_params=_cp(),
    )(c_arr, gp, got)


class _ScatterChipSums(_Exchange):
    def __init__(self, sums):
        self.inputs = tuple(sums)
        self.out_shapes = tuple(jax.ShapeDtypeStruct((3,) + cs.shape[1:], cs.dtype) for cs in sums)
        self.scratch = (pltpu.SemaphoreType.DMA((3 * len(sums),)), pltpu.SemaphoreType.DMA((3 * len(sums),)))

    def _copies(self, *refs):
        nbuf = len(self.inputs)
        send_sems, recv_sems = refs[2 * nbuf:]
        x, y, c = _place()
        return [pltpu.make_async_remote_copy(
            src_ref=s_ref.at[2 * px + py], dst_ref=o_ref.at[j],
            send_sem=send_sems.at[3 * b + j], recv_sem=recv_sems.at[3 * b + j], device_id=(px, py, c), device_id_type=MESH)
            for b, (s_ref, o_ref) in enumerate(zip(refs[:nbuf], refs[nbuf:2 * nbuf]))
            for j, (px, py) in enumerate(_other_chips(x, y))]

    def start(self, *refs):
        for cp in self._copies(*refs):
            cp.start()

    def finish(self, *refs):
        for cp in self._copies(*refs):
            cp.wait()


def _shard_sum(name, cs, got, kc_arr):
    h, width = cs.shape[1], cs.shape[2]
    tr = _add_rows(h)
    nb = h // tr

    def body(k_ref, a_ref, b_ref, o_ref):
        o_ref[...] = ((a_ref[...] + b_ref[0].astype(F32)) + b_ref[1].astype(F32)) + b_ref[2].astype(F32)

    grid_spec = pltpu.PrefetchScalarGridSpec(
        num_scalar_prefetch=1, grid=(nb,),
        in_specs=[pl.BlockSpec((None, tr, width), lambda i, k: (k[0], i, 0)),
                  pl.BlockSpec((3, tr, width), lambda i, k: (0, i, 0))],
        out_specs=pl.BlockSpec((tr, width), lambda i, k: (k[1] * nb + i, 0)),
    )
    return _call(
        body, name=name, grid_spec=grid_spec,
        out_shape=jax.ShapeDtypeStruct((2 * h, width), F32),
        compiler_params=_cp(),
    )(kc_arr, cs, got)


def _join_halves(name, boths):
    nbuf = len(boths)

    def body(*refs):
        send_sems, recv_sems = refs[2 * nbuf:]
        x, y, c = _place()
        sent, landing = [], []
        for b, (m_ref, o_ref) in enumerate(zip(refs[:nbuf], refs[nbuf:2 * nbuf])):
            h = boths[b].shape[0] // 2
            mine = m_ref.at[pl.ds(c * h, h), :]
            sent.append(pltpu.make_async_remote_copy(
                src_ref=mine, dst_ref=o_ref.at[pl.ds(c * h, h), :],
                send_sem=send_sems.at[b], recv_sem=recv_sems.at[b], device_id=(x, y, 1 - c), device_id_type=MESH))
            landing.append(pltpu.make_async_remote_copy(
                src_ref=mine, dst_ref=o_ref.at[pl.ds((1 - c) * h, h), :],
                send_sem=send_sems.at[b], recv_sem=recv_sems.at[b], device_id=(x, y, 1 - c), device_id_type=MESH))
        for cp in sent:
            cp.start()
        for cp in sent:
            cp.wait_send()
        for cp in landing:
            cp.wait_recv()

    anywhere = pl.BlockSpec(memory_space=pl.ANY)
    return _call(
        body, name=name,
        in_specs=[anywhere] * nbuf, out_specs=[anywhere] * nbuf,
        out_shape=[jax.ShapeDtypeStruct(g.shape, g.dtype) for g in boths],
        input_output_aliases={b: b for b in range(nbuf)},
        scratch_shapes=[pltpu.SemaphoreType.DMA((nbuf,)), pltpu.SemaphoreType.DMA((nbuf,))],
    )(*boths)


def _all_reduce_small(v):
    r = v.shape[0]

    def body(v_ref, o_ref, buf, send_sems, recv_sems):
        x, y, c = _place()
        me = 4 * x + 2 * y + c
        buf[me] = v_ref[...]
        cps = []
        for k in range(1, 8):
            peer = (x ^ (k >> 2), y ^ ((k >> 1) & 1), c ^ (k & 1))
            cps.append(pltpu.make_async_remote_copy(
                src_ref=v_ref, dst_ref=buf.at[me],
                send_sem=send_sems.at[k - 1], recv_sem=recv_sems.at[k - 1], device_id=peer, device_id_type=MESH))
        for cp in cps:
            cp.start()
        for k in range(1, 8):
            pltpu.make_async_remote_copy(
                src_ref=v_ref, dst_ref=buf.at[me ^ k],
                send_sem=send_sems.at[k - 1], recv_sem=recv_sems.at[k - 1],
                device_id=(x, y, c), device_id_type=MESH).wait_recv()
        for cp in cps:
            cp.wait_send()
        acc = buf[0]
        for k in range(1, 8):
            acc = acc + buf[k]
        o_ref[...] = acc

    return _call(
        body, name="all_reduce_small",
        in_specs=[pl.BlockSpec(memory_space=pltpu.VMEM)],
        out_specs=pl.BlockSpec(memory_space=pltpu.VMEM),
        out_shape=jax.ShapeDtypeStruct((r, 128), F32),
        scratch_shapes=[pltpu.VMEM((8, r, 128), F32), pltpu.SemaphoreType.DMA((7,)), pltpu.SemaphoreType.DMA((7,))],
    )(v)


def _group(names):
    return tuple(e for e in BIG if e[0] in names)


def _pack(shards, dtype):
    return jnp.concatenate([s.astype(dtype).reshape(-1, PACK_W) for s in shards], axis=0)


def _unpack_full(g, group):
    out, at = {}, 0
    for name, rows, cols, axis in group:
        n = rows * cols // 4 // PACK_W
        blk = g[:, at:at + n, :]
        at += n
        if axis == 1:
            out[name] = blk.reshape(4, rows, cols // 4).transpose(1, 0, 2).reshape(rows, cols)
        else:
            out[name] = blk.reshape(rows, cols)
    return out


def _pack_grads(grads, group):
    parts = []
    for name, rows, cols, axis in group:
        g = grads[name]
        if axis == 1:
            g = g.reshape(rows, 4, cols // 4).transpose(1, 0, 2)
        parts.append(g.reshape(4, -1, PACK_W))
    rows_total = sum(p.shape[1] for p in parts)
    pad = -rows_total % PACK_ALIGN
    if pad:
        parts.append(jnp.zeros((4, pad, PACK_W), F32))
    return jnp.concatenate(parts, axis=1)


def _unpack_shard(s, group):
    out, at = {}, 0
    for name, rows, cols, axis in group:
        n = rows * cols // 4 // PACK_W
        shape = (rows, cols // 4) if axis == 1 else (rows // 4, cols)
        out[name] = s[at:at + n, :].reshape(shape)
        at += n
    return out


def _pack_small(parts):
    flat = jnp.concatenate([p.reshape(-1) for p in parts])
    pad = -flat.shape[0] % 1024
    return jnp.concatenate([flat, jnp.zeros((pad,), F32)]).reshape(-1, 128)


def _ffn_in(tag, h, gain, w_in, side=None):
    t = h.shape[0]
    wide = DFF // 2

    def compute_in(rows, weights, outs):
        hv, w_ref = rows[0][...], weights[0]
        r = lax.rsqrt(jnp.mean(hv * hv, axis=-1, keepdims=True) + EPS)
        a = (hv * r * weights[1][...]).astype(BF16)
        outs[0][...] = a

        def emit(gate, up, cols):
            outs[1][:, cols] = gate.astype(BF16)
            outs[2][:, cols] = up.astype(BF16)
            outs[3][:, cols] = (_silu(gate) * up).astype(BF16)

        for s in range(2):
            emit(_dot(a, w_ref[s, :, 0:FFN_MAIN]), _dot(a, w_ref[2 + s, :, 0:FFN_MAIN]),
                 slice(s * wide, s * wide + FFN_MAIN))
        gate = _dot(a, jnp.concatenate([w_ref[0, :, FFN_MAIN:wide], w_ref[1, :, FFN_MAIN:wide]], axis=1))
        up = _dot(a, jnp.concatenate([w_ref[2, :, FFN_MAIN:wide], w_ref[3, :, FFN_MAIN:wide]], axis=1))
        rest = wide - FFN_MAIN
        for s in range(2):
            emit(gate[:, s * rest:(s + 1) * rest], up[:, s * rest:(s + 1) * rest],
                 slice(s * wide + FFN_MAIN, (s + 1) * wide))

    return _rows_call(tag + "_in", [h], [w_in, gain], [(D, BF16)] + [(DFF, BF16)] * 3, compute_in, min(FFN_TM, t),
                      side=side)


def _ffn_out(tag, act, h, w_out, next_gain, target=None):
    t = h.shape[0]
    tm = min(FFN_TM, t)

    def compute_out(rows, weights, outs):
        hn = rows[1][...] + 0.5 * _dot(rows[0][...], weights[0][...])
        g = weights[1][...]
        r = lax.rsqrt(jnp.mean(hn * hn, axis=-1, keepdims=True) + EPS)
        xh = hn * r
        if target is None:
            outs[0][...] = hn
            outs[1][...] = (xh * g).astype(BF16)
        else:
            err = xh * g - rows[2][...]
            dy = err * (1.0 / D)
            dxh = dy * g
            outs[0][...] = r * (dxh - xh * jnp.mean(dxh * xh, axis=-1, keepdims=True))
            outs[1][...] += jnp.sum(dy * xh, axis=0, keepdims=True)
            outs[2][...] += 0.5 * jnp.sum(jnp.mean(err * err, axis=-1, keepdims=True), axis=0, keepdims=True)

    if target is None:
        return _rows_call(tag + "_out", [act, h], [w_out, next_gain], [(D, F32), (D, BF16)], compute_out, tm)
    return _rows_call(tag + "_out", [act, h, target], [w_out, next_gain], [(D, F32)], compute_out, tm, sums=(D, 128))


class _Reduction:
    def __init__(self, tag, c_arr, k_arr):
        self.tag, self.c_arr, self.k_arr = tag, c_arr, k_arr

    def begin(self, bufs, swapped=None):
        if swapped is None:
            swapped = _SwapHalves(bufs).alone("grad_swap_" + self.tag)
        sums = [_chip_sum("grad_chip_sum_%s%d" % (self.tag, b), gp, got, self.c_arr)
                for b, (gp, got) in enumerate(zip(bufs, swapped))]
        self.sums = [s[0] for s in sums]
        return _ScatterChipSums([s[1] for s in sums])

    def end(self, got):
        mine = [_shard_sum("grad_shard_sum_%s%d" % (self.tag, b), cs, g, self.k_arr)
                for b, (cs, g) in enumerate(zip(self.sums, got))]
        return _join_halves("grad_join_" + self.tag, mine)


def _ffn_bwd(tag, h, gain, w_in, w_out, saved, dout, side, reduction):
    t = h.shape[0]
    tm = min(TM, t)
    n, gate, up, act = saved

    def compute(rows, weights, outs):
        d = rows[0][...].astype(BF16)
        for j in range(DFF // FFN_CHUNK):
            cols = slice(j * FFN_CHUNK, (j + 1) * FFN_CHUNK)
            da = 0.5 * _dot_nt(d, weights[0][cols, :])
            g, u = rows[1][:, cols].astype(F32), rows[2][:, cols].astype(F32)
            s = _sig(g)
            silu = g * s
            outs[0][:, cols] = (da * u * (s + silu * (1.0 - s))).astype(BF16)
            outs[1][:, cols] = (da * silu).astype(BF16)

    dgate, dup, *side_out = _rows_call(tag + "_dact", [dout, gate, up], [w_out], [(DFF, BF16)] * 2, compute,
                                       min(FFN_TM, t), side=side)
    dw_in = _mm_tn(tag + "_dw_gate", n, dgate, tm=D, tn=DFF // 2, stacked=(4, 0))
    dw_in = _mm_tn(tag + "_dw_up", n, dup, tm=D, tn=DFF // 2, stacked=(4, 2), into=dw_in)
    dw_out, dw_in_swapped = _mm_tn(tag + "_dw_out", act, dout, scale=0.5, tm=DFF // 2, tn=D, side=_SwapHalves([dw_in]))
    dw_out = dw_out.reshape(4, DFF // 4, D)
    (dw_out_swapped,) = _SwapHalves([dw_out]).alone("grad_swap_" + tag)
    sending = reduction.begin([dw_in, dw_out], [dw_in_swapped, dw_out_swapped])

    def compute_dn(rows, weights, outs):
        w_ref = weights[0]
        wide = DFF // 2
        dn = jnp.zeros((rows[0].shape[0], D), F32)
        for s in range(2):
            cols = slice(s * wide, s * wide + FFN_MAIN)
            dn = (dn + _dot_nt(rows[0][:, cols], w_ref[s, :, 0:FFN_MAIN])
                  + _dot_nt(rows[1][:, cols], w_ref[2 + s, :, 0:FFN_MAIN]))
        for r, first in ((0, 0), (1, 2)):
            x = jnp.concatenate([rows[r][:, FFN_MAIN:wide], rows[r][:, wide + FFN_MAIN:2 * wide]], axis=1)
            wt = jnp.concatenate([w_ref[first, :, FFN_MAIN:wide], w_ref[first + 1, :, FFN_MAIN:wide]], axis=1)
            dn = dn + _dot_nt(x, wt)
        dx, dg = _rms_bwd_vals(rows[2][...], weights[1][...], dn)
        outs[0][...] = rows[3][...] + dx
        outs[1][...] += jnp.sum(dg, axis=0, keepdims=True)

    dh, dgain, *got = _rows_call(tag + "_dn", [dgate, dup, h, dout], [w_in, gain], [(D, F32)], compute_dn,
                                 min(FFN_TM, t), side=sending, sums=(D,), vmem_mb=58)
    return dh, dgain, side_out, got


def kernel(x, positions, ffn1_norm, ffn1_w_in, ffn1_w_out, mix_norm, w_in, hg_lb_table, hg_out_norm, w_hg_branch, mla_q_lora_norm, w_q_up, mla_kv_lora_norm, w_kv_up, q_head_norm, k_head_norm, w_mla_branch, w_merge, b_merge, w_out, ffn2_norm, ffn2_w_in, ffn2_w_out, final_norm, loss_target, m_ffn1_norm, m_ffn1_w_in, m_ffn1_w_out, m_mix_norm, m_w_in, m_hg_lb_table, m_hg_out_norm, m_w_hg_branch, m_mla_q_lora_norm, m_w_q_up, m_mla_kv_lora_norm, m_w_kv_up, m_q_head_norm, m_k_head_norm, m_w_mla_branch, m_w_merge, m_b_merge, m_w_out, m_ffn2_norm, m_ffn2_w_in, m_ffn2_w_out, m_final_norm, v_ffn1_norm, v_ffn1_w_in, v_ffn1_w_out, v_mix_norm, v_w_in, v_hg_lb_table, v_hg_out_norm, v_w_hg_branch, v_mla_q_lora_norm, v_w_q_up, v_mla_kv_lora_norm, v_w_kv_up, v_q_head_norm, v_k_head_norm, v_w_mla_branch, v_w_merge, v_b_merge, v_w_out, v_ffn2_norm, v_ffn2_w_in, v_ffn2_w_out, v_final_norm):
    a = dict(locals())
    w = {n: a[n] for n in WEIGHT_ORDER}
    mom = {n: a["m_" + n] for n in WEIGHT_ORDER}
    var = {n: a["v_" + n] for n in WEIGHT_ORDER}
    t = x.shape[1]
    tm = min(TM, t)
    xt = x.reshape(t, D)
    target = loss_target.reshape(t, D)
    pos = positions.reshape(t, 1)
    x_i, y_i, c_i = _place()
    k_idx = (2 * x_i + y_i).astype(jnp.int32)
    c_arr = c_i.astype(jnp.int32).reshape(1)
    k_arr = jnp.stack([k_idx, c_i.astype(jnp.int32)])

    group_mid = _group(("w_in", "w_hg_branch", "w_q_up", "w_kv_up", "w_mla_branch", "w_merge", "w_out"))
    use_early = _group(("ffn1_w_out", "w_in", "w_hg_branch", "w_q_up", "w_kv_up"))
    use_late = _group(("w_mla_branch", "w_merge", "w_out", "ffn2_w_out"))
    gather_first = _GatherWeights([w["ffn1_w_in"][0].astype(BF16)])
    gather_early = _GatherWeights([_pack([w[e[0]][0] for e in use_early], BF16)])
    gather_late = _GatherWeights([_pack([w[e[0]][0] for e in use_late], BF16), w["ffn2_w_in"][0].astype(BF16)])
    (ffn1_w_in_g,) = gather_first.gathered(gather_first.alone("gather_first"), k_idx)
    n1, gate1, up1, act1, got = _ffn_in("ffn1", xt, w["ffn1_norm"], ffn1_w_in_g, gather_early)
    full = _unpack_full(gather_early.gathered([got], k_idx)[0], use_early)
    h1, u = _ffn_out("ffn1", act1, xt, full["ffn1_w_out"], w["mix_norm"])
    ffn1_saved = (n1, gate1, up1, act1)
    w_in_full = full["w_in"]
    w_in_hg = w_in_full[:, :4 * D]
    w_in_mla = jnp.pad(w_in_full[:, 4 * D:], ((0, 0), (0, MLA_COLS - (4800 - 4 * D))))
    w_q_pad = jnp.pad(full["w_q_up"].reshape(Q_LORA, HEADS, QK), ((0, 0), (0, 0), (0, QKP - QK))).reshape(Q_LORA, HEADS * QKP)
    w_kv = full["w_kv_up"]
    gq = jnp.pad(w["q_head_norm"], ((0, 0), (0, QKP - QK)))
    gk = jnp.pad(w["k_head_norm"], ((0, 0), (0, QKP - QK)))

    ident = lambda accs, ex: (accs[0],)
    def in_hg(rows, weights, outs):
        a = rows[0][...]
        for j in range(4 * D // 512):
            cols = slice(j * 512, (j + 1) * 512)
            outs[0][:, cols] = _dot(a, weights[0][:, cols])

    (p_hg,) = _rows_call("in_hg", [u], [w_in_hg], [(4 * D, F32)], in_hg, min(FFN_TM, t))
    p_mla, cqn, ckvn = _in_mla(u, w_in_mla, w["mla_q_lora_norm"], w["mla_kv_lora_norm"])
    o_raw, hg_o, states = _hgrn_fwd(p_hg, w["hg_lb_table"], w["hg_out_norm"])
    (y_hg,) = _mm("hg_branch", [_a_spec(hg_o, tm)], [_b_nn(full["w_hg_branch"], 512)], [(0, 0)], ident, [], [BF16], t, D, tm, 512)
    (qf,) = _mm("q_up", [_a_spec(cqn, tm)], [_b_nn(w_q_pad, 512)], [(0, 0)], ident, [], [F32], t, HEADS * QKP, tm, 512)
    (kvf,) = _mm("kv_up", [_a_spec(ckvn, tm)], [_b_nn(w_kv, 512)], [(0, 0)], ident, [], [F32], t, HEADS * QKP, tm, 512)
    cos, sin = _rope_tables(pos)
    qh, kh, vh = _mla_prep_fwd(qf, kvf, p_mla, cos, sin, gq, gk)
    o_mla, lse, *got = _flash_fwd(qh, kh, vh, side=gather_late)
    late, ffn2_w_in_g = gather_late.gathered(got, k_idx)
    full.update(_unpack_full(late, use_late))
    (y_mla,) = _mm("mla_branch", [_a_spec(o_mla, tm)], [_b_nn(full["w_mla_branch"], 512)], [(0, 0)], ident, [], [BF16], t, D, tm, 512)

    def merge_epi(accs, ex):
        g_hg = _sig(accs[0] + ex[2])
        g_mla = _sig(accs[1] + ex[3])
        return g_hg * ex[0].astype(F32) + g_mla * ex[1].astype(F32), g_hg, g_mla

    w_merge_f = full["w_merge"]
    mix, g_hg, g_mla = _mm(
        "merge", [_a_spec(u, tm)], [_b_nn(w_merge_f, 512), _b_nn(w_merge_f, 512, D // 512)], [(0, 0), (0, 1)], merge_epi,
        [_e_tile(y_hg, tm, 512), _e_tile(y_mla, tm, 512), _e_row(w["b_merge"], 512), _e_row(w["b_merge"], 512, D // 512)],
        [BF16, BF16, BF16], t, D, tm, 512)
    (h2,) = _mm("out_proj", [_a_spec(mix, tm)], [_b_nn(full["w_out"], 512)], [(0, 0)],
                lambda accs, ex: (ex[0] + accs[0],), [_e_tile(h1, tm, 512)], [F32], t, D, tm, 512)
    ffn2_saved = _ffn_in("ffn2", h2, w["ffn2_norm"], ffn2_w_in_g)
    dh3, d_final_norm, loss_part = _ffn_out("ffn2", ffn2_saved[3], h2, full["ffn2_w_out"], w["final_norm"], target=target)

    grads, small = {}, {}
    small["final_norm"] = d_final_norm
    reduce_last = _Reduction("last", c_arr, k_arr)
    reduce_mid = _Reduction("mid", c_arr, k_arr)
    reduce_first = _Reduction("first", c_arr, k_arr)
    dh2, small["ffn2_norm"], _, got_last = _ffn_bwd(
        "ffn2", h2, w["ffn2_norm"], ffn2_w_in_g, full["ffn2_w_out"], ffn2_saved, dh3, None, reduce_last)

    def dmix_epi(accs, ex):
        dm = accs[0]
        ghg, gml, yhg, yml = [e.astype(F32) for e in ex]
        return dm * ghg, dm * gml, dm * yhg * ghg * (1.0 - ghg), dm * yml * gml * (1.0 - gml)

    dy_hg, dy_mla, dpre_hg, dpre_mla = _mm(
        "d_mix", [_a_spec(dh2, tm)], [_b_nt(full["w_out"], 512)], [(0, 0)], dmix_epi,
        [_e_tile(g_hg, tm, 512), _e_tile(g_mla, tm, 512), _e_tile(y_hg, tm, 512), _e_tile(y_mla, tm, 512)],
        [BF16, BF16, BF16, BF16], t, D, tm, 512, trans_b=True)
    grads["w_out"] = _mm_tn("dw_out", mix, dh2)
    small["b_merge"] = jnp.concatenate([_colsum("db_hg", dpre_hg), _colsum("db_mla", dpre_mla)], axis=1)
    grads["w_merge"] = jnp.concatenate([_mm_tn("dw_merge_hg", u, dpre_hg), _mm_tn("dw_merge_mla", u, dpre_mla)], axis=1)
    grads["w_hg_branch"] = _mm_tn("dw_hg_branch", hg_o, dy_hg)
    grads["w_mla_branch"] = _mm_tn("dw_mla_branch", o_mla, dy_mla)
    (dho,) = _mm("d_hg_o", [_a_spec(dy_hg, tm)], [_b_nt(full["w_hg_branch"], 512)], [(0, 0)], ident, [], [BF16], t, D, tm, 512, trans_b=True)
    (do_mla,) = _mm("d_o_mla", [_a_spec(dy_mla, tm)], [_b_nt(full["w_mla_branch"], 512)], [(0, 0)], ident, [], [BF16], t, D, tm, 512, trans_b=True)

    dq_raw, df_raw, di_raw, dg_raw, small["hg_lb_table"], small["hg_out_norm"] = _hgrn_bwd(
        p_hg, w["hg_lb_table"], w["hg_out_norm"], o_raw, states, dho)
    dp_hg = [dq_raw, df_raw, di_raw, dg_raw]

    dqh, dkh, dvh = _flash_bwd(qh, kh, vh, lse, _attn_do(do_mla, o_mla))
    dqf, dkvf, dkpe, dgq, dgk = _mla_prep_bwd(qf, kvf, p_mla, cos, sin, gq, gk, dqh, dkh, dvh)
    small["q_head_norm"] = dgq[:, :QK]
    small["k_head_norm"] = dgk[:, :QK]
    dwq_pad = _mm_tn("dw_q_up", cqn, dqf, tm=Q_LORA, tn=1024)
    grads["w_q_up"] = dwq_pad.reshape(Q_LORA, HEADS, QKP)[:, :, :QK].reshape(Q_LORA, HEADS * QK)
    grads["w_kv_up"] = _mm_tn("dw_kv_up", ckvn, dkvf, tm=KV_LORA, tn=1024)
    (dcqn,) = _mm("d_cq", [_a_spec(dqf, tm)], [_b_nt(w_q_pad, Q_LORA)], [(0, 0)], ident, [], [F32], t, Q_LORA, tm, Q_LORA, trans_b=True)
    (dckvn,) = _mm("d_ckv", [_a_spec(dkvf, tm)], [_b_nt(w_kv, KV_LORA)], [(0, 0)], ident, [], [F32], t, KV_LORA, tm, KV_LORA, trans_b=True)
    dp_mla, small["mla_q_lora_norm"], small["mla_kv_lora_norm"] = _lora_norm_bwd(
        p_mla, w["mla_q_lora_norm"], w["mla_kv_lora_norm"], dcqn, dckvn, dkpe)

    dw_in_hg = [_mm_tn("dw_in_hg%d" % k, u, dp_hg[k]) for k in range(4)]
    dw_in_mla = _mm_tn("dw_in_mla", u, dp_mla, tn=MLA_COLS)
    grads["w_in"] = jnp.concatenate(dw_in_hg + [dw_in_mla[:, :4800 - 4 * D]], axis=1)
    tm_du = min(TM // 2, t)
    du, *got_mid = _mm(
        "d_u",
        [_a_spec(dpre_hg, tm_du), _a_spec(dpre_mla, tm_du)] + [_a_spec(d, tm_du) for d in dp_hg] + [_a_spec(dp_mla, tm_du)],
        [_b_nt(w_merge_f, 512, D, 0), _b_nt(w_merge_f, 512, D, 1)]
        + [_b_nt(w_in_hg, 512, D, k) for k in range(4)] + [_b_nt(w_in_mla, 512)],
        [(k, k) for k in range(7)],
        lambda accs, ex: (functools.reduce(lambda p, q: p + q, accs),), [], [F32], t, D, tm_du, 512, trans_b=True,
        side=reduce_mid.begin([_pack_grads(grads, group_mid)]))
    dh1, small["mix_norm"] = _rms_bwd("mix_dnorm", h1, w["mix_norm"], du, dh2)
    dx, small["ffn1_norm"], _, got_first = _ffn_bwd(
        "ffn1", xt, w["ffn1_norm"], ffn1_w_in_g, full["ffn1_w_out"], ffn1_saved, dh1, None, reduce_first)

    g_shard = _unpack_shard(reduce_mid.end(got_mid)[0], group_mid)
    g_shard["ffn2_w_in"], g_shard["ffn2_w_out"] = reduce_last.end(got_last)
    g_shard["ffn1_w_in"], g_shard["ffn1_w_out"] = reduce_first.end(got_first)
    small_sum = _all_reduce_small(_pack_small([small[n] for n, _ in SMALL] + [loss_part])).reshape(-1)
    g_small, at = {}, 0
    for n, shape in SMALL:
        size = shape[0] * shape[1]
        g_small[n] = small_sum[at:at + size].reshape(shape)
        at += size
    loss = small_sum[at]

    g_out, d_out, m_out, v_out = {}, {}, {}, {}
    for n in WEIGHT_ORDER:
        shape = w[n].shape
        g = g_shard[n] if n in g_shard else g_small[n]
        two = g.shape
        d_, m_, v_ = _adamw("adamw_" + n, w[n].reshape(two), g, mom[n].reshape(two), var[n].reshape(two))
        g_out[n], d_out[n], m_out[n], v_out[n] = g.reshape(shape), d_.reshape(shape), m_.reshape(shape), v_.reshape(shape)

    return (loss, dx.reshape(x.shape), *[g_out[n] for n in WEIGHT_ORDER], *[d_out[n] for n in WEIGHT_ORDER],
            *[m_out[n] for n in WEIGHT_ORDER], *[v_out[n] for n in WEIGHT_ORDER])
```

```python
import functools

import numpy as np
import jax
import jax.numpy as jnp
from jax import lax
from jax.experimental import pallas as pl
from jax.experimental.pallas import tpu as pltpu

F32 = jnp.float32
BF16 = jnp.bfloat16
MESH = pl.DeviceIdType.MESH

D = 1024
DFF = 2816
HEADS = 8
HK = 128
CHUNK = 64
ROPE = 64
QK = 192
QKP = 256
Q_LORA = 384
KV_LORA = 256
MLA_COLS = 768
EPS = 1e-6
ROPE_THETA = 10000.0
SCALE = QK ** -0.5
LOG2E = 1.4426950408889634
LN2 = 0.6931471805599453
NEG = -1e30
EXP_CLAMP = 80.0

ADAM_LR = 0.001
ADAM_B1 = 0.9
ADAM_B2 = 0.999
ADAM_EPS = 1e-08
ADAM_WD = 0.01
ADAM_STEP = 10

PACK_W = 1024
ADD_ROWS = 352
PACK_ALIGN = 2 * ADD_ROWS

TM = 1024
FFN_TM = 512
FFN_CHUNK = 256
FFN_MAIN = 1280
TQ = 2048
SUBQ = 256
HG_BT = 512
HG_HPB = 8
TT = 2048
ROW_TM = 512
PREP_TM = 256

VMEM_MB = 48

BIG = (
    ("ffn1_w_in", D, 2 * DFF, 1),
    ("ffn1_w_out", DFF, D, 0),
    ("w_in", D, 4800, 1),
    ("w_hg_branch", D, D, 0),
    ("w_q_up", Q_LORA, HEADS * QK, 1),
    ("w_kv_up", KV_LORA, HEADS * 2 * HK, 1),
    ("w_mla_branch", D, D, 0),
    ("w_merge", D, 2 * D, 1),
    ("w_out", D, D, 0),
    ("ffn2_w_in", D, 2 * DFF, 1),
    ("ffn2_w_out", DFF, D, 0),
)
SMALL = (
    ("ffn1_norm", (1, D)),
    ("mix_norm", (1, D)),
    ("hg_lb_table", (2, D)),
    ("hg_out_norm", (1, HK)),
    ("mla_q_lora_norm", (1, Q_LORA)),
    ("mla_kv_lora_norm", (1, KV_LORA)),
    ("q_head_norm", (1, QK)),
    ("k_head_norm", (1, QK)),
    ("b_merge", (1, 2 * D)),
    ("ffn2_norm", (1, D)),
    ("final_norm", (1, D)),
)
WEIGHT_ORDER = ("ffn1_norm", "ffn1_w_in", "ffn1_w_out", "mix_norm", "w_in", "hg_lb_table", "hg_out_norm",
                "w_hg_branch", "mla_q_lora_norm", "w_q_up", "mla_kv_lora_norm", "w_kv_up", "q_head_norm",
                "k_head_norm", "w_mla_branch", "w_merge", "b_merge", "w_out", "ffn2_norm", "ffn2_w_in",
                "ffn2_w_out", "final_norm")


def _call(body, **kw):
    return pl.pallas_call(body, **kw)


def _cp(vmem_mb=VMEM_MB):
    return pltpu.CompilerParams(vmem_limit_bytes=vmem_mb << 20)


def _dot(a, b):
    return lax.dot_general(a, b, (((1,), (0,)), ((), ())), preferred_element_type=F32)


def _dot_nt(a, b):
    return lax.dot_general(a, b, (((1,), (1,)), ((), ())), preferred_element_type=F32)


def _dot_tn(a, b):
    return lax.dot_general(a, b, (((0,), (0,)), ((), ())), preferred_element_type=F32)


def _sig(x):
    return jax.nn.sigmoid(x)


def _silu(x):
    return x * _sig(x)


def _dsilu(x):
    s = _sig(x)
    return s * (1.0 + x * (1.0 - s))


def _a_spec(arr, tm, kblk=None, kidx=0):
    kb = arr.shape[1] if kblk is None else kblk
    return arr, pl.BlockSpec((tm, kb), lambda i, j, kidx=kidx: (i, kidx)), slice(kidx * kb, (kidx + 1) * kb)


def _b_nn(arr, tn, off=0):
    return arr, pl.BlockSpec((arr.shape[0], tn), lambda i, j, off=off: (0, j + off)), ("cols", off)


def _b_nt(arr, tn, kblk=None, kidx=0):
    kb = arr.shape[1] if kblk is None else kblk
    return arr, pl.BlockSpec((tn, kb), lambda i, j, kidx=kidx: (j, kidx)), ("rows", slice(kidx * kb, (kidx + 1) * kb))


def _e_tile(arr, tm, tn, off=0):
    return arr, pl.BlockSpec((tm, tn), lambda i, j, off=off: (i, j + off)), ("tile", off)


def _e_row(arr, tn, off=0):
    return arr, pl.BlockSpec((1, tn), lambda i, j, off=off: (0, j + off)), ("row", off)


def _mm_resident(name, As, Bs, dots, epi, extras, out_dtypes, m, n, tn):
    def unique(arrays):
        seen = []
        for a in arrays:
            if not any(a is s for s in seen):
                seen.append(a)
        return seen

    rows = unique([a for a, _, _ in As] + [e for e, _, where in extras if where[0] == "tile"])
    weights = unique([b for b, _, _ in Bs] + [e for e, _, where in extras if where[0] == "row"])

    def ref_of(arr, row_refs, weight_refs):
        for r, ref in zip(rows, row_refs):
            if r is arr:
                return ref
        for wt, ref in zip(weights, weight_refs):
            if wt is arr:
                return ref

    def compute(row_refs, weight_refs, out_refs):
        a_vals = [ref_of(a, row_refs, weight_refs)[:, ks].astype(BF16) for a, _, ks in As]
        for j in range(n // tn):
            accs = []
            for ai, bi in dots:
                b, _, where = Bs[bi]
                b_ref = ref_of(b, row_refs, weight_refs)
                if where[0] == "cols":
                    accs.append(_dot(a_vals[ai], b_ref[:, (j + where[1]) * tn:(j + where[1] + 1) * tn]))
                else:
                    accs.append(_dot_nt(a_vals[ai], b_ref[j * tn:(j + 1) * tn, where[1]]))
            ex = [ref_of(e, row_refs, weight_refs)[:, (j + where[1]) * tn:(j + where[1] + 1) * tn]
                  for e, _, where in extras]
            for o_ref, o in zip(out_refs, epi(accs, ex)):
                o_ref[:, j * tn:(j + 1) * tn] = o.astype(o_ref.dtype)

    return _rows_call(name, rows, weights, [(n, dt) for dt in out_dtypes], compute, min(FFN_TM, m))


def _mm(name, As, Bs, dots, epi, extras, out_dtypes, m, n, tm, tn, trans_b=False, side=None):
    if side is None:
        return _mm_resident(name, As, Bs, dots, epi, extras, out_dtypes, m, n, tn)
    na, nb, ne, no = len(As), len(Bs), len(extras), len(out_dtypes)
    ni, nj = m // tm, n // tn
    s_in = len(side.inputs) if side else 0
    s_out = len(side.out_shapes) if side else 0

    def body(*refs):
        a_refs = refs[:na]
        b_refs = refs[na:na + nb]
        e_refs = refs[na + nb:na + nb + ne]
        at = na + nb + ne
        side_refs = refs[at:at + s_in]
        o_refs = refs[at + s_in:at + s_in + no]
        side_refs = list(side_refs) + list(refs[at + s_in + no:])
        if side:
            i, j = pl.program_id(0), pl.program_id(1)

            @pl.when(jnp.logical_and(i == 0, j == 0))
            def _():
                side.start(*side_refs)

        a_vals = [r[...].astype(BF16) for r in a_refs]
        accs = []
        for ai, bi in dots:
            b = b_refs[bi][...]
            accs.append(_dot_nt(a_vals[ai], b) if trans_b else _dot(a_vals[ai], b))
        outs = epi(accs, [r[...] for r in e_refs])
        for o_ref, o in zip(o_refs, outs):
            o_ref[...] = o.astype(o_ref.dtype)
        if side:
            @pl.when(jnp.logical_and(i == ni - 1, j == nj - 1))
            def _():
                side.finish(*side_refs)

    ops = list(As) + list(Bs) + list(extras)
    anywhere = pl.BlockSpec(memory_space=pl.ANY)
    res = _call(
        body, name=name,
        grid=(ni, nj),
        in_specs=[op[1] for op in ops] + [anywhere] * s_in,
        out_specs=[pl.BlockSpec((tm, tn), lambda i, j: (i, j)) for _ in out_dtypes] + [anywhere] * s_out,
        out_shape=[jax.ShapeDtypeStruct((m, n), dt) for dt in out_dtypes] + (list(side.out_shapes) if side else []),
        scratch_shapes=list(side.scratch) if side else [],
        compiler_params=_cp(),
    )(*[op[0] for op in ops], *(side.inputs if side else []))
    return res


def _rows_call(name, rows, weights, outs, compute, tm, side=None, sums=(), vmem_mb=VMEM_MB):
    t = rows[0].shape[0]
    nr, nw, no = len(rows), len(weights), len(outs) + len(sums)
    ni = t // tm
    s_in = len(side.inputs) if side else 0
    s_out = len(side.out_shapes) if side else 0

    def body(*refs):
        at = nr + nw
        side_refs = list(refs[at:at + s_in]) + list(refs[at + s_in + no:])
        if side:
            @pl.when(pl.program_id(0) == 0)
            def _():
                side.start(*side_refs)

        out_refs = refs[at + s_in:at + s_in + no]
        if sums:
            @pl.when(pl.program_id(0) == 0)
            def _():
                for r in out_refs[len(outs):]:
                    r[...] = jnp.zeros_like(r)

        compute(refs[:nr], refs[nr:at], out_refs)
        if side:
            @pl.when(pl.program_id(0) == ni - 1)
            def _():
                side.finish(*side_refs)

    anywhere = pl.BlockSpec(memory_space=pl.ANY)
    return _call(
        body, name=name, grid=(ni,),
        in_specs=[pl.BlockSpec((tm, r.shape[1]), lambda i: (i, 0)) for r in rows]
        + [pl.BlockSpec(wt.shape, lambda i, nd=wt.ndim: (0,) * nd) for wt in weights] + [anywhere] * s_in,
        out_specs=[pl.BlockSpec((tm, width), lambda i: (i, 0)) for width, _ in outs]
        + [pl.BlockSpec((1, width), lambda i: (0, 0)) for width in sums] + [anywhere] * s_out,
        out_shape=[jax.ShapeDtypeStruct((t, width), dt) for width, dt in outs]
        + [jax.ShapeDtypeStruct((1, width), F32) for width in sums] + (list(side.out_shapes) if side else []),
        scratch_shapes=list(side.scratch) if side else [],
        compiler_params=_cp(vmem_mb),
    )(*rows, *weights, *(side.inputs if side else []))


def _mm_tn(name, a, b, scale=1.0, tm=1024, tn=1024, stacked=None, into=None, side=None):
    t, m = a.shape
    n = b.shape[1]
    tm, tn, tt = min(tm, m), min(tn, n), min(TT, t)
    ni, nj, nk = m // tm, n // tn, t // tt
    extra_in = [into] if into is not None else list(side.inputs) if side else []
    s_out = len(side.out_shapes) if side else 0

    def body(a_ref, b_ref, *rest):
        o_ref = rest[len(extra_in)]
        i, j, k = pl.program_id(0), pl.program_id(1), pl.program_id(2)
        if side:
            side_refs = list(rest[:len(extra_in)]) + list(rest[len(extra_in) + 1:])

            @pl.when(jnp.logical_and(jnp.logical_and(i == 0, j == 0), k == 0))
            def _():
                side.start(*side_refs)

        @pl.when(k == 0)
        def _():
            o_ref[...] = jnp.zeros_like(o_ref)

        o_ref[...] += _dot_tn(a_ref[...].astype(BF16), b_ref[...].astype(BF16))
        if scale != 1.0:
            @pl.when(k == nk - 1)
            def _():
                o_ref[...] = o_ref[...] * scale
        if side:
            @pl.when(jnp.logical_and(jnp.logical_and(i == ni - 1, j == nj - 1), k == nk - 1))
            def _():
                side.finish(*side_refs)

    anywhere = pl.BlockSpec(memory_space=pl.ANY)
    product = jax.ShapeDtypeStruct((stacked[0], m, tn) if stacked else (m, n), F32)
    res = _call(
        body, name=name,
        grid=(ni, nj, nk),
        in_specs=[pl.BlockSpec((tt, tm), lambda i, j, k: (k, i)), pl.BlockSpec((tt, tn), lambda i, j, k: (k, j))]
        + [anywhere] * len(extra_in),
        out_specs=[pl.BlockSpec((None, tm, tn), lambda i, j, k: (stacked[1] + j, i, 0)) if stacked
                   else pl.BlockSpec((tm, tn), lambda i, j, k: (i, j))] + [anywhere] * s_out,
        out_shape=[product] + (list(side.out_shapes) if side else []),
        input_output_aliases={2: 0} if into is not None else {},
        scratch_shapes=list(side.scratch) if side else [],
        compiler_params=_cp(),
    )(a, b, *extra_in)
    return res if side else res[0]


def _rms_bwd_vals(xv, g, dn):
    r = lax.rsqrt(jnp.mean(xv * xv, axis=-1, keepdims=True) + EPS)
    xh = xv * r
    dxh = dn * g
    c = jnp.mean(dxh * xh, axis=-1, keepdims=True)
    return r * (dxh - xh * c), dn * xh


def _rms_bwd(name, x, gain, dn, dres):
    t, d = x.shape
    tm = min(ROW_TM, t)

    def body(x_ref, g_ref, dn_ref, dr_ref, dx_ref, dg_ref):
        @pl.when(pl.program_id(0) == 0)
        def _():
            dg_ref[...] = jnp.zeros_like(dg_ref)

        dx, dg = _rms_bwd_vals(x_ref[...], g_ref[...], dn_ref[...].astype(F32))
        dx_ref[...] = dr_ref[...] + dx
        dg_ref[...] += jnp.sum(dg, axis=0, keepdims=True)

    row = pl.BlockSpec((tm, d), lambda i: (i, 0))
    one = pl.BlockSpec((1, d), lambda i: (0, 0))
    return _call(
        body, name=name, grid=(t // tm,),
        in_specs=[row, one, row, row],
        out_specs=[row, one],
        out_shape=[jax.ShapeDtypeStruct((t, d), F32), jax.ShapeDtypeStruct((1, d), F32)],
        compiler_params=_cp(),
    )(x, gain, dn, dres)


def _colsum(name, x):
    t, n = x.shape
    tm = min(TM, t)

    def body(x_ref, o_ref):
        @pl.when(pl.program_id(0) == 0)
        def _():
            o_ref[...] = jnp.zeros_like(o_ref)

        o_ref[...] += jnp.sum(x_ref[...].astype(F32), axis=0, keepdims=True)

    return _call(
        body, name=name, grid=(t // tm,),
        in_specs=[pl.BlockSpec((tm, n), lambda i: (i, 0))],
        out_specs=pl.BlockSpec((1, n), lambda i: (0, 0)),
        out_shape=jax.ShapeDtypeStruct((1, n), F32),
        compiler_params=_cp(),
    )(x)


def _in_mla(u, w_in_mla, gq, gkv):
    t = u.shape[0]

    def compute(rows, weights, outs):
        p = _dot(rows[0][...], weights[0][...])
        outs[0][...] = p
        cq = p[:, 0:Q_LORA]
        ckv = p[:, Q_LORA:Q_LORA + KV_LORA]
        rq = lax.rsqrt(jnp.mean(cq * cq, axis=-1, keepdims=True) + EPS)
        rkv = lax.rsqrt(jnp.mean(ckv * ckv, axis=-1, keepdims=True) + EPS)
        outs[1][...] = (cq * rq * weights[1][...]).astype(BF16)
        outs[2][...] = (ckv * rkv * weights[2][...]).astype(BF16)

    return _rows_call("in_mla", [u], [w_in_mla, gq, gkv], [(MLA_COLS, F32), (Q_LORA, BF16), (KV_LORA, BF16)], compute,
                      min(FFN_TM, t))


def _lora_norm_bwd(p_mla, gq, gkv, dcqn, dckvn, dkpe):
    t = p_mla.shape[0]
    tm = min(ROW_TM, t)

    def body(p_ref, gq_ref, gkv_ref, dq_ref, dkv_ref, dkpe_ref, dp_ref, dgq_ref, dgkv_ref):
        @pl.when(pl.program_id(0) == 0)
        def _():
            dgq_ref[...] = jnp.zeros_like(dgq_ref)
            dgkv_ref[...] = jnp.zeros_like(dgkv_ref)

        dcq, dgq = _rms_bwd_vals(p_ref[:, 0:Q_LORA], gq_ref[...], dq_ref[...])
        dckv, dgkv = _rms_bwd_vals(p_ref[:, Q_LORA:Q_LORA + KV_LORA], gkv_ref[...], dkv_ref[...])
        dp_ref[:, 0:Q_LORA] = dcq.astype(BF16)
        dp_ref[:, Q_LORA:Q_LORA + KV_LORA] = dckv.astype(BF16)
        dp_ref[:, Q_LORA + KV_LORA:MLA_COLS] = dkpe_ref[...].astype(BF16)
        dgq_ref[...] += jnp.sum(dgq, axis=0, keepdims=True)
        dgkv_ref[...] += jnp.sum(dgkv, axis=0, keepdims=True)

    return _call(
        body, name="lora_norm_bwd", grid=(t // tm,),
        in_specs=[pl.BlockSpec((tm, MLA_COLS), lambda i: (i, 0)),
                  pl.BlockSpec((1, Q_LORA), lambda i: (0, 0)), pl.BlockSpec((1, KV_LORA), lambda i: (0, 0)),
                  pl.BlockSpec((tm, Q_LORA), lambda i: (i, 0)), pl.BlockSpec((tm, KV_LORA), lambda i: (i, 0)),
                  pl.BlockSpec((tm, HK), lambda i: (i, 0))],
        out_specs=[pl.BlockSpec((tm, MLA_COLS), lambda i: (i, 0)),
                   pl.BlockSpec((1, Q_LORA), lambda i: (0, 0)), pl.BlockSpec((1, KV_LORA), lambda i: (0, 0))],
        out_shape=[jax.ShapeDtypeStruct((t, MLA_COLS), BF16), jax.ShapeDtypeStruct((1, Q_LORA), F32),
                   jax.ShapeDtypeStruct((1, KV_LORA), F32)],
        compiler_params=_cp(),
    )(p_mla, gq, gkv, dcqn, dckvn, dkpe)


def _cumsum_rows(x, row):
    for s in (1, 2, 4, 8, 16, 32):
        x = x + jnp.where(row >= s, pltpu.roll(x, s, 0), 0.0)
    return x


def _rcumsum_rows(x, row):
    for s in (1, 2, 4, 8, 16, 32):
        x = x + jnp.where(row < CHUNK - s, pltpu.roll(x, CHUNK - s, 0), 0.0)
    return x


def _hg_gates(qr, z, lb, row):
    q = _silu(qr)
    sg = _sig(z)
    f = lb + (1.0 - lb) * sg
    lf = jnp.log(f)
    k = (1.0 - lb) * (1.0 - sg)
    cum = _cumsum_rows(lf, row)
    mid = jnp.sum(jnp.where(row < CHUNK // 2, lf, 0.0), axis=0, keepdims=True)
    last = jnp.sum(lf, axis=0, keepdims=True)
    e_q = jnp.exp(jnp.minimum(cum - mid, EXP_CLAMP))
    e_k = jnp.exp(jnp.minimum(mid - cum, EXP_CLAMP))
    e_a = jnp.exp(cum)
    e_l = jnp.exp(last - cum)
    return q, sg, f, k, last, e_q, e_k, e_a, e_l


def _hgrn_fwd(p_hg, tab, gain):
    t = p_hg.shape[0]
    bt = min(HG_BT, t)
    nb, nc = t // bt, bt // CHUNK

    hpb = HG_HPB
    wide = hpb * HK

    def body(q_ref, f_ref, i_ref, g_ref, tab_ref, gain_ref, o_ref, ho_ref, st_ref, state):
        @pl.when(pl.program_id(1) == 0)
        def _():
            state[...] = jnp.zeros_like(state)

        row = lax.broadcasted_iota(jnp.int32, (CHUNK, HK), 0)
        tril = lax.broadcasted_iota(jnp.int32, (CHUNK, CHUNK), 0) >= lax.broadcasted_iota(jnp.int32, (CHUNK, CHUNK), 1)
        gain_v = gain_ref[...]

        def chunk(c, carry):
            sl = pl.ds(pl.multiple_of(c * CHUNK, CHUNK), CHUNK)
            for hh in range(hpb):
                ln = slice(hh * HK, (hh + 1) * HK)
                lb = _sig(tab_ref[0:1, ln] - tab_ref[1:2, ln])
                v = i_ref[sl, ln].astype(BF16)
                q, _, _, k, last, e_q, e_k, e_a, e_l = _hg_gates(
                    q_ref[sl, ln].astype(F32), f_ref[sl, ln].astype(F32), lb, row)
                st = state[hh]
                st_ref[hh, c] = st
                p = jnp.where(tril, _dot_nt((q * e_q).astype(BF16), (k * e_k).astype(BF16)), 0.0)
                o = _dot(p.astype(BF16), v) + _dot_nt((q * e_a).astype(BF16), st.astype(BF16))
                state[hh] = jnp.exp(last) * st + _dot_tn(v, (k * e_l).astype(BF16))
                o_ref[sl, ln] = o
                r = lax.rsqrt(jnp.mean(o * o, axis=-1, keepdims=True) + EPS)
                ho_ref[sl, ln] = (o * r * gain_v * _silu(g_ref[sl, ln].astype(F32))).astype(BF16)
            return carry

        lax.fori_loop(0, nc, chunk, 0)

    def col(k):
        return pl.BlockSpec((bt, wide), lambda h, j, k=k: (j, k * (HEADS // hpb) + h))

    return _call(
        body, name="hgrn_fwd", grid=(HEADS // hpb, nb),
        in_specs=[col(0), col(1), col(2), col(3),
                  pl.BlockSpec((2, wide), lambda h, j: (0, h)), pl.BlockSpec((1, HK), lambda h, j: (0, 0))],
        out_specs=[pl.BlockSpec((bt, wide), lambda h, j: (j, h)), pl.BlockSpec((bt, wide), lambda h, j: (j, h)),
                   pl.BlockSpec((hpb, nc, HK, HK), lambda h, j: (h, j, 0, 0))],
        out_shape=[jax.ShapeDtypeStruct((t, D), F32), jax.ShapeDtypeStruct((t, D), BF16),
                   jax.ShapeDtypeStruct((HEADS, t // CHUNK, HK, HK), F32)],
        scratch_shapes=[pltpu.VMEM((hpb, HK, HK), F32)],
        compiler_params=_cp(),
    )(p_hg, p_hg, p_hg, p_hg, tab, gain)


def _hgrn_bwd(p_hg, tab, gain, o_raw, states, dho):
    t = p_hg.shape[0]
    bt = min(HG_BT, t)
    nb, nc = t // bt, bt // CHUNK
    hpb = HG_HPB
    wide = hpb * HK

    def body(q_ref, f_ref, i_ref, g_ref, tab_ref, gain_ref, o_ref, st_ref, dho_ref,
             dq_ref, df_ref, di_ref, dg_ref, dtab_ref, dgain_ref, dstate, dlb):
        h, j = pl.program_id(0), pl.program_id(1)

        @pl.when(jnp.logical_and(h == 0, j == 0))
        def _():
            dgain_ref[...] = jnp.zeros_like(dgain_ref)

        @pl.when(j == 0)
        def _():
            dstate[...] = jnp.zeros_like(dstate)
            dlb[...] = jnp.zeros_like(dlb)

        row = lax.broadcasted_iota(jnp.int32, (CHUNK, HK), 0)
        tril = lax.broadcasted_iota(jnp.int32, (CHUNK, CHUNK), 0) >= lax.broadcasted_iota(jnp.int32, (CHUNK, CHUNK), 1)
        gain_v = gain_ref[...]

        def chunk(cc, carry):
            c = nc - 1 - cc
            sl = pl.ds(pl.multiple_of(c * CHUNK, CHUNK), CHUNK)
            dgain = jnp.zeros((1, HK), F32)
            for hh in range(hpb):
                ln = slice(hh * HK, (hh + 1) * HK)
                lb = _sig(tab_ref[0:1, ln] - tab_ref[1:2, ln])
                qr = q_ref[sl, ln].astype(F32)
                v = i_ref[sl, ln].astype(BF16)
                gr = g_ref[sl, ln].astype(F32)
                q, sg, f, k, last, e_q, e_k, e_a, e_l = _hg_gates(qr, f_ref[sl, ln].astype(F32), lb, row)
                o = o_ref[sl, ln]
                r = lax.rsqrt(jnp.mean(o * o, axis=-1, keepdims=True) + EPS)
                oh = o * r
                dh = dho_ref[sl, ln].astype(F32)
                dnorm = dh * _silu(gr)
                dg_ref[sl, ln] = (dh * oh * gain_v * _dsilu(gr)).astype(BF16)
                dgain = dgain + jnp.sum(dnorm * oh, axis=0, keepdims=True)
                dxh = dnorm * gain_v
                do = (r * (dxh - oh * jnp.mean(dxh * oh, axis=-1, keepdims=True))).astype(BF16)
                st0 = st_ref[hh, c]
                st0_b = st0.astype(BF16)
                ds1 = dstate[hh]
                ds1_b = ds1.astype(BF16)
                qt = (q * e_q).astype(BF16)
                kt = (k * e_k).astype(BF16)
                qd = (q * e_a).astype(BF16)
                kd = (k * e_l).astype(BF16)
                p = jnp.where(tril, _dot_nt(qt, kt), 0.0).astype(BF16)
                dp = jnp.where(tril, _dot_nt(do, v), 0.0).astype(BF16)
                dv = _dot_tn(p, do) + _dot_nt(kd, ds1_b)
                dqt = _dot(dp, kt)
                dkt = _dot_tn(dp, qt)
                dq_inter = _dot(do, st0_b) * e_a
                dk_inter = _dot(v, ds1_b) * e_l
                dq = dqt * e_q + dq_inter
                dk = dkt * e_k + dk_inter
                e_last = jnp.exp(last)
                dstate[hh] = _dot_tn(do, qd) + e_last * ds1
                dlast = (jnp.sum(k * dk_inter, axis=0, keepdims=True)
                         + e_last * jnp.sum(ds1 * st0, axis=0, keepdims=True))
                da = (qt.astype(F32) * dqt - kt.astype(F32) * dkt + q * dq_inter - k * dk_inter
                      + jnp.where(row == CHUNK - 1, dlast, 0.0))
                dlf = _rcumsum_rows(da, row)
                dfv = dlf / f - dk
                df_ref[sl, ln] = (dfv * (1.0 - lb) * sg * (1.0 - sg)).astype(BF16)
                dlb[:, ln] += jnp.sum(dfv * (1.0 - sg), axis=0, keepdims=True)
                dq_ref[sl, ln] = (dq * _dsilu(qr)).astype(BF16)
                di_ref[sl, ln] = dv.astype(BF16)
            dgain_ref[...] += dgain
            return carry

        lax.fori_loop(0, nc, chunk, 0)

        @pl.when(j == nb - 1)
        def _():
            lb = _sig(tab_ref[0:1, :] - tab_ref[1:2, :])
            d0 = dlb[...] * lb * (1.0 - lb)
            dtab_ref[0:1, :] = d0
            dtab_ref[1:2, :] = -d0

    def col(k):
        return pl.BlockSpec((bt, wide), lambda h, j, k=k: (nb - 1 - j, k * (HEADS // hpb) + h))

    tok = pl.BlockSpec((bt, wide), lambda h, j: (nb - 1 - j, h))
    return _call(
        body, name="hgrn_bwd", grid=(HEADS // hpb, nb),
        in_specs=[col(0), col(1), col(2), col(3),
                  pl.BlockSpec((2, wide), lambda h, j: (0, h)), pl.BlockSpec((1, HK), lambda h, j: (0, 0)),
                  tok, pl.BlockSpec((hpb, nc, HK, HK), lambda h, j: (h, nb - 1 - j, 0, 0)), tok],
        out_specs=[tok, tok, tok, tok,
                   pl.BlockSpec((2, wide), lambda h, j: (0, h)), pl.BlockSpec((1, HK), lambda h, j: (0, 0))],
        out_shape=[jax.ShapeDtypeStruct((t, D), BF16)] * 4
        + [jax.ShapeDtypeStruct((2, D), F32), jax.ShapeDtypeStruct((1, HK), F32)],
        scratch_shapes=[pltpu.VMEM((hpb, HK, HK), F32), pltpu.VMEM((1, wide), F32)],
        compiler_params=_cp(),
    )(p_hg, p_hg, p_hg, p_hg, tab, gain, o_raw, states, dho)


def _rope_tables(pos):
    t = pos.shape[0]
    tm = min(ROW_TM, t)
    inv = np.zeros((1, HK), np.float32)
    freq = (ROPE_THETA ** (-np.arange(0, ROPE, 2, dtype=np.float32) / ROPE)).astype(np.float32)
    inv[0, 0:ROPE // 2] = freq
    inv[0, ROPE // 2:ROPE] = freq
    sign = np.zeros((1, HK), np.float32)
    sign[0, 0:ROPE // 2] = -1.0
    sign[0, ROPE // 2:ROPE] = 1.0

    def body(pos_ref, inv_ref, sign_ref, cos_ref, sin_ref):
        ang = pos_ref[...].astype(F32) * inv_ref[...]
        cos_ref[...] = jnp.cos(ang)
        sin_ref[...] = jnp.sin(ang) * sign_ref[...]

    one = pl.BlockSpec((1, HK), lambda i: (0, 0))
    row = pl.BlockSpec((tm, HK), lambda i: (i, 0))
    return _call(
        body, name="rope_tables", grid=(t // tm,),
        in_specs=[pl.BlockSpec((tm, 1), lambda i: (i, 0)), one, one],
        out_specs=[row, row],
        out_shape=[jax.ShapeDtypeStruct((t, HK), F32)] * 2,
        compiler_params=_cp(),
    )(pos, jnp.asarray(inv), jnp.asarray(sign))


def _rope(x, cos, sin_signed):
    r = lax.broadcasted_iota(jnp.int32, (HK, HK), 0)
    c = lax.broadcasted_iota(jnp.int32, (HK, HK), 1)
    half = ROPE // 2
    swap = jnp.logical_or(jnp.logical_and(c < half, r == c + half),
                          jnp.logical_and(jnp.logical_and(c >= half, c < ROPE), r == c - half))
    return x * cos + _dot_split(x, swap.astype(BF16)) * sin_signed


def _dot_split(x, m):
    hi = x.astype(BF16)
    lo = (x - hi.astype(F32)).astype(BF16)
    return _dot(hi, m) + _dot(lo, m)


def _lane_sum(x):
    return _dot_split(x, jnp.ones((HK, HK), BF16))


def _head_norm(xn, xr):
    r = lax.rsqrt(_lane_sum(xn * xn + xr * xr) * (1.0 / QK) + EPS)
    return xn * r, xr * r, r


def _head_norm_bwd(xn, xr, g_n, g_r, dn, dr):
    hn, hr, r = _head_norm(xn, xr)
    dxn, dxr = dn * g_n, dr * g_r
    c = _lane_sum(dxn * hn + dxr * hr) * (1.0 / QK)
    return r * (dxn - hn * c), r * (dxr - hr * c), dn * hn, dr * hr


def _mla_prep_fwd(qf, kv, p_mla, cos, sin, gq, gk):
    t = qf.shape[0]
    tm = min(PREP_TM, t)

    def body(qf_ref, kv_ref, kpe_ref, cos_ref, sin_ref, gq_ref, gk_ref, q_ref, k_ref, v_ref):
        cos_v, sin_v = cos_ref[...], sin_ref[...]
        kpe = kpe_ref[...]
        for h in range(HEADS):
            lo, mid, hi = h * QKP, h * QKP + HK, (h + 1) * QKP
            qn, qr, _ = _head_norm(qf_ref[:, lo:mid], qf_ref[:, mid:hi])
            q_ref[h, :, 0:HK] = (qn * gq_ref[:, 0:HK] * (SCALE * LOG2E)).astype(BF16)
            q_ref[h, :, HK:QKP] = (_rope(qr * gq_ref[:, HK:QKP], cos_v, sin_v) * (SCALE * LOG2E)).astype(BF16)
            kn, kr, _ = _head_norm(kv_ref[:, lo:mid], kpe)
            k_ref[h, :, 0:HK] = (kn * gk_ref[:, 0:HK]).astype(BF16)
            k_ref[h, :, HK:QKP] = _rope(kr * gk_ref[:, HK:QKP], cos_v, sin_v).astype(BF16)
            v_ref[h, :, 0:HK] = kv_ref[:, mid:hi].astype(BF16)
            v_ref[h, :, HK:QKP] = jnp.full((tm, HK), -1.0, BF16)

    head = pl.BlockSpec((tm, HEADS * QKP), lambda i: (i, 0))
    tok = pl.BlockSpec((tm, HK), lambda i: (i, 0))
    gain = pl.BlockSpec((1, QKP), lambda i: (0, 0))
    return _call(
        body, name="mla_prep_fwd", grid=(t // tm,),
        in_specs=[head, head, pl.BlockSpec((tm, HK), lambda i: (i, MLA_COLS // HK - 1)), tok, tok, gain, gain],
        out_specs=[pl.BlockSpec((HEADS, tm, QKP), lambda i: (0, i, 0)),
                   pl.BlockSpec((HEADS, tm, QKP), lambda i: (0, i, 0)),
                   pl.BlockSpec((HEADS, tm, QKP), lambda i: (0, i, 0))],
        out_shape=[jax.ShapeDtypeStruct((HEADS, t, QKP), BF16), jax.ShapeDtypeStruct((HEADS, t, QKP), BF16),
                   jax.ShapeDtypeStruct((HEADS, t, QKP), BF16)],
        compiler_params=_cp(),
    )(qf, kv, p_mla, cos, sin, gq, gk)


def _mla_prep_bwd(qf, kv, p_mla, cos, sin, gq, gk, dq, dk, dv):
    t = qf.shape[0]
    tm = min(PREP_TM, t)

    def body(qf_ref, kv_ref, kpe_ref, cos_ref, sin_ref, gq_ref, gk_ref, dq_ref, dk_ref, dv_ref,
             dqf_ref, dkv_ref, dkpe_ref, dgq_ref, dgk_ref):
        @pl.when(pl.program_id(0) == 0)
        def _():
            dgq_ref[...] = jnp.zeros_like(dgq_ref)
            dgk_ref[...] = jnp.zeros_like(dgk_ref)

        cos_v, sin_v = cos_ref[...], -sin_ref[...]
        kpe = kpe_ref[...]
        gqn, gqr, gkn, gkr = gq_ref[:, 0:HK], gq_ref[:, HK:QKP], gk_ref[:, 0:HK], gk_ref[:, HK:QKP]
        dkpe = jnp.zeros((tm, HK), F32)
        dgq_n, dgq_r, dgk_n, dgk_r = [jnp.zeros((1, HK), F32) for _ in range(4)]
        for h in range(HEADS):
            lo, mid, hi = h * QKP, h * QKP + HK, (h + 1) * QKP
            dqn = dq_ref[h, :, 0:HK].astype(F32) * SCALE
            dqr = _rope(dq_ref[h, :, HK:QKP].astype(F32), cos_v, sin_v) * SCALE
            a, b, ga, gb = _head_norm_bwd(qf_ref[:, lo:mid], qf_ref[:, mid:hi], gqn, gqr, dqn, dqr)
            dqf_ref[:, lo:mid] = a.astype(BF16)
            dqf_ref[:, mid:hi] = b.astype(BF16)
            dgq_n = dgq_n + jnp.sum(ga, axis=0, keepdims=True)
            dgq_r = dgq_r + jnp.sum(gb, axis=0, keepdims=True)
            dkn = dk_ref[h, :, 0:HK].astype(F32) * LN2
            dkr = _rope(dk_ref[h, :, HK:QKP].astype(F32), cos_v, sin_v) * LN2
            a, b, ga, gb = _head_norm_bwd(kv_ref[:, lo:mid], kpe, gkn, gkr, dkn, dkr)
            dkv_ref[:, lo:mid] = a.astype(BF16)
            dkv_ref[:, mid:hi] = dv_ref[h].astype(BF16)
            dkpe = dkpe + b
            dgk_n = dgk_n + jnp.sum(ga, axis=0, keepdims=True)
            dgk_r = dgk_r + jnp.sum(gb, axis=0, keepdims=True)
        dkpe_ref[...] = dkpe
        dgq_ref[:, 0:HK] += dgq_n
        dgq_ref[:, HK:QKP] += dgq_r
        dgk_ref[:, 0:HK] += dgk_n
        dgk_ref[:, HK:QKP] += dgk_r

    head = pl.BlockSpec((tm, HEADS * QKP), lambda i: (i, 0))
    tok = pl.BlockSpec((tm, HK), lambda i: (i, 0))
    gain = pl.BlockSpec((1, QKP), lambda i: (0, 0))
    hq = pl.BlockSpec((HEADS, tm, QKP), lambda i: (0, i, 0))
    return _call(
        body, name="mla_prep_bwd", grid=(t // tm,),
        in_specs=[head, head, pl.BlockSpec((tm, HK), lambda i: (i, MLA_COLS // HK - 1)), tok, tok, gain, gain,
                  hq, hq, pl.BlockSpec((HEADS, tm, HK), lambda i: (0, i, 0))],
        out_specs=[head, head, tok, gain, gain],
        out_shape=[jax.ShapeDtypeStruct((t, HEADS * QKP), BF16), jax.ShapeDtypeStruct((t, HEADS * QKP), BF16),
                   jax.ShapeDtypeStruct((t, HK), F32), jax.ShapeDtypeStruct((1, QKP), F32),
                   jax.ShapeDtypeStruct((1, QKP), F32)],
        compiler_params=_cp(),
    )(qf, kv, p_mla, cos, sin, gq, gk, dq, dk, dv)


def _chunk_mask(row0, rows, cols):
    r = lax.broadcasted_iota(jnp.int32, (rows, cols), 0) + row0
    c = lax.broadcasted_iota(jnp.int32, (rows, cols), 1)
    return jnp.right_shift(r, 6) >= jnp.right_shift(c, 6)


def _flash_fwd(q, k, v, side=None):
    t = q.shape[1]
    tq = min(TQ, t)
    nq = t // tq
    sub = min(SUBQ, tq)
    pairs = [(i, j) for i in range(nq) for j in range(i + 1)]
    qi = jnp.asarray([p[0] for p in pairs], jnp.int32)
    kj = jnp.asarray([p[1] for p in pairs], jnp.int32)
    s_in = len(side.inputs) if side else 0
    s_out = len(side.out_shapes) if side else 0

    def body(qi_ref, kj_ref, q_ref, k_ref, v_ref, *rest):
        o_ref, lse_ref = rest[s_in:s_in + 2]
        m_s, acc_s = rest[s_in + 2 + s_out:s_in + 4 + s_out]
        side_refs = list(rest[:s_in]) + list(rest[s_in + 2:s_in + 2 + s_out]) + list(rest[s_in + 4 + s_out:])
        n = pl.program_id(1)
        i, j = qi_ref[n], kj_ref[n]
        if side:
            @pl.when(jnp.logical_and(pl.program_id(0) == 0, n == 0))
            def _():
                side.start(*side_refs)

        @pl.when(j == 0)
        def _():
            m_s[...] = jnp.full_like(m_s, NEG)
            acc_s[...] = jnp.zeros_like(acc_s)

        def step(diag):
            subs = range(tq // sub)
            width = [(r + 1) * sub if diag else tq for r in subs]
            logits = [_dot_nt(q_ref[r * sub:(r + 1) * sub, :], k_ref[0:width[r], :]) for r in subs]
            for r in subs:
                rows = slice(r * sub, (r + 1) * sub)
                cols = width[r]
                s = logits[r]
                if diag:
                    s = jnp.where(_chunk_mask(r * sub, sub, cols), s, NEG)
                m_old = m_s[rows, :]
                m_new = jnp.maximum(m_old, jnp.max(s, axis=-1, keepdims=True))
                alpha = jnp.exp2(m_old - m_new)
                p = jnp.exp2((s - jnp.tile(m_new, (1, cols // HK))).astype(BF16))
                acc_s[rows, :] = jnp.tile(alpha, (1, 2)) * acc_s[rows, :] + _dot(p, v_ref[0:cols, :])
                m_s[rows, :] = m_new

        @pl.when(j < i)
        def _():
            step(False)

        @pl.when(j == i)
        def _():
            step(True)
            l = -acc_s[:, HK:QKP]
            o_ref[...] = (acc_s[:, 0:HK] / l).astype(BF16)
            lse_ref[...] = m_s[...] + jnp.log(l) * LOG2E

        if side:
            @pl.when(jnp.logical_and(pl.program_id(0) == HEADS - 1, n == len(pairs) - 1))
            def _():
                side.finish(*side_refs)

    anywhere = pl.BlockSpec(memory_space=pl.ANY)
    grid_spec = pltpu.PrefetchScalarGridSpec(
        num_scalar_prefetch=2, grid=(HEADS, len(pairs)),
        in_specs=[pl.BlockSpec((None, tq, QKP), lambda h, n, qi, kj: (h, qi[n], 0)),
                  pl.BlockSpec((None, tq, QKP), lambda h, n, qi, kj: (h, kj[n], 0)),
                  pl.BlockSpec((None, tq, QKP), lambda h, n, qi, kj: (h, kj[n], 0))] + [anywhere] * s_in,
        out_specs=[pl.BlockSpec((tq, HK), lambda h, n, qi, kj: (qi[n], h)),
                   pl.BlockSpec((None, tq, HK), lambda h, n, qi, kj: (h, qi[n], 0))] + [anywhere] * s_out,
        scratch_shapes=[pltpu.VMEM((tq, HK), F32), pltpu.VMEM((tq, QKP), F32)] + (list(side.scratch) if side else []),
    )
    return _call(
        body, name="flash_fwd", grid_spec=grid_spec,
        out_shape=[jax.ShapeDtypeStruct((t, D), BF16), jax.ShapeDtypeStruct((HEADS, t, HK), F32)]
        + (list(side.out_shapes) if side else []),
        compiler_params=_cp(),
    )(qi, kj, q, k, v, *(side.inputs if side else []))


def _attn_do(do, o):
    t = do.shape[0]
    tm = min(TM, t)

    def body(do_ref, o_ref, d_ref):
        lane = lax.broadcasted_iota(jnp.int32, (tm, HK), 1)
        for h in range(HEADS):
            ln = slice(h * HK, (h + 1) * HK)
            dov = do_ref[:, ln]
            d = jnp.sum(dov.astype(F32) * o_ref[:, ln].astype(F32), axis=-1, keepdims=True)
            hi = d.astype(BF16).astype(F32)
            d_ref[h, :, 0:HK] = dov
            d_ref[h, :, HK:QKP] = jnp.where(lane == 0, hi, jnp.where(lane == 1, d - hi, 0.0)).astype(BF16)

    blk = pl.BlockSpec((tm, D), lambda i: (i, 0))
    return _call(
        body, name="attn_do", grid=(t // tm,),
        in_specs=[blk, blk],
        out_specs=pl.BlockSpec((HEADS, tm, QKP), lambda i: (0, i, 0)),
        out_shape=jax.ShapeDtypeStruct((HEADS, t, QKP), BF16),
        compiler_params=_cp(),
    )(do, o)


def _flash_bwd(q, k, v, lse, do):
    t = q.shape[1]
    tq = min(TQ, t)
    nq = t // tq
    sub = min(SUBQ, tq)
    pairs = [(i, j) for j in range(nq) for i in range(j, nq)]
    qi = jnp.asarray([p[0] for p in pairs], jnp.int32)
    kj = jnp.asarray([p[1] for p in pairs], jnp.int32)
    npairs = len(pairs)

    def body(qi_ref, kj_ref, q_ref, k_ref, v_ref, lse_ref, do_ref, dq_ref, dk_ref, dv_ref):
        n = pl.program_id(1)
        i, j = qi_ref[n], kj_ref[n]

        @pl.when(n == 0)
        def _():
            dq_ref[...] = jnp.zeros_like(dq_ref)

        @pl.when(i == j)
        def _():
            dk_ref[...] = jnp.zeros_like(dk_ref)
            dv_ref[...] = jnp.zeros_like(dv_ref)

        def step(diag):
            for r in range(tq // sub):
                rows = slice(r * sub, (r + 1) * sub)
                cols = (r + 1) * sub if diag else tq
                qv, kv_ = q_ref[rows, :], k_ref[0:cols, :]
                p = jnp.exp2(_dot_nt(qv, kv_) - jnp.tile(lse_ref[rows, :], (1, cols // HK)))
                if diag:
                    p = jnp.where(_chunk_mask(r * sub, sub, cols), p, 0.0)
                dp_less_delta = _dot_nt(do_ref[rows, :], v_ref[0:cols, :])
                ds = (p * dp_less_delta).astype(BF16)
                dv_ref[0:cols, :] += _dot_tn(p.astype(BF16), do_ref[rows, 0:HK])
                dk_ref[0:cols, :] += _dot_tn(ds, qv)
                dq_rows = pl.ds(pl.multiple_of(i * tq + r * sub, sub), sub)
                dq_ref[dq_rows, :] += _dot(ds, kv_)

        @pl.when(j < i)
        def _():
            step(False)

        @pl.when(j == i)
        def _():
            step(True)

    grid_spec = pltpu.PrefetchScalarGridSpec(
        num_scalar_prefetch=2, grid=(HEADS, npairs),
        in_specs=[pl.BlockSpec((None, tq, QKP), lambda h, n, qi, kj: (h, qi[n], 0)),
                  pl.BlockSpec((None, tq, QKP), lambda h, n, qi, kj: (h, kj[n], 0)),
                  pl.BlockSpec((None, tq, QKP), lambda h, n, qi, kj: (h, kj[n], 0)),
                  pl.BlockSpec((None, tq, HK), lambda h, n, qi, kj: (h, qi[n], 0)),
                  pl.BlockSpec((None, tq, QKP), lambda h, n, qi, kj: (h, qi[n], 0))],
        out_specs=[pl.BlockSpec((None, t, QKP), lambda h, n, qi, kj: (h, 0, 0)),
                   pl.BlockSpec((None, tq, QKP), lambda h, n, qi, kj: (h, kj[n], 0)),
                   pl.BlockSpec((None, tq, HK), lambda h, n, qi, kj: (h, kj[n], 0))],
    )
    return _call(
        body, name="flash_bwd", grid_spec=grid_spec,
        out_shape=[jax.ShapeDtypeStruct((HEADS, t, QKP), F32), jax.ShapeDtypeStruct((HEADS, t, QKP), F32),
                   jax.ShapeDtypeStruct((HEADS, t, HK), F32)],
        compiler_params=_cp(56),
    )(qi, kj, q, k, v, lse, do)


def _adamw(name, w, g, m, v):
    r, c = w.shape
    tr = r if r <= 256 else next(k for k in (256, 352, 384) if r % k == 0)

    def body(w_ref, g_ref, m_ref, v_ref, d_ref, nm_ref, nv_ref):
        gv = g_ref[...]
        nm = ADAM_B1 * m_ref[...] + (1.0 - ADAM_B1) * gv
        nv = ADAM_B2 * v_ref[...] + (1.0 - ADAM_B2) * (gv * gv)
        m_hat = nm / (1.0 - ADAM_B1 ** ADAM_STEP)
        v_hat = nv / (1.0 - ADAM_B2 ** ADAM_STEP)
        d_ref[...] = -ADAM_LR * (m_hat / (jnp.sqrt(v_hat) + ADAM_EPS) + ADAM_WD * w_ref[...])
        nm_ref[...] = nm
        nv_ref[...] = nv

    blk = pl.BlockSpec((tr, c), lambda i: (i, 0))
    return _call(
        body, name=name, grid=(r // tr,),
        in_specs=[blk] * 4, out_specs=[blk] * 3,
        out_shape=[jax.ShapeDtypeStruct((r, c), F32)] * 3,
        compiler_params=_cp(),
    )(w, g, m, v)


def _place():
    return lax.axis_index("x"), lax.axis_index("y"), lax.axis_index("c")


def _other_chips(x, y):
    return [(1 - x, y), (x, 1 - y), (1 - x, 1 - y)]


class _Exchange:
    inputs = ()
    out_shapes = ()
    scratch = ()

    def start(self, *refs):
        raise NotImplementedError

    def finish(self, *refs):
        raise NotImplementedError

    def alone(self, name):
        def body(*refs):
            self.start(*refs)
            self.finish(*refs)

        anywhere = pl.BlockSpec(memory_space=pl.ANY)
        return _call(
            body, name=name,
            in_specs=[anywhere] * len(self.inputs), out_specs=[anywhere] * len(self.out_shapes),
            out_shape=list(self.out_shapes), scratch_shapes=list(self.scratch),
        )(*self.inputs)


class _GatherWeights(_Exchange):
    def __init__(self, shards):
        self.inputs = tuple(shards)
        self.out_shapes = tuple(jax.ShapeDtypeStruct((4,) + s.shape, s.dtype) for s in shards)
        self.scratch = (pltpu.SemaphoreType.DMA((6 * len(shards),)), pltpu.SemaphoreType.DMA((6 * len(shards),)))

    def gathered(self, got, k):
        return [lax.dynamic_update_slice(g, s[None], (k, 0, 0)) for g, s in zip(got, self.inputs)]

    def _copies(self, *refs):
        nbuf = len(self.inputs)
        send_sems, recv_sems = refs[2 * nbuf:]
        x, y, c = _place()
        chips = _other_chips(x, y)
        first, passed, landed, relayed = [], [], [], []
        for b, (s_ref, g_ref) in enumerate(zip(refs[:nbuf], refs[nbuf:2 * nbuf])):
            half = self.inputs[b].shape[0] // 2

            def rows(px, py, pc, g_ref=g_ref, half=half):
                return g_ref.at[2 * px + py, pl.ds(pc * half, half), :]

            def copy(k, block, to, src=None, rows=rows, b=b):
                return pltpu.make_async_remote_copy(
                    src_ref=rows(*block) if src is None else src, dst_ref=rows(*block),
                    send_sem=send_sems.at[6 * b + k], recv_sem=recv_sems.at[6 * b + k], device_id=to, device_id_type=MESH)

            mine = s_ref.at[pl.ds(c * half, half), :]
            first += [copy(j, (x, y, c), (*chip, c), src=mine) for j, chip in enumerate(chips)]
            passed += [copy(3 + j, (*chip, c), (x, y, 1 - c)) for j, chip in enumerate(chips)]
            landed += [copy(j, (*chip, c), (x, y, c)) for j, chip in enumerate(chips)]
            relayed += [copy(3 + j, (*chip, 1 - c), (x, y, c)) for j, chip in enumerate(chips)]
        return first, passed, landed, relayed

    def start(self, *refs):
        for cp in self._copies(*refs)[0]:
            cp.start()

    def finish(self, *refs):
        first, passed, landed, relayed = self._copies(*refs)
        for arrived, onward in zip(landed, passed):
            arrived.wait_recv()
            onward.start()
        for cp in relayed:
            cp.wait_recv()
        for cp in first + passed:
            cp.wait_send()


class _SwapHalves(_Exchange):
    def __init__(self, bufs):
        self.inputs = tuple(bufs)
        self.out_shapes = tuple(jax.ShapeDtypeStruct((4, g.shape[1] // 2, g.shape[2]), g.dtype) for g in bufs)
        self.scratch = (pltpu.SemaphoreType.DMA((len(bufs),)), pltpu.SemaphoreType.DMA((len(bufs),)))

    def _copies(self, *refs):
        nbuf = len(self.inputs)
        send_sems, recv_sems = refs[2 * nbuf:]
        x, y, c = _place()
        cps = []
        for b, (g_ref, o_ref) in enumerate(zip(refs[:nbuf], refs[nbuf:2 * nbuf])):
            half = self.inputs[b].shape[1] // 2
            cps.append(pltpu.make_async_remote_copy(
                src_ref=g_ref.at[:, pl.ds((1 - c) * half, half), :], dst_ref=o_ref,
                send_sem=send_sems.at[b], recv_sem=recv_sems.at[b], device_id=(x, y, 1 - c), device_id_type=MESH))
        return cps

    def start(self, *refs):
        for cp in self._copies(*refs):
            cp.start()

    def finish(self, *refs):
        for cp in self._copies(*refs):
            cp.wait()


def _add_rows(half):
    return next(tr for tr in range(512, 15, -16) if half % tr == 0)


def _chip_sum(name, gp, got, c_arr):
    half, width = got.shape[1], got.shape[2]
    tr = _add_rows(half)
    nb = half // tr

    def body(c_ref, a_ref, b_ref, o_ref, ob_ref):
        s = a_ref[...] + b_ref[...]
        o_ref[...] = s
        ob_ref[...] = s.astype(BF16)

    grid_spec = pltpu.PrefetchScalarGridSpec(
        num_scalar_prefetch=1, grid=(4, nb),
        in_specs=[pl.BlockSpec((None, tr, width), lambda s, i, c: (s, c[0] * nb + i, 0)),
                  pl.BlockSpec((None, tr, width), lambda s, i, c: (s, i, 0))],
        out_specs=[pl.BlockSpec((None, tr, width), lambda s, i, c: (s, i, 0)),
                   pl.BlockSpec((None, tr, width), lambda s, i, c: (s, i, 0))],
    )
    return _call(
        body, name=name, grid_spec=grid_spec,
        out_shape=[jax.ShapeDtypeStruct(got.shape, F32), jax.ShapeDtypeStruct(got.shape, BF16)],
        compiler_params=_cp(),
    )(c_arr, gp, got)


class _ScatterChipSums(_Exchange):
    def __init__(self, sums):
        self.inputs = tuple(sums)
        self.out_shapes = tuple(jax.ShapeDtypeStruct((3,) + cs.shape[1:], cs.dtype) for cs in sums)
        self.scratch = (pltpu.SemaphoreType.DMA((3 * len(sums),)), pltpu.SemaphoreType.DMA((3 * len(sums),)))

    def _copies(self, *refs):
        nbuf = len(self.inputs)
        send_sems, recv_sems = refs[2 * nbuf:]
        x, y, c = _place()
        return [pltpu.make_async_remote_copy(
            src_ref=s_ref.at[2 * px + py], dst_ref=o_ref.at[j],
            send_sem=send_sems.at[3 * b + j], recv_sem=recv_sems.at[3 * b + j], device_id=(px, py, c), device_id_type=MESH)
            for b, (s_ref, o_ref) in enumerate(zip(refs[:nbuf], refs[nbuf:2 * nbuf]))
            for j, (px, py) in enumerate(_other_chips(x, y))]

    def start(self, *refs):
        for cp in self._copies(*refs):
            cp.start()

    def finish(self, *refs):
        for cp in self._copies(*refs):
            cp.wait()


def _shard_sum(name, cs, got, kc_arr):
    h, width = cs.shape[1], cs.shape[2]
    tr = _add_rows(h)
    nb = h // tr

    def body(k_ref, a_ref, b_ref, o_ref):
        o_ref[...] = ((a_ref[...] + b_ref[0].astype(F32)) + b_ref[1].astype(F32)) + b_ref[2].astype(F32)

    grid_spec = pltpu.PrefetchScalarGridSpec(
        num_scalar_prefetch=1, grid=(nb,),
        in_specs=[pl.BlockSpec((None, tr, width), lambda i, k: (k[0], i, 0)),
                  pl.BlockSpec((3, tr, width), lambda i, k: (0, i, 0))],
        out_specs=pl.BlockSpec((tr, width), lambda i, k: (k[1] * nb + i, 0)),
    )
    return _call(
        body, name=name, grid_spec=grid_spec,
        out_shape=jax.ShapeDtypeStruct((2 * h, width), F32),
        compiler_params=_cp(),
    )(kc_arr, cs, got)


def _join_halves(name, boths):
    nbuf = len(boths)

    def body(*refs):
        send_sems, recv_sems = refs[2 * nbuf:]
        x, y, c = _place()
        sent, landing = [], []
        for b, (m_ref, o_ref) in enumerate(zip(refs[:nbuf], refs[nbuf:2 * nbuf])):
            h = boths[b].shape[0] // 2
            mine = m_ref.at[pl.ds(c * h, h), :]
            sent.append(pltpu.make_async_remote_copy(
                src_ref=mine, dst_ref=o_ref.at[pl.ds(c * h, h), :],
                send_sem=send_sems.at[b], recv_sem=recv_sems.at[b], device_id=(x, y, 1 - c), device_id_type=MESH))
            landing.append(pltpu.make_async_remote_copy(
                src_ref=mine, dst_ref=o_ref.at[pl.ds((1 - c) * h, h), :],
                send_sem=send_sems.at[b], recv_sem=recv_sems.at[b], device_id=(x, y, 1 - c), device_id_type=MESH))
        for cp in sent:
            cp.start()
        for cp in sent:
            cp.wait_send()
        for cp in landing:
            cp.wait_recv()

    anywhere = pl.BlockSpec(memory_space=pl.ANY)
    return _call(
        body, name=name,
        in_specs=[anywhere] * nbuf, out_specs=[anywhere] * nbuf,
        out_shape=[jax.ShapeDtypeStruct(g.shape, g.dtype) for g in boths],
        input_output_aliases={b: b for b in range(nbuf)},
        scratch_shapes=[pltpu.SemaphoreType.DMA((nbuf,)), pltpu.SemaphoreType.DMA((nbuf,))],
    )(*boths)


def _all_reduce_small(v):
    r = v.shape[0]

    def body(v_ref, o_ref, buf, send_sems, recv_sems):
        x, y, c = _place()
        me = 4 * x + 2 * y + c
        buf[me] = v_ref[...]
        cps = []
        for k in range(1, 8):
            peer = (x ^ (k >> 2), y ^ ((k >> 1) & 1), c ^ (k & 1))
            cps.append(pltpu.make_async_remote_copy(
                src_ref=v_ref, dst_ref=buf.at[me],
                send_sem=send_sems.at[k - 1], recv_sem=recv_sems.at[k - 1], device_id=peer, device_id_type=MESH))
        for cp in cps:
            cp.start()
        for k in range(1, 8):
            pltpu.make_async_remote_copy(
                src_ref=v_ref, dst_ref=buf.at[me ^ k],
                send_sem=send_sems.at[k - 1], recv_sem=recv_sems.at[k - 1],
                device_id=(x, y, c), device_id_type=MESH).wait_recv()
        for cp in cps:
            cp.wait_send()
        acc = buf[0]
        for k in range(1, 8):
            acc = acc + buf[k]
        o_ref[...] = acc

    return _call(
        body, name="all_reduce_small",
        in_specs=[pl.BlockSpec(memory_space=pltpu.VMEM)],
        out_specs=pl.BlockSpec(memory_space=pltpu.VMEM),
        out_shape=jax.ShapeDtypeStruct((r, 128), F32),
        scratch_shapes=[pltpu.VMEM((8, r, 128), F32), pltpu.SemaphoreType.DMA((7,)), pltpu.SemaphoreType.DMA((7,))],
    )(v)


def _group(names):
    return tuple(e for e in BIG if e[0] in names)


def _pack(shards, dtype):
    return jnp.concatenate([s.astype(dtype).reshape(-1, PACK_W) for s in shards], axis=0)


def _unpack_full(g, group):
    out, at = {}, 0
    for name, rows, cols, axis in group:
        n = rows * cols // 4 // PACK_W
        blk = g[:, at:at + n, :]
        at += n
        if axis == 1:
            out[name] = blk.reshape(4, rows, cols // 4).transpose(1, 0, 2).reshape(rows, cols)
        else:
            out[name] = blk.reshape(rows, cols)
    return out


def _pack_grads(grads, group):
    parts = []
    for name, rows, cols, axis in group:
        g = grads[name]
        if axis == 1:
            g = g.reshape(rows, 4, cols // 4).transpose(1, 0, 2)
        parts.append(g.reshape(4, -1, PACK_W))
    rows_total = sum(p.shape[1] for p in parts)
    pad = -rows_total % PACK_ALIGN
    if pad:
        parts.append(jnp.zeros((4, pad, PACK_W), F32))
    return jnp.concatenate(parts, axis=1)


def _unpack_shard(s, group):
    out, at = {}, 0
    for name, rows, cols, axis in group:
        n = rows * cols // 4 // PACK_W
        shape = (rows, cols // 4) if axis == 1 else (rows // 4, cols)
        out[name] = s[at:at + n, :].reshape(shape)
        at += n
    return out


def _pack_small(parts):
    flat = jnp.concatenate([p.reshape(-1) for p in parts])
    pad = -flat.shape[0] % 1024
    return jnp.concatenate([flat, jnp.zeros((pad,), F32)]).reshape(-1, 128)


def _ffn_in(tag, h, gain, w_in, side=None):
    t = h.shape[0]
    wide = DFF // 2

    def compute_in(rows, weights, outs):
        hv, w_ref = rows[0][...], weights[0]
        r = lax.rsqrt(jnp.mean(hv * hv, axis=-1, keepdims=True) + EPS)
        a = (hv * r * weights[1][...]).astype(BF16)
        outs[0][...] = a

        def emit(gate, up, cols):
            outs[1][:, cols] = gate.astype(BF16)
            outs[2][:, cols] = up.astype(BF16)
            outs[3][:, cols] = (_silu(gate) * up).astype(BF16)

        for s in range(2):
            emit(_dot(a, w_ref[s, :, 0:FFN_MAIN]), _dot(a, w_ref[2 + s, :, 0:FFN_MAIN]),
                 slice(s * wide, s * wide + FFN_MAIN))
        gate = _dot(a, jnp.concatenate([w_ref[0, :, FFN_MAIN:wide], w_ref[1, :, FFN_MAIN:wide]], axis=1))
        up = _dot(a, jnp.concatenate([w_ref[2, :, FFN_MAIN:wide], w_ref[3, :, FFN_MAIN:wide]], axis=1))
        rest = wide - FFN_MAIN
        for s in range(2):
            emit(gate[:, s * rest:(s + 1) * rest], up[:, s * rest:(s + 1) * rest],
                 slice(s * wide + FFN_MAIN, (s + 1) * wide))

    return _rows_call(tag + "_in", [h], [w_in, gain], [(D, BF16)] + [(DFF, BF16)] * 3, compute_in, min(FFN_TM, t),
                      side=side)


def _ffn_out(tag, act, h, w_out, next_gain, target=None):
    t = h.shape[0]
    tm = min(FFN_TM, t)

    def compute_out(rows, weights, outs):
        hn = rows[1][...] + 0.5 * _dot(rows[0][...], weights[0][...])
        g = weights[1][...]
        r = lax.rsqrt(jnp.mean(hn * hn, axis=-1, keepdims=True) + EPS)
        xh = hn * r
        if target is None:
            outs[0][...] = hn
            outs[1][...] = (xh * g).astype(BF16)
        else:
            err = xh * g - rows[2][...]
            dy = err * (1.0 / D)
            dxh = dy * g
            outs[0][...] = r * (dxh - xh * jnp.mean(dxh * xh, axis=-1, keepdims=True))
            outs[1][...] += jnp.sum(dy * xh, axis=0, keepdims=True)
            outs[2][...] += 0.5 * jnp.sum(jnp.mean(err * err, axis=-1, keepdims=True), axis=0, keepdims=True)

    if target is None:
        return _rows_call(tag + "_out", [act, h], [w_out, next_gain], [(D, F32), (D, BF16)], compute_out, tm)
    return _rows_call(tag + "_out", [act, h, target], [w_out, next_gain], [(D, F32)], compute_out, tm, sums=(D, 128))


class _Reduction:
    def __init__(self, tag, c_arr, k_arr):
        self.tag, self.c_arr, self.k_arr = tag, c_arr, k_arr

    def begin(self, bufs, swapped=None):
        if swapped is None:
            swapped = _SwapHalves(bufs).alone("grad_swap_" + self.tag)
        sums = [_chip_sum("grad_chip_sum_%s%d" % (self.tag, b), gp, got, self.c_arr)
                for b, (gp, got) in enumerate(zip(bufs, swapped))]
        self.sums = [s[0] for s in sums]
        return _ScatterChipSums([s[1] for s in sums])

    def end(self, got):
        mine = [_shard_sum("grad_shard_sum_%s%d" % (self.tag, b), cs, g, self.k_arr)
                for b, (cs, g) in enumerate(zip(self.sums, got))]
        return _join_halves("grad_join_" + self.tag, mine)


def _ffn_bwd(tag, h, gain, w_in, w_out, saved, dout, side, reduction):
    t = h.shape[0]
    tm = min(TM, t)
    n, gate, up, act = saved

    def compute(rows, weights, outs):
        d = rows[0][...].astype(BF16)
        for j in range(DFF // FFN_CHUNK):
            cols = slice(j * FFN_CHUNK, (j + 1) * FFN_CHUNK)
            da = 0.5 * _dot_nt(d, weights[0][cols, :])
            g, u = rows[1][:, cols].astype(F32), rows[2][:, cols].astype(F32)
            s = _sig(g)
            silu = g * s
            outs[0][:, cols] = (da * u * (s + silu * (1.0 - s))).astype(BF16)
            outs[1][:, cols] = (da * silu).astype(BF16)

    dgate, dup, *side_out = _rows_call(tag + "_dact", [dout, gate, up], [w_out], [(DFF, BF16)] * 2, compute,
                                       min(FFN_TM, t), side=side)
    dw_in = _mm_tn(tag + "_dw_gate", n, dgate, tm=D, tn=DFF // 2, stacked=(4, 0))
    dw_in = _mm_tn(tag + "_dw_up", n, dup, tm=D, tn=DFF // 2, stacked=(4, 2), into=dw_in)
    dw_out, dw_in_swapped = _mm_tn(tag + "_dw_out", act, dout, scale=0.5, tm=DFF // 2, tn=D, side=_SwapHalves([dw_in]))
    dw_out = dw_out.reshape(4, DFF // 4, D)
    (dw_out_swapped,) = _SwapHalves([dw_out]).alone("grad_swap_" + tag)
    sending = reduction.begin([dw_in, dw_out], [dw_in_swapped, dw_out_swapped])

    def compute_dn(rows, weights, outs):
        w_ref = weights[0]
        wide = DFF // 2
        dn = jnp.zeros((rows[0].shape[0], D), F32)
        for s in range(2):
            cols = slice(s * wide, s * wide + FFN_MAIN)
            dn = (dn + _dot_nt(rows[0][:, cols], w_ref[s, :, 0:FFN_MAIN])
                  + _dot_nt(rows[1][:, cols], w_ref[2 + s, :, 0:FFN_MAIN]))
        for r, first in ((0, 0), (1, 2)):
            x = jnp.concatenate([rows[r][:, FFN_MAIN:wide], rows[r][:, wide + FFN_MAIN:2 * wide]], axis=1)
            wt = jnp.concatenate([w_ref[first, :, FFN_MAIN:wide], w_ref[first + 1, :, FFN_MAIN:wide]], axis=1)
            dn = dn + _dot_nt(x, wt)
        dx, dg = _rms_bwd_vals(rows[2][...], weights[1][...], dn)
        outs[0][...] = rows[3][...] + dx
        outs[1][...] += jnp.sum(dg, axis=0, keepdims=True)

    dh, dgain, *got = _rows_call(tag + "_dn", [dgate, dup, h, dout], [w_in, gain], [(D, F32)], compute_dn,
                                 min(FFN_TM, t), side=sending, sums=(D,), vmem_mb=58)
    return dh, dgain, side_out, got


def kernel(x, positions, ffn1_norm, ffn1_w_in, ffn1_w_out, mix_norm, w_in, hg_lb_table, hg_out_norm, w_hg_branch, mla_q_lora_norm, w_q_up, mla_kv_lora_norm, w_kv_up, q_head_norm, k_head_norm, w_mla_branch, w_merge, b_merge, w_out, ffn2_norm, ffn2_w_in, ffn2_w_out, final_norm, loss_target, m_ffn1_norm, m_ffn1_w_in, m_ffn1_w_out, m_mix_norm, m_w_in, m_hg_lb_table, m_hg_out_norm, m_w_hg_branch, m_mla_q_lora_norm, m_w_q_up, m_mla_kv_lora_norm, m_w_kv_up, m_q_head_norm, m_k_head_norm, m_w_mla_branch, m_w_merge, m_b_merge, m_w_out, m_ffn2_norm, m_ffn2_w_in, m_ffn2_w_out, m_final_norm, v_ffn1_norm, v_ffn1_w_in, v_ffn1_w_out, v_mix_norm, v_w_in, v_hg_lb_table, v_hg_out_norm, v_w_hg_branch, v_mla_q_lora_norm, v_w_q_up, v_mla_kv_lora_norm, v_w_kv_up, v_q_head_norm, v_k_head_norm, v_w_mla_branch, v_w_merge, v_b_merge, v_w_out, v_ffn2_norm, v_ffn2_w_in, v_ffn2_w_out, v_final_norm):
    a = dict(locals())
    w = {n: a[n] for n in WEIGHT_ORDER}
    mom = {n: a["m_" + n] for n in WEIGHT_ORDER}
    var = {n: a["v_" + n] for n in WEIGHT_ORDER}
    t = x.shape[1]
    tm = min(TM, t)
    xt = x.reshape(t, D)
    target = loss_target.reshape(t, D)
    pos = positions.reshape(t, 1)
    x_i, y_i, c_i = _place()
    k_idx = (2 * x_i + y_i).astype(jnp.int32)
    c_arr = c_i.astype(jnp.int32).reshape(1)
    k_arr = jnp.stack([k_idx, c_i.astype(jnp.int32)])

    group_mid = _group(("w_in", "w_hg_branch", "w_q_up", "w_kv_up", "w_mla_branch", "w_merge", "w_out"))
    use_early = _group(("ffn1_w_out", "w_in", "w_hg_branch", "w_q_up", "w_kv_up"))
    use_late = _group(("w_mla_branch", "w_merge", "w_out", "ffn2_w_out"))
    gather_first = _GatherWeights([w["ffn1_w_in"][0].astype(BF16)])
    gather_early = _GatherWeights([_pack([w[e[0]][0] for e in use_early], BF16)])
    gather_late = _GatherWeights([_pack([w[e[0]][0] for e in use_late], BF16), w["ffn2_w_in"][0].astype(BF16)])
    (ffn1_w_in_g,) = gather_first.gathered(gather_first.alone("gather_first"), k_idx)
    n1, gate1, up1, act1, got = _ffn_in("ffn1", xt, w["ffn1_norm"], ffn1_w_in_g, gather_early)
    full = _unpack_full(gather_early.gathered([got], k_idx)[0], use_early)
    h1, u = _ffn_out("ffn1", act1, xt, full["ffn1_w_out"], w["mix_norm"])
    ffn1_saved = (n1, gate1, up1, act1)
    w_in_full = full["w_in"]
    w_in_hg = w_in_full[:, :4 * D]
    w_in_mla = jnp.pad(w_in_full[:, 4 * D:], ((0, 0), (0, MLA_COLS - (4800 - 4 * D))))
    w_q_pad = jnp.pad(full["w_q_up"].reshape(Q_LORA, HEADS, QK), ((0, 0), (0, 0), (0, QKP - QK))).reshape(Q_LORA, HEADS * QKP)
    w_kv = full["w_kv_up"]
    gq = jnp.pad(w["q_head_norm"], ((0, 0), (0, QKP - QK)))
    gk = jnp.pad(w["k_head_norm"], ((0, 0), (0, QKP - QK)))

    ident = lambda accs, ex: (accs[0],)
    def in_hg(rows, weights, outs):
        a = rows[0][...]
        for j in range(4 * D // 512):
            cols = slice(j * 512, (j + 1) * 512)
            outs[0][:, cols] = _dot(a, weights[0][:, cols]).astype(BF16)

    (p_hg,) = _rows_call("in_hg", [u], [w_in_hg], [(4 * D, BF16)], in_hg, min(FFN_TM, t))
    p_mla, cqn, ckvn = _in_mla(u, w_in_mla, w["mla_q_lora_norm"], w["mla_kv_lora_norm"])
    o_raw, hg_o, states = _hgrn_fwd(p_hg, w["hg_lb_table"], w["hg_out_norm"])
    (y_hg,) = _mm("hg_branch", [_a_spec(hg_o, tm)], [_b_nn(full["w_hg_branch"], 512)], [(0, 0)], ident, [], [BF16], t, D, tm, 512)
    (qf,) = _mm("q_up", [_a_spec(cqn, tm)], [_b_nn(w_q_pad, 512)], [(0, 0)], ident, [], [F32], t, HEADS * QKP, tm, 512)
    (kvf,) = _mm("kv_up", [_a_spec(ckvn, tm)], [_b_nn(w_kv, 512)], [(0, 0)], ident, [], [F32], t, HEADS * QKP, tm, 512)
    cos, sin = _rope_tables(pos)
    qh, kh, vh = _mla_prep_fwd(qf, kvf, p_mla, cos, sin, gq, gk)
    o_mla, lse, *got = _flash_fwd(qh, kh, vh, side=gather_late)
    late, ffn2_w_in_g = gather_late.gathered(got, k_idx)
    full.update(_unpack_full(late, use_late))
    (y_mla,) = _mm("mla_branch", [_a_spec(o_mla, tm)], [_b_nn(full["w_mla_branch"], 512)], [(0, 0)], ident, [], [BF16], t, D, tm, 512)

    def merge_epi(accs, ex):
        g_hg = _sig(accs[0] + ex[2])
        g_mla = _sig(accs[1] + ex[3])
        return g_hg * ex[0].astype(F32) + g_mla * ex[1].astype(F32), g_hg, g_mla

    w_merge_f = full["w_merge"]
    mix, g_hg, g_mla = _mm(
        "merge", [_a_spec(u, tm)], [_b_nn(w_merge_f, 512), _b_nn(w_merge_f, 512, D // 512)], [(0, 0), (0, 1)], merge_epi,
        [_e_tile(y_hg, tm, 512), _e_tile(y_mla, tm, 512), _e_row(w["b_merge"], 512), _e_row(w["b_merge"], 512, D // 512)],
        [BF16, BF16, BF16], t, D, tm, 512)
    (h2,) = _mm("out_proj", [_a_spec(mix, tm)], [_b_nn(full["w_out"], 512)], [(0, 0)],
                lambda accs, ex: (ex[0] + accs[0],), [_e_tile(h1, tm, 512)], [F32], t, D, tm, 512)
    ffn2_saved = _ffn_in("ffn2", h2, w["ffn2_norm"], ffn2_w_in_g)
    dh3, d_final_norm, loss_part = _ffn_out("ffn2", ffn2_saved[3], h2, full["ffn2_w_out"], w["final_norm"], target=target)

    grads, small = {}, {}
    small["final_norm"] = d_final_norm
    reduce_last = _Reduction("last", c_arr, k_arr)
    reduce_mid = _Reduction("mid", c_arr, k_arr)
    reduce_first = _Reduction("first", c_arr, k_arr)
    dh2, small["ffn2_norm"], _, got_last = _ffn_bwd(
        "ffn2", h2, w["ffn2_norm"], ffn2_w_in_g, full["ffn2_w_out"], ffn2_saved, dh3, None, reduce_last)

    def dmix_epi(accs, ex):
        dm = accs[0]
        ghg, gml, yhg, yml = [e.astype(F32) for e in ex]
        return dm * ghg, dm * gml, dm * yhg * ghg * (1.0 - ghg), dm * yml * gml * (1.0 - gml)

    dy_hg, dy_mla, dpre_hg, dpre_mla = _mm(
        "d_mix", [_a_spec(dh2, tm)], [_b_nt(full["w_out"], 512)], [(0, 0)], dmix_epi,
        [_e_tile(g_hg, tm, 512), _e_tile(g_mla, tm, 512), _e_tile(y_hg, tm, 512), _e_tile(y_mla, tm, 512)],
        [BF16, BF16, BF16, BF16], t, D, tm, 512, trans_b=True)
    grads["w_out"] = _mm_tn("dw_out", mix, dh2)
    small["b_merge"] = jnp.concatenate([_colsum("db_hg", dpre_hg), _colsum("db_mla", dpre_mla)], axis=1)
    grads["w_merge"] = jnp.concatenate([_mm_tn("dw_merge_hg", u, dpre_hg), _mm_tn("dw_merge_mla", u, dpre_mla)], axis=1)
    grads["w_hg_branch"] = _mm_tn("dw_hg_branch", hg_o, dy_hg)
    grads["w_mla_branch"] = _mm_tn("dw_mla_branch", o_mla, dy_mla)
    (dho,) = _mm("d_hg_o", [_a_spec(dy_hg, tm)], [_b_nt(full["w_hg_branch"], 512)], [(0, 0)], ident, [], [BF16], t, D, tm, 512, trans_b=True)
    (do_mla,) = _mm("d_o_mla", [_a_spec(dy_mla, tm)], [_b_nt(full["w_mla_branch"], 512)], [(0, 0)], ident, [], [BF16], t, D, tm, 512, trans_b=True)

    dq_raw, df_raw, di_raw, dg_raw, small["hg_lb_table"], small["hg_out_norm"] = _hgrn_bwd(
        p_hg, w["hg_lb_table"], w["hg_out_norm"], o_raw, states, dho)
    dp_hg = [dq_raw, df_raw, di_raw, dg_raw]

    dqh, dkh, dvh = _flash_bwd(qh, kh, vh, lse, _attn_do(do_mla, o_mla))
    dqf, dkvf, dkpe, dgq, dgk = _mla_prep_bwd(qf, kvf, p_mla, cos, sin, gq, gk, dqh, dkh, dvh)
    small["q_head_norm"] = dgq[:, :QK]
    small["k_head_norm"] = dgk[:, :QK]
    dwq_pad = _mm_tn("dw_q_up", cqn, dqf, tm=Q_LORA, tn=1024)
    grads["w_q_up"] = dwq_pad.reshape(Q_LORA, HEADS, QKP)[:, :, :QK].reshape(Q_LORA, HEADS * QK)
    grads["w_kv_up"] = _mm_tn("dw_kv_up", ckvn, dkvf, tm=KV_LORA, tn=1024)
    (dcqn,) = _mm("d_cq", [_a_spec(dqf, tm)], [_b_nt(w_q_pad, Q_LORA)], [(0, 0)], ident, [], [F32], t, Q_LORA, tm, Q_LORA, trans_b=True)
    (dckvn,) = _mm("d_ckv", [_a_spec(dkvf, tm)], [_b_nt(w_kv, KV_LORA)], [(0, 0)], ident, [], [F32], t, KV_LORA, tm, KV_LORA, trans_b=True)
    dp_mla, small["mla_q_lora_norm"], small["mla_kv_lora_norm"] = _lora_norm_bwd(
        p_mla, w["mla_q_lora_norm"], w["mla_kv_lora_norm"], dcqn, dckvn, dkpe)

    dw_in_hg = [_mm_tn("dw_in_hg%d" % k, u, dp_hg[k]) for k in range(4)]
    dw_in_mla = _mm_tn("dw_in_mla", u, dp_mla, tn=MLA_COLS)
    grads["w_in"] = jnp.concatenate(dw_in_hg + [dw_in_mla[:, :4800 - 4 * D]], axis=1)
    tm_du = min(TM // 2, t)
    du, *got_mid = _mm(
        "d_u",
        [_a_spec(dpre_hg, tm_du), _a_spec(dpre_mla, tm_du)] + [_a_spec(d, tm_du) for d in dp_hg] + [_a_spec(dp_mla, tm_du)],
        [_b_nt(w_merge_f, 512, D, 0), _b_nt(w_merge_f, 512, D, 1)]
        + [_b_nt(w_in_hg, 512, D, k) for k in range(4)] + [_b_nt(w_in_mla, 512)],
        [(k, k) for k in range(7)],
        lambda accs, ex: (functools.reduce(lambda p, q: p + q, accs),), [], [F32], t, D, tm_du, 512, trans_b=True,
        side=reduce_mid.begin([_pack_grads(grads, group_mid)]))
    dh1, small["mix_norm"] = _rms_bwd("mix_dnorm", h1, w["mix_norm"], du, dh2)
    dx, small["ffn1_norm"], _, got_first = _ffn_bwd(
        "ffn1", xt, w["ffn1_norm"], ffn1_w_in_g, full["ffn1_w_out"], ffn1_saved, dh1, None, reduce_first)

    g_shard = _unpack_shard(reduce_mid.end(got_mid)[0], group_mid)
    g_shard["ffn2_w_in"], g_shard["ffn2_w_out"] = reduce_last.end(got_last)
    g_shard["ffn1_w_in"], g_shard["ffn1_w_out"] = reduce_first.end(got_first)
    small_sum = _all_reduce_small(_pack_small([small[n] for n, _ in SMALL] + [loss_part])).reshape(-1)
    g_small, at = {}, 0
    for n, shape in SMALL:
        size = shape[0] * shape[1]
        g_small[n] = small_sum[at:at + size].reshape(shape)
        at += size
    loss = small_sum[at]

    g_out, d_out, m_out, v_out = {}, {}, {}, {}
    for n in WEIGHT_ORDER:
        shape = w[n].shape
        g = g_shard[n] if n in g_shard else g_small[n]
        two = g.shape
        d_, m_, v_ = _adamw("adamw_" + n, w[n].reshape(two), g, mom[n].reshape(two), var[n].reshape(two))
        g_out[n], d_out[n], m_out[n], v_out[n] = g.reshape(shape), d_.reshape(shape), m_.reshape(shape), v_.reshape(shape)

    return (loss, dx.reshape(x.shape), *[g_out[n] for n in WEIGHT_ORDER], *[d_out[n] for n in WEIGHT_ORDER],
            *[m_out[n] for n in WEIGHT_ORDER], *[v_out[n] for n in WEIGHT_ORDER])
```

```python
import functools

import numpy as np
import jax
import jax.numpy as jnp
from jax import lax
from jax.experimental import pallas as pl
from jax.experimental.pallas import tpu as pltpu

F32 = jnp.float32
BF16 = jnp.bfloat16
MESH = pl.DeviceIdType.MESH

D = 1024
DFF = 2816
HEADS = 8
HK = 128
CHUNK = 64
ROPE = 64
QK = 192
QKP = 256
Q_LORA = 384
KV_LORA = 256
MLA_COLS = 768
EPS = 1e-6
ROPE_THETA = 10000.0
SCALE = QK ** -0.5
LOG2E = 1.4426950408889634
LN2 = 0.6931471805599453
NEG = -1e30
EXP_CLAMP = 80.0

ADAM_LR = 0.001
ADAM_B1 = 0.9
ADAM_B2 = 0.999
ADAM_EPS = 1e-08
ADAM_WD = 0.01
ADAM_STEP = 10

PACK_W = 1024
ADD_ROWS = 352
PACK_ALIGN = 2 * ADD_ROWS

TM = 1024
FFN_TM = 512
FFN_CHUNK = 256
FFN_MAIN = 1280
TQ = 2048
SUBQ = 256
HG_BT = 512
HG_HPB = 8
TT = 2048
ROW_TM = 512
PREP_TM = 256

VMEM_MB = 48

BIG = (
    ("ffn1_w_in", D, 2 * DFF, 1),
    ("ffn1_w_out", DFF, D, 0),
    ("w_in", D, 4800, 1),
    ("w_hg_branch", D, D, 0),
    ("w_q_up", Q_LORA, HEADS * QK, 1),
    ("w_kv_up", KV_LORA, HEADS * 2 * HK, 1),
    ("w_mla_branch", D, D, 0),
    ("w_merge", D, 2 * D, 1),
    ("w_out", D, D, 0),
    ("ffn2_w_in", D, 2 * DFF, 1),
    ("ffn2_w_out", DFF, D, 0),
)
SMALL = (
    ("ffn1_norm", (1, D)),
    ("mix_norm", (1, D)),
    ("hg_lb_table", (2, D)),
    ("hg_out_norm", (1, HK)),
    ("mla_q_lora_norm", (1, Q_LORA)),
    ("mla_kv_lora_norm", (1, KV_LORA)),
    ("q_head_norm", (1, QK)),
    ("k_head_norm", (1, QK)),
    ("b_merge", (1, 2 * D)),
    ("ffn2_norm", (1, D)),
    ("final_norm", (1, D)),
)
WEIGHT_ORDER = ("ffn1_norm", "ffn1_w_in", "ffn1_w_out", "mix_norm", "w_in", "hg_lb_table", "hg_out_norm",
                "w_hg_branch", "mla_q_lora_norm", "w_q_up", "mla_kv_lora_norm", "w_kv_up", "q_head_norm",
                "k_head_norm", "w_mla_branch", "w_merge", "b_merge", "w_out", "ffn2_norm", "ffn2_w_in",
                "ffn2_w_out", "final_norm")


def _call(body, **kw):
    return pl.pallas_call(body, **kw)


def _cp(vmem_mb=VMEM_MB):
    return pltpu.CompilerParams(vmem_limit_bytes=vmem_mb << 20)


def _dot(a, b):
    return lax.dot_general(a, b, (((1,), (0,)), ((), ())), preferred_element_type=F32)


def _dot_nt(a, b):
    return lax.dot_general(a, b, (((1,), (1,)), ((), ())), preferred_element_type=F32)


def _dot_tn(a, b):
    return lax.dot_general(a, b, (((0,), (0,)), ((), ())), preferred_element_type=F32)


def _sig(x):
    return jax.nn.sigmoid(x)


def _silu(x):
    return x * _sig(x)


def _dsilu(x):
    s = _sig(x)
    return s * (1.0 + x * (1.0 - s))


def _a_spec(arr, tm, kblk=None, kidx=0):
    kb = arr.shape[1] if kblk is None else kblk
    return arr, pl.BlockSpec((tm, kb), lambda i, j, kidx=kidx: (i, kidx)), slice(kidx * kb, (kidx + 1) * kb)


def _b_nn(arr, tn, off=0):
    return arr, pl.BlockSpec((arr.shape[0], tn), lambda i, j, off=off: (0, j + off)), ("cols", off)


def _b_nt(arr, tn, kblk=None, kidx=0):
    kb = arr.shape[1] if kblk is None else kblk
    return arr, pl.BlockSpec((tn, kb), lambda i, j, kidx=kidx: (j, kidx)), ("rows", slice(kidx * kb, (kidx + 1) * kb))


def _e_tile(arr, tm, tn, off=0):
    return arr, pl.BlockSpec((tm, tn), lambda i, j, off=off: (i, j + off)), ("tile", off)


def _e_row(arr, tn, off=0):
    return arr, pl.BlockSpec((1, tn), lambda i, j, off=off: (0, j + off)), ("row", off)


def _mm_resident(name, As, Bs, dots, epi, extras, out_dtypes, m, n, tn):
    def unique(arrays):
        seen = []
        for a in arrays:
            if not any(a is s for s in seen):
                seen.append(a)
        return seen

    rows = unique([a for a, _, _ in As] + [e for e, _, where in extras if where[0] == "tile"])
    weights = unique([b for b, _, _ in Bs] + [e for e, _, where in extras if where[0] == "row"])

    def ref_of(arr, row_refs, weight_refs):
        for r, ref in zip(rows, row_refs):
            if r is arr:
                return ref
        for wt, ref in zip(weights, weight_refs):
            if wt is arr:
                return ref

    def compute(row_refs, weight_refs, out_refs):
        a_vals = [ref_of(a, row_refs, weight_refs)[:, ks].astype(BF16) for a, _, ks in As]
        for j in range(n // tn):
            accs = []
            for ai, bi in dots:
                b, _, where = Bs[bi]
                b_ref = ref_of(b, row_refs, weight_refs)
                if where[0] == "cols":
                    accs.append(_dot(a_vals[ai], b_ref[:, (j + where[1]) * tn:(j + where[1] + 1) * tn]))
                else:
                    accs.append(_dot_nt(a_vals[ai], b_ref[j * tn:(j + 1) * tn, where[1]]))
            ex = [ref_of(e, row_refs, weight_refs)[:, (j + where[1]) * tn:(j + where[1] + 1) * tn]
                  for e, _, where in extras]
            for o_ref, o in zip(out_refs, epi(accs, ex)):
                o_ref[:, j * tn:(j + 1) * tn] = o.astype(o_ref.dtype)

    return _rows_call(name, rows, weights, [(n, dt) for dt in out_dtypes], compute, min(FFN_TM, m))


def _mm(name, As, Bs, dots, epi, extras, out_dtypes, m, n, tm, tn, trans_b=False, side=None):
    if side is None:
        return _mm_resident(name, As, Bs, dots, epi, extras, out_dtypes, m, n, tn)
    na, nb, ne, no = len(As), len(Bs), len(extras), len(out_dtypes)
    ni, nj = m // tm, n // tn
    s_in = len(side.inputs) if side else 0
    s_out = len(side.out_shapes) if side else 0

    def body(*refs):
        a_refs = refs[:na]
        b_refs = refs[na:na + nb]
        e_refs = refs[na + nb:na + nb + ne]
        at = na + nb + ne
        side_refs = refs[at:at + s_in]
        o_refs = refs[at + s_in:at + s_in + no]
        side_refs = list(side_refs) + list(refs[at + s_in + no:])
        if side:
            i, j = pl.program_id(0), pl.program_id(1)

            @pl.when(jnp.logical_and(i == 0, j == 0))
            def _():
                side.start(*side_refs)

        a_vals = [r[...].astype(BF16) for r in a_refs]
        accs = []
        for ai, bi in dots:
            b = b_refs[bi][...]
            accs.append(_dot_nt(a_vals[ai], b) if trans_b else _dot(a_vals[ai], b))
        outs = epi(accs, [r[...] for r in e_refs])
        for o_ref, o in zip(o_refs, outs):
            o_ref[...] = o.astype(o_ref.dtype)
        if side:
            @pl.when(jnp.logical_and(i == ni - 1, j == nj - 1))
            def _():
                side.finish(*side_refs)

    ops = list(As) + list(Bs) + list(extras)
    anywhere = pl.BlockSpec(memory_space=pl.ANY)
    res = _call(
        body, name=name,
        grid=(ni, nj),
        in_specs=[op[1] for op in ops] + [anywhere] * s_in,
        out_specs=[pl.BlockSpec((tm, tn), lambda i, j: (i, j)) for _ in out_dtypes] + [anywhere] * s_out,
        out_shape=[jax.ShapeDtypeStruct((m, n), dt) for dt in out_dtypes] + (list(side.out_shapes) if side else []),
        scratch_shapes=list(side.scratch) if side else [],
        compiler_params=_cp(),
    )(*[op[0] for op in ops], *(side.inputs if side else []))
    return res


def _rows_call(name, rows, weights, outs, compute, tm, side=None, sums=(), vmem_mb=VMEM_MB):
    t = rows[0].shape[0]
    nr, nw, no = len(rows), len(weights), len(outs) + len(sums)
    ni = t // tm
    s_in = len(side.inputs) if side else 0
    s_out = len(side.out_shapes) if side else 0

    def body(*refs):
        at = nr + nw
        side_refs = list(refs[at:at + s_in]) + list(refs[at + s_in + no:])
        if side:
            @pl.when(pl.program_id(0) == 0)
            def _():
                side.start(*side_refs)

        out_refs = refs[at + s_in:at + s_in + no]
        if sums:
            @pl.when(pl.program_id(0) == 0)
            def _():
                for r in out_refs[len(outs):]:
                    r[...] = jnp.zeros_like(r)

        compute(refs[:nr], refs[nr:at], out_refs)
        if side:
            @pl.when(pl.program_id(0) == ni - 1)
            def _():
                side.finish(*side_refs)

    anywhere = pl.BlockSpec(memory_space=pl.ANY)
    return _call(
        body, name=name, grid=(ni,),
        in_specs=[pl.BlockSpec((tm, r.shape[1]), lambda i: (i, 0)) for r in rows]
        + [pl.BlockSpec(wt.shape, lambda i, nd=wt.ndim: (0,) * nd) for wt in weights] + [anywhere] * s_in,
        out_specs=[pl.BlockSpec((tm, width), lambda i: (i, 0)) for width, _ in outs]
        + [pl.BlockSpec((1, width), lambda i: (0, 0)) for width in sums] + [anywhere] * s_out,
        out_shape=[jax.ShapeDtypeStruct((t, width), dt) for width, dt in outs]
        + [jax.ShapeDtypeStruct((1, width), F32) for width in sums] + (list(side.out_shapes) if side else []),
        scratch_shapes=list(side.scratch) if side else [],
        compiler_params=_cp(vmem_mb),
    )(*rows, *weights, *(side.inputs if side else []))


def _mm_tn(name, a, b, scale=1.0, tm=1024, tn=1024, stacked=None, into=None, side=None):
    t, m = a.shape
    n = b.shape[1]
    tm, tn, tt = min(tm, m), min(tn, n), min(TT, t)
    ni, nj, nk = m // tm, n // tn, t // tt
    extra_in = [into] if into is not None else list(side.inputs) if side else []
    s_out = len(side.out_shapes) if side else 0

    def body(a_ref, b_ref, *rest):
        o_ref = rest[len(extra_in)]
        i, j, k = pl.program_id(0), pl.program_id(1), pl.program_id(2)
        if side:
            side_refs = list(rest[:len(extra_in)]) + list(rest[len(extra_in) + 1:])

            @pl.when(jnp.logical_and(jnp.logical_and(i == 0, j == 0), k == 0))
            def _():
                side.start(*side_refs)

        @pl.when(k == 0)
        def _():
            o_ref[...] = jnp.zeros_like(o_ref)

        o_ref[...] += _dot_tn(a_ref[...].astype(BF16), b_ref[...].astype(BF16))
        if scale != 1.0:
            @pl.when(k == nk - 1)
            def _():
                o_ref[...] = o_ref[...] * scale
        if side:
            @pl.when(jnp.logical_and(jnp.logical_and(i == ni - 1, j == nj - 1), k == nk - 1))
            def _():
                side.finish(*side_refs)

    anywhere = pl.BlockSpec(memory_space=pl.ANY)
    product = jax.ShapeDtypeStruct((stacked[0], m, tn) if stacked else (m, n), F32)
    res = _call(
        body, name=name,
        grid=(ni, nj, nk),
        in_specs=[pl.BlockSpec((tt, tm), lambda i, j, k: (k, i)), pl.BlockSpec((tt, tn), lambda i, j, k: (k, j))]
        + [anywhere] * len(extra_in),
        out_specs=[pl.BlockSpec((None, tm, tn), lambda i, j, k: (stacked[1] + j, i, 0)) if stacked
                   else pl.BlockSpec((tm, tn), lambda i, j, k: (i, j))] + [anywhere] * s_out,
        out_shape=[product] + (list(side.out_shapes) if side else []),
        input_output_aliases={2: 0} if into is not None else {},
        scratch_shapes=list(side.scratch) if side else [],
        compiler_params=_cp(),
    )(a, b, *extra_in)
    return res if side else res[0]


def _rms_bwd_vals(xv, g, dn):
    r = lax.rsqrt(jnp.mean(xv * xv, axis=-1, keepdims=True) + EPS)
    xh = xv * r
    dxh = dn * g
    c = jnp.mean(dxh * xh, axis=-1, keepdims=True)
    return r * (dxh - xh * c), dn * xh


def _rms_bwd(name, x, gain, dn, dres):
    t, d = x.shape
    tm = min(ROW_TM, t)

    def body(x_ref, g_ref, dn_ref, dr_ref, dx_ref, dg_ref):
        @pl.when(pl.program_id(0) == 0)
        def _():
            dg_ref[...] = jnp.zeros_like(dg_ref)

        dx, dg = _rms_bwd_vals(x_ref[...], g_ref[...], dn_ref[...].astype(F32))
        dx_ref[...] = dr_ref[...] + dx
        dg_ref[...] += jnp.sum(dg, axis=0, keepdims=True)

    row = pl.BlockSpec((tm, d), lambda i: (i, 0))
    one = pl.BlockSpec((1, d), lambda i: (0, 0))
    return _call(
        body, name=name, grid=(t // tm,),
        in_specs=[row, one, row, row],
        out_specs=[row, one],
        out_shape=[jax.ShapeDtypeStruct((t, d), F32), jax.ShapeDtypeStruct((1, d), F32)],
        compiler_params=_cp(),
    )(x, gain, dn, dres)


def _colsum(name, x):
    t, n = x.shape
    tm = min(TM, t)

    def body(x_ref, o_ref):
        @pl.when(pl.program_id(0) == 0)
        def _():
            o_ref[...] = jnp.zeros_like(o_ref)

        o_ref[...] += jnp.sum(x_ref[...].astype(F32), axis=0, keepdims=True)

    return _call(
        body, name=name, grid=(t // tm,),
        in_specs=[pl.BlockSpec((tm, n), lambda i: (i, 0))],
        out_specs=pl.BlockSpec((1, n), lambda i: (0, 0)),
        out_shape=jax.ShapeDtypeStruct((1, n), F32),
        compiler_params=_cp(),
    )(x)


def _in_mla(u, w_in_mla, gq, gkv):
    t = u.shape[0]

    def compute(rows, weights, outs):
        p = _dot(rows[0][...], weights[0][...])
        outs[0][...] = p
        cq = p[:, 0:Q_LORA]
        ckv = p[:, Q_LORA:Q_LORA + KV_LORA]
        rq = lax.rsqrt(jnp.mean(cq * cq, axis=-1, keepdims=True) + EPS)
        rkv = lax.rsqrt(jnp.mean(ckv * ckv, axis=-1, keepdims=True) + EPS)
        outs[1][...] = (cq * rq * weights[1][...]).astype(BF16)
        outs[2][...] = (ckv * rkv * weights[2][...]).astype(BF16)

    return _rows_call("in_mla", [u], [w_in_mla, gq, gkv], [(MLA_COLS, F32), (Q_LORA, BF16), (KV_LORA, BF16)], compute,
                      min(FFN_TM, t))


def _lora_norm_bwd(p_mla, gq, gkv, dcqn, dckvn, dkpe):
    t = p_mla.shape[0]
    tm = min(ROW_TM, t)

    def body(p_ref, gq_ref, gkv_ref, dq_ref, dkv_ref, dkpe_ref, dp_ref, dgq_ref, dgkv_ref):
        @pl.when(pl.program_id(0) == 0)
        def _():
            dgq_ref[...] = jnp.zeros_like(dgq_ref)
            dgkv_ref[...] = jnp.zeros_like(dgkv_ref)

        dcq, dgq = _rms_bwd_vals(p_ref[:, 0:Q_LORA], gq_ref[...], dq_ref[...])
        dckv, dgkv = _rms_bwd_vals(p_ref[:, Q_LORA:Q_LORA + KV_LORA], gkv_ref[...], dkv_ref[...])
        dp_ref[:, 0:Q_LORA] = dcq.astype(BF16)
        dp_ref[:, Q_LORA:Q_LORA + KV_LORA] = dckv.astype(BF16)
        dp_ref[:, Q_LORA + KV_LORA:MLA_COLS] = dkpe_ref[...].astype(BF16)
        dgq_ref[...] += jnp.sum(dgq, axis=0, keepdims=True)
        dgkv_ref[...] += jnp.sum(dgkv, axis=0, keepdims=True)

    return _call(
        body, name="lora_norm_bwd", grid=(t // tm,),
        in_specs=[pl.BlockSpec((tm, MLA_COLS), lambda i: (i, 0)),
                  pl.BlockSpec((1, Q_LORA), lambda i: (0, 0)), pl.BlockSpec((1, KV_LORA), lambda i: (0, 0)),
                  pl.BlockSpec((tm, Q_LORA), lambda i: (i, 0)), pl.BlockSpec((tm, KV_LORA), lambda i: (i, 0)),
                  pl.BlockSpec((tm, HK), lambda i: (i, 0))],
        out_specs=[pl.BlockSpec((tm, MLA_COLS), lambda i: (i, 0)),
                   pl.BlockSpec((1, Q_LORA), lambda i: (0, 0)), pl.BlockSpec((1, KV_LORA), lambda i: (0, 0))],
        out_shape=[jax.ShapeDtypeStruct((t, MLA_COLS), BF16), jax.ShapeDtypeStruct((1, Q_LORA), F32),
                   jax.ShapeDtypeStruct((1, KV_LORA), F32)],
        compiler_params=_cp(),
    )(p_mla, gq, gkv, dcqn, dckvn, dkpe)


def _cumsum_rows(x, row):
    for s in (1, 2, 4, 8, 16, 32):
        x = x + jnp.where(row >= s, pltpu.roll(x, s, 0), 0.0)
    return x


def _rcumsum_rows(x, row):
    for s in (1, 2, 4, 8, 16, 32):
        x = x + jnp.where(row < CHUNK - s, pltpu.roll(x, CHUNK - s, 0), 0.0)
    return x


def _hg_gates(qr, z, lb, row):
    q = _silu(qr)
    sg = _sig(z)
    f = lb + (1.0 - lb) * sg
    lf = jnp.log(f)
    k = (1.0 - lb) * (1.0 - sg)
    cum = _cumsum_rows(lf, row)
    mid = jnp.sum(jnp.where(row < CHUNK // 2, lf, 0.0), axis=0, keepdims=True)
    last = jnp.sum(lf, axis=0, keepdims=True)
    e_q = jnp.exp(jnp.minimum(cum - mid, EXP_CLAMP))
    e_k = jnp.exp(jnp.minimum(mid - cum, EXP_CLAMP))
    e_a = jnp.exp(cum)
    e_l = jnp.exp(last - cum)
    return q, sg, f, k, last, e_q, e_k, e_a, e_l


def _hgrn_fwd(p_hg, tab, gain):
    t = p_hg.shape[0]
    bt = min(HG_BT, t)
    nb, nc = t // bt, bt // CHUNK

    hpb = HG_HPB
    wide = hpb * HK

    def body(q_ref, f_ref, i_ref, g_ref, tab_ref, gain_ref, o_ref, ho_ref, st_ref, state):
        @pl.when(pl.program_id(1) == 0)
        def _():
            state[...] = jnp.zeros_like(state)

        row = lax.broadcasted_iota(jnp.int32, (CHUNK, HK), 0)
        tril = lax.broadcasted_iota(jnp.int32, (CHUNK, CHUNK), 0) >= lax.broadcasted_iota(jnp.int32, (CHUNK, CHUNK), 1)
        gain_v = gain_ref[...]

        def chunk(c, carry):
            sl = pl.ds(pl.multiple_of(c * CHUNK, CHUNK), CHUNK)
            for hh in range(hpb):
                ln = slice(hh * HK, (hh + 1) * HK)
                lb = _sig(tab_ref[0:1, ln] - tab_ref[1:2, ln])
                v = i_ref[sl, ln].astype(BF16)
                q, _, _, k, last, e_q, e_k, e_a, e_l = _hg_gates(
                    q_ref[sl, ln].astype(F32), f_ref[sl, ln].astype(F32), lb, row)
                st = state[hh]
                st_ref[hh, c] = st
                p = jnp.where(tril, _dot_nt((q * e_q).astype(BF16), (k * e_k).astype(BF16)), 0.0)
                o = _dot(p.astype(BF16), v) + _dot_nt((q * e_a).astype(BF16), st.astype(BF16))
                state[hh] = jnp.exp(last) * st + _dot_tn(v, (k * e_l).astype(BF16))
                o_ref[sl, ln] = o
                r = lax.rsqrt(jnp.mean(o * o, axis=-1, keepdims=True) + EPS)
                ho_ref[sl, ln] = (o * r * gain_v * _silu(g_ref[sl, ln].astype(F32))).astype(BF16)
            return carry

        lax.fori_loop(0, nc, chunk, 0)

    def col(k):
        return pl.BlockSpec((bt, wide), lambda h, j, k=k: (j, k * (HEADS // hpb) + h))

    return _call(
        body, name="hgrn_fwd", grid=(HEADS // hpb, nb),
        in_specs=[col(0), col(1), col(2), col(3),
                  pl.BlockSpec((2, wide), lambda h, j: (0, h)), pl.BlockSpec((1, HK), lambda h, j: (0, 0))],
        out_specs=[pl.BlockSpec((bt, wide), lambda h, j: (j, h)), pl.BlockSpec((bt, wide), lambda h, j: (j, h)),
                   pl.BlockSpec((hpb, nc, HK, HK), lambda h, j: (h, j, 0, 0))],
        out_shape=[jax.ShapeDtypeStruct((t, D), F32), jax.ShapeDtypeStruct((t, D), BF16),
                   jax.ShapeDtypeStruct((HEADS, t // CHUNK, HK, HK), F32)],
        scratch_shapes=[pltpu.VMEM((hpb, HK, HK), F32)],
        compiler_params=_cp(),
    )(p_hg, p_hg, p_hg, p_hg, tab, gain)


def _hgrn_bwd(p_hg, tab, gain, o_raw, states, dho):
    t = p_hg.shape[0]
    bt = min(HG_BT, t)
    nb, nc = t // bt, bt // CHUNK
    hpb = HG_HPB
    wide = hpb * HK

    def body(q_ref, f_ref, i_ref, g_ref, tab_ref, gain_ref, o_ref, st_ref, dho_ref,
             dq_ref, df_ref, di_ref, dg_ref, dtab_ref, dgain_ref, dstate, dlb):
        h, j = pl.program_id(0), pl.program_id(1)

        @pl.when(jnp.logical_and(h == 0, j == 0))
        def _():
            dgain_ref[...] = jnp.zeros_like(dgain_ref)

        @pl.when(j == 0)
        def _():
            dstate[...] = jnp.zeros_like(dstate)
            dlb[...] = jnp.zeros_like(dlb)

        row = lax.broadcasted_iota(jnp.int32, (CHUNK, HK), 0)
        tril = lax.broadcasted_iota(jnp.int32, (CHUNK, CHUNK), 0) >= lax.broadcasted_iota(jnp.int32, (CHUNK, CHUNK), 1)
        gain_v = gain_ref[...]

        def chunk(cc, carry):
            c = nc - 1 - cc
            sl = pl.ds(pl.multiple_of(c * CHUNK, CHUNK), CHUNK)
            dgain = jnp.zeros((1, HK), F32)
            for hh in range(hpb):
                ln = slice(hh * HK, (hh + 1) * HK)
                lb = _sig(tab_ref[0:1, ln] - tab_ref[1:2, ln])
                qr = q_ref[sl, ln].astype(F32)
                v = i_ref[sl, ln].astype(BF16)
                gr = g_ref[sl, ln].astype(F32)
                q, sg, f, k, last, e_q, e_k, e_a, e_l = _hg_gates(qr, f_ref[sl, ln].astype(F32), lb, row)
                o = o_ref[sl, ln]
                r = lax.rsqrt(jnp.mean(o * o, axis=-1, keepdims=True) + EPS)
                oh = o * r
                dh = dho_ref[sl, ln].astype(F32)
                dnorm = dh * _silu(gr)
                dg_ref[sl, ln] = (dh * oh * gain_v * _dsilu(gr)).astype(BF16)
                dgain = dgain + jnp.sum(dnorm * oh, axis=0, keepdims=True)
                dxh = dnorm * gain_v
                do = (r * (dxh - oh * jnp.mean(dxh * oh, axis=-1, keepdims=True))).astype(BF16)
                st0 = st_ref[hh, c]
                st0_b = st0.astype(BF16)
                ds1 = dstate[hh]
                ds1_b = ds1.astype(BF16)
                qt = (q * e_q).astype(BF16)
                kt = (k * e_k).astype(BF16)
                qd = (q * e_a).astype(BF16)
                kd = (k * e_l).astype(BF16)
                p = jnp.where(tril, _dot_nt(qt, kt), 0.0).astype(BF16)
                dp = jnp.where(tril, _dot_nt(do, v), 0.0).astype(BF16)
                dv = _dot_tn(p, do) + _dot_nt(kd, ds1_b)
                dqt = _dot(dp, kt)
                dkt = _dot_tn(dp, qt)
                dq_inter = _dot(do, st0_b) * e_a
                dk_inter = _dot(v, ds1_b) * e_l
                dq = dqt * e_q + dq_inter
                dk = dkt * e_k + dk_inter
                e_last = jnp.exp(last)
                dstate[hh] = _dot_tn(do, qd) + e_last * ds1
                dlast = (jnp.sum(k * dk_inter, axis=0, keepdims=True)
                         + e_last * jnp.sum(ds1 * st0, axis=0, keepdims=True))
                da = (qt.astype(F32) * dqt - kt.astype(F32) * dkt + q * dq_inter - k * dk_inter
                      + jnp.where(row == CHUNK - 1, dlast, 0.0))
                dlf = _rcumsum_rows(da, row)
                dfv = dlf / f - dk
                df_ref[sl, ln] = (dfv * (1.0 - lb) * sg * (1.0 - sg)).astype(BF16)
                dlb[:, ln] += jnp.sum(dfv * (1.0 - sg), axis=0, keepdims=True)
                dq_ref[sl, ln] = (dq * _dsilu(qr)).astype(BF16)
                di_ref[sl, ln] = dv.astype(BF16)
            dgain_ref[...] += dgain
            return carry

        lax.fori_loop(0, nc, chunk, 0)

        @pl.when(j == nb - 1)
        def _():
            lb = _sig(tab_ref[0:1, :] - tab_ref[1:2, :])
            d0 = dlb[...] * lb * (1.0 - lb)
            dtab_ref[0:1, :] = d0
            dtab_ref[1:2, :] = -d0

    def col(k):
        return pl.BlockSpec((bt, wide), lambda h, j, k=k: (nb - 1 - j, k * (HEADS // hpb) + h))

    tok = pl.BlockSpec((bt, wide), lambda h, j: (nb - 1 - j, h))
    return _call(
        body, name="hgrn_bwd", grid=(HEADS // hpb, nb),
        in_specs=[col(0), col(1), col(2), col(3),
                  pl.BlockSpec((2, wide), lambda h, j: (0, h)), pl.BlockSpec((1, HK), lambda h, j: (0, 0)),
                  tok, pl.BlockSpec((hpb, nc, HK, HK), lambda h, j: (h, nb - 1 - j, 0, 0)), tok],
        out_specs=[tok, tok, tok, tok,
                   pl.BlockSpec((2, wide), lambda h, j: (0, h)), pl.BlockSpec((1, HK), lambda h, j: (0, 0))],
        out_shape=[jax.ShapeDtypeStruct((t, D), BF16)] * 4
        + [jax.ShapeDtypeStruct((2, D), F32), jax.ShapeDtypeStruct((1, HK), F32)],
        scratch_shapes=[pltpu.VMEM((hpb, HK, HK), F32), pltpu.VMEM((1, wide), F32)],
        compiler_params=_cp(),
    )(p_hg, p_hg, p_hg, p_hg, tab, gain, o_raw, states, dho)


def _rope_tables(pos):
    t = pos.shape[0]
    tm = min(ROW_TM, t)
    inv = np.zeros((1, HK), np.float32)
    freq = (ROPE_THETA ** (-np.arange(0, ROPE, 2, dtype=np.float32) / ROPE)).astype(np.float32)
    inv[0, 0:ROPE // 2] = freq
    inv[0, ROPE // 2:ROPE] = freq
    sign = np.zeros((1, HK), np.float32)
    sign[0, 0:ROPE // 2] = -1.0
    sign[0, ROPE // 2:ROPE] = 1.0

    def body(pos_ref, inv_ref, sign_ref, cos_ref, sin_ref):
        ang = pos_ref[...].astype(F32) * inv_ref[...]
        cos_ref[...] = jnp.cos(ang)
        sin_ref[...] = jnp.sin(ang) * sign_ref[...]

    one = pl.BlockSpec((1, HK), lambda i: (0, 0))
    row = pl.BlockSpec((tm, HK), lambda i: (i, 0))
    return _call(
        body, name="rope_tables", grid=(t // tm,),
        in_specs=[pl.BlockSpec((tm, 1), lambda i: (i, 0)), one, one],
        out_specs=[row, row],
        out_shape=[jax.ShapeDtypeStruct((t, HK), F32)] * 2,
        compiler_params=_cp(),
    )(pos, jnp.asarray(inv), jnp.asarray(sign))


def _rope(x, cos, sin_signed):
    r = lax.broadcasted_iota(jnp.int32, (HK, HK), 0)
    c = lax.broadcasted_iota(jnp.int32, (HK, HK), 1)
    half = ROPE // 2
    swap = jnp.logical_or(jnp.logical_and(c < half, r == c + half),
                          jnp.logical_and(jnp.logical_and(c >= half, c < ROPE), r == c - half))
    return x * cos + _dot_split(x, swap.astype(BF16)) * sin_signed


def _dot_split(x, m):
    hi = x.astype(BF16)
    lo = (x - hi.astype(F32)).astype(BF16)
    return _dot(hi, m) + _dot(lo, m)


def _lane_sum(x):
    return _dot_split(x, jnp.ones((HK, HK), BF16))


def _head_norm(xn, xr):
    r = lax.rsqrt(_lane_sum(xn * xn + xr * xr) * (1.0 / QK) + EPS)
    return xn * r, xr * r, r


def _head_norm_bwd(xn, xr, g_n, g_r, dn, dr):
    hn, hr, r = _head_norm(xn, xr)
    dxn, dxr = dn * g_n, dr * g_r
    c = _lane_sum(dxn * hn + dxr * hr) * (1.0 / QK)
    return r * (dxn - hn * c), r * (dxr - hr * c), dn * hn, dr * hr


def _mla_prep_fwd(qf, kv, p_mla, cos, sin, gq, gk):
    t = qf.shape[0]
    tm = min(PREP_TM, t)

    def body(qf_ref, kv_ref, kpe_ref, cos_ref, sin_ref, gq_ref, gk_ref, q_ref, k_ref, v_ref):
        cos_v, sin_v = cos_ref[...], sin_ref[...]
        kpe = kpe_ref[...]
        for h in range(HEADS):
            lo, mid, hi = h * QKP, h * QKP + HK, (h + 1) * QKP
            qn, qr, _ = _head_norm(qf_ref[:, lo:mid].astype(F32), qf_ref[:, mid:hi].astype(F32))
            q_ref[h, :, 0:HK] = (qn * gq_ref[:, 0:HK] * (SCALE * LOG2E)).astype(BF16)
            q_ref[h, :, HK:QKP] = (_rope(qr * gq_ref[:, HK:QKP], cos_v, sin_v) * (SCALE * LOG2E)).astype(BF16)
            kn, kr, _ = _head_norm(kv_ref[:, lo:mid].astype(F32), kpe)
            k_ref[h, :, 0:HK] = (kn * gk_ref[:, 0:HK]).astype(BF16)
            k_ref[h, :, HK:QKP] = _rope(kr * gk_ref[:, HK:QKP], cos_v, sin_v).astype(BF16)
            v_ref[h, :, 0:HK] = kv_ref[:, mid:hi].astype(BF16)
            v_ref[h, :, HK:QKP] = jnp.full((tm, HK), -1.0, BF16)

    head = pl.BlockSpec((tm, HEADS * QKP), lambda i: (i, 0))
    tok = pl.BlockSpec((tm, HK), lambda i: (i, 0))
    gain = pl.BlockSpec((1, QKP), lambda i: (0, 0))
    return _call(
        body, name="mla_prep_fwd", grid=(t // tm,),
        in_specs=[head, head, pl.BlockSpec((tm, HK), lambda i: (i, MLA_COLS // HK - 1)), tok, tok, gain, gain],
        out_specs=[pl.BlockSpec((HEADS, tm, QKP), lambda i: (0, i, 0)),
                   pl.BlockSpec((HEADS, tm, QKP), lambda i: (0, i, 0)),
                   pl.BlockSpec((HEADS, tm, QKP), lambda i: (0, i, 0))],
        out_shape=[jax.ShapeDtypeStruct((HEADS, t, QKP), BF16), jax.ShapeDtypeStruct((HEADS, t, QKP), BF16),
                   jax.ShapeDtypeStruct((HEADS, t, QKP), BF16)],
        compiler_params=_cp(),
    )(qf, kv, p_mla, cos, sin, gq, gk)


def _mla_prep_bwd(qf, kv, p_mla, cos, sin, gq, gk, dq, dk, dv):
    t = qf.shape[0]
    tm = min(PREP_TM, t)

    def body(qf_ref, kv_ref, kpe_ref, cos_ref, sin_ref, gq_ref, gk_ref, dq_ref, dk_ref, dv_ref,
             dqf_ref, dkv_ref, dkpe_ref, dgq_ref, dgk_ref):
        @pl.when(pl.program_id(0) == 0)
        def _():
            dgq_ref[...] = jnp.zeros_like(dgq_ref)
            dgk_ref[...] = jnp.zeros_like(dgk_ref)

        cos_v, sin_v = cos_ref[...], -sin_ref[...]
        kpe = kpe_ref[...]
        gqn, gqr, gkn, gkr = gq_ref[:, 0:HK], gq_ref[:, HK:QKP], gk_ref[:, 0:HK], gk_ref[:, HK:QKP]
        dkpe = jnp.zeros((tm, HK), F32)
        dgq_n, dgq_r, dgk_n, dgk_r = [jnp.zeros((1, HK), F32) for _ in range(4)]
        for h in range(HEADS):
            lo, mid, hi = h * QKP, h * QKP + HK, (h + 1) * QKP
            dqn = dq_ref[h, :, 0:HK].astype(F32) * SCALE
            dqr = _rope(dq_ref[h, :, HK:QKP].astype(F32), cos_v, sin_v) * SCALE
            a, b, ga, gb = _head_norm_bwd(qf_ref[:, lo:mid].astype(F32), qf_ref[:, mid:hi].astype(F32),
                                          gqn, gqr, dqn, dqr)
            dqf_ref[:, lo:mid] = a.astype(BF16)
            dqf_ref[:, mid:hi] = b.astype(BF16)
            dgq_n = dgq_n + jnp.sum(ga, axis=0, keepdims=True)
            dgq_r = dgq_r + jnp.sum(gb, axis=0, keepdims=True)
            dkn = dk_ref[h, :, 0:HK].astype(F32) * LN2
            dkr = _rope(dk_ref[h, :, HK:QKP].astype(F32), cos_v, sin_v) * LN2
            a, b, ga, gb = _head_norm_bwd(kv_ref[:, lo:mid].astype(F32), kpe, gkn, gkr, dkn, dkr)
            dkv_ref[:, lo:mid] = a.astype(BF16)
            dkv_ref[:, mid:hi] = dv_ref[h].astype(BF16)
            dkpe = dkpe + b
            dgk_n = dgk_n + jnp.sum(ga, axis=0, keepdims=True)
            dgk_r = dgk_r + jnp.sum(gb, axis=0, keepdims=True)
        dkpe_ref[...] = dkpe
        dgq_ref[:, 0:HK] += dgq_n
        dgq_ref[:, HK:QKP] += dgq_r
        dgk_ref[:, 0:HK] += dgk_n
        dgk_ref[:, HK:QKP] += dgk_r

    head = pl.BlockSpec((tm, HEADS * QKP), lambda i: (i, 0))
    tok = pl.BlockSpec((tm, HK), lambda i: (i, 0))
    gain = pl.BlockSpec((1, QKP), lambda i: (0, 0))
    hq = pl.BlockSpec((HEADS, tm, QKP), lambda i: (0, i, 0))
    return _call(
        body, name="mla_prep_bwd", grid=(t // tm,),
        in_specs=[head, head, pl.BlockSpec((tm, HK), lambda i: (i, MLA_COLS // HK - 1)), tok, tok, gain, gain,
                  hq, hq, pl.BlockSpec((HEADS, tm, HK), lambda i: (0, i, 0))],
        out_specs=[head, head, tok, gain, gain],
        out_shape=[jax.ShapeDtypeStruct((t, HEADS * QKP), BF16), jax.ShapeDtypeStruct((t, HEADS * QKP), BF16),
                   jax.ShapeDtypeStruct((t, HK), F32), jax.ShapeDtypeStruct((1, QKP), F32),
                   jax.ShapeDtypeStruct((1, QKP), F32)],
        compiler_params=_cp(),
    )(qf, kv, p_mla, cos, sin, gq, gk, dq, dk, dv)


def _chunk_mask(row0, rows, cols):
    r = lax.broadcasted_iota(jnp.int32, (rows, cols), 0) + row0
    c = lax.broadcasted_iota(jnp.int32, (rows, cols), 1)
    return jnp.right_shift(r, 6) >= jnp.right_shift(c, 6)


def _flash_fwd(q, k, v, side=None):
    t = q.shape[1]
    tq = min(TQ, t)
    nq = t // tq
    sub = min(SUBQ, tq)
    pairs = [(i, j) for i in range(nq) for j in range(i + 1)]
    qi = jnp.asarray([p[0] for p in pairs], jnp.int32)
    kj = jnp.asarray([p[1] for p in pairs], jnp.int32)
    s_in = len(side.inputs) if side else 0
    s_out = len(side.out_shapes) if side else 0

    def body(qi_ref, kj_ref, q_ref, k_ref, v_ref, *rest):
        o_ref, lse_ref = rest[s_in:s_in + 2]
        m_s, acc_s = rest[s_in + 2 + s_out:s_in + 4 + s_out]
        side_refs = list(rest[:s_in]) + list(rest[s_in + 2:s_in + 2 + s_out]) + list(rest[s_in + 4 + s_out:])
        n = pl.program_id(1)
        i, j = qi_ref[n], kj_ref[n]
        if side:
            @pl.when(jnp.logical_and(pl.program_id(0) == 0, n == 0))
            def _():
                side.start(*side_refs)

        @pl.when(j == 0)
        def _():
            m_s[...] = jnp.full_like(m_s, NEG)
            acc_s[...] = jnp.zeros_like(acc_s)

        def step(diag):
            subs = range(tq // sub)
            width = [(r + 1) * sub if diag else tq for r in subs]
            logits = [_dot_nt(q_ref[r * sub:(r + 1) * sub, :], k_ref[0:width[r], :]) for r in subs]
            for r in subs:
                rows = slice(r * sub, (r + 1) * sub)
                cols = width[r]
                s = logits[r]
                if diag:
                    s = jnp.where(_chunk_mask(r * sub, sub, cols), s, NEG)
                m_old = m_s[rows, :]
                m_new = jnp.maximum(m_old, jnp.max(s, axis=-1, keepdims=True))
                alpha = jnp.exp2(m_old - m_new)
                p = jnp.exp2((s - jnp.tile(m_new, (1, cols // HK))).astype(BF16))
                acc_s[rows, :] = jnp.tile(alpha, (1, 2)) * acc_s[rows, :] + _dot(p, v_ref[0:cols, :])
                m_s[rows, :] = m_new

        @pl.when(j < i)
        def _():
            step(False)

        @pl.when(j == i)
        def _():
            step(True)
            l = -acc_s[:, HK:QKP]
            o_ref[...] = (acc_s[:, 0:HK] / l).astype(BF16)
            lse_ref[...] = m_s[...] + jnp.log(l) * LOG2E

        if side:
            @pl.when(jnp.logical_and(pl.program_id(0) == HEADS - 1, n == len(pairs) - 1))
            def _():
                side.finish(*side_refs)

    anywhere = pl.BlockSpec(memory_space=pl.ANY)
    grid_spec = pltpu.PrefetchScalarGridSpec(
        num_scalar_prefetch=2, grid=(HEADS, len(pairs)),
        in_specs=[pl.BlockSpec((None, tq, QKP), lambda h, n, qi, kj: (h, qi[n], 0)),
                  pl.BlockSpec((None, tq, QKP), lambda h, n, qi, kj: (h, kj[n], 0)),
                  pl.BlockSpec((None, tq, QKP), lambda h, n, qi, kj: (h, kj[n], 0))] + [anywhere] * s_in,
        out_specs=[pl.BlockSpec((tq, HK), lambda h, n, qi, kj: (qi[n], h)),
                   pl.BlockSpec((None, tq, HK), lambda h, n, qi, kj: (h, qi[n], 0))] + [anywhere] * s_out,
        scratch_shapes=[pltpu.VMEM((tq, HK), F32), pltpu.VMEM((tq, QKP), F32)] + (list(side.scratch) if side else []),
    )
    return _call(
        body, name="flash_fwd", grid_spec=grid_spec,
        out_shape=[jax.ShapeDtypeStruct((t, D), BF16), jax.ShapeDtypeStruct((HEADS, t, HK), F32)]
        + (list(side.out_shapes) if side else []),
        compiler_params=_cp(),
    )(qi, kj, q, k, v, *(side.inputs if side else []))


def _attn_do(do, o):
    t = do.shape[0]
    tm = min(TM, t)

    def body(do_ref, o_ref, d_ref):
        lane = lax.broadcasted_iota(jnp.int32, (tm, HK), 1)
        for h in range(HEADS):
            ln = slice(h * HK, (h + 1) * HK)
            dov = do_ref[:, ln]
            d = jnp.sum(dov.astype(F32) * o_ref[:, ln].astype(F32), axis=-1, keepdims=True)
            hi = d.astype(BF16).astype(F32)
            d_ref[h, :, 0:HK] = dov
            d_ref[h, :, HK:QKP] = jnp.where(lane == 0, hi, jnp.where(lane == 1, d - hi, 0.0)).astype(BF16)

    blk = pl.BlockSpec((tm, D), lambda i: (i, 0))
    return _call(
        body, name="attn_do", grid=(t // tm,),
        in_specs=[blk, blk],
        out_specs=pl.BlockSpec((HEADS, tm, QKP), lambda i: (0, i, 0)),
        out_shape=jax.ShapeDtypeStruct((HEADS, t, QKP), BF16),
        compiler_params=_cp(),
    )(do, o)


def _flash_bwd(q, k, v, lse, do):
    t = q.shape[1]
    tq = min(TQ, t)
    nq = t // tq
    sub = min(SUBQ, tq)
    pairs = [(i, j) for j in range(nq) for i in range(j, nq)]
    qi = jnp.asarray([p[0] for p in pairs], jnp.int32)
    kj = jnp.asarray([p[1] for p in pairs], jnp.int32)
    npairs = len(pairs)

    def body(qi_ref, kj_ref, q_ref, k_ref, v_ref, lse_ref, do_ref, dq_ref, dk_ref, dv_ref):
        n = pl.program_id(1)
        i, j = qi_ref[n], kj_ref[n]

        @pl.when(n == 0)
        def _():
            dq_ref[...] = jnp.zeros_like(dq_ref)

        @pl.when(i == j)
        def _():
            dk_ref[...] = jnp.zeros_like(dk_ref)
            dv_ref[...] = jnp.zeros_like(dv_ref)

        def step(diag):
            for r in range(tq // sub):
                rows = slice(r * sub, (r + 1) * sub)
                cols = (r + 1) * sub if diag else tq
                qv, kv_ = q_ref[rows, :], k_ref[0:cols, :]
                p = jnp.exp2(_dot_nt(qv, kv_) - jnp.tile(lse_ref[rows, :], (1, cols // HK)))
                if diag:
                    p = jnp.where(_chunk_mask(r * sub, sub, cols), p, 0.0)
                dp_less_delta = _dot_nt(do_ref[rows, :], v_ref[0:cols, :])
                ds = (p * dp_less_delta).astype(BF16)
                dv_ref[0:cols, :] += _dot_tn(p.astype(BF16), do_ref[rows, 0:HK])
                dk_ref[0:cols, :] += _dot_tn(ds, qv)
                dq_rows = pl.ds(pl.multiple_of(i * tq + r * sub, sub), sub)
                dq_ref[dq_rows, :] += _dot(ds, kv_)

        @pl.when(j < i)
        def _():
            step(False)

        @pl.when(j == i)
        def _():
            step(True)

    grid_spec = pltpu.PrefetchScalarGridSpec(
        num_scalar_prefetch=2, grid=(HEADS, npairs),
        in_specs=[pl.BlockSpec((None, tq, QKP), lambda h, n, qi, kj: (h, qi[n], 0)),
                  pl.BlockSpec((None, tq, QKP), lambda h, n, qi, kj: (h, kj[n], 0)),
                  pl.BlockSpec((None, tq, QKP), lambda h, n, qi, kj: (h, kj[n], 0)),
                  pl.BlockSpec((None, tq, HK), lambda h, n, qi, kj: (h, qi[n], 0)),
                  pl.BlockSpec((None, tq, QKP), lambda h, n, qi, kj: (h, qi[n], 0))],
        out_specs=[pl.BlockSpec((None, t, QKP), lambda h, n, qi, kj: (h, 0, 0)),
                   pl.BlockSpec((None, tq, QKP), lambda h, n, qi, kj: (h, kj[n], 0)),
                   pl.BlockSpec((None, tq, HK), lambda h, n, qi, kj: (h, kj[n], 0))],
    )
    return _call(
        body, name="flash_bwd", grid_spec=grid_spec,
        out_shape=[jax.ShapeDtypeStruct((HEADS, t, QKP), F32), jax.ShapeDtypeStruct((HEADS, t, QKP), F32),
                   jax.ShapeDtypeStruct((HEADS, t, HK), F32)],
        compiler_params=_cp(56),
    )(qi, kj, q, k, v, lse, do)


def _adamw(name, w, g, m, v):
    r, c = w.shape
    tr = r if r <= 256 else next(k for k in (256, 352, 384) if r % k == 0)

    def body(w_ref, g_ref, m_ref, v_ref, d_ref, nm_ref, nv_ref):
        gv = g_ref[...]
        nm = ADAM_B1 * m_ref[...] + (1.0 - ADAM_B1) * gv
        nv = ADAM_B2 * v_ref[...] + (1.0 - ADAM_B2) * (gv * gv)
        m_hat = nm / (1.0 - ADAM_B1 ** ADAM_STEP)
        v_hat = nv / (1.0 - ADAM_B2 ** ADAM_STEP)
        d_ref[...] = -ADAM_LR * (m_hat / (jnp.sqrt(v_hat) + ADAM_EPS) + ADAM_WD * w_ref[...])
        nm_ref[...] = nm
        nv_ref[...] = nv

    blk = pl.BlockSpec((tr, c), lambda i: (i, 0))
    return _call(
        body, name=name, grid=(r // tr,),
        in_specs=[blk] * 4, out_specs=[blk] * 3,
        out_shape=[jax.ShapeDtypeStruct((r, c), F32)] * 3,
        compiler_params=_cp(),
    )(w, g, m, v)


def _place():
    return lax.axis_index("x"), lax.axis_index("y"), lax.axis_index("c")


def _other_chips(x, y):
    return [(1 - x, y), (x, 1 - y), (1 - x, 1 - y)]


class _Exchange:
    inputs = ()
    out_shapes = ()
    scratch = ()

    def start(self, *refs):
        raise NotImplementedError

    def finish(self, *refs):
        raise NotImplementedError

    def alone(self, name):
        def body(*refs):
            self.start(*refs)
            self.finish(*refs)

        anywhere = pl.BlockSpec(memory_space=pl.ANY)
        return _call(
            body, name=name,
            in_specs=[anywhere] * len(self.inputs), out_specs=[anywhere] * len(self.out_shapes),
            out_shape=list(self.out_shapes), scratch_shapes=list(self.scratch),
        )(*self.inputs)


class _GatherWeights(_Exchange):
    def __init__(self, shards):
        self.inputs = tuple(shards)
        self.out_shapes = tuple(jax.ShapeDtypeStruct((4,) + s.shape, s.dtype) for s in shards)
        self.scratch = (pltpu.SemaphoreType.DMA((6 * len(shards),)), pltpu.SemaphoreType.DMA((6 * len(shards),)))

    def gathered(self, got, k):
        return [lax.dynamic_update_slice(g, s[None], (k, 0, 0)) for g, s in zip(got, self.inputs)]

    def _copies(self, *refs):
        nbuf = len(self.inputs)
        send_sems, recv_sems = refs[2 * nbuf:]
        x, y, c = _place()
        chips = _other_chips(x, y)
        first, passed, landed, relayed = [], [], [], []
        for b, (s_ref, g_ref) in enumerate(zip(refs[:nbuf], refs[nbuf:2 * nbuf])):
            half = self.inputs[b].shape[0] // 2

            def rows(px, py, pc, g_ref=g_ref, half=half):
                return g_ref.at[2 * px + py, pl.ds(pc * half, half), :]

            def copy(k, block, to, src=None, rows=rows, b=b):
                return pltpu.make_async_remote_copy(
                    src_ref=rows(*block) if src is None else src, dst_ref=rows(*block),
                    send_sem=send_sems.at[6 * b + k], recv_sem=recv_sems.at[6 * b + k], device_id=to, device_id_type=MESH)

            mine = s_ref.at[pl.ds(c * half, half), :]
            first += [copy(j, (x, y, c), (*chip, c), src=mine) for j, chip in enumerate(chips)]
            passed += [copy(3 + j, (*chip, c), (x, y, 1 - c)) for j, chip in enumerate(chips)]
            landed += [copy(j, (*chip, c), (x, y, c)) for j, chip in enumerate(chips)]
            relayed += [copy(3 + j, (*chip, 1 - c), (x, y, c)) for j, chip in enumerate(chips)]
        return first, passed, landed, relayed

    def start(self, *refs):
        for cp in self._copies(*refs)[0]:
            cp.start()

    def finish(self, *refs):
        first, passed, landed, relayed = self._copies(*refs)
        for arrived, onward in zip(landed, passed):
            arrived.wait_recv()
            onward.start()
        for cp in relayed:
            cp.wait_recv()
        for cp in first + passed:
            cp.wait_send()


class _SwapHalves(_Exchange):
    def __init__(self, bufs):
        self.inputs = tuple(bufs)
        self.out_shapes = tuple(jax.ShapeDtypeStruct((4, g.shape[1] // 2, g.shape[2]), g.dtype) for g in bufs)
        self.scratch = (pltpu.SemaphoreType.DMA((len(bufs),)), pltpu.SemaphoreType.DMA((len(bufs),)))

    def _copies(self, *refs):
        nbuf = len(self.inputs)
        send_sems, recv_sems = refs[2 * nbuf:]
        x, y, c = _place()
        cps = []
        for b, (g_ref, o_ref) in enumerate(zip(refs[:nbuf], refs[nbuf:2 * nbuf])):
            half = self.inputs[b].shape[1] // 2
            cps.append(pltpu.make_async_remote_copy(
                src_ref=g_ref.at[:, pl.ds((1 - c) * half, half), :], dst_ref=o_ref,
                send_sem=send_sems.at[b], recv_sem=recv_sems.at[b], device_id=(x, y, 1 - c), device_id_type=MESH))
        return cps

    def start(self, *refs):
        for cp in self._copies(*refs):
            cp.start()

    def finish(self, *refs):
        for cp in self._copies(*refs):
            cp.wait()


def _add_rows(half):
    return next(tr for tr in range(512, 15, -16) if half % tr == 0)


def _chip_sum(name, gp, got, c_arr):
    half, width = got.shape[1], got.shape[2]
    tr = _add_rows(half)
    nb = half // tr

    def body(c_ref, a_ref, b_ref, o_ref, ob_ref):
        s = a_ref[...] + b_ref[...]
        o_ref[...] = s
        ob_ref[...] = s.astype(BF16)

    grid_spec = pltpu.PrefetchScalarGridSpec(
        num_scalar_prefetch=1, grid=(4, nb),
        in_specs=[pl.BlockSpec((None, tr, width), lambda s, i, c: (s, c[0] * nb + i, 0)),
                  pl.BlockSpec((None, tr, width), lambda s, i, c: (s, i, 0))],
        out_specs=[pl.BlockSpec((None, tr, width), lambda s, i, c: (s, i, 0)),
                   pl.BlockSpec((None, tr, width), lambda s, i, c: (s, i, 0))],
    )
    return _call(
        body, name=name, grid_spec=grid_spec,
        out_shape=[jax.ShapeDtypeStruct(got.shape, F32), jax.ShapeDtypeStruct(got.shape, BF16)],
        compiler_params=_cp(),
    )(c_arr, gp, got)


class _ScatterChipSums(_Exchange):
    def __init__(self, sums):
        self.inputs = tuple(sums)
        self.out_shapes = tuple(jax.ShapeDtypeStruct((3,) + cs.shape[1:], cs.dtype) for cs in sums)
        self.scratch = (pltpu.SemaphoreType.DMA((3 * len(sums),)), pltpu.SemaphoreType.DMA((3 * len(sums),)))

    def _copies(self, *refs):
        nbuf = len(self.inputs)
        send_sems, recv_sems = refs[2 * nbuf:]
        x, y, c = _place()
        return [pltpu.make_async_remote_copy(
            src_ref=s_ref.at[2 * px + py], dst_ref=o_ref.at[j],
            send_sem=send_sems.at[3 * b + j], recv_sem=recv_sems.at[3 * b + j], device_id=(px, py, c), device_id_type=MESH)
            for b, (s_ref, o_ref) in enumerate(zip(refs[:nbuf], refs[nbuf:2 * nbuf]))
            for j, (px, py) in enumerate(_other_chips(x, y))]

    def start(self, *refs):
        for cp in self._copies(*refs):
            cp.start()

    def finish(self, *refs):
        for cp in self._copies(*refs):
            cp.wait()


def _shard_sum(name, cs, got, kc_arr):
    h, width = cs.shape[1], cs.shape[2]
    tr = _add_rows(h)
    nb = h // tr

    def body(k_ref, a_ref, b_ref, o_ref):
        o_ref[...] = ((a_ref[...] + b_ref[0].astype(F32)) + b_ref[1].astype(F32)) + b_ref[2].astype(F32)

    grid_spec = pltpu.PrefetchScalarGridSpec(
        num_scalar_prefetch=1, grid=(nb,),
        in_specs=[pl.BlockSpec((None, tr, width), lambda i, k: (k[0], i, 0)),
                  pl.BlockSpec((3, tr, width), lambda i, k: (0, i, 0))],
        out_specs=pl.BlockSpec((tr, width), lambda i, k: (k[1] * nb + i, 0)),
    )
    return _call(
        body, name=name, grid_spec=grid_spec,
        out_shape=jax.ShapeDtypeStruct((2 * h, width), F32),
        compiler_params=_cp(),
    )(kc_arr, cs, got)


def _join_halves(name, boths):
    nbuf = len(boths)

    def body(*refs):
        send_sems, recv_sems = refs[2 * nbuf:]
        x, y, c = _place()
        sent, landing = [], []
        for b, (m_ref, o_ref) in enumerate(zip(refs[:nbuf], refs[nbuf:2 * nbuf])):
            h = boths[b].shape[0] // 2
            mine = m_ref.at[pl.ds(c * h, h), :]
            sent.append(pltpu.make_async_remote_copy(
                src_ref=mine, dst_ref=o_ref.at[pl.ds(c * h, h), :],
                send_sem=send_sems.at[b], recv_sem=recv_sems.at[b], device_id=(x, y, 1 - c), device_id_type=MESH))
            landing.append(pltpu.make_async_remote_copy(
                src_ref=mine, dst_ref=o_ref.at[pl.ds((1 - c) * h, h), :],
                send_sem=send_sems.at[b], recv_sem=recv_sems.at[b], device_id=(x, y, 1 - c), device_id_type=MESH))
        for cp in sent:
            cp.start()
        for cp in sent:
            cp.wait_send()
        for cp in landing:
            cp.wait_recv()

    anywhere = pl.BlockSpec(memory_space=pl.ANY)
    return _call(
        body, name=name,
        in_specs=[anywhere] * nbuf, out_specs=[anywhere] * nbuf,
        out_shape=[jax.ShapeDtypeStruct(g.shape, g.dtype) for g in boths],
        input_output_aliases={b: b for b in range(nbuf)},
        scratch_shapes=[pltpu.SemaphoreType.DMA((nbuf,)), pltpu.SemaphoreType.DMA((nbuf,))],
    )(*boths)


def _all_reduce_small(v):
    r = v.shape[0]

    def body(v_ref, o_ref, buf, send_sems, recv_sems):
        x, y, c = _place()
        me = 4 * x + 2 * y + c
        buf[me] = v_ref[...]
        cps = []
        for k in range(1, 8):
            peer = (x ^ (k >> 2), y ^ ((k >> 1) & 1), c ^ (k & 1))
            cps.append(pltpu.make_async_remote_copy(
                src_ref=v_ref, dst_ref=buf.at[me],
                send_sem=send_sems.at[k - 1], recv_sem=recv_sems.at[k - 1], device_id=peer, device_id_type=MESH))
        for cp in cps:
            cp.start()
        for k in range(1, 8):
            pltpu.make_async_remote_copy(
                src_ref=v_ref, dst_ref=buf.at[me ^ k],
                send_sem=send_sems.at[k - 1], recv_sem=recv_sems.at[k - 1],
                device_id=(x, y, c), device_id_type=MESH).wait_recv()
        for cp in cps:
            cp.wait_send()
        acc = buf[0]
        for k in range(1, 8):
            acc = acc + buf[k]
        o_ref[...] = acc

    return _call(
        body, name="all_reduce_small",
        in_specs=[pl.BlockSpec(memory_space=pltpu.VMEM)],
        out_specs=pl.BlockSpec(memory_space=pltpu.VMEM),
        out_shape=jax.ShapeDtypeStruct((r, 128), F32),
        scratch_shapes=[pltpu.VMEM((8, r, 128), F32), pltpu.SemaphoreType.DMA((7,)), pltpu.SemaphoreType.DMA((7,))],
    )(v)


def _group(names):
    return tuple(e for e in BIG if e[0] in names)


def _pack(shards, dtype):
    return jnp.concatenate([s.astype(dtype).reshape(-1, PACK_W) for s in shards], axis=0)


def _unpack_full(g, group):
    out, at = {}, 0
    for name, rows, cols, axis in group:
        n = rows * cols // 4 // PACK_W
        blk = g[:, at:at + n, :]
        at += n
        if axis == 1:
            out[name] = blk.reshape(4, rows, cols // 4).transpose(1, 0, 2).reshape(rows, cols)
        else:
            out[name] = blk.reshape(rows, cols)
    return out


def _pack_grads(grads, group):
    parts = []
    for name, rows, cols, axis in group:
        g = grads[name]
        if axis == 1:
            g = g.reshape(rows, 4, cols // 4).transpose(1, 0, 2)
        parts.append(g.reshape(4, -1, PACK_W))
    rows_total = sum(p.shape[1] for p in parts)
    pad = -rows_total % PACK_ALIGN
    if pad:
        parts.append(jnp.zeros((4, pad, PACK_W), F32))
    return jnp.concatenate(parts, axis=1)


def _unpack_shard(s, group):
    out, at = {}, 0
    for name, rows, cols, axis in group:
        n = rows * cols // 4 // PACK_W
        shape = (rows, cols // 4) if axis == 1 else (rows // 4, cols)
        out[name] = s[at:at + n, :].reshape(shape)
        at += n
    return out


def _pack_small(parts):
    flat = jnp.concatenate([p.reshape(-1) for p in parts])
    pad = -flat.shape[0] % 1024
    return jnp.concatenate([flat, jnp.zeros((pad,), F32)]).reshape(-1, 128)


def _ffn_in(tag, h, gain, w_in, side=None):
    t = h.shape[0]
    wide = DFF // 2

    def compute_in(rows, weights, outs):
        hv, w_ref = rows[0][...], weights[0]
        r = lax.rsqrt(jnp.mean(hv * hv, axis=-1, keepdims=True) + EPS)
        a = (hv * r * weights[1][...]).astype(BF16)
        outs[0][...] = a

        def emit(gate, up, cols):
            outs[1][:, cols] = gate.astype(BF16)
            outs[2][:, cols] = up.astype(BF16)
            outs[3][:, cols] = (_silu(gate) * up).astype(BF16)

        for s in range(2):
            emit(_dot(a, w_ref[s, :, 0:FFN_MAIN]), _dot(a, w_ref[2 + s, :, 0:FFN_MAIN]),
                 slice(s * wide, s * wide + FFN_MAIN))
        gate = _dot(a, jnp.concatenate([w_ref[0, :, FFN_MAIN:wide], w_ref[1, :, FFN_MAIN:wide]], axis=1))
        up = _dot(a, jnp.concatenate([w_ref[2, :, FFN_MAIN:wide], w_ref[3, :, FFN_MAIN:wide]], axis=1))
        rest = wide - FFN_MAIN
        for s in range(2):
            emit(gate[:, s * rest:(s + 1) * rest], up[:, s * rest:(s + 1) * rest],
                 slice(s * wide + FFN_MAIN, (s + 1) * wide))

    return _rows_call(tag + "_in", [h], [w_in, gain], [(D, BF16)] + [(DFF, BF16)] * 3, compute_in, min(FFN_TM, t),
                      side=side)


def _ffn_out(tag, act, h, w_out, next_gain, target=None):
    t = h.shape[0]
    tm = min(FFN_TM, t)

    def compute_out(rows, weights, outs):
        hn = rows[1][...] + 0.5 * _dot(rows[0][...], weights[0][...])
        g = weights[1][...]
        r = lax.rsqrt(jnp.mean(hn * hn, axis=-1, keepdims=True) + EPS)
        xh = hn * r
        if target is None:
            outs[0][...] = hn
            outs[1][...] = (xh * g).astype(BF16)
        else:
            err = xh * g - rows[2][...]
            dy = err * (1.0 / D)
            dxh = dy * g
            outs[0][...] = r * (dxh - xh * jnp.mean(dxh * xh, axis=-1, keepdims=True))
            outs[1][...] += jnp.sum(dy * xh, axis=0, keepdims=True)
            outs[2][...] += 0.5 * jnp.sum(jnp.mean(err * err, axis=-1, keepdims=True), axis=0, keepdims=True)

    if target is None:
        return _rows_call(tag + "_out", [act, h], [w_out, next_gain], [(D, F32), (D, BF16)], compute_out, tm)
    return _rows_call(tag + "_out", [act, h, target], [w_out, next_gain], [(D, F32)], compute_out, tm, sums=(D, 128))


class _Reduction:
    def __init__(self, tag, c_arr, k_arr):
        self.tag, self.c_arr, self.k_arr = tag, c_arr, k_arr

    def begin(self, bufs, swapped=None):
        if swapped is None:
            swapped = _SwapHalves(bufs).alone("grad_swap_" + self.tag)
        sums = [_chip_sum("grad_chip_sum_%s%d" % (self.tag, b), gp, got, self.c_arr)
                for b, (gp, got) in enumerate(zip(bufs, swapped))]
        self.sums = [s[0] for s in sums]
        return _ScatterChipSums([s[1] for s in sums])

    def end(self, got):
        mine = [_shard_sum("grad_shard_sum_%s%d" % (self.tag, b), cs, g, self.k_arr)
                for b, (cs, g) in enumerate(zip(self.sums, got))]
        return _join_halves("grad_join_" + self.tag, mine)


def _ffn_bwd(tag, h, gain, w_in, w_out, saved, dout, side, reduction):
    t = h.shape[0]
    tm = min(TM, t)
    n, gate, up, act = saved

    def compute(rows, weights, outs):
        d = rows[0][...].astype(BF16)
        for j in range(DFF // FFN_CHUNK):
            cols = slice(j * FFN_CHUNK, (j + 1) * FFN_CHUNK)
            da = 0.5 * _dot_nt(d, weights[0][cols, :])
            g, u = rows[1][:, cols].astype(F32), rows[2][:, cols].astype(F32)
            s = _sig(g)
            silu = g * s
            outs[0][:, cols] = (da * u * (s + silu * (1.0 - s))).astype(BF16)
            outs[1][:, cols] = (da * silu).astype(BF16)

    dgate, dup, *side_out = _rows_call(tag + "_dact", [dout, gate, up], [w_out], [(DFF, BF16)] * 2, compute,
                                       min(FFN_TM, t), side=side)
    dw_in = _mm_tn(tag + "_dw_gate", n, dgate, tm=D, tn=DFF // 2, stacked=(4, 0))
    dw_in = _mm_tn(tag + "_dw_up", n, dup, tm=D, tn=DFF // 2, stacked=(4, 2), into=dw_in)
    dw_out, dw_in_swapped = _mm_tn(tag + "_dw_out", act, dout, scale=0.5, tm=DFF // 2, tn=D, side=_SwapHalves([dw_in]))
    dw_out = dw_out.reshape(4, DFF // 4, D)
    (dw_out_swapped,) = _SwapHalves([dw_out]).alone("grad_swap_" + tag)
    sending = reduction.begin([dw_in, dw_out], [dw_in_swapped, dw_out_swapped])

    def compute_dn(rows, weights, outs):
        w_ref = weights[0]
        wide = DFF // 2
        dn = jnp.zeros((rows[0].shape[0], D), F32)
        for s in range(2):
            cols = slice(s * wide, s * wide + FFN_MAIN)
            dn = (dn + _dot_nt(rows[0][:, cols], w_ref[s, :, 0:FFN_MAIN])
                  + _dot_nt(rows[1][:, cols], w_ref[2 + s, :, 0:FFN_MAIN]))
        for r, first in ((0, 0), (1, 2)):
            x = jnp.concatenate([rows[r][:, FFN_MAIN:wide], rows[r][:, wide + FFN_MAIN:2 * wide]], axis=1)
            wt = jnp.concatenate([w_ref[first, :, FFN_MAIN:wide], w_ref[first + 1, :, FFN_MAIN:wide]], axis=1)
            dn = dn + _dot_nt(x, wt)
        dx, dg = _rms_bwd_vals(rows[2][...], weights[1][...], dn)
        outs[0][...] = rows[3][...] + dx
        outs[1][...] += jnp.sum(dg, axis=0, keepdims=True)

    dh, dgain, *got = _rows_call(tag + "_dn", [dgate, dup, h, dout], [w_in, gain], [(D, F32)], compute_dn,
                                 min(FFN_TM, t), side=sending, sums=(D,), vmem_mb=58)
    return dh, dgain, side_out, got


def kernel(x, positions, ffn1_norm, ffn1_w_in, ffn1_w_out, mix_norm, w_in, hg_lb_table, hg_out_norm, w_hg_branch, mla_q_lora_norm, w_q_up, mla_kv_lora_norm, w_kv_up, q_head_norm, k_head_norm, w_mla_branch, w_merge, b_merge, w_out, ffn2_norm, ffn2_w_in, ffn2_w_out, final_norm, loss_target, m_ffn1_norm, m_ffn1_w_in, m_ffn1_w_out, m_mix_norm, m_w_in, m_hg_lb_table, m_hg_out_norm, m_w_hg_branch, m_mla_q_lora_norm, m_w_q_up, m_mla_kv_lora_norm, m_w_kv_up, m_q_head_norm, m_k_head_norm, m_w_mla_branch, m_w_merge, m_b_merge, m_w_out, m_ffn2_norm, m_ffn2_w_in, m_ffn2_w_out, m_final_norm, v_ffn1_norm, v_ffn1_w_in, v_ffn1_w_out, v_mix_norm, v_w_in, v_hg_lb_table, v_hg_out_norm, v_w_hg_branch, v_mla_q_lora_norm, v_w_q_up, v_mla_kv_lora_norm, v_w_kv_up, v_q_head_norm, v_k_head_norm, v_w_mla_branch, v_w_merge, v_b_merge, v_w_out, v_ffn2_norm, v_ffn2_w_in, v_ffn2_w_out, v_final_norm):
    a = dict(locals())
    w = {n: a[n] for n in WEIGHT_ORDER}
    mom = {n: a["m_" + n] for n in WEIGHT_ORDER}
    var = {n: a["v_" + n] for n in WEIGHT_ORDER}
    t = x.shape[1]
    tm = min(TM, t)
    xt = x.reshape(t, D)
    target = loss_target.reshape(t, D)
    pos = positions.reshape(t, 1)
    x_i, y_i, c_i = _place()
    k_idx = (2 * x_i + y_i).astype(jnp.int32)
    c_arr = c_i.astype(jnp.int32).reshape(1)
    k_arr = jnp.stack([k_idx, c_i.astype(jnp.int32)])

    group_mid = _group(("w_in", "w_hg_branch", "w_q_up", "w_kv_up", "w_mla_branch", "w_merge", "w_out"))
    use_early = _group(("ffn1_w_out", "w_in", "w_hg_branch", "w_q_up", "w_kv_up"))
    use_late = _group(("w_mla_branch", "w_merge", "w_out", "ffn2_w_out"))
    gather_first = _GatherWeights([w["ffn1_w_in"][0].astype(BF16)])
    gather_early = _GatherWeights([_pack([w[e[0]][0] for e in use_early], BF16)])
    gather_late = _GatherWeights([_pack([w[e[0]][0] for e in use_late], BF16), w["ffn2_w_in"][0].astype(BF16)])
    (ffn1_w_in_g,) = gather_first.gathered(gather_first.alone("gather_first"), k_idx)
    n1, gate1, up1, act1, got = _ffn_in("ffn1", xt, w["ffn1_norm"], ffn1_w_in_g, gather_early)
    full = _unpack_full(gather_early.gathered([got], k_idx)[0], use_early)
    h1, u = _ffn_out("ffn1", act1, xt, full["ffn1_w_out"], w["mix_norm"])
    ffn1_saved = (n1, gate1, up1, act1)
    w_in_full = full["w_in"]
    w_in_hg = w_in_full[:, :4 * D]
    w_in_mla = jnp.pad(w_in_full[:, 4 * D:], ((0, 0), (0, MLA_COLS - (4800 - 4 * D))))
    w_q_pad = jnp.pad(full["w_q_up"].reshape(Q_LORA, HEADS, QK), ((0, 0), (0, 0), (0, QKP - QK))).reshape(Q_LORA, HEADS * QKP)
    w_kv = full["w_kv_up"]
    gq = jnp.pad(w["q_head_norm"], ((0, 0), (0, QKP - QK)))
    gk = jnp.pad(w["k_head_norm"], ((0, 0), (0, QKP - QK)))

    ident = lambda accs, ex: (accs[0],)
    def in_hg(rows, weights, outs):
        a = rows[0][...]
        for j in range(4 * D // 512):
            cols = slice(j * 512, (j + 1) * 512)
            outs[0][:, cols] = _dot(a, weights[0][:, cols]).astype(BF16)

    (p_hg,) = _rows_call("in_hg", [u], [w_in_hg], [(4 * D, BF16)], in_hg, min(FFN_TM, t))
    p_mla, cqn, ckvn = _in_mla(u, w_in_mla, w["mla_q_lora_norm"], w["mla_kv_lora_norm"])
    o_raw, hg_o, states = _hgrn_fwd(p_hg, w["hg_lb_table"], w["hg_out_norm"])
    (y_hg,) = _mm("hg_branch", [_a_spec(hg_o, tm)], [_b_nn(full["w_hg_branch"], 512)], [(0, 0)], ident, [], [BF16], t, D, tm, 512)
    (qf,) = _mm("q_up", [_a_spec(cqn, tm)], [_b_nn(w_q_pad, 512)], [(0, 0)], ident, [], [BF16], t, HEADS * QKP, tm, 512)
    (kvf,) = _mm("kv_up", [_a_spec(ckvn, tm)], [_b_nn(w_kv, 512)], [(0, 0)], ident, [], [BF16], t, HEADS * QKP, tm, 512)
    cos, sin = _rope_tables(pos)
    qh, kh, vh = _mla_prep_fwd(qf, kvf, p_mla, cos, sin, gq, gk)
    o_mla, lse, *got = _flash_fwd(qh, kh, vh, side=gather_late)
    late, ffn2_w_in_g = gather_late.gathered(got, k_idx)
    full.update(_unpack_full(late, use_late))
    (y_mla,) = _mm("mla_branch", [_a_spec(o_mla, tm)], [_b_nn(full["w_mla_branch"], 512)], [(0, 0)], ident, [], [BF16], t, D, tm, 512)

    def merge_epi(accs, ex):
        g_hg = _sig(accs[0] + ex[2])
        g_mla = _sig(accs[1] + ex[3])
        return g_hg * ex[0].astype(F32) + g_mla * ex[1].astype(F32), g_hg, g_mla

    w_merge_f = full["w_merge"]
    mix, g_hg, g_mla = _mm(
        "merge", [_a_spec(u, tm)], [_b_nn(w_merge_f, 512), _b_nn(w_merge_f, 512, D // 512)], [(0, 0), (0, 1)], merge_epi,
        [_e_tile(y_hg, tm, 512), _e_tile(y_mla, tm, 512), _e_row(w["b_merge"], 512), _e_row(w["b_merge"], 512, D // 512)],
        [BF16, BF16, BF16], t, D, tm, 512)
    (h2,) = _mm("out_proj", [_a_spec(mix, tm)], [_b_nn(full["w_out"], 512)], [(0, 0)],
                lambda accs, ex: (ex[0] + accs[0],), [_e_tile(h1, tm, 512)], [F32], t, D, tm, 512)
    ffn2_saved = _ffn_in("ffn2", h2, w["ffn2_norm"], ffn2_w_in_g)
    dh3, d_final_norm, loss_part = _ffn_out("ffn2", ffn2_saved[3], h2, full["ffn2_w_out"], w["final_norm"], target=target)

    grads, small = {}, {}
    small["final_norm"] = d_final_norm
    reduce_last = _Reduction("last", c_arr, k_arr)
    reduce_mid = _Reduction("mid", c_arr, k_arr)
    reduce_first = _Reduction("first", c_arr, k_arr)
    dh2, small["ffn2_norm"], _, got_last = _ffn_bwd(
        "ffn2", h2, w["ffn2_norm"], ffn2_w_in_g, full["ffn2_w_out"], ffn2_saved, dh3, None, reduce_last)

    def dmix_epi(accs, ex):
        dm = accs[0]
        ghg, gml, yhg, yml = [e.astype(F32) for e in ex]
        return dm * ghg, dm * gml, dm * yhg * ghg * (1.0 - ghg), dm * yml * gml * (1.0 - gml)

    dy_hg, dy_mla, dpre_hg, dpre_mla = _mm(
        "d_mix", [_a_spec(dh2, tm)], [_b_nt(full["w_out"], 512)], [(0, 0)], dmix_epi,
        [_e_tile(g_hg, tm, 512), _e_tile(g_mla, tm, 512), _e_tile(y_hg, tm, 512), _e_tile(y_mla, tm, 512)],
        [BF16, BF16, BF16, BF16], t, D, tm, 512, trans_b=True)
    grads["w_out"] = _mm_tn("dw_out", mix, dh2)
    small["b_merge"] = jnp.concatenate([_colsum("db_hg", dpre_hg), _colsum("db_mla", dpre_mla)], axis=1)
    grads["w_merge"] = jnp.concatenate([_mm_tn("dw_merge_hg", u, dpre_hg), _mm_tn("dw_merge_mla", u, dpre_mla)], axis=1)
    grads["w_hg_branch"] = _mm_tn("dw_hg_branch", hg_o, dy_hg)
    grads["w_mla_branch"] = _mm_tn("dw_mla_branch", o_mla, dy_mla)
    (dho,) = _mm("d_hg_o", [_a_spec(dy_hg, tm)], [_b_nt(full["w_hg_branch"], 512)], [(0, 0)], ident, [], [BF16], t, D, tm, 512, trans_b=True)
    (do_mla,) = _mm("d_o_mla", [_a_spec(dy_mla, tm)], [_b_nt(full["w_mla_branch"], 512)], [(0, 0)], ident, [], [BF16], t, D, tm, 512, trans_b=True)

    dq_raw, df_raw, di_raw, dg_raw, small["hg_lb_table"], small["hg_out_norm"] = _hgrn_bwd(
        p_hg, w["hg_lb_table"], w["hg_out_norm"], o_raw, states, dho)
    dp_hg = [dq_raw, df_raw, di_raw, dg_raw]

    dqh, dkh, dvh = _flash_bwd(qh, kh, vh, lse, _attn_do(do_mla, o_mla))
    dqf, dkvf, dkpe, dgq, dgk = _mla_prep_bwd(qf, kvf, p_mla, cos, sin, gq, gk, dqh, dkh, dvh)
    small["q_head_norm"] = dgq[:, :QK]
    small["k_head_norm"] = dgk[:, :QK]
    dwq_pad = _mm_tn("dw_q_up", cqn, dqf, tm=Q_LORA, tn=1024)
    grads["w_q_up"] = dwq_pad.reshape(Q_LORA, HEADS, QKP)[:, :, :QK].reshape(Q_LORA, HEADS * QK)
    grads["w_kv_up"] = _mm_tn("dw_kv_up", ckvn, dkvf, tm=KV_LORA, tn=1024)
    (dcqn,) = _mm("d_cq", [_a_spec(dqf, tm)], [_b_nt(w_q_pad, Q_LORA)], [(0, 0)], ident, [], [F32], t, Q_LORA, tm, Q_LORA, trans_b=True)
    (dckvn,) = _mm("d_ckv", [_a_spec(dkvf, tm)], [_b_nt(w_kv, KV_LORA)], [(0, 0)], ident, [], [F32], t, KV_LORA, tm, KV_LORA, trans_b=True)
    dp_mla, small["mla_q_lora_norm"], small["mla_kv_lora_norm"] = _lora_norm_bwd(
        p_mla, w["mla_q_lora_norm"], w["mla_kv_lora_norm"], dcqn, dckvn, dkpe)

    dw_in_hg = [_mm_tn("dw_in_hg%d" % k, u, dp_hg[k]) for k in range(4)]
    dw_in_mla = _mm_tn("dw_in_mla", u, dp_mla, tn=MLA_COLS)
    grads["w_in"] = jnp.concatenate(dw_in_hg + [dw_in_mla[:, :4800 - 4 * D]], axis=1)
    tm_du = min(TM // 2, t)
    du, *got_mid = _mm(
        "d_u",
        [_a_spec(dpre_hg, tm_du), _a_spec(dpre_mla, tm_du)] + [_a_spec(d, tm_du) for d in dp_hg] + [_a_spec(dp_mla, tm_du)],
        [_b_nt(w_merge_f, 512, D, 0), _b_nt(w_merge_f, 512, D, 1)]
        + [_b_nt(w_in_hg, 512, D, k) for k in range(4)] + [_b_nt(w_in_mla, 512)],
        [(k, k) for k in range(7)],
        lambda accs, ex: (functools.reduce(lambda p, q: p + q, accs),), [], [F32], t, D, tm_du, 512, trans_b=True,
        side=reduce_mid.begin([_pack_grads(grads, group_mid)]))
    dh1, small["mix_norm"] = _rms_bwd("mix_dnorm", h1, w["mix_norm"], du, dh2)
    dx, small["ffn1_norm"], _, got_first = _ffn_bwd(
        "ffn1", xt, w["ffn1_norm"], ffn1_w_in_g, full["ffn1_w_out"], ffn1_saved, dh1, None, reduce_first)

    g_shard = _unpack_shard(reduce_mid.end(got_mid)[0], group_mid)
    g_shard["ffn2_w_in"], g_shard["ffn2_w_out"] = reduce_last.end(got_last)
    g_shard["ffn1_w_in"], g_shard["ffn1_w_out"] = reduce_first.end(got_first)
    small_sum = _all_reduce_small(_pack_small([small[n] for n, _ in SMALL] + [loss_part])).reshape(-1)
    g_small, at = {}, 0
    for n, shape in SMALL:
        size = shape[0] * shape[1]
        g_small[n] = small_sum[at:at + size].reshape(shape)
        at += size
    loss = small_sum[at]

    g_out, d_out, m_out, v_out = {}, {}, {}, {}
    for n in WEIGHT_ORDER:
        shape = w[n].shape
        g = g_shard[n] if n in g_shard else g_small[n]
        two = g.shape
        d_, m_, v_ = _adamw("adamw_" + n, w[n].reshape(two), g, mom[n].reshape(two), var[n].reshape(two))
        g_out[n], d_out[n], m_out[n], v_out[n] = g.reshape(shape), d_.reshape(shape), m_.reshape(shape), v_.reshape(shape)

    return (loss, dx.reshape(x.shape), *[g_out[n] for n in WEIGHT_ORDER], *[d_out[n] for n in WEIGHT_ORDER],
            *[m_out[n] for n in WEIGHT_ORDER], *[v_out[n] for n in WEIGHT_ORDER])
```
